```python
import math
import jax
import jax.numpy as jnp
from jax import lax
import numpy as np

D_MODEL = 2048
BATCH = 2
SEQ = 4096
DEPTH = 4
DEC_BATCH = 8
DEC_SEQ = 4
PAST_LEN = 16384
PAGE_SIZE = 128

N_MIXERS = 4
HEAD_DIM = 128
N_HEADS = D_MODEL // HEAD_DIM
KV_HEADS = 4
GROUP = N_HEADS // KV_HEADS
ATTN_SCALE = HEAD_DIM ** -0.5
REL_BUCKETS = 32
REL_MAX_DIST = 128
CMP_BLOCK = 32
CMP_STRIDE = 16
CMP_HIDDEN = HEAD_DIM
SEL_BLOCK = 64
N_SEL = 16
WINDOW = 512
FORCE_SCORE = 1.0e4
SEL_QBLK = 32
WIN_QBLK = 128
MOBA_BLOCK = 256
MOBA_TOPK = 3
MOBA_QBLK = 8
HG_DK = 128
HG_HEADS = D_MODEL // HG_DK
HG_DV = D_MODEL // HG_HEADS
GDN_DK = 128
GDN_DV = 128
GDN_QK_HEADS = D_MODEL // GDN_DK
GDN_V_HEADS = 2 * GDN_QK_HEADS
GDN_CONV = 4
GDN_CONV_DIM = 2 * GDN_QK_HEADS * GDN_DK + GDN_V_HEADS * GDN_DV
CHUNK = 64
FFN_HIDDEN = ((8 * D_MODEL + 3 * 256 - 1) // (3 * 256)) * 256
NSA_IN = N_HEADS * HEAD_DIM + 3 * 2 * KV_HEADS * HEAD_DIM + 3 * N_HEADS
MOBA_IN = N_HEADS * HEAD_DIM + 2 * KV_HEADS * HEAD_DIM
HG_IN = 2 * HG_HEADS * HG_DK + 2 * HG_HEADS * HG_DV
GDN_IN = GDN_CONV_DIM + GDN_V_HEADS * GDN_DV + 2 * GDN_V_HEADS
NEG_INF = -1.0e30
NORM_EPS = 1e-6

kernel_name = 'hybrid_nsa_moba_hgrn2_gdn_step'


def rms_norm(x, g):
    xf = x.astype(jnp.float32)
    y = xf * lax.rsqrt(jnp.mean(xf * xf, axis=-1, keepdims=True) + NORM_EPS)
    return (y * g.astype(jnp.float32)).astype(x.dtype)


def l2norm(x):
    return x * lax.rsqrt(jnp.sum(x * x, axis=-1, keepdims=True) + NORM_EPS)


def masked_softmax(logits, mask):
    logits = jnp.where(mask, logits, NEG_INF)
    m = jnp.max(logits, axis=-1, keepdims=True)
    e = jnp.where(mask, jnp.exp(logits - m), 0.0)
    return e / jnp.maximum(jnp.sum(e, axis=-1, keepdims=True), 1e-30)


def t5_bucket(dist):
    exact = REL_BUCKETS // 2
    d = jnp.maximum(dist, 0)
    ratio = jnp.log(jnp.maximum(d, 1).astype(jnp.float32) / exact) / math.log(REL_MAX_DIST / exact)
    large = jnp.minimum(exact + (ratio * (REL_BUCKETS - exact)).astype(jnp.int32), REL_BUCKETS - 1)
    return jnp.where(d < exact, d, large)


def rel_bias(rel_table, dist):
    return rel_table.astype(jnp.float32)[t5_bucket(dist)]


def gather_pages(pool, page_table):
    rows = pool[page_table]
    return rows.reshape(page_table.shape[0], page_table.shape[1] * pool.shape[1], *pool.shape[2:])


def to_qblocks(a, qb):
    return jnp.moveaxis(a.reshape(a.shape[0], a.shape[1] // qb, qb, *a.shape[2:]), 1, 0)


def from_qblocks(a):
    a = jnp.moveaxis(a, 0, 1)
    return a.reshape(a.shape[0], a.shape[1] * a.shape[2], *a.shape[3:])


def swiglu(h, w_up, w_down):
    a, b = jnp.split(h @ w_up, 2, axis=-1)
    return (jax.nn.silu(a) * b) @ w_down


def causal_conv(x, buf, w):
    T = x.shape[1]
    xp = jnp.concatenate([buf.astype(x.dtype), x], axis=1)
    y = xp[:, 0:T] * w[0]
    for i in range(1, GDN_CONV):
        y = y + xp[:, i:i + T] * w[i]
    return jax.nn.silu(y), xp[:, T:]


def nsa_project(h, w_in):
    B, T, _ = h.shape
    proj = h @ w_in
    qw = N_HEADS * HEAD_DIM
    kvw = 2 * KV_HEADS * HEAD_DIM
    q = proj[..., :qw].reshape(B, T, KV_HEADS, GROUP, HEAD_DIM)
    kv_c = proj[..., qw:qw + kvw].reshape(B, T, 2, KV_HEADS, HEAD_DIM)
    kv_s = proj[..., qw + kvw:qw + 2 * kvw].reshape(B, T, 2, KV_HEADS, HEAD_DIM)
    kv_w = proj[..., qw + 2 * kvw:qw + 3 * kvw].reshape(B, T, 2, KV_HEADS, HEAD_DIM)
    gates = jax.nn.sigmoid(proj[..., qw + 3 * kvw:].astype(jnp.float32)).reshape(B, T, 3, KV_HEADS, GROUP)
    return q, kv_c, kv_s, kv_w, gates


def nsa_compress(kv, cmp_w1, cmp_w2, cmp_pe):
    B, Tk = kv.shape[:2]
    nc = (Tk - CMP_BLOCK) // CMP_STRIDE + 1
    n_part = CMP_BLOCK // CMP_STRIDE
    n_chunk = nc + n_part - 1
    chunks = kv[:, :n_chunk * CMP_STRIDE].astype(jnp.float32).reshape(B, n_chunk, CMP_STRIDE, 2, KV_HEADS, HEAD_DIM)
    w1 = cmp_w1.astype(jnp.float32)
    part = jnp.einsum('bnlcgd,crldh->bncrgh', chunks, w1.reshape(2, n_part, CMP_STRIDE, HEAD_DIM, CMP_HIDDEN))
    hid = jnp.einsum('cld,cldh->ch', cmp_pe.astype(jnp.float32), w1)[None, None, :, None, :]
    for r in range(n_part):
        hid = hid + part[:, r:r + nc, :, r]
    return jnp.einsum('bncgh,chd->bncgd', jax.nn.gelu(hid), cmp_w2.astype(jnp.float32))


def nsa_cmp_attn(q, q_pos, kc, rel_table):
    nc = kc.shape[1]
    end_pos = jnp.arange(nc) * CMP_STRIDE + (CMP_BLOCK - 1)
    dist = q_pos[:, None] - end_pos[None, :]
    bias = rel_bias(rel_table, dist).reshape(q_pos.shape[0], nc, KV_HEADS, GROUP).transpose(0, 2, 3, 1)
    logits = jnp.einsum('bqgrd,bcgd->bqgrc', q, kc[:, :, 0]) * ATTN_SCALE + bias
    p = masked_softmax(logits, (dist >= 0)[:, None, None, :])
    return jnp.einsum('bqgrc,bcgd->bqgrd', p, kc[:, :, 1]), p


def nsa_select(p_cmp, q_pos, tk):
    ns = -(-tk // SEL_BLOCK)
    ratio = SEL_BLOCK // CMP_STRIDE
    imp = p_cmp.sum(axis=3)
    nc = imp.shape[-1]
    imp = jnp.pad(imp, ((0, 0), (0, 0), (0, 0), (1, ratio * ns + ratio - 1 - nc)))
    score = imp[..., :ratio * ns].reshape(*imp.shape[:3], ns, ratio).sum(-1) + imp[..., ratio::ratio]
    cur = q_pos // SEL_BLOCK
    blk = jnp.arange(ns)
    forced = (blk[None, :] == 0) | (blk[None, :] == cur[:, None]) | (blk[None, :] == cur[:, None] - 1)
    causal = blk[None, :] <= cur[:, None]
    score = jnp.where(forced[None, :, None, :], FORCE_SCORE, score)
    score = jnp.where(causal[None, :, None, :], score, -1.0)
    _, idx = lax.top_k(score, min(N_SEL, ns))
    valid = idx <= cur[None, :, None, None]
    return idx, valid


def nsa_sel_attn(q, q_pos, kv, idx, valid, rel_table):
    B, Tq = q.shape[:2]
    Tk = kv.shape[1]
    ns = -(-Tk // SEL_BLOCK)
    kvf = jnp.pad(kv.astype(jnp.float32), ((0, 0), (0, ns * SEL_BLOCK - Tk), (0, 0), (0, 0), (0, 0)))
    kvb_g = jnp.moveaxis(kvf.reshape(B, ns, SEL_BLOCK, 2, KV_HEADS, HEAD_DIM), 4, 1)
    table3 = rel_table.astype(jnp.float32).reshape(REL_BUCKETS, KV_HEADS, GROUP)
    bi = jnp.arange(B)[:, None, None, None]
    gi = jnp.arange(KV_HEADS)[None, None, :, None]
    gi5 = jnp.arange(KV_HEADS)[None, None, :, None, None]
    ar = jnp.arange(SEL_BLOCK)
    qb = math.gcd(Tq, SEL_QBLK)

    def step(args):
        q_b, pos_b, idx_b, val_b = args
        kv_sel = kvb_g[bi, gi, idx_b]
        dist = pos_b[None, :, None, None, None] - (idx_b[..., None] * SEL_BLOCK + ar)
        mask = val_b[..., None] & (dist >= 0)
        bias = jnp.moveaxis(table3[t5_bucket(dist), gi5], -1, 3)
        logits = jnp.einsum('bqgrd,bqgkld->bqgrkl', q_b, kv_sel[..., 0, :]) * ATTN_SCALE + bias
        p = masked_softmax(logits.reshape(*logits.shape[:4], -1), mask.reshape(*mask.shape[:3], 1, -1))
        return jnp.einsum('bqgrkl,bqgkld->bqgrd', p.reshape(logits.shape), kv_sel[..., 1, :])

    o = lax.map(step, (to_qblocks(q, qb), q_pos.reshape(-1, qb), to_qblocks(idx, qb), to_qblocks(valid, qb)))
    return from_qblocks(o)


def band_attn(q, q_pos, kv_band, k_pos, rel_table):
    dist = q_pos[:, :, None] - k_pos[:, None, :]
    mask = (dist >= 0) & (dist < WINDOW) & (k_pos[:, None, :] >= 0)
    n, qb, kb = dist.shape
    bias = rel_bias(rel_table, dist).reshape(n, qb, kb, KV_HEADS, GROUP).transpose(0, 1, 3, 4, 2)
    kvf = kv_band.astype(jnp.float32)
    logits = jnp.einsum('bnqgrd,bnkgd->bnqgrk', q, kvf[:, :, :, 0]) * ATTN_SCALE + bias
    p = masked_softmax(logits, mask[:, :, None, None, :])
    return jnp.einsum('bnqgrk,bnkgd->bnqgrd', p, kvf[:, :, :, 1])


def nsa_core(q, q_pos, kv_c, kv_s, band_kv, band_q_pos, band_k_pos, gates, cmp_w1, cmp_w2, cmp_pe, rel_table):
    B, T = q.shape[:2]
    q = q.astype(jnp.float32)
    kc = nsa_compress(kv_c, cmp_w1, cmp_w2, cmp_pe)
    o_cmp, p_cmp = nsa_cmp_attn(q, q_pos, kc, rel_table)
    idx, valid = nsa_select(p_cmp, q_pos, kv_s.shape[1])
    o_sel = nsa_sel_attn(q, q_pos, kv_s, idx, valid, rel_table)
    nqb, qb = band_q_pos.shape
    o_win = band_attn(q.reshape(B, nqb, qb, KV_HEADS, GROUP, HEAD_DIM), band_q_pos, band_kv, band_k_pos,
                      rel_table).reshape(B, T, KV_HEADS, GROUP, HEAD_DIM)
    o = gates[:, :, 0, ..., None] * o_cmp + gates[:, :, 1, ..., None] * o_sel + gates[:, :, 2, ..., None] * o_win
    return o.reshape(B, T, N_HEADS * HEAD_DIM)


def nsa_prompt(h, w_in, w_out, cmp_w1, cmp_w2, cmp_pe, rel_table):
    B, T, _ = h.shape
    q, kv_c, kv_s, kv_w, gates = nsa_project(h, w_in)
    pos = jnp.arange(T)
    qb = math.gcd(T, WIN_QBLK)
    band_idx = jnp.arange(T // qb)[:, None] * qb + jnp.arange(WINDOW + qb)[None, :]
    band = jnp.pad(kv_w, ((0, 0), (WINDOW, 0), (0, 0), (0, 0), (0, 0)))[:, band_idx]
    o = nsa_core(q, pos, kv_c, kv_s, band, pos.reshape(-1, qb), band_idx - WINDOW, gates,
                 cmp_w1, cmp_w2, cmp_pe, rel_table)
    return o.astype(h.dtype) @ w_out, kv_c, kv_s, kv_w[:, -min(WINDOW, T):]


def nsa_sample(h, cache_c, cache_s, cache_w, page_table, w_in, w_out, cmp_w1, cmp_w2, cmp_pe, rel_table):
    B, T, _ = h.shape
    past = page_table.shape[1] * cache_c.shape[1]
    q, kv_c, kv_s, kv_w, gates = nsa_project(h, w_in)
    pos = past + jnp.arange(T)
    kv_c_full = jnp.concatenate([gather_pages(cache_c, page_table).astype(kv_c.dtype), kv_c], axis=1)
    kv_s_full = jnp.concatenate([gather_pages(cache_s, page_table).astype(kv_s.dtype), kv_s], axis=1)
    wbuf = cache_w.shape[1]
    band = jnp.concatenate([cache_w.astype(kv_w.dtype), kv_w], axis=1)
    band_k_pos = past - wbuf + jnp.arange(wbuf + T)
    o = nsa_core(q, pos, kv_c_full, kv_s_full, band[:, None], pos[None, :], band_k_pos[None, :], gates,
                 cmp_w1, cmp_w2, cmp_pe, rel_table)
    return o.astype(h.dtype) @ w_out, kv_c, kv_s, band[:, -wbuf:]


def moba_project(h, w_in):
    B, T, _ = h.shape
    proj = h @ w_in
    qw = N_HEADS * HEAD_DIM
    return (proj[..., :qw].reshape(B, T, KV_HEADS, GROUP, HEAD_DIM),
            proj[..., qw:].reshape(B, T, 2, KV_HEADS, HEAD_DIM))


def moba_attn(q, q_pos, kv, rel_table):
    B, Tq = q.shape[:2]
    Tk = kv.shape[1]
    q = q.astype(jnp.float32)
    nb = -(-Tk // MOBA_BLOCK)
    kvf = jnp.pad(kv.astype(jnp.float32), ((0, 0), (0, nb * MOBA_BLOCK - Tk), (0, 0), (0, 0), (0, 0)))
    kvb = kvf.reshape(B, nb, MOBA_BLOCK, 2, KV_HEADS, HEAD_DIM)
    kmean = jnp.mean(kvb[:, :, :, 0], axis=2)
    own = q_pos // MOBA_BLOCK
    gate = jnp.einsum('bqgrd,bngd->bqgrn', q, kmean)
    past = jnp.arange(nb)[None, :] < own[:, None]
    gate = jnp.where(past[:, None, None, :], gate, NEG_INF)
    _, idx = lax.top_k(gate, min(MOBA_TOPK, nb))
    valid = idx < own[:, None, None, None]
    kvb_g = jnp.moveaxis(kvb, 4, 1)
    table3 = rel_table.astype(jnp.float32).reshape(REL_BUCKETS, KV_HEADS, GROUP)
    bi = jnp.arange(B)[:, None, None, None, None]
    gi = jnp.arange(KV_HEADS)[None, None, :, None, None]
    gi6 = jnp.arange(KV_HEADS)[None, None, :, None, None, None]
    ri6 = jnp.arange(GROUP)[None, None, None, :, None, None]
    ar = jnp.arange(MOBA_BLOCK)
    qb = math.gcd(Tq, MOBA_QBLK)

    def step(args):
        q_b, pos_b, idx_b, val_b = args
        k_top = idx_b.shape[-1]
        sel = kvb_g[bi, gi, idx_b]
        own_b = pos_b // MOBA_BLOCK
        own_kv = kvb[:, own_b]
        d_sel = pos_b[None, :, None, None, None, None] - (idx_b[..., None] * MOBA_BLOCK + ar)
        d_own = pos_b[:, None] - (own_b[:, None] * MOBA_BLOCK + ar[None, :])
        b_sel = table3[t5_bucket(d_sel), gi6, ri6]
        b_own = rel_bias(rel_table, d_own).reshape(qb, MOBA_BLOCK, KV_HEADS, GROUP).transpose(0, 2, 3, 1)
        l_sel = jnp.einsum('bqgrd,bqgrkld->bqgrkl', q_b, sel[..., 0, :]) * ATTN_SCALE + b_sel
        l_own = jnp.einsum('bqgrd,bqlgd->bqgrl', q_b, own_kv[:, :, :, 0]) * ATTN_SCALE + b_own
        lead = l_own.shape[:4]
        logits = jnp.concatenate([l_sel.reshape(*lead, k_top * MOBA_BLOCK), l_own], axis=-1)
        m_sel = jnp.broadcast_to(val_b[..., None], l_sel.shape).reshape(*lead, k_top * MOBA_BLOCK)
        m_own = jnp.broadcast_to((d_own >= 0)[None, :, None, None, :], l_own.shape)
        p = masked_softmax(logits, jnp.concatenate([m_sel, m_own], axis=-1))
        p_sel = p[..., :k_top * MOBA_BLOCK].reshape(l_sel.shape)
        p_own = p[..., k_top * MOBA_BLOCK:]
        return (jnp.einsum('bqgrkl,bqgrkld->bqgrd', p_sel, sel[..., 1, :])
                + jnp.einsum('bqgrl,bqlgd->bqgrd', p_own, own_kv[:, :, :, 1]))

    o = lax.map(step, (to_qblocks(q, qb), q_pos.reshape(-1, qb), to_qblocks(idx, qb), to_qblocks(valid, qb)))
    return from_qblocks(o).reshape(B, Tq, N_HEADS * HEAD_DIM)


def gla_chunked(q, k, v, logf, s0):
    B, T, H, dk = q.shape
    dv = v.shape[-1]
    C = math.gcd(T, CHUNK)
    N = T // C
    q, k, v, logf = (a.reshape(B, N, C, *a.shape[2:]) for a in (q, k, v, logf))
    b = jnp.cumsum(logf, axis=2)
    b_ref = b[:, :, C // 2:C // 2 + 1]
    a = jnp.einsum('bnihd,bnjhd->bnhij', q * jnp.exp(b - b_ref), k * jnp.exp(b_ref - b))
    causal = jnp.arange(C)[:, None] >= jnp.arange(C)[None, :]
    a = jnp.where(causal, a, 0.0)
    o_intra = jnp.einsum('bnhij,bnjhv->bnihv', a, v)
    q_in = q * jnp.exp(b)
    k_out = k * jnp.exp(b[:, :, -1:] - b)
    d_last = jnp.exp(b[:, :, -1])

    def step(s, xs):
        q_c, k_c, v_c, d_c = xs
        o = jnp.einsum('bihd,bhdv->bihv', q_c, s)
        s = d_c[..., None] * s + jnp.einsum('bjhd,bjhv->bhdv', k_c, v_c)
        return s, o

    s, o_inter = lax.scan(step, s0, tuple(jnp.moveaxis(t, 1, 0) for t in (q_in, k_out, v, d_last)))
    o = o_intra + jnp.moveaxis(o_inter, 0, 1)
    return o.reshape(B, T, H, dv), s


def hgrn2_mix(h, s0, w_in, lb_logits, layer, norm_g, w_out):
    B, T, _ = h.shape
    proj = (h @ w_in).astype(jnp.float32)
    dk = HG_HEADS * HG_DK
    dv = HG_HEADS * HG_DV
    q, f, i, g = jnp.split(proj, [dk, 2 * dk, 2 * dk + dv], axis=-1)
    p = jax.nn.softmax(lb_logits.astype(jnp.float32), axis=0)
    lb = (jnp.cumsum(p, axis=0) - p[0])[layer]
    fg = lb + (1.0 - lb) * jax.nn.sigmoid(f)
    shp = (B, T, HG_HEADS, HG_DK)
    o, s = gla_chunked((jax.nn.silu(q) * HG_DK ** -0.5).reshape(shp), (1.0 - fg).reshape(shp),
                       i.reshape(B, T, HG_HEADS, HG_DV), jnp.log(fg).reshape(shp), s0.astype(jnp.float32))
    o = rms_norm(o, norm_g) * jax.nn.silu(g.reshape(B, T, HG_HEADS, HG_DV))
    return o.reshape(B, T, dv).astype(h.dtype) @ w_out, s.astype(h.dtype)


def gdn_chunked(q, k, v, log_a, beta, s0):
    B, T, H, dk = q.shape
    dv = v.shape[-1]
    C = math.gcd(T, CHUNK)
    N = T // C

    def heads_first(a):
        return jnp.moveaxis(a.reshape(B, N, C, *a.shape[2:]), 3, 2)

    qh, kh, vh, bt = heads_first(q), heads_first(k), heads_first(v), heads_first(beta)
    g = jnp.cumsum(heads_first(log_a), axis=-1)
    ar = jnp.arange(C)
    strict = ar[:, None] > ar[None, :]
    incl = ar[:, None] >= ar[None, :]
    gdiff = g[..., :, None] - g[..., None, :]
    d_strict = jnp.where(strict, jnp.exp(jnp.where(strict, gdiff, 0.0)), 0.0)
    d_incl = jnp.where(incl, jnp.exp(jnp.where(incl, gdiff, 0.0)), 0.0)
    kk = jnp.einsum('bnhid,bnhjd->bnhij', kh, kh)
    m = jnp.eye(C, dtype=jnp.float32) + bt[..., :, None] * kk * d_strict
    rhs = jnp.concatenate([bt[..., None] * vh, (bt * jnp.exp(g))[..., None] * kh], axis=-1)
    sol = lax.linalg.triangular_solve(m, rhs, left_side=True, lower=True, unit_diagonal=True)
    u0, w = sol[..., :dv], sol[..., dv:]
    aq = jnp.einsum('bnhid,bnhjd->bnhij', qh, kh) * d_incl
    q_in = qh * jnp.exp(g)[..., None]
    k_out = kh * jnp.exp(g[..., -1:] - g)[..., None]
    d_last = jnp.exp(g[..., -1])

    def step(s, xs):
        u0_c, w_c, aq_c, q_c, k_c, d_c = xs
        u = u0_c - jnp.einsum('bhcd,bhdv->bhcv', w_c, s)
        o = jnp.einsum('bhcd,bhdv->bhcv', q_c, s) + jnp.einsum('bhij,bhjv->bhiv', aq_c, u)
        s = d_c[..., None, None] * s + jnp.einsum('bhcd,bhcv->bhdv', k_c, u)
        return s, o

    xs = tuple(jnp.moveaxis(a, 1, 0) for a in (u0, w, aq, q_in, k_out, d_last))
    s, o = lax.scan(step, s0, xs)
    return jnp.transpose(o, (1, 0, 3, 2, 4)).reshape(B, T, H, dv), s


def gdn_mix(h, conv_buf, s0, w_in, conv_w, a_log, dt_bias, norm_g, w_out):
    B, T, _ = h.shape
    proj = h @ w_in
    vw = GDN_V_HEADS * GDN_DV
    qkv = proj[..., :GDN_CONV_DIM]
    z = proj[..., GDN_CONV_DIM:GDN_CONV_DIM + vw].astype(jnp.float32)
    b_logit = proj[..., GDN_CONV_DIM + vw:GDN_CONV_DIM + vw + GDN_V_HEADS].astype(jnp.float32)
    a_in = proj[..., GDN_CONV_DIM + vw + GDN_V_HEADS:].astype(jnp.float32)
    conv_out, new_buf = causal_conv(qkv, conv_buf, conv_w)
    conv_out = conv_out.astype(jnp.float32)
    qkw = GDN_QK_HEADS * GDN_DK
    rep = GDN_V_HEADS // GDN_QK_HEADS
    q = jnp.repeat(l2norm(conv_out[..., :qkw].reshape(B, T, GDN_QK_HEADS, GDN_DK)) * GDN_DK ** -0.5, rep, axis=2)
    k = jnp.repeat(l2norm(conv_out[..., qkw:2 * qkw].reshape(B, T, GDN_QK_HEADS, GDN_DK)), rep, axis=2)
    v = conv_out[..., 2 * qkw:].reshape(B, T, GDN_V_HEADS, GDN_DV)
    beta = jax.nn.sigmoid(b_logit)
    log_a = -jnp.exp(a_log.astype(jnp.float32)) * jax.nn.softplus(a_in + dt_bias.astype(jnp.float32))
    o, s = gdn_chunked(q, k, v, log_a, beta, s0.astype(jnp.float32))
    o = rms_norm(o, norm_g) * jax.nn.silu(z.reshape(B, T, GDN_V_HEADS, GDN_DV))
    return o.reshape(B, T, vw).astype(h.dtype) @ w_out, new_buf, s.astype(h.dtype)


def setup_inputs(seed: int = 0) -> dict:
    key = jax.random.key(seed)
    ks = iter(jax.random.split(key, 48))

    def nrm(shape, scale):
        return jax.random.normal(next(ks), shape, jnp.float32) * scale

    n_pages = PAST_LEN // PAGE_SIZE
    n_phys = (5 * DEC_BATCH * n_pages + 3) // 4
    wbuf = min(WINDOW, PAST_LEN)
    pool_shape = (n_phys, PAGE_SIZE, 2, KV_HEADS, HEAD_DIM)
    perm = jax.random.permutation(next(ks), n_phys)
    page_table = perm[:DEC_BATCH * n_pages].reshape(DEC_BATCH, n_pages).astype(jnp.int32)
    dt = jnp.exp(jax.random.uniform(next(ks), (GDN_V_HEADS,), jnp.float32, math.log(1e-3), math.log(1e-1)))
    dt_bias = dt + jnp.log(-jnp.expm1(-dt))
    a_log = jnp.log(jax.random.uniform(next(ks), (GDN_V_HEADS,), jnp.float32, 1.0, 16.0))
    return {
        'x_prompt': nrm((BATCH, SEQ, D_MODEL), 1.0),
        'x_sample': nrm((DEC_BATCH, DEC_SEQ, D_MODEL), 1.0),
        'cache_nsa_cmp_kv': nrm(pool_shape, 1.0),
        'cache_nsa_sel_kv': nrm(pool_shape, 1.0),
        'cache_nsa_win_kv': nrm((DEC_BATCH, wbuf, 2, KV_HEADS, HEAD_DIM), 1.0),
        'cache_moba_kv': nrm(pool_shape, 1.0),
        'state_hgrn2': nrm((DEC_BATCH, HG_HEADS, HG_DK, HG_DV), 0.5),
        'state_gdn_conv': nrm((DEC_BATCH, GDN_CONV - 1, GDN_CONV_DIM), 1.0),
        'state_gdn_ssm': nrm((DEC_BATCH, GDN_V_HEADS, GDN_DK, GDN_DV), 0.1),
        'page_table': page_table,
        'rel_table': nrm((REL_BUCKETS, N_HEADS), 0.5),
        'ln_mix': 1.0 + nrm((DEPTH, D_MODEL), 0.05),
        'ln_ffn': 1.0 + nrm((DEPTH, D_MODEL), 0.05),
        'ln_final': 1.0 + nrm((D_MODEL,), 0.05),
        'ffn_w_up': nrm((DEPTH, D_MODEL, 2 * FFN_HIDDEN), D_MODEL ** -0.5),
        'ffn_w_down': nrm((DEPTH, FFN_HIDDEN, D_MODEL), FFN_HIDDEN ** -0.5),
        'nsa_w_in': nrm((D_MODEL, NSA_IN), D_MODEL ** -0.5),
        'nsa_cmp_w1': nrm((2, CMP_BLOCK, HEAD_DIM, CMP_HIDDEN), (CMP_BLOCK * HEAD_DIM) ** -0.5),
        'nsa_cmp_w2': nrm((2, CMP_HIDDEN, HEAD_DIM), CMP_HIDDEN ** -0.5),
        'nsa_cmp_pe': nrm((2, CMP_BLOCK, HEAD_DIM), 0.1),
        'nsa_w_out': nrm((N_HEADS * HEAD_DIM, D_MODEL), (N_HEADS * HEAD_DIM) ** -0.5),
        'moba_w_in': nrm((D_MODEL, MOBA_IN), D_MODEL ** -0.5),
        'moba_w_out': nrm((N_HEADS * HEAD_DIM, D_MODEL), (N_HEADS * HEAD_DIM) ** -0.5),
        'hg_w_in': nrm((D_MODEL, HG_IN), D_MODEL ** -0.5),
        'hg_lb_logits': nrm((DEPTH, HG_HEADS * HG_DK), 0.5),
        'hg_norm': 1.0 + nrm((HG_DV,), 0.05),
        'hg_w_out': nrm((HG_HEADS * HG_DV, D_MODEL), (HG_HEADS * HG_DV) ** -0.5),
        'gdn_w_in': nrm((D_MODEL, GDN_IN), D_MODEL ** -0.5),
        'gdn_conv_w': nrm((GDN_CONV, GDN_CONV_DIM), GDN_CONV ** -0.5),
        'gdn_a_log': a_log,
        'gdn_dt_bias': dt_bias,
        'gdn_norm': 1.0 + nrm((GDN_DV,), 0.05),
        'gdn_w_out': nrm((GDN_V_HEADS * GDN_DV, D_MODEL), (GDN_V_HEADS * GDN_DV) ** -0.5),
    }


def reference(x_prompt, x_sample, cache_nsa_cmp_kv, cache_nsa_sel_kv, cache_nsa_win_kv, cache_moba_kv,
              state_hgrn2, state_gdn_conv, state_gdn_ssm, page_table, rel_table, ln_mix, ln_ffn, ln_final,
              ffn_w_up, ffn_w_down, nsa_w_in, nsa_cmp_w1, nsa_cmp_w2, nsa_cmp_pe, nsa_w_out, moba_w_in, moba_w_out,
              hg_w_in, hg_lb_logits, hg_norm, hg_w_out, gdn_w_in, gdn_conv_w, gdn_a_log, gdn_dt_bias, gdn_norm,
              gdn_w_out):
    xp, xs = x_prompt, x_sample
    bp, tp = x_prompt.shape[:2]
    ts = x_sample.shape[1]
    past = page_table.shape[1] * cache_moba_kv.shape[1]
    for layer in range(DEPTH):
        kind = layer % N_MIXERS
        hp = rms_norm(xp, ln_mix[layer])
        hs = rms_norm(xs, ln_mix[layer])
        if kind == 0:
            mp, nsa_cmp_p, nsa_sel_p, nsa_win_p = nsa_prompt(hp, nsa_w_in, nsa_w_out, nsa_cmp_w1, nsa_cmp_w2,
                                                             nsa_cmp_pe, rel_table)
            ms, nsa_cmp_s, nsa_sel_s, nsa_win_s = nsa_sample(hs, cache_nsa_cmp_kv, cache_nsa_sel_kv,
                                                             cache_nsa_win_kv, page_table, nsa_w_in, nsa_w_out,
                                                             nsa_cmp_w1, nsa_cmp_w2, nsa_cmp_pe, rel_table)
        elif kind == 1:
            q, moba_p = moba_project(hp, moba_w_in)
            mp = moba_attn(q, jnp.arange(tp), moba_p, rel_table).astype(hp.dtype) @ moba_w_out
            q, moba_s = moba_project(hs, moba_w_in)
            kv_full = jnp.concatenate([gather_pages(cache_moba_kv, page_table).astype(moba_s.dtype), moba_s], axis=1)
            ms = moba_attn(q, past + jnp.arange(ts), kv_full, rel_table).astype(hs.dtype) @ moba_w_out
        elif kind == 2:
            s0 = jnp.zeros((bp, HG_HEADS, HG_DK, HG_DV), jnp.float32)
            mp, hg_p = hgrn2_mix(hp, s0, hg_w_in, hg_lb_logits, layer, hg_norm, hg_w_out)
            ms, hg_s = hgrn2_mix(hs, state_hgrn2, hg_w_in, hg_lb_logits, layer, hg_norm, hg_w_out)
        else:
            buf0 = jnp.zeros((bp, GDN_CONV - 1, GDN_CONV_DIM), hp.dtype)
            s0 = jnp.zeros((bp, GDN_V_HEADS, GDN_DK, GDN_DV), jnp.float32)
            mp, conv_p, ssm_p = gdn_mix(hp, buf0, s0, gdn_w_in, gdn_conv_w, gdn_a_log, gdn_dt_bias, gdn_norm,
                                        gdn_w_out)
            ms, conv_s, ssm_s = gdn_mix(hs, state_gdn_conv, state_gdn_ssm, gdn_w_in, gdn_conv_w, gdn_a_log,
                                        gdn_dt_bias, gdn_norm, gdn_w_out)
        xp = xp + mp
        xs = xs + ms
        xp = xp + swiglu(rms_norm(xp, ln_ffn[layer]), ffn_w_up[layer], ffn_w_down[layer])
        xs = xs + swiglu(rms_norm(xs, ln_ffn[layer]), ffn_w_up[layer], ffn_w_down[layer])
    y_prompt = rms_norm(xp, ln_final)
    y_sample = rms_norm(xs, ln_final)
    return (y_prompt, y_sample, nsa_cmp_p, nsa_cmp_s, nsa_sel_p, nsa_sel_s, nsa_win_p, nsa_win_s,
            moba_p, moba_s, hg_p, hg_s, conv_p, conv_s, ssm_p, ssm_s)
```

```python
import functools
import math

import jax
import jax.numpy as jnp
import numpy as np
from jax import lax
from jax.experimental import pallas as pl
from jax.experimental.pallas import tpu as pltpu

D_MODEL = 2048
DEPTH = 4
N_MIXERS = 4
HEAD_DIM = 128
N_HEADS = D_MODEL // HEAD_DIM
KV_HEADS = 4
GROUP = N_HEADS // KV_HEADS
ATTN_SCALE = HEAD_DIM ** -0.5
REL_BUCKETS = 32
REL_MAX_DIST = 128
CMP_BLOCK = 32
CMP_STRIDE = 16
CMP_HIDDEN = HEAD_DIM
SEL_BLOCK = 64
N_SEL = 16
WINDOW = 512
FORCE_SCORE = 1.0e4
MOBA_BLOCK = 256
MOBA_TOPK = 3
HG_DK = 128
HG_HEADS = D_MODEL // HG_DK
HG_DV = D_MODEL // HG_HEADS
GDN_DK = 128
GDN_DV = 128
GDN_QK_HEADS = D_MODEL // GDN_DK
GDN_V_HEADS = 2 * GDN_QK_HEADS
GDN_CONV = 4
GDN_CONV_DIM = 2 * GDN_QK_HEADS * GDN_DK + GDN_V_HEADS * GDN_DV
CHUNK = 64
NEG_INF = -1.0e30
NORM_EPS = 1e-6

Q_WIDTH = N_HEADS * HEAD_DIM
KV_WIDTH = 2 * KV_HEADS * HEAD_DIM
NSA_MAIN = Q_WIDTH + 3 * KV_WIDTH
GDN_MAIN = GDN_CONV_DIM + GDN_V_HEADS * GDN_DV

V7X_VMEM_LIMIT_BYTES = 56 * 1024 * 1024
LANE = 128
ATTN_TILE = 256
CMP_TILE = 128
NT_DIMS = (((1,), (1,)), ((), ()))


def _cparams(*sem):
    return pltpu.CompilerParams(dimension_semantics=sem, vmem_limit_bytes=V7X_VMEM_LIMIT_BYTES)


def _row_tile(m, target):
    t = min(m, target)
    while m % t:
        t //= 2
    return t


def _col_tile(n, target):
    t = min(n, target)
    while n % t or t % LANE:
        t -= LANE
    return t


def _split_bf16(x, parts):
    out = []
    for _ in range(parts - 1):
        hi = x.astype(jnp.bfloat16)
        out.append(hi)
        x = x - hi.astype(jnp.float32)
    out.append(x.astype(jnp.bfloat16))
    return out


def _norm_matmul_body(x_ref, g_ref, w_ref, o_ref, h_ref):
    @pl.when(pl.program_id(1) == 0)
    def _():
        x = x_ref[...]
        ms = jnp.mean(x * x, axis=-1, keepdims=True)
        h_ref[...] = (x * lax.rsqrt(ms + NORM_EPS) * g_ref[...]).astype(jnp.bfloat16)

    o_ref[...] = jnp.dot(h_ref[...], w_ref[...], preferred_element_type=jnp.float32)


def norm_matmul(x, g, w):
    m, k = x.shape
    n = w.shape[1]
    tm = _row_tile(m, 512)
    tn = _col_tile(n, 512)
    return pl.pallas_call(
        _norm_matmul_body,
        grid=(m // tm, n // tn),
        in_specs=[pl.BlockSpec((tm, k), lambda i, j: (i, 0)),
                  pl.BlockSpec((1, k), lambda i, j: (0, 0)),
                  pl.BlockSpec((k, tn), lambda i, j: (0, j))],
        out_specs=pl.BlockSpec((tm, tn), lambda i, j: (i, j)),
        out_shape=jax.ShapeDtypeStruct((m, n), jnp.float32),
        scratch_shapes=[pltpu.VMEM((tm, k), jnp.bfloat16)],
        compiler_params=_cparams("parallel", "arbitrary"),
        name="norm_matmul",
    )(x, g.reshape(1, k), w)


def _matmul_res_body(*refs):
    *a_refs, w_ref, r_ref, o_ref = refs
    a = a_refs[0][...]
    for a_ref in a_refs[1:]:
        a = a + a_ref[...]
    o_ref[...] = r_ref[...] + jnp.dot(a.astype(jnp.bfloat16), w_ref[...], preferred_element_type=jnp.float32)


def matmul_res(a_list, w, res):
    m, k = a_list[0].shape
    n = w.shape[1]
    tm = _row_tile(m, 512)
    tn = _col_tile(n, 1024)
    return pl.pallas_call(
        _matmul_res_body,
        grid=(m // tm, n // tn),
        in_specs=[pl.BlockSpec((tm, k), lambda i, j: (i, 0)) for _ in a_list]
        + [pl.BlockSpec((k, tn), lambda i, j: (0, j)),
           pl.BlockSpec((tm, tn), lambda i, j: (i, j))],
        out_specs=pl.BlockSpec((tm, tn), lambda i, j: (i, j)),
        out_shape=jax.ShapeDtypeStruct((m, n), jnp.float32),
        compiler_params=_cparams("parallel", "arbitrary"),
        name="matmul_res",
    )(*a_list, w, res)


def _ffn_body(x_ref, g_ref, wa_ref, wb_ref, wd_ref, o_ref, h_ref, acc_ref):
    j = pl.program_id(1)

    @pl.when(j == 0)
    def _():
        x = x_ref[...]
        ms = jnp.mean(x * x, axis=-1, keepdims=True)
        h_ref[...] = (x * lax.rsqrt(ms + NORM_EPS) * g_ref[...]).astype(jnp.bfloat16)
        acc_ref[...] = jnp.zeros_like(acc_ref)

    h = h_ref[...]
    a = jnp.dot(h, wa_ref[...], preferred_element_type=jnp.float32)
    b = jnp.dot(h, wb_ref[...], preferred_element_type=jnp.float32)
    act = (a * jax.nn.sigmoid(a) * b).astype(jnp.bfloat16)
    acc_ref[...] += jnp.dot(act, wd_ref[...], preferred_element_type=jnp.float32)

    @pl.when(j == pl.num_programs(1) - 1)
    def _():
        o_ref[...] = x_ref[...] + acc_ref[...]


def ffn(x, g, w_up, w_down):
    m, k = x.shape
    hdim = w_down.shape[0]
    tm = _row_tile(m, 512)
    th = _col_tile(hdim, 512)
    nh = hdim // th
    return pl.pallas_call(
        _ffn_body,
        grid=(m // tm, nh),
        in_specs=[pl.BlockSpec((tm, k), lambda i, j: (i, 0)),
                  pl.BlockSpec((1, k), lambda i, j: (0, 0)),
                  pl.BlockSpec((k, th), lambda i, j: (0, j)),
                  pl.BlockSpec((k, th), lambda i, j: (0, j + nh)),
                  pl.BlockSpec((th, k), lambda i, j: (j, 0))],
        out_specs=pl.BlockSpec((tm, k), lambda i, j: (i, 0)),
        out_shape=jax.ShapeDtypeStruct((m, k), jnp.float32),
        scratch_shapes=[pltpu.VMEM((tm, k), jnp.bfloat16), pltpu.VMEM((tm, k), jnp.float32)],
        compiler_params=_cparams("parallel", "arbitrary"),
        name="ffn",
    )(x, g.reshape(1, k), w_up, w_up, w_down)


def _norm_body(x_ref, g_ref, o_ref):
    x = x_ref[...]
    ms = jnp.mean(x * x, axis=-1, keepdims=True)
    o_ref[...] = x * lax.rsqrt(ms + NORM_EPS) * g_ref[...]


def final_norm(x, g):
    m, k = x.shape
    tm = _row_tile(m, 512)
    return pl.pallas_call(
        _norm_body,
        grid=(m // tm,),
        in_specs=[pl.BlockSpec((tm, k), lambda i: (i, 0)), pl.BlockSpec((1, k), lambda i: (0, 0))],
        out_specs=pl.BlockSpec((tm, k), lambda i: (i, 0)),
        out_shape=jax.ShapeDtypeStruct((m, k), jnp.float32),
        compiler_params=_cparams("parallel"),
        name="final_norm",
    )(x, g.reshape(1, k))


def _bucket_np(dist):
    exact = REL_BUCKETS // 2
    d = np.maximum(dist, 0)
    ratio = np.log(np.maximum(d, 1).astype(np.float32) / exact) / math.log(REL_MAX_DIST / exact)
    large = np.minimum(exact + (ratio * (REL_BUCKETS - exact)).astype(np.int32), REL_BUCKETS - 1)
    return np.where(d < exact, d, large)


def _heads_to_rows(t):
    rows, cols, _ = t.shape
    return t.reshape(rows, cols, KV_HEADS, GROUP).transpose(2, 3, 0, 1).reshape(KV_HEADS, GROUP * rows, cols)


def _flash_bias_tiles(rel_table, window):
    i = np.arange(ATTN_TILE)[:, None]
    j = np.arange(ATTN_TILE)[None, :]
    tiles = []
    for d in range(3):
        dist = d * ATTN_TILE + i - j
        ok = dist >= 0
        if window:
            ok = ok & (dist < window)
        b = jnp.where(ok[..., None], rel_table[_bucket_np(dist)], NEG_INF)
        tiles.append(_heads_to_rows(b))
    return jnp.stack(tiles, axis=1)


def _cmp_bias_table(rel_table):
    i = np.arange(CMP_TILE)[:, None]
    x = np.arange(2 * LANE)[None, :]
    dist = i - CMP_STRIDE * (x - 16) - (CMP_BLOCK - 1)
    far = rel_table[REL_BUCKETS - 1]
    b = jnp.where((dist >= 0)[..., None], rel_table[_bucket_np(dist)], far)
    return _heads_to_rows(b)


def _stack_heads(q):
    return jnp.concatenate([q[:, r * HEAD_DIM:(r + 1) * HEAD_DIM] for r in range(GROUP)], axis=0)


def _gate_columns(gt_ref, col0, rows):
    gt = jax.nn.sigmoid(gt_ref[...])
    lane = lax.broadcasted_iota(jnp.int32, gt.shape, 1)
    return [jnp.sum(jnp.where(lane == col0 + r, gt, 0.0), axis=1, keepdims=True) for r in range(GROUP)]


def _unstack_heads(o4, rows, cols=None):
    parts = []
    for r in range(GROUP):
        part = o4[r * rows:(r + 1) * rows]
        if cols is not None:
            part = part * cols[r]
        parts.append(part)
    return jnp.concatenate(parts, axis=1)


def _cmp_select_body(q_ref, kc_ref, dt_ref, gt_ref, o_ref, pen_ref, *, n_sel_blocks):
    g = pl.program_id(1)
    qi = pl.program_id(2)
    tq = CMP_TILE
    rows = GROUP * tq
    ncp = kc_ref.shape[1]
    q4 = (_stack_heads(q_ref[...]) * ATTN_SCALE).astype(jnp.bfloat16)
    kk = kc_ref[0].astype(jnp.bfloat16)
    vv = kc_ref[1].astype(jnp.bfloat16)
    s = lax.dot_general(q4, kk, NT_DIMS, preferred_element_type=jnp.float32)
    shift = (qi * (tq // CMP_STRIDE) + ncp - 16) % ncp
    bias = pltpu.roll(dt_ref[...], shift, 1)
    t_row = qi * tq + (lax.broadcasted_iota(jnp.int32, (rows, ncp), 0) & (tq - 1))
    end_pos = lax.broadcasted_iota(jnp.int32, (rows, ncp), 1) * CMP_STRIDE + (CMP_BLOCK - 1)
    mask = t_row >= end_pos
    s = jnp.where(mask, s + bias, NEG_INF)
    m = jnp.max(s, axis=1, keepdims=True)
    e = jnp.where(mask, jnp.exp(s - m), 0.0)
    p = e / jnp.maximum(jnp.sum(e, axis=1, keepdims=True), 1e-30)
    o4 = jnp.dot(p.astype(jnp.bfloat16), vv, preferred_element_type=jnp.float32)
    o_ref[...] = _unstack_heads(o4, tq, _gate_columns(gt_ref, g * GROUP, tq))

    imp = p[0:tq]
    for r in range(1, GROUP):
        imp = imp + p[r * tq:(r + 1) * tq]
    ratio = SEL_BLOCK // CMP_STRIDE
    c_i = lax.broadcasted_iota(jnp.int32, (ncp, LANE), 0)
    j_i = lax.broadcasted_iota(jnp.int32, (ncp, LANE), 1)
    w = ((c_i >= ratio * j_i - 1) & (c_i <= ratio * j_i + ratio - 1) & (j_i < n_sel_blocks)).astype(jnp.bfloat16)
    score = sum(jnp.dot(part, w, preferred_element_type=jnp.float32) for part in _split_bf16(imp, 3))
    blk = lax.broadcasted_iota(jnp.int32, (tq, LANE), 1)
    tok = qi * tq + lax.broadcasted_iota(jnp.int32, (tq, LANE), 0)
    cur = lax.shift_right_logical(tok, int(math.log2(SEL_BLOCK)))
    forced = (blk == 0) | (blk == cur) | (blk == cur - 1)
    causal = blk <= cur
    score = jnp.where(forced, FORCE_SCORE, score)
    score = jnp.where(causal, score, -1.0)
    score = jnp.where(blk < n_sel_blocks, score, -2.0)
    rank = jnp.zeros((tq, LANE), jnp.int32)
    for mm in range(n_sel_blocks):
        sm = score[:, mm:mm + 1]
        ahead = (sm > score) | ((sm == score) & (blk > mm))
        rank = rank + ahead.astype(jnp.int32)
    chosen = (rank < N_SEL) & causal
    pen_ref[...] = jnp.where(chosen, 0.0, NEG_INF).astype(jnp.bfloat16)


def cmp_select(proj, tail, kc, dt, batch, seq):
    tq = CMP_TILE
    nq = seq // tq
    ncp = kc.shape[3]
    return pl.pallas_call(
        functools.partial(_cmp_select_body, n_sel_blocks=seq // SEL_BLOCK),
        grid=(batch, KV_HEADS, nq),
        in_specs=[pl.BlockSpec((tq, GROUP * HEAD_DIM), lambda b, g, i: (b * nq + i, g)),
                  pl.BlockSpec((None, None, 2, ncp, HEAD_DIM), lambda b, g, i: (b, g, 0, 0, 0)),
                  pl.BlockSpec((None, GROUP * tq, ncp), lambda b, g, i: (g, 0, 0)),
                  pl.BlockSpec((tq, LANE), lambda b, g, i: (b * nq + i, 0))],
        out_specs=[pl.BlockSpec((tq, GROUP * HEAD_DIM), lambda b, g, i: (b * nq + i, g)),
                   pl.BlockSpec((None, None, tq, LANE), lambda b, g, i: (b, g, i, 0))],
        out_shape=[jax.ShapeDtypeStruct((batch * seq, Q_WIDTH), jnp.float32),
                   jax.ShapeDtypeStruct((batch, KV_HEADS, seq, LANE), jnp.bfloat16)],
        compiler_params=_cparams("parallel", "parallel", "arbitrary"),
        name="cmp_select",
    )(proj, kc.reshape(batch, KV_HEADS, 2, ncp, HEAD_DIM), dt, tail)


def _flash_body(*refs, blk_shift, pen_rows, k_back, gate_col0):
    it = iter(refs)
    q_ref, k_ref, v_ref, bt_ref = next(it), next(it), next(it), next(it)
    pen_ref = next(it) if pen_rows else None
    gt_ref = next(it) if gate_col0 is not None else None
    o_ref, m_ref, l_ref, acc_ref, qa_ref = next(it), next(it), next(it), next(it), next(it)
    g = pl.program_id(1)
    qi = pl.program_id(2)
    tq = tk = ATTN_TILE
    rows = GROUP * tq
    n_cls = bt_ref.shape[0]

    q4 = (_stack_heads(q_ref[...]) * ATTN_SCALE).astype(jnp.bfloat16)
    if pen_rows:
        pen = pen_ref[...]
        if pen_rows == tq:
            pen = jnp.concatenate([pen] * GROUP, axis=0)
        qa_ref[...] = jnp.concatenate([q4, pen], axis=1)
    else:
        qa_ref[...] = q4
    m_ref[...] = jnp.full(m_ref.shape, NEG_INF, jnp.float32)
    l_ref[...] = jnp.zeros(l_ref.shape, jnp.float32)
    acc_ref[...] = jnp.zeros(acc_ref.shape, jnp.float32)

    def step(kj, carry):
        k0 = pl.multiple_of(kj * tk, tk)
        ka = k_ref[pl.ds(k0, tk), :].astype(jnp.bfloat16)
        v = v_ref[pl.ds(k0, tk), :].astype(jnp.bfloat16)
        if pen_rows:
            kblk = lax.shift_right_logical(k0 + lax.broadcasted_iota(jnp.int32, (tk, LANE), 0), blk_shift)
            onehot = (kblk == lax.broadcasted_iota(jnp.int32, (tk, LANE), 1)).astype(jnp.bfloat16)
            ka = jnp.concatenate([ka, onehot], axis=1)
        s = lax.dot_general(qa_ref[...], ka, NT_DIMS, preferred_element_type=jnp.float32)
        s = s + bt_ref[jnp.minimum(qi - kj, n_cls - 1)]
        m_prev = m_ref[...]
        m_new = jnp.maximum(m_prev, jnp.max(s, axis=1, keepdims=True))
        alpha = jnp.exp(m_prev - m_new)
        p = jnp.exp(s - m_new)
        l_ref[...] = alpha * l_ref[...] + jnp.sum(p, axis=1, keepdims=True)
        acc_ref[...] = alpha * acc_ref[...] + jnp.dot(p.astype(jnp.bfloat16), v, preferred_element_type=jnp.float32)
        m_ref[...] = m_new
        return carry

    k_lo = jnp.maximum(qi - k_back, 0) if k_back is not None else 0
    lax.fori_loop(k_lo, qi + 1, step, 0)
    o4 = acc_ref[...] / jnp.maximum(l_ref[...], 1e-30)
    cols = _gate_columns(gt_ref, gate_col0 + g * GROUP, tq) if gate_col0 is not None else None
    o_ref[...] = _unstack_heads(o4, tq, cols)


def flash_attention(q_arr, kv_arr, k_col, v_col, bias_tiles, batch, seq, *, pen=None, blk_shift=0,
                    k_back=None, gate_arr=None, gate_col0=None):
    tq = ATTN_TILE
    nq = seq // tq
    rows = GROUP * tq
    pen_rows = 0
    in_specs = [pl.BlockSpec((tq, GROUP * HEAD_DIM), lambda b, g, i: (b * nq + i, g)),
                pl.BlockSpec((seq, HEAD_DIM), lambda b, g, i: (b, k_col + g)),
                pl.BlockSpec((seq, HEAD_DIM), lambda b, g, i: (b, v_col + g)),
                pl.BlockSpec((None,) + bias_tiles.shape[1:], lambda b, g, i: (g, 0, 0, 0))]
    args = [q_arr, kv_arr, kv_arr, bias_tiles]
    if pen is not None:
        if pen.ndim == 4:
            pen_rows = tq
            in_specs.append(pl.BlockSpec((None, None, tq, LANE), lambda b, g, i: (b, g, i, 0)))
        else:
            pen_rows = rows
            in_specs.append(pl.BlockSpec((None, None, None, rows, LANE), lambda b, g, i: (b, g, i, 0, 0)))
        args.append(pen)
    if gate_arr is not None:
        in_specs.append(pl.BlockSpec((tq, LANE), lambda b, g, i: (b * nq + i, 0)))
        args.append(gate_arr)
    qa_width = HEAD_DIM + (LANE if pen_rows else 0)
    return pl.pallas_call(
        functools.partial(_flash_body, blk_shift=blk_shift, pen_rows=pen_rows, k_back=k_back,
                          gate_col0=gate_col0 if gate_arr is not None else None),
        grid=(batch, KV_HEADS, nq),
        in_specs=in_specs,
        out_specs=pl.BlockSpec((tq, GROUP * HEAD_DIM), lambda b, g, i: (b * nq + i, g)),
        out_shape=jax.ShapeDtypeStruct((batch * seq, Q_WIDTH), jnp.float32),
        scratch_shapes=[pltpu.VMEM((rows, 1), jnp.float32), pltpu.VMEM((rows, 1), jnp.float32),
                        pltpu.VMEM((rows, HEAD_DIM), jnp.float32), pltpu.VMEM((rows, qa_width), jnp.bfloat16)],
        compiler_params=_cparams("parallel", "parallel", "arbitrary"),
        name="flash_attention",
    )(*args)


def _moba_gate_body(q_ref, k_ref, pen_ref, km_ref, *, n_blocks):
    qi = pl.program_id(2)
    tq = ATTN_TILE
    rows = GROUP * tq

    @pl.when(qi == 0)
    def _():
        k = k_ref[...]
        kmean = jnp.sum(k.reshape(n_blocks, MOBA_BLOCK, HEAD_DIM), axis=1) / MOBA_BLOCK
        km_ref[...] = jnp.concatenate([kmean, jnp.zeros((LANE - n_blocks, HEAD_DIM), jnp.float32)], axis=0)

    qh, ql = _split_bf16(_stack_heads(q_ref[...]), 2)
    kh, kl = _split_bf16(km_ref[...], 2)
    gate = (lax.dot_general(qh, kh, NT_DIMS, preferred_element_type=jnp.float32)
            + lax.dot_general(qh, kl, NT_DIMS, preferred_element_type=jnp.float32)
            + lax.dot_general(ql, kh, NT_DIMS, preferred_element_type=jnp.float32))
    blk = lax.broadcasted_iota(jnp.int32, (rows, LANE), 1)
    tok = qi * tq + (lax.broadcasted_iota(jnp.int32, (rows, LANE), 0) & (tq - 1))
    own = lax.shift_right_logical(tok, int(math.log2(MOBA_BLOCK)))
    gate = jnp.where(blk < own, gate, NEG_INF)
    rank = jnp.zeros((rows, LANE), jnp.int32)
    for mm in range(n_blocks):
        gm = gate[:, mm:mm + 1]
        ahead = (gm > gate) | ((gm == gate) & (blk > mm))
        rank = rank + ahead.astype(jnp.int32)
    chosen = ((rank < MOBA_TOPK) & (blk < own)) | (blk == own)
    pen_ref[...] = jnp.where(chosen, 0.0, NEG_INF).astype(jnp.bfloat16)


def moba_gate(proj, k_col, batch, seq):
    tq = ATTN_TILE
    nq = seq // tq
    rows = GROUP * tq
    return pl.pallas_call(
        functools.partial(_moba_gate_body, n_blocks=seq // MOBA_BLOCK),
        grid=(batch, KV_HEADS, nq),
        in_specs=[pl.BlockSpec((tq, GROUP * HEAD_DIM), lambda b, g, i: (b * nq + i, g)),
                  pl.BlockSpec((seq, HEAD_DIM), lambda b, g, i: (b, k_col + g))],
        out_specs=pl.BlockSpec((None, None, None, rows, LANE), lambda b, g, i: (b, g, i, 0, 0)),
        out_shape=jax.ShapeDtypeStruct((batch, KV_HEADS, nq, rows, LANE), jnp.bfloat16),
        scratch_shapes=[pltpu.VMEM((LANE, HEAD_DIM), jnp.float32)],
        compiler_params=_cparams("parallel", "parallel", "arbitrary"),
        name="moba_gate",
    )(proj, proj)


def _rms_norm(x, g):
    xf = x.astype(jnp.float32)
    y = xf * lax.rsqrt(jnp.mean(xf * xf, axis=-1, keepdims=True) + NORM_EPS)
    return (y * g.astype(jnp.float32)).astype(x.dtype)


def _l2norm(x):
    return x * lax.rsqrt(jnp.sum(x * x, axis=-1, keepdims=True) + NORM_EPS)


def _masked_softmax(logits, mask):
    logits = jnp.where(mask, logits, NEG_INF)
    m = jnp.max(logits, axis=-1, keepdims=True)
    e = jnp.where(mask, jnp.exp(logits - m), 0.0)
    return e / jnp.maximum(jnp.sum(e, axis=-1, keepdims=True), 1e-30)


def _t5_bucket(dist):
    exact = REL_BUCKETS // 2
    d = jnp.maximum(dist, 0)
    ratio = jnp.log(jnp.maximum(d, 1).astype(jnp.float32) / exact) / math.log(REL_MAX_DIST / exact)
    large = jnp.minimum(exact + (ratio * (REL_BUCKETS - exact)).astype(jnp.int32), REL_BUCKETS - 1)
    return jnp.where(d < exact, d, large)


def _rel_bias(rel_table, dist):
    return rel_table.astype(jnp.float32)[_t5_bucket(dist)]


def _gather_pages(pool, page_table):
    rows = pool[page_table]
    return rows.reshape(page_table.shape[0], page_table.shape[1] * pool.shape[1], *pool.shape[2:])


def _causal_conv(x, buf, w):
    T = x.shape[1]
    xp = jnp.concatenate([buf.astype(x.dtype), x], axis=1)
    y = xp[:, 0:T] * w[0]
    for i in range(1, GDN_CONV):
        y = y + xp[:, i:i + T] * w[i]
    return jax.nn.silu(y), xp[:, T:]


def _nsa_split(proj):
    B, T, _ = proj.shape
    q = proj[..., :Q_WIDTH].reshape(B, T, KV_HEADS, GROUP, HEAD_DIM)
    kv_c, kv_s, kv_w = (proj[..., Q_WIDTH + c * KV_WIDTH:Q_WIDTH + (c + 1) * KV_WIDTH]
                        .reshape(B, T, 2, KV_HEADS, HEAD_DIM) for c in range(3))
    gates = jax.nn.sigmoid(proj[..., NSA_MAIN:NSA_MAIN + 3 * N_HEADS]).reshape(B, T, 3, KV_HEADS, GROUP)
    return q, kv_c, kv_s, kv_w, gates


def _nsa_compress(kv, cmp_w1, cmp_w2, cmp_pe):
    B, Tk = kv.shape[:2]
    nc = (Tk - CMP_BLOCK) // CMP_STRIDE + 1
    n_part = CMP_BLOCK // CMP_STRIDE
    n_chunk = nc + n_part - 1
    chunks = kv[:, :n_chunk * CMP_STRIDE].astype(jnp.float32).reshape(B, n_chunk, CMP_STRIDE, 2, KV_HEADS, HEAD_DIM)
    w1 = cmp_w1.astype(jnp.float32)
    part = jnp.einsum('bnlcgd,crldh->bncrgh', chunks, w1.reshape(2, n_part, CMP_STRIDE, HEAD_DIM, CMP_HIDDEN))
    hid = jnp.einsum('cld,cldh->ch', cmp_pe.astype(jnp.float32), w1)[None, None, :, None, :]
    for r in range(n_part):
        hid = hid + part[:, r:r + nc, :, r]
    return jnp.einsum('bncgh,chd->bncgd', jax.nn.gelu(hid), cmp_w2.astype(jnp.float32))


def _nsa_cmp_attn(q, q_pos, kc, rel_table):
    nc = kc.shape[1]
    end_pos = jnp.arange(nc) * CMP_STRIDE + (CMP_BLOCK - 1)
    dist = q_pos[:, None] - end_pos[None, :]
    bias = _rel_bias(rel_table, dist).reshape(q_pos.shape[0], nc, KV_HEADS, GROUP).transpose(0, 2, 3, 1)
    logits = jnp.einsum('bqgrd,bcgd->bqgrc', q, kc[:, :, 0]) * ATTN_SCALE + bias
    p = _masked_softmax(logits, (dist >= 0)[:, None, None, :])
    return jnp.einsum('bqgrc,bcgd->bqgrd', p, kc[:, :, 1]), p


def _nsa_select(p_cmp, q_pos, tk):
    ns = -(-tk // SEL_BLOCK)
    ratio = SEL_BLOCK // CMP_STRIDE
    imp = p_cmp.sum(axis=3)
    nc = imp.shape[-1]
    imp = jnp.pad(imp, ((0, 0), (0, 0), (0, 0), (1, ratio * ns + ratio - 1 - nc)))
    score = imp[..., :ratio * ns].reshape(*imp.shape[:3], ns, ratio).sum(-1) + imp[..., ratio::ratio]
    cur = q_pos // SEL_BLOCK
    blk = jnp.arange(ns)
    forced = (blk[None, :] == 0) | (blk[None, :] == cur[:, None]) | (blk[None, :] == cur[:, None] - 1)
    causal = blk[None, :] <= cur[:, None]
    score = jnp.where(forced[None, :, None, :], FORCE_SCORE, score)
    score = jnp.where(causal[None, :, None, :], score, -1.0)
    _, idx = lax.top_k(score, min(N_SEL, ns))
    valid = idx <= cur[None, :, None, None]
    return idx, valid


def _block_mask(idx, valid, n_blocks, block, tk):
    hit = (idx[..., None] == jnp.arange(n_blocks)) & valid[..., None]
    return jnp.repeat(jnp.any(hit, axis=-2), block, axis=-1)[..., :tk]


def _dense_attn(q, q_pos, kv, key_ok, rel_table):
    tk = kv.shape[1]
    dist = q_pos[:, None] - jnp.arange(tk)[None, :]
    bias = _rel_bias(rel_table, dist).reshape(q_pos.shape[0], tk, KV_HEADS, GROUP).transpose(0, 2, 3, 1)
    logits = jnp.einsum('bqgrd,bkgd->bqgrk', q, kv[:, :, 0]) * ATTN_SCALE + bias
    p = _masked_softmax(logits, key_ok & (dist >= 0)[None, :, None, None, :])
    return jnp.einsum('bqgrk,bkgd->bqgrd', p, kv[:, :, 1])


def _band_attn(q, q_pos, kv_band, k_pos, rel_table):
    dist = q_pos[:, :, None] - k_pos[:, None, :]
    mask = (dist >= 0) & (dist < WINDOW) & (k_pos[:, None, :] >= 0)
    n, qb, kb = dist.shape
    bias = _rel_bias(rel_table, dist).reshape(n, qb, kb, KV_HEADS, GROUP).transpose(0, 1, 3, 4, 2)
    kvf = kv_band.astype(jnp.float32)
    logits = jnp.einsum('bnqgrd,bnkgd->bnqgrk', q, kvf[:, :, :, 0]) * ATTN_SCALE + bias
    p = _masked_softmax(logits, mask[:, :, None, None, :])
    return jnp.einsum('bnqgrk,bnkgd->bnqgrd', p, kvf[:, :, :, 1])


def _nsa_sample(proj, cache_c, cache_s, cache_w, page_table, cmp_w1, cmp_w2, cmp_pe, rel_table):
    B, T, _ = proj.shape
    past = page_table.shape[1] * cache_c.shape[1]
    q, kv_c, kv_s, kv_w, gates = _nsa_split(proj)
    pos = past + jnp.arange(T)
    kv_c_full = jnp.concatenate([_gather_pages(cache_c, page_table), kv_c], axis=1)
    kv_s_full = jnp.concatenate([_gather_pages(cache_s, page_table), kv_s], axis=1)
    wbuf = cache_w.shape[1]
    band = jnp.concatenate([cache_w, kv_w], axis=1)
    band_k_pos = past - wbuf + jnp.arange(wbuf + T)
    kc = _nsa_compress(kv_c_full, cmp_w1, cmp_w2, cmp_pe)
    o_cmp, p_cmp = _nsa_cmp_attn(q, pos, kc, rel_table)
    tk = kv_s_full.shape[1]
    idx, valid = _nsa_select(p_cmp, pos, tk)
    key_ok = _block_mask(idx, valid, -(-tk // SEL_BLOCK), SEL_BLOCK, tk)[:, :, :, None, :]
    o_sel = _dense_attn(q, pos, kv_s_full, key_ok, rel_table)
    o_win = _band_attn(q[:, None], pos[None, :], band[:, None], band_k_pos[None, :], rel_table)[:, 0]
    o = gates[:, :, 0, ..., None] * o_cmp + gates[:, :, 1, ..., None] * o_sel + gates[:, :, 2, ..., None] * o_win
    return o.reshape(B, T, Q_WIDTH), kv_c, kv_s, band[:, -wbuf:]


def _moba_sample(proj, cache_kv, page_table, rel_table):
    B, T, _ = proj.shape
    past = page_table.shape[1] * cache_kv.shape[1]
    q = proj[..., :Q_WIDTH].reshape(B, T, KV_HEADS, GROUP, HEAD_DIM)
    kv_new = proj[..., Q_WIDTH:].reshape(B, T, 2, KV_HEADS, HEAD_DIM)
    kv = jnp.concatenate([_gather_pages(cache_kv, page_table), kv_new], axis=1)
    q_pos = past + jnp.arange(T)
    tk = kv.shape[1]
    nb = -(-tk // MOBA_BLOCK)
    kpad = jnp.pad(kv[:, :, 0], ((0, 0), (0, nb * MOBA_BLOCK - tk), (0, 0), (0, 0)))
    kmean = jnp.mean(kpad.reshape(B, nb, MOBA_BLOCK, KV_HEADS, HEAD_DIM), axis=2)
    own = q_pos // MOBA_BLOCK
    gate = jnp.einsum('bqgrd,bngd->bqgrn', q, kmean)
    is_past = jnp.arange(nb)[None, :] < own[:, None]
    gate = jnp.where(is_past[:, None, None, :], gate, NEG_INF)
    _, idx = lax.top_k(gate, min(MOBA_TOPK, nb))
    valid = idx < own[:, None, None, None]
    own_ok = (jnp.arange(tk)[None, :] // MOBA_BLOCK) == own[:, None]
    key_ok = _block_mask(idx, valid, nb, MOBA_BLOCK, tk) | own_ok[None, :, None, None, :]
    o = _dense_attn(q, q_pos, kv, key_ok, rel_table)
    return o.reshape(B, T, Q_WIDTH), kv_new


def _gla_chunked(q, k, v, logf, s0):
    B, T, H, dk = q.shape
    dv = v.shape[-1]
    C = math.gcd(T, CHUNK)
    N = T // C
    q, k, v, logf = (a.reshape(B, N, C, *a.shape[2:]) for a in (q, k, v, logf))
    b = jnp.cumsum(logf, axis=2)
    b_ref = b[:, :, C // 2:C // 2 + 1]
    a = jnp.einsum('bnihd,bnjhd->bnhij', q * jnp.exp(b - b_ref), k * jnp.exp(b_ref - b))
    causal = jnp.arange(C)[:, None] >= jnp.arange(C)[None, :]
    a = jnp.where(causal, a, 0.0)
    o_intra = jnp.einsum('bnhij,bnjhv->bnihv', a, v)
    q_in = q * jnp.exp(b)
    k_out = k * jnp.exp(b[:, :, -1:] - b)
    d_last = jnp.exp(b[:, :, -1])

    def step(s, xs):
        q_c, k_c, v_c, d_c = xs
        o = jnp.einsum('bihd,bhdv->bihv', q_c, s)
        s = d_c[..., None] * s + jnp.einsum('bjhd,bjhv->bhdv', k_c, v_c)
        return s, o

    s, o_inter = lax.scan(step, s0, tuple(jnp.moveaxis(t, 1, 0) for t in (q_in, k_out, v, d_last)))
    o = o_intra + jnp.moveaxis(o_inter, 0, 1)
    return o.reshape(B, T, H, dv), s


def _hgrn2_core(proj, s0, lb_logits, layer, norm_g):
    B, T, _ = proj.shape
    dk = HG_HEADS * HG_DK
    dv = HG_HEADS * HG_DV
    q, f, i, g = jnp.split(proj, [dk, 2 * dk, 2 * dk + dv], axis=-1)
    p = jax.nn.softmax(lb_logits.astype(jnp.float32), axis=0)
    lb = (jnp.cumsum(p, axis=0) - p[0])[layer]
    fg = lb + (1.0 - lb) * jax.nn.sigmoid(f)
    shp = (B, T, HG_HEADS, HG_DK)
    o, s = _gla_chunked((jax.nn.silu(q) * HG_DK ** -0.5).reshape(shp), (1.0 - fg).reshape(shp),
                        i.reshape(B, T, HG_HEADS, HG_DV), jnp.log(fg).reshape(shp), s0.astype(jnp.float32))
    o = _rms_norm(o, norm_g) * jax.nn.silu(g.reshape(B, T, HG_HEADS, HG_DV))
    return o.reshape(B, T, dv), s


def _gdn_chunked(q, k, v, log_a, beta, s0):
    B, T, H, dk = q.shape
    dv = v.shape[-1]
    C = math.gcd(T, CHUNK)
    N = T // C

    def heads_first(a):
        return jnp.moveaxis(a.reshape(B, N, C, *a.shape[2:]), 3, 2)

    qh, kh, vh, bt = heads_first(q), heads_first(k), heads_first(v), heads_first(beta)
    g = jnp.cumsum(heads_first(log_a), axis=-1)
    ar = jnp.arange(C)
    strict = ar[:, None] > ar[None, :]
    incl = ar[:, None] >= ar[None, :]
    gdiff = g[..., :, None] - g[..., None, :]
    d_strict = jnp.where(strict, jnp.exp(jnp.where(strict, gdiff, 0.0)), 0.0)
    d_incl = jnp.where(incl, jnp.exp(jnp.where(incl, gdiff, 0.0)), 0.0)
    kk = jnp.einsum('bnhid,bnhjd->bnhij', kh, kh)
    m = jnp.eye(C, dtype=jnp.float32) + bt[..., :, None] * kk * d_strict
    rhs = jnp.concatenate([bt[..., None] * vh, (bt * jnp.exp(g))[..., None] * kh], axis=-1)
    sol = lax.linalg.triangular_solve(m, rhs, left_side=True, lower=True, unit_diagonal=True)
    u0, w = sol[..., :dv], sol[..., dv:]
    aq = jnp.einsum('bnhid,bnhjd->bnhij', qh, kh) * d_incl
    q_in = qh * jnp.exp(g)[..., None]
    k_out = kh * jnp.exp(g[..., -1:] - g)[..., None]
    d_last = jnp.exp(g[..., -1])

    def step(s, xs):
        u0_c, w_c, aq_c, q_c, k_c, d_c = xs
        u = u0_c - jnp.einsum('bhcd,bhdv->bhcv', w_c, s)
        o = jnp.einsum('bhcd,bhdv->bhcv', q_c, s) + jnp.einsum('bhij,bhjv->bhiv', aq_c, u)
        s = d_c[..., None, None] * s + jnp.einsum('bhcd,bhcv->bhdv', k_c, u)
        return s, o

    xs = tuple(jnp.moveaxis(a, 1, 0) for a in (u0, w, aq, q_in, k_out, d_last))
    s, o = lax.scan(step, s0, xs)
    return jnp.transpose(o, (1, 0, 3, 2, 4)).reshape(B, T, H, dv), s


def _gdn_core(proj, conv_buf, s0, conv_w, a_log, dt_bias, norm_g):
    B, T, _ = proj.shape
    vw = GDN_V_HEADS * GDN_DV
    qkv = proj[..., :GDN_CONV_DIM]
    z = proj[..., GDN_CONV_DIM:GDN_MAIN]
    b_logit = proj[..., GDN_MAIN:GDN_MAIN + GDN_V_HEADS]
    a_in = proj[..., GDN_MAIN + GDN_V_HEADS:GDN_MAIN + 2 * GDN_V_HEADS]
    conv_out, new_buf = _causal_conv(qkv, conv_buf, conv_w)
    qkw = GDN_QK_HEADS * GDN_DK
    rep = GDN_V_HEADS // GDN_QK_HEADS
    q = jnp.repeat(_l2norm(conv_out[..., :qkw].reshape(B, T, GDN_QK_HEADS, GDN_DK)) * GDN_DK ** -0.5, rep, axis=2)
    k = jnp.repeat(_l2norm(conv_out[..., qkw:2 * qkw].reshape(B, T, GDN_QK_HEADS, GDN_DK)), rep, axis=2)
    v = conv_out[..., 2 * qkw:].reshape(B, T, GDN_V_HEADS, GDN_DV)
    beta = jax.nn.sigmoid(b_logit)
    log_a = -jnp.exp(a_log.astype(jnp.float32)) * jax.nn.softplus(a_in + dt_bias.astype(jnp.float32))
    o, s = _gdn_chunked(q, k, v, log_a, beta, s0.astype(jnp.float32))
    o = _rms_norm(o, norm_g) * jax.nn.silu(z.reshape(B, T, GDN_V_HEADS, GDN_DV))
    return o.reshape(B, T, vw), new_buf, s


def _pad_cols(w, mult=LANE):
    n = w.shape[1]
    return jnp.pad(w, ((0, 0), (0, (-n) % mult)))


def _nsa_prompt(main, tail, batch, seq, cmp_w1, cmp_w2, cmp_pe, rel_table):
    kv = main[:, Q_WIDTH:].reshape(batch, seq, 3, 2, KV_HEADS, HEAD_DIM)
    kv_c, kv_s, kv_w = kv[:, :, 0], kv[:, :, 1], kv[:, :, 2]
    kc = _nsa_compress(kv_c, cmp_w1, cmp_w2, cmp_pe)
    ncp = -(-kc.shape[1] // (2 * LANE)) * (2 * LANE)
    kc = jnp.pad(kc, ((0, 0), (0, ncp - kc.shape[1]), (0, 0), (0, 0), (0, 0))).transpose(0, 3, 2, 1, 4)
    o_cmp, pen = cmp_select(main, tail, kc, _cmp_bias_table(rel_table), batch, seq)
    col = Q_WIDTH // HEAD_DIM
    o_sel = flash_attention(main, main, col + 2 * KV_HEADS, col + 3 * KV_HEADS, _flash_bias_tiles(rel_table, 0),
                            batch, seq, pen=pen, blk_shift=int(math.log2(SEL_BLOCK)),
                            gate_arr=tail, gate_col0=N_HEADS)
    o_win = flash_attention(main, main, col + 4 * KV_HEADS, col + 5 * KV_HEADS,
                            _flash_bias_tiles(rel_table, WINDOW), batch, seq, k_back=WINDOW // ATTN_TILE,
                            gate_arr=tail, gate_col0=2 * N_HEADS)
    return [o_cmp, o_sel, o_win], kv_c, kv_s, kv_w[:, -min(WINDOW, seq):]


def _moba_prompt(proj, batch, seq, rel_table):
    col = Q_WIDTH // HEAD_DIM
    pen = moba_gate(proj, col, batch, seq)
    o = flash_attention(proj, proj, col, col + KV_HEADS, _flash_bias_tiles(rel_table, 0), batch, seq,
                        pen=pen, blk_shift=int(math.log2(MOBA_BLOCK)))
    return o, proj[:, Q_WIDTH:].reshape(batch, seq, 2, KV_HEADS, HEAD_DIM)


def kernel(x_prompt, x_sample, cache_nsa_cmp_kv, cache_nsa_sel_kv, cache_nsa_win_kv, cache_moba_kv,
           state_hgrn2, state_gdn_conv, state_gdn_ssm, page_table, rel_table, ln_mix, ln_ffn, ln_final,
           ffn_w_up, ffn_w_down, nsa_w_in, nsa_cmp_w1, nsa_cmp_w2, nsa_cmp_pe, nsa_w_out, moba_w_in, moba_w_out,
           hg_w_in, hg_lb_logits, hg_norm, hg_w_out, gdn_w_in, gdn_conv_w, gdn_a_log, gdn_dt_bias, gdn_norm,
           gdn_w_out):
    bf = jnp.bfloat16
    bp, tp = x_prompt.shape[:2]
    bs, ts = x_sample.shape[:2]
    assert tp % ATTN_TILE == 0 and WINDOW % ATTN_TILE == 0 and ATTN_TILE == MOBA_BLOCK
    xp = x_prompt.reshape(bp * tp, D_MODEL)
    xs = x_sample.reshape(bs * ts, D_MODEL)

    for layer in range(DEPTH):
        kind = layer % N_MIXERS
        g_mix = ln_mix[layer]
        if kind == 0:
            w_main, w_tail = nsa_w_in[:, :NSA_MAIN].astype(bf), _pad_cols(nsa_w_in[:, NSA_MAIN:]).astype(bf)
            main_p, tail_p = norm_matmul(xp, g_mix, w_main), norm_matmul(xp, g_mix, w_tail)
            ps = jnp.concatenate([norm_matmul(xs, g_mix, w_main), norm_matmul(xs, g_mix, w_tail)], axis=-1)
            op, nsa_cmp_p, nsa_sel_p, nsa_win_p = _nsa_prompt(main_p, tail_p, bp, tp, nsa_cmp_w1, nsa_cmp_w2,
                                                              nsa_cmp_pe, rel_table)
            os_, nsa_cmp_s, nsa_sel_s, nsa_win_s = _nsa_sample(ps.reshape(bs, ts, -1), cache_nsa_cmp_kv,
                                                              cache_nsa_sel_kv, cache_nsa_win_kv, page_table,
                                                              nsa_cmp_w1, nsa_cmp_w2, nsa_cmp_pe, rel_table)
            w_out = nsa_w_out.astype(bf)
        elif kind == 1:
            w_in = moba_w_in.astype(bf)
            pp = norm_matmul(xp, g_mix, w_in)
            ps = norm_matmul(xs, g_mix, w_in).reshape(bs, ts, -1)
            op, moba_p = _moba_prompt(pp, bp, tp, rel_table)
            os_, moba_s = _moba_sample(ps, cache_moba_kv, page_table, rel_table)
            op = [op]
            w_out = moba_w_out.astype(bf)
        elif kind == 2:
            w_in = hg_w_in.astype(bf)
            pp = norm_matmul(xp, g_mix, w_in).reshape(bp, tp, -1)
            ps = norm_matmul(xs, g_mix, w_in).reshape(bs, ts, -1)
            s0 = jnp.zeros((bp, HG_HEADS, HG_DK, HG_DV), jnp.float32)
            op, hg_p = _hgrn2_core(pp, s0, hg_lb_logits, layer, hg_norm)
            os_, hg_s = _hgrn2_core(ps, state_hgrn2, hg_lb_logits, layer, hg_norm)
            op = [op.reshape(bp * tp, -1)]
            w_out = hg_w_out.astype(bf)
        else:
            w_main, w_tail = gdn_w_in[:, :GDN_MAIN].astype(bf), _pad_cols(gdn_w_in[:, GDN_MAIN:]).astype(bf)
            pp = jnp.concatenate([norm_matmul(xp, g_mix, w_main), norm_matmul(xp, g_mix, w_tail)], axis=-1)
            ps = jnp.concatenate([norm_matmul(xs, g_mix, w_main), norm_matmul(xs, g_mix, w_tail)], axis=-1)
            buf0 = jnp.zeros((bp, GDN_CONV - 1, GDN_CONV_DIM), jnp.float32)
            s0 = jnp.zeros((bp, GDN_V_HEADS, GDN_DK, GDN_DV), jnp.float32)
            op, conv_p, ssm_p = _gdn_core(pp.reshape(bp, tp, -1), buf0, s0, gdn_conv_w, gdn_a_log, gdn_dt_bias,
                                          gdn_norm)
            os_, conv_s, ssm_s = _gdn_core(ps.reshape(bs, ts, -1), state_gdn_conv, state_gdn_ssm, gdn_conv_w,
                                           gdn_a_log, gdn_dt_bias, gdn_norm)
            op = [op.reshape(bp * tp, -1)]
            w_out = gdn_w_out.astype(bf)
        xp = matmul_res(op, w_out, xp)
        xs = matmul_res([os_.reshape(bs * ts, -1)], w_out, xs)
        w_up, w_down = ffn_w_up[layer].astype(bf), ffn_w_down[layer].astype(bf)
        xp = ffn(xp, ln_ffn[layer], w_up, w_down)
        xs = ffn(xs, ln_ffn[layer], w_up, w_down)
    y_prompt = final_norm(xp, ln_final).reshape(bp, tp, D_MODEL)
    y_sample = final_norm(xs, ln_final).reshape(bs, ts, D_MODEL)
    return (y_prompt, y_sample, nsa_cmp_p, nsa_cmp_s, nsa_sel_p, nsa_sel_s, nsa_win_p, nsa_win_s,
            moba_p, moba_s, hg_p, hg_s, conv_p, conv_s, ssm_p, ssm_s)
```

```python
import functools
import math

import jax
import jax.numpy as jnp
import numpy as np
from jax import lax
from jax.experimental import pallas as pl
from jax.experimental.pallas import tpu as pltpu

D_MODEL = 2048
DEPTH = 4
N_MIXERS = 4
HEAD_DIM = 128
N_HEADS = D_MODEL // HEAD_DIM
KV_HEADS = 4
GROUP = N_HEADS // KV_HEADS
ATTN_SCALE = HEAD_DIM ** -0.5
REL_BUCKETS = 32
REL_MAX_DIST = 128
CMP_BLOCK = 32
CMP_STRIDE = 16
CMP_HIDDEN = HEAD_DIM
SEL_BLOCK = 64
N_SEL = 16
WINDOW = 512
FORCE_SCORE = 1.0e4
MOBA_BLOCK = 256
MOBA_TOPK = 3
HG_DK = 128
HG_HEADS = D_MODEL // HG_DK
HG_DV = D_MODEL // HG_HEADS
GDN_DK = 128
GDN_DV = 128
GDN_QK_HEADS = D_MODEL // GDN_DK
GDN_V_HEADS = 2 * GDN_QK_HEADS
GDN_REP = GDN_V_HEADS // GDN_QK_HEADS
GDN_CONV = 4
GDN_CONV_DIM = 2 * GDN_QK_HEADS * GDN_DK + GDN_V_HEADS * GDN_DV
CHUNK = 64
NEG_INF = -1.0e30
NORM_EPS = 1e-6

Q_WIDTH = N_HEADS * HEAD_DIM
KV_WIDTH = 2 * KV_HEADS * HEAD_DIM
NSA_MAIN = Q_WIDTH + 3 * KV_WIDTH
GDN_MAIN = GDN_CONV_DIM + GDN_V_HEADS * GDN_DV

V7X_VMEM_LIMIT_BYTES = 56 * 1024 * 1024
LANE = 128
SUBLANE = 8
ATTN_TILE = 256
CMP_TILE = 128
SCAN_TILE = 512
NT_DIMS = (((1,), (1,)), ((), ()))
TN_DIMS = (((0,), (0,)), ((), ()))


def _cparams(*sem):
    return pltpu.CompilerParams(dimension_semantics=sem, vmem_limit_bytes=V7X_VMEM_LIMIT_BYTES)


def _row_tile(m, target):
    t = min(m, target)
    while m % t:
        t //= 2
    return t


def _col_tile(n, target):
    t = min(n, target)
    while n % t or t % LANE:
        t -= LANE
    return t


def _split_bf16(x, parts):
    out = []
    for _ in range(parts - 1):
        hi = x.astype(jnp.bfloat16)
        out.append(hi)
        x = x - hi.astype(jnp.float32)
    out.append(x.astype(jnp.bfloat16))
    return out


def _bf(x):
    return x.astype(jnp.bfloat16)


def _dot(a, b, dims=None):
    if dims is None:
        return jnp.dot(a, b, preferred_element_type=jnp.float32)
    return lax.dot_general(a, b, dims, preferred_element_type=jnp.float32)


def _norm_matmul_body(x_ref, g_ref, w_ref, o_ref, h_ref):
    @pl.when(pl.program_id(1) == 0)
    def _():
        x = x_ref[...]
        ms = jnp.mean(x * x, axis=-1, keepdims=True)
        h_ref[...] = _bf(x * lax.rsqrt(ms + NORM_EPS) * g_ref[...])

    o_ref[...] = _dot(h_ref[...], w_ref[...])


def norm_matmul(x, g, w):
    m, k = x.shape
    n = w.shape[1]
    tm = _row_tile(m, 512)
    tn = _col_tile(n, 512)
    return pl.pallas_call(
        _norm_matmul_body,
        grid=(m // tm, n // tn),
        in_specs=[pl.BlockSpec((tm, k), lambda i, j: (i, 0)),
                  pl.BlockSpec((1, k), lambda i, j: (0, 0)),
                  pl.BlockSpec((k, tn), lambda i, j: (0, j))],
        out_specs=pl.BlockSpec((tm, tn), lambda i, j: (i, j)),
        out_shape=jax.ShapeDtypeStruct((m, n), jnp.float32),
        scratch_shapes=[pltpu.VMEM((tm, k), jnp.bfloat16)],
        compiler_params=_cparams("parallel", "arbitrary"),
        name="norm_matmul",
    )(x, g.reshape(1, k), w)


def _matmul_res_body(*refs):
    *a_refs, w_ref, r_ref, o_ref = refs
    a = a_refs[0][...]
    for a_ref in a_refs[1:]:
        a = a + a_ref[...]
    o_ref[...] = r_ref[...] + _dot(_bf(a), w_ref[...])


def matmul_res(a_list, w, res):
    m, k = a_list[0].shape
    n = w.shape[1]
    tm = _row_tile(m, 512)
    tn = _col_tile(n, 1024)
    return pl.pallas_call(
        _matmul_res_body,
        grid=(m // tm, n // tn),
        in_specs=[pl.BlockSpec((tm, k), lambda i, j: (i, 0)) for _ in a_list]
        + [pl.BlockSpec((k, tn), lambda i, j: (0, j)),
           pl.BlockSpec((tm, tn), lambda i, j: (i, j))],
        out_specs=pl.BlockSpec((tm, tn), lambda i, j: (i, j)),
        out_shape=jax.ShapeDtypeStruct((m, n), jnp.float32),
        compiler_params=_cparams("parallel", "arbitrary"),
        name="matmul_res",
    )(*a_list, w, res)


def _ffn_body(x_ref, g_ref, wa_ref, wb_ref, wd_ref, o_ref, h_ref, acc_ref):
    j = pl.program_id(1)

    @pl.when(j == 0)
    def _():
        x = x_ref[...]
        ms = jnp.mean(x * x, axis=-1, keepdims=True)
        h_ref[...] = _bf(x * lax.rsqrt(ms + NORM_EPS) * g_ref[...])
        acc_ref[...] = jnp.zeros_like(acc_ref)

    h = h_ref[...]
    a = _dot(h, wa_ref[...])
    b = _dot(h, wb_ref[...])
    acc_ref[...] += _dot(_bf(a * jax.nn.sigmoid(a) * b), wd_ref[...])

    @pl.when(j == pl.num_programs(1) - 1)
    def _():
        o_ref[...] = x_ref[...] + acc_ref[...]


def ffn(x, g, w_up, w_down):
    m, k = x.shape
    hdim = w_down.shape[0]
    tm = _row_tile(m, 512)
    th = _col_tile(hdim, 512)
    nh = hdim // th
    return pl.pallas_call(
        _ffn_body,
        grid=(m // tm, nh),
        in_specs=[pl.BlockSpec((tm, k), lambda i, j: (i, 0)),
                  pl.BlockSpec((1, k), lambda i, j: (0, 0)),
                  pl.BlockSpec((k, th), lambda i, j: (0, j)),
                  pl.BlockSpec((k, th), lambda i, j: (0, j + nh)),
                  pl.BlockSpec((th, k), lambda i, j: (j, 0))],
        out_specs=pl.BlockSpec((tm, k), lambda i, j: (i, 0)),
        out_shape=jax.ShapeDtypeStruct((m, k), jnp.float32),
        scratch_shapes=[pltpu.VMEM((tm, k), jnp.bfloat16), pltpu.VMEM((tm, k), jnp.float32)],
        compiler_params=_cparams("parallel", "arbitrary"),
        name="ffn",
    )(x, g.reshape(1, k), w_up, w_up, w_down)


def _norm_body(x_ref, g_ref, o_ref):
    x = x_ref[...]
    ms = jnp.mean(x * x, axis=-1, keepdims=True)
    o_ref[...] = x * lax.rsqrt(ms + NORM_EPS) * g_ref[...]


def final_norm(x, g):
    m, k = x.shape
    tm = _row_tile(m, 512)
    return pl.pallas_call(
        _norm_body,
        grid=(m // tm,),
        in_specs=[pl.BlockSpec((tm, k), lambda i: (i, 0)), pl.BlockSpec((1, k), lambda i: (0, 0))],
        out_specs=pl.BlockSpec((tm, k), lambda i: (i, 0)),
        out_shape=jax.ShapeDtypeStruct((m, k), jnp.float32),
        compiler_params=_cparams("parallel"),
        name="final_norm",
    )(x, g.reshape(1, k))


def _bucket_np(dist):
    exact = REL_BUCKETS // 2
    d = np.maximum(dist, 0)
    ratio = np.log(np.maximum(d, 1).astype(np.float32) / exact) / math.log(REL_MAX_DIST / exact)
    large = np.minimum(exact + (ratio * (REL_BUCKETS - exact)).astype(np.int32), REL_BUCKETS - 1)
    return np.where(d < exact, d, large)


def _heads_to_lanes(t):
    keys, queries, _ = t.shape
    return t.reshape(keys, queries, KV_HEADS, GROUP).transpose(2, 0, 3, 1).reshape(KV_HEADS, keys, GROUP * queries)


def _flash_bias_tiles(rel_table, window):
    j = np.arange(ATTN_TILE)[:, None]
    i = np.arange(ATTN_TILE)[None, :]
    tiles = []
    for d in range(3):
        dist = d * ATTN_TILE + i - j
        ok = dist >= 0
        if window:
            ok = ok & (dist < window)
        tiles.append(_heads_to_lanes(jnp.where(ok[..., None], rel_table[_bucket_np(dist)], NEG_INF)))
    return jnp.stack(tiles, axis=1)


def _cmp_bias_table(rel_table, ncp):
    x = np.arange(ncp)[:, None]
    i = np.arange(CMP_TILE)[None, :]
    dist = i - CMP_STRIDE * (x - 16) - (CMP_BLOCK - 1)
    far = rel_table[REL_BUCKETS - 1]
    b = _heads_to_lanes(jnp.where((dist >= 0)[..., None], rel_table[_bucket_np(dist)], far))
    return jnp.concatenate([b, b], axis=1)


def _stack_heads(q):
    return jnp.concatenate([q[:, r * HEAD_DIM:(r + 1) * HEAD_DIM] for r in range(GROUP)], axis=0)


def _gate_columns(gt_ref, col0):
    gt = jax.nn.sigmoid(gt_ref[...])
    lane = lax.broadcasted_iota(jnp.int32, gt.shape, 1)
    return [jnp.sum(jnp.where(lane == col0 + r, gt, 0.0), axis=1, keepdims=True) for r in range(GROUP)]


def _heads_from_lanes(o_t, rows, cols=None):
    parts = []
    for r in range(GROUP):
        part = o_t[:, r * rows:(r + 1) * rows].T
        if cols is not None:
            part = part * cols[r]
        parts.append(part)
    return jnp.concatenate(parts, axis=1)


def _rank_rows(score, n_rows):
    row = lax.broadcasted_iota(jnp.int32, score.shape, 0)
    rank = jnp.zeros(score.shape, jnp.int32)
    for mm in range(n_rows):
        sm = score[mm:mm + 1, :]
        ahead = (sm > score) | ((sm == score) & (row > mm))
        rank = rank + ahead.astype(jnp.int32)
    return rank


def _cmp_select_body(q_ref, kc_ref, dt_ref, gt_ref, o_ref, pen_ref, *, n_sel_blocks):
    g = pl.program_id(1)
    qi = pl.program_id(2)
    tq = CMP_TILE
    cols = GROUP * tq
    ncp = kc_ref.shape[1]
    q4 = _bf(_stack_heads(q_ref[...]) * ATTN_SCALE)
    s = _dot(_bf(kc_ref[0]), q4, NT_DIMS)
    shift = (qi * (tq // CMP_STRIDE) + ncp - 16) % ncp
    bias = dt_ref[pl.ds(pl.multiple_of(ncp - shift, SUBLANE), ncp), :]
    t_col = qi * tq + (lax.broadcasted_iota(jnp.int32, (ncp, cols), 1) & (tq - 1))
    end_pos = lax.broadcasted_iota(jnp.int32, (ncp, cols), 0) * CMP_STRIDE + (CMP_BLOCK - 1)
    mask = t_col >= end_pos
    s = jnp.where(mask, s + bias, NEG_INF)
    m = jnp.max(s, axis=0, keepdims=True)
    e = jnp.where(mask, jnp.exp(s - m), 0.0)
    p = e / jnp.maximum(jnp.sum(e, axis=0, keepdims=True), 1e-30)
    o_t = _dot(_bf(kc_ref[1].T), _bf(p))
    o_ref[...] = _heads_from_lanes(o_t, tq, _gate_columns(gt_ref, g * GROUP))

    imp = p[:, 0:tq]
    for r in range(1, GROUP):
        imp = imp + p[:, r * tq:(r + 1) * tq]
    ratio = SEL_BLOCK // CMP_STRIDE
    j_i = lax.broadcasted_iota(jnp.int32, (n_sel_blocks, ncp), 0)
    c_i = lax.broadcasted_iota(jnp.int32, (n_sel_blocks, ncp), 1)
    w = _bf((c_i >= ratio * j_i - 1) & (c_i <= ratio * j_i + ratio - 1))
    score = sum(_dot(w, part) for part in _split_bf16(imp, 3))
    blk = lax.broadcasted_iota(jnp.int32, (n_sel_blocks, tq), 0)
    tok = qi * tq + lax.broadcasted_iota(jnp.int32, (n_sel_blocks, tq), 1)
    cur = lax.shift_right_logical(tok, int(math.log2(SEL_BLOCK)))
    forced = (blk == 0) | (blk == cur) | (blk == cur - 1)
    causal = blk <= cur
    score = jnp.where(forced, FORCE_SCORE, score)
    score = jnp.where(causal, score, -1.0)
    chosen = (_rank_rows(score, n_sel_blocks) < N_SEL) & causal
    pen_ref[...] = jnp.where(chosen, 0.0, NEG_INF)


def cmp_select(proj, tail, kc, dt, batch, seq):
    tq = CMP_TILE
    nq = seq // tq
    ncp = kc.shape[3]
    nsb = seq // SEL_BLOCK
    return pl.pallas_call(
        functools.partial(_cmp_select_body, n_sel_blocks=nsb),
        grid=(batch, KV_HEADS, nq),
        in_specs=[pl.BlockSpec((tq, GROUP * HEAD_DIM), lambda b, g, i: (b * nq + i, g)),
                  pl.BlockSpec((None, None, 2, ncp, HEAD_DIM), lambda b, g, i: (b, g, 0, 0, 0)),
                  pl.BlockSpec((None, 2 * ncp, GROUP * tq), lambda b, g, i: (g, 0, 0)),
                  pl.BlockSpec((tq, LANE), lambda b, g, i: (b * nq + i, 0))],
        out_specs=[pl.BlockSpec((tq, GROUP * HEAD_DIM), lambda b, g, i: (b * nq + i, g)),
                   pl.BlockSpec((None, None, nsb, tq), lambda b, g, i: (b, g, 0, i))],
        out_shape=[jax.ShapeDtypeStruct((batch * seq, Q_WIDTH), jnp.float32),
                   jax.ShapeDtypeStruct((batch, KV_HEADS, nsb, seq), jnp.float32)],
        compiler_params=_cparams("parallel", "parallel", "arbitrary"),
        name="cmp_select",
    )(proj, kc, dt, tail)


def _flash_body(*refs, pen_block, pen_per_head, k_back, gate_col0, seq):
    it = iter(refs)
    q_ref, k_ref, v_ref, bt_ref = next(it), next(it), next(it), next(it)
    pen_ref = next(it) if pen_block else None
    gt_ref = next(it) if gate_col0 is not None else None
    o_ref, m_ref, l_ref, acc_ref, qa_ref, kb_ref, vt_ref = (next(it) for _ in range(7))
    g = pl.program_id(1)
    qi = pl.program_id(2)
    tq = tk = ATTN_TILE
    cols = GROUP * tq
    n_cls = bt_ref.shape[0]

    @pl.when(qi == 0)
    def _():
        for c in range(seq // tk):
            kb_ref[c * tk:(c + 1) * tk, :] = _bf(k_ref[c * tk:(c + 1) * tk, :])
            vt_ref[:, c * tk:(c + 1) * tk] = _bf(v_ref[c * tk:(c + 1) * tk, :].T)

    qa_ref[...] = _bf(_stack_heads(q_ref[...]) * ATTN_SCALE)
    m_ref[...] = jnp.full(m_ref.shape, NEG_INF, jnp.float32)
    l_ref[...] = jnp.zeros(l_ref.shape, jnp.float32)
    acc_ref[...] = jnp.zeros(acc_ref.shape, jnp.float32)

    def step(kj, carry):
        k0 = pl.multiple_of(kj * tk, tk)
        s = _dot(kb_ref[pl.ds(k0, tk), :], qa_ref[...], NT_DIMS)
        s = s + bt_ref[jnp.minimum(qi - kj, n_cls - 1)]
        if pen_block:
            per_tile = tk // pen_block
            pieces = []
            for a in range(per_tile):
                pen = pen_ref[pl.ds(kj * per_tile + a, 1), :]
                if not pen_per_head:
                    pen = jnp.concatenate([pen] * GROUP, axis=1)
                pieces.append(s[a * pen_block:(a + 1) * pen_block] + pen)
            s = pieces[0] if per_tile == 1 else jnp.concatenate(pieces, axis=0)
        m_prev = m_ref[...]
        m_new = jnp.maximum(m_prev, jnp.max(s, axis=0, keepdims=True))
        alpha = jnp.exp(m_prev - m_new)
        p = jnp.exp(s - m_new)
        l_ref[...] = alpha * l_ref[...] + jnp.sum(p, axis=0, keepdims=True)
        acc_ref[...] = alpha * acc_ref[...] + _dot(vt_ref[:, pl.ds(k0, tk)], _bf(p))
        m_ref[...] = m_new
        return carry

    k_lo = jnp.maximum(qi - k_back, 0) if k_back is not None else 0
    lax.fori_loop(k_lo, qi + 1, step, 0)
    o_t = acc_ref[...] / jnp.maximum(l_ref[...], 1e-30)
    cols_g = _gate_columns(gt_ref, gate_col0 + g * GROUP) if gate_col0 is not None else None
    o_ref[...] = _heads_from_lanes(o_t, tq, cols_g)


def flash_attention(q_arr, kv_arr, k_col, v_col, bias_tiles, batch, seq, *, pen=None, pen_block=0,
                    k_back=None, gate_arr=None, gate_col0=None):
    tq = ATTN_TILE
    nq = seq // tq
    cols = GROUP * tq
    in_specs = [pl.BlockSpec((tq, GROUP * HEAD_DIM), lambda b, g, i: (b * nq + i, g)),
                pl.BlockSpec((seq, HEAD_DIM), lambda b, g, i: (b, k_col + g)),
                pl.BlockSpec((seq, HEAD_DIM), lambda b, g, i: (b, v_col + g)),
                pl.BlockSpec((None,) + bias_tiles.shape[1:], lambda b, g, i: (g, 0, 0, 0))]
    args = [q_arr, kv_arr, kv_arr, bias_tiles]
    pen_per_head = False
    if pen is not None:
        if pen.ndim == 4:
            in_specs.append(pl.BlockSpec((None, None, pen.shape[2], tq), lambda b, g, i: (b, g, 0, i)))
        else:
            pen_per_head = True
            in_specs.append(pl.BlockSpec((None, None, None, pen.shape[3], cols), lambda b, g, i: (b, g, i, 0, 0)))
        args.append(pen)
    if gate_arr is not None:
        in_specs.append(pl.BlockSpec((tq, LANE), lambda b, g, i: (b * nq + i, 0)))
        args.append(gate_arr)
    return pl.pallas_call(
        functools.partial(_flash_body, pen_block=pen_block if pen is not None else 0, pen_per_head=pen_per_head,
                          k_back=k_back, gate_col0=gate_col0 if gate_arr is not None else None, seq=seq),
        grid=(batch, KV_HEADS, nq),
        in_specs=in_specs,
        out_specs=pl.BlockSpec((tq, GROUP * HEAD_DIM), lambda b, g, i: (b * nq + i, g)),
        out_shape=jax.ShapeDtypeStruct((batch * seq, Q_WIDTH), jnp.float32),
        scratch_shapes=[pltpu.VMEM((1, cols), jnp.float32), pltpu.VMEM((1, cols), jnp.float32),
                        pltpu.VMEM((HEAD_DIM, cols), jnp.float32), pltpu.VMEM((cols, HEAD_DIM), jnp.bfloat16),
                        pltpu.VMEM((seq, HEAD_DIM), jnp.bfloat16), pltpu.VMEM((HEAD_DIM, seq), jnp.bfloat16)],
        compiler_params=_cparams("parallel", "parallel", "arbitrary"),
        name="flash_attention",
    )(*args)


def _moba_gate_body(q_ref, k_ref, pen_ref, km_ref, *, n_blocks):
    qi = pl.program_id(2)
    tq = ATTN_TILE
    cols = GROUP * tq

    @pl.when(qi == 0)
    def _():
        k = k_ref[...]
        km_ref[...] = jnp.sum(k.reshape(n_blocks, MOBA_BLOCK, HEAD_DIM), axis=1) / MOBA_BLOCK

    qh, ql = _split_bf16(_stack_heads(q_ref[...]), 2)
    kh, kl = _split_bf16(km_ref[...], 2)
    gate = _dot(kh, qh, NT_DIMS) + _dot(kh, ql, NT_DIMS) + _dot(kl, qh, NT_DIMS)
    blk = lax.broadcasted_iota(jnp.int32, (n_blocks, cols), 0)
    tok = qi * tq + (lax.broadcasted_iota(jnp.int32, (n_blocks, cols), 1) & (tq - 1))
    own = lax.shift_right_logical(tok, int(math.log2(MOBA_BLOCK)))
    gate = jnp.where(blk < own, gate, NEG_INF)
    chosen = ((_rank_rows(gate, n_blocks) < MOBA_TOPK) & (blk < own)) | (blk == own)
    pen_ref[...] = jnp.where(chosen, 0.0, NEG_INF)


def moba_gate(proj, k_col, batch, seq):
    tq = ATTN_TILE
    nq = seq // tq
    cols = GROUP * tq
    nb = seq // MOBA_BLOCK
    return pl.pallas_call(
        functools.partial(_moba_gate_body, n_blocks=nb),
        grid=(batch, KV_HEADS, nq),
        in_specs=[pl.BlockSpec((tq, GROUP * HEAD_DIM), lambda b, g, i: (b * nq + i, g)),
                  pl.BlockSpec((seq, HEAD_DIM), lambda b, g, i: (b, k_col + g))],
        out_specs=pl.BlockSpec((None, None, None, nb, cols), lambda b, g, i: (b, g, i, 0, 0)),
        out_shape=jax.ShapeDtypeStruct((batch, KV_HEADS, nq, nb, cols), jnp.float32),
        scratch_shapes=[pltpu.VMEM((nb, HEAD_DIM), jnp.float32)],
        compiler_params=_cparams("parallel", "parallel", "arbitrary"),
        name="moba_gate",
    )(proj, proj)


def _tril_ones(n, strict=False):
    r = lax.broadcasted_iota(jnp.int32, (n, n), 0)
    c = lax.broadcasted_iota(jnp.int32, (n, n), 1)
    return (r > c) if strict else (r >= c)


def _chunk_cumsum(x):
    tril = _bf(_tril_ones(x.shape[0]))
    return sum(_dot(tril, part) for part in _split_bf16(x, 3))


def _head_rms_gate(o, norm_g, gate):
    ms = jnp.mean(o * o, axis=-1, keepdims=True)
    return o * lax.rsqrt(ms + NORM_EPS) * norm_g * (gate * jax.nn.sigmoid(gate))


def _hgrn2_body(q_ref, f_ref, i_ref, g_ref, lbl_ref, ng_ref, s0_ref, o_ref, s_out_ref, st_ref, *, layer):
    ti = pl.program_id(2)

    @pl.when(ti == 0)
    def _():
        st_ref[...] = s0_ref[...].T

    lbl = lbl_ref[...]
    e = jnp.exp(lbl - jnp.max(lbl, axis=0, keepdims=True))
    p = e / jnp.sum(e, axis=0, keepdims=True)
    lb = jnp.zeros((1, HG_DK), jnp.float32)
    for r in range(1, layer + 1):
        lb = lb + p[r:r + 1]
    causal = _tril_ones(CHUNK)
    work = []
    for c in range(q_ref.shape[0] // CHUNK):
        sl = slice(c * CHUNK, (c + 1) * CHUNK)
        q = q_ref[sl, :]
        qh = q * jax.nn.sigmoid(q) * HG_DK ** -0.5
        fg = lb + (1.0 - lb) * jax.nn.sigmoid(f_ref[sl, :])
        k = 1.0 - fg
        v = _bf(i_ref[sl, :])
        b = _chunk_cumsum(jnp.log(fg))
        b_mid = b[CHUNK // 2:CHUNK // 2 + 1]
        b_last = b[CHUNK - 1:CHUNK]
        a = _dot(_bf(qh * jnp.exp(b - b_mid)), _bf(k * jnp.exp(b_mid - b)), NT_DIMS)
        a = jnp.where(causal, a, 0.0)
        work.append((sl, _dot(_bf(a), v), _bf(qh * jnp.exp(b)), jnp.exp(b_last),
                     _dot(v, _bf(k * jnp.exp(b_last - b)), TN_DIMS)))
    for sl, o_intra, q_in, d_last, kv in work:
        st = st_ref[...]
        o = o_intra + _dot(q_in, _bf(st), NT_DIMS)
        st_ref[...] = st * d_last + kv
        o_ref[sl, :] = _head_rms_gate(o, ng_ref[...], g_ref[sl, :])

    @pl.when(ti == pl.num_programs(2) - 1)
    def _():
        s_out_ref[...] = st_ref[...].T


def hgrn2_scan(proj, lb_logits, norm_g, s0, layer, batch, seq):
    tt = _row_tile(seq, SCAN_TILE)
    nt = seq // tt
    h = HG_HEADS

    def col(k):
        return pl.BlockSpec((tt, HG_DK), lambda b, hh, t: (b * nt + t, k * h + hh))

    return pl.pallas_call(
        functools.partial(_hgrn2_body, layer=layer),
        grid=(batch, h, nt),
        in_specs=[col(0), col(1), col(2), col(3),
                  pl.BlockSpec((DEPTH, HG_DK), lambda b, hh, t: (0, hh)),
                  pl.BlockSpec((1, HG_DV), lambda b, hh, t: (0, 0)),
                  pl.BlockSpec((None, None, HG_DK, HG_DV), lambda b, hh, t: (b, hh, 0, 0))],
        out_specs=[pl.BlockSpec((tt, HG_DV), lambda b, hh, t: (b * nt + t, hh)),
                   pl.BlockSpec((None, None, HG_DK, HG_DV), lambda b, hh, t: (b, hh, 0, 0))],
        out_shape=[jax.ShapeDtypeStruct((batch * seq, h * HG_DV), jnp.float32),
                   jax.ShapeDtypeStruct((batch, h, HG_DK, HG_DV), jnp.float32)],
        scratch_shapes=[pltpu.VMEM((HG_DV, HG_DK), jnp.float32)],
        compiler_params=_cparams("parallel", "parallel", "arbitrary"),
        name="hgrn2_scan",
    )(proj, proj, proj, proj, lb_logits, norm_g.reshape(1, HG_DV), s0)


def _lane_column(x, lane_idx):
    lane = lax.broadcasted_iota(jnp.int32, x.shape, 1)
    return jnp.sum(jnp.where(lane == lane_idx, x, 0.0), axis=1, keepdims=True)


def _softplus(x):
    return jnp.maximum(x, 0.0) + jnp.log(1.0 + jnp.exp(-jnp.abs(x)))


def _l2n(x):
    return x * lax.rsqrt(jnp.sum(x * x, axis=-1, keepdims=True) + NORM_EPS)


def _gdn_body(q_ref, k_ref, v_ref, z_ref, t_ref, bq_ref, bk_ref, bv_ref, wq_ref, wk_ref, wv_ref,
              al_ref, dtb_ref, ng_ref, s0_ref, o_ref, s_out_ref, xs_ref, y_ref, s_ref):
    hq = pl.program_id(1)
    ti = pl.program_id(2)
    tt = q_ref.shape[0]
    dk, dv = GDN_DK, GDN_DV
    pad = SUBLANE

    @pl.when(ti == 0)
    def _():
        s_ref[...] = s0_ref[...]
        xs_ref[0:pad, :] = jnp.concatenate([bq_ref[...], bk_ref[...], bv_ref[...]], axis=1)

    x = jnp.concatenate([q_ref[...], k_ref[...], v_ref[...]], axis=1)
    xs_ref[pad:, :] = x
    cw = jnp.concatenate([wq_ref[...], wk_ref[...], wv_ref[...]], axis=1)
    y = xs_ref[pad - 3:pad - 3 + tt, :] * cw[0:1]
    for i in range(1, GDN_CONV - 1):
        y = y + xs_ref[pad - 3 + i:pad - 3 + i + tt, :] * cw[i:i + 1]
    y = y + x * cw[GDN_CONV - 1:GDN_CONV]
    xs_ref[0:pad, :] = x[tt - pad:tt]
    y_ref[...] = y * jax.nn.sigmoid(y)

    strict = _tril_ones(CHUNK, strict=True)
    incl = _tril_ones(CHUNK)
    sel_rows = lax.shift_right_logical(lax.broadcasted_iota(jnp.int32, (GDN_REP * CHUNK, LANE), 0),
                                       int(math.log2(CHUNK)))
    sel_lane = lax.broadcasted_iota(jnp.int32, (GDN_REP * CHUNK, LANE), 1)
    pick = _bf(sel_lane == GDN_V_HEADS + hq * GDN_REP + sel_rows)
    n_chunks = tt // CHUNK

    work = []
    for c in range(n_chunks):
        sl = slice(c * CHUNK, (c + 1) * CHUNK)
        yc = y_ref[sl, :]
        q = _l2n(yc[:, 0:dk]) * dk ** -0.5
        k = _l2n(yc[:, dk:2 * dk])
        qb, kb = _bf(q), _bf(k)
        kk = _dot(kb, kb, NT_DIMS)
        qk = _dot(qb, kb, NT_DIMS)
        tl = t_ref[sl, :]
        beta_all = jax.nn.sigmoid(tl)
        g_all = _chunk_cumsum(-jnp.exp(al_ref[...]) * _softplus(tl + dtb_ref[...]))
        g_rows = sum(_dot(pick, part, NT_DIMS) for part in _split_bf16(g_all, 3))
        for e in range(GDN_REP):
            hv = hq * GDN_REP + e
            v = yc[:, 2 * dk + e * dv:2 * dk + (e + 1) * dv]
            bt = _lane_column(beta_all, hv)
            gc = _lane_column(g_all, GDN_V_HEADS + hv)
            gdiff = gc - g_rows[e * CHUNK:(e + 1) * CHUNK]
            d_strict = jnp.where(strict, jnp.exp(jnp.where(strict, gdiff, 0.0)), 0.0)
            d_incl = jnp.where(incl, jnp.exp(jnp.where(incl, gdiff, 0.0)), 0.0)
            eg = jnp.exp(gc)
            g_last = gc[CHUNK - 1:CHUNK]
            work.append(dict(
                c=c, e=e, sol=jnp.concatenate([bt * v, (bt * eg) * k], axis=1), pw=bt * kk * d_strict,
                aq=_bf(qk * d_incl), q_in=_bf(q * eg), k_out=_bf(k * jnp.exp(g_last - gc)), d_last=jnp.exp(g_last)))

    r_i = lax.broadcasted_iota(jnp.int32, (CHUNK, CHUNK), 0)
    c_i = lax.broadcasted_iota(jnp.int32, (CHUNK, CHUNK), 1)
    same = [lax.shift_right_logical(r_i, sh) == lax.shift_right_logical(c_i, sh) for sh in range(3, 7)]
    eye = (r_i == c_i).astype(jnp.float32)
    for wk in work:
        l8 = jnp.where(same[0], wk["pw"], 0.0)
        l8b = _bf(l8)
        wk["t"] = eye - l8
        wk["p"] = _dot(l8b, l8b)
    for wk in work:
        pb = _bf(wk["p"])
        wk["t"] = wk["t"] + _dot(_bf(wk["t"]), pb)
        wk["p"] = _dot(pb, pb)
    for wk in work:
        wk["t"] = wk["t"] + _dot(_bf(wk["t"]), _bf(wk["p"]))
    for lvl in range(1, len(same)):
        for wk in work:
            tb = _bf(wk["t"])
            off = _bf(jnp.where(same[lvl] & jnp.logical_not(same[lvl - 1]), wk["pw"], 0.0))
            wk["t"] = wk["t"] - _dot(tb, _bf(_dot(off, tb)))
    for wk in work:
        wk["sol"] = _dot(_bf(wk["t"]), _bf(wk["sol"]))

    for wk in work:
        c, e = wk["c"], wk["e"]
        sl = slice(c * CHUNK, (c + 1) * CHUNK)
        u0, w = wk["sol"][:, :dv], wk["sol"][:, dv:]
        s = s_ref[e]
        sb = _bf(s)
        u = u0 - _dot(_bf(w), sb)
        o = _dot(wk["q_in"], sb) + _dot(wk["aq"], _bf(u))
        s_ref[e] = wk["d_last"] * s + _dot(wk["k_out"], _bf(u), TN_DIMS)
        o_ref[sl, e * dv:(e + 1) * dv] = _head_rms_gate(o, ng_ref[...], z_ref[sl, e * dv:(e + 1) * dv])

    @pl.when(ti == pl.num_programs(2) - 1)
    def _():
        s_out_ref[...] = s_ref[...]


def gdn_scan(main, tail, conv_buf, conv_w, a_log, dt_bias, norm_g, s0, batch, seq):
    tt = _row_tile(seq, SCAN_TILE)
    nt = seq // tt
    hq, rep, dk, dv = GDN_QK_HEADS, GDN_REP, GDN_DK, GDN_DV
    vw = rep * dv
    buf = jnp.pad(conv_buf, ((0, 0), (SUBLANE - (GDN_CONV - 1), 0), (0, 0)))
    pad_lanes = jnp.zeros((LANE - 2 * GDN_V_HEADS,), jnp.float32)
    a_row = jnp.concatenate([jnp.zeros((GDN_V_HEADS,), jnp.float32), a_log, pad_lanes]).reshape(1, LANE)
    dt_row = jnp.concatenate([jnp.zeros((GDN_V_HEADS,), jnp.float32), dt_bias, pad_lanes]).reshape(1, LANE)
    k0 = hq
    v0 = 2 * hq * dk // vw
    z0 = GDN_CONV_DIM // vw
    row = lambda b, h, t: b * nt + t
    return pl.pallas_call(
        _gdn_body,
        grid=(batch, hq, nt),
        in_specs=[pl.BlockSpec((tt, dk), lambda b, h, t: (row(b, h, t), h)),
                  pl.BlockSpec((tt, dk), lambda b, h, t: (row(b, h, t), k0 + h)),
                  pl.BlockSpec((tt, vw), lambda b, h, t: (row(b, h, t), v0 + h)),
                  pl.BlockSpec((tt, vw), lambda b, h, t: (row(b, h, t), z0 + h)),
                  pl.BlockSpec((tt, LANE), lambda b, h, t: (row(b, h, t), 0)),
                  pl.BlockSpec((None, SUBLANE, dk), lambda b, h, t: (b, 0, h)),
                  pl.BlockSpec((None, SUBLANE, dk), lambda b, h, t: (b, 0, k0 + h)),
                  pl.BlockSpec((None, SUBLANE, vw), lambda b, h, t: (b, 0, v0 + h)),
                  pl.BlockSpec((GDN_CONV, dk), lambda b, h, t: (0, h)),
                  pl.BlockSpec((GDN_CONV, dk), lambda b, h, t: (0, k0 + h)),
                  pl.BlockSpec((GDN_CONV, vw), lambda b, h, t: (0, v0 + h)),
                  pl.BlockSpec((1, LANE), lambda b, h, t: (0, 0)),
                  pl.BlockSpec((1, LANE), lambda b, h, t: (0, 0)),
                  pl.BlockSpec((1, dv), lambda b, h, t: (0, 0)),
                  pl.BlockSpec((None, rep, dk, dv), lambda b, h, t: (b, h, 0, 0))],
        out_specs=[pl.BlockSpec((tt, vw), lambda b, h, t: (row(b, h, t), h)),
                   pl.BlockSpec((None, rep, dk, dv), lambda b, h, t: (b, h, 0, 0))],
        out_shape=[jax.ShapeDtypeStruct((batch * seq, GDN_V_HEADS * dv), jnp.float32),
                   jax.ShapeDtypeStruct((batch, GDN_V_HEADS, dk, dv), jnp.float32)],
        scratch_shapes=[pltpu.VMEM((tt + SUBLANE, 2 * dk + vw), jnp.float32),
                        pltpu.VMEM((tt, 2 * dk + vw), jnp.float32),
                        pltpu.VMEM((rep, dk, dv), jnp.float32)],
        compiler_params=_cparams("parallel", "parallel", "arbitrary"),
        name="gdn_scan",
    )(main, main, main, main, tail, buf, buf, buf, conv_w, conv_w, conv_w, a_row, dt_row,
      norm_g.reshape(1, dv), s0)


def _rms_norm(x, g):
    xf = x.astype(jnp.float32)
    y = xf * lax.rsqrt(jnp.mean(xf * xf, axis=-1, keepdims=True) + NORM_EPS)
    return (y * g.astype(jnp.float32)).astype(x.dtype)


def _l2norm(x):
    return x * lax.rsqrt(jnp.sum(x * x, axis=-1, keepdims=True) + NORM_EPS)


def _masked_softmax(logits, mask):
    logits = jnp.where(mask, logits, NEG_INF)
    m = jnp.max(logits, axis=-1, keepdims=True)
    e = jnp.where(mask, jnp.exp(logits - m), 0.0)
    return e / jnp.maximum(jnp.sum(e, axis=-1, keepdims=True), 1e-30)


def _t5_bucket(dist):
    exact = REL_BUCKETS // 2
    d = jnp.maximum(dist, 0)
    ratio = jnp.log(jnp.maximum(d, 1).astype(jnp.float32) / exact) / math.log(REL_MAX_DIST / exact)
    large = jnp.minimum(exact + (ratio * (REL_BUCKETS - exact)).astype(jnp.int32), REL_BUCKETS - 1)
    return jnp.where(d < exact, d, large)


def _rel_bias(rel_table, dist):
    return rel_table.astype(jnp.float32)[_t5_bucket(dist)]


def _gather_pages(pool, page_table):
    rows = pool[page_table]
    return rows.reshape(page_table.shape[0], page_table.shape[1] * pool.shape[1], *pool.shape[2:])


def _causal_conv(x, buf, w):
    T = x.shape[1]
    xp = jnp.concatenate([buf.astype(x.dtype), x], axis=1)
    y = xp[:, 0:T] * w[0]
    for i in range(1, GDN_CONV):
        y = y + xp[:, i:i + T] * w[i]
    return jax.nn.silu(y), xp[:, T:]


def _nsa_split(proj):
    B, T, _ = proj.shape
    q = proj[..., :Q_WIDTH].reshape(B, T, KV_HEADS, GROUP, HEAD_DIM)
    kv_c, kv_s, kv_w = (proj[..., Q_WIDTH + c * KV_WIDTH:Q_WIDTH + (c + 1) * KV_WIDTH]
                        .reshape(B, T, 2, KV_HEADS, HEAD_DIM) for c in range(3))
    gates = jax.nn.sigmoid(proj[..., NSA_MAIN:NSA_MAIN + 3 * N_HEADS]).reshape(B, T, 3, KV_HEADS, GROUP)
    return q, kv_c, kv_s, kv_w, gates


def _nsa_compress(kv, cmp_w1, cmp_w2, cmp_pe):
    B, Tk = kv.shape[:2]
    nc = (Tk - CMP_BLOCK) // CMP_STRIDE + 1
    n_part = CMP_BLOCK // CMP_STRIDE
    n_chunk = nc + n_part - 1
    chunks = kv[:, :n_chunk * CMP_STRIDE].astype(jnp.float32).reshape(B, n_chunk, CMP_STRIDE, 2, KV_HEADS, HEAD_DIM)
    w1 = cmp_w1.astype(jnp.float32)
    part = jnp.einsum('bnlcgd,crldh->bncrgh', chunks, w1.reshape(2, n_part, CMP_STRIDE, HEAD_DIM, CMP_HIDDEN))
    hid = jnp.einsum('cld,cldh->ch', cmp_pe.astype(jnp.float32), w1)[None, None, :, None, :]
    for r in range(n_part):
        hid = hid + part[:, r:r + nc, :, r]
    return jnp.einsum('bncgh,chd->bncgd', jax.nn.gelu(hid), cmp_w2.astype(jnp.float32))


def _nsa_cmp_attn(q, q_pos, kc, rel_table):
    nc = kc.shape[1]
    end_pos = jnp.arange(nc) * CMP_STRIDE + (CMP_BLOCK - 1)
    dist = q_pos[:, None] - end_pos[None, :]
    bias = _rel_bias(rel_table, dist).reshape(q_pos.shape[0], nc, KV_HEADS, GROUP).transpose(0, 2, 3, 1)
    logits = jnp.einsum('bqgrd,bcgd->bqgrc', q, kc[:, :, 0]) * ATTN_SCALE + bias
    p = _masked_softmax(logits, (dist >= 0)[:, None, None, :])
    return jnp.einsum('bqgrc,bcgd->bqgrd', p, kc[:, :, 1]), p


def _nsa_select(p_cmp, q_pos, tk):
    ns = -(-tk // SEL_BLOCK)
    ratio = SEL_BLOCK // CMP_STRIDE
    imp = p_cmp.sum(axis=3)
    nc = imp.shape[-1]
    imp = jnp.pad(imp, ((0, 0), (0, 0), (0, 0), (1, ratio * ns + ratio - 1 - nc)))
    score = imp[..., :ratio * ns].reshape(*imp.shape[:3], ns, ratio).sum(-1) + imp[..., ratio::ratio]
    cur = q_pos // SEL_BLOCK
    blk = jnp.arange(ns)
    forced = (blk[None, :] == 0) | (blk[None, :] == cur[:, None]) | (blk[None, :] == cur[:, None] - 1)
    causal = blk[None, :] <= cur[:, None]
    score = jnp.where(forced[None, :, None, :], FORCE_SCORE, score)
    score = jnp.where(causal[None, :, None, :], score, -1.0)
    _, idx = lax.top_k(score, min(N_SEL, ns))
    valid = idx <= cur[None, :, None, None]
    return idx, valid


def _block_mask(idx, valid, n_blocks, block, tk):
    hit = (idx[..., None] == jnp.arange(n_blocks)) & valid[..., None]
    return jnp.repeat(jnp.any(hit, axis=-2), block, axis=-1)[..., :tk]


def _dense_attn(q, q_pos, kv, key_ok, rel_table):
    tk = kv.shape[1]
    dist = q_pos[:, None] - jnp.arange(tk)[None, :]
    bias = _rel_bias(rel_table, dist).reshape(q_pos.shape[0], tk, KV_HEADS, GROUP).transpose(0, 2, 3, 1)
    logits = jnp.einsum('bqgrd,bkgd->bqgrk', q, kv[:, :, 0]) * ATTN_SCALE + bias
    p = _masked_softmax(logits, key_ok & (dist >= 0)[None, :, None, None, :])
    return jnp.einsum('bqgrk,bkgd->bqgrd', p, kv[:, :, 1])


def _band_attn(q, q_pos, kv_band, k_pos, rel_table):
    dist = q_pos[:, :, None] - k_pos[:, None, :]
    mask = (dist >= 0) & (dist < WINDOW) & (k_pos[:, None, :] >= 0)
    n, qb, kb = dist.shape
    bias = _rel_bias(rel_table, dist).reshape(n, qb, kb, KV_HEADS, GROUP).transpose(0, 1, 3, 4, 2)
    kvf = kv_band.astype(jnp.float32)
    logits = jnp.einsum('bnqgrd,bnkgd->bnqgrk', q, kvf[:, :, :, 0]) * ATTN_SCALE + bias
    p = _masked_softmax(logits, mask[:, :, None, None, :])
    return jnp.einsum('bnqgrk,bnkgd->bnqgrd', p, kvf[:, :, :, 1])


def _nsa_sample(proj, cache_c, cache_s, cache_w, page_table, cmp_w1, cmp_w2, cmp_pe, rel_table):
    B, T, _ = proj.shape
    past = page_table.shape[1] * cache_c.shape[1]
    q, kv_c, kv_s, kv_w, gates = _nsa_split(proj)
    pos = past + jnp.arange(T)
    kv_c_full = jnp.concatenate([_gather_pages(cache_c, page_table), kv_c], axis=1)
    kv_s_full = jnp.concatenate([_gather_pages(cache_s, page_table), kv_s], axis=1)
    wbuf = cache_w.shape[1]
    band = jnp.concatenate([cache_w, kv_w], axis=1)
    band_k_pos = past - wbuf + jnp.arange(wbuf + T)
    kc = _nsa_compress(kv_c_full, cmp_w1, cmp_w2, cmp_pe)
    o_cmp, p_cmp = _nsa_cmp_attn(q, pos, kc, rel_table)
    tk = kv_s_full.shape[1]
    idx, valid = _nsa_select(p_cmp, pos, tk)
    key_ok = _block_mask(idx, valid, -(-tk // SEL_BLOCK), SEL_BLOCK, tk)[:, :, :, None, :]
    o_sel = _dense_attn(q, pos, kv_s_full, key_ok, rel_table)
    o_win = _band_attn(q[:, None], pos[None, :], band[:, None], band_k_pos[None, :], rel_table)[:, 0]
    o = gates[:, :, 0, ..., None] * o_cmp + gates[:, :, 1, ..., None] * o_sel + gates[:, :, 2, ..., None] * o_win
    return o.reshape(B, T, Q_WIDTH), kv_c, kv_s, band[:, -wbuf:]


def _moba_sample(proj, cache_kv, page_table, rel_table):
    B, T, _ = proj.shape
    past = page_table.shape[1] * cache_kv.shape[1]
    q = proj[..., :Q_WIDTH].reshape(B, T, KV_HEADS, GROUP, HEAD_DIM)
    kv_new = proj[..., Q_WIDTH:].reshape(B, T, 2, KV_HEADS, HEAD_DIM)
    kv = jnp.concatenate([_gather_pages(cache_kv, page_table), kv_new], axis=1)
    q_pos = past + jnp.arange(T)
    tk = kv.shape[1]
    nb = -(-tk // MOBA_BLOCK)
    kpad = jnp.pad(kv[:, :, 0], ((0, 0), (0, nb * MOBA_BLOCK - tk), (0, 0), (0, 0)))
    kmean = jnp.mean(kpad.reshape(B, nb, MOBA_BLOCK, KV_HEADS, HEAD_DIM), axis=2)
    own = q_pos // MOBA_BLOCK
    gate = jnp.einsum('bqgrd,bngd->bqgrn', q, kmean)
    is_past = jnp.arange(nb)[None, :] < own[:, None]
    gate = jnp.where(is_past[:, None, None, :], gate, NEG_INF)
    _, idx = lax.top_k(gate, min(MOBA_TOPK, nb))
    valid = idx < own[:, None, None, None]
    own_ok = (jnp.arange(tk)[None, :] // MOBA_BLOCK) == own[:, None]
    key_ok = _block_mask(idx, valid, nb, MOBA_BLOCK, tk) | own_ok[None, :, None, None, :]
    o = _dense_attn(q, q_pos, kv, key_ok, rel_table)
    return o.reshape(B, T, Q_WIDTH), kv_new


def _gla_chunked(q, k, v, logf, s0):
    B, T, H, dk = q.shape
    dv = v.shape[-1]
    C = math.gcd(T, CHUNK)
    N = T // C
    q, k, v, logf = (a.reshape(B, N, C, *a.shape[2:]) for a in (q, k, v, logf))
    b = jnp.cumsum(logf, axis=2)
    b_ref = b[:, :, C // 2:C // 2 + 1]
    a = jnp.einsum('bnihd,bnjhd->bnhij', q * jnp.exp(b - b_ref), k * jnp.exp(b_ref - b))
    causal = jnp.arange(C)[:, None] >= jnp.arange(C)[None, :]
    a = jnp.where(causal, a, 0.0)
    o_intra = jnp.einsum('bnhij,bnjhv->bnihv', a, v)
    q_in = q * jnp.exp(b)
    k_out = k * jnp.exp(b[:, :, -1:] - b)
    d_last = jnp.exp(b[:, :, -1])

    def step(s, xs):
        q_c, k_c, v_c, d_c = xs
        o = jnp.einsum('bihd,bhdv->bihv', q_c, s)
        s = d_c[..., None] * s + jnp.einsum('bjhd,bjhv->bhdv', k_c, v_c)
        return s, o

    s, o_inter = lax.scan(step, s0, tuple(jnp.moveaxis(t, 1, 0) for t in (q_in, k_out, v, d_last)))
    o = o_intra + jnp.moveaxis(o_inter, 0, 1)
    return o.reshape(B, T, H, dv), s


def _hgrn2_core(proj, s0, lb_logits, layer, norm_g):
    B, T, _ = proj.shape
    dk = HG_HEADS * HG_DK
    dv = HG_HEADS * HG_DV
    q, f, i, g = jnp.split(proj, [dk, 2 * dk, 2 * dk + dv], axis=-1)
    p = jax.nn.softmax(lb_logits.astype(jnp.float32), axis=0)
    lb = (jnp.cumsum(p, axis=0) - p[0])[layer]
    fg = lb + (1.0 - lb) * jax.nn.sigmoid(f)
    shp = (B, T, HG_HEADS, HG_DK)
    o, s = _gla_chunked((jax.nn.silu(q) * HG_DK ** -0.5).reshape(shp), (1.0 - fg).reshape(shp),
                        i.reshape(B, T, HG_HEADS, HG_DV), jnp.log(fg).reshape(shp), s0.astype(jnp.float32))
    o = _rms_norm(o, norm_g) * jax.nn.silu(g.reshape(B, T, HG_HEADS, HG_DV))
    return o.reshape(B, T, dv), s


def _gdn_chunked(q, k, v, log_a, beta, s0):
    B, T, H, dk = q.shape
    dv = v.shape[-1]
    C = math.gcd(T, CHUNK)
    N = T // C

    def heads_first(a):
        return jnp.moveaxis(a.reshape(B, N, C, *a.shape[2:]), 3, 2)

    qh, kh, vh, bt = heads_first(q), heads_first(k), heads_first(v), heads_first(beta)
    g = jnp.cumsum(heads_first(log_a), axis=-1)
    ar = jnp.arange(C)
    strict = ar[:, None] > ar[None, :]
    incl = ar[:, None] >= ar[None, :]
    gdiff = g[..., :, None] - g[..., None, :]
    d_strict = jnp.where(strict, jnp.exp(jnp.where(strict, gdiff, 0.0)), 0.0)
    d_incl = jnp.where(incl, jnp.exp(jnp.where(incl, gdiff, 0.0)), 0.0)
    kk = jnp.einsum('bnhid,bnhjd->bnhij', kh, kh)
    m = jnp.eye(C, dtype=jnp.float32) + bt[..., :, None] * kk * d_strict
    rhs = jnp.concatenate([bt[..., None] * vh, (bt * jnp.exp(g))[..., None] * kh], axis=-1)
    sol = lax.linalg.triangular_solve(m, rhs, left_side=True, lower=True, unit_diagonal=True)
    u0, w = sol[..., :dv], sol[..., dv:]
    aq = jnp.einsum('bnhid,bnhjd->bnhij', qh, kh) * d_incl
    q_in = qh * jnp.exp(g)[..., None]
    k_out = kh * jnp.exp(g[..., -1:] - g)[..., None]
    d_last = jnp.exp(g[..., -1])

    def step(s, xs):
        u0_c, w_c, aq_c, q_c, k_c, d_c = xs
        u = u0_c - jnp.einsum('bhcd,bhdv->bhcv', w_c, s)
        o = jnp.einsum('bhcd,bhdv->bhcv', q_c, s) + jnp.einsum('bhij,bhjv->bhiv', aq_c, u)
        s = d_c[..., None, None] * s + jnp.einsum('bhcd,bhcv->bhdv', k_c, u)
        return s, o

    xs = tuple(jnp.moveaxis(a, 1, 0) for a in (u0, w, aq, q_in, k_out, d_last))
    s, o = lax.scan(step, s0, xs)
    return jnp.transpose(o, (1, 0, 3, 2, 4)).reshape(B, T, H, dv), s


def _gdn_core(proj, conv_buf, s0, conv_w, a_log, dt_bias, norm_g):
    B, T, _ = proj.shape
    vw = GDN_V_HEADS * GDN_DV
    qkv = proj[..., :GDN_CONV_DIM]
    z = proj[..., GDN_CONV_DIM:GDN_MAIN]
    b_logit = proj[..., GDN_MAIN:GDN_MAIN + GDN_V_HEADS]
    a_in = proj[..., GDN_MAIN + GDN_V_HEADS:GDN_MAIN + 2 * GDN_V_HEADS]
    conv_out, new_buf = _causal_conv(qkv, conv_buf, conv_w)
    qkw = GDN_QK_HEADS * GDN_DK
    q = jnp.repeat(_l2norm(conv_out[..., :qkw].reshape(B, T, GDN_QK_HEADS, GDN_DK)) * GDN_DK ** -0.5, GDN_REP, axis=2)
    k = jnp.repeat(_l2norm(conv_out[..., qkw:2 * qkw].reshape(B, T, GDN_QK_HEADS, GDN_DK)), GDN_REP, axis=2)
    v = conv_out[..., 2 * qkw:].reshape(B, T, GDN_V_HEADS, GDN_DV)
    beta = jax.nn.sigmoid(b_logit)
    log_a = -jnp.exp(a_log.astype(jnp.float32)) * jax.nn.softplus(a_in + dt_bias.astype(jnp.float32))
    o, s = _gdn_chunked(q, k, v, log_a, beta, s0.astype(jnp.float32))
    o = _rms_norm(o, norm_g) * jax.nn.silu(z.reshape(B, T, GDN_V_HEADS, GDN_DV))
    return o.reshape(B, T, vw), new_buf, s


def _pad_cols(w, mult=LANE):
    n = w.shape[1]
    return jnp.pad(w, ((0, 0), (0, (-n) % mult)))


def _nsa_prompt(main, tail, batch, seq, cmp_w1, cmp_w2, cmp_pe, rel_table):
    kv = main[:, Q_WIDTH:].reshape(batch, seq, 3, 2, KV_HEADS, HEAD_DIM)
    kv_c, kv_s, kv_w = kv[:, :, 0], kv[:, :, 1], kv[:, :, 2]
    kc = _nsa_compress(kv_c, cmp_w1, cmp_w2, cmp_pe)
    ncp = -(-kc.shape[1] // (2 * LANE)) * (2 * LANE)
    kc = jnp.pad(kc, ((0, 0), (0, ncp - kc.shape[1]), (0, 0), (0, 0), (0, 0))).transpose(0, 3, 2, 1, 4)
    o_cmp, pen = cmp_select(main, tail, kc, _cmp_bias_table(rel_table, ncp), batch, seq)
    col = Q_WIDTH // HEAD_DIM
    o_sel = flash_attention(main, main, col + 2 * KV_HEADS, col + 3 * KV_HEADS, _flash_bias_tiles(rel_table, 0),
                            batch, seq, pen=pen, pen_block=SEL_BLOCK, gate_arr=tail, gate_col0=N_HEADS)
    o_win = flash_attention(main, main, col + 4 * KV_HEADS, col + 5 * KV_HEADS,
                            _flash_bias_tiles(rel_table, WINDOW), batch, seq, k_back=WINDOW // ATTN_TILE,
                            gate_arr=tail, gate_col0=2 * N_HEADS)
    return [o_cmp, o_sel, o_win], kv_c, kv_s, kv_w[:, -min(WINDOW, seq):]


def _moba_prompt(proj, batch, seq, rel_table):
    col = Q_WIDTH // HEAD_DIM
    pen = moba_gate(proj, col, batch, seq)
    o = flash_attention(proj, proj, col, col + KV_HEADS, _flash_bias_tiles(rel_table, 0), batch, seq,
                        pen=pen, pen_block=MOBA_BLOCK)
    return o, proj[:, Q_WIDTH:].reshape(batch, seq, 2, KV_HEADS, HEAD_DIM)


def kernel(x_prompt, x_sample, cache_nsa_cmp_kv, cache_nsa_sel_kv, cache_nsa_win_kv, cache_moba_kv,
           state_hgrn2, state_gdn_conv, state_gdn_ssm, page_table, rel_table, ln_mix, ln_ffn, ln_final,
           ffn_w_up, ffn_w_down, nsa_w_in, nsa_cmp_w1, nsa_cmp_w2, nsa_cmp_pe, nsa_w_out, moba_w_in, moba_w_out,
           hg_w_in, hg_lb_logits, hg_norm, hg_w_out, gdn_w_in, gdn_conv_w, gdn_a_log, gdn_dt_bias, gdn_norm,
           gdn_w_out):
    bf = jnp.bfloat16
    bp, tp = x_prompt.shape[:2]
    bs, ts = x_sample.shape[:2]
    assert tp % ATTN_TILE == 0 and WINDOW % ATTN_TILE == 0 and ATTN_TILE == MOBA_BLOCK and tp % SCAN_TILE == 0
    xp = x_prompt.reshape(bp * tp, D_MODEL)
    xs = x_sample.reshape(bs * ts, D_MODEL)

    for layer in range(DEPTH):
        kind = layer % N_MIXERS
        g_mix = ln_mix[layer]
        if kind == 0:
            w_main, w_tail = nsa_w_in[:, :NSA_MAIN].astype(bf), _pad_cols(nsa_w_in[:, NSA_MAIN:]).astype(bf)
            main_p, tail_p = norm_matmul(xp, g_mix, w_main), norm_matmul(xp, g_mix, w_tail)
            ps = jnp.concatenate([norm_matmul(xs, g_mix, w_main), norm_matmul(xs, g_mix, w_tail)], axis=-1)
            op, nsa_cmp_p, nsa_sel_p, nsa_win_p = _nsa_prompt(main_p, tail_p, bp, tp, nsa_cmp_w1, nsa_cmp_w2,
                                                              nsa_cmp_pe, rel_table)
            os_, nsa_cmp_s, nsa_sel_s, nsa_win_s = _nsa_sample(ps.reshape(bs, ts, -1), cache_nsa_cmp_kv,
                                                              cache_nsa_sel_kv, cache_nsa_win_kv, page_table,
                                                              nsa_cmp_w1, nsa_cmp_w2, nsa_cmp_pe, rel_table)
            w_out = nsa_w_out.astype(bf)
        elif kind == 1:
            w_in = moba_w_in.astype(bf)
            pp = norm_matmul(xp, g_mix, w_in)
            ps = norm_matmul(xs, g_mix, w_in).reshape(bs, ts, -1)
            op, moba_p = _moba_prompt(pp, bp, tp, rel_table)
            os_, moba_s = _moba_sample(ps, cache_moba_kv, page_table, rel_table)
            op = [op]
            w_out = moba_w_out.astype(bf)
        elif kind == 2:
            w_in = hg_w_in.astype(bf)
            pp = norm_matmul(xp, g_mix, w_in)
            ps = norm_matmul(xs, g_mix, w_in).reshape(bs, ts, -1)
            s0 = jnp.zeros((bp, HG_HEADS, HG_DK, HG_DV), jnp.float32)
            op, hg_p = hgrn2_scan(pp, hg_lb_logits, hg_norm, s0, layer, bp, tp)
            os_, hg_s = _hgrn2_core(ps, state_hgrn2, hg_lb_logits, layer, hg_norm)
            op = [op]
            w_out = hg_w_out.astype(bf)
        else:
            w_main, w_tail = gdn_w_in[:, :GDN_MAIN].astype(bf), _pad_cols(gdn_w_in[:, GDN_MAIN:]).astype(bf)
            main_p, tail_p = norm_matmul(xp, g_mix, w_main), norm_matmul(xp, g_mix, w_tail)
            ps = jnp.concatenate([norm_matmul(xs, g_mix, w_main), norm_matmul(xs, g_mix, w_tail)], axis=-1)
            buf0 = jnp.zeros((bp, GDN_CONV - 1, GDN_CONV_DIM), jnp.float32)
            s0 = jnp.zeros((bp, GDN_V_HEADS, GDN_DK, GDN_DV), jnp.float32)
            op, ssm_p = gdn_scan(main_p, tail_p, buf0, gdn_conv_w, gdn_a_log, gdn_dt_bias, gdn_norm, s0, bp, tp)
            conv_p = main_p.reshape(bp, tp, -1)[:, tp - (GDN_CONV - 1):, :GDN_CONV_DIM]
            os_, conv_s, ssm_s = _gdn_core(ps.reshape(bs, ts, -1), state_gdn_conv, state_gdn_ssm, gdn_conv_w,
                                           gdn_a_log, gdn_dt_bias, gdn_norm)
            op = [op]
            w_out = gdn_w_out.astype(bf)
        xp = matmul_res(op, w_out, xp)
        xs = matmul_res([os_.reshape(bs * ts, -1)], w_out, xs)
        w_up, w_down = ffn_w_up[layer].astype(bf), ffn_w_down[layer].astype(bf)
        xp = ffn(xp, ln_ffn[layer], w_up, w_down)
        xs = ffn(xs, ln_ffn[layer], w_up, w_down)
    y_prompt = final_norm(xp, ln_final).reshape(bp, tp, D_MODEL)
    y_sample = final_norm(xs, ln_final).reshape(bs, ts, D_MODEL)
    return (y_prompt, y_sample, nsa_cmp_p, nsa_cmp_s, nsa_sel_p, nsa_sel_s, nsa_win_p, nsa_win_s,
            moba_p, moba_s, hg_p, hg_s, conv_p, conv_s, ssm_p, ssm_s)
```

```python
import functools
import math

import jax
import jax.numpy as jnp
import numpy as np
from jax import lax
from jax.experimental import pallas as pl
from jax.experimental.pallas import tpu as pltpu

D_MODEL = 2048
DEPTH = 4
N_MIXERS = 4
HEAD_DIM = 128
N_HEADS = D_MODEL // HEAD_DIM
KV_HEADS = 4
GROUP = N_HEADS // KV_HEADS
ATTN_SCALE = HEAD_DIM ** -0.5
REL_BUCKETS = 32
REL_MAX_DIST = 128
CMP_BLOCK = 32
CMP_STRIDE = 16
CMP_HIDDEN = HEAD_DIM
SEL_BLOCK = 64
N_SEL = 16
WINDOW = 512
FORCE_SCORE = 1.0e4
MOBA_BLOCK = 256
MOBA_TOPK = 3
HG_DK = 128
HG_HEADS = D_MODEL // HG_DK
HG_DV = D_MODEL // HG_HEADS
GDN_DK = 128
GDN_DV = 128
GDN_QK_HEADS = D_MODEL // GDN_DK
GDN_V_HEADS = 2 * GDN_QK_HEADS
GDN_REP = GDN_V_HEADS // GDN_QK_HEADS
GDN_CONV = 4
GDN_CONV_DIM = 2 * GDN_QK_HEADS * GDN_DK + GDN_V_HEADS * GDN_DV
CHUNK = 64
NEG_INF = -1.0e30
NORM_EPS = 1e-6

Q_WIDTH = N_HEADS * HEAD_DIM
KV_WIDTH = 2 * KV_HEADS * HEAD_DIM
NSA_MAIN = Q_WIDTH + 3 * KV_WIDTH
GDN_MAIN = GDN_CONV_DIM + GDN_V_HEADS * GDN_DV

V7X_VMEM_LIMIT_BYTES = 56 * 1024 * 1024
LANE = 128
SUBLANE = 8
ATTN_TILE = 256
CMP_TILE = 128
SCAN_TILE = 512
NT_DIMS = (((1,), (1,)), ((), ()))
TN_DIMS = (((0,), (0,)), ((), ()))


def _cparams(*sem):
    return pltpu.CompilerParams(dimension_semantics=sem, vmem_limit_bytes=V7X_VMEM_LIMIT_BYTES)


def _row_tile(m, target):
    t = min(m, target)
    while m % t:
        t //= 2
    return t


def _col_tile(n, target):
    t = min(n, target)
    while n % t or t % LANE:
        t -= LANE
    return t


def _split_bf16(x, parts):
    out = []
    for _ in range(parts - 1):
        hi = x.astype(jnp.bfloat16)
        out.append(hi)
        x = x - hi.astype(jnp.float32)
    out.append(x.astype(jnp.bfloat16))
    return out


def _bf(x):
    return x.astype(jnp.bfloat16)


def _dot(a, b, dims=None):
    if dims is None:
        return jnp.dot(a, b, preferred_element_type=jnp.float32)
    return lax.dot_general(a, b, dims, preferred_element_type=jnp.float32)


def _norm_matmul_body(x_ref, g_ref, w_ref, o_ref, h_ref):
    @pl.when(pl.program_id(1) == 0)
    def _():
        x = x_ref[...]
        ms = jnp.mean(x * x, axis=-1, keepdims=True)
        h_ref[...] = _bf(x * lax.rsqrt(ms + NORM_EPS) * g_ref[...])

    o_ref[...] = _dot(h_ref[...], w_ref[...])


def norm_matmul(x, g, w):
    m, k = x.shape
    n = w.shape[1]
    tm = _row_tile(m, 512)
    tn = _col_tile(n, 512)
    return pl.pallas_call(
        _norm_matmul_body,
        grid=(m // tm, n // tn),
        in_specs=[pl.BlockSpec((tm, k), lambda i, j: (i, 0)),
                  pl.BlockSpec((1, k), lambda i, j: (0, 0)),
                  pl.BlockSpec((k, tn), lambda i, j: (0, j))],
        out_specs=pl.BlockSpec((tm, tn), lambda i, j: (i, j)),
        out_shape=jax.ShapeDtypeStruct((m, n), jnp.float32),
        scratch_shapes=[pltpu.VMEM((tm, k), jnp.bfloat16)],
        compiler_params=_cparams("parallel", "arbitrary"),
        name="norm_matmul",
    )(x, g.reshape(1, k), w)


def _matmul_res_body(*refs):
    *a_refs, w_ref, r_ref, o_ref = refs
    a = a_refs[0][...]
    for a_ref in a_refs[1:]:
        a = a + a_ref[...]
    o_ref[...] = r_ref[...] + _dot(_bf(a), w_ref[...])


def matmul_res(a_list, w, res):
    m, k = a_list[0].shape
    n = w.shape[1]
    tm = _row_tile(m, 512)
    tn = _col_tile(n, 1024)
    return pl.pallas_call(
        _matmul_res_body,
        grid=(m // tm, n // tn),
        in_specs=[pl.BlockSpec((tm, k), lambda i, j: (i, 0)) for _ in a_list]
        + [pl.BlockSpec((k, tn), lambda i, j: (0, j)),
           pl.BlockSpec((tm, tn), lambda i, j: (i, j))],
        out_specs=pl.BlockSpec((tm, tn), lambda i, j: (i, j)),
        out_shape=jax.ShapeDtypeStruct((m, n), jnp.float32),
        compiler_params=_cparams("parallel", "arbitrary"),
        name="matmul_res",
    )(*a_list, w, res)


def _ffn_body(x_ref, g_ref, wa_ref, wb_ref, wd_ref, o_ref, h_ref, acc_ref):
    j = pl.program_id(1)

    @pl.when(j == 0)
    def _():
        x = x_ref[...]
        ms = jnp.mean(x * x, axis=-1, keepdims=True)
        h_ref[...] = _bf(x * lax.rsqrt(ms + NORM_EPS) * g_ref[...])
        acc_ref[...] = jnp.zeros_like(acc_ref)

    h = h_ref[...]
    a = _dot(h, wa_ref[...])
    b = _dot(h, wb_ref[...])
    acc_ref[...] += _dot(_bf(a * jax.nn.sigmoid(a) * b), wd_ref[...])

    @pl.when(j == pl.num_programs(1) - 1)
    def _():
        o_ref[...] = x_ref[...] + acc_ref[...]


def ffn(x, g, w_up, w_down):
    m, k = x.shape
    hdim = w_down.shape[0]
    tm = _row_tile(m, 512)
    th = _col_tile(hdim, 512)
    nh = hdim // th
    return pl.pallas_call(
        _ffn_body,
        grid=(m // tm, nh),
        in_specs=[pl.BlockSpec((tm, k), lambda i, j: (i, 0)),
                  pl.BlockSpec((1, k), lambda i, j: (0, 0)),
                  pl.BlockSpec((k, th), lambda i, j: (0, j)),
                  pl.BlockSpec((k, th), lambda i, j: (0, j + nh)),
                  pl.BlockSpec((th, k), lambda i, j: (j, 0))],
        out_specs=pl.BlockSpec((tm, k), lambda i, j: (i, 0)),
        out_shape=jax.ShapeDtypeStruct((m, k), jnp.float32),
        scratch_shapes=[pltpu.VMEM((tm, k), jnp.bfloat16), pltpu.VMEM((tm, k), jnp.float32)],
        compiler_params=_cparams("parallel", "arbitrary"),
        name="ffn",
    )(x, g.reshape(1, k), w_up, w_up, w_down)


def _norm_body(x_ref, g_ref, o_ref):
    x = x_ref[...]
    ms = jnp.mean(x * x, axis=-1, keepdims=True)
    o_ref[...] = x * lax.rsqrt(ms + NORM_EPS) * g_ref[...]


def final_norm(x, g):
    m, k = x.shape
    tm = _row_tile(m, 512)
    return pl.pallas_call(
        _norm_body,
        grid=(m // tm,),
        in_specs=[pl.BlockSpec((tm, k), lambda i: (i, 0)), pl.BlockSpec((1, k), lambda i: (0, 0))],
        out_specs=pl.BlockSpec((tm, k), lambda i: (i, 0)),
        out_shape=jax.ShapeDtypeStruct((m, k), jnp.float32),
        compiler_params=_cparams("parallel"),
        name="final_norm",
    )(x, g.reshape(1, k))


def _bucket_np(dist):
    exact = REL_BUCKETS // 2
    d = np.maximum(dist, 0)
    ratio = np.log(np.maximum(d, 1).astype(np.float32) / exact) / math.log(REL_MAX_DIST / exact)
    large = np.minimum(exact + (ratio * (REL_BUCKETS - exact)).astype(np.int32), REL_BUCKETS - 1)
    return np.where(d < exact, d, large)


def _heads_to_lanes(t):
    keys, queries, _ = t.shape
    return t.reshape(keys, queries, KV_HEADS, GROUP).transpose(2, 0, 3, 1).reshape(KV_HEADS, keys, GROUP * queries)


def _flash_bias_tiles(rel_table, window):
    j = np.arange(ATTN_TILE)[:, None]
    i = np.arange(ATTN_TILE)[None, :]
    tiles = []
    for d in range(3):
        dist = d * ATTN_TILE + i - j
        ok = dist >= 0
        if window:
            ok = ok & (dist < window)
        tiles.append(_heads_to_lanes(jnp.where(ok[..., None], rel_table[_bucket_np(dist)], NEG_INF)))
    return jnp.stack(tiles, axis=1)


def _cmp_bias_table(rel_table, ncp):
    x = np.arange(ncp)[:, None]
    i = np.arange(CMP_TILE)[None, :]
    dist = i - CMP_STRIDE * (x - 16) - (CMP_BLOCK - 1)
    far = rel_table[REL_BUCKETS - 1]
    b = _heads_to_lanes(jnp.where((dist >= 0)[..., None], rel_table[_bucket_np(dist)], far))
    return jnp.concatenate([b, b], axis=1)


def _stack_heads(q):
    return jnp.concatenate([q[:, r * HEAD_DIM:(r + 1) * HEAD_DIM] for r in range(GROUP)], axis=0)


def _gate_columns(gt_ref, col0):
    gt = jax.nn.sigmoid(gt_ref[...])
    lane = lax.broadcasted_iota(jnp.int32, gt.shape, 1)
    return [jnp.sum(jnp.where(lane == col0 + r, gt, 0.0), axis=1, keepdims=True) for r in range(GROUP)]


def _heads_from_lanes(o_t, rows, cols=None):
    parts = []
    for r in range(GROUP):
        part = o_t[:, r * rows:(r + 1) * rows].T
        if cols is not None:
            part = part * cols[r]
        parts.append(part)
    return jnp.concatenate(parts, axis=1)


def _rank_rows(score, n_rows):
    row = lax.broadcasted_iota(jnp.int32, score.shape, 0)
    rank = jnp.zeros(score.shape, jnp.int32)
    for mm in range(n_rows):
        sm = score[mm:mm + 1, :]
        ahead = (sm > score) | ((sm == score) & (row > mm))
        rank = rank + ahead.astype(jnp.int32)
    return rank


def _cmp_select_body(q_ref, kc_ref, dt_ref, gt_ref, o_ref, pen_ref, *, n_sel_blocks):
    g = pl.program_id(1)
    qi = pl.program_id(2)
    tq = CMP_TILE
    cols = GROUP * tq
    ncp = kc_ref.shape[1]
    q4 = _bf(_stack_heads(q_ref[...]) * ATTN_SCALE)
    s = _dot(_bf(kc_ref[0]), q4, NT_DIMS)
    shift = (qi * (tq // CMP_STRIDE) + ncp - 16) % ncp
    bias = dt_ref[pl.ds(pl.multiple_of(ncp - shift, SUBLANE), ncp), :]
    t_col = qi * tq + (lax.broadcasted_iota(jnp.int32, (ncp, cols), 1) & (tq - 1))
    end_pos = lax.broadcasted_iota(jnp.int32, (ncp, cols), 0) * CMP_STRIDE + (CMP_BLOCK - 1)
    mask = t_col >= end_pos
    s = jnp.where(mask, s + bias, NEG_INF)
    m = jnp.max(s, axis=0, keepdims=True)
    e = jnp.where(mask, jnp.exp(s - m), 0.0)
    p = e / jnp.maximum(jnp.sum(e, axis=0, keepdims=True), 1e-30)
    o_t = _dot(_bf(kc_ref[1].T), _bf(p))
    o_ref[...] = _heads_from_lanes(o_t, tq, _gate_columns(gt_ref, g * GROUP))

    imp = p[:, 0:tq]
    for r in range(1, GROUP):
        imp = imp + p[:, r * tq:(r + 1) * tq]
    ratio = SEL_BLOCK // CMP_STRIDE
    j_i = lax.broadcasted_iota(jnp.int32, (n_sel_blocks, ncp), 0)
    c_i = lax.broadcasted_iota(jnp.int32, (n_sel_blocks, ncp), 1)
    w = _bf((c_i >= ratio * j_i - 1) & (c_i <= ratio * j_i + ratio - 1))
    score = sum(_dot(w, part) for part in _split_bf16(imp, 3))
    blk = lax.broadcasted_iota(jnp.int32, (n_sel_blocks, tq), 0)
    tok = qi * tq + lax.broadcasted_iota(jnp.int32, (n_sel_blocks, tq), 1)
    cur = lax.shift_right_logical(tok, int(math.log2(SEL_BLOCK)))
    forced = (blk == 0) | (blk == cur) | (blk == cur - 1)
    causal = blk <= cur
    score = jnp.where(forced, FORCE_SCORE, score)
    score = jnp.where(causal, score, -1.0)
    chosen = (_rank_rows(score, n_sel_blocks) < N_SEL) & causal
    pen_ref[...] = jnp.where(chosen, 0.0, NEG_INF)


def cmp_select(proj, tail, kc, dt, batch, seq):
    tq = CMP_TILE
    nq = seq // tq
    ncp = kc.shape[3]
    nsb = seq // SEL_BLOCK
    return pl.pallas_call(
        functools.partial(_cmp_select_body, n_sel_blocks=nsb),
        grid=(batch, KV_HEADS, nq),
        in_specs=[pl.BlockSpec((tq, GROUP * HEAD_DIM), lambda b, g, i: (b * nq + i, g)),
                  pl.BlockSpec((None, None, 2, ncp, HEAD_DIM), lambda b, g, i: (b, g, 0, 0, 0)),
                  pl.BlockSpec((None, 2 * ncp, GROUP * tq), lambda b, g, i: (g, 0, 0)),
                  pl.BlockSpec((tq, LANE), lambda b, g, i: (b * nq + i, 0))],
        out_specs=[pl.BlockSpec((tq, GROUP * HEAD_DIM), lambda b, g, i: (b * nq + i, g)),
                   pl.BlockSpec((None, None, nsb, tq), lambda b, g, i: (b, g, 0, i))],
        out_shape=[jax.ShapeDtypeStruct((batch * seq, Q_WIDTH), jnp.float32),
                   jax.ShapeDtypeStruct((batch, KV_HEADS, nsb, seq), jnp.float32)],
        compiler_params=_cparams("parallel", "parallel", "arbitrary"),
        name="cmp_select",
    )(proj, kc, dt, tail)


def _flash_body(*refs, pen_block, pen_per_head, k_back, gate_col0, seq):
    it = iter(refs)
    q_ref, k_ref, v_ref, bt_ref = next(it), next(it), next(it), next(it)
    pen_ref = next(it) if pen_block else None
    gt_ref = next(it) if gate_col0 is not None else None
    o_ref, m_ref, l_ref, acc_ref, qa_ref, kb_ref, vt_ref = (next(it) for _ in range(7))
    g = pl.program_id(1)
    qi = pl.program_id(2)
    tq = tk = ATTN_TILE
    cols = GROUP * tq
    n_cls = bt_ref.shape[0]

    @pl.when(qi == 0)
    def _():
        for c in range(seq // tk):
            kb_ref[c * tk:(c + 1) * tk, :] = _bf(k_ref[c * tk:(c + 1) * tk, :])
            vt_ref[:, c * tk:(c + 1) * tk] = _bf(v_ref[c * tk:(c + 1) * tk, :].T)

    qa_ref[...] = _bf(_stack_heads(q_ref[...]) * ATTN_SCALE)
    m_ref[...] = jnp.full(m_ref.shape, NEG_INF, jnp.float32)
    l_ref[...] = jnp.zeros(l_ref.shape, jnp.float32)
    acc_ref[...] = jnp.zeros(acc_ref.shape, jnp.float32)

    def step(kj, carry):
        k0 = pl.multiple_of(kj * tk, tk)
        s = _dot(kb_ref[pl.ds(k0, tk), :], qa_ref[...], NT_DIMS)
        s = s + bt_ref[jnp.minimum(qi - kj, n_cls - 1)]
        if pen_block:
            per_tile = tk // pen_block
            pieces = []
            for a in range(per_tile):
                pen = pen_ref[pl.ds(kj * per_tile + a, 1), :]
                if not pen_per_head:
                    pen = jnp.concatenate([pen] * GROUP, axis=1)
                pieces.append(s[a * pen_block:(a + 1) * pen_block] + pen)
            s = pieces[0] if per_tile == 1 else jnp.concatenate(pieces, axis=0)
        m_prev = m_ref[...]
        m_new = jnp.maximum(m_prev, jnp.max(s, axis=0, keepdims=True))
        alpha = jnp.exp(m_prev - m_new)
        p = jnp.exp(s - m_new)
        l_ref[...] = alpha * l_ref[...] + jnp.sum(p, axis=0, keepdims=True)
        acc_ref[...] = alpha * acc_ref[...] + _dot(vt_ref[:, pl.ds(k0, tk)], _bf(p))
        m_ref[...] = m_new
        return carry

    k_lo = jnp.maximum(qi - k_back, 0) if k_back is not None else 0
    lax.fori_loop(k_lo, qi + 1, step, 0)
    o_t = acc_ref[...] / jnp.maximum(l_ref[...], 1e-30)
    cols_g = _gate_columns(gt_ref, gate_col0 + g * GROUP) if gate_col0 is not None else None
    o_ref[...] = _heads_from_lanes(o_t, tq, cols_g)


def flash_attention(q_arr, kv_arr, k_col, v_col, bias_tiles, batch, seq, *, pen=None, pen_block=0,
                    k_back=None, gate_arr=None, gate_col0=None):
    tq = ATTN_TILE
    nq = seq // tq
    cols = GROUP * tq
    in_specs = [pl.BlockSpec((tq, GROUP * HEAD_DIM), lambda b, g, i: (b * nq + i, g)),
                pl.BlockSpec((seq, HEAD_DIM), lambda b, g, i: (b, k_col + g)),
                pl.BlockSpec((seq, HEAD_DIM), lambda b, g, i: (b, v_col + g)),
                pl.BlockSpec((None,) + bias_tiles.shape[1:], lambda b, g, i: (g, 0, 0, 0))]
    args = [q_arr, kv_arr, kv_arr, bias_tiles]
    pen_per_head = False
    if pen is not None:
        if pen.ndim == 4:
            in_specs.append(pl.BlockSpec((None, None, pen.shape[2], tq), lambda b, g, i: (b, g, 0, i)))
        else:
            pen_per_head = True
            in_specs.append(pl.BlockSpec((None, None, None, pen.shape[3], cols), lambda b, g, i: (b, g, i, 0, 0)))
        args.append(pen)
    if gate_arr is not None:
        in_specs.append(pl.BlockSpec((tq, LANE), lambda b, g, i: (b * nq + i, 0)))
        args.append(gate_arr)
    return pl.pallas_call(
        functools.partial(_flash_body, pen_block=pen_block if pen is not None else 0, pen_per_head=pen_per_head,
                          k_back=k_back, gate_col0=gate_col0 if gate_arr is not None else None, seq=seq),
        grid=(batch, KV_HEADS, nq),
        in_specs=in_specs,
        out_specs=pl.BlockSpec((tq, GROUP * HEAD_DIM), lambda b, g, i: (b * nq + i, g)),
        out_shape=jax.ShapeDtypeStruct((batch * seq, Q_WIDTH), jnp.float32),
        scratch_shapes=[pltpu.VMEM((1, cols), jnp.float32), pltpu.VMEM((1, cols), jnp.float32),
                        pltpu.VMEM((HEAD_DIM, cols), jnp.float32), pltpu.VMEM((cols, HEAD_DIM), jnp.bfloat16),
                        pltpu.VMEM((seq, HEAD_DIM), jnp.bfloat16), pltpu.VMEM((HEAD_DIM, seq), jnp.bfloat16)],
        compiler_params=_cparams("parallel", "parallel", "arbitrary"),
        name="flash_attention",
    )(*args)


def _moba_gate_body(q_ref, k_ref, pen_ref, km_ref, *, n_blocks):
    qi = pl.program_id(2)
    tq = ATTN_TILE
    cols = GROUP * tq

    @pl.when(qi == 0)
    def _():
        k = k_ref[...]
        km_ref[...] = jnp.sum(k.reshape(n_blocks, MOBA_BLOCK, HEAD_DIM), axis=1) / MOBA_BLOCK

    qh, ql = _split_bf16(_stack_heads(q_ref[...]), 2)
    kh, kl = _split_bf16(km_ref[...], 2)
    gate = _dot(kh, qh, NT_DIMS) + _dot(kh, ql, NT_DIMS) + _dot(kl, qh, NT_DIMS)
    blk = lax.broadcasted_iota(jnp.int32, (n_blocks, cols), 0)
    tok = qi * tq + (lax.broadcasted_iota(jnp.int32, (n_blocks, cols), 1) & (tq - 1))
    own = lax.shift_right_logical(tok, int(math.log2(MOBA_BLOCK)))
    gate = jnp.where(blk < own, gate, NEG_INF)
    chosen = ((_rank_rows(gate, n_blocks) < MOBA_TOPK) & (blk < own)) | (blk == own)
    pen_ref[...] = jnp.where(chosen, 0.0, NEG_INF)


def moba_gate(proj, k_col, batch, seq):
    tq = ATTN_TILE
    nq = seq // tq
    cols = GROUP * tq
    nb = seq // MOBA_BLOCK
    return pl.pallas_call(
        functools.partial(_moba_gate_body, n_blocks=nb),
        grid=(batch, KV_HEADS, nq),
        in_specs=[pl.BlockSpec((tq, GROUP * HEAD_DIM), lambda b, g, i: (b * nq + i, g)),
                  pl.BlockSpec((seq, HEAD_DIM), lambda b, g, i: (b, k_col + g))],
        out_specs=pl.BlockSpec((None, None, None, nb, cols), lambda b, g, i: (b, g, i, 0, 0)),
        out_shape=jax.ShapeDtypeStruct((batch, KV_HEADS, nq, nb, cols), jnp.float32),
        scratch_shapes=[pltpu.VMEM((nb, HEAD_DIM), jnp.float32)],
        compiler_params=_cparams("parallel", "parallel", "arbitrary"),
        name="moba_gate",
    )(proj, proj)


def _tril_ones(n, strict=False):
    r = lax.broadcasted_iota(jnp.int32, (n, n), 0)
    c = lax.broadcasted_iota(jnp.int32, (n, n), 1)
    return (r > c) if strict else (r >= c)


def _chunk_cumsum(x):
    tril = _bf(_tril_ones(x.shape[0]))
    return sum(_dot(tril, part) for part in _split_bf16(x, 3))


def _head_rms_gate(o, norm_g, gate):
    ms = jnp.mean(o * o, axis=-1, keepdims=True)
    return o * lax.rsqrt(ms + NORM_EPS) * norm_g * (gate * jax.nn.sigmoid(gate))


def _hgrn2_body(q_ref, f_ref, i_ref, g_ref, lbl_ref, ng_ref, s0_ref, o_ref, s_out_ref, st_ref, *, layer):
    ti = pl.program_id(2)

    @pl.when(ti == 0)
    def _():
        st_ref[...] = s0_ref[...].T

    lbl = lbl_ref[...]
    e = jnp.exp(lbl - jnp.max(lbl, axis=0, keepdims=True))
    p = e / jnp.sum(e, axis=0, keepdims=True)
    lb = jnp.zeros((1, HG_DK), jnp.float32)
    for r in range(1, layer + 1):
        lb = lb + p[r:r + 1]
    causal = _tril_ones(CHUNK)
    work = []
    for c in range(q_ref.shape[0] // CHUNK):
        sl = slice(c * CHUNK, (c + 1) * CHUNK)
        q = q_ref[sl, :]
        qh = q * jax.nn.sigmoid(q) * HG_DK ** -0.5
        fg = lb + (1.0 - lb) * jax.nn.sigmoid(f_ref[sl, :])
        k = 1.0 - fg
        v = _bf(i_ref[sl, :])
        b = _chunk_cumsum(jnp.log(fg))
        b_mid = b[CHUNK // 2:CHUNK // 2 + 1]
        b_last = b[CHUNK - 1:CHUNK]
        a = _dot(_bf(qh * jnp.exp(b - b_mid)), _bf(k * jnp.exp(b_mid - b)), NT_DIMS)
        a = jnp.where(causal, a, 0.0)
        work.append((sl, _dot(_bf(a), v), _bf(qh * jnp.exp(b)), jnp.exp(b_last),
                     _dot(v, _bf(k * jnp.exp(b_last - b)), TN_DIMS)))
    for sl, o_intra, q_in, d_last, kv in work:
        st = st_ref[...]
        o = o_intra + _dot(q_in, _bf(st), NT_DIMS)
        st_ref[...] = st * d_last + kv
        o_ref[sl, :] = _head_rms_gate(o, ng_ref[...], g_ref[sl, :])

    @pl.when(ti == pl.num_programs(2) - 1)
    def _():
        s_out_ref[...] = st_ref[...].T


def hgrn2_scan(proj, lb_logits, norm_g, s0, layer, batch, seq):
    tt = _row_tile(seq, SCAN_TILE)
    nt = seq // tt
    h = HG_HEADS

    def col(k):
        return pl.BlockSpec((tt, HG_DK), lambda b, hh, t: (b * nt + t, k * h + hh))

    return pl.pallas_call(
        functools.partial(_hgrn2_body, layer=layer),
        grid=(batch, h, nt),
        in_specs=[col(0), col(1), col(2), col(3),
                  pl.BlockSpec((DEPTH, HG_DK), lambda b, hh, t: (0, hh)),
                  pl.BlockSpec((1, HG_DV), lambda b, hh, t: (0, 0)),
                  pl.BlockSpec((None, None, HG_DK, HG_DV), lambda b, hh, t: (b, hh, 0, 0))],
        out_specs=[pl.BlockSpec((tt, HG_DV), lambda b, hh, t: (b * nt + t, hh)),
                   pl.BlockSpec((None, None, HG_DK, HG_DV), lambda b, hh, t: (b, hh, 0, 0))],
        out_shape=[jax.ShapeDtypeStruct((batch * seq, h * HG_DV), jnp.float32),
                   jax.ShapeDtypeStruct((batch, h, HG_DK, HG_DV), jnp.float32)],
        scratch_shapes=[pltpu.VMEM((HG_DV, HG_DK), jnp.float32)],
        compiler_params=_cparams("parallel", "parallel", "arbitrary"),
        name="hgrn2_scan",
    )(proj, proj, proj, proj, lb_logits, norm_g.reshape(1, HG_DV), s0)


def _lane_column(x, lane_idx):
    lane = lax.broadcasted_iota(jnp.int32, x.shape, 1)
    return jnp.sum(jnp.where(lane == lane_idx, x, 0.0), axis=1, keepdims=True)


def _softplus(x):
    return jnp.maximum(x, 0.0) + jnp.log(1.0 + jnp.exp(-jnp.abs(x)))


def _l2n(x):
    return x * lax.rsqrt(jnp.sum(x * x, axis=-1, keepdims=True) + NORM_EPS)


def _gdn_body(q_ref, k_ref, v_ref, z_ref, t_ref, bq_ref, bk_ref, bv_ref, wq_ref, wk_ref, wv_ref,
              al_ref, dtb_ref, ng_ref, s0_ref, o_ref, s_out_ref, xs_ref, y_ref, s_ref):
    hq = pl.program_id(1)
    ti = pl.program_id(2)
    tt = q_ref.shape[0]
    dk, dv = GDN_DK, GDN_DV
    pad = SUBLANE

    @pl.when(ti == 0)
    def _():
        s_ref[...] = s0_ref[...]
        xs_ref[0:pad, :] = jnp.concatenate([bq_ref[...], bk_ref[...], bv_ref[...]], axis=1)

    x = jnp.concatenate([q_ref[...], k_ref[...], v_ref[...]], axis=1)
    xs_ref[pad:, :] = x
    cw = jnp.concatenate([wq_ref[...], wk_ref[...], wv_ref[...]], axis=1)
    y = xs_ref[pad - 3:pad - 3 + tt, :] * cw[0:1]
    for i in range(1, GDN_CONV - 1):
        y = y + xs_ref[pad - 3 + i:pad - 3 + i + tt, :] * cw[i:i + 1]
    y = y + x * cw[GDN_CONV - 1:GDN_CONV]
    xs_ref[0:pad, :] = x[tt - pad:tt]
    y_ref[...] = y * jax.nn.sigmoid(y)

    strict = _tril_ones(CHUNK, strict=True)
    incl = _tril_ones(CHUNK)
    sel_rows = lax.shift_right_logical(lax.broadcasted_iota(jnp.int32, (GDN_REP * CHUNK, LANE), 0),
                                       int(math.log2(CHUNK)))
    sel_lane = lax.broadcasted_iota(jnp.int32, (GDN_REP * CHUNK, LANE), 1)
    pick = _bf(sel_lane == GDN_V_HEADS + hq * GDN_REP + sel_rows)
    n_chunks = tt // CHUNK

    work = []
    for c in range(n_chunks):
        sl = slice(c * CHUNK, (c + 1) * CHUNK)
        yc = y_ref[sl, :]
        q = _l2n(yc[:, 0:dk]) * dk ** -0.5
        k = _l2n(yc[:, dk:2 * dk])
        qb, kb = _bf(q), _bf(k)
        kk = _dot(kb, kb, NT_DIMS)
        qk = _dot(qb, kb, NT_DIMS)
        tl = t_ref[sl, :]
        beta_all = jax.nn.sigmoid(tl)
        g_all = _chunk_cumsum(-jnp.exp(al_ref[...]) * _softplus(tl + dtb_ref[...]))
        g_rows = sum(_dot(pick, part, NT_DIMS) for part in _split_bf16(g_all, 3))
        for e in range(GDN_REP):
            hv = hq * GDN_REP + e
            v = yc[:, 2 * dk + e * dv:2 * dk + (e + 1) * dv]
            bt = _lane_column(beta_all, hv)
            gc = _lane_column(g_all, GDN_V_HEADS + hv)
            gdiff = gc - g_rows[e * CHUNK:(e + 1) * CHUNK]
            d_strict = jnp.where(strict, jnp.exp(jnp.where(strict, gdiff, 0.0)), 0.0)
            d_incl = jnp.where(incl, jnp.exp(jnp.where(incl, gdiff, 0.0)), 0.0)
            eg = jnp.exp(gc)
            g_last = gc[CHUNK - 1:CHUNK]
            work.append(dict(
                c=c, e=e, sol=jnp.concatenate([bt * v, (bt * eg) * k], axis=1), pw=bt * kk * d_strict,
                aq=_bf(qk * d_incl), q_in=_bf(q * eg), k_out=_bf(k * jnp.exp(g_last - gc)), d_last=jnp.exp(g_last)))

    r_i = lax.broadcasted_iota(jnp.int32, (CHUNK, CHUNK), 0)
    c_i = lax.broadcasted_iota(jnp.int32, (CHUNK, CHUNK), 1)
    same = [lax.shift_right_logical(r_i, sh) == lax.shift_right_logical(c_i, sh) for sh in range(3, 7)]
    eye = (r_i == c_i).astype(jnp.float32)
    for wk in work:
        l8 = jnp.where(same[0], wk["pw"], 0.0)
        l8b = _bf(l8)
        wk["t"] = eye - l8
        wk["p"] = _dot(l8b, l8b)
    for wk in work:
        pb = _bf(wk["p"])
        wk["t"] = wk["t"] + _dot(_bf(wk["t"]), pb)
        wk["p"] = _dot(pb, pb)
    for wk in work:
        wk["t"] = wk["t"] + _dot(_bf(wk["t"]), _bf(wk["p"]))
    for lvl in range(1, len(same)):
        for wk in work:
            tb = _bf(wk["t"])
            off = _bf(jnp.where(same[lvl] & jnp.logical_not(same[lvl - 1]), wk["pw"], 0.0))
            wk["t"] = wk["t"] - _dot(tb, _bf(_dot(off, tb)))
    for wk in work:
        wk["sol"] = _dot(_bf(wk["t"]), _bf(wk["sol"]))

    for wk in work:
        c, e = wk["c"], wk["e"]
        sl = slice(c * CHUNK, (c + 1) * CHUNK)
        u0, w = wk["sol"][:, :dv], wk["sol"][:, dv:]
        s = s_ref[e]
        sb = _bf(s)
        u = u0 - _dot(_bf(w), sb)
        o = _dot(wk["q_in"], sb) + _dot(wk["aq"], _bf(u))
        s_ref[e] = wk["d_last"] * s + _dot(wk["k_out"], _bf(u), TN_DIMS)
        o_ref[sl, e * dv:(e + 1) * dv] = _head_rms_gate(o, ng_ref[...], z_ref[sl, e * dv:(e + 1) * dv])

    @pl.when(ti == pl.num_programs(2) - 1)
    def _():
        s_out_ref[...] = s_ref[...]


def gdn_scan(main, tail, conv_buf, conv_w, a_log, dt_bias, norm_g, s0, batch, seq):
    tt = _row_tile(seq, SCAN_TILE)
    nt = seq // tt
    hq, rep, dk, dv = GDN_QK_HEADS, GDN_REP, GDN_DK, GDN_DV
    vw = rep * dv
    buf = jnp.pad(conv_buf, ((0, 0), (SUBLANE - (GDN_CONV - 1), 0), (0, 0)))
    pad_lanes = jnp.zeros((LANE - 2 * GDN_V_HEADS,), jnp.float32)
    a_row = jnp.concatenate([jnp.zeros((GDN_V_HEADS,), jnp.float32), a_log, pad_lanes]).reshape(1, LANE)
    dt_row = jnp.concatenate([jnp.zeros((GDN_V_HEADS,), jnp.float32), dt_bias, pad_lanes]).reshape(1, LANE)
    k0 = hq
    v0 = 2 * hq * dk // vw
    z0 = GDN_CONV_DIM // vw
    row = lambda b, h, t: b * nt + t
    return pl.pallas_call(
        _gdn_body,
        grid=(batch, hq, nt),
        in_specs=[pl.BlockSpec((tt, dk), lambda b, h, t: (row(b, h, t), h)),
                  pl.BlockSpec((tt, dk), lambda b, h, t: (row(b, h, t), k0 + h)),
                  pl.BlockSpec((tt, vw), lambda b, h, t: (row(b, h, t), v0 + h)),
                  pl.BlockSpec((tt, vw), lambda b, h, t: (row(b, h, t), z0 + h)),
                  pl.BlockSpec((tt, LANE), lambda b, h, t: (row(b, h, t), 0)),
                  pl.BlockSpec((None, SUBLANE, dk), lambda b, h, t: (b, 0, h)),
                  pl.BlockSpec((None, SUBLANE, dk), lambda b, h, t: (b, 0, k0 + h)),
                  pl.BlockSpec((None, SUBLANE, vw), lambda b, h, t: (b, 0, v0 + h)),
                  pl.BlockSpec((GDN_CONV, dk), lambda b, h, t: (0, h)),
                  pl.BlockSpec((GDN_CONV, dk), lambda b, h, t: (0, k0 + h)),
                  pl.BlockSpec((GDN_CONV, vw), lambda b, h, t: (0, v0 + h)),
                  pl.BlockSpec((1, LANE), lambda b, h, t: (0, 0)),
                  pl.BlockSpec((1, LANE), lambda b, h, t: (0, 0)),
                  pl.BlockSpec((1, dv), lambda b, h, t: (0, 0)),
                  pl.BlockSpec((None, rep, dk, dv), lambda b, h, t: (b, h, 0, 0))],
        out_specs=[pl.BlockSpec((tt, vw), lambda b, h, t: (row(b, h, t), h)),
                   pl.BlockSpec((None, rep, dk, dv), lambda b, h, t: (b, h, 0, 0))],
        out_shape=[jax.ShapeDtypeStruct((batch * seq, GDN_V_HEADS * dv), jnp.float32),
                   jax.ShapeDtypeStruct((batch, GDN_V_HEADS, dk, dv), jnp.float32)],
        scratch_shapes=[pltpu.VMEM((tt + SUBLANE, 2 * dk + vw), jnp.float32),
                        pltpu.VMEM((tt, 2 * dk + vw), jnp.float32),
                        pltpu.VMEM((rep, dk, dv), jnp.float32)],
        compiler_params=_cparams("parallel", "parallel", "arbitrary"),
        name="gdn_scan",
    )(main, main, main, main, tail, buf, buf, buf, conv_w, conv_w, conv_w, a_row, dt_row,
      norm_g.reshape(1, dv), s0)


PAGE = 128
CMP_PAGES = 8
CHUNKS_PER_PAGE = PAGE // CMP_STRIDE
ROWS = GROUP * 4


def _compress_part_body(*refs, n_prefetch):
    refs = refs[n_prefetch:]
    pages, w_ref, o_ref, xs_ref = refs[:CMP_PAGES], refs[CMP_PAGES], refs[CMP_PAGES + 1], refs[CMP_PAGES + 2]
    per_g = CMP_PAGES * CHUNKS_PER_PAGE
    for k, pg in enumerate(pages):
        for cg in range(2 * KV_HEADS):
            xs_ref[k, cg] = pg[:, cg * HEAD_DIM:(cg + 1) * HEAD_DIM]
    for c in range(2):
        acc = None
        for l in range(CMP_STRIDE):
            rows = [xs_ref.at[k, c * KV_HEADS + g][pl.ds(l, CHUNKS_PER_PAGE, stride=CMP_STRIDE), :]
                    for g in range(KV_HEADS) for k in range(CMP_PAGES)]
            d = _dot(_bf(jnp.concatenate(rows, axis=0)), w_ref[c, l])
            acc = d if acc is None else acc + d
        for g in range(KV_HEADS):
            o_ref[c, g] = acc[g * per_g:(g + 1) * per_g]


def _compress_weights(cmp_w1):
    n_part = CMP_BLOCK // CMP_STRIDE
    w = cmp_w1.reshape(2, n_part, CMP_STRIDE, HEAD_DIM, CMP_HIDDEN).transpose(0, 2, 3, 1, 4)
    return w.reshape(2, CMP_STRIDE, HEAD_DIM, n_part * CMP_HIDDEN).astype(jnp.bfloat16)


def compress_part_rows(kv_arr, col_block, cmp_w1, batch, seq):
    w = _compress_weights(cmp_w1)
    pages_per_b = seq // PAGE
    steps = pages_per_b // CMP_PAGES
    per_g = CMP_PAGES * CHUNKS_PER_PAGE

    def page_spec(k):
        return pl.BlockSpec((PAGE, KV_WIDTH), lambda b, s: (b * pages_per_b + s * CMP_PAGES + k, col_block))

    return pl.pallas_call(
        functools.partial(_compress_part_body, n_prefetch=0),
        grid=(batch, steps),
        in_specs=[page_spec(k) for k in range(CMP_PAGES)] + [pl.BlockSpec(w.shape, lambda b, s: (0, 0, 0, 0))],
        out_specs=pl.BlockSpec((None, 2, KV_HEADS, per_g, w.shape[-1]), lambda b, s: (b, 0, 0, s, 0)),
        out_shape=jax.ShapeDtypeStruct((batch, 2, KV_HEADS, seq // CMP_STRIDE, w.shape[-1]), jnp.float32),
        scratch_shapes=[pltpu.VMEM((CMP_PAGES, 2 * KV_HEADS, PAGE, HEAD_DIM), jnp.float32)],
        compiler_params=_cparams("parallel", "arbitrary"),
        name="compress_part_rows",
    )(*([kv_arr] * CMP_PAGES), w)


def compress_part_paged(pool, page_table, cmp_w1):
    w = _compress_weights(cmp_w1)
    batch, n_pages = page_table.shape
    steps = n_pages // CMP_PAGES
    per_g = CMP_PAGES * CHUNKS_PER_PAGE

    def page_spec(k):
        return pl.BlockSpec((None, PAGE, KV_WIDTH), lambda b, s, pt: (pt[b, s * CMP_PAGES + k], 0, 0))

    return pl.pallas_call(
        functools.partial(_compress_part_body, n_prefetch=1),
        grid_spec=pltpu.PrefetchScalarGridSpec(
            num_scalar_prefetch=1, grid=(batch, steps),
            in_specs=[page_spec(k) for k in range(CMP_PAGES)]
            + [pl.BlockSpec(w.shape, lambda b, s, pt: (0, 0, 0, 0))],
            out_specs=pl.BlockSpec((None, 2, KV_HEADS, per_g, w.shape[-1]), lambda b, s, pt: (b, 0, 0, s, 0)),
            scratch_shapes=[pltpu.VMEM((CMP_PAGES, 2 * KV_HEADS, PAGE, HEAD_DIM), jnp.float32)]),
        out_shape=jax.ShapeDtypeStruct((batch, 2, KV_HEADS, n_pages * CHUNKS_PER_PAGE, w.shape[-1]), jnp.float32),
        compiler_params=_cparams("parallel", "arbitrary"),
        name="compress_part_paged",
    )(page_table, *([pool] * CMP_PAGES), w)


def _gelu_tanh(x):
    return x * (0.5 * (1.0 + jnp.tanh(math.sqrt(2.0 / math.pi) * (x + 0.044715 * (x * x * x)))))


def _compress_finish_body(p_ref, peh_ref, w2_ref, o_ref):
    n = p_ref.shape[1]
    for c in range(2):
        p = p_ref[c]
        hid = peh_ref[c:c + 1, :] + p[:, :CMP_HIDDEN]
        hid = hid + pltpu.roll(p[:, CMP_HIDDEN:], n - 1, 0)
        o_ref[c] = _dot(_bf(_gelu_tanh(hid)), _bf(w2_ref[c]))


def compress_finish(part, cmp_w1, cmp_w2, cmp_pe):
    batch, _, _, n, width = part.shape
    pe_hid = jnp.einsum('cld,cldh->ch', cmp_pe, cmp_w1)
    return pl.pallas_call(
        _compress_finish_body,
        grid=(batch, KV_HEADS),
        in_specs=[pl.BlockSpec((None, 2, None, n, width), lambda b, g: (b, 0, g, 0, 0)),
                  pl.BlockSpec((2, CMP_HIDDEN), lambda b, g: (0, 0)),
                  pl.BlockSpec((2, CMP_HIDDEN, HEAD_DIM), lambda b, g: (0, 0, 0))],
        out_specs=pl.BlockSpec((None, None, 2, n, HEAD_DIM), lambda b, g: (b, g, 0, 0, 0)),
        out_shape=jax.ShapeDtypeStruct((batch, KV_HEADS, 2, n, HEAD_DIM), jnp.float32),
        compiler_params=_cparams("parallel", "parallel"),
        name="compress_finish",
    )(part, pe_hid, cmp_w2)


def _rows_to_col(row, n):
    eye = lax.broadcasted_iota(jnp.int32, (n, n), 0) == lax.broadcasted_iota(jnp.int32, (n, n), 1)
    return jnp.sum(jnp.where(eye, jnp.broadcast_to(row, (n, n)), 0.0), axis=1, keepdims=True)


def _sample_bias_tiles(rel_table, past, n_new):
    j = np.arange(PAGE)[:, None]
    t = np.arange(n_new)[None, :]
    far = np.full((PAGE, n_new), REL_MAX_DIST)
    first = WINDOW + t - j
    last = PAGE + t - j
    new = t - j
    tiles = []
    for dist, ok in ((far, far > 0), (first, first < WINDOW), (last, last > 0), (new, (new >= 0) & (j < n_new))):
        b = jnp.where(ok[..., None], rel_table[_bucket_np(dist)], NEG_INF)
        b = b.reshape(PAGE, n_new, KV_HEADS, GROUP).transpose(2, 0, 3, 1).reshape(KV_HEADS, PAGE, GROUP * n_new)
        tiles.append(b)
    return jnp.stack(tiles, axis=1)


def _sample_cmp_body(q_ref, kc_ref, bias_ref, gt_ref, o_ref, pen_ref, sc_ref, *, n_sel_blocks, past):
    n = kc_ref.shape[1]
    nbp = pen_ref.shape[0]
    q = _bf(q_ref[...] * ATTN_SCALE)
    s = _dot(_bf(kc_ref[0]), q, NT_DIMS) + bias_ref[...]
    m = jnp.max(s, axis=0, keepdims=True)
    e = jnp.exp(s - m)
    p = e / jnp.maximum(jnp.sum(e, axis=0, keepdims=True), 1e-30)
    o = _dot(_bf(p), _bf(kc_ref[1]), TN_DIMS)
    o_ref[...] = o * jax.nn.sigmoid(gt_ref[...])

    r_i = lax.broadcasted_iota(jnp.int32, (ROWS, ROWS), 0)
    c_i = lax.broadcasted_iota(jnp.int32, (ROWS, ROWS), 1)
    n_tok = ROWS // GROUP
    same_tok = _bf((r_i & (n_tok - 1)) == (c_i & (n_tok - 1)))
    ratio = SEL_BLOCK // CMP_STRIDE
    j_i = lax.broadcasted_iota(jnp.int32, (nbp, n), 0)
    k_i = lax.broadcasted_iota(jnp.int32, (nbp, n), 1)
    w = _bf((k_i >= ratio * j_i - 1) & (k_i <= ratio * j_i + ratio - 1) & (j_i < n_sel_blocks))
    imp = sum(_dot(part, same_tok) for part in _split_bf16(p, 3))
    score = sum(_dot(w, part) for part in _split_bf16(imp, 3))
    blk = lax.broadcasted_iota(jnp.int32, (nbp, ROWS), 0)
    tok = past + (lax.broadcasted_iota(jnp.int32, (nbp, ROWS), 1) & (n_tok - 1))
    cur = lax.shift_right_logical(tok, int(math.log2(SEL_BLOCK)))
    forced = (blk == 0) | (blk == cur) | (blk == cur - 1)
    causal = blk <= cur
    score = jnp.where(forced, FORCE_SCORE, score)
    score = jnp.where(causal, score, -1.0)
    sc_ref[...] = score

    def count(mm, rank):
        sm = sc_ref[pl.ds(mm, 1), :]
        ahead = (sm > score) | ((sm == score) & (blk > mm))
        return rank + ahead.astype(jnp.int32)

    rank = lax.fori_loop(0, n_sel_blocks, count, jnp.zeros((nbp, ROWS), jnp.int32))
    chosen = (rank < N_SEL) & causal & (blk < n_sel_blocks)
    pen_ref[...] = jnp.where(chosen, 0.0, NEG_INF)


def sample_cmp_select(qs, kc, bias, gate, past, n_sel_blocks):
    batch = qs.shape[0]
    n = kc.shape[3]
    nbp = -(-n_sel_blocks // SUBLANE) * SUBLANE
    return pl.pallas_call(
        functools.partial(_sample_cmp_body, n_sel_blocks=n_sel_blocks, past=past),
        grid=(batch, KV_HEADS),
        in_specs=[pl.BlockSpec((None, None, ROWS, HEAD_DIM), lambda b, g: (b, g, 0, 0)),
                  pl.BlockSpec((None, None, 2, n, HEAD_DIM), lambda b, g: (b, g, 0, 0, 0)),
                  pl.BlockSpec((None, n, ROWS), lambda b, g: (g, 0, 0)),
                  pl.BlockSpec((None, None, ROWS, HEAD_DIM), lambda b, g: (b, g, 0, 0))],
        out_specs=[pl.BlockSpec((None, None, ROWS, HEAD_DIM), lambda b, g: (b, g, 0, 0)),
                   pl.BlockSpec((None, None, nbp, ROWS), lambda b, g: (b, g, 0, 0))],
        out_shape=[jax.ShapeDtypeStruct((batch, KV_HEADS, ROWS, HEAD_DIM), jnp.float32),
                   jax.ShapeDtypeStruct((batch, KV_HEADS, nbp, ROWS), jnp.float32)],
        scratch_shapes=[pltpu.VMEM((nbp, ROWS), jnp.float32)],
        compiler_params=_cparams("parallel", "parallel"),
        name="sample_cmp_select",
    )(qs, kc, bias, gate)


ATTN_PAGES = 4


def _paged_attn_body(*refs, pen_block, gated, n_pages):
    it = iter(refs)
    pt_ref, tid_ref, q_ref = next(it), next(it), next(it)
    pages = [next(it) for _ in range(ATTN_PAGES)]
    new_ref, bt_ref = next(it), next(it)
    pen_ref = next(it) if pen_block else None
    gt_ref = next(it) if gated else None
    o_ref, m_ref, l_ref, acc_ref = next(it), next(it), next(it), next(it)
    step = pl.program_id(1)

    @pl.when(step == 0)
    def _():
        m_ref[...] = jnp.full(m_ref.shape, NEG_INF, jnp.float32)
        l_ref[...] = jnp.zeros(l_ref.shape, jnp.float32)
        acc_ref[...] = jnp.zeros(acc_ref.shape, jnp.float32)

    def attend(g, k, v, bias, page, n_keys):
        s = _dot(_bf(k), _bf(q_ref[g] * ATTN_SCALE), NT_DIMS) + bias
        if pen_block:
            if pen_block >= PAGE:
                s = s + pen_ref[g, pl.ds(page // (pen_block // PAGE), 1), :]
            else:
                per_page = PAGE // pen_block
                pieces = [s[a * pen_block:min((a + 1) * pen_block, n_keys)]
                          + pen_ref[g, pl.ds(page * per_page + a, 1), :]
                          for a in range(-(-n_keys // pen_block))]
                s = pieces[0] if len(pieces) == 1 else jnp.concatenate(pieces, axis=0)
        m_prev = m_ref[g]
        m_new = jnp.maximum(m_prev, jnp.max(s, axis=0, keepdims=True))
        alpha = jnp.exp(m_prev - m_new)
        p = jnp.exp(s - m_new)
        l_ref[g] = alpha * l_ref[g] + jnp.sum(p, axis=0, keepdims=True)
        acc_ref[g] = _rows_to_col(alpha, ROWS) * acc_ref[g] + _dot(_bf(p), _bf(v), TN_DIMS)
        m_ref[g] = m_new

    half = KV_HEADS * HEAD_DIM
    for kk in range(ATTN_PAGES):
        page = step * ATTN_PAGES + kk
        tile = tid_ref[page]
        for g in range(KV_HEADS):
            attend(g, pages[kk][:, g * HEAD_DIM:(g + 1) * HEAD_DIM],
                   pages[kk][:, half + g * HEAD_DIM:half + (g + 1) * HEAD_DIM], bt_ref[g, tile], page, PAGE)

    @pl.when(step == pl.num_programs(1) - 1)
    def _():
        n_new = new_ref.shape[0]
        tile = tid_ref[n_pages]
        for g in range(KV_HEADS):
            attend(g, new_ref[:, g * HEAD_DIM:(g + 1) * HEAD_DIM],
                   new_ref[:, half + g * HEAD_DIM:half + (g + 1) * HEAD_DIM], bt_ref[g, tile, 0:n_new, :],
                   n_pages, n_new)
            o = acc_ref[g] / _rows_to_col(jnp.maximum(l_ref[g], 1e-30), ROWS)
            if gated:
                o = o * jax.nn.sigmoid(gt_ref[g])
            o_ref[g] = o


def paged_attention(qs, pool, page_table, tile_ids, new_kv, bias_tiles, *, pen=None, pen_block=0, gate=None):
    batch, n_pages = page_table.shape
    steps = n_pages // ATTN_PAGES
    n_new = new_kv.shape[1]

    def page_spec(k):
        return pl.BlockSpec((None, PAGE, KV_WIDTH), lambda b, s, pt, tid: (pt[b, s * ATTN_PAGES + k], 0, 0))

    grp = lambda b, s, pt, tid: (b, 0, 0, 0)
    in_specs = ([pl.BlockSpec((None, KV_HEADS, ROWS, HEAD_DIM), grp)] + [page_spec(k) for k in range(ATTN_PAGES)]
                + [pl.BlockSpec((None, n_new, KV_WIDTH), lambda b, s, pt, tid: (b, 0, 0)),
                   pl.BlockSpec(bias_tiles.shape, lambda b, s, pt, tid: (0, 0, 0, 0))])
    args = [qs] + [pool] * ATTN_PAGES + [new_kv, bias_tiles]
    if pen is not None:
        in_specs.append(pl.BlockSpec((None,) + pen.shape[1:], grp))
        args.append(pen)
    if gate is not None:
        in_specs.append(pl.BlockSpec((None, KV_HEADS, ROWS, HEAD_DIM), grp))
        args.append(gate)
    return pl.pallas_call(
        functools.partial(_paged_attn_body, pen_block=pen_block if pen is not None else 0, gated=gate is not None,
                          n_pages=n_pages),
        grid_spec=pltpu.PrefetchScalarGridSpec(
            num_scalar_prefetch=2, grid=(batch, steps), in_specs=in_specs,
            out_specs=pl.BlockSpec((None, KV_HEADS, ROWS, HEAD_DIM), grp),
            scratch_shapes=[pltpu.VMEM((KV_HEADS, 1, ROWS), jnp.float32), pltpu.VMEM((KV_HEADS, 1, ROWS), jnp.float32),
                            pltpu.VMEM((KV_HEADS, ROWS, HEAD_DIM), jnp.float32)]),
        out_shape=jax.ShapeDtypeStruct((batch, KV_HEADS, ROWS, HEAD_DIM), jnp.float32),
        compiler_params=_cparams("parallel", "arbitrary"),
        name="paged_attention",
    )(page_table, tile_ids, *args)


def _moba_sample_gate_body(pt_ref, q_ref, *refs, n_blocks, past):
    pages, (pen_ref, km_ref) = refs[:ATTN_PAGES], refs[ATTN_PAGES:]
    step = pl.program_id(1)
    half = KV_HEADS * HEAD_DIM
    per_block = MOBA_BLOCK // PAGE

    @pl.when(step == 0)
    def _():
        km_ref[...] = jnp.zeros(km_ref.shape, jnp.float32)

    for kk in range(ATTN_PAGES):
        blk = (step * ATTN_PAGES + kk) // per_block
        km_ref[pl.ds(blk, 1), :] += jnp.sum(pages[kk][:, 0:half], axis=0, keepdims=True)

    @pl.when(step == pl.num_programs(1) - 1)
    def _():
        nbp = pen_ref.shape[1]
        blk = lax.broadcasted_iota(jnp.int32, (nbp, ROWS), 0)
        n_tok = ROWS // GROUP
        tok = past + (lax.broadcasted_iota(jnp.int32, (nbp, ROWS), 1) & (n_tok - 1))
        own = lax.shift_right_logical(tok, int(math.log2(MOBA_BLOCK)))
        for g in range(KV_HEADS):
            kh, kl = _split_bf16(km_ref[:, g * HEAD_DIM:(g + 1) * HEAD_DIM] / MOBA_BLOCK, 2)
            qh, ql = _split_bf16(q_ref[g], 2)
            gate = _dot(kh, qh, NT_DIMS) + _dot(kh, ql, NT_DIMS) + _dot(kl, qh, NT_DIMS)
            gate = jnp.where(blk < own, gate, NEG_INF)
            chosen = ((_rank_rows(gate, n_blocks) < MOBA_TOPK) & (blk < own)) | (blk == own)
            pen_ref[g] = jnp.where(chosen, 0.0, NEG_INF)


def moba_sample_gate(qs, pool, page_table, past, n_new):
    batch, n_pages = page_table.shape
    steps = n_pages // ATTN_PAGES
    n_blocks = -(-(past + n_new) // MOBA_BLOCK)
    nbp = -(-n_blocks // SUBLANE) * SUBLANE

    def page_spec(k):
        return pl.BlockSpec((None, PAGE, KV_WIDTH), lambda b, s, pt: (pt[b, s * ATTN_PAGES + k], 0, 0))

    return pl.pallas_call(
        functools.partial(_moba_sample_gate_body, n_blocks=n_blocks, past=past),
        grid_spec=pltpu.PrefetchScalarGridSpec(
            num_scalar_prefetch=1, grid=(batch, steps),
            in_specs=[pl.BlockSpec((None, KV_HEADS, ROWS, HEAD_DIM), lambda b, s, pt: (b, 0, 0, 0))]
            + [page_spec(k) for k in range(ATTN_PAGES)],
            out_specs=pl.BlockSpec((None, KV_HEADS, nbp, ROWS), lambda b, s, pt: (b, 0, 0, 0)),
            scratch_shapes=[pltpu.VMEM((nbp, KV_HEADS * HEAD_DIM), jnp.float32)]),
        out_shape=jax.ShapeDtypeStruct((batch, KV_HEADS, nbp, ROWS), jnp.float32),
        compiler_params=_cparams("parallel", "arbitrary"),
        name="moba_sample_gate",
    )(page_table, qs, *([pool] * ATTN_PAGES))


def _sample_rows(x, batch, n_tok):
    return x.reshape(batch, n_tok, KV_HEADS, GROUP, HEAD_DIM).transpose(0, 2, 3, 1, 4).reshape(
        batch, KV_HEADS, GROUP * n_tok, HEAD_DIM)


def _sample_unrows(o, batch, n_tok):
    return o.reshape(batch, KV_HEADS, GROUP, n_tok, HEAD_DIM).transpose(0, 3, 1, 2, 4).reshape(batch * n_tok, Q_WIDTH)


def _sample_gate_rows(tail, branch, batch, n_tok):
    gt = tail[:, branch * N_HEADS:(branch + 1) * N_HEADS].reshape(batch, n_tok, KV_HEADS, GROUP)
    gt = gt.transpose(0, 2, 3, 1).reshape(batch, KV_HEADS, GROUP * n_tok, 1)
    return jnp.broadcast_to(gt, (batch, KV_HEADS, GROUP * n_tok, HEAD_DIM))


def _pad_new(kv_new, batch, n_tok):
    return jnp.pad(kv_new.reshape(batch, n_tok, KV_WIDTH), ((0, 0), (0, SUBLANE - n_tok), (0, 0)))


def _nsa_sample_pallas(main, tail, cache_c, cache_s, cache_w, page_table, cmp_w1, cmp_w2, cmp_pe, rel_table):
    batch, n_pages = page_table.shape
    n_tok = main.shape[0] // batch
    past = n_pages * PAGE
    wbuf = cache_w.shape[1]
    assert n_tok == ROWS // GROUP and cache_c.shape[1] == PAGE and wbuf == WINDOW and WINDOW % PAGE == 0
    n_cmp = (past + n_tok - CMP_BLOCK) // CMP_STRIDE + 1
    assert n_cmp + CMP_BLOCK // CMP_STRIDE - 1 == past // CMP_STRIDE
    qs = _sample_rows(main[:, :Q_WIDTH], batch, n_tok)
    kv_new = [main[:, Q_WIDTH + c * KV_WIDTH:Q_WIDTH + (c + 1) * KV_WIDTH] for c in range(3)]
    tiles = _sample_bias_tiles(rel_table, past, n_tok)
    flat = lambda pool: pool.reshape(pool.shape[0], PAGE, KV_WIDTH)

    kc = compress_finish(compress_part_paged(flat(cache_c), page_table, cmp_w1), cmp_w1, cmp_w2, cmp_pe)
    n = kc.shape[3]
    dist = past + np.arange(n_tok)[None, :] - (np.arange(n)[:, None] * CMP_STRIDE + CMP_BLOCK - 1)
    ok = (dist >= 0) & (np.arange(n)[:, None] < n_cmp)
    cb = jnp.where(ok[..., None], rel_table[_bucket_np(dist)], NEG_INF)
    cb = cb.reshape(n, n_tok, KV_HEADS, GROUP).transpose(2, 0, 3, 1).reshape(KV_HEADS, n, GROUP * n_tok)
    n_sel_blocks = -(-(past + n_tok) // SEL_BLOCK)
    o_cmp, pen = sample_cmp_select(qs, kc, cb, _sample_gate_rows(tail, 0, batch, n_tok), past, n_sel_blocks)

    far_then_last = jnp.asarray([0] * (n_pages - 1) + [2, 3], jnp.int32)
    o_sel = paged_attention(qs, flat(cache_s), page_table, far_then_last, _pad_new(kv_new[1], batch, n_tok), tiles,
                            pen=pen, pen_block=SEL_BLOCK, gate=_sample_gate_rows(tail, 1, batch, n_tok))
    w_pages = wbuf // PAGE
    win_table = jnp.arange(batch * w_pages, dtype=jnp.int32).reshape(batch, w_pages)
    win_tiles = jnp.asarray([1] + [0] * (w_pages - 2) + [2, 3], jnp.int32)
    o_win = paged_attention(qs, cache_w.reshape(batch * w_pages, PAGE, KV_WIDTH), win_table, win_tiles,
                            _pad_new(kv_new[2], batch, n_tok), tiles, gate=_sample_gate_rows(tail, 2, batch, n_tok))
    outs = [_sample_unrows(o, batch, n_tok) for o in (o_cmp, o_sel, o_win)]
    shape = (batch, n_tok, 2, KV_HEADS, HEAD_DIM)
    new_win = jnp.concatenate([cache_w[:, n_tok:], kv_new[2].reshape(shape)], axis=1)
    return outs, kv_new[0].reshape(shape), kv_new[1].reshape(shape), new_win


def _moba_sample_pallas(proj, cache_kv, page_table, rel_table):
    batch, n_pages = page_table.shape
    n_tok = proj.shape[0] // batch
    past = n_pages * PAGE
    assert n_tok == ROWS // GROUP and (past // MOBA_BLOCK) * MOBA_BLOCK == past
    qs = _sample_rows(proj[:, :Q_WIDTH], batch, n_tok)
    kv_new = proj[:, Q_WIDTH:]
    pool = cache_kv.reshape(cache_kv.shape[0], PAGE, KV_WIDTH)
    pen = moba_sample_gate(qs, pool, page_table, past, n_tok)
    tile_ids = jnp.asarray([0] * (n_pages - 1) + [2, 3], jnp.int32)
    o = paged_attention(qs, pool, page_table, tile_ids, _pad_new(kv_new, batch, n_tok),
                        _sample_bias_tiles(rel_table, past, n_tok), pen=pen, pen_block=MOBA_BLOCK)
    return _sample_unrows(o, batch, n_tok), kv_new.reshape(batch, n_tok, 2, KV_HEADS, HEAD_DIM)


def _rms_norm(x, g):
    xf = x.astype(jnp.float32)
    y = xf * lax.rsqrt(jnp.mean(xf * xf, axis=-1, keepdims=True) + NORM_EPS)
    return (y * g.astype(jnp.float32)).astype(x.dtype)


def _l2norm(x):
    return x * lax.rsqrt(jnp.sum(x * x, axis=-1, keepdims=True) + NORM_EPS)


def _masked_softmax(logits, mask):
    logits = jnp.where(mask, logits, NEG_INF)
    m = jnp.max(logits, axis=-1, keepdims=True)
    e = jnp.where(mask, jnp.exp(logits - m), 0.0)
    return e / jnp.maximum(jnp.sum(e, axis=-1, keepdims=True), 1e-30)


def _t5_bucket(dist):
    exact = REL_BUCKETS // 2
    d = jnp.maximum(dist, 0)
    ratio = jnp.log(jnp.maximum(d, 1).astype(jnp.float32) / exact) / math.log(REL_MAX_DIST / exact)
    large = jnp.minimum(exact + (ratio * (REL_BUCKETS - exact)).astype(jnp.int32), REL_BUCKETS - 1)
    return jnp.where(d < exact, d, large)


def _rel_bias(rel_table, dist):
    return rel_table.astype(jnp.float32)[_t5_bucket(dist)]


def _gather_pages(pool, page_table):
    rows = pool[page_table]
    return rows.reshape(page_table.shape[0], page_table.shape[1] * pool.shape[1], *pool.shape[2:])


def _causal_conv(x, buf, w):
    T = x.shape[1]
    xp = jnp.concatenate([buf.astype(x.dtype), x], axis=1)
    y = xp[:, 0:T] * w[0]
    for i in range(1, GDN_CONV):
        y = y + xp[:, i:i + T] * w[i]
    return jax.nn.silu(y), xp[:, T:]


def _nsa_split(proj):
    B, T, _ = proj.shape
    q = proj[..., :Q_WIDTH].reshape(B, T, KV_HEADS, GROUP, HEAD_DIM)
    kv_c, kv_s, kv_w = (proj[..., Q_WIDTH + c * KV_WIDTH:Q_WIDTH + (c + 1) * KV_WIDTH]
                        .reshape(B, T, 2, KV_HEADS, HEAD_DIM) for c in range(3))
    gates = jax.nn.sigmoid(proj[..., NSA_MAIN:NSA_MAIN + 3 * N_HEADS]).reshape(B, T, 3, KV_HEADS, GROUP)
    return q, kv_c, kv_s, kv_w, gates


def _nsa_compress(kv, cmp_w1, cmp_w2, cmp_pe):
    B, Tk = kv.shape[:2]
    nc = (Tk - CMP_BLOCK) // CMP_STRIDE + 1
    n_part = CMP_BLOCK // CMP_STRIDE
    n_chunk = nc + n_part - 1
    chunks = kv[:, :n_chunk * CMP_STRIDE].astype(jnp.float32).reshape(B, n_chunk, CMP_STRIDE, 2, KV_HEADS, HEAD_DIM)
    w1 = cmp_w1.astype(jnp.float32)
    part = jnp.einsum('bnlcgd,crldh->bncrgh', chunks, w1.reshape(2, n_part, CMP_STRIDE, HEAD_DIM, CMP_HIDDEN))
    hid = jnp.einsum('cld,cldh->ch', cmp_pe.astype(jnp.float32), w1)[None, None, :, None, :]
    for r in range(n_part):
        hid = hid + part[:, r:r + nc, :, r]
    return jnp.einsum('bncgh,chd->bncgd', jax.nn.gelu(hid), cmp_w2.astype(jnp.float32))


def _nsa_cmp_attn(q, q_pos, kc, rel_table):
    nc = kc.shape[1]
    end_pos = jnp.arange(nc) * CMP_STRIDE + (CMP_BLOCK - 1)
    dist = q_pos[:, None] - end_pos[None, :]
    bias = _rel_bias(rel_table, dist).reshape(q_pos.shape[0], nc, KV_HEADS, GROUP).transpose(0, 2, 3, 1)
    logits = jnp.einsum('bqgrd,bcgd->bqgrc', q, kc[:, :, 0]) * ATTN_SCALE + bias
    p = _masked_softmax(logits, (dist >= 0)[:, None, None, :])
    return jnp.einsum('bqgrc,bcgd->bqgrd', p, kc[:, :, 1]), p


def _nsa_select(p_cmp, q_pos, tk):
    ns = -(-tk // SEL_BLOCK)
    ratio = SEL_BLOCK // CMP_STRIDE
    imp = p_cmp.sum(axis=3)
    nc = imp.shape[-1]
    imp = jnp.pad(imp, ((0, 0), (0, 0), (0, 0), (1, ratio * ns + ratio - 1 - nc)))
    score = imp[..., :ratio * ns].reshape(*imp.shape[:3], ns, ratio).sum(-1) + imp[..., ratio::ratio]
    cur = q_pos // SEL_BLOCK
    blk = jnp.arange(ns)
    forced = (blk[None, :] == 0) | (blk[None, :] == cur[:, None]) | (blk[None, :] == cur[:, None] - 1)
    causal = blk[None, :] <= cur[:, None]
    score = jnp.where(forced[None, :, None, :], FORCE_SCORE, score)
    score = jnp.where(causal[None, :, None, :], score, -1.0)
    _, idx = lax.top_k(score, min(N_SEL, ns))
    valid = idx <= cur[None, :, None, None]
    return idx, valid


def _block_mask(idx, valid, n_blocks, block, tk):
    hit = (idx[..., None] == jnp.arange(n_blocks)) & valid[..., None]
    return jnp.repeat(jnp.any(hit, axis=-2), block, axis=-1)[..., :tk]


def _dense_attn(q, q_pos, kv, key_ok, rel_table):
    tk = kv.shape[1]
    dist = q_pos[:, None] - jnp.arange(tk)[None, :]
    bias = _rel_bias(rel_table, dist).reshape(q_pos.shape[0], tk, KV_HEADS, GROUP).transpose(0, 2, 3, 1)
    logits = jnp.einsum('bqgrd,bkgd->bqgrk', q, kv[:, :, 0]) * ATTN_SCALE + bias
    p = _masked_softmax(logits, key_ok & (dist >= 0)[None, :, None, None, :])
    return jnp.einsum('bqgrk,bkgd->bqgrd', p, kv[:, :, 1])


def _band_attn(q, q_pos, kv_band, k_pos, rel_table):
    dist = q_pos[:, :, None] - k_pos[:, None, :]
    mask = (dist >= 0) & (dist < WINDOW) & (k_pos[:, None, :] >= 0)
    n, qb, kb = dist.shape
    bias = _rel_bias(rel_table, dist).reshape(n, qb, kb, KV_HEADS, GROUP).transpose(0, 1, 3, 4, 2)
    kvf = kv_band.astype(jnp.float32)
    logits = jnp.einsum('bnqgrd,bnkgd->bnqgrk', q, kvf[:, :, :, 0]) * ATTN_SCALE + bias
    p = _masked_softmax(logits, mask[:, :, None, None, :])
    return jnp.einsum('bnqgrk,bnkgd->bnqgrd', p, kvf[:, :, :, 1])


def _nsa_sample(proj, cache_c, cache_s, cache_w, page_table, cmp_w1, cmp_w2, cmp_pe, rel_table):
    B, T, _ = proj.shape
    past = page_table.shape[1] * cache_c.shape[1]
    q, kv_c, kv_s, kv_w, gates = _nsa_split(proj)
    pos = past + jnp.arange(T)
    kv_c_full = jnp.concatenate([_gather_pages(cache_c, page_table), kv_c], axis=1)
    kv_s_full = jnp.concatenate([_gather_pages(cache_s, page_table), kv_s], axis=1)
    wbuf = cache_w.shape[1]
    band = jnp.concatenate([cache_w, kv_w], axis=1)
    band_k_pos = past - wbuf + jnp.arange(wbuf + T)
    kc = _nsa_compress(kv_c_full, cmp_w1, cmp_w2, cmp_pe)
    o_cmp, p_cmp = _nsa_cmp_attn(q, pos, kc, rel_table)
    tk = kv_s_full.shape[1]
    idx, valid = _nsa_select(p_cmp, pos, tk)
    key_ok = _block_mask(idx, valid, -(-tk // SEL_BLOCK), SEL_BLOCK, tk)[:, :, :, None, :]
    o_sel = _dense_attn(q, pos, kv_s_full, key_ok, rel_table)
    o_win = _band_attn(q[:, None], pos[None, :], band[:, None], band_k_pos[None, :], rel_table)[:, 0]
    o = gates[:, :, 0, ..., None] * o_cmp + gates[:, :, 1, ..., None] * o_sel + gates[:, :, 2, ..., None] * o_win
    return o.reshape(B, T, Q_WIDTH), kv_c, kv_s, band[:, -wbuf:]


def _moba_sample(proj, cache_kv, page_table, rel_table):
    B, T, _ = proj.shape
    past = page_table.shape[1] * cache_kv.shape[1]
    q = proj[..., :Q_WIDTH].reshape(B, T, KV_HEADS, GROUP, HEAD_DIM)
    kv_new = proj[..., Q_WIDTH:].reshape(B, T, 2, KV_HEADS, HEAD_DIM)
    kv = jnp.concatenate([_gather_pages(cache_kv, page_table), kv_new], axis=1)
    q_pos = past + jnp.arange(T)
    tk = kv.shape[1]
    nb = -(-tk // MOBA_BLOCK)
    kpad = jnp.pad(kv[:, :, 0], ((0, 0), (0, nb * MOBA_BLOCK - tk), (0, 0), (0, 0)))
    kmean = jnp.mean(kpad.reshape(B, nb, MOBA_BLOCK, KV_HEADS, HEAD_DIM), axis=2)
    own = q_pos // MOBA_BLOCK
    gate = jnp.einsum('bqgrd,bngd->bqgrn', q, kmean)
    is_past = jnp.arange(nb)[None, :] < own[:, None]
    gate = jnp.where(is_past[:, None, None, :], gate, NEG_INF)
    _, idx = lax.top_k(gate, min(MOBA_TOPK, nb))
    valid = idx < own[:, None, None, None]
    own_ok = (jnp.arange(tk)[None, :] // MOBA_BLOCK) == own[:, None]
    key_ok = _block_mask(idx, valid, nb, MOBA_BLOCK, tk) | own_ok[None, :, None, None, :]
    o = _dense_attn(q, q_pos, kv, key_ok, rel_table)
    return o.reshape(B, T, Q_WIDTH), kv_new


def _gla_chunked(q, k, v, logf, s0):
    B, T, H, dk = q.shape
    dv = v.shape[-1]
    C = math.gcd(T, CHUNK)
    N = T // C
    q, k, v, logf = (a.reshape(B, N, C, *a.shape[2:]) for a in (q, k, v, logf))
    b = jnp.cumsum(logf, axis=2)
    b_ref = b[:, :, C // 2:C // 2 + 1]
    a = jnp.einsum('bnihd,bnjhd->bnhij', q * jnp.exp(b - b_ref), k * jnp.exp(b_ref - b))
    causal = jnp.arange(C)[:, None] >= jnp.arange(C)[None, :]
    a = jnp.where(causal, a, 0.0)
    o_intra = jnp.einsum('bnhij,bnjhv->bnihv', a, v)
    q_in = q * jnp.exp(b)
    k_out = k * jnp.exp(b[:, :, -1:] - b)
    d_last = jnp.exp(b[:, :, -1])

    def step(s, xs):
        q_c, k_c, v_c, d_c = xs
        o = jnp.einsum('bihd,bhdv->bihv', q_c, s)
        s = d_c[..., None] * s + jnp.einsum('bjhd,bjhv->bhdv', k_c, v_c)
        return s, o

    s, o_inter = lax.scan(step, s0, tuple(jnp.moveaxis(t, 1, 0) for t in (q_in, k_out, v, d_last)))
    o = o_intra + jnp.moveaxis(o_inter, 0, 1)
    return o.reshape(B, T, H, dv), s


def _hgrn2_core(proj, s0, lb_logits, layer, norm_g):
    B, T, _ = proj.shape
    dk = HG_HEADS * HG_DK
    dv = HG_HEADS * HG_DV
    q, f, i, g = jnp.split(proj, [dk, 2 * dk, 2 * dk + dv], axis=-1)
    p = jax.nn.softmax(lb_logits.astype(jnp.float32), axis=0)
    lb = (jnp.cumsum(p, axis=0) - p[0])[layer]
    fg = lb + (1.0 - lb) * jax.nn.sigmoid(f)
    shp = (B, T, HG_HEADS, HG_DK)
    o, s = _gla_chunked((jax.nn.silu(q) * HG_DK ** -0.5).reshape(shp), (1.0 - fg).reshape(shp),
                        i.reshape(B, T, HG_HEADS, HG_DV), jnp.log(fg).reshape(shp), s0.astype(jnp.float32))
    o = _rms_norm(o, norm_g) * jax.nn.silu(g.reshape(B, T, HG_HEADS, HG_DV))
    return o.reshape(B, T, dv), s


def _gdn_chunked(q, k, v, log_a, beta, s0):
    B, T, H, dk = q.shape
    dv = v.shape[-1]
    C = math.gcd(T, CHUNK)
    N = T // C

    def heads_first(a):
        return jnp.moveaxis(a.reshape(B, N, C, *a.shape[2:]), 3, 2)

    qh, kh, vh, bt = heads_first(q), heads_first(k), heads_first(v), heads_first(beta)
    g = jnp.cumsum(heads_first(log_a), axis=-1)
    ar = jnp.arange(C)
    strict = ar[:, None] > ar[None, :]
    incl = ar[:, None] >= ar[None, :]
    gdiff = g[..., :, None] - g[..., None, :]
    d_strict = jnp.where(strict, jnp.exp(jnp.where(strict, gdiff, 0.0)), 0.0)
    d_incl = jnp.where(incl, jnp.exp(jnp.where(incl, gdiff, 0.0)), 0.0)
    kk = jnp.einsum('bnhid,bnhjd->bnhij', kh, kh)
    m = jnp.eye(C, dtype=jnp.float32) + bt[..., :, None] * kk * d_strict
    rhs = jnp.concatenate([bt[..., None] * vh, (bt * jnp.exp(g))[..., None] * kh], axis=-1)
    sol = lax.linalg.triangular_solve(m, rhs, left_side=True, lower=True, unit_diagonal=True)
    u0, w = sol[..., :dv], sol[..., dv:]
    aq = jnp.einsum('bnhid,bnhjd->bnhij', qh, kh) * d_incl
    q_in = qh * jnp.exp(g)[..., None]
    k_out = kh * jnp.exp(g[..., -1:] - g)[..., None]
    d_last = jnp.exp(g[..., -1])

    def step(s, xs):
        u0_c, w_c, aq_c, q_c, k_c, d_c = xs
        u = u0_c - jnp.einsum('bhcd,bhdv->bhcv', w_c, s)
        o = jnp.einsum('bhcd,bhdv->bhcv', q_c, s) + jnp.einsum('bhij,bhjv->bhiv', aq_c, u)
        s = d_c[..., None, None] * s + jnp.einsum('bhcd,bhcv->bhdv', k_c, u)
        return s, o

    xs = tuple(jnp.moveaxis(a, 1, 0) for a in (u0, w, aq, q_in, k_out, d_last))
    s, o = lax.scan(step, s0, xs)
    return jnp.transpose(o, (1, 0, 3, 2, 4)).reshape(B, T, H, dv), s


def _gdn_core(proj, conv_buf, s0, conv_w, a_log, dt_bias, norm_g):
    B, T, _ = proj.shape
    vw = GDN_V_HEADS * GDN_DV
    qkv = proj[..., :GDN_CONV_DIM]
    z = proj[..., GDN_CONV_DIM:GDN_MAIN]
    b_logit = proj[..., GDN_MAIN:GDN_MAIN + GDN_V_HEADS]
    a_in = proj[..., GDN_MAIN + GDN_V_HEADS:GDN_MAIN + 2 * GDN_V_HEADS]
    conv_out, new_buf = _causal_conv(qkv, conv_buf, conv_w)
    qkw = GDN_QK_HEADS * GDN_DK
    q = jnp.repeat(_l2norm(conv_out[..., :qkw].reshape(B, T, GDN_QK_HEADS, GDN_DK)) * GDN_DK ** -0.5, GDN_REP, axis=2)
    k = jnp.repeat(_l2norm(conv_out[..., qkw:2 * qkw].reshape(B, T, GDN_QK_HEADS, GDN_DK)), GDN_REP, axis=2)
    v = conv_out[..., 2 * qkw:].reshape(B, T, GDN_V_HEADS, GDN_DV)
    beta = jax.nn.sigmoid(b_logit)
    log_a = -jnp.exp(a_log.astype(jnp.float32)) * jax.nn.softplus(a_in + dt_bias.astype(jnp.float32))
    o, s = _gdn_chunked(q, k, v, log_a, beta, s0.astype(jnp.float32))
    o = _rms_norm(o, norm_g) * jax.nn.silu(z.reshape(B, T, GDN_V_HEADS, GDN_DV))
    return o.reshape(B, T, vw), new_buf, s


def _pad_cols(w, mult=LANE):
    n = w.shape[1]
    return jnp.pad(w, ((0, 0), (0, (-n) % mult)))


def _nsa_prompt(main, tail, batch, seq, cmp_w1, cmp_w2, cmp_pe, rel_table):
    kv = main[:, Q_WIDTH:].reshape(batch, seq, 3, 2, KV_HEADS, HEAD_DIM)
    kv_c, kv_s, kv_w = kv[:, :, 0], kv[:, :, 1], kv[:, :, 2]
    kc = compress_finish(compress_part_rows(main, Q_WIDTH // KV_WIDTH, cmp_w1, batch, seq), cmp_w1, cmp_w2, cmp_pe)
    o_cmp, pen = cmp_select(main, tail, kc, _cmp_bias_table(rel_table, kc.shape[3]), batch, seq)
    col = Q_WIDTH // HEAD_DIM
    o_sel = flash_attention(main, main, col + 2 * KV_HEADS, col + 3 * KV_HEADS, _flash_bias_tiles(rel_table, 0),
                            batch, seq, pen=pen, pen_block=SEL_BLOCK, gate_arr=tail, gate_col0=N_HEADS)
    o_win = flash_attention(main, main, col + 4 * KV_HEADS, col + 5 * KV_HEADS,
                            _flash_bias_tiles(rel_table, WINDOW), batch, seq, k_back=WINDOW // ATTN_TILE,
                            gate_arr=tail, gate_col0=2 * N_HEADS)
    return [o_cmp, o_sel, o_win], kv_c, kv_s, kv_w[:, -min(WINDOW, seq):]


def _moba_prompt(proj, batch, seq, rel_table):
    col = Q_WIDTH // HEAD_DIM
    pen = moba_gate(proj, col, batch, seq)
    o = flash_attention(proj, proj, col, col + KV_HEADS, _flash_bias_tiles(rel_table, 0), batch, seq,
                        pen=pen, pen_block=MOBA_BLOCK)
    return o, proj[:, Q_WIDTH:].reshape(batch, seq, 2, KV_HEADS, HEAD_DIM)


def kernel(x_prompt, x_sample, cache_nsa_cmp_kv, cache_nsa_sel_kv, cache_nsa_win_kv, cache_moba_kv,
           state_hgrn2, state_gdn_conv, state_gdn_ssm, page_table, rel_table, ln_mix, ln_ffn, ln_final,
           ffn_w_up, ffn_w_down, nsa_w_in, nsa_cmp_w1, nsa_cmp_w2, nsa_cmp_pe, nsa_w_out, moba_w_in, moba_w_out,
           hg_w_in, hg_lb_logits, hg_norm, hg_w_out, gdn_w_in, gdn_conv_w, gdn_a_log, gdn_dt_bias, gdn_norm,
           gdn_w_out):
    bf = jnp.bfloat16
    bp, tp = x_prompt.shape[:2]
    bs, ts = x_sample.shape[:2]
    assert tp % ATTN_TILE == 0 and WINDOW % ATTN_TILE == 0 and ATTN_TILE == MOBA_BLOCK and tp % SCAN_TILE == 0
    xp = x_prompt.reshape(bp * tp, D_MODEL)
    xs = x_sample.reshape(bs * ts, D_MODEL)

    for layer in range(DEPTH):
        kind = layer % N_MIXERS
        g_mix = ln_mix[layer]
        if kind == 0:
            w_main, w_tail = nsa_w_in[:, :NSA_MAIN].astype(bf), _pad_cols(nsa_w_in[:, NSA_MAIN:]).astype(bf)
            main_p, tail_p = norm_matmul(xp, g_mix, w_main), norm_matmul(xp, g_mix, w_tail)
            main_s, tail_s = norm_matmul(xs, g_mix, w_main), norm_matmul(xs, g_mix, w_tail)
            op, nsa_cmp_p, nsa_sel_p, nsa_win_p = _nsa_prompt(main_p, tail_p, bp, tp, nsa_cmp_w1, nsa_cmp_w2,
                                                              nsa_cmp_pe, rel_table)
            os_, nsa_cmp_s, nsa_sel_s, nsa_win_s = _nsa_sample_pallas(main_s, tail_s, cache_nsa_cmp_kv,
                                                                     cache_nsa_sel_kv, cache_nsa_win_kv, page_table,
                                                                     nsa_cmp_w1, nsa_cmp_w2, nsa_cmp_pe, rel_table)
            w_out = nsa_w_out.astype(bf)
        elif kind == 1:
            w_in = moba_w_in.astype(bf)
            pp = norm_matmul(xp, g_mix, w_in)
            op, moba_p = _moba_prompt(pp, bp, tp, rel_table)
            os_, moba_s = _moba_sample_pallas(norm_matmul(xs, g_mix, w_in), cache_moba_kv, page_table, rel_table)
            op, os_ = [op], [os_]
            w_out = moba_w_out.astype(bf)
        elif kind == 2:
            w_in = hg_w_in.astype(bf)
            pp = norm_matmul(xp, g_mix, w_in)
            ps = norm_matmul(xs, g_mix, w_in).reshape(bs, ts, -1)
            s0 = jnp.zeros((bp, HG_HEADS, HG_DK, HG_DV), jnp.float32)
            op, hg_p = hgrn2_scan(pp, hg_lb_logits, hg_norm, s0, layer, bp, tp)
            os_, hg_s = _hgrn2_core(ps, state_hgrn2, hg_lb_logits, layer, hg_norm)
            op, os_ = [op], [os_.reshape(bs * ts, -1)]
            w_out = hg_w_out.astype(bf)
        else:
            w_main, w_tail = gdn_w_in[:, :GDN_MAIN].astype(bf), _pad_cols(gdn_w_in[:, GDN_MAIN:]).astype(bf)
            main_p, tail_p = norm_matmul(xp, g_mix, w_main), norm_matmul(xp, g_mix, w_tail)
            ps = jnp.concatenate([norm_matmul(xs, g_mix, w_main), norm_matmul(xs, g_mix, w_tail)], axis=-1)
            buf0 = jnp.zeros((bp, GDN_CONV - 1, GDN_CONV_DIM), jnp.float32)
            s0 = jnp.zeros((bp, GDN_V_HEADS, GDN_DK, GDN_DV), jnp.float32)
            op, ssm_p = gdn_scan(main_p, tail_p, buf0, gdn_conv_w, gdn_a_log, gdn_dt_bias, gdn_norm, s0, bp, tp)
            conv_p = main_p.reshape(bp, tp, -1)[:, tp - (GDN_CONV - 1):, :GDN_CONV_DIM]
            os_, conv_s, ssm_s = _gdn_core(ps.reshape(bs, ts, -1), state_gdn_conv, state_gdn_ssm, gdn_conv_w,
                                           gdn_a_log, gdn_dt_bias, gdn_norm)
            op, os_ = [op], [os_.reshape(bs * ts, -1)]
            w_out = gdn_w_out.astype(bf)
        xp = matmul_res(op, w_out, xp)
        xs = matmul_res(os_, w_out, xs)
        w_up, w_down = ffn_w_up[layer].astype(bf), ffn_w_down[layer].astype(bf)
        xp = ffn(xp, ln_ffn[layer], w_up, w_down)
        xs = ffn(xs, ln_ffn[layer], w_up, w_down)
    y_prompt = final_norm(xp, ln_final).reshape(bp, tp, D_MODEL)
    y_sample = final_norm(xs, ln_final).reshape(bs, ts, D_MODEL)
    return (y_prompt, y_sample, nsa_cmp_p, nsa_cmp_s, nsa_sel_p, nsa_sel_s, nsa_win_p, nsa_win_s,
            moba_p, moba_s, hg_p, hg_s, conv_p, conv_s, ssm_p, ssm_s)
```

```python
import functools
import math

import jax
import jax.numpy as jnp
import numpy as np
from jax import lax
from jax.experimental import pallas as pl
from jax.experimental.pallas import tpu as pltpu

D_MODEL = 2048
DEPTH = 4
N_MIXERS = 4
HEAD_DIM = 128
N_HEADS = D_MODEL // HEAD_DIM
KV_HEADS = 4
GROUP = N_HEADS // KV_HEADS
ATTN_SCALE = HEAD_DIM ** -0.5
REL_BUCKETS = 32
REL_MAX_DIST = 128
CMP_BLOCK = 32
CMP_STRIDE = 16
CMP_HIDDEN = HEAD_DIM
SEL_BLOCK = 64
N_SEL = 16
WINDOW = 512
FORCE_SCORE = 1.0e4
MOBA_BLOCK = 256
MOBA_TOPK = 3
HG_DK = 128
HG_HEADS = D_MODEL // HG_DK
HG_DV = D_MODEL // HG_HEADS
GDN_DK = 128
GDN_DV = 128
GDN_QK_HEADS = D_MODEL // GDN_DK
GDN_V_HEADS = 2 * GDN_QK_HEADS
GDN_REP = GDN_V_HEADS // GDN_QK_HEADS
GDN_CONV = 4
GDN_CONV_DIM = 2 * GDN_QK_HEADS * GDN_DK + GDN_V_HEADS * GDN_DV
CHUNK = 64
NEG_INF = -1.0e30
NORM_EPS = 1e-6

Q_WIDTH = N_HEADS * HEAD_DIM
KV_WIDTH = 2 * KV_HEADS * HEAD_DIM
NSA_MAIN = Q_WIDTH + 3 * KV_WIDTH
GDN_MAIN = GDN_CONV_DIM + GDN_V_HEADS * GDN_DV

V7X_VMEM_LIMIT_BYTES = 56 * 1024 * 1024
LANE = 128
SUBLANE = 8
ATTN_TILE = 256
CMP_TILE = 128
SCAN_TILE = 512
NT_DIMS = (((1,), (1,)), ((), ()))
TN_DIMS = (((0,), (0,)), ((), ()))


def _cparams(*sem):
    return pltpu.CompilerParams(dimension_semantics=sem, vmem_limit_bytes=V7X_VMEM_LIMIT_BYTES)


def _row_tile(m, target):
    t = min(m, target)
    while m % t:
        t //= 2
    return t


def _col_tile(n, target):
    t = min(n, target)
    while n % t or t % LANE:
        t -= LANE
    return t


def _split_bf16(x, parts):
    out = []
    for _ in range(parts - 1):
        hi = x.astype(jnp.bfloat16)
        out.append(hi)
        x = x - hi.astype(jnp.float32)
    out.append(x.astype(jnp.bfloat16))
    return out


def _bf(x):
    return x.astype(jnp.bfloat16)


def _dot(a, b, dims=None):
    if dims is None:
        return jnp.dot(a, b, preferred_element_type=jnp.float32)
    return lax.dot_general(a, b, dims, preferred_element_type=jnp.float32)


def _norm_matmul_body(x_ref, g_ref, w_ref, o_ref, h_ref):
    @pl.when(pl.program_id(1) == 0)
    def _():
        x = x_ref[...]
        ms = jnp.mean(x * x, axis=-1, keepdims=True)
        h_ref[...] = _bf(x * lax.rsqrt(ms + NORM_EPS) * g_ref[...])

    o_ref[...] = _dot(h_ref[...], w_ref[...])


def norm_matmul(x, g, w):
    m, k = x.shape
    n = w.shape[1]
    tm = _row_tile(m, 512)
    tn = _col_tile(n, 512)
    return pl.pallas_call(
        _norm_matmul_body,
        grid=(m // tm, n // tn),
        in_specs=[pl.BlockSpec((tm, k), lambda i, j: (i, 0)),
                  pl.BlockSpec((1, k), lambda i, j: (0, 0)),
                  pl.BlockSpec((k, tn), lambda i, j: (0, j))],
        out_specs=pl.BlockSpec((tm, tn), lambda i, j: (i, j)),
        out_shape=jax.ShapeDtypeStruct((m, n), jnp.float32),
        scratch_shapes=[pltpu.VMEM((tm, k), jnp.bfloat16)],
        compiler_params=_cparams("parallel", "arbitrary"),
        name="norm_matmul",
    )(x, g.reshape(1, k), w)


def _matmul_res_body(*refs):
    *a_refs, w_ref, r_ref, o_ref = refs
    a = a_refs[0][...]
    for a_ref in a_refs[1:]:
        a = a + a_ref[...]
    o_ref[...] = r_ref[...] + _dot(_bf(a), w_ref[...])


def matmul_res(a_list, w, res):
    m, k = a_list[0].shape
    n = w.shape[1]
    tm = _row_tile(m, 512)
    tn = _col_tile(n, 1024)
    return pl.pallas_call(
        _matmul_res_body,
        grid=(m // tm, n // tn),
        in_specs=[pl.BlockSpec((tm, k), lambda i, j: (i, 0)) for _ in a_list]
        + [pl.BlockSpec((k, tn), lambda i, j: (0, j)),
           pl.BlockSpec((tm, tn), lambda i, j: (i, j))],
        out_specs=pl.BlockSpec((tm, tn), lambda i, j: (i, j)),
        out_shape=jax.ShapeDtypeStruct((m, n), jnp.float32),
        compiler_params=_cparams("parallel", "arbitrary"),
        name="matmul_res",
    )(*a_list, w, res)


def _ffn_body(x_ref, g_ref, wa_ref, wb_ref, wd_ref, o_ref, h_ref, acc_ref):
    j = pl.program_id(1)

    @pl.when(j == 0)
    def _():
        x = x_ref[...]
        ms = jnp.mean(x * x, axis=-1, keepdims=True)
        h_ref[...] = _bf(x * lax.rsqrt(ms + NORM_EPS) * g_ref[...])
        acc_ref[...] = jnp.zeros_like(acc_ref)

    h = h_ref[...]
    a = _dot(h, wa_ref[...])
    b = _dot(h, wb_ref[...])
    acc_ref[...] += _dot(_bf(a * jax.nn.sigmoid(a) * b), wd_ref[...])

    @pl.when(j == pl.num_programs(1) - 1)
    def _():
        o_ref[...] = x_ref[...] + acc_ref[...]


def ffn(x, g, w_up, w_down):
    m, k = x.shape
    hdim = w_down.shape[0]
    tm = _row_tile(m, 512)
    th = _col_tile(hdim, 512)
    nh = hdim // th
    return pl.pallas_call(
        _ffn_body,
        grid=(m // tm, nh),
        in_specs=[pl.BlockSpec((tm, k), lambda i, j: (i, 0)),
                  pl.BlockSpec((1, k), lambda i, j: (0, 0)),
                  pl.BlockSpec((k, th), lambda i, j: (0, j)),
                  pl.BlockSpec((k, th), lambda i, j: (0, j + nh)),
                  pl.BlockSpec((th, k), lambda i, j: (j, 0))],
        out_specs=pl.BlockSpec((tm, k), lambda i, j: (i, 0)),
        out_shape=jax.ShapeDtypeStruct((m, k), jnp.float32),
        scratch_shapes=[pltpu.VMEM((tm, k), jnp.bfloat16), pltpu.VMEM((tm, k), jnp.float32)],
        compiler_params=_cparams("parallel", "arbitrary"),
        name="ffn",
    )(x, g.reshape(1, k), w_up, w_up, w_down)


def _norm_body(x_ref, g_ref, o_ref):
    x = x_ref[...]
    ms = jnp.mean(x * x, axis=-1, keepdims=True)
    o_ref[...] = x * lax.rsqrt(ms + NORM_EPS) * g_ref[...]


def final_norm(x, g):
    m, k = x.shape
    tm = _row_tile(m, 512)
    return pl.pallas_call(
        _norm_body,
        grid=(m // tm,),
        in_specs=[pl.BlockSpec((tm, k), lambda i: (i, 0)), pl.BlockSpec((1, k), lambda i: (0, 0))],
        out_specs=pl.BlockSpec((tm, k), lambda i: (i, 0)),
        out_shape=jax.ShapeDtypeStruct((m, k), jnp.float32),
        compiler_params=_cparams("parallel"),
        name="final_norm",
    )(x, g.reshape(1, k))


def _bucket_np(dist):
    exact = REL_BUCKETS // 2
    d = np.maximum(dist, 0)
    ratio = np.log(np.maximum(d, 1).astype(np.float32) / exact) / math.log(REL_MAX_DIST / exact)
    large = np.minimum(exact + (ratio * (REL_BUCKETS - exact)).astype(np.int32), REL_BUCKETS - 1)
    return np.where(d < exact, d, large)


def _bias_lookup(rel_table, dist):
    bucket = _bucket_np(dist).astype(np.int32)
    ids = [int(b) for b in np.unique(bucket)]
    bk = jnp.asarray(bucket)[..., None]
    out = jnp.broadcast_to(rel_table[ids[0]], bucket.shape + (rel_table.shape[1],))
    for b in ids[1:]:
        out = jnp.where(bk == b, rel_table[b], out)
    return out


def _heads_to_lanes(t):
    keys, queries, _ = t.shape
    return t.reshape(keys, queries, KV_HEADS, GROUP).transpose(2, 0, 3, 1).reshape(KV_HEADS, keys, GROUP * queries)


def _flash_bias_tiles(rel_table, window):
    j = np.arange(ATTN_TILE)[:, None]
    i = np.arange(ATTN_TILE)[None, :]
    tiles = []
    for d in range(3):
        dist = d * ATTN_TILE + i - j
        ok = dist >= 0
        if window:
            ok = ok & (dist < window)
        tiles.append(_heads_to_lanes(jnp.where(ok[..., None], _bias_lookup(rel_table, dist), NEG_INF)))
    return jnp.stack(tiles, axis=1)


def _cmp_bias_table(rel_table, ncp):
    x = np.arange(ncp)[:, None]
    i = np.arange(CMP_TILE)[None, :]
    dist = i - CMP_STRIDE * (x - 16) - (CMP_BLOCK - 1)
    far = rel_table[REL_BUCKETS - 1]
    b = _heads_to_lanes(jnp.where((dist >= 0)[..., None], _bias_lookup(rel_table, dist), far))
    return jnp.concatenate([b, b], axis=1)


def _stack_heads(q):
    return jnp.concatenate([q[:, r * HEAD_DIM:(r + 1) * HEAD_DIM] for r in range(GROUP)], axis=0)


def _gate_columns(gt_ref, col0):
    gt = jax.nn.sigmoid(gt_ref[...])
    lane = lax.broadcasted_iota(jnp.int32, gt.shape, 1)
    return [jnp.sum(jnp.where(lane == col0 + r, gt, 0.0), axis=1, keepdims=True) for r in range(GROUP)]


def _heads_from_lanes(o_t, rows, cols=None):
    parts = []
    for r in range(GROUP):
        part = o_t[:, r * rows:(r + 1) * rows].T
        if cols is not None:
            part = part * cols[r]
        parts.append(part)
    return jnp.concatenate(parts, axis=1)


def _rank_rows(score, n_rows):
    row = lax.broadcasted_iota(jnp.int32, score.shape, 0)
    rank = jnp.zeros(score.shape, jnp.int32)
    for mm in range(n_rows):
        sm = score[mm:mm + 1, :]
        ahead = (sm > score) | ((sm == score) & (row > mm))
        rank = rank + ahead.astype(jnp.int32)
    return rank


def _cmp_select_body(q_ref, kc_ref, dt_ref, gt_ref, o_ref, pen_ref, *, n_sel_blocks):
    g = pl.program_id(1)
    qi = pl.program_id(2)
    tq = CMP_TILE
    cols = GROUP * tq
    ncp = kc_ref.shape[1]
    q4 = _bf(_stack_heads(q_ref[...]) * ATTN_SCALE)
    s = _dot(_bf(kc_ref[0]), q4, NT_DIMS)
    shift = (qi * (tq // CMP_STRIDE) + ncp - 16) % ncp
    bias = dt_ref[pl.ds(pl.multiple_of(ncp - shift, SUBLANE), ncp), :]
    t_col = qi * tq + (lax.broadcasted_iota(jnp.int32, (ncp, cols), 1) & (tq - 1))
    end_pos = lax.broadcasted_iota(jnp.int32, (ncp, cols), 0) * CMP_STRIDE + (CMP_BLOCK - 1)
    mask = t_col >= end_pos
    s = jnp.where(mask, s + bias, NEG_INF)
    m = jnp.max(s, axis=0, keepdims=True)
    e = jnp.where(mask, jnp.exp(s - m), 0.0)
    p = e / jnp.maximum(jnp.sum(e, axis=0, keepdims=True), 1e-30)
    o_t = _dot(_bf(kc_ref[1].T), _bf(p))
    o_ref[...] = _heads_from_lanes(o_t, tq, _gate_columns(gt_ref, g * GROUP))

    imp = p[:, 0:tq]
    for r in range(1, GROUP):
        imp = imp + p[:, r * tq:(r + 1) * tq]
    ratio = SEL_BLOCK // CMP_STRIDE
    j_i = lax.broadcasted_iota(jnp.int32, (n_sel_blocks, ncp), 0)
    c_i = lax.broadcasted_iota(jnp.int32, (n_sel_blocks, ncp), 1)
    w = _bf((c_i >= ratio * j_i - 1) & (c_i <= ratio * j_i + ratio - 1))
    score = sum(_dot(w, part) for part in _split_bf16(imp, 3))
    blk = lax.broadcasted_iota(jnp.int32, (n_sel_blocks, tq), 0)
    tok = qi * tq + lax.broadcasted_iota(jnp.int32, (n_sel_blocks, tq), 1)
    cur = lax.shift_right_logical(tok, int(math.log2(SEL_BLOCK)))
    forced = (blk == 0) | (blk == cur) | (blk == cur - 1)
    causal = blk <= cur
    score = jnp.where(forced, FORCE_SCORE, score)
    score = jnp.where(causal, score, -1.0)
    chosen = (_rank_rows(score, n_sel_blocks) < N_SEL) & causal
    pen_ref[...] = jnp.where(chosen, 0.0, NEG_INF)


def cmp_select(proj, tail, kc, dt, batch, seq):
    tq = CMP_TILE
    nq = seq // tq
    ncp = kc.shape[3]
    nsb = seq // SEL_BLOCK
    return pl.pallas_call(
        functools.partial(_cmp_select_body, n_sel_blocks=nsb),
        grid=(batch, KV_HEADS, nq),
        in_specs=[pl.BlockSpec((tq, GROUP * HEAD_DIM), lambda b, g, i: (b * nq + i, g)),
                  pl.BlockSpec((None, None, 2, ncp, HEAD_DIM), lambda b, g, i: (b, g, 0, 0, 0)),
                  pl.BlockSpec((None, 2 * ncp, GROUP * tq), lambda b, g, i: (g, 0, 0)),
                  pl.BlockSpec((tq, LANE), lambda b, g, i: (b * nq + i, 0))],
        out_specs=[pl.BlockSpec((tq, GROUP * HEAD_DIM), lambda b, g, i: (b * nq + i, g)),
                   pl.BlockSpec((None, None, nsb, tq), lambda b, g, i: (b, g, 0, i))],
        out_shape=[jax.ShapeDtypeStruct((batch * seq, Q_WIDTH), jnp.float32),
                   jax.ShapeDtypeStruct((batch, KV_HEADS, nsb, seq), jnp.float32)],
        compiler_params=_cparams("parallel", "parallel", "arbitrary"),
        name="cmp_select",
    )(proj, kc, dt, tail)


def _flash_body(*refs, pen_block, pen_per_head, k_back, gate_col0, seq):
    it = iter(refs)
    q_ref, k_ref, v_ref, bt_ref = next(it), next(it), next(it), next(it)
    pen_ref = next(it) if pen_block else None
    gt_ref = next(it) if gate_col0 is not None else None
    o_ref, m_ref, l_ref, acc_ref, qa_ref, kb_ref, vt_ref = (next(it) for _ in range(7))
    g = pl.program_id(1)
    qi = pl.program_id(2)
    tq = tk = ATTN_TILE
    cols = GROUP * tq
    n_cls = bt_ref.shape[0]

    @pl.when(qi == 0)
    def _():
        for c in range(seq // tk):
            kb_ref[c * tk:(c + 1) * tk, :] = _bf(k_ref[c * tk:(c + 1) * tk, :])
            vt_ref[:, c * tk:(c + 1) * tk] = _bf(v_ref[c * tk:(c + 1) * tk, :].T)

    qa_ref[...] = _bf(_stack_heads(q_ref[...]) * ATTN_SCALE)
    m_ref[...] = jnp.full(m_ref.shape, NEG_INF, jnp.float32)
    l_ref[...] = jnp.zeros(l_ref.shape, jnp.float32)
    acc_ref[...] = jnp.zeros(acc_ref.shape, jnp.float32)

    def step(kj, carry):
        k0 = pl.multiple_of(kj * tk, tk)
        s = _dot(kb_ref[pl.ds(k0, tk), :], qa_ref[...], NT_DIMS)
        s = s + bt_ref[jnp.minimum(qi - kj, n_cls - 1)]
        if pen_block:
            per_tile = tk // pen_block
            pieces = []
            for a in range(per_tile):
                pen = pen_ref[pl.ds(kj * per_tile + a, 1), :]
                if not pen_per_head:
                    pen = jnp.concatenate([pen] * GROUP, axis=1)
                pieces.append(s[a * pen_block:(a + 1) * pen_block] + pen)
            s = pieces[0] if per_tile == 1 else jnp.concatenate(pieces, axis=0)
        m_prev = m_ref[...]
        m_new = jnp.maximum(m_prev, jnp.max(s, axis=0, keepdims=True))
        alpha = jnp.exp(m_prev - m_new)
        p = jnp.exp(s - m_new)
        l_ref[...] = alpha * l_ref[...] + jnp.sum(p, axis=0, keepdims=True)
        acc_ref[...] = alpha * acc_ref[...] + _dot(vt_ref[:, pl.ds(k0, tk)], _bf(p))
        m_ref[...] = m_new
        return carry

    k_lo = jnp.maximum(qi - k_back, 0) if k_back is not None else 0
    lax.fori_loop(k_lo, qi + 1, step, 0)
    o_t = acc_ref[...] / jnp.maximum(l_ref[...], 1e-30)
    cols_g = _gate_columns(gt_ref, gate_col0 + g * GROUP) if gate_col0 is not None else None
    o_ref[...] = _heads_from_lanes(o_t, tq, cols_g)


def flash_attention(q_arr, kv_arr, k_col, v_col, bias_tiles, batch, seq, *, pen=None, pen_block=0,
                    k_back=None, gate_arr=None, gate_col0=None):
    tq = ATTN_TILE
    nq = seq // tq
    cols = GROUP * tq
    in_specs = [pl.BlockSpec((tq, GROUP * HEAD_DIM), lambda b, g, i: (b * nq + i, g)),
                pl.BlockSpec((seq, HEAD_DIM), lambda b, g, i: (b, k_col + g)),
                pl.BlockSpec((seq, HEAD_DIM), lambda b, g, i: (b, v_col + g)),
                pl.BlockSpec((None,) + bias_tiles.shape[1:], lambda b, g, i: (g, 0, 0, 0))]
    args = [q_arr, kv_arr, kv_arr, bias_tiles]
    pen_per_head = False
    if pen is not None:
        if pen.ndim == 4:
            in_specs.append(pl.BlockSpec((None, None, pen.shape[2], tq), lambda b, g, i: (b, g, 0, i)))
        else:
            pen_per_head = True
            in_specs.append(pl.BlockSpec((None, None, None, pen.shape[3], cols), lambda b, g, i: (b, g, i, 0, 0)))
        args.append(pen)
    if gate_arr is not None:
        in_specs.append(pl.BlockSpec((tq, LANE), lambda b, g, i: (b * nq + i, 0)))
        args.append(gate_arr)
    return pl.pallas_call(
        functools.partial(_flash_body, pen_block=pen_block if pen is not None else 0, pen_per_head=pen_per_head,
                          k_back=k_back, gate_col0=gate_col0 if gate_arr is not None else None, seq=seq),
        grid=(batch, KV_HEADS, nq),
        in_specs=in_specs,
        out_specs=pl.BlockSpec((tq, GROUP * HEAD_DIM), lambda b, g, i: (b * nq + i, g)),
        out_shape=jax.ShapeDtypeStruct((batch * seq, Q_WIDTH), jnp.float32),
        scratch_shapes=[pltpu.VMEM((1, cols), jnp.float32), pltpu.VMEM((1, cols), jnp.float32),
                        pltpu.VMEM((HEAD_DIM, cols), jnp.float32), pltpu.VMEM((cols, HEAD_DIM), jnp.bfloat16),
                        pltpu.VMEM((seq, HEAD_DIM), jnp.bfloat16), pltpu.VMEM((HEAD_DIM, seq), jnp.bfloat16)],
        compiler_params=_cparams("parallel", "parallel", "arbitrary"),
        name="flash_attention",
    )(*args)


def _moba_gate_body(q_ref, k_ref, pen_ref, km_ref, *, n_blocks):
    qi = pl.program_id(2)
    tq = ATTN_TILE
    cols = GROUP * tq

    @pl.when(qi == 0)
    def _():
        k = k_ref[...]
        km_ref[...] = jnp.sum(k.reshape(n_blocks, MOBA_BLOCK, HEAD_DIM), axis=1) / MOBA_BLOCK

    qh, ql = _split_bf16(_stack_heads(q_ref[...]), 2)
    kh, kl = _split_bf16(km_ref[...], 2)
    gate = _dot(kh, qh, NT_DIMS) + _dot(kh, ql, NT_DIMS) + _dot(kl, qh, NT_DIMS)
    blk = lax.broadcasted_iota(jnp.int32, (n_blocks, cols), 0)
    tok = qi * tq + (lax.broadcasted_iota(jnp.int32, (n_blocks, cols), 1) & (tq - 1))
    own = lax.shift_right_logical(tok, int(math.log2(MOBA_BLOCK)))
    gate = jnp.where(blk < own, gate, NEG_INF)
    chosen = ((_rank_rows(gate, n_blocks) < MOBA_TOPK) & (blk < own)) | (blk == own)
    pen_ref[...] = jnp.where(chosen, 0.0, NEG_INF)


def moba_gate(proj, k_col, batch, seq):
    tq = ATTN_TILE
    nq = seq // tq
    cols = GROUP * tq
    nb = seq // MOBA_BLOCK
    return pl.pallas_call(
        functools.partial(_moba_gate_body, n_blocks=nb),
        grid=(batch, KV_HEADS, nq),
        in_specs=[pl.BlockSpec((tq, GROUP * HEAD_DIM), lambda b, g, i: (b * nq + i, g)),
                  pl.BlockSpec((seq, HEAD_DIM), lambda b, g, i: (b, k_col + g))],
        out_specs=pl.BlockSpec((None, None, None, nb, cols), lambda b, g, i: (b, g, i, 0, 0)),
        out_shape=jax.ShapeDtypeStruct((batch, KV_HEADS, nq, nb, cols), jnp.float32),
        scratch_shapes=[pltpu.VMEM((nb, HEAD_DIM), jnp.float32)],
        compiler_params=_cparams("parallel", "parallel", "arbitrary"),
        name="moba_gate",
    )(proj, proj)


def _tril_ones(n, strict=False):
    r = lax.broadcasted_iota(jnp.int32, (n, n), 0)
    c = lax.broadcasted_iota(jnp.int32, (n, n), 1)
    return (r > c) if strict else (r >= c)


def _chunk_cumsum(x):
    tril = _bf(_tril_ones(x.shape[0]))
    return sum(_dot(tril, part) for part in _split_bf16(x, 3))


def _head_rms_gate(o, norm_g, gate):
    ms = jnp.mean(o * o, axis=-1, keepdims=True)
    return o * lax.rsqrt(ms + NORM_EPS) * norm_g * (gate * jax.nn.sigmoid(gate))


def _hgrn2_body(q_ref, f_ref, i_ref, g_ref, lbl_ref, ng_ref, s0_ref, o_ref, s_out_ref, st_ref, *, layer):
    ti = pl.program_id(2)

    @pl.when(ti == 0)
    def _():
        st_ref[...] = s0_ref[...].T

    lbl = lbl_ref[...]
    e = jnp.exp(lbl - jnp.max(lbl, axis=0, keepdims=True))
    p = e / jnp.sum(e, axis=0, keepdims=True)
    lb = jnp.zeros((1, HG_DK), jnp.float32)
    for r in range(1, layer + 1):
        lb = lb + p[r:r + 1]
    causal = _tril_ones(CHUNK)
    work = []
    for c in range(q_ref.shape[0] // CHUNK):
        sl = slice(c * CHUNK, (c + 1) * CHUNK)
        q = q_ref[sl, :]
        qh = q * jax.nn.sigmoid(q) * HG_DK ** -0.5
        fg = lb + (1.0 - lb) * jax.nn.sigmoid(f_ref[sl, :])
        k = 1.0 - fg
        v = _bf(i_ref[sl, :])
        b = _chunk_cumsum(jnp.log(fg))
        b_mid = b[CHUNK // 2:CHUNK // 2 + 1]
        b_last = b[CHUNK - 1:CHUNK]
        a = _dot(_bf(qh * jnp.exp(b - b_mid)), _bf(k * jnp.exp(b_mid - b)), NT_DIMS)
        a = jnp.where(causal, a, 0.0)
        work.append((sl, _dot(_bf(a), v), _bf(qh * jnp.exp(b)), jnp.exp(b_last),
                     _dot(v, _bf(k * jnp.exp(b_last - b)), TN_DIMS)))
    for sl, o_intra, q_in, d_last, kv in work:
        st = st_ref[...]
        o = o_intra + _dot(q_in, _bf(st), NT_DIMS)
        st_ref[...] = st * d_last + kv
        o_ref[sl, :] = _head_rms_gate(o, ng_ref[...], g_ref[sl, :])

    @pl.when(ti == pl.num_programs(2) - 1)
    def _():
        s_out_ref[...] = st_ref[...].T


def hgrn2_scan(proj, lb_logits, norm_g, s0, layer, batch, seq):
    tt = _row_tile(seq, SCAN_TILE)
    nt = seq // tt
    h = HG_HEADS

    def col(k):
        return pl.BlockSpec((tt, HG_DK), lambda b, hh, t: (b * nt + t, k * h + hh))

    return pl.pallas_call(
        functools.partial(_hgrn2_body, layer=layer),
        grid=(batch, h, nt),
        in_specs=[col(0), col(1), col(2), col(3),
                  pl.BlockSpec((DEPTH, HG_DK), lambda b, hh, t: (0, hh)),
                  pl.BlockSpec((1, HG_DV), lambda b, hh, t: (0, 0)),
                  pl.BlockSpec((None, None, HG_DK, HG_DV), lambda b, hh, t: (b, hh, 0, 0))],
        out_specs=[pl.BlockSpec((tt, HG_DV), lambda b, hh, t: (b * nt + t, hh)),
                   pl.BlockSpec((None, None, HG_DK, HG_DV), lambda b, hh, t: (b, hh, 0, 0))],
        out_shape=[jax.ShapeDtypeStruct((batch * seq, h * HG_DV), jnp.float32),
                   jax.ShapeDtypeStruct((batch, h, HG_DK, HG_DV), jnp.float32)],
        scratch_shapes=[pltpu.VMEM((HG_DV, HG_DK), jnp.float32)],
        compiler_params=_cparams("parallel", "parallel", "arbitrary"),
        name="hgrn2_scan",
    )(proj, proj, proj, proj, lb_logits, norm_g.reshape(1, HG_DV), s0)


def _lane_column(x, lane_idx):
    lane = lax.broadcasted_iota(jnp.int32, x.shape, 1)
    return jnp.sum(jnp.where(lane == lane_idx, x, 0.0), axis=1, keepdims=True)


def _softplus(x):
    return jnp.maximum(x, 0.0) + jnp.log(1.0 + jnp.exp(-jnp.abs(x)))


def _l2n(x):
    return x * lax.rsqrt(jnp.sum(x * x, axis=-1, keepdims=True) + NORM_EPS)


def _gdn_body(q_ref, k_ref, v_ref, z_ref, t_ref, bq_ref, bk_ref, bv_ref, wq_ref, wk_ref, wv_ref,
              al_ref, dtb_ref, ng_ref, s0_ref, o_ref, s_out_ref, xs_ref, y_ref, s_ref):
    hq = pl.program_id(1)
    ti = pl.program_id(2)
    tt = q_ref.shape[0]
    dk, dv = GDN_DK, GDN_DV
    pad = SUBLANE

    @pl.when(ti == 0)
    def _():
        s_ref[...] = s0_ref[...]
        xs_ref[0:pad, :] = jnp.concatenate([bq_ref[...], bk_ref[...], bv_ref[...]], axis=1)

    x = jnp.concatenate([q_ref[...], k_ref[...], v_ref[...]], axis=1)
    xs_ref[pad:, :] = x
    cw = jnp.concatenate([wq_ref[...], wk_ref[...], wv_ref[...]], axis=1)
    y = xs_ref[pad - 3:pad - 3 + tt, :] * cw[0:1]
    for i in range(1, GDN_CONV - 1):
        y = y + xs_ref[pad - 3 + i:pad - 3 + i + tt, :] * cw[i:i + 1]
    y = y + x * cw[GDN_CONV - 1:GDN_CONV]
    xs_ref[0:pad, :] = x[tt - pad:tt]
    y_ref[...] = y * jax.nn.sigmoid(y)

    strict = _tril_ones(CHUNK, strict=True)
    incl = _tril_ones(CHUNK)
    sel_rows = lax.shift_right_logical(lax.broadcasted_iota(jnp.int32, (GDN_REP * CHUNK, LANE), 0),
                                       int(math.log2(CHUNK)))
    sel_lane = lax.broadcasted_iota(jnp.int32, (GDN_REP * CHUNK, LANE), 1)
    pick = _bf(sel_lane == GDN_V_HEADS + hq * GDN_REP + sel_rows)
    n_chunks = tt // CHUNK

    work = []
    for c in range(n_chunks):
        sl = slice(c * CHUNK, (c + 1) * CHUNK)
        yc = y_ref[sl, :]
        q = _l2n(yc[:, 0:dk]) * dk ** -0.5
        k = _l2n(yc[:, dk:2 * dk])
        qb, kb = _bf(q), _bf(k)
        kk = _dot(kb, kb, NT_DIMS)
        qk = _dot(qb, kb, NT_DIMS)
        tl = t_ref[sl, :]
        beta_all = jax.nn.sigmoid(tl)
        g_all = _chunk_cumsum(-jnp.exp(al_ref[...]) * _softplus(tl + dtb_ref[...]))
        g_rows = sum(_dot(pick, part, NT_DIMS) for part in _split_bf16(g_all, 3))
        for e in range(GDN_REP):
            hv = hq * GDN_REP + e
            v = yc[:, 2 * dk + e * dv:2 * dk + (e + 1) * dv]
            bt = _lane_column(beta_all, hv)
            gc = _lane_column(g_all, GDN_V_HEADS + hv)
            gdiff = gc - g_rows[e * CHUNK:(e + 1) * CHUNK]
            d_strict = jnp.where(strict, jnp.exp(jnp.where(strict, gdiff, 0.0)), 0.0)
            d_incl = jnp.where(incl, jnp.exp(jnp.where(incl, gdiff, 0.0)), 0.0)
            eg = jnp.exp(gc)
            g_last = gc[CHUNK - 1:CHUNK]
            work.append(dict(
                c=c, e=e, sol=jnp.concatenate([bt * v, (bt * eg) * k], axis=1), pw=bt * kk * d_strict,
                aq=_bf(qk * d_incl), q_in=_bf(q * eg), k_out=_bf(k * jnp.exp(g_last - gc)), d_last=jnp.exp(g_last)))

    r_i = lax.broadcasted_iota(jnp.int32, (CHUNK, CHUNK), 0)
    c_i = lax.broadcasted_iota(jnp.int32, (CHUNK, CHUNK), 1)
    same = [lax.shift_right_logical(r_i, sh) == lax.shift_right_logical(c_i, sh) for sh in range(3, 7)]
    eye = (r_i == c_i).astype(jnp.float32)
    for wk in work:
        l8 = jnp.where(same[0], wk["pw"], 0.0)
        l8b = _bf(l8)
        wk["t"] = eye - l8
        wk["p"] = _dot(l8b, l8b)
    for wk in work:
        pb = _bf(wk["p"])
        wk["t"] = wk["t"] + _dot(_bf(wk["t"]), pb)
        wk["p"] = _dot(pb, pb)
    for wk in work:
        wk["t"] = wk["t"] + _dot(_bf(wk["t"]), _bf(wk["p"]))
    for lvl in range(1, len(same)):
        for wk in work:
            tb = _bf(wk["t"])
            off = _bf(jnp.where(same[lvl] & jnp.logical_not(same[lvl - 1]), wk["pw"], 0.0))
            wk["t"] = wk["t"] - _dot(tb, _bf(_dot(off, tb)))
    for wk in work:
        wk["sol"] = _dot(_bf(wk["t"]), _bf(wk["sol"]))

    for wk in work:
        c, e = wk["c"], wk["e"]
        sl = slice(c * CHUNK, (c + 1) * CHUNK)
        u0, w = wk["sol"][:, :dv], wk["sol"][:, dv:]
        s = s_ref[e]
        sb = _bf(s)
        u = u0 - _dot(_bf(w), sb)
        o = _dot(wk["q_in"], sb) + _dot(wk["aq"], _bf(u))
        s_ref[e] = wk["d_last"] * s + _dot(wk["k_out"], _bf(u), TN_DIMS)
        o_ref[sl, e * dv:(e + 1) * dv] = _head_rms_gate(o, ng_ref[...], z_ref[sl, e * dv:(e + 1) * dv])

    @pl.when(ti == pl.num_programs(2) - 1)
    def _():
        s_out_ref[...] = s_ref[...]


def gdn_scan(main, tail, conv_buf, conv_w, a_log, dt_bias, norm_g, s0, batch, seq):
    tt = _row_tile(seq, SCAN_TILE)
    nt = seq // tt
    hq, rep, dk, dv = GDN_QK_HEADS, GDN_REP, GDN_DK, GDN_DV
    vw = rep * dv
    buf = jnp.pad(conv_buf, ((0, 0), (SUBLANE - (GDN_CONV - 1), 0), (0, 0)))
    pad_lanes = jnp.zeros((LANE - 2 * GDN_V_HEADS,), jnp.float32)
    a_row = jnp.concatenate([jnp.zeros((GDN_V_HEADS,), jnp.float32), a_log, pad_lanes]).reshape(1, LANE)
    dt_row = jnp.concatenate([jnp.zeros((GDN_V_HEADS,), jnp.float32), dt_bias, pad_lanes]).reshape(1, LANE)
    k0 = hq
    v0 = 2 * hq * dk // vw
    z0 = GDN_CONV_DIM // vw
    row = lambda b, h, t: b * nt + t
    return pl.pallas_call(
        _gdn_body,
        grid=(batch, hq, nt),
        in_specs=[pl.BlockSpec((tt, dk), lambda b, h, t: (row(b, h, t), h)),
                  pl.BlockSpec((tt, dk), lambda b, h, t: (row(b, h, t), k0 + h)),
                  pl.BlockSpec((tt, vw), lambda b, h, t: (row(b, h, t), v0 + h)),
                  pl.BlockSpec((tt, vw), lambda b, h, t: (row(b, h, t), z0 + h)),
                  pl.BlockSpec((tt, LANE), lambda b, h, t: (row(b, h, t), 0)),
                  pl.BlockSpec((None, SUBLANE, dk), lambda b, h, t: (b, 0, h)),
                  pl.BlockSpec((None, SUBLANE, dk), lambda b, h, t: (b, 0, k0 + h)),
                  pl.BlockSpec((None, SUBLANE, vw), lambda b, h, t: (b, 0, v0 + h)),
                  pl.BlockSpec((GDN_CONV, dk), lambda b, h, t: (0, h)),
                  pl.BlockSpec((GDN_CONV, dk), lambda b, h, t: (0, k0 + h)),
                  pl.BlockSpec((GDN_CONV, vw), lambda b, h, t: (0, v0 + h)),
                  pl.BlockSpec((1, LANE), lambda b, h, t: (0, 0)),
                  pl.BlockSpec((1, LANE), lambda b, h, t: (0, 0)),
                  pl.BlockSpec((1, dv), lambda b, h, t: (0, 0)),
                  pl.BlockSpec((None, rep, dk, dv), lambda b, h, t: (b, h, 0, 0))],
        out_specs=[pl.BlockSpec((tt, vw), lambda b, h, t: (row(b, h, t), h)),
                   pl.BlockSpec((None, rep, dk, dv), lambda b, h, t: (b, h, 0, 0))],
        out_shape=[jax.ShapeDtypeStruct((batch * seq, GDN_V_HEADS * dv), jnp.float32),
                   jax.ShapeDtypeStruct((batch, GDN_V_HEADS, dk, dv), jnp.float32)],
        scratch_shapes=[pltpu.VMEM((tt + SUBLANE, 2 * dk + vw), jnp.float32),
                        pltpu.VMEM((tt, 2 * dk + vw), jnp.float32),
                        pltpu.VMEM((rep, dk, dv), jnp.float32)],
        compiler_params=_cparams("parallel", "parallel", "arbitrary"),
        name="gdn_scan",
    )(main, main, main, main, tail, buf, buf, buf, conv_w, conv_w, conv_w, a_row, dt_row,
      norm_g.reshape(1, dv), s0)


PAGE = 128
KV_SLABS = 2 * KV_HEADS
PAGE_ROWS = PAGE * KV_SLABS
CMP_PAGES = 8
CHUNKS_PER_PAGE = PAGE // CMP_STRIDE
ROWS = GROUP * 4


def _page_slab(pg, slab):
    return pg[pl.ds(slab, PAGE, stride=KV_SLABS), :]


def _compress_part_body(*refs, n_prefetch, paged):
    refs = refs[n_prefetch:]
    pages, w_ref, o_ref, xs_ref = refs[:CMP_PAGES], refs[CMP_PAGES], refs[CMP_PAGES + 1], refs[CMP_PAGES + 2]
    per_g = CMP_PAGES * CHUNKS_PER_PAGE
    for k, pg in enumerate(pages):
        for cg in range(KV_SLABS):
            xs_ref[k, cg] = _page_slab(pg, cg) if paged else pg[:, cg * HEAD_DIM:(cg + 1) * HEAD_DIM]
    for c in range(2):
        acc = None
        for l in range(CMP_STRIDE):
            rows = [xs_ref.at[k, c * KV_HEADS + g][pl.ds(l, CHUNKS_PER_PAGE, stride=CMP_STRIDE), :]
                    for g in range(KV_HEADS) for k in range(CMP_PAGES)]
            d = _dot(_bf(jnp.concatenate(rows, axis=0)), w_ref[c, l])
            acc = d if acc is None else acc + d
        for g in range(KV_HEADS):
            o_ref[c, g] = acc[g * per_g:(g + 1) * per_g]


def _compress_weights(cmp_w1):
    n_part = CMP_BLOCK // CMP_STRIDE
    w = cmp_w1.reshape(2, n_part, CMP_STRIDE, HEAD_DIM, CMP_HIDDEN).transpose(0, 2, 3, 1, 4)
    return w.reshape(2, CMP_STRIDE, HEAD_DIM, n_part * CMP_HIDDEN).astype(jnp.bfloat16)


def compress_part_rows(kv_arr, col_block, cmp_w1, batch, seq):
    w = _compress_weights(cmp_w1)
    pages_per_b = seq // PAGE
    steps = pages_per_b // CMP_PAGES
    per_g = CMP_PAGES * CHUNKS_PER_PAGE

    def page_spec(k):
        return pl.BlockSpec((PAGE, KV_WIDTH), lambda b, s: (b * pages_per_b + s * CMP_PAGES + k, col_block))

    return pl.pallas_call(
        functools.partial(_compress_part_body, n_prefetch=0, paged=False),
        grid=(batch, steps),
        in_specs=[page_spec(k) for k in range(CMP_PAGES)] + [pl.BlockSpec(w.shape, lambda b, s: (0, 0, 0, 0))],
        out_specs=pl.BlockSpec((None, 2, KV_HEADS, per_g, w.shape[-1]), lambda b, s: (b, 0, 0, s, 0)),
        out_shape=jax.ShapeDtypeStruct((batch, 2, KV_HEADS, seq // CMP_STRIDE, w.shape[-1]), jnp.float32),
        scratch_shapes=[pltpu.VMEM((CMP_PAGES, 2 * KV_HEADS, PAGE, HEAD_DIM), jnp.float32)],
        compiler_params=_cparams("parallel", "arbitrary"),
        name="compress_part_rows",
    )(*([kv_arr] * CMP_PAGES), w)


def compress_part_paged(pool, page_table, cmp_w1):
    w = _compress_weights(cmp_w1)
    batch, n_pages = page_table.shape
    steps = n_pages // CMP_PAGES
    per_g = CMP_PAGES * CHUNKS_PER_PAGE

    def page_spec(k):
        return pl.BlockSpec((None, PAGE_ROWS, HEAD_DIM), lambda b, s, pt: (pt[b, s * CMP_PAGES + k], 0, 0))

    return pl.pallas_call(
        functools.partial(_compress_part_body, n_prefetch=1, paged=True),
        grid_spec=pltpu.PrefetchScalarGridSpec(
            num_scalar_prefetch=1, grid=(batch, steps),
            in_specs=[page_spec(k) for k in range(CMP_PAGES)]
            + [pl.BlockSpec(w.shape, lambda b, s, pt: (0, 0, 0, 0))],
            out_specs=pl.BlockSpec((None, 2, KV_HEADS, per_g, w.shape[-1]), lambda b, s, pt: (b, 0, 0, s, 0)),
            scratch_shapes=[pltpu.VMEM((CMP_PAGES, 2 * KV_HEADS, PAGE, HEAD_DIM), jnp.float32)]),
        out_shape=jax.ShapeDtypeStruct((batch, 2, KV_HEADS, n_pages * CHUNKS_PER_PAGE, w.shape[-1]), jnp.float32),
        compiler_params=_cparams("parallel", "arbitrary"),
        name="compress_part_paged",
    )(page_table, *([pool] * CMP_PAGES), w)


def _gelu_tanh(x):
    return x * (0.5 * (1.0 + jnp.tanh(math.sqrt(2.0 / math.pi) * (x + 0.044715 * (x * x * x)))))


def _compress_finish_body(p_ref, peh_ref, w2_ref, o_ref):
    n = p_ref.shape[1]
    for c in range(2):
        p = p_ref[c]
        hid = peh_ref[c:c + 1, :] + p[:, :CMP_HIDDEN]
        hid = hid + pltpu.roll(p[:, CMP_HIDDEN:], n - 1, 0)
        o_ref[c] = _dot(_bf(_gelu_tanh(hid)), _bf(w2_ref[c]))


def compress_finish(part, cmp_w1, cmp_w2, cmp_pe):
    batch, _, _, n, width = part.shape
    pe_hid = jnp.einsum('cld,cldh->ch', cmp_pe, cmp_w1)
    return pl.pallas_call(
        _compress_finish_body,
        grid=(batch, KV_HEADS),
        in_specs=[pl.BlockSpec((None, 2, None, n, width), lambda b, g: (b, 0, g, 0, 0)),
                  pl.BlockSpec((2, CMP_HIDDEN), lambda b, g: (0, 0)),
                  pl.BlockSpec((2, CMP_HIDDEN, HEAD_DIM), lambda b, g: (0, 0, 0))],
        out_specs=pl.BlockSpec((None, None, 2, n, HEAD_DIM), lambda b, g: (b, g, 0, 0, 0)),
        out_shape=jax.ShapeDtypeStruct((batch, KV_HEADS, 2, n, HEAD_DIM), jnp.float32),
        compiler_params=_cparams("parallel", "parallel"),
        name="compress_finish",
    )(part, pe_hid, cmp_w2)


def _rows_to_col(row, n):
    eye = lax.broadcasted_iota(jnp.int32, (n, n), 0) == lax.broadcasted_iota(jnp.int32, (n, n), 1)
    return jnp.sum(jnp.where(eye, jnp.broadcast_to(row, (n, n)), 0.0), axis=1, keepdims=True)


def _sample_bias_tiles(rel_table, past, n_new):
    j = np.arange(PAGE)[:, None]
    t = np.arange(n_new)[None, :]
    far = np.full((PAGE, n_new), REL_MAX_DIST)
    first = WINDOW + t - j
    last = PAGE + t - j
    new = t - j
    tiles = []
    for dist, ok in ((far, far > 0), (first, first < WINDOW), (last, last > 0), (new, (new >= 0) & (j < n_new))):
        b = jnp.where(ok[..., None], _bias_lookup(rel_table, dist), NEG_INF)
        b = b.reshape(PAGE, n_new, KV_HEADS, GROUP).transpose(2, 0, 3, 1).reshape(KV_HEADS, PAGE, GROUP * n_new)
        tiles.append(b)
    return jnp.stack(tiles, axis=1)


def _sample_cmp_body(q_ref, kc_ref, bias_ref, gt_ref, o_ref, pen_ref, sc_ref, *, n_sel_blocks, past):
    n = kc_ref.shape[1]
    nbp = pen_ref.shape[0]
    q = _bf(q_ref[...] * ATTN_SCALE)
    s = _dot(_bf(kc_ref[0]), q, NT_DIMS) + bias_ref[...]
    m = jnp.max(s, axis=0, keepdims=True)
    e = jnp.exp(s - m)
    p = e / jnp.maximum(jnp.sum(e, axis=0, keepdims=True), 1e-30)
    o = _dot(_bf(p), _bf(kc_ref[1]), TN_DIMS)
    o_ref[...] = o * jax.nn.sigmoid(gt_ref[...])

    r_i = lax.broadcasted_iota(jnp.int32, (ROWS, ROWS), 0)
    c_i = lax.broadcasted_iota(jnp.int32, (ROWS, ROWS), 1)
    n_tok = ROWS // GROUP
    same_tok = _bf((r_i & (n_tok - 1)) == (c_i & (n_tok - 1)))
    ratio = SEL_BLOCK // CMP_STRIDE
    j_i = lax.broadcasted_iota(jnp.int32, (nbp, n), 0)
    k_i = lax.broadcasted_iota(jnp.int32, (nbp, n), 1)
    w = _bf((k_i >= ratio * j_i - 1) & (k_i <= ratio * j_i + ratio - 1) & (j_i < n_sel_blocks))
    imp = sum(_dot(part, same_tok) for part in _split_bf16(p, 3))
    score = sum(_dot(w, part) for part in _split_bf16(imp, 3))
    blk = lax.broadcasted_iota(jnp.int32, (nbp, ROWS), 0)
    tok = past + (lax.broadcasted_iota(jnp.int32, (nbp, ROWS), 1) & (n_tok - 1))
    cur = lax.shift_right_logical(tok, int(math.log2(SEL_BLOCK)))
    forced = (blk == 0) | (blk == cur) | (blk == cur - 1)
    causal = blk <= cur
    score = jnp.where(forced, FORCE_SCORE, score)
    score = jnp.where(causal, score, -1.0)
    sc_ref[...] = score

    def count(mm, rank):
        sm = sc_ref[pl.ds(mm, 1), :]
        ahead = (sm > score) | ((sm == score) & (blk > mm))
        return rank + ahead.astype(jnp.int32)

    rank = lax.fori_loop(0, n_sel_blocks, count, jnp.zeros((nbp, ROWS), jnp.int32))
    chosen = (rank < N_SEL) & causal & (blk < n_sel_blocks)
    pen_ref[...] = jnp.where(chosen, 0.0, NEG_INF)


def sample_cmp_select(qs, kc, bias, gate, past, n_sel_blocks):
    batch = qs.shape[0]
    n = kc.shape[3]
    nbp = -(-n_sel_blocks // SUBLANE) * SUBLANE
    return pl.pallas_call(
        functools.partial(_sample_cmp_body, n_sel_blocks=n_sel_blocks, past=past),
        grid=(batch, KV_HEADS),
        in_specs=[pl.BlockSpec((None, None, ROWS, HEAD_DIM), lambda b, g: (b, g, 0, 0)),
                  pl.BlockSpec((None, None, 2, n, HEAD_DIM), lambda b, g: (b, g, 0, 0, 0)),
                  pl.BlockSpec((None, n, ROWS), lambda b, g: (g, 0, 0)),
                  pl.BlockSpec((None, None, ROWS, HEAD_DIM), lambda b, g: (b, g, 0, 0))],
        out_specs=[pl.BlockSpec((None, None, ROWS, HEAD_DIM), lambda b, g: (b, g, 0, 0)),
                   pl.BlockSpec((None, None, nbp, ROWS), lambda b, g: (b, g, 0, 0))],
        out_shape=[jax.ShapeDtypeStruct((batch, KV_HEADS, ROWS, HEAD_DIM), jnp.float32),
                   jax.ShapeDtypeStruct((batch, KV_HEADS, nbp, ROWS), jnp.float32)],
        scratch_shapes=[pltpu.VMEM((nbp, ROWS), jnp.float32)],
        compiler_params=_cparams("parallel", "parallel"),
        name="sample_cmp_select",
    )(qs, kc, bias, gate)


ATTN_PAGES = 4


def _paged_attn_body(*refs, pen_block, gated, n_pages):
    it = iter(refs)
    pt_ref, tid_ref, q_ref = next(it), next(it), next(it)
    pages = [next(it) for _ in range(ATTN_PAGES)]
    new_ref, bt_ref = next(it), next(it)
    pen_ref = next(it) if pen_block else None
    gt_ref = next(it) if gated else None
    o_ref, m_ref, l_ref, acc_ref = next(it), next(it), next(it), next(it)
    step = pl.program_id(1)

    @pl.when(step == 0)
    def _():
        m_ref[...] = jnp.full(m_ref.shape, NEG_INF, jnp.float32)
        l_ref[...] = jnp.zeros(l_ref.shape, jnp.float32)
        acc_ref[...] = jnp.zeros(acc_ref.shape, jnp.float32)

    def attend(g, k, v, bias, page, n_keys):
        s = _dot(_bf(k), _bf(q_ref[g] * ATTN_SCALE), NT_DIMS) + bias
        if pen_block:
            if pen_block >= PAGE:
                s = s + pen_ref[g, pl.ds(page // (pen_block // PAGE), 1), :]
            else:
                per_page = PAGE // pen_block
                pieces = [s[a * pen_block:min((a + 1) * pen_block, n_keys)]
                          + pen_ref[g, pl.ds(page * per_page + a, 1), :]
                          for a in range(-(-n_keys // pen_block))]
                s = pieces[0] if len(pieces) == 1 else jnp.concatenate(pieces, axis=0)
        m_prev = m_ref[g]
        m_new = jnp.maximum(m_prev, jnp.max(s, axis=0, keepdims=True))
        alpha = jnp.exp(m_prev - m_new)
        p = jnp.exp(s - m_new)
        l_ref[g] = alpha * l_ref[g] + jnp.sum(p, axis=0, keepdims=True)
        acc_ref[g] = _rows_to_col(alpha, ROWS) * acc_ref[g] + _dot(_bf(p), _bf(v), TN_DIMS)
        m_ref[g] = m_new

    half = KV_HEADS * HEAD_DIM
    for kk in range(ATTN_PAGES):
        page = step * ATTN_PAGES + kk
        tile = tid_ref[page]
        for g in range(KV_HEADS):
            attend(g, _page_slab(pages[kk], g), _page_slab(pages[kk], KV_HEADS + g), bt_ref[g, tile], page, PAGE)

    @pl.when(step == pl.num_programs(1) - 1)
    def _():
        n_new = new_ref.shape[0]
        tile = tid_ref[n_pages]
        for g in range(KV_HEADS):
            attend(g, new_ref[:, g * HEAD_DIM:(g + 1) * HEAD_DIM],
                   new_ref[:, half + g * HEAD_DIM:half + (g + 1) * HEAD_DIM], bt_ref[g, tile, 0:n_new, :],
                   n_pages, n_new)
            o = acc_ref[g] / _rows_to_col(jnp.maximum(l_ref[g], 1e-30), ROWS)
            if gated:
                o = o * jax.nn.sigmoid(gt_ref[g])
            o_ref[g] = o


def paged_attention(qs, pool, page_table, tile_ids, new_kv, bias_tiles, *, pen=None, pen_block=0, gate=None):
    batch, n_pages = page_table.shape
    steps = n_pages // ATTN_PAGES
    n_new = new_kv.shape[1]

    def page_spec(k):
        return pl.BlockSpec((None, PAGE_ROWS, HEAD_DIM), lambda b, s, pt, tid: (pt[b, s * ATTN_PAGES + k], 0, 0))

    grp = lambda b, s, pt, tid: (b, 0, 0, 0)
    in_specs = ([pl.BlockSpec((None, KV_HEADS, ROWS, HEAD_DIM), grp)] + [page_spec(k) for k in range(ATTN_PAGES)]
                + [pl.BlockSpec((None, n_new, KV_WIDTH), lambda b, s, pt, tid: (b, 0, 0)),
                   pl.BlockSpec(bias_tiles.shape, lambda b, s, pt, tid: (0, 0, 0, 0))])
    args = [qs] + [pool] * ATTN_PAGES + [new_kv, bias_tiles]
    if pen is not None:
        in_specs.append(pl.BlockSpec((None,) + pen.shape[1:], grp))
        args.append(pen)
    if gate is not None:
        in_specs.append(pl.BlockSpec((None, KV_HEADS, ROWS, HEAD_DIM), grp))
        args.append(gate)
    return pl.pallas_call(
        functools.partial(_paged_attn_body, pen_block=pen_block if pen is not None else 0, gated=gate is not None,
                          n_pages=n_pages),
        grid_spec=pltpu.PrefetchScalarGridSpec(
            num_scalar_prefetch=2, grid=(batch, steps), in_specs=in_specs,
            out_specs=pl.BlockSpec((None, KV_HEADS, ROWS, HEAD_DIM), grp),
            scratch_shapes=[pltpu.VMEM((KV_HEADS, 1, ROWS), jnp.float32), pltpu.VMEM((KV_HEADS, 1, ROWS), jnp.float32),
                            pltpu.VMEM((KV_HEADS, ROWS, HEAD_DIM), jnp.float32)]),
        out_shape=jax.ShapeDtypeStruct((batch, KV_HEADS, ROWS, HEAD_DIM), jnp.float32),
        compiler_params=_cparams("parallel", "arbitrary"),
        name="paged_attention",
    )(page_table, tile_ids, *args)


def _moba_sample_gate_body(pt_ref, q_ref, *refs, n_blocks, past):
    pages, (pen_ref, km_ref) = refs[:ATTN_PAGES], refs[ATTN_PAGES:]
    step = pl.program_id(1)
    half = KV_HEADS * HEAD_DIM
    per_block = MOBA_BLOCK // PAGE

    @pl.when(step == 0)
    def _():
        km_ref[...] = jnp.zeros(km_ref.shape, jnp.float32)

    for kk in range(ATTN_PAGES):
        blk = (step * ATTN_PAGES + kk) // per_block
        slab_sums = jnp.sum(pages[kk][...].reshape(PAGE, KV_SLABS, HEAD_DIM), axis=0)
        for g in range(KV_HEADS):
            km_ref[g, pl.ds(blk, 1), :] += slab_sums[g:g + 1]

    @pl.when(step == pl.num_programs(1) - 1)
    def _():
        nbp = pen_ref.shape[1]
        blk = lax.broadcasted_iota(jnp.int32, (nbp, ROWS), 0)
        n_tok = ROWS // GROUP
        tok = past + (lax.broadcasted_iota(jnp.int32, (nbp, ROWS), 1) & (n_tok - 1))
        own = lax.shift_right_logical(tok, int(math.log2(MOBA_BLOCK)))
        for g in range(KV_HEADS):
            kh, kl = _split_bf16(km_ref[g] / MOBA_BLOCK, 2)
            qh, ql = _split_bf16(q_ref[g], 2)
            gate = _dot(kh, qh, NT_DIMS) + _dot(kh, ql, NT_DIMS) + _dot(kl, qh, NT_DIMS)
            gate = jnp.where(blk < own, gate, NEG_INF)
            chosen = ((_rank_rows(gate, n_blocks) < MOBA_TOPK) & (blk < own)) | (blk == own)
            pen_ref[g] = jnp.where(chosen, 0.0, NEG_INF)


def moba_sample_gate(qs, pool, page_table, past, n_new):
    batch, n_pages = page_table.shape
    steps = n_pages // ATTN_PAGES
    n_blocks = -(-(past + n_new) // MOBA_BLOCK)
    nbp = -(-n_blocks // SUBLANE) * SUBLANE

    def page_spec(k):
        return pl.BlockSpec((None, PAGE_ROWS, HEAD_DIM), lambda b, s, pt: (pt[b, s * ATTN_PAGES + k], 0, 0))

    return pl.pallas_call(
        functools.partial(_moba_sample_gate_body, n_blocks=n_blocks, past=past),
        grid_spec=pltpu.PrefetchScalarGridSpec(
            num_scalar_prefetch=1, grid=(batch, steps),
            in_specs=[pl.BlockSpec((None, KV_HEADS, ROWS, HEAD_DIM), lambda b, s, pt: (b, 0, 0, 0))]
            + [page_spec(k) for k in range(ATTN_PAGES)],
            out_specs=pl.BlockSpec((None, KV_HEADS, nbp, ROWS), lambda b, s, pt: (b, 0, 0, 0)),
            scratch_shapes=[pltpu.VMEM((KV_HEADS, nbp, HEAD_DIM), jnp.float32)]),
        out_shape=jax.ShapeDtypeStruct((batch, KV_HEADS, nbp, ROWS), jnp.float32),
        compiler_params=_cparams("parallel", "arbitrary"),
        name="moba_sample_gate",
    )(page_table, qs, *([pool] * ATTN_PAGES))


def _sample_rows(x, batch, n_tok):
    return x.reshape(batch, n_tok, KV_HEADS, GROUP, HEAD_DIM).transpose(0, 2, 3, 1, 4).reshape(
        batch, KV_HEADS, GROUP * n_tok, HEAD_DIM)


def _sample_unrows(o, batch, n_tok):
    return o.reshape(batch, KV_HEADS, GROUP, n_tok, HEAD_DIM).transpose(0, 3, 1, 2, 4).reshape(batch * n_tok, Q_WIDTH)


def _sample_gate_rows(tail, branch, batch, n_tok):
    gt = tail[:, branch * N_HEADS:(branch + 1) * N_HEADS].reshape(batch, n_tok, KV_HEADS, GROUP)
    gt = gt.transpose(0, 2, 3, 1).reshape(batch, KV_HEADS, GROUP * n_tok, 1)
    return jnp.broadcast_to(gt, (batch, KV_HEADS, GROUP * n_tok, HEAD_DIM))


def _pad_new(kv_new, batch, n_tok):
    return jnp.pad(kv_new.reshape(batch, n_tok, KV_WIDTH), ((0, 0), (0, SUBLANE - n_tok), (0, 0)))


def _nsa_sample_pallas(main, tail, cache_c, cache_s, cache_w, page_table, cmp_w1, cmp_w2, cmp_pe, rel_table):
    batch, n_pages = page_table.shape
    n_tok = main.shape[0] // batch
    past = n_pages * PAGE
    wbuf = cache_w.shape[1]
    assert n_tok == ROWS // GROUP and cache_c.shape[1] == PAGE and wbuf == WINDOW and WINDOW % PAGE == 0
    n_cmp = (past + n_tok - CMP_BLOCK) // CMP_STRIDE + 1
    assert n_cmp + CMP_BLOCK // CMP_STRIDE - 1 == past // CMP_STRIDE
    qs = _sample_rows(main[:, :Q_WIDTH], batch, n_tok)
    kv_new = [main[:, Q_WIDTH + c * KV_WIDTH:Q_WIDTH + (c + 1) * KV_WIDTH] for c in range(3)]
    tiles = _sample_bias_tiles(rel_table, past, n_tok)
    flat = lambda pool: pool.reshape(pool.shape[0], PAGE_ROWS, HEAD_DIM)

    kc = compress_finish(compress_part_paged(flat(cache_c), page_table, cmp_w1), cmp_w1, cmp_w2, cmp_pe)
    n = kc.shape[3]
    dist = past + np.arange(n_tok)[None, :] - (np.arange(n)[:, None] * CMP_STRIDE + CMP_BLOCK - 1)
    ok = (dist >= 0) & (np.arange(n)[:, None] < n_cmp)
    cb = jnp.where(ok[..., None], _bias_lookup(rel_table, dist), NEG_INF)
    cb = cb.reshape(n, n_tok, KV_HEADS, GROUP).transpose(2, 0, 3, 1).reshape(KV_HEADS, n, GROUP * n_tok)
    n_sel_blocks = -(-(past + n_tok) // SEL_BLOCK)
    o_cmp, pen = sample_cmp_select(qs, kc, cb, _sample_gate_rows(tail, 0, batch, n_tok), past, n_sel_blocks)

    far_then_last = jnp.asarray([0] * (n_pages - 1) + [2, 3], jnp.int32)
    o_sel = paged_attention(qs, flat(cache_s), page_table, far_then_last, _pad_new(kv_new[1], batch, n_tok), tiles,
                            pen=pen, pen_block=SEL_BLOCK, gate=_sample_gate_rows(tail, 1, batch, n_tok))
    w_pages = wbuf // PAGE
    win_table = jnp.arange(batch * w_pages, dtype=jnp.int32).reshape(batch, w_pages)
    win_tiles = jnp.asarray([1] + [0] * (w_pages - 2) + [2, 3], jnp.int32)
    o_win = paged_attention(qs, cache_w.reshape(batch * w_pages, PAGE_ROWS, HEAD_DIM), win_table, win_tiles,
                            _pad_new(kv_new[2], batch, n_tok), tiles, gate=_sample_gate_rows(tail, 2, batch, n_tok))
    outs = [_sample_unrows(o, batch, n_tok) for o in (o_cmp, o_sel, o_win)]
    shape = (batch, n_tok, 2, KV_HEADS, HEAD_DIM)
    new_win = jnp.concatenate([cache_w[:, n_tok:], kv_new[2].reshape(shape)], axis=1)
    return outs, kv_new[0].reshape(shape), kv_new[1].reshape(shape), new_win


def _moba_sample_pallas(proj, cache_kv, page_table, rel_table):
    batch, n_pages = page_table.shape
    n_tok = proj.shape[0] // batch
    past = n_pages * PAGE
    assert n_tok == ROWS // GROUP and (past // MOBA_BLOCK) * MOBA_BLOCK == past
    qs = _sample_rows(proj[:, :Q_WIDTH], batch, n_tok)
    kv_new = proj[:, Q_WIDTH:]
    pool = cache_kv.reshape(cache_kv.shape[0], PAGE_ROWS, HEAD_DIM)
    pen = moba_sample_gate(qs, pool, page_table, past, n_tok)
    tile_ids = jnp.asarray([0] * (n_pages - 1) + [2, 3], jnp.int32)
    o = paged_attention(qs, pool, page_table, tile_ids, _pad_new(kv_new, batch, n_tok),
                        _sample_bias_tiles(rel_table, past, n_tok), pen=pen, pen_block=MOBA_BLOCK)
    return _sample_unrows(o, batch, n_tok), kv_new.reshape(batch, n_tok, 2, KV_HEADS, HEAD_DIM)


def _rms_norm(x, g):
    xf = x.astype(jnp.float32)
    y = xf * lax.rsqrt(jnp.mean(xf * xf, axis=-1, keepdims=True) + NORM_EPS)
    return (y * g.astype(jnp.float32)).astype(x.dtype)


def _l2norm(x):
    return x * lax.rsqrt(jnp.sum(x * x, axis=-1, keepdims=True) + NORM_EPS)


def _masked_softmax(logits, mask):
    logits = jnp.where(mask, logits, NEG_INF)
    m = jnp.max(logits, axis=-1, keepdims=True)
    e = jnp.where(mask, jnp.exp(logits - m), 0.0)
    return e / jnp.maximum(jnp.sum(e, axis=-1, keepdims=True), 1e-30)


def _t5_bucket(dist):
    exact = REL_BUCKETS // 2
    d = jnp.maximum(dist, 0)
    ratio = jnp.log(jnp.maximum(d, 1).astype(jnp.float32) / exact) / math.log(REL_MAX_DIST / exact)
    large = jnp.minimum(exact + (ratio * (REL_BUCKETS - exact)).astype(jnp.int32), REL_BUCKETS - 1)
    return jnp.where(d < exact, d, large)


def _rel_bias(rel_table, dist):
    return rel_table.astype(jnp.float32)[_t5_bucket(dist)]


def _gather_pages(pool, page_table):
    rows = pool[page_table]
    return rows.reshape(page_table.shape[0], page_table.shape[1] * pool.shape[1], *pool.shape[2:])


def _causal_conv(x, buf, w):
    T = x.shape[1]
    xp = jnp.concatenate([buf.astype(x.dtype), x], axis=1)
    y = xp[:, 0:T] * w[0]
    for i in range(1, GDN_CONV):
        y = y + xp[:, i:i + T] * w[i]
    return jax.nn.silu(y), xp[:, T:]


def _nsa_split(proj):
    B, T, _ = proj.shape
    q = proj[..., :Q_WIDTH].reshape(B, T, KV_HEADS, GROUP, HEAD_DIM)
    kv_c, kv_s, kv_w = (proj[..., Q_WIDTH + c * KV_WIDTH:Q_WIDTH + (c + 1) * KV_WIDTH]
                        .reshape(B, T, 2, KV_HEADS, HEAD_DIM) for c in range(3))
    gates = jax.nn.sigmoid(proj[..., NSA_MAIN:NSA_MAIN + 3 * N_HEADS]).reshape(B, T, 3, KV_HEADS, GROUP)
    return q, kv_c, kv_s, kv_w, gates


def _nsa_compress(kv, cmp_w1, cmp_w2, cmp_pe):
    B, Tk = kv.shape[:2]
    nc = (Tk - CMP_BLOCK) // CMP_STRIDE + 1
    n_part = CMP_BLOCK // CMP_STRIDE
    n_chunk = nc + n_part - 1
    chunks = kv[:, :n_chunk * CMP_STRIDE].astype(jnp.float32).reshape(B, n_chunk, CMP_STRIDE, 2, KV_HEADS, HEAD_DIM)
    w1 = cmp_w1.astype(jnp.float32)
    part = jnp.einsum('bnlcgd,crldh->bncrgh', chunks, w1.reshape(2, n_part, CMP_STRIDE, HEAD_DIM, CMP_HIDDEN))
    hid = jnp.einsum('cld,cldh->ch', cmp_pe.astype(jnp.float32), w1)[None, None, :, None, :]
    for r in range(n_part):
        hid = hid + part[:, r:r + nc, :, r]
    return jnp.einsum('bncgh,chd->bncgd', jax.nn.gelu(hid), cmp_w2.astype(jnp.float32))


def _nsa_cmp_attn(q, q_pos, kc, rel_table):
    nc = kc.shape[1]
    end_pos = jnp.arange(nc) * CMP_STRIDE + (CMP_BLOCK - 1)
    dist = q_pos[:, None] - end_pos[None, :]
    bias = _rel_bias(rel_table, dist).reshape(q_pos.shape[0], nc, KV_HEADS, GROUP).transpose(0, 2, 3, 1)
    logits = jnp.einsum('bqgrd,bcgd->bqgrc', q, kc[:, :, 0]) * ATTN_SCALE + bias
    p = _masked_softmax(logits, (dist >= 0)[:, None, None, :])
    return jnp.einsum('bqgrc,bcgd->bqgrd', p, kc[:, :, 1]), p


def _nsa_select(p_cmp, q_pos, tk):
    ns = -(-tk // SEL_BLOCK)
    ratio = SEL_BLOCK // CMP_STRIDE
    imp = p_cmp.sum(axis=3)
    nc = imp.shape[-1]
    imp = jnp.pad(imp, ((0, 0), (0, 0), (0, 0), (1, ratio * ns + ratio - 1 - nc)))
    score = imp[..., :ratio * ns].reshape(*imp.shape[:3], ns, ratio).sum(-1) + imp[..., ratio::ratio]
    cur = q_pos // SEL_BLOCK
    blk = jnp.arange(ns)
    forced = (blk[None, :] == 0) | (blk[None, :] == cur[:, None]) | (blk[None, :] == cur[:, None] - 1)
    causal = blk[None, :] <= cur[:, None]
    score = jnp.where(forced[None, :, None, :], FORCE_SCORE, score)
    score = jnp.where(causal[None, :, None, :], score, -1.0)
    _, idx = lax.top_k(score, min(N_SEL, ns))
    valid = idx <= cur[None, :, None, None]
    return idx, valid


def _block_mask(idx, valid, n_blocks, block, tk):
    hit = (idx[..., None] == jnp.arange(n_blocks)) & valid[..., None]
    return jnp.repeat(jnp.any(hit, axis=-2), block, axis=-1)[..., :tk]


def _dense_attn(q, q_pos, kv, key_ok, rel_table):
    tk = kv.shape[1]
    dist = q_pos[:, None] - jnp.arange(tk)[None, :]
    bias = _rel_bias(rel_table, dist).reshape(q_pos.shape[0], tk, KV_HEADS, GROUP).transpose(0, 2, 3, 1)
    logits = jnp.einsum('bqgrd,bkgd->bqgrk', q, kv[:, :, 0]) * ATTN_SCALE + bias
    p = _masked_softmax(logits, key_ok & (dist >= 0)[None, :, None, None, :])
    return jnp.einsum('bqgrk,bkgd->bqgrd', p, kv[:, :, 1])


def _band_attn(q, q_pos, kv_band, k_pos, rel_table):
    dist = q_pos[:, :, None] - k_pos[:, None, :]
    mask = (dist >= 0) & (dist < WINDOW) & (k_pos[:, None, :] >= 0)
    n, qb, kb = dist.shape
    bias = _rel_bias(rel_table, dist).reshape(n, qb, kb, KV_HEADS, GROUP).transpose(0, 1, 3, 4, 2)
    kvf = kv_band.astype(jnp.float32)
    logits = jnp.einsum('bnqgrd,bnkgd->bnqgrk', q, kvf[:, :, :, 0]) * ATTN_SCALE + bias
    p = _masked_softmax(logits, mask[:, :, None, None, :])
    return jnp.einsum('bnqgrk,bnkgd->bnqgrd', p, kvf[:, :, :, 1])


def _nsa_sample(proj, cache_c, cache_s, cache_w, page_table, cmp_w1, cmp_w2, cmp_pe, rel_table):
    B, T, _ = proj.shape
    past = page_table.shape[1] * cache_c.shape[1]
    q, kv_c, kv_s, kv_w, gates = _nsa_split(proj)
    pos = past + jnp.arange(T)
    kv_c_full = jnp.concatenate([_gather_pages(cache_c, page_table), kv_c], axis=1)
    kv_s_full = jnp.concatenate([_gather_pages(cache_s, page_table), kv_s], axis=1)
    wbuf = cache_w.shape[1]
    band = jnp.concatenate([cache_w, kv_w], axis=1)
    band_k_pos = past - wbuf + jnp.arange(wbuf + T)
    kc = _nsa_compress(kv_c_full, cmp_w1, cmp_w2, cmp_pe)
    o_cmp, p_cmp = _nsa_cmp_attn(q, pos, kc, rel_table)
    tk = kv_s_full.shape[1]
    idx, valid = _nsa_select(p_cmp, pos, tk)
    key_ok = _block_mask(idx, valid, -(-tk // SEL_BLOCK), SEL_BLOCK, tk)[:, :, :, None, :]
    o_sel = _dense_attn(q, pos, kv_s_full, key_ok, rel_table)
    o_win = _band_attn(q[:, None], pos[None, :], band[:, None], band_k_pos[None, :], rel_table)[:, 0]
    o = gates[:, :, 0, ..., None] * o_cmp + gates[:, :, 1, ..., None] * o_sel + gates[:, :, 2, ..., None] * o_win
    return o.reshape(B, T, Q_WIDTH), kv_c, kv_s, band[:, -wbuf:]


def _moba_sample(proj, cache_kv, page_table, rel_table):
    B, T, _ = proj.shape
    past = page_table.shape[1] * cache_kv.shape[1]
    q = proj[..., :Q_WIDTH].reshape(B, T, KV_HEADS, GROUP, HEAD_DIM)
    kv_new = proj[..., Q_WIDTH:].reshape(B, T, 2, KV_HEADS, HEAD_DIM)
    kv = jnp.concatenate([_gather_pages(cache_kv, page_table), kv_new], axis=1)
    q_pos = past + jnp.arange(T)
    tk = kv.shape[1]
    nb = -(-tk // MOBA_BLOCK)
    kpad = jnp.pad(kv[:, :, 0], ((0, 0), (0, nb * MOBA_BLOCK - tk), (0, 0), (0, 0)))
    kmean = jnp.mean(kpad.reshape(B, nb, MOBA_BLOCK, KV_HEADS, HEAD_DIM), axis=2)
    own = q_pos // MOBA_BLOCK
    gate = jnp.einsum('bqgrd,bngd->bqgrn', q, kmean)
    is_past = jnp.arange(nb)[None, :] < own[:, None]
    gate = jnp.where(is_past[:, None, None, :], gate, NEG_INF)
    _, idx = lax.top_k(gate, min(MOBA_TOPK, nb))
    valid = idx < own[:, None, None, None]
    own_ok = (jnp.arange(tk)[None, :] // MOBA_BLOCK) == own[:, None]
    key_ok = _block_mask(idx, valid, nb, MOBA_BLOCK, tk) | own_ok[None, :, None, None, :]
    o = _dense_attn(q, q_pos, kv, key_ok, rel_table)
    return o.reshape(B, T, Q_WIDTH), kv_new


def _gla_chunked(q, k, v, logf, s0):
    B, T, H, dk = q.shape
    dv = v.shape[-1]
    C = math.gcd(T, CHUNK)
    N = T // C
    q, k, v, logf = (a.reshape(B, N, C, *a.shape[2:]) for a in (q, k, v, logf))
    b = jnp.cumsum(logf, axis=2)
    b_ref = b[:, :, C // 2:C // 2 + 1]
    a = jnp.einsum('bnihd,bnjhd->bnhij', q * jnp.exp(b - b_ref), k * jnp.exp(b_ref - b))
    causal = jnp.arange(C)[:, None] >= jnp.arange(C)[None, :]
    a = jnp.where(causal, a, 0.0)
    o_intra = jnp.einsum('bnhij,bnjhv->bnihv', a, v)
    q_in = q * jnp.exp(b)
    k_out = k * jnp.exp(b[:, :, -1:] - b)
    d_last = jnp.exp(b[:, :, -1])

    def step(s, xs):
        q_c, k_c, v_c, d_c = xs
        o = jnp.einsum('bihd,bhdv->bihv', q_c, s)
        s = d_c[..., None] * s + jnp.einsum('bjhd,bjhv->bhdv', k_c, v_c)
        return s, o

    s, o_inter = lax.scan(step, s0, tuple(jnp.moveaxis(t, 1, 0) for t in (q_in, k_out, v, d_last)))
    o = o_intra + jnp.moveaxis(o_inter, 0, 1)
    return o.reshape(B, T, H, dv), s


def _hgrn2_core(proj, s0, lb_logits, layer, norm_g):
    B, T, _ = proj.shape
    dk = HG_HEADS * HG_DK
    dv = HG_HEADS * HG_DV
    q, f, i, g = jnp.split(proj, [dk, 2 * dk, 2 * dk + dv], axis=-1)
    p = jax.nn.softmax(lb_logits.astype(jnp.float32), axis=0)
    lb = (jnp.cumsum(p, axis=0) - p[0])[layer]
    fg = lb + (1.0 - lb) * jax.nn.sigmoid(f)
    shp = (B, T, HG_HEADS, HG_DK)
    o, s = _gla_chunked((jax.nn.silu(q) * HG_DK ** -0.5).reshape(shp), (1.0 - fg).reshape(shp),
                        i.reshape(B, T, HG_HEADS, HG_DV), jnp.log(fg).reshape(shp), s0.astype(jnp.float32))
    o = _rms_norm(o, norm_g) * jax.nn.silu(g.reshape(B, T, HG_HEADS, HG_DV))
    return o.reshape(B, T, dv), s


def _gdn_chunked(q, k, v, log_a, beta, s0):
    B, T, H, dk = q.shape
    dv = v.shape[-1]
    C = math.gcd(T, CHUNK)
    N = T // C

    def heads_first(a):
        return jnp.moveaxis(a.reshape(B, N, C, *a.shape[2:]), 3, 2)

    qh, kh, vh, bt = heads_first(q), heads_first(k), heads_first(v), heads_first(beta)
    g = jnp.cumsum(heads_first(log_a), axis=-1)
    ar = jnp.arange(C)
    strict = ar[:, None] > ar[None, :]
    incl = ar[:, None] >= ar[None, :]
    gdiff = g[..., :, None] - g[..., None, :]
    d_strict = jnp.where(strict, jnp.exp(jnp.where(strict, gdiff, 0.0)), 0.0)
    d_incl = jnp.where(incl, jnp.exp(jnp.where(incl, gdiff, 0.0)), 0.0)
    kk = jnp.einsum('bnhid,bnhjd->bnhij', kh, kh)
    m = jnp.eye(C, dtype=jnp.float32) + bt[..., :, None] * kk * d_strict
    rhs = jnp.concatenate([bt[..., None] * vh, (bt * jnp.exp(g))[..., None] * kh], axis=-1)
    sol = lax.linalg.triangular_solve(m, rhs, left_side=True, lower=True, unit_diagonal=True)
    u0, w = sol[..., :dv], sol[..., dv:]
    aq = jnp.einsum('bnhid,bnhjd->bnhij', qh, kh) * d_incl
    q_in = qh * jnp.exp(g)[..., None]
    k_out = kh * jnp.exp(g[..., -1:] - g)[..., None]
    d_last = jnp.exp(g[..., -1])

    def step(s, xs):
        u0_c, w_c, aq_c, q_c, k_c, d_c = xs
        u = u0_c - jnp.einsum('bhcd,bhdv->bhcv', w_c, s)
        o = jnp.einsum('bhcd,bhdv->bhcv', q_c, s) + jnp.einsum('bhij,bhjv->bhiv', aq_c, u)
        s = d_c[..., None, None] * s + jnp.einsum('bhcd,bhcv->bhdv', k_c, u)
        return s, o

    xs = tuple(jnp.moveaxis(a, 1, 0) for a in (u0, w, aq, q_in, k_out, d_last))
    s, o = lax.scan(step, s0, xs)
    return jnp.transpose(o, (1, 0, 3, 2, 4)).reshape(B, T, H, dv), s


def _gdn_core(proj, conv_buf, s0, conv_w, a_log, dt_bias, norm_g):
    B, T, _ = proj.shape
    vw = GDN_V_HEADS * GDN_DV
    qkv = proj[..., :GDN_CONV_DIM]
    z = proj[..., GDN_CONV_DIM:GDN_MAIN]
    b_logit = proj[..., GDN_MAIN:GDN_MAIN + GDN_V_HEADS]
    a_in = proj[..., GDN_MAIN + GDN_V_HEADS:GDN_MAIN + 2 * GDN_V_HEADS]
    conv_out, new_buf = _causal_conv(qkv, conv_buf, conv_w)
    qkw = GDN_QK_HEADS * GDN_DK
    q = jnp.repeat(_l2norm(conv_out[..., :qkw].reshape(B, T, GDN_QK_HEADS, GDN_DK)) * GDN_DK ** -0.5, GDN_REP, axis=2)
    k = jnp.repeat(_l2norm(conv_out[..., qkw:2 * qkw].reshape(B, T, GDN_QK_HEADS, GDN_DK)), GDN_REP, axis=2)
    v = conv_out[..., 2 * qkw:].reshape(B, T, GDN_V_HEADS, GDN_DV)
    beta = jax.nn.sigmoid(b_logit)
    log_a = -jnp.exp(a_log.astype(jnp.float32)) * jax.nn.softplus(a_in + dt_bias.astype(jnp.float32))
    o, s = _gdn_chunked(q, k, v, log_a, beta, s0.astype(jnp.float32))
    o = _rms_norm(o, norm_g) * jax.nn.silu(z.reshape(B, T, GDN_V_HEADS, GDN_DV))
    return o.reshape(B, T, vw), new_buf, s


def _pad_cols(w, mult=LANE):
    n = w.shape[1]
    return jnp.pad(w, ((0, 0), (0, (-n) % mult)))


def _nsa_prompt(main, tail, batch, seq, cmp_w1, cmp_w2, cmp_pe, rel_table):
    kv = main[:, Q_WIDTH:].reshape(batch, seq, 3, 2, KV_HEADS, HEAD_DIM)
    kv_c, kv_s, kv_w = kv[:, :, 0], kv[:, :, 1], kv[:, :, 2]
    kc = compress_finish(compress_part_rows(main, Q_WIDTH // KV_WIDTH, cmp_w1, batch, seq), cmp_w1, cmp_w2, cmp_pe)
    o_cmp, pen = cmp_select(main, tail, kc, _cmp_bias_table(rel_table, kc.shape[3]), batch, seq)
    col = Q_WIDTH // HEAD_DIM
    o_sel = flash_attention(main, main, col + 2 * KV_HEADS, col + 3 * KV_HEADS, _flash_bias_tiles(rel_table, 0),
                            batch, seq, pen=pen, pen_block=SEL_BLOCK, gate_arr=tail, gate_col0=N_HEADS)
    o_win = flash_attention(main, main, col + 4 * KV_HEADS, col + 5 * KV_HEADS,
                            _flash_bias_tiles(rel_table, WINDOW), batch, seq, k_back=WINDOW // ATTN_TILE,
                            gate_arr=tail, gate_col0=2 * N_HEADS)
    return [o_cmp, o_sel, o_win], kv_c, kv_s, kv_w[:, -min(WINDOW, seq):]


def _moba_prompt(proj, batch, seq, rel_table):
    col = Q_WIDTH // HEAD_DIM
    pen = moba_gate(proj, col, batch, seq)
    o = flash_attention(proj, proj, col, col + KV_HEADS, _flash_bias_tiles(rel_table, 0), batch, seq,
                        pen=pen, pen_block=MOBA_BLOCK)
    return o, proj[:, Q_WIDTH:].reshape(batch, seq, 2, KV_HEADS, HEAD_DIM)


def kernel(x_prompt, x_sample, cache_nsa_cmp_kv, cache_nsa_sel_kv, cache_nsa_win_kv, cache_moba_kv,
           state_hgrn2, state_gdn_conv, state_gdn_ssm, page_table, rel_table, ln_mix, ln_ffn, ln_final,
           ffn_w_up, ffn_w_down, nsa_w_in, nsa_cmp_w1, nsa_cmp_w2, nsa_cmp_pe, nsa_w_out, moba_w_in, moba_w_out,
           hg_w_in, hg_lb_logits, hg_norm, hg_w_out, gdn_w_in, gdn_conv_w, gdn_a_log, gdn_dt_bias, gdn_norm,
           gdn_w_out):
    bf = jnp.bfloat16
    bp, tp = x_prompt.shape[:2]
    bs, ts = x_sample.shape[:2]
    assert tp % ATTN_TILE == 0 and WINDOW % ATTN_TILE == 0 and ATTN_TILE == MOBA_BLOCK and tp % SCAN_TILE == 0
    xp = x_prompt.reshape(bp * tp, D_MODEL)
    xs = x_sample.reshape(bs * ts, D_MODEL)

    for layer in range(DEPTH):
        kind = layer % N_MIXERS
        g_mix = ln_mix[layer]
        if kind == 0:
            w_main, w_tail = nsa_w_in[:, :NSA_MAIN].astype(bf), _pad_cols(nsa_w_in[:, NSA_MAIN:]).astype(bf)
            main_p, tail_p = norm_matmul(xp, g_mix, w_main), norm_matmul(xp, g_mix, w_tail)
            main_s, tail_s = norm_matmul(xs, g_mix, w_main), norm_matmul(xs, g_mix, w_tail)
            op, nsa_cmp_p, nsa_sel_p, nsa_win_p = _nsa_prompt(main_p, tail_p, bp, tp, nsa_cmp_w1, nsa_cmp_w2,
                                                              nsa_cmp_pe, rel_table)
            os_, nsa_cmp_s, nsa_sel_s, nsa_win_s = _nsa_sample_pallas(main_s, tail_s, cache_nsa_cmp_kv,
                                                                     cache_nsa_sel_kv, cache_nsa_win_kv, page_table,
                                                                     nsa_cmp_w1, nsa_cmp_w2, nsa_cmp_pe, rel_table)
            w_out = nsa_w_out.astype(bf)
        elif kind == 1:
            w_in = moba_w_in.astype(bf)
            pp = norm_matmul(xp, g_mix, w_in)
            op, moba_p = _moba_prompt(pp, bp, tp, rel_table)
            os_, moba_s = _moba_sample_pallas(norm_matmul(xs, g_mix, w_in), cache_moba_kv, page_table, rel_table)
            op, os_ = [op], [os_]
            w_out = moba_w_out.astype(bf)
        elif kind == 2:
            w_in = hg_w_in.astype(bf)
            pp = norm_matmul(xp, g_mix, w_in)
            ps = norm_matmul(xs, g_mix, w_in).reshape(bs, ts, -1)
            s0 = jnp.zeros((bp, HG_HEADS, HG_DK, HG_DV), jnp.float32)
            op, hg_p = hgrn2_scan(pp, hg_lb_logits, hg_norm, s0, layer, bp, tp)
            os_, hg_s = _hgrn2_core(ps, state_hgrn2, hg_lb_logits, layer, hg_norm)
            op, os_ = [op], [os_.reshape(bs * ts, -1)]
            w_out = hg_w_out.astype(bf)
        else:
            w_main, w_tail = gdn_w_in[:, :GDN_MAIN].astype(bf), _pad_cols(gdn_w_in[:, GDN_MAIN:]).astype(bf)
            main_p, tail_p = norm_matmul(xp, g_mix, w_main), norm_matmul(xp, g_mix, w_tail)
            ps = jnp.concatenate([norm_matmul(xs, g_mix, w_main), norm_matmul(xs, g_mix, w_tail)], axis=-1)
            buf0 = jnp.zeros((bp, GDN_CONV - 1, GDN_CONV_DIM), jnp.float32)
            s0 = jnp.zeros((bp, GDN_V_HEADS, GDN_DK, GDN_DV), jnp.float32)
            op, ssm_p = gdn_scan(main_p, tail_p, buf0, gdn_conv_w, gdn_a_log, gdn_dt_bias, gdn_norm, s0, bp, tp)
            conv_p = main_p.reshape(bp, tp, -1)[:, tp - (GDN_CONV - 1):, :GDN_CONV_DIM]
            os_, conv_s, ssm_s = _gdn_core(ps.reshape(bs, ts, -1), state_gdn_conv, state_gdn_ssm, gdn_conv_w,
                                           gdn_a_log, gdn_dt_bias, gdn_norm)
            op, os_ = [op], [os_.reshape(bs * ts, -1)]
            w_out = gdn_w_out.astype(bf)
        xp = matmul_res(op, w_out, xp)
        xs = matmul_res(os_, w_out, xs)
        w_up, w_down = ffn_w_up[layer].astype(bf), ffn_w_down[layer].astype(bf)
        xp = ffn(xp, ln_ffn[layer], w_up, w_down)
        xs = ffn(xs, ln_ffn[layer], w_up, w_down)
    y_prompt = final_norm(xp, ln_final).reshape(bp, tp, D_MODEL)
    y_sample = final_norm(xs, ln_final).reshape(bs, ts, D_MODEL)
    return (y_prompt, y_sample, nsa_cmp_p, nsa_cmp_s, nsa_sel_p, nsa_sel_s, nsa_win_p, nsa_win_s,
            moba_p, moba_s, hg_p, hg_s, conv_p, conv_s, ssm_p, ssm_s)
```

```python
import functools
import math

import jax
import jax.numpy as jnp
import numpy as np
from jax import lax
from jax.experimental import pallas as pl
from jax.experimental.pallas import tpu as pltpu

D_MODEL = 2048
DEPTH = 4
N_MIXERS = 4
HEAD_DIM = 128
N_HEADS = D_MODEL // HEAD_DIM
KV_HEADS = 4
GROUP = N_HEADS // KV_HEADS
ATTN_SCALE = HEAD_DIM ** -0.5
REL_BUCKETS = 32
REL_MAX_DIST = 128
CMP_BLOCK = 32
CMP_STRIDE = 16
CMP_HIDDEN = HEAD_DIM
SEL_BLOCK = 64
N_SEL = 16
WINDOW = 512
FORCE_SCORE = 1.0e4
MOBA_BLOCK = 256
MOBA_TOPK = 3
HG_DK = 128
HG_HEADS = D_MODEL // HG_DK
HG_DV = D_MODEL // HG_HEADS
GDN_DK = 128
GDN_DV = 128
GDN_QK_HEADS = D_MODEL // GDN_DK
GDN_V_HEADS = 2 * GDN_QK_HEADS
GDN_REP = GDN_V_HEADS // GDN_QK_HEADS
GDN_CONV = 4
GDN_CONV_DIM = 2 * GDN_QK_HEADS * GDN_DK + GDN_V_HEADS * GDN_DV
CHUNK = 64
NEG_INF = -1.0e30
NORM_EPS = 1e-6

Q_WIDTH = N_HEADS * HEAD_DIM
KV_WIDTH = 2 * KV_HEADS * HEAD_DIM
NSA_MAIN = Q_WIDTH + 3 * KV_WIDTH
GDN_MAIN = GDN_CONV_DIM + GDN_V_HEADS * GDN_DV

V7X_VMEM_LIMIT_BYTES = 56 * 1024 * 1024
LANE = 128
SUBLANE = 8
ATTN_TILE = 256
CMP_TILE = 128
SCAN_TILE = 512
LOG2E = math.log2(math.e)
NT_DIMS = (((1,), (1,)), ((), ()))
TN_DIMS = (((0,), (0,)), ((), ()))


def _cparams(*sem):
    return pltpu.CompilerParams(dimension_semantics=sem, vmem_limit_bytes=V7X_VMEM_LIMIT_BYTES)


def _row_tile(m, target):
    t = min(m, target)
    while m % t:
        t //= 2
    return t


def _col_tile(n, target):
    t = min(n, target)
    while n % t or t % LANE:
        t -= LANE
    return t


def _split_bf16(x, parts):
    out = []
    for _ in range(parts - 1):
        hi = x.astype(jnp.bfloat16)
        out.append(hi)
        x = x - hi.astype(jnp.float32)
    out.append(x.astype(jnp.bfloat16))
    return out


def _bf(x):
    return x.astype(jnp.bfloat16)


def _dot(a, b, dims=None):
    if dims is None:
        return jnp.dot(a, b, preferred_element_type=jnp.float32)
    return lax.dot_general(a, b, dims, preferred_element_type=jnp.float32)


def _norm_matmul_body(x_ref, g_ref, w_ref, o_ref, h_ref):
    @pl.when(pl.program_id(1) == 0)
    def _():
        x = x_ref[...]
        ms = jnp.mean(x * x, axis=-1, keepdims=True)
        h_ref[...] = _bf(x * lax.rsqrt(ms + NORM_EPS) * g_ref[...])

    o_ref[...] = _dot(h_ref[...], w_ref[...])


def norm_matmul(x, g, w):
    m, k = x.shape
    n = w.shape[1]
    tm = _row_tile(m, 1024)
    tn = _col_tile(n, 1024)
    return pl.pallas_call(
        _norm_matmul_body,
        grid=(m // tm, n // tn),
        in_specs=[pl.BlockSpec((tm, k), lambda i, j: (i, 0)),
                  pl.BlockSpec((1, k), lambda i, j: (0, 0)),
                  pl.BlockSpec((k, tn), lambda i, j: (0, j))],
        out_specs=pl.BlockSpec((tm, tn), lambda i, j: (i, j)),
        out_shape=jax.ShapeDtypeStruct((m, n), jnp.float32),
        scratch_shapes=[pltpu.VMEM((tm, k), jnp.bfloat16)],
        compiler_params=_cparams("parallel", "arbitrary"),
        name="norm_matmul",
    )(x, g.reshape(1, k), w)


def _matmul_res_body(*refs):
    *a_refs, w_ref, r_ref, o_ref = refs
    a = a_refs[0][...]
    for a_ref in a_refs[1:]:
        a = a + a_ref[...]
    o_ref[...] = r_ref[...] + _dot(_bf(a), w_ref[...])


def matmul_res(a_list, w, res):
    m, k = a_list[0].shape
    n = w.shape[1]
    tm = _row_tile(m, 512)
    tn = _col_tile(n, 1024)
    return pl.pallas_call(
        _matmul_res_body,
        grid=(m // tm, n // tn),
        in_specs=[pl.BlockSpec((tm, k), lambda i, j: (i, 0)) for _ in a_list]
        + [pl.BlockSpec((k, tn), lambda i, j: (0, j)),
           pl.BlockSpec((tm, tn), lambda i, j: (i, j))],
        out_specs=pl.BlockSpec((tm, tn), lambda i, j: (i, j)),
        out_shape=jax.ShapeDtypeStruct((m, n), jnp.float32),
        compiler_params=_cparams("parallel", "arbitrary"),
        name="matmul_res",
    )(*a_list, w, res)


def _ffn_body(x_ref, g_ref, wa_ref, wb_ref, wd_ref, o_ref, h_ref, acc_ref):
    j = pl.program_id(1)

    @pl.when(j == 0)
    def _():
        x = x_ref[...]
        ms = jnp.mean(x * x, axis=-1, keepdims=True)
        h_ref[...] = _bf(x * lax.rsqrt(ms + NORM_EPS) * g_ref[...])
        acc_ref[...] = jnp.zeros_like(acc_ref)

    h = h_ref[...]
    a = _dot(h, wa_ref[...])
    b = _dot(h, wb_ref[...])
    acc_ref[...] += _dot(_bf(a * jax.nn.sigmoid(a) * b), wd_ref[...])

    @pl.when(j == pl.num_programs(1) - 1)
    def _():
        o_ref[...] = x_ref[...] + acc_ref[...]


def ffn(x, g, w_up, w_down):
    m, k = x.shape
    hdim = w_down.shape[0]
    tm = _row_tile(m, 512)
    th = _col_tile(hdim, 512)
    nh = hdim // th
    return pl.pallas_call(
        _ffn_body,
        grid=(m // tm, nh),
        in_specs=[pl.BlockSpec((tm, k), lambda i, j: (i, 0)),
                  pl.BlockSpec((1, k), lambda i, j: (0, 0)),
                  pl.BlockSpec((k, th), lambda i, j: (0, j)),
                  pl.BlockSpec((k, th), lambda i, j: (0, j + nh)),
                  pl.BlockSpec((th, k), lambda i, j: (j, 0))],
        out_specs=pl.BlockSpec((tm, k), lambda i, j: (i, 0)),
        out_shape=jax.ShapeDtypeStruct((m, k), jnp.float32),
        scratch_shapes=[pltpu.VMEM((tm, k), jnp.bfloat16), pltpu.VMEM((tm, k), jnp.float32)],
        compiler_params=_cparams("parallel", "arbitrary"),
        name="ffn",
    )(x, g.reshape(1, k), w_up, w_up, w_down)


def _norm_body(x_ref, g_ref, o_ref):
    x = x_ref[...]
    ms = jnp.mean(x * x, axis=-1, keepdims=True)
    o_ref[...] = x * lax.rsqrt(ms + NORM_EPS) * g_ref[...]


def final_norm(x, g):
    m, k = x.shape
    tm = _row_tile(m, 512)
    return pl.pallas_call(
        _norm_body,
        grid=(m // tm,),
        in_specs=[pl.BlockSpec((tm, k), lambda i: (i, 0)), pl.BlockSpec((1, k), lambda i: (0, 0))],
        out_specs=pl.BlockSpec((tm, k), lambda i: (i, 0)),
        out_shape=jax.ShapeDtypeStruct((m, k), jnp.float32),
        compiler_params=_cparams("parallel"),
        name="final_norm",
    )(x, g.reshape(1, k))


def _bucket_np(dist):
    exact = REL_BUCKETS // 2
    d = np.maximum(dist, 0)
    ratio = np.log(np.maximum(d, 1).astype(np.float32) / exact) / math.log(REL_MAX_DIST / exact)
    large = np.minimum(exact + (ratio * (REL_BUCKETS - exact)).astype(np.int32), REL_BUCKETS - 1)
    return np.where(d < exact, d, large)


def _bias_lookup(rel_table, dist):
    bucket = _bucket_np(dist).astype(np.int32)
    ids = [int(b) for b in np.unique(bucket)]
    bk = jnp.asarray(bucket)[..., None]
    out = jnp.broadcast_to(rel_table[ids[0]], bucket.shape + (rel_table.shape[1],))
    for b in ids[1:]:
        out = jnp.where(bk == b, rel_table[b], out)
    return out


def _heads_to_lanes(t):
    keys, queries, _ = t.shape
    return t.reshape(keys, queries, KV_HEADS, GROUP).transpose(2, 0, 3, 1).reshape(KV_HEADS, keys, GROUP * queries)


def _flash_bias_tiles(rel_table, window):
    j = np.arange(ATTN_TILE)[:, None]
    i = np.arange(ATTN_TILE)[None, :]
    n_cls = window // ATTN_TILE + 1 if window else -(-REL_MAX_DIST // ATTN_TILE) + 2
    tiles = []
    for d in range(n_cls):
        dist = d * ATTN_TILE + i - j
        ok = dist >= 0
        if window:
            ok = ok & (dist < window)
        tiles.append(_heads_to_lanes(jnp.where(ok[..., None], _bias_lookup(rel_table, dist) * LOG2E, NEG_INF)))
    return jnp.stack(tiles, axis=1)


def _cmp_bias_table(rel_table, ncp):
    x = np.arange(ncp)[:, None]
    i = np.arange(CMP_TILE)[None, :]
    dist = i - CMP_STRIDE * (x - 16) - (CMP_BLOCK - 1)
    far = rel_table[REL_BUCKETS - 1]
    b = _heads_to_lanes(jnp.where((dist >= 0)[..., None], _bias_lookup(rel_table, dist), far))
    return jnp.concatenate([b, b], axis=1)


def _stack_heads(q):
    return jnp.concatenate([q[:, r * HEAD_DIM:(r + 1) * HEAD_DIM] for r in range(GROUP)], axis=0)


def _gate_columns(gt_ref, col0):
    gt = jax.nn.sigmoid(gt_ref[...])
    lane = lax.broadcasted_iota(jnp.int32, gt.shape, 1)
    return [jnp.sum(jnp.where(lane == col0 + r, gt, 0.0), axis=1, keepdims=True) for r in range(GROUP)]


def _heads_from_lanes(o_t, rows, cols=None):
    parts = []
    for r in range(GROUP):
        part = o_t[:, r * rows:(r + 1) * rows].T
        if cols is not None:
            part = part * cols[r]
        parts.append(part)
    return jnp.concatenate(parts, axis=1)


def _rank_rows(score, n_rows):
    row = lax.broadcasted_iota(jnp.int32, score.shape, 0)
    rank = jnp.zeros(score.shape, jnp.int32)
    for mm in range(n_rows):
        sm = score[mm:mm + 1, :]
        ahead = (sm > score) | ((sm == score) & (row > mm))
        rank = rank + ahead.astype(jnp.int32)
    return rank


def _cmp_select_body(q_ref, kc_ref, dt_ref, gt_ref, o_ref, pen_ref, *, n_sel_blocks):
    g = pl.program_id(1)
    qi = pl.program_id(2)
    tq = CMP_TILE
    cols = GROUP * tq
    ncp = kc_ref.shape[1]
    q4 = _bf(_stack_heads(q_ref[...]) * ATTN_SCALE)
    s = _dot(_bf(kc_ref[0]), q4, NT_DIMS)
    shift = (qi * (tq // CMP_STRIDE) + ncp - 16) % ncp
    bias = dt_ref[pl.ds(pl.multiple_of(ncp - shift, SUBLANE), ncp), :]
    t_col = qi * tq + (lax.broadcasted_iota(jnp.int32, (ncp, cols), 1) & (tq - 1))
    end_pos = lax.broadcasted_iota(jnp.int32, (ncp, cols), 0) * CMP_STRIDE + (CMP_BLOCK - 1)
    mask = t_col >= end_pos
    s = jnp.where(mask, s + bias, NEG_INF)
    m = jnp.max(s, axis=0, keepdims=True)
    e = jnp.where(mask, jnp.exp(s - m), 0.0)
    p = e / jnp.maximum(jnp.sum(e, axis=0, keepdims=True), 1e-30)
    o_t = _dot(_bf(kc_ref[1].T), _bf(p))
    o_ref[...] = _heads_from_lanes(o_t, tq, _gate_columns(gt_ref, g * GROUP))

    imp = p[:, 0:tq]
    for r in range(1, GROUP):
        imp = imp + p[:, r * tq:(r + 1) * tq]
    ratio = SEL_BLOCK // CMP_STRIDE
    j_i = lax.broadcasted_iota(jnp.int32, (n_sel_blocks, ncp), 0)
    c_i = lax.broadcasted_iota(jnp.int32, (n_sel_blocks, ncp), 1)
    w = _bf((c_i >= ratio * j_i - 1) & (c_i <= ratio * j_i + ratio - 1))
    score = sum(_dot(w, part) for part in _split_bf16(imp, 3))
    blk = lax.broadcasted_iota(jnp.int32, (n_sel_blocks, tq), 0)
    tok = qi * tq + lax.broadcasted_iota(jnp.int32, (n_sel_blocks, tq), 1)
    cur = lax.shift_right_logical(tok, int(math.log2(SEL_BLOCK)))
    forced = (blk == 0) | (blk == cur) | (blk == cur - 1)
    causal = blk <= cur
    score = jnp.where(forced, FORCE_SCORE, score)
    score = jnp.where(causal, score, -1.0)
    chosen = (_rank_rows(score, n_sel_blocks) < N_SEL) & causal
    pen_ref[...] = jnp.where(chosen, 0.0, NEG_INF)


def cmp_select(proj, tail, kc, dt, batch, seq):
    tq = CMP_TILE
    nq = seq // tq
    ncp = kc.shape[3]
    nsb = seq // SEL_BLOCK
    return pl.pallas_call(
        functools.partial(_cmp_select_body, n_sel_blocks=nsb),
        grid=(batch, KV_HEADS, nq),
        in_specs=[pl.BlockSpec((tq, GROUP * HEAD_DIM), lambda b, g, i: (b * nq + i, g)),
                  pl.BlockSpec((None, None, 2, ncp, HEAD_DIM), lambda b, g, i: (b, g, 0, 0, 0)),
                  pl.BlockSpec((None, 2 * ncp, GROUP * tq), lambda b, g, i: (g, 0, 0)),
                  pl.BlockSpec((tq, LANE), lambda b, g, i: (b * nq + i, 0))],
        out_specs=[pl.BlockSpec((tq, GROUP * HEAD_DIM), lambda b, g, i: (b * nq + i, g)),
                   pl.BlockSpec((None, None, nsb, tq), lambda b, g, i: (b, g, 0, i))],
        out_shape=[jax.ShapeDtypeStruct((batch * seq, Q_WIDTH), jnp.float32),
                   jax.ShapeDtypeStruct((batch, KV_HEADS, nsb, seq), jnp.float32)],
        compiler_params=_cparams("parallel", "parallel", "arbitrary"),
        name="cmp_select",
    )(proj, kc, dt, tail)


def _flash_body(*refs, pen_block, pen_per_head, k_back, gate_col0, seq):
    it = iter(refs)
    q_ref, k_ref, v_ref, bt_ref = next(it), next(it), next(it), next(it)
    pen_ref = next(it) if pen_block else None
    gt_ref = next(it) if gate_col0 is not None else None
    o_ref, m_ref, acc_ref, qa_ref, kb_ref, vt_ref, sa_ref, sb_ref = (next(it) for _ in range(8))
    g = pl.program_id(1)
    qi = pl.program_id(2)
    tq = tk = ATTN_TILE
    n_cls = bt_ref.shape[0]

    @pl.when(qi == 0)
    def _():
        vt_ref[HEAD_DIM:, :] = jnp.ones((SUBLANE, seq), jnp.bfloat16)
        for c in range(seq // tk):
            kb_ref[c * tk:(c + 1) * tk, :] = _bf(k_ref[c * tk:(c + 1) * tk, :])
            vt_ref[0:HEAD_DIM, c * tk:(c + 1) * tk] = _bf(v_ref[c * tk:(c + 1) * tk, :].T)

    qa_ref[...] = _bf(_stack_heads(q_ref[...]) * (ATTN_SCALE * LOG2E))
    m_ref[...] = jnp.full(m_ref.shape, NEG_INF, jnp.float32)
    acc_ref[...] = jnp.zeros(acc_ref.shape, jnp.float32)

    def raw_logits(kj):
        k0 = pl.multiple_of(jnp.minimum(kj, qi) * tk, tk)
        return _dot(kb_ref[pl.ds(k0, tk), :], qa_ref[...], NT_DIMS)

    def attend(s_ref, kj):
        kc = jnp.minimum(kj, qi)
        k0 = pl.multiple_of(kc * tk, tk)
        skip = jnp.where(kj <= qi, 0.0, NEG_INF)
        bias = bt_ref[jnp.minimum(qi - kc, n_cls - 1)]
        if pen_block:
            per_tile = tk // pen_block
            pieces = []
            for a in range(per_tile):
                pen = pen_ref[pl.ds(kc * per_tile + a, 1), :] + skip
                if not pen_per_head:
                    pen = jnp.concatenate([pen] * GROUP, axis=1)
                rows = slice(a * pen_block, (a + 1) * pen_block)
                pieces.append(s_ref[rows, :] + bias[rows] + pen)
            s = pieces[0] if per_tile == 1 else jnp.concatenate(pieces, axis=0)
        else:
            s = s_ref[...] + bias + skip
        m_prev = m_ref[...]
        m_new = jnp.maximum(m_prev, jnp.max(s, axis=0, keepdims=True))
        p = jnp.exp2(s - m_new)
        acc_ref[...] = jnp.exp2(m_prev - m_new) * acc_ref[...] + _dot(vt_ref[:, pl.ds(k0, tk)], _bf(p))
        m_ref[...] = m_new

    k_lo = jnp.maximum(qi - k_back, 0) if k_back is not None else 0
    sa_ref[...] = raw_logits(k_lo)

    def pair(pi, carry):
        ka = k_lo + 2 * pi
        sb_ref[...] = raw_logits(ka + 1)
        attend(sa_ref, ka)
        sa_ref[...] = raw_logits(ka + 2)
        attend(sb_ref, ka + 1)
        return carry

    lax.fori_loop(0, (qi - k_lo + 2) // 2, pair, 0)
    acc = acc_ref[...]
    o_t = acc[0:HEAD_DIM] / jnp.maximum(acc[HEAD_DIM:HEAD_DIM + 1], 1e-30)
    cols_g = _gate_columns(gt_ref, gate_col0 + g * GROUP) if gate_col0 is not None else None
    o_ref[...] = _heads_from_lanes(o_t, tq, cols_g)


def flash_attention(q_arr, kv_arr, k_col, v_col, bias, batch, seq, *, pen=None, pen_block=0,
                    k_back=None, gate_arr=None, gate_col0=None):
    tq = ATTN_TILE
    nq = seq // tq
    cols = GROUP * tq
    assert k_back is None or k_back == bias.shape[1] - 1
    in_specs = [pl.BlockSpec((tq, GROUP * HEAD_DIM), lambda b, g, i: (b * nq + i, g)),
                pl.BlockSpec((seq, HEAD_DIM), lambda b, g, i: (b, k_col + g)),
                pl.BlockSpec((seq, HEAD_DIM), lambda b, g, i: (b, v_col + g)),
                pl.BlockSpec((None,) + bias.shape[1:], lambda b, g, i: (g, 0, 0, 0))]
    args = [q_arr, kv_arr, kv_arr, bias]
    pen_per_head = False
    if pen is not None:
        if pen.ndim == 4:
            in_specs.append(pl.BlockSpec((None, None, pen.shape[2], tq), lambda b, g, i: (b, g, 0, i)))
        else:
            pen_per_head = True
            in_specs.append(pl.BlockSpec((None, None, None, pen.shape[3], cols), lambda b, g, i: (b, g, i, 0, 0)))
        args.append(pen)
    if gate_arr is not None:
        in_specs.append(pl.BlockSpec((tq, LANE), lambda b, g, i: (b * nq + i, 0)))
        args.append(gate_arr)
    return pl.pallas_call(
        functools.partial(_flash_body, pen_block=pen_block if pen is not None else 0, pen_per_head=pen_per_head,
                          k_back=k_back, gate_col0=gate_col0 if gate_arr is not None else None, seq=seq),
        grid=(batch, KV_HEADS, nq),
        in_specs=in_specs,
        out_specs=pl.BlockSpec((tq, GROUP * HEAD_DIM), lambda b, g, i: (b * nq + i, g)),
        out_shape=jax.ShapeDtypeStruct((batch * seq, Q_WIDTH), jnp.float32),
        scratch_shapes=[pltpu.VMEM((1, cols), jnp.float32),
                        pltpu.VMEM((HEAD_DIM + SUBLANE, cols), jnp.float32),
                        pltpu.VMEM((cols, HEAD_DIM), jnp.bfloat16),
                        pltpu.VMEM((seq, HEAD_DIM), jnp.bfloat16),
                        pltpu.VMEM((HEAD_DIM + SUBLANE, seq), jnp.bfloat16),
                        pltpu.VMEM((tq, cols), jnp.float32), pltpu.VMEM((tq, cols), jnp.float32)],
        compiler_params=_cparams("parallel", "parallel", "arbitrary"),
        name="flash_attention",
    )(*args)


def _moba_gate_body(q_ref, k_ref, pen_ref, km_ref, *, n_blocks):
    qi = pl.program_id(2)
    tq = ATTN_TILE
    cols = GROUP * tq

    @pl.when(qi == 0)
    def _():
        k = k_ref[...]
        km_ref[...] = jnp.sum(k.reshape(n_blocks, MOBA_BLOCK, HEAD_DIM), axis=1) / MOBA_BLOCK

    qh, ql = _split_bf16(_stack_heads(q_ref[...]), 2)
    kh, kl = _split_bf16(km_ref[...], 2)
    gate = _dot(kh, qh, NT_DIMS) + _dot(kh, ql, NT_DIMS) + _dot(kl, qh, NT_DIMS)
    blk = lax.broadcasted_iota(jnp.int32, (n_blocks, cols), 0)
    tok = qi * tq + (lax.broadcasted_iota(jnp.int32, (n_blocks, cols), 1) & (tq - 1))
    own = lax.shift_right_logical(tok, int(math.log2(MOBA_BLOCK)))
    gate = jnp.where(blk < own, gate, NEG_INF)
    chosen = ((_rank_rows(gate, n_blocks) < MOBA_TOPK) & (blk < own)) | (blk == own)
    pen_ref[...] = jnp.where(chosen, 0.0, NEG_INF)


def moba_gate(proj, k_col, batch, seq):
    tq = ATTN_TILE
    nq = seq // tq
    cols = GROUP * tq
    nb = seq // MOBA_BLOCK
    return pl.pallas_call(
        functools.partial(_moba_gate_body, n_blocks=nb),
        grid=(batch, KV_HEADS, nq),
        in_specs=[pl.BlockSpec((tq, GROUP * HEAD_DIM), lambda b, g, i: (b * nq + i, g)),
                  pl.BlockSpec((seq, HEAD_DIM), lambda b, g, i: (b, k_col + g))],
        out_specs=pl.BlockSpec((None, None, None, nb, cols), lambda b, g, i: (b, g, i, 0, 0)),
        out_shape=jax.ShapeDtypeStruct((batch, KV_HEADS, nq, nb, cols), jnp.float32),
        scratch_shapes=[pltpu.VMEM((nb, HEAD_DIM), jnp.float32)],
        compiler_params=_cparams("parallel", "parallel", "arbitrary"),
        name="moba_gate",
    )(proj, proj)


def _tril_ones(n, strict=False):
    r = lax.broadcasted_iota(jnp.int32, (n, n), 0)
    c = lax.broadcasted_iota(jnp.int32, (n, n), 1)
    return (r > c) if strict else (r >= c)


def _chunk_cumsum(x):
    tril = _bf(_tril_ones(x.shape[0]))
    return sum(_dot(tril, part) for part in _split_bf16(x, 3))


def _head_rms_gate(o, norm_g, gate):
    ms = jnp.mean(o * o, axis=-1, keepdims=True)
    return o * lax.rsqrt(ms + NORM_EPS) * norm_g * (gate * jax.nn.sigmoid(gate))


def _hgrn2_body(q_ref, f_ref, i_ref, g_ref, lbl_ref, ng_ref, s0_ref, o_ref, s_out_ref, st_ref, *, layer):
    ti = pl.program_id(2)

    @pl.when(ti == 0)
    def _():
        st_ref[...] = s0_ref[...].T

    lbl = lbl_ref[...]
    e = jnp.exp(lbl - jnp.max(lbl, axis=0, keepdims=True))
    p = e / jnp.sum(e, axis=0, keepdims=True)
    lb = jnp.zeros((1, HG_DK), jnp.float32)
    for r in range(1, layer + 1):
        lb = lb + p[r:r + 1]
    causal = _tril_ones(CHUNK)
    work = []
    for c in range(q_ref.shape[0] // CHUNK):
        sl = slice(c * CHUNK, (c + 1) * CHUNK)
        q = q_ref[sl, :]
        qh = q * jax.nn.sigmoid(q) * HG_DK ** -0.5
        fg = lb + (1.0 - lb) * jax.nn.sigmoid(f_ref[sl, :])
        k = 1.0 - fg
        v = _bf(i_ref[sl, :])
        b = _chunk_cumsum(jnp.log(fg))
        b_mid = b[CHUNK // 2:CHUNK // 2 + 1]
        b_last = b[CHUNK - 1:CHUNK]
        a = _dot(_bf(qh * jnp.exp(b - b_mid)), _bf(k * jnp.exp(b_mid - b)), NT_DIMS)
        a = jnp.where(causal, a, 0.0)
        work.append((sl, _dot(_bf(a), v), _bf(qh * jnp.exp(b)), jnp.exp(b_last),
                     _dot(v, _bf(k * jnp.exp(b_last - b)), TN_DIMS)))
    for sl, o_intra, q_in, d_last, kv in work:
        st = st_ref[...]
        o = o_intra + _dot(q_in, _bf(st), NT_DIMS)
        st_ref[...] = st * d_last + kv
        o_ref[sl, :] = _head_rms_gate(o, ng_ref[...], g_ref[sl, :])

    @pl.when(ti == pl.num_programs(2) - 1)
    def _():
        s_out_ref[...] = st_ref[...].T


def hgrn2_scan(proj, lb_logits, norm_g, s0, layer, batch, seq):
    tt = _row_tile(seq, SCAN_TILE)
    nt = seq // tt
    h = HG_HEADS

    def col(k):
        return pl.BlockSpec((tt, HG_DK), lambda b, hh, t: (b * nt + t, k * h + hh))

    return pl.pallas_call(
        functools.partial(_hgrn2_body, layer=layer),
        grid=(batch, h, nt),
        in_specs=[col(0), col(1), col(2), col(3),
                  pl.BlockSpec((DEPTH, HG_DK), lambda b, hh, t: (0, hh)),
                  pl.BlockSpec((1, HG_DV), lambda b, hh, t: (0, 0)),
                  pl.BlockSpec((None, None, HG_DK, HG_DV), lambda b, hh, t: (b, hh, 0, 0))],
        out_specs=[pl.BlockSpec((tt, HG_DV), lambda b, hh, t: (b * nt + t, hh)),
                   pl.BlockSpec((None, None, HG_DK, HG_DV), lambda b, hh, t: (b, hh, 0, 0))],
        out_shape=[jax.ShapeDtypeStruct((batch * seq, h * HG_DV), jnp.float32),
                   jax.ShapeDtypeStruct((batch, h, HG_DK, HG_DV), jnp.float32)],
        scratch_shapes=[pltpu.VMEM((HG_DV, HG_DK), jnp.float32)],
        compiler_params=_cparams("parallel", "parallel", "arbitrary"),
        name="hgrn2_scan",
    )(proj, proj, proj, proj, lb_logits, norm_g.reshape(1, HG_DV), s0)


def _lane_column(x, lane_idx):
    lane = lax.broadcasted_iota(jnp.int32, x.shape, 1)
    return jnp.sum(jnp.where(lane == lane_idx, x, 0.0), axis=1, keepdims=True)


def _softplus(x):
    return jnp.maximum(x, 0.0) + jnp.log(1.0 + jnp.exp(-jnp.abs(x)))


def _l2n(x):
    return x * lax.rsqrt(jnp.sum(x * x, axis=-1, keepdims=True) + NORM_EPS)


def _gdn_body(q_ref, k_ref, v_ref, z_ref, t_ref, bq_ref, bk_ref, bv_ref, wq_ref, wk_ref, wv_ref,
              al_ref, dtb_ref, ng_ref, s0_ref, o_ref, s_out_ref, xs_ref, y_ref, s_ref):
    hq = pl.program_id(1)
    ti = pl.program_id(2)
    tt = q_ref.shape[0]
    dk, dv = GDN_DK, GDN_DV
    pad = SUBLANE

    @pl.when(ti == 0)
    def _():
        s_ref[...] = s0_ref[...]
        xs_ref[0:pad, :] = jnp.concatenate([bq_ref[...], bk_ref[...], bv_ref[...]], axis=1)

    x = jnp.concatenate([q_ref[...], k_ref[...], v_ref[...]], axis=1)
    xs_ref[pad:, :] = x
    cw = jnp.concatenate([wq_ref[...], wk_ref[...], wv_ref[...]], axis=1)
    y = xs_ref[pad - 3:pad - 3 + tt, :] * cw[0:1]
    for i in range(1, GDN_CONV - 1):
        y = y + xs_ref[pad - 3 + i:pad - 3 + i + tt, :] * cw[i:i + 1]
    y = y + x * cw[GDN_CONV - 1:GDN_CONV]
    xs_ref[0:pad, :] = x[tt - pad:tt]
    y_ref[...] = y * jax.nn.sigmoid(y)

    strict = _tril_ones(CHUNK, strict=True)
    incl = _tril_ones(CHUNK)
    sel_rows = lax.shift_right_logical(lax.broadcasted_iota(jnp.int32, (GDN_REP * CHUNK, LANE), 0),
                                       int(math.log2(CHUNK)))
    sel_lane = lax.broadcasted_iota(jnp.int32, (GDN_REP * CHUNK, LANE), 1)
    pick = _bf(sel_lane == GDN_V_HEADS + hq * GDN_REP + sel_rows)
    n_chunks = tt // CHUNK

    work = []
    for c in range(n_chunks):
        sl = slice(c * CHUNK, (c + 1) * CHUNK)
        yc = y_ref[sl, :]
        q = _l2n(yc[:, 0:dk]) * dk ** -0.5
        k = _l2n(yc[:, dk:2 * dk])
        qb, kb = _bf(q), _bf(k)
        kk = _dot(kb, kb, NT_DIMS)
        qk = _dot(qb, kb, NT_DIMS)
        tl = t_ref[sl, :]
        beta_all = jax.nn.sigmoid(tl)
        g_all = _chunk_cumsum(-jnp.exp(al_ref[...]) * _softplus(tl + dtb_ref[...]))
        g_rows = sum(_dot(pick, part, NT_DIMS) for part in _split_bf16(g_all, 3))
        for e in range(GDN_REP):
            hv = hq * GDN_REP + e
            v = yc[:, 2 * dk + e * dv:2 * dk + (e + 1) * dv]
            bt = _lane_column(beta_all, hv)
            gc = _lane_column(g_all, GDN_V_HEADS + hv)
            gdiff = gc - g_rows[e * CHUNK:(e + 1) * CHUNK]
            d_strict = jnp.where(strict, jnp.exp(jnp.where(strict, gdiff, 0.0)), 0.0)
            d_incl = jnp.where(incl, jnp.exp(jnp.where(incl, gdiff, 0.0)), 0.0)
            eg = jnp.exp(gc)
            g_last = gc[CHUNK - 1:CHUNK]
            work.append(dict(
                c=c, e=e, sol=jnp.concatenate([bt * v, (bt * eg) * k], axis=1), pw=bt * kk * d_strict,
                aq=_bf(qk * d_incl), q_in=_bf(q * eg), k_out=_bf(k * jnp.exp(g_last - gc)), d_last=jnp.exp(g_last)))

    r_i = lax.broadcasted_iota(jnp.int32, (CHUNK, CHUNK), 0)
    c_i = lax.broadcasted_iota(jnp.int32, (CHUNK, CHUNK), 1)
    same = [lax.shift_right_logical(r_i, sh) == lax.shift_right_logical(c_i, sh) for sh in range(3, 7)]
    eye = (r_i == c_i).astype(jnp.float32)
    for wk in work:
        l8 = jnp.where(same[0], wk["pw"], 0.0)
        l8b = _bf(l8)
        wk["t"] = eye - l8
        wk["p"] = _dot(l8b, l8b)
    for wk in work:
        pb = _bf(wk["p"])
        wk["t"] = wk["t"] + _dot(_bf(wk["t"]), pb)
        wk["p"] = _dot(pb, pb)
    for wk in work:
        wk["t"] = wk["t"] + _dot(_bf(wk["t"]), _bf(wk["p"]))
    for lvl in range(1, len(same)):
        for wk in work:
            tb = _bf(wk["t"])
            off = _bf(jnp.where(same[lvl] & jnp.logical_not(same[lvl - 1]), wk["pw"], 0.0))
            wk["t"] = wk["t"] - _dot(tb, _bf(_dot(off, tb)))
    for wk in work:
        wk["sol"] = _dot(_bf(wk["t"]), _bf(wk["sol"]))

    for wk in work:
        c, e = wk["c"], wk["e"]
        sl = slice(c * CHUNK, (c + 1) * CHUNK)
        u0, w = wk["sol"][:, :dv], wk["sol"][:, dv:]
        s = s_ref[e]
        sb = _bf(s)
        u = u0 - _dot(_bf(w), sb)
        o = _dot(wk["q_in"], sb) + _dot(wk["aq"], _bf(u))
        s_ref[e] = wk["d_last"] * s + _dot(wk["k_out"], _bf(u), TN_DIMS)
        o_ref[sl, e * dv:(e + 1) * dv] = _head_rms_gate(o, ng_ref[...], z_ref[sl, e * dv:(e + 1) * dv])

    @pl.when(ti == pl.num_programs(2) - 1)
    def _():
        s_out_ref[...] = s_ref[...]


def gdn_scan(main, tail, conv_buf, conv_w, a_log, dt_bias, norm_g, s0, batch, seq):
    tt = _row_tile(seq, SCAN_TILE)
    nt = seq // tt
    hq, rep, dk, dv = GDN_QK_HEADS, GDN_REP, GDN_DK, GDN_DV
    vw = rep * dv
    buf = jnp.pad(conv_buf, ((0, 0), (SUBLANE - (GDN_CONV - 1), 0), (0, 0)))
    pad_lanes = jnp.zeros((LANE - 2 * GDN_V_HEADS,), jnp.float32)
    a_row = jnp.concatenate([jnp.zeros((GDN_V_HEADS,), jnp.float32), a_log, pad_lanes]).reshape(1, LANE)
    dt_row = jnp.concatenate([jnp.zeros((GDN_V_HEADS,), jnp.float32), dt_bias, pad_lanes]).reshape(1, LANE)
    k0 = hq
    v0 = 2 * hq * dk // vw
    z0 = GDN_CONV_DIM // vw
    row = lambda b, h, t: b * nt + t
    return pl.pallas_call(
        _gdn_body,
        grid=(batch, hq, nt),
        in_specs=[pl.BlockSpec((tt, dk), lambda b, h, t: (row(b, h, t), h)),
                  pl.BlockSpec((tt, dk), lambda b, h, t: (row(b, h, t), k0 + h)),
                  pl.BlockSpec((tt, vw), lambda b, h, t: (row(b, h, t), v0 + h)),
                  pl.BlockSpec((tt, vw), lambda b, h, t: (row(b, h, t), z0 + h)),
                  pl.BlockSpec((tt, LANE), lambda b, h, t: (row(b, h, t), 0)),
                  pl.BlockSpec((None, SUBLANE, dk), lambda b, h, t: (b, 0, h)),
                  pl.BlockSpec((None, SUBLANE, dk), lambda b, h, t: (b, 0, k0 + h)),
                  pl.BlockSpec((None, SUBLANE, vw), lambda b, h, t: (b, 0, v0 + h)),
                  pl.BlockSpec((GDN_CONV, dk), lambda b, h, t: (0, h)),
                  pl.BlockSpec((GDN_CONV, dk), lambda b, h, t: (0, k0 + h)),
                  pl.BlockSpec((GDN_CONV, vw), lambda b, h, t: (0, v0 + h)),
                  pl.BlockSpec((1, LANE), lambda b, h, t: (0, 0)),
                  pl.BlockSpec((1, LANE), lambda b, h, t: (0, 0)),
                  pl.BlockSpec((1, dv), lambda b, h, t: (0, 0)),
                  pl.BlockSpec((None, rep, dk, dv), lambda b, h, t: (b, h, 0, 0))],
        out_specs=[pl.BlockSpec((tt, vw), lambda b, h, t: (row(b, h, t), h)),
                   pl.BlockSpec((None, rep, dk, dv), lambda b, h, t: (b, h, 0, 0))],
        out_shape=[jax.ShapeDtypeStruct((batch * seq, GDN_V_HEADS * dv), jnp.float32),
                   jax.ShapeDtypeStruct((batch, GDN_V_HEADS, dk, dv), jnp.float32)],
        scratch_shapes=[pltpu.VMEM((tt + SUBLANE, 2 * dk + vw), jnp.float32),
                        pltpu.VMEM((tt, 2 * dk + vw), jnp.float32),
                        pltpu.VMEM((rep, dk, dv), jnp.float32)],
        compiler_params=_cparams("parallel", "parallel", "arbitrary"),
        name="gdn_scan",
    )(main, main, main, main, tail, buf, buf, buf, conv_w, conv_w, conv_w, a_row, dt_row,
      norm_g.reshape(1, dv), s0)


PAGE = 128
KV_SLABS = 2 * KV_HEADS
PAGE_ROWS = PAGE * KV_SLABS
CMP_PAGES = 8
CHUNKS_PER_PAGE = PAGE // CMP_STRIDE
ROWS = GROUP * 4


def _page_slab(pg, slab):
    return pg[pl.ds(slab, PAGE, stride=KV_SLABS), :]


def _compress_part_body(*refs, n_prefetch, paged):
    refs = refs[n_prefetch:]
    pages, w_ref, o_ref, xs_ref = refs[:CMP_PAGES], refs[CMP_PAGES], refs[CMP_PAGES + 1], refs[CMP_PAGES + 2]
    per_g = CMP_PAGES * CHUNKS_PER_PAGE
    for k, pg in enumerate(pages):
        for cg in range(KV_SLABS):
            xs_ref[k, cg] = _page_slab(pg, cg) if paged else pg[:, cg * HEAD_DIM:(cg + 1) * HEAD_DIM]
    for c in range(2):
        acc = None
        for l in range(CMP_STRIDE):
            rows = [xs_ref.at[k, c * KV_HEADS + g][pl.ds(l, CHUNKS_PER_PAGE, stride=CMP_STRIDE), :]
                    for g in range(KV_HEADS) for k in range(CMP_PAGES)]
            d = _dot(_bf(jnp.concatenate(rows, axis=0)), w_ref[c, l])
            acc = d if acc is None else acc + d
        for g in range(KV_HEADS):
            o_ref[c, g] = acc[g * per_g:(g + 1) * per_g]


def _compress_weights(cmp_w1):
    n_part = CMP_BLOCK // CMP_STRIDE
    w = cmp_w1.reshape(2, n_part, CMP_STRIDE, HEAD_DIM, CMP_HIDDEN).transpose(0, 2, 3, 1, 4)
    return w.reshape(2, CMP_STRIDE, HEAD_DIM, n_part * CMP_HIDDEN).astype(jnp.bfloat16)


def compress_part_rows(kv_arr, col_block, cmp_w1, batch, seq):
    w = _compress_weights(cmp_w1)
    pages_per_b = seq // PAGE
    steps = pages_per_b // CMP_PAGES
    per_g = CMP_PAGES * CHUNKS_PER_PAGE

    def page_spec(k):
        return pl.BlockSpec((PAGE, KV_WIDTH), lambda b, s: (b * pages_per_b + s * CMP_PAGES + k, col_block))

    return pl.pallas_call(
        functools.partial(_compress_part_body, n_prefetch=0, paged=False),
        grid=(batch, steps),
        in_specs=[page_spec(k) for k in range(CMP_PAGES)] + [pl.BlockSpec(w.shape, lambda b, s: (0, 0, 0, 0))],
        out_specs=pl.BlockSpec((None, 2, KV_HEADS, per_g, w.shape[-1]), lambda b, s: (b, 0, 0, s, 0)),
        out_shape=jax.ShapeDtypeStruct((batch, 2, KV_HEADS, seq // CMP_STRIDE, w.shape[-1]), jnp.float32),
        scratch_shapes=[pltpu.VMEM((CMP_PAGES, 2 * KV_HEADS, PAGE, HEAD_DIM), jnp.float32)],
        compiler_params=_cparams("parallel", "arbitrary"),
        name="compress_part_rows",
    )(*([kv_arr] * CMP_PAGES), w)


def compress_part_paged(pool, page_table, cmp_w1):
    w = _compress_weights(cmp_w1)
    batch, n_pages = page_table.shape
    steps = n_pages // CMP_PAGES
    per_g = CMP_PAGES * CHUNKS_PER_PAGE

    def page_spec(k):
        return pl.BlockSpec((None, PAGE_ROWS, HEAD_DIM), lambda b, s, pt: (pt[b, s * CMP_PAGES + k], 0, 0))

    return pl.pallas_call(
        functools.partial(_compress_part_body, n_prefetch=1, paged=True),
        grid_spec=pltpu.PrefetchScalarGridSpec(
            num_scalar_prefetch=1, grid=(batch, steps),
            in_specs=[page_spec(k) for k in range(CMP_PAGES)]
            + [pl.BlockSpec(w.shape, lambda b, s, pt: (0, 0, 0, 0))],
            out_specs=pl.BlockSpec((None, 2, KV_HEADS, per_g, w.shape[-1]), lambda b, s, pt: (b, 0, 0, s, 0)),
            scratch_shapes=[pltpu.VMEM((CMP_PAGES, 2 * KV_HEADS, PAGE, HEAD_DIM), jnp.float32)]),
        out_shape=jax.ShapeDtypeStruct((batch, 2, KV_HEADS, n_pages * CHUNKS_PER_PAGE, w.shape[-1]), jnp.float32),
        compiler_params=_cparams("parallel", "arbitrary"),
        name="compress_part_paged",
    )(page_table, *([pool] * CMP_PAGES), w)


def _gelu_tanh(x):
    return x * (0.5 * (1.0 + jnp.tanh(math.sqrt(2.0 / math.pi) * (x + 0.044715 * (x * x * x)))))


def _compress_finish_body(p_ref, peh_ref, w2_ref, o_ref):
    n = p_ref.shape[1]
    for c in range(2):
        p = p_ref[c]
        hid = peh_ref[c:c + 1, :] + p[:, :CMP_HIDDEN]
        hid = hid + pltpu.roll(p[:, CMP_HIDDEN:], n - 1, 0)
        o_ref[c] = _dot(_bf(_gelu_tanh(hid)), _bf(w2_ref[c]))


def compress_finish(part, cmp_w1, cmp_w2, cmp_pe):
    batch, _, _, n, width = part.shape
    pe_hid = jnp.einsum('cld,cldh->ch', cmp_pe, cmp_w1)
    return pl.pallas_call(
        _compress_finish_body,
        grid=(batch, KV_HEADS),
        in_specs=[pl.BlockSpec((None, 2, None, n, width), lambda b, g: (b, 0, g, 0, 0)),
                  pl.BlockSpec((2, CMP_HIDDEN), lambda b, g: (0, 0)),
                  pl.BlockSpec((2, CMP_HIDDEN, HEAD_DIM), lambda b, g: (0, 0, 0))],
        out_specs=pl.BlockSpec((None, None, 2, n, HEAD_DIM), lambda b, g: (b, g, 0, 0, 0)),
        out_shape=jax.ShapeDtypeStruct((batch, KV_HEADS, 2, n, HEAD_DIM), jnp.float32),
        compiler_params=_cparams("parallel", "parallel"),
        name="compress_finish",
    )(part, pe_hid, cmp_w2)


def _rows_to_col(row, n):
    eye = lax.broadcasted_iota(jnp.int32, (n, n), 0) == lax.broadcasted_iota(jnp.int32, (n, n), 1)
    return jnp.sum(jnp.where(eye, jnp.broadcast_to(row, (n, n)), 0.0), axis=1, keepdims=True)


def _sample_bias_tiles(rel_table, past, n_new):
    j = np.arange(PAGE)[:, None]
    t = np.arange(n_new)[None, :]
    far = np.full((PAGE, n_new), REL_MAX_DIST)
    first = WINDOW + t - j
    last = PAGE + t - j
    new = t - j
    tiles = []
    for dist, ok in ((far, far > 0), (first, first < WINDOW), (last, last > 0), (new, (new >= 0) & (j < n_new))):
        b = jnp.where(ok[..., None], _bias_lookup(rel_table, dist), NEG_INF)
        b = b.reshape(PAGE, n_new, KV_HEADS, GROUP).transpose(2, 0, 3, 1).reshape(KV_HEADS, PAGE, GROUP * n_new)
        tiles.append(b)
    return jnp.stack(tiles, axis=1)


def _sample_cmp_body(q_ref, kc_ref, bias_ref, gt_ref, o_ref, pen_ref, *, n_sel_blocks, past):
    n = kc_ref.shape[1]
    nbp = pen_ref.shape[0]
    q = _bf(q_ref[...] * ATTN_SCALE)
    s = _dot(_bf(kc_ref[0]), q, NT_DIMS) + bias_ref[...]
    m = jnp.max(s, axis=0, keepdims=True)
    e = jnp.exp(s - m)
    p = e / jnp.maximum(jnp.sum(e, axis=0, keepdims=True), 1e-30)
    o = _dot(_bf(p), _bf(kc_ref[1]), TN_DIMS)
    o_ref[...] = o * jax.nn.sigmoid(gt_ref[...])

    r_i = lax.broadcasted_iota(jnp.int32, (ROWS, ROWS), 0)
    c_i = lax.broadcasted_iota(jnp.int32, (ROWS, ROWS), 1)
    n_tok = ROWS // GROUP
    same_tok = _bf((r_i & (n_tok - 1)) == (c_i & (n_tok - 1)))
    ratio = SEL_BLOCK // CMP_STRIDE
    j_i = lax.broadcasted_iota(jnp.int32, (nbp, n), 0)
    k_i = lax.broadcasted_iota(jnp.int32, (nbp, n), 1)
    w = _bf((k_i >= ratio * j_i - 1) & (k_i <= ratio * j_i + ratio - 1) & (j_i < n_sel_blocks))
    imp = sum(_dot(part, same_tok) for part in _split_bf16(p, 3))
    score = sum(_dot(w, part) for part in _split_bf16(imp, 3))
    blk = lax.broadcasted_iota(jnp.int32, (nbp, ROWS), 0)
    tok = past + (lax.broadcasted_iota(jnp.int32, (nbp, ROWS), 1) & (n_tok - 1))
    cur = lax.shift_right_logical(tok, int(math.log2(SEL_BLOCK)))
    forced = (blk == 0) | (blk == cur) | (blk == cur - 1)
    causal = blk <= cur
    score = jnp.where(forced, FORCE_SCORE, score)
    score = jnp.where(causal, score, -1.0)
    nbl = -(-nbp // LANE) * LANE
    n_idx = lax.broadcasted_iota(jnp.int32, (nbp, nbl), 0)
    m_idx = lax.broadcasted_iota(jnp.int32, (nbp, nbl), 1)
    lane_tok = lax.broadcasted_iota(jnp.int32, (nbp, ROWS), 1) & (n_tok - 1)
    rank = jnp.zeros((nbp, ROWS), jnp.float32)
    for t in range(n_tok):
        s_col = score[:, t:t + 1]
        s_row = jnp.sum(jnp.where(n_idx == m_idx, s_col, 0.0), axis=0, keepdims=True)
        s_row = jnp.where(m_idx[0:1] < n_sel_blocks, s_row, -2.0)
        ahead = (s_row > s_col) | ((s_row == s_col) & (m_idx < n_idx))
        rank_t = jnp.sum(ahead.astype(jnp.float32), axis=1, keepdims=True)
        rank = jnp.where(lane_tok == t, rank_t, rank)
    chosen = (rank < N_SEL) & causal & (blk < n_sel_blocks)
    pen_ref[...] = jnp.where(chosen, 0.0, NEG_INF)


def sample_cmp_select(qs, kc, bias, gate, past, n_sel_blocks):
    batch = qs.shape[0]
    n = kc.shape[3]
    nbp = -(-n_sel_blocks // SUBLANE) * SUBLANE
    return pl.pallas_call(
        functools.partial(_sample_cmp_body, n_sel_blocks=n_sel_blocks, past=past),
        grid=(batch, KV_HEADS),
        in_specs=[pl.BlockSpec((None, None, ROWS, HEAD_DIM), lambda b, g: (b, g, 0, 0)),
                  pl.BlockSpec((None, None, 2, n, HEAD_DIM), lambda b, g: (b, g, 0, 0, 0)),
                  pl.BlockSpec((None, n, ROWS), lambda b, g: (g, 0, 0)),
                  pl.BlockSpec((None, None, ROWS, HEAD_DIM), lambda b, g: (b, g, 0, 0))],
        out_specs=[pl.BlockSpec((None, None, ROWS, HEAD_DIM), lambda b, g: (b, g, 0, 0)),
                   pl.BlockSpec((None, None, nbp, ROWS), lambda b, g: (b, g, 0, 0))],
        out_shape=[jax.ShapeDtypeStruct((batch, KV_HEADS, ROWS, HEAD_DIM), jnp.float32),
                   jax.ShapeDtypeStruct((batch, KV_HEADS, nbp, ROWS), jnp.float32)],
        compiler_params=_cparams("parallel", "parallel"),
        name="sample_cmp_select",
    )(qs, kc, bias, gate)


ATTN_PAGES = 4


def _paged_attn_body(*refs, pen_block, gated, n_pages):
    it = iter(refs)
    pt_ref, tid_ref, q_ref = next(it), next(it), next(it)
    pages = [next(it) for _ in range(ATTN_PAGES)]
    new_ref, bt_ref = next(it), next(it)
    pen_ref = next(it) if pen_block else None
    gt_ref = next(it) if gated else None
    o_ref, m_ref, l_ref, acc_ref = next(it), next(it), next(it), next(it)
    step = pl.program_id(1)

    @pl.when(step == 0)
    def _():
        m_ref[...] = jnp.full(m_ref.shape, NEG_INF, jnp.float32)
        l_ref[...] = jnp.zeros(l_ref.shape, jnp.float32)
        acc_ref[...] = jnp.zeros(acc_ref.shape, jnp.float32)

    def attend(g, k, v, bias, page, n_keys):
        s = _dot(_bf(k), _bf(q_ref[g] * ATTN_SCALE), NT_DIMS) + bias
        if pen_block:
            if pen_block >= PAGE:
                s = s + pen_ref[g, pl.ds(page // (pen_block // PAGE), 1), :]
            else:
                per_page = PAGE // pen_block
                pieces = [s[a * pen_block:min((a + 1) * pen_block, n_keys)]
                          + pen_ref[g, pl.ds(page * per_page + a, 1), :]
                          for a in range(-(-n_keys // pen_block))]
                s = pieces[0] if len(pieces) == 1 else jnp.concatenate(pieces, axis=0)
        m_prev = m_ref[g]
        m_new = jnp.maximum(m_prev, jnp.max(s, axis=0, keepdims=True))
        alpha = jnp.exp(m_prev - m_new)
        p = jnp.exp(s - m_new)
        l_ref[g] = alpha * l_ref[g] + jnp.sum(p, axis=0, keepdims=True)
        acc_ref[g] = _rows_to_col(alpha, ROWS) * acc_ref[g] + _dot(_bf(p), _bf(v), TN_DIMS)
        m_ref[g] = m_new

    half = KV_HEADS * HEAD_DIM
    for kk in range(ATTN_PAGES):
        page = step * ATTN_PAGES + kk
        tile = tid_ref[page]
        for g in range(KV_HEADS):
            attend(g, _page_slab(pages[kk], g), _page_slab(pages[kk], KV_HEADS + g), bt_ref[g, tile], page, PAGE)

    @pl.when(step == pl.num_programs(1) - 1)
    def _():
        n_new = new_ref.shape[0]
        tile = tid_ref[n_pages]
        for g in range(KV_HEADS):
            attend(g, new_ref[:, g * HEAD_DIM:(g + 1) * HEAD_DIM],
                   new_ref[:, half + g * HEAD_DIM:half + (g + 1) * HEAD_DIM], bt_ref[g, tile, 0:n_new, :],
                   n_pages, n_new)
            o = acc_ref[g] / _rows_to_col(jnp.maximum(l_ref[g], 1e-30), ROWS)
            if gated:
                o = o * jax.nn.sigmoid(gt_ref[g])
            o_ref[g] = o


def paged_attention(qs, pool, page_table, tile_ids, new_kv, bias_tiles, *, pen=None, pen_block=0, gate=None):
    batch, n_pages = page_table.shape
    steps = n_pages // ATTN_PAGES
    n_new = new_kv.shape[1]

    def page_spec(k):
        return pl.BlockSpec((None, PAGE_ROWS, HEAD_DIM), lambda b, s, pt, tid: (pt[b, s * ATTN_PAGES + k], 0, 0))

    grp = lambda b, s, pt, tid: (b, 0, 0, 0)
    in_specs = ([pl.BlockSpec((None, KV_HEADS, ROWS, HEAD_DIM), grp)] + [page_spec(k) for k in range(ATTN_PAGES)]
                + [pl.BlockSpec((None, n_new, KV_WIDTH), lambda b, s, pt, tid: (b, 0, 0)),
                   pl.BlockSpec(bias_tiles.shape, lambda b, s, pt, tid: (0, 0, 0, 0))])
    args = [qs] + [pool] * ATTN_PAGES + [new_kv, bias_tiles]
    if pen is not None:
        in_specs.append(pl.BlockSpec((None,) + pen.shape[1:], grp))
        args.append(pen)
    if gate is not None:
        in_specs.append(pl.BlockSpec((None, KV_HEADS, ROWS, HEAD_DIM), grp))
        args.append(gate)
    return pl.pallas_call(
        functools.partial(_paged_attn_body, pen_block=pen_block if pen is not None else 0, gated=gate is not None,
                          n_pages=n_pages),
        grid_spec=pltpu.PrefetchScalarGridSpec(
            num_scalar_prefetch=2, grid=(batch, steps), in_specs=in_specs,
            out_specs=pl.BlockSpec((None, KV_HEADS, ROWS, HEAD_DIM), grp),
            scratch_shapes=[pltpu.VMEM((KV_HEADS, 1, ROWS), jnp.float32), pltpu.VMEM((KV_HEADS, 1, ROWS), jnp.float32),
                            pltpu.VMEM((KV_HEADS, ROWS, HEAD_DIM), jnp.float32)]),
        out_shape=jax.ShapeDtypeStruct((batch, KV_HEADS, ROWS, HEAD_DIM), jnp.float32),
        compiler_params=_cparams("parallel", "arbitrary"),
        name="paged_attention",
    )(page_table, tile_ids, *args)


def _moba_sample_gate_body(pt_ref, q_ref, *refs, n_blocks, past):
    pages, (pen_ref, km_ref) = refs[:ATTN_PAGES], refs[ATTN_PAGES:]
    step = pl.program_id(1)
    half = KV_HEADS * HEAD_DIM
    per_block = MOBA_BLOCK // PAGE

    @pl.when(step == 0)
    def _():
        km_ref[...] = jnp.zeros(km_ref.shape, jnp.float32)

    for kk in range(ATTN_PAGES):
        blk = (step * ATTN_PAGES + kk) // per_block
        slab_sums = jnp.sum(pages[kk][...].reshape(PAGE, KV_SLABS, HEAD_DIM), axis=0)
        for g in range(KV_HEADS):
            km_ref[g, pl.ds(blk, 1), :] += slab_sums[g:g + 1]

    @pl.when(step == pl.num_programs(1) - 1)
    def _():
        nbp = pen_ref.shape[1]
        blk = lax.broadcasted_iota(jnp.int32, (nbp, ROWS), 0)
        n_tok = ROWS // GROUP
        tok = past + (lax.broadcasted_iota(jnp.int32, (nbp, ROWS), 1) & (n_tok - 1))
        own = lax.shift_right_logical(tok, int(math.log2(MOBA_BLOCK)))
        for g in range(KV_HEADS):
            kh, kl = _split_bf16(km_ref[g] / MOBA_BLOCK, 2)
            qh, ql = _split_bf16(q_ref[g], 2)
            gate = _dot(kh, qh, NT_DIMS) + _dot(kh, ql, NT_DIMS) + _dot(kl, qh, NT_DIMS)
            gate = jnp.where(blk < own, gate, NEG_INF)
            chosen = ((_rank_rows(gate, n_blocks) < MOBA_TOPK) & (blk < own)) | (blk == own)
            pen_ref[g] = jnp.where(chosen, 0.0, NEG_INF)


def moba_sample_gate(qs, pool, page_table, past, n_new):
    batch, n_pages = page_table.shape
    steps = n_pages // ATTN_PAGES
    n_blocks = -(-(past + n_new) // MOBA_BLOCK)
    nbp = -(-n_blocks // SUBLANE) * SUBLANE

    def page_spec(k):
        return pl.BlockSpec((None, PAGE_ROWS, HEAD_DIM), lambda b, s, pt: (pt[b, s * ATTN_PAGES + k], 0, 0))

    return pl.pallas_call(
        functools.partial(_moba_sample_gate_body, n_blocks=n_blocks, past=past),
        grid_spec=pltpu.PrefetchScalarGridSpec(
            num_scalar_prefetch=1, grid=(batch, steps),
            in_specs=[pl.BlockSpec((None, KV_HEADS, ROWS, HEAD_DIM), lambda b, s, pt: (b, 0, 0, 0))]
            + [page_spec(k) for k in range(ATTN_PAGES)],
            out_specs=pl.BlockSpec((None, KV_HEADS, nbp, ROWS), lambda b, s, pt: (b, 0, 0, 0)),
            scratch_shapes=[pltpu.VMEM((KV_HEADS, nbp, HEAD_DIM), jnp.float32)]),
        out_shape=jax.ShapeDtypeStruct((batch, KV_HEADS, nbp, ROWS), jnp.float32),
        compiler_params=_cparams("parallel", "arbitrary"),
        name="moba_sample_gate",
    )(page_table, qs, *([pool] * ATTN_PAGES))


def _sample_rows(x, batch, n_tok):
    return x.reshape(batch, n_tok, KV_HEADS, GROUP, HEAD_DIM).transpose(0, 2, 3, 1, 4).reshape(
        batch, KV_HEADS, GROUP * n_tok, HEAD_DIM)


def _sample_unrows(o, batch, n_tok):
    return o.reshape(batch, KV_HEADS, GROUP, n_tok, HEAD_DIM).transpose(0, 3, 1, 2, 4).reshape(batch * n_tok, Q_WIDTH)


def _sample_gate_rows(tail, branch, batch, n_tok):
    gt = tail[:, branch * N_HEADS:(branch + 1) * N_HEADS].reshape(batch, n_tok, KV_HEADS, GROUP)
    gt = gt.transpose(0, 2, 3, 1).reshape(batch, KV_HEADS, GROUP * n_tok, 1)
    return jnp.broadcast_to(gt, (batch, KV_HEADS, GROUP * n_tok, HEAD_DIM))


def _pad_new(kv_new, batch, n_tok):
    return jnp.pad(kv_new.reshape(batch, n_tok, KV_WIDTH), ((0, 0), (0, SUBLANE - n_tok), (0, 0)))


def _nsa_sample_pallas(main, tail, cache_c, cache_s, cache_w, page_table, cmp_w1, cmp_w2, cmp_pe, rel_table):
    batch, n_pages = page_table.shape
    n_tok = main.shape[0] // batch
    past = n_pages * PAGE
    wbuf = cache_w.shape[1]
    assert n_tok == ROWS // GROUP and cache_c.shape[1] == PAGE and wbuf == WINDOW and WINDOW % PAGE == 0
    n_cmp = (past + n_tok - CMP_BLOCK) // CMP_STRIDE + 1
    assert n_cmp + CMP_BLOCK // CMP_STRIDE - 1 == past // CMP_STRIDE
    qs = _sample_rows(main[:, :Q_WIDTH], batch, n_tok)
    kv_new = [main[:, Q_WIDTH + c * KV_WIDTH:Q_WIDTH + (c + 1) * KV_WIDTH] for c in range(3)]
    tiles = _sample_bias_tiles(rel_table, past, n_tok)
    flat = lambda pool: pool.reshape(pool.shape[0], PAGE_ROWS, HEAD_DIM)

    kc = compress_finish(compress_part_paged(flat(cache_c), page_table, cmp_w1), cmp_w1, cmp_w2, cmp_pe)
    n = kc.shape[3]
    dist = past + np.arange(n_tok)[None, :] - (np.arange(n)[:, None] * CMP_STRIDE + CMP_BLOCK - 1)
    ok = (dist >= 0) & (np.arange(n)[:, None] < n_cmp)
    cb = jnp.where(ok[..., None], _bias_lookup(rel_table, dist), NEG_INF)
    cb = cb.reshape(n, n_tok, KV_HEADS, GROUP).transpose(2, 0, 3, 1).reshape(KV_HEADS, n, GROUP * n_tok)
    n_sel_blocks = -(-(past + n_tok) // SEL_BLOCK)
    o_cmp, pen = sample_cmp_select(qs, kc, cb, _sample_gate_rows(tail, 0, batch, n_tok), past, n_sel_blocks)

    far_then_last = jnp.asarray([0] * (n_pages - 1) + [2, 3], jnp.int32)
    o_sel = paged_attention(qs, flat(cache_s), page_table, far_then_last, _pad_new(kv_new[1], batch, n_tok), tiles,
                            pen=pen, pen_block=SEL_BLOCK, gate=_sample_gate_rows(tail, 1, batch, n_tok))
    w_pages = wbuf // PAGE
    win_table = jnp.arange(batch * w_pages, dtype=jnp.int32).reshape(batch, w_pages)
    win_tiles = jnp.asarray([1] + [0] * (w_pages - 2) + [2, 3], jnp.int32)
    o_win = paged_attention(qs, cache_w.reshape(batch * w_pages, PAGE_ROWS, HEAD_DIM), win_table, win_tiles,
                            _pad_new(kv_new[2], batch, n_tok), tiles, gate=_sample_gate_rows(tail, 2, batch, n_tok))
    outs = [_sample_unrows(o, batch, n_tok) for o in (o_cmp, o_sel, o_win)]
    shape = (batch, n_tok, 2, KV_HEADS, HEAD_DIM)
    new_win = jnp.concatenate([cache_w[:, n_tok:], kv_new[2].reshape(shape)], axis=1)
    return outs, kv_new[0].reshape(shape), kv_new[1].reshape(shape), new_win


def _moba_sample_pallas(proj, cache_kv, page_table, rel_table):
    batch, n_pages = page_table.shape
    n_tok = proj.shape[0] // batch
    past = n_pages * PAGE
    assert n_tok == ROWS // GROUP and (past // MOBA_BLOCK) * MOBA_BLOCK == past
    qs = _sample_rows(proj[:, :Q_WIDTH], batch, n_tok)
    kv_new = proj[:, Q_WIDTH:]
    pool = cache_kv.reshape(cache_kv.shape[0], PAGE_ROWS, HEAD_DIM)
    pen = moba_sample_gate(qs, pool, page_table, past, n_tok)
    tile_ids = jnp.asarray([0] * (n_pages - 1) + [2, 3], jnp.int32)
    o = paged_attention(qs, pool, page_table, tile_ids, _pad_new(kv_new, batch, n_tok),
                        _sample_bias_tiles(rel_table, past, n_tok), pen=pen, pen_block=MOBA_BLOCK)
    return _sample_unrows(o, batch, n_tok), kv_new.reshape(batch, n_tok, 2, KV_HEADS, HEAD_DIM)


def _rms_norm(x, g):
    xf = x.astype(jnp.float32)
    y = xf * lax.rsqrt(jnp.mean(xf * xf, axis=-1, keepdims=True) + NORM_EPS)
    return (y * g.astype(jnp.float32)).astype(x.dtype)


def _l2norm(x):
    return x * lax.rsqrt(jnp.sum(x * x, axis=-1, keepdims=True) + NORM_EPS)


def _masked_softmax(logits, mask):
    logits = jnp.where(mask, logits, NEG_INF)
    m = jnp.max(logits, axis=-1, keepdims=True)
    e = jnp.where(mask, jnp.exp(logits - m), 0.0)
    return e / jnp.maximum(jnp.sum(e, axis=-1, keepdims=True), 1e-30)


def _t5_bucket(dist):
    exact = REL_BUCKETS // 2
    d = jnp.maximum(dist, 0)
    ratio = jnp.log(jnp.maximum(d, 1).astype(jnp.float32) / exact) / math.log(REL_MAX_DIST / exact)
    large = jnp.minimum(exact + (ratio * (REL_BUCKETS - exact)).astype(jnp.int32), REL_BUCKETS - 1)
    return jnp.where(d < exact, d, large)


def _rel_bias(rel_table, dist):
    return rel_table.astype(jnp.float32)[_t5_bucket(dist)]


def _gather_pages(pool, page_table):
    rows = pool[page_table]
    return rows.reshape(page_table.shape[0], page_table.shape[1] * pool.shape[1], *pool.shape[2:])


def _causal_conv(x, buf, w):
    T = x.shape[1]
    xp = jnp.concatenate([buf.astype(x.dtype), x], axis=1)
    y = xp[:, 0:T] * w[0]
    for i in range(1, GDN_CONV):
        y = y + xp[:, i:i + T] * w[i]
    return jax.nn.silu(y), xp[:, T:]


def _nsa_split(proj):
    B, T, _ = proj.shape
    q = proj[..., :Q_WIDTH].reshape(B, T, KV_HEADS, GROUP, HEAD_DIM)
    kv_c, kv_s, kv_w = (proj[..., Q_WIDTH + c * KV_WIDTH:Q_WIDTH + (c + 1) * KV_WIDTH]
                        .reshape(B, T, 2, KV_HEADS, HEAD_DIM) for c in range(3))
    gates = jax.nn.sigmoid(proj[..., NSA_MAIN:NSA_MAIN + 3 * N_HEADS]).reshape(B, T, 3, KV_HEADS, GROUP)
    return q, kv_c, kv_s, kv_w, gates


def _nsa_compress(kv, cmp_w1, cmp_w2, cmp_pe):
    B, Tk = kv.shape[:2]
    nc = (Tk - CMP_BLOCK) // CMP_STRIDE + 1
    n_part = CMP_BLOCK // CMP_STRIDE
    n_chunk = nc + n_part - 1
    chunks = kv[:, :n_chunk * CMP_STRIDE].astype(jnp.float32).reshape(B, n_chunk, CMP_STRIDE, 2, KV_HEADS, HEAD_DIM)
    w1 = cmp_w1.astype(jnp.float32)
    part = jnp.einsum('bnlcgd,crldh->bncrgh', chunks, w1.reshape(2, n_part, CMP_STRIDE, HEAD_DIM, CMP_HIDDEN))
    hid = jnp.einsum('cld,cldh->ch', cmp_pe.astype(jnp.float32), w1)[None, None, :, None, :]
    for r in range(n_part):
        hid = hid + part[:, r:r + nc, :, r]
    return jnp.einsum('bncgh,chd->bncgd', jax.nn.gelu(hid), cmp_w2.astype(jnp.float32))


def _nsa_cmp_attn(q, q_pos, kc, rel_table):
    nc = kc.shape[1]
    end_pos = jnp.arange(nc) * CMP_STRIDE + (CMP_BLOCK - 1)
    dist = q_pos[:, None] - end_pos[None, :]
    bias = _rel_bias(rel_table, dist).reshape(q_pos.shape[0], nc, KV_HEADS, GROUP).transpose(0, 2, 3, 1)
    logits = jnp.einsum('bqgrd,bcgd->bqgrc', q, kc[:, :, 0]) * ATTN_SCALE + bias
    p = _masked_softmax(logits, (dist >= 0)[:, None, None, :])
    return jnp.einsum('bqgrc,bcgd->bqgrd', p, kc[:, :, 1]), p


def _nsa_select(p_cmp, q_pos, tk):
    ns = -(-tk // SEL_BLOCK)
    ratio = SEL_BLOCK // CMP_STRIDE
    imp = p_cmp.sum(axis=3)
    nc = imp.shape[-1]
    imp = jnp.pad(imp, ((0, 0), (0, 0), (0, 0), (1, ratio * ns + ratio - 1 - nc)))
    score = imp[..., :ratio * ns].reshape(*imp.shape[:3], ns, ratio).sum(-1) + imp[..., ratio::ratio]
    cur = q_pos // SEL_BLOCK
    blk = jnp.arange(ns)
    forced = (blk[None, :] == 0) | (blk[None, :] == cur[:, None]) | (blk[None, :] == cur[:, None] - 1)
    causal = blk[None, :] <= cur[:, None]
    score = jnp.where(forced[None, :, None, :], FORCE_SCORE, score)
    score = jnp.where(causal[None, :, None, :], score, -1.0)
    _, idx = lax.top_k(score, min(N_SEL, ns))
    valid = idx <= cur[None, :, None, None]
    return idx, valid


def _block_mask(idx, valid, n_blocks, block, tk):
    hit = (idx[..., None] == jnp.arange(n_blocks)) & valid[..., None]
    return jnp.repeat(jnp.any(hit, axis=-2), block, axis=-1)[..., :tk]


def _dense_attn(q, q_pos, kv, key_ok, rel_table):
    tk = kv.shape[1]
    dist = q_pos[:, None] - jnp.arange(tk)[None, :]
    bias = _rel_bias(rel_table, dist).reshape(q_pos.shape[0], tk, KV_HEADS, GROUP).transpose(0, 2, 3, 1)
    logits = jnp.einsum('bqgrd,bkgd->bqgrk', q, kv[:, :, 0]) * ATTN_SCALE + bias
    p = _masked_softmax(logits, key_ok & (dist >= 0)[None, :, None, None, :])
    return jnp.einsum('bqgrk,bkgd->bqgrd', p, kv[:, :, 1])


def _band_attn(q, q_pos, kv_band, k_pos, rel_table):
    dist = q_pos[:, :, None] - k_pos[:, None, :]
    mask = (dist >= 0) & (dist < WINDOW) & (k_pos[:, None, :] >= 0)
    n, qb, kb = dist.shape
    bias = _rel_bias(rel_table, dist).reshape(n, qb, kb, KV_HEADS, GROUP).transpose(0, 1, 3, 4, 2)
    kvf = kv_band.astype(jnp.float32)
    logits = jnp.einsum('bnqgrd,bnkgd->bnqgrk', q, kvf[:, :, :, 0]) * ATTN_SCALE + bias
    p = _masked_softmax(logits, mask[:, :, None, None, :])
    return jnp.einsum('bnqgrk,bnkgd->bnqgrd', p, kvf[:, :, :, 1])


def _nsa_sample(proj, cache_c, cache_s, cache_w, page_table, cmp_w1, cmp_w2, cmp_pe, rel_table):
    B, T, _ = proj.shape
    past = page_table.shape[1] * cache_c.shape[1]
    q, kv_c, kv_s, kv_w, gates = _nsa_split(proj)
    pos = past + jnp.arange(T)
    kv_c_full = jnp.concatenate([_gather_pages(cache_c, page_table), kv_c], axis=1)
    kv_s_full = jnp.concatenate([_gather_pages(cache_s, page_table), kv_s], axis=1)
    wbuf = cache_w.shape[1]
    band = jnp.concatenate([cache_w, kv_w], axis=1)
    band_k_pos = past - wbuf + jnp.arange(wbuf + T)
    kc = _nsa_compress(kv_c_full, cmp_w1, cmp_w2, cmp_pe)
    o_cmp, p_cmp = _nsa_cmp_attn(q, pos, kc, rel_table)
    tk = kv_s_full.shape[1]
    idx, valid = _nsa_select(p_cmp, pos, tk)
    key_ok = _block_mask(idx, valid, -(-tk // SEL_BLOCK), SEL_BLOCK, tk)[:, :, :, None, :]
    o_sel = _dense_attn(q, pos, kv_s_full, key_ok, rel_table)
    o_win = _band_attn(q[:, None], pos[None, :], band[:, None], band_k_pos[None, :], rel_table)[:, 0]
    o = gates[:, :, 0, ..., None] * o_cmp + gates[:, :, 1, ..., None] * o_sel + gates[:, :, 2, ..., None] * o_win
    return o.reshape(B, T, Q_WIDTH), kv_c, kv_s, band[:, -wbuf:]


def _moba_sample(proj, cache_kv, page_table, rel_table):
    B, T, _ = proj.shape
    past = page_table.shape[1] * cache_kv.shape[1]
    q = proj[..., :Q_WIDTH].reshape(B, T, KV_HEADS, GROUP, HEAD_DIM)
    kv_new = proj[..., Q_WIDTH:].reshape(B, T, 2, KV_HEADS, HEAD_DIM)
    kv = jnp.concatenate([_gather_pages(cache_kv, page_table), kv_new], axis=1)
    q_pos = past + jnp.arange(T)
    tk = kv.shape[1]
    nb = -(-tk // MOBA_BLOCK)
    kpad = jnp.pad(kv[:, :, 0], ((0, 0), (0, nb * MOBA_BLOCK - tk), (0, 0), (0, 0)))
    kmean = jnp.mean(kpad.reshape(B, nb, MOBA_BLOCK, KV_HEADS, HEAD_DIM), axis=2)
    own = q_pos // MOBA_BLOCK
    gate = jnp.einsum('bqgrd,bngd->bqgrn', q, kmean)
    is_past = jnp.arange(nb)[None, :] < own[:, None]
    gate = jnp.where(is_past[:, None, None, :], gate, NEG_INF)
    _, idx = lax.top_k(gate, min(MOBA_TOPK, nb))
    valid = idx < own[:, None, None, None]
    own_ok = (jnp.arange(tk)[None, :] // MOBA_BLOCK) == own[:, None]
    key_ok = _block_mask(idx, valid, nb, MOBA_BLOCK, tk) | own_ok[None, :, None, None, :]
    o = _dense_attn(q, q_pos, kv, key_ok, rel_table)
    return o.reshape(B, T, Q_WIDTH), kv_new


def _gla_chunked(q, k, v, logf, s0):
    B, T, H, dk = q.shape
    dv = v.shape[-1]
    C = math.gcd(T, CHUNK)
    N = T // C
    q, k, v, logf = (a.reshape(B, N, C, *a.shape[2:]) for a in (q, k, v, logf))
    b = jnp.cumsum(logf, axis=2)
    b_ref = b[:, :, C // 2:C // 2 + 1]
    a = jnp.einsum('bnihd,bnjhd->bnhij', q * jnp.exp(b - b_ref), k * jnp.exp(b_ref - b))
    causal = jnp.arange(C)[:, None] >= jnp.arange(C)[None, :]
    a = jnp.where(causal, a, 0.0)
    o_intra = jnp.einsum('bnhij,bnjhv->bnihv', a, v)
    q_in = q * jnp.exp(b)
    k_out = k * jnp.exp(b[:, :, -1:] - b)
    d_last = jnp.exp(b[:, :, -1])

    def step(s, xs):
        q_c, k_c, v_c, d_c = xs
        o = jnp.einsum('bihd,bhdv->bihv', q_c, s)
        s = d_c[..., None] * s + jnp.einsum('bjhd,bjhv->bhdv', k_c, v_c)
        return s, o

    s, o_inter = lax.scan(step, s0, tuple(jnp.moveaxis(t, 1, 0) for t in (q_in, k_out, v, d_last)))
    o = o_intra + jnp.moveaxis(o_inter, 0, 1)
    return o.reshape(B, T, H, dv), s


def _hgrn2_core(proj, s0, lb_logits, layer, norm_g):
    B, T, _ = proj.shape
    dk = HG_HEADS * HG_DK
    dv = HG_HEADS * HG_DV
    q, f, i, g = jnp.split(proj, [dk, 2 * dk, 2 * dk + dv], axis=-1)
    p = jax.nn.softmax(lb_logits.astype(jnp.float32), axis=0)
    lb = (jnp.cumsum(p, axis=0) - p[0])[layer]
    fg = lb + (1.0 - lb) * jax.nn.sigmoid(f)
    shp = (B, T, HG_HEADS, HG_DK)
    o, s = _gla_chunked((jax.nn.silu(q) * HG_DK ** -0.5).reshape(shp), (1.0 - fg).reshape(shp),
                        i.reshape(B, T, HG_HEADS, HG_DV), jnp.log(fg).reshape(shp), s0.astype(jnp.float32))
    o = _rms_norm(o, norm_g) * jax.nn.silu(g.reshape(B, T, HG_HEADS, HG_DV))
    return o.reshape(B, T, dv), s


def _gdn_chunked(q, k, v, log_a, beta, s0):
    B, T, H, dk = q.shape
    dv = v.shape[-1]
    C = math.gcd(T, CHUNK)
    N = T // C

    def heads_first(a):
        return jnp.moveaxis(a.reshape(B, N, C, *a.shape[2:]), 3, 2)

    qh, kh, vh, bt = heads_first(q), heads_first(k), heads_first(v), heads_first(beta)
    g = jnp.cumsum(heads_first(log_a), axis=-1)
    ar = jnp.arange(C)
    strict = ar[:, None] > ar[None, :]
    incl = ar[:, None] >= ar[None, :]
    gdiff = g[..., :, None] - g[..., None, :]
    d_strict = jnp.where(strict, jnp.exp(jnp.where(strict, gdiff, 0.0)), 0.0)
    d_incl = jnp.where(incl, jnp.exp(jnp.where(incl, gdiff, 0.0)), 0.0)
    kk = jnp.einsum('bnhid,bnhjd->bnhij', kh, kh)
    m = jnp.eye(C, dtype=jnp.float32) + bt[..., :, None] * kk * d_strict
    rhs = jnp.concatenate([bt[..., None] * vh, (bt * jnp.exp(g))[..., None] * kh], axis=-1)
    sol = lax.linalg.triangular_solve(m, rhs, left_side=True, lower=True, unit_diagonal=True)
    u0, w = sol[..., :dv], sol[..., dv:]
    aq = jnp.einsum('bnhid,bnhjd->bnhij', qh, kh) * d_incl
    q_in = qh * jnp.exp(g)[..., None]
    k_out = kh * jnp.exp(g[..., -1:] - g)[..., None]
    d_last = jnp.exp(g[..., -1])

    def step(s, xs):
        u0_c, w_c, aq_c, q_c, k_c, d_c = xs
        u = u0_c - jnp.einsum('bhcd,bhdv->bhcv', w_c, s)
        o = jnp.einsum('bhcd,bhdv->bhcv', q_c, s) + jnp.einsum('bhij,bhjv->bhiv', aq_c, u)
        s = d_c[..., None, None] * s + jnp.einsum('bhcd,bhcv->bhdv', k_c, u)
        return s, o

    xs = tuple(jnp.moveaxis(a, 1, 0) for a in (u0, w, aq, q_in, k_out, d_last))
    s, o = lax.scan(step, s0, xs)
    return jnp.transpose(o, (1, 0, 3, 2, 4)).reshape(B, T, H, dv), s


def _gdn_core(proj, conv_buf, s0, conv_w, a_log, dt_bias, norm_g):
    B, T, _ = proj.shape
    vw = GDN_V_HEADS * GDN_DV
    qkv = proj[..., :GDN_CONV_DIM]
    z = proj[..., GDN_CONV_DIM:GDN_MAIN]
    b_logit = proj[..., GDN_MAIN:GDN_MAIN + GDN_V_HEADS]
    a_in = proj[..., GDN_MAIN + GDN_V_HEADS:GDN_MAIN + 2 * GDN_V_HEADS]
    conv_out, new_buf = _causal_conv(qkv, conv_buf, conv_w)
    qkw = GDN_QK_HEADS * GDN_DK
    q = jnp.repeat(_l2norm(conv_out[..., :qkw].reshape(B, T, GDN_QK_HEADS, GDN_DK)) * GDN_DK ** -0.5, GDN_REP, axis=2)
    k = jnp.repeat(_l2norm(conv_out[..., qkw:2 * qkw].reshape(B, T, GDN_QK_HEADS, GDN_DK)), GDN_REP, axis=2)
    v = conv_out[..., 2 * qkw:].reshape(B, T, GDN_V_HEADS, GDN_DV)
    beta = jax.nn.sigmoid(b_logit)
    log_a = -jnp.exp(a_log.astype(jnp.float32)) * jax.nn.softplus(a_in + dt_bias.astype(jnp.float32))
    o, s = _gdn_chunked(q, k, v, log_a, beta, s0.astype(jnp.float32))
    o = _rms_norm(o, norm_g) * jax.nn.silu(z.reshape(B, T, GDN_V_HEADS, GDN_DV))
    return o.reshape(B, T, vw), new_buf, s


def _pad_cols(w, mult=LANE):
    n = w.shape[1]
    return jnp.pad(w, ((0, 0), (0, (-n) % mult)))


def _nsa_prompt(main, tail, batch, seq, cmp_w1, cmp_w2, cmp_pe, rel_table):
    kv = main[:, Q_WIDTH:].reshape(batch, seq, 3, 2, KV_HEADS, HEAD_DIM)
    kv_c, kv_s, kv_w = kv[:, :, 0], kv[:, :, 1], kv[:, :, 2]
    kc = compress_finish(compress_part_rows(main, Q_WIDTH // KV_WIDTH, cmp_w1, batch, seq), cmp_w1, cmp_w2, cmp_pe)
    o_cmp, pen = cmp_select(main, tail, kc, _cmp_bias_table(rel_table, kc.shape[3]), batch, seq)
    col = Q_WIDTH // HEAD_DIM
    o_sel = flash_attention(main, main, col + 2 * KV_HEADS, col + 3 * KV_HEADS, _flash_bias_tiles(rel_table, 0),
                            batch, seq, pen=pen, pen_block=SEL_BLOCK, gate_arr=tail, gate_col0=N_HEADS)
    o_win = flash_attention(main, main, col + 4 * KV_HEADS, col + 5 * KV_HEADS,
                            _flash_bias_tiles(rel_table, WINDOW), batch, seq, k_back=WINDOW // ATTN_TILE,
                            gate_arr=tail, gate_col0=2 * N_HEADS)
    return [o_cmp, o_sel, o_win], kv_c, kv_s, kv_w[:, -min(WINDOW, seq):]


def _moba_prompt(proj, batch, seq, rel_table):
    col = Q_WIDTH // HEAD_DIM
    pen = moba_gate(proj, col, batch, seq)
    o = flash_attention(proj, proj, col, col + KV_HEADS, _flash_bias_tiles(rel_table, 0), batch, seq,
                        pen=pen, pen_block=MOBA_BLOCK)
    return o, proj[:, Q_WIDTH:].reshape(batch, seq, 2, KV_HEADS, HEAD_DIM)


def kernel(x_prompt, x_sample, cache_nsa_cmp_kv, cache_nsa_sel_kv, cache_nsa_win_kv, cache_moba_kv,
           state_hgrn2, state_gdn_conv, state_gdn_ssm, page_table, rel_table, ln_mix, ln_ffn, ln_final,
           ffn_w_up, ffn_w_down, nsa_w_in, nsa_cmp_w1, nsa_cmp_w2, nsa_cmp_pe, nsa_w_out, moba_w_in, moba_w_out,
           hg_w_in, hg_lb_logits, hg_norm, hg_w_out, gdn_w_in, gdn_conv_w, gdn_a_log, gdn_dt_bias, gdn_norm,
           gdn_w_out):
    bf = jnp.bfloat16
    bp, tp = x_prompt.shape[:2]
    bs, ts = x_sample.shape[:2]
    assert tp % ATTN_TILE == 0 and WINDOW % ATTN_TILE == 0 and ATTN_TILE == MOBA_BLOCK and tp % SCAN_TILE == 0
    xp = x_prompt.reshape(bp * tp, D_MODEL)
    xs = x_sample.reshape(bs * ts, D_MODEL)

    for layer in range(DEPTH):
        kind = layer % N_MIXERS
        g_mix = ln_mix[layer]
        if kind == 0:
            w_main, w_tail = nsa_w_in[:, :NSA_MAIN].astype(bf), _pad_cols(nsa_w_in[:, NSA_MAIN:]).astype(bf)
            main_p, tail_p = norm_matmul(xp, g_mix, w_main), norm_matmul(xp, g_mix, w_tail)
            main_s, tail_s = norm_matmul(xs, g_mix, w_main), norm_matmul(xs, g_mix, w_tail)
            op, nsa_cmp_p, nsa_sel_p, nsa_win_p = _nsa_prompt(main_p, tail_p, bp, tp, nsa_cmp_w1, nsa_cmp_w2,
                                                              nsa_cmp_pe, rel_table)
            os_, nsa_cmp_s, nsa_sel_s, nsa_win_s = _nsa_sample_pallas(main_s, tail_s, cache_nsa_cmp_kv,
                                                                     cache_nsa_sel_kv, cache_nsa_win_kv, page_table,
                                                                     nsa_cmp_w1, nsa_cmp_w2, nsa_cmp_pe, rel_table)
            w_out = nsa_w_out.astype(bf)
        elif kind == 1:
            w_in = moba_w_in.astype(bf)
            pp = norm_matmul(xp, g_mix, w_in)
            op, moba_p = _moba_prompt(pp, bp, tp, rel_table)
            os_, moba_s = _moba_sample_pallas(norm_matmul(xs, g_mix, w_in), cache_moba_kv, page_table, rel_table)
            op, os_ = [op], [os_]
            w_out = moba_w_out.astype(bf)
        elif kind == 2:
            w_in = hg_w_in.astype(bf)
            pp = norm_matmul(xp, g_mix, w_in)
            ps = norm_matmul(xs, g_mix, w_in).reshape(bs, ts, -1)
            s0 = jnp.zeros((bp, HG_HEADS, HG_DK, HG_DV), jnp.float32)
            op, hg_p = hgrn2_scan(pp, hg_lb_logits, hg_norm, s0, layer, bp, tp)
            os_, hg_s = _hgrn2_core(ps, state_hgrn2, hg_lb_logits, layer, hg_norm)
            op, os_ = [op], [os_.reshape(bs * ts, -1)]
            w_out = hg_w_out.astype(bf)
        else:
            w_main, w_tail = gdn_w_in[:, :GDN_MAIN].astype(bf), _pad_cols(gdn_w_in[:, GDN_MAIN:]).astype(bf)
            main_p, tail_p = norm_matmul(xp, g_mix, w_main), norm_matmul(xp, g_mix, w_tail)
            ps = jnp.concatenate([norm_matmul(xs, g_mix, w_main), norm_matmul(xs, g_mix, w_tail)], axis=-1)
            buf0 = jnp.zeros((bp, GDN_CONV - 1, GDN_CONV_DIM), jnp.float32)
            s0 = jnp.zeros((bp, GDN_V_HEADS, GDN_DK, GDN_DV), jnp.float32)
            op, ssm_p = gdn_scan(main_p, tail_p, buf0, gdn_conv_w, gdn_a_log, gdn_dt_bias, gdn_norm, s0, bp, tp)
            conv_p = main_p.reshape(bp, tp, -1)[:, tp - (GDN_CONV - 1):, :GDN_CONV_DIM]
            os_, conv_s, ssm_s = _gdn_core(ps.reshape(bs, ts, -1), state_gdn_conv, state_gdn_ssm, gdn_conv_w,
                                           gdn_a_log, gdn_dt_bias, gdn_norm)
            op, os_ = [op], [os_.reshape(bs * ts, -1)]
            w_out = gdn_w_out.astype(bf)
        xp = matmul_res(op, w_out, xp)
        xs = matmul_res(os_, w_out, xs)
        w_up, w_down = ffn_w_up[layer].astype(bf), ffn_w_down[layer].astype(bf)
        xp = ffn(xp, ln_ffn[layer], w_up, w_down)
        xs = ffn(xs, ln_ffn[layer], w_up, w_down)
    y_prompt = final_norm(xp, ln_final).reshape(bp, tp, D_MODEL)
    y_sample = final_norm(xs, ln_final).reshape(bs, ts, D_MODEL)
    return (y_prompt, y_sample, nsa_cmp_p, nsa_cmp_s, nsa_sel_p, nsa_sel_s, nsa_win_p, nsa_win_s,
            moba_p, moba_s, hg_p, hg_s, conv_p, conv_s, ssm_p, ssm_s)
```

```python
import functools
import math

import jax
import jax.numpy as jnp
import numpy as np
from jax import lax
from jax.experimental import pallas as pl
from jax.experimental.pallas import tpu as pltpu

D_MODEL = 2048
DEPTH = 4
N_MIXERS = 4
HEAD_DIM = 128
N_HEADS = D_MODEL // HEAD_DIM
KV_HEADS = 4
GROUP = N_HEADS // KV_HEADS
ATTN_SCALE = HEAD_DIM ** -0.5
REL_BUCKETS = 32
REL_MAX_DIST = 128
CMP_BLOCK = 32
CMP_STRIDE = 16
CMP_HIDDEN = HEAD_DIM
SEL_BLOCK = 64
N_SEL = 16
WINDOW = 512
FORCE_SCORE = 1.0e4
MOBA_BLOCK = 256
MOBA_TOPK = 3
HG_DK = 128
HG_HEADS = D_MODEL // HG_DK
HG_DV = D_MODEL // HG_HEADS
GDN_DK = 128
GDN_DV = 128
GDN_QK_HEADS = D_MODEL // GDN_DK
GDN_V_HEADS = 2 * GDN_QK_HEADS
GDN_REP = GDN_V_HEADS // GDN_QK_HEADS
GDN_CONV = 4
GDN_CONV_DIM = 2 * GDN_QK_HEADS * GDN_DK + GDN_V_HEADS * GDN_DV
CHUNK = 64
NEG_INF = -1.0e30
NORM_EPS = 1e-6

Q_WIDTH = N_HEADS * HEAD_DIM
KV_WIDTH = 2 * KV_HEADS * HEAD_DIM
NSA_MAIN = Q_WIDTH + 3 * KV_WIDTH
GDN_MAIN = GDN_CONV_DIM + GDN_V_HEADS * GDN_DV

V7X_VMEM_LIMIT_BYTES = 56 * 1024 * 1024
LANE = 128
SUBLANE = 8
ATTN_TILE = 256
CMP_TILE = 128
SCAN_TILE = 512
GDN_HPS = 2
LOG2E = math.log2(math.e)
NT_DIMS = (((1,), (1,)), ((), ()))
TN_DIMS = (((0,), (0,)), ((), ()))


def _cparams(*sem):
    return pltpu.CompilerParams(dimension_semantics=sem, vmem_limit_bytes=V7X_VMEM_LIMIT_BYTES)


def _row_tile(m, target):
    t = min(m, target)
    while m % t:
        t //= 2
    return t


def _col_tile(n, target):
    t = min(n, target)
    while n % t or t % LANE:
        t -= LANE
    return t


def _split_bf16(x, parts):
    out = []
    for _ in range(parts - 1):
        hi = x.astype(jnp.bfloat16)
        out.append(hi)
        x = x - hi.astype(jnp.float32)
    out.append(x.astype(jnp.bfloat16))
    return out


def _bf(x):
    return x.astype(jnp.bfloat16)


def _dot(a, b, dims=None):
    if dims is None:
        return jnp.dot(a, b, preferred_element_type=jnp.float32)
    return lax.dot_general(a, b, dims, preferred_element_type=jnp.float32)


def _norm_matmul_body(x_ref, g_ref, w_ref, o_ref, h_ref):
    @pl.when(pl.program_id(1) == 0)
    def _():
        x = x_ref[...]
        ms = jnp.mean(x * x, axis=-1, keepdims=True)
        h_ref[...] = _bf(x * lax.rsqrt(ms + NORM_EPS) * g_ref[...])

    o_ref[...] = _dot(h_ref[...], w_ref[...])


def norm_matmul(x, g, w):
    m, k = x.shape
    n = w.shape[1]
    tm = _row_tile(m, 1024)
    tn = _col_tile(n, 1024)
    return pl.pallas_call(
        _norm_matmul_body,
        grid=(m // tm, n // tn),
        in_specs=[pl.BlockSpec((tm, k), lambda i, j: (i, 0)),
                  pl.BlockSpec((1, k), lambda i, j: (0, 0)),
                  pl.BlockSpec((k, tn), lambda i, j: (0, j))],
        out_specs=pl.BlockSpec((tm, tn), lambda i, j: (i, j)),
        out_shape=jax.ShapeDtypeStruct((m, n), jnp.float32),
        scratch_shapes=[pltpu.VMEM((tm, k), jnp.bfloat16)],
        compiler_params=_cparams("parallel", "arbitrary"),
        name="norm_matmul",
    )(x, g.reshape(1, k), w)


def _matmul_res_body(*refs):
    *a_refs, w_ref, r_ref, o_ref = refs
    a = a_refs[0][...]
    for a_ref in a_refs[1:]:
        a = a + a_ref[...]
    o_ref[...] = r_ref[...] + _dot(_bf(a), w_ref[...])


def matmul_res(a_list, w, res):
    m, k = a_list[0].shape
    n = w.shape[1]
    tm = _row_tile(m, 512)
    tn = _col_tile(n, 1024)
    return pl.pallas_call(
        _matmul_res_body,
        grid=(m // tm, n // tn),
        in_specs=[pl.BlockSpec((tm, k), lambda i, j: (i, 0)) for _ in a_list]
        + [pl.BlockSpec((k, tn), lambda i, j: (0, j)),
           pl.BlockSpec((tm, tn), lambda i, j: (i, j))],
        out_specs=pl.BlockSpec((tm, tn), lambda i, j: (i, j)),
        out_shape=jax.ShapeDtypeStruct((m, n), jnp.float32),
        compiler_params=_cparams("parallel", "arbitrary"),
        name="matmul_res",
    )(*a_list, w, res)


def _ffn_body(x_ref, g_ref, wa_ref, wb_ref, wd_ref, o_ref, h_ref):
    @pl.when(pl.program_id(1) == 0)
    def _():
        x = x_ref[...]
        ms = jnp.mean(x * x, axis=-1, keepdims=True)
        h_ref[...] = _bf(x * lax.rsqrt(ms + NORM_EPS) * g_ref[...])
        o_ref[...] = x

    h = h_ref[...]
    a = _dot(h, wa_ref[...])
    b = _dot(h, wb_ref[...])
    o_ref[...] += _dot(_bf(a * jax.nn.sigmoid(a) * b), wd_ref[...])


def ffn(x, g, w_up, w_down):
    m, k = x.shape
    hdim = w_down.shape[0]
    tm = _row_tile(m, 1024)
    th = _col_tile(hdim, 512)
    nh = hdim // th
    return pl.pallas_call(
        _ffn_body,
        grid=(m // tm, nh),
        in_specs=[pl.BlockSpec((tm, k), lambda i, j: (i, 0)),
                  pl.BlockSpec((1, k), lambda i, j: (0, 0)),
                  pl.BlockSpec((k, th), lambda i, j: (0, j)),
                  pl.BlockSpec((k, th), lambda i, j: (0, j + nh)),
                  pl.BlockSpec((th, k), lambda i, j: (j, 0))],
        out_specs=pl.BlockSpec((tm, k), lambda i, j: (i, 0)),
        out_shape=jax.ShapeDtypeStruct((m, k), jnp.float32),
        scratch_shapes=[pltpu.VMEM((tm, k), jnp.bfloat16)],
        compiler_params=_cparams("parallel", "arbitrary"),
        name="ffn",
    )(x, g.reshape(1, k), w_up, w_up, w_down)


def _norm_body(x_ref, g_ref, o_ref):
    x = x_ref[...]
    ms = jnp.mean(x * x, axis=-1, keepdims=True)
    o_ref[...] = x * lax.rsqrt(ms + NORM_EPS) * g_ref[...]


def final_norm(x, g):
    m, k = x.shape
    tm = _row_tile(m, 512)
    return pl.pallas_call(
        _norm_body,
        grid=(m // tm,),
        in_specs=[pl.BlockSpec((tm, k), lambda i: (i, 0)), pl.BlockSpec((1, k), lambda i: (0, 0))],
        out_specs=pl.BlockSpec((tm, k), lambda i: (i, 0)),
        out_shape=jax.ShapeDtypeStruct((m, k), jnp.float32),
        compiler_params=_cparams("parallel"),
        name="final_norm",
    )(x, g.reshape(1, k))


def _bucket_np(dist):
    exact = REL_BUCKETS // 2
    d = np.maximum(dist, 0)
    ratio = np.log(np.maximum(d, 1).astype(np.float32) / exact) / math.log(REL_MAX_DIST / exact)
    large = np.minimum(exact + (ratio * (REL_BUCKETS - exact)).astype(np.int32), REL_BUCKETS - 1)
    return np.where(d < exact, d, large)


def _bias_lookup(rel_table, dist):
    bucket = _bucket_np(dist).astype(np.int32)
    ids = [int(b) for b in np.unique(bucket)]
    bk = jnp.asarray(bucket)[..., None]
    out = jnp.broadcast_to(rel_table[ids[0]], bucket.shape + (rel_table.shape[1],))
    for b in ids[1:]:
        out = jnp.where(bk == b, rel_table[b], out)
    return out


def _heads_to_lanes(t):
    keys, queries, _ = t.shape
    return t.reshape(keys, queries, KV_HEADS, GROUP).transpose(2, 0, 3, 1).reshape(KV_HEADS, keys, GROUP * queries)


def _flash_bias_tiles(rel_table, window):
    j = np.arange(ATTN_TILE)[:, None]
    i = np.arange(ATTN_TILE)[None, :]
    n_cls = window // ATTN_TILE + 1 if window else -(-REL_MAX_DIST // ATTN_TILE) + 2
    tiles = []
    for d in range(n_cls):
        dist = d * ATTN_TILE + i - j
        ok = dist >= 0
        if window:
            ok = ok & (dist < window)
        tiles.append(_heads_to_lanes(jnp.where(ok[..., None], _bias_lookup(rel_table, dist) * LOG2E, NEG_INF)))
    return jnp.stack(tiles, axis=1)


def _cmp_bias_table(rel_table, ncp):
    x = np.arange(ncp)[:, None]
    i = np.arange(CMP_TILE)[None, :]
    dist = i - CMP_STRIDE * (x - 16) - (CMP_BLOCK - 1)
    far = rel_table[REL_BUCKETS - 1]
    b = _heads_to_lanes(jnp.where((dist >= 0)[..., None], _bias_lookup(rel_table, dist), far))
    return jnp.concatenate([b, b], axis=1)


def _stack_heads(q):
    return jnp.concatenate([q[:, r * HEAD_DIM:(r + 1) * HEAD_DIM] for r in range(GROUP)], axis=0)


def _gate_columns(gt_ref, col0):
    gt = jax.nn.sigmoid(gt_ref[...])
    lane = lax.broadcasted_iota(jnp.int32, gt.shape, 1)
    return [jnp.sum(jnp.where(lane == col0 + r, gt, 0.0), axis=1, keepdims=True) for r in range(GROUP)]


def _heads_from_lanes(o_t, rows, cols=None):
    parts = []
    for r in range(GROUP):
        part = o_t[:, r * rows:(r + 1) * rows].T
        if cols is not None:
            part = part * cols[r]
        parts.append(part)
    return jnp.concatenate(parts, axis=1)


def _rank_rows(score, n_rows):
    row = lax.broadcasted_iota(jnp.int32, score.shape, 0)
    rank = jnp.zeros(score.shape, jnp.int32)
    for mm in range(n_rows):
        sm = score[mm:mm + 1, :]
        ahead = (sm > score) | ((sm == score) & (row > mm))
        rank = rank + ahead.astype(jnp.int32)
    return rank


def _cmp_select_body(q_ref, kc_ref, dt_ref, gt_ref, o_ref, pen_ref, *, n_sel_blocks):
    g = pl.program_id(1)
    qi = pl.program_id(2)
    tq = CMP_TILE
    cols = GROUP * tq
    ncp = kc_ref.shape[1]
    q4 = _bf(_stack_heads(q_ref[...]) * ATTN_SCALE)
    s = _dot(_bf(kc_ref[0]), q4, NT_DIMS)
    shift = (qi * (tq // CMP_STRIDE) + ncp - 16) % ncp
    bias = dt_ref[pl.ds(pl.multiple_of(ncp - shift, SUBLANE), ncp), :]
    t_col = qi * tq + (lax.broadcasted_iota(jnp.int32, (ncp, cols), 1) & (tq - 1))
    end_pos = lax.broadcasted_iota(jnp.int32, (ncp, cols), 0) * CMP_STRIDE + (CMP_BLOCK - 1)
    mask = t_col >= end_pos
    s = jnp.where(mask, s + bias, NEG_INF)
    m = jnp.max(s, axis=0, keepdims=True)
    e = jnp.where(mask, jnp.exp(s - m), 0.0)
    p = e / jnp.maximum(jnp.sum(e, axis=0, keepdims=True), 1e-30)
    o_t = _dot(_bf(kc_ref[1].T), _bf(p))
    o_ref[...] = _heads_from_lanes(o_t, tq, _gate_columns(gt_ref, g * GROUP))

    imp = p[:, 0:tq]
    for r in range(1, GROUP):
        imp = imp + p[:, r * tq:(r + 1) * tq]
    ratio = SEL_BLOCK // CMP_STRIDE
    j_i = lax.broadcasted_iota(jnp.int32, (n_sel_blocks, ncp), 0)
    c_i = lax.broadcasted_iota(jnp.int32, (n_sel_blocks, ncp), 1)
    w = _bf((c_i >= ratio * j_i - 1) & (c_i <= ratio * j_i + ratio - 1))
    score = sum(_dot(w, part) for part in _split_bf16(imp, 3))
    blk = lax.broadcasted_iota(jnp.int32, (n_sel_blocks, tq), 0)
    tok = qi * tq + lax.broadcasted_iota(jnp.int32, (n_sel_blocks, tq), 1)
    cur = lax.shift_right_logical(tok, int(math.log2(SEL_BLOCK)))
    forced = (blk == 0) | (blk == cur) | (blk == cur - 1)
    causal = blk <= cur
    score = jnp.where(forced, FORCE_SCORE, score)
    score = jnp.where(causal, score, -1.0)
    chosen = (_rank_rows(score, n_sel_blocks) < N_SEL) & causal
    pen_ref[...] = jnp.where(chosen, 0.0, NEG_INF)


def cmp_select(proj, tail, kc, dt, batch, seq):
    tq = CMP_TILE
    nq = seq // tq
    ncp = kc.shape[3]
    nsb = seq // SEL_BLOCK
    return pl.pallas_call(
        functools.partial(_cmp_select_body, n_sel_blocks=nsb),
        grid=(batch, KV_HEADS, nq),
        in_specs=[pl.BlockSpec((tq, GROUP * HEAD_DIM), lambda b, g, i: (b * nq + i, g)),
                  pl.BlockSpec((None, None, 2, ncp, HEAD_DIM), lambda b, g, i: (b, g, 0, 0, 0)),
                  pl.BlockSpec((None, 2 * ncp, GROUP * tq), lambda b, g, i: (g, 0, 0)),
                  pl.BlockSpec((tq, LANE), lambda b, g, i: (b * nq + i, 0))],
        out_specs=[pl.BlockSpec((tq, GROUP * HEAD_DIM), lambda b, g, i: (b * nq + i, g)),
                   pl.BlockSpec((None, None, nsb, tq), lambda b, g, i: (b, g, 0, i))],
        out_shape=[jax.ShapeDtypeStruct((batch * seq, Q_WIDTH), jnp.float32),
                   jax.ShapeDtypeStruct((batch, KV_HEADS, nsb, seq), jnp.float32)],
        compiler_params=_cparams("parallel", "parallel", "arbitrary"),
        name="cmp_select",
    )(proj, kc, dt, tail)


def _flash_body(*refs, pen_block, pen_per_head, k_back, gate_col0, seq):
    it = iter(refs)
    q_ref, k_ref, v_ref, bt_ref = next(it), next(it), next(it), next(it)
    pen_ref = next(it) if pen_block else None
    gt_ref = next(it) if gate_col0 is not None else None
    o_ref, m_ref, acc_ref, qa_ref, kb_ref, vt_ref, sa_ref, sb_ref = (next(it) for _ in range(8))
    g = pl.program_id(1)
    qi = pl.program_id(2)
    tq = tk = ATTN_TILE
    n_cls = bt_ref.shape[0]

    @pl.when(qi == 0)
    def _():
        vt_ref[HEAD_DIM:, :] = jnp.ones((SUBLANE, seq), jnp.bfloat16)
        for c in range(seq // tk):
            kb_ref[c * tk:(c + 1) * tk, :] = _bf(k_ref[c * tk:(c + 1) * tk, :])
            vt_ref[0:HEAD_DIM, c * tk:(c + 1) * tk] = _bf(v_ref[c * tk:(c + 1) * tk, :].T)

    qa_ref[...] = _bf(_stack_heads(q_ref[...]) * (ATTN_SCALE * LOG2E))
    m_ref[...] = jnp.full(m_ref.shape, NEG_INF, jnp.float32)
    acc_ref[...] = jnp.zeros(acc_ref.shape, jnp.float32)

    def raw_logits(kj):
        k0 = pl.multiple_of(jnp.minimum(kj, qi) * tk, tk)
        return _dot(kb_ref[pl.ds(k0, tk), :], qa_ref[...], NT_DIMS)

    def attend(s_ref, kj):
        kc = jnp.minimum(kj, qi)
        k0 = pl.multiple_of(kc * tk, tk)
        skip = jnp.where(kj <= qi, 0.0, NEG_INF)
        bias = bt_ref[jnp.minimum(qi - kc, n_cls - 1)]
        if pen_block:
            per_tile = tk // pen_block
            pieces = []
            for a in range(per_tile):
                pen = pen_ref[pl.ds(kc * per_tile + a, 1), :] + skip
                if not pen_per_head:
                    pen = jnp.concatenate([pen] * GROUP, axis=1)
                rows = slice(a * pen_block, (a + 1) * pen_block)
                pieces.append(s_ref[rows, :] + bias[rows] + pen)
            s = pieces[0] if per_tile == 1 else jnp.concatenate(pieces, axis=0)
        else:
            s = s_ref[...] + bias + skip
        m_prev = m_ref[...]
        m_new = jnp.maximum(m_prev, jnp.max(s, axis=0, keepdims=True))
        p = jnp.exp2(s - m_new)
        acc_ref[...] = jnp.exp2(m_prev - m_new) * acc_ref[...] + _dot(vt_ref[:, pl.ds(k0, tk)], _bf(p))
        m_ref[...] = m_new

    k_lo = jnp.maximum(qi - k_back, 0) if k_back is not None else 0
    sa_ref[...] = raw_logits(k_lo)

    def pair(pi, carry):
        ka = k_lo + 2 * pi
        sb_ref[...] = raw_logits(ka + 1)
        attend(sa_ref, ka)
        sa_ref[...] = raw_logits(ka + 2)
        attend(sb_ref, ka + 1)
        return carry

    lax.fori_loop(0, (qi - k_lo + 2) // 2, pair, 0)
    acc = acc_ref[...]
    o_t = acc[0:HEAD_DIM] / jnp.maximum(acc[HEAD_DIM:HEAD_DIM + 1], 1e-30)
    cols_g = _gate_columns(gt_ref, gate_col0 + g * GROUP) if gate_col0 is not None else None
    o_ref[...] = _heads_from_lanes(o_t, tq, cols_g)


def flash_attention(q_arr, kv_arr, k_col, v_col, bias, batch, seq, *, pen=None, pen_block=0,
                    k_back=None, gate_arr=None, gate_col0=None):
    tq = ATTN_TILE
    nq = seq // tq
    cols = GROUP * tq
    assert k_back is None or k_back == bias.shape[1] - 1
    in_specs = [pl.BlockSpec((tq, GROUP * HEAD_DIM), lambda b, g, i: (b * nq + i, g)),
                pl.BlockSpec((seq, HEAD_DIM), lambda b, g, i: (b, k_col + g)),
                pl.BlockSpec((seq, HEAD_DIM), lambda b, g, i: (b, v_col + g)),
                pl.BlockSpec((None,) + bias.shape[1:], lambda b, g, i: (g, 0, 0, 0))]
    args = [q_arr, kv_arr, kv_arr, bias]
    pen_per_head = False
    if pen is not None:
        if pen.ndim == 4:
            in_specs.append(pl.BlockSpec((None, None, pen.shape[2], tq), lambda b, g, i: (b, g, 0, i)))
        else:
            pen_per_head = True
            in_specs.append(pl.BlockSpec((None, None, None, pen.shape[3], cols), lambda b, g, i: (b, g, i, 0, 0)))
        args.append(pen)
    if gate_arr is not None:
        in_specs.append(pl.BlockSpec((tq, LANE), lambda b, g, i: (b * nq + i, 0)))
        args.append(gate_arr)
    return pl.pallas_call(
        functools.partial(_flash_body, pen_block=pen_block if pen is not None else 0, pen_per_head=pen_per_head,
                          k_back=k_back, gate_col0=gate_col0 if gate_arr is not None else None, seq=seq),
        grid=(batch, KV_HEADS, nq),
        in_specs=in_specs,
        out_specs=pl.BlockSpec((tq, GROUP * HEAD_DIM), lambda b, g, i: (b * nq + i, g)),
        out_shape=jax.ShapeDtypeStruct((batch * seq, Q_WIDTH), jnp.float32),
        scratch_shapes=[pltpu.VMEM((1, cols), jnp.float32),
                        pltpu.VMEM((HEAD_DIM + SUBLANE, cols), jnp.float32),
                        pltpu.VMEM((cols, HEAD_DIM), jnp.bfloat16),
                        pltpu.VMEM((seq, HEAD_DIM), jnp.bfloat16),
                        pltpu.VMEM((HEAD_DIM + SUBLANE, seq), jnp.bfloat16),
                        pltpu.VMEM((tq, cols), jnp.float32), pltpu.VMEM((tq, cols), jnp.float32)],
        compiler_params=_cparams("parallel", "parallel", "arbitrary"),
        name="flash_attention",
    )(*args)


def _moba_gate_body(q_ref, k_ref, pen_ref, km_ref, *, n_blocks):
    qi = pl.program_id(2)
    tq = ATTN_TILE
    cols = GROUP * tq

    @pl.when(qi == 0)
    def _():
        k = k_ref[...]
        km_ref[...] = jnp.sum(k.reshape(n_blocks, MOBA_BLOCK, HEAD_DIM), axis=1) / MOBA_BLOCK

    qh, ql = _split_bf16(_stack_heads(q_ref[...]), 2)
    kh, kl = _split_bf16(km_ref[...], 2)
    gate = _dot(kh, qh, NT_DIMS) + _dot(kh, ql, NT_DIMS) + _dot(kl, qh, NT_DIMS)
    blk = lax.broadcasted_iota(jnp.int32, (n_blocks, cols), 0)
    tok = qi * tq + (lax.broadcasted_iota(jnp.int32, (n_blocks, cols), 1) & (tq - 1))
    own = lax.shift_right_logical(tok, int(math.log2(MOBA_BLOCK)))
    gate = jnp.where(blk < own, gate, NEG_INF)
    chosen = ((_rank_rows(gate, n_blocks) < MOBA_TOPK) & (blk < own)) | (blk == own)
    pen_ref[...] = jnp.where(chosen, 0.0, NEG_INF)


def moba_gate(proj, k_col, batch, seq):
    tq = ATTN_TILE
    nq = seq // tq
    cols = GROUP * tq
    nb = seq // MOBA_BLOCK
    return pl.pallas_call(
        functools.partial(_moba_gate_body, n_blocks=nb),
        grid=(batch, KV_HEADS, nq),
        in_specs=[pl.BlockSpec((tq, GROUP * HEAD_DIM), lambda b, g, i: (b * nq + i, g)),
                  pl.BlockSpec((seq, HEAD_DIM), lambda b, g, i: (b, k_col + g))],
        out_specs=pl.BlockSpec((None, None, None, nb, cols), lambda b, g, i: (b, g, i, 0, 0)),
        out_shape=jax.ShapeDtypeStruct((batch, KV_HEADS, nq, nb, cols), jnp.float32),
        scratch_shapes=[pltpu.VMEM((nb, HEAD_DIM), jnp.float32)],
        compiler_params=_cparams("parallel", "parallel", "arbitrary"),
        name="moba_gate",
    )(proj, proj)


def _tril_ones(n, strict=False):
    r = lax.broadcasted_iota(jnp.int32, (n, n), 0)
    c = lax.broadcasted_iota(jnp.int32, (n, n), 1)
    return (r > c) if strict else (r >= c)


def _chunk_cumsum(x):
    tril = _bf(_tril_ones(x.shape[0]))
    return sum(_dot(tril, part) for part in _split_bf16(x, 3))


def _head_rms_gate(o, norm_g, gate):
    ms = jnp.mean(o * o, axis=-1, keepdims=True)
    return o * lax.rsqrt(ms + NORM_EPS) * norm_g * (gate * jax.nn.sigmoid(gate))


def _hgrn2_body(q_ref, f_ref, i_ref, g_ref, lbl_ref, ng_ref, s0_ref, o_ref, s_out_ref, st_ref, *, layer):
    ti = pl.program_id(2)

    @pl.when(ti == 0)
    def _():
        st_ref[...] = s0_ref[...].T

    lbl = lbl_ref[...]
    e = jnp.exp(lbl - jnp.max(lbl, axis=0, keepdims=True))
    p = e / jnp.sum(e, axis=0, keepdims=True)
    lb = jnp.zeros((1, HG_DK), jnp.float32)
    for r in range(1, layer + 1):
        lb = lb + p[r:r + 1]
    causal = _tril_ones(CHUNK)
    work = []
    for c in range(q_ref.shape[0] // CHUNK):
        sl = slice(c * CHUNK, (c + 1) * CHUNK)
        q = q_ref[sl, :]
        qh = q * jax.nn.sigmoid(q) * HG_DK ** -0.5
        fg = lb + (1.0 - lb) * jax.nn.sigmoid(f_ref[sl, :])
        k = 1.0 - fg
        v = _bf(i_ref[sl, :])
        b = _chunk_cumsum(jnp.log(fg))
        b_mid = b[CHUNK // 2:CHUNK // 2 + 1]
        b_last = b[CHUNK - 1:CHUNK]
        a = _dot(_bf(qh * jnp.exp(b - b_mid)), _bf(k * jnp.exp(b_mid - b)), NT_DIMS)
        a = jnp.where(causal, a, 0.0)
        work.append((sl, _dot(_bf(a), v), _bf(qh * jnp.exp(b)), jnp.exp(b_last),
                     _dot(v, _bf(k * jnp.exp(b_last - b)), TN_DIMS)))
    for sl, o_intra, q_in, d_last, kv in work:
        st = st_ref[...]
        o = o_intra + _dot(q_in, _bf(st), NT_DIMS)
        st_ref[...] = st * d_last + kv
        o_ref[sl, :] = _head_rms_gate(o, ng_ref[...], g_ref[sl, :])

    @pl.when(ti == pl.num_programs(2) - 1)
    def _():
        s_out_ref[...] = st_ref[...].T


def hgrn2_scan(proj, lb_logits, norm_g, s0, layer, batch, seq):
    tt = _row_tile(seq, SCAN_TILE)
    nt = seq // tt
    h = HG_HEADS

    def col(k):
        return pl.BlockSpec((tt, HG_DK), lambda b, hh, t: (b * nt + t, k * h + hh))

    return pl.pallas_call(
        functools.partial(_hgrn2_body, layer=layer),
        grid=(batch, h, nt),
        in_specs=[col(0), col(1), col(2), col(3),
                  pl.BlockSpec((DEPTH, HG_DK), lambda b, hh, t: (0, hh)),
                  pl.BlockSpec((1, HG_DV), lambda b, hh, t: (0, 0)),
                  pl.BlockSpec((None, None, HG_DK, HG_DV), lambda b, hh, t: (b, hh, 0, 0))],
        out_specs=[pl.BlockSpec((tt, HG_DV), lambda b, hh, t: (b * nt + t, hh)),
                   pl.BlockSpec((None, None, HG_DK, HG_DV), lambda b, hh, t: (b, hh, 0, 0))],
        out_shape=[jax.ShapeDtypeStruct((batch * seq, h * HG_DV), jnp.float32),
                   jax.ShapeDtypeStruct((batch, h, HG_DK, HG_DV), jnp.float32)],
        scratch_shapes=[pltpu.VMEM((HG_DV, HG_DK), jnp.float32)],
        compiler_params=_cparams("parallel", "parallel", "arbitrary"),
        name="hgrn2_scan",
    )(proj, proj, proj, proj, lb_logits, norm_g.reshape(1, HG_DV), s0)


def _lane_column(x, lane_idx):
    lane = lax.broadcasted_iota(jnp.int32, x.shape, 1)
    return jnp.sum(jnp.where(lane == lane_idx, x, 0.0), axis=1, keepdims=True)


def _softplus(x):
    return jnp.maximum(x, 0.0) + jnp.log(1.0 + jnp.exp(-jnp.abs(x)))


def _l2n(x):
    return x * lax.rsqrt(jnp.sum(x * x, axis=-1, keepdims=True) + NORM_EPS)


def _gdn_body(q_ref, k_ref, v_ref, z_ref, t_ref, bq_ref, bk_ref, bv_ref, wq_ref, wk_ref, wv_ref,
              al_ref, dtb_ref, ng_ref, s0_ref, o_ref, s_out_ref, xs_ref, y_ref, s_ref):
    hq0 = pl.program_id(1) * GDN_HPS
    n_v = GDN_HPS * GDN_REP
    ti = pl.program_id(2)
    tt = q_ref.shape[0]
    dk, dv = GDN_DK, GDN_DV
    pad = SUBLANE

    @pl.when(ti == 0)
    def _():
        s_ref[...] = s0_ref[...]
        xs_ref[0:pad, :] = jnp.concatenate([bq_ref[...], bk_ref[...], bv_ref[...]], axis=1)

    x = jnp.concatenate([q_ref[...], k_ref[...], v_ref[...]], axis=1)
    xs_ref[pad:, :] = x
    cw = jnp.concatenate([wq_ref[...], wk_ref[...], wv_ref[...]], axis=1)
    y = xs_ref[pad - 3:pad - 3 + tt, :] * cw[0:1]
    for i in range(1, GDN_CONV - 1):
        y = y + xs_ref[pad - 3 + i:pad - 3 + i + tt, :] * cw[i:i + 1]
    y = y + x * cw[GDN_CONV - 1:GDN_CONV]
    xs_ref[0:pad, :] = x[tt - pad:tt]
    y_ref[...] = y * jax.nn.sigmoid(y)

    strict = _tril_ones(CHUNK, strict=True)
    incl = _tril_ones(CHUNK)
    sel_rows = lax.shift_right_logical(lax.broadcasted_iota(jnp.int32, (n_v * CHUNK, LANE), 0), int(math.log2(CHUNK)))
    sel_lane = lax.broadcasted_iota(jnp.int32, (n_v * CHUNK, LANE), 1)
    pick = _bf(sel_lane == GDN_V_HEADS + hq0 * GDN_REP + sel_rows)
    n_chunks = tt // CHUNK

    work = []
    for c in range(n_chunks):
        sl = slice(c * CHUNK, (c + 1) * CHUNK)
        yc = y_ref[sl, :]
        tl = t_ref[sl, :]
        beta_all = jax.nn.sigmoid(tl)
        g_all = _chunk_cumsum(-jnp.exp(al_ref[...]) * _softplus(tl + dtb_ref[...]))
        g_rows = sum(_dot(pick, part, NT_DIMS) for part in _split_bf16(g_all, 3))
        for hh in range(GDN_HPS):
            q = _l2n(yc[:, hh * dk:(hh + 1) * dk]) * dk ** -0.5
            k = _l2n(yc[:, (GDN_HPS + hh) * dk:(GDN_HPS + hh + 1) * dk])
            qb, kb = _bf(q), _bf(k)
            kk = _dot(kb, kb, NT_DIMS)
            qk = _dot(qb, kb, NT_DIMS)
            for e in range(hh * GDN_REP, (hh + 1) * GDN_REP):
                hv = hq0 * GDN_REP + e
                v = yc[:, 2 * GDN_HPS * dk + e * dv:2 * GDN_HPS * dk + (e + 1) * dv]
                bt = _lane_column(beta_all, hv)
                gc = _lane_column(g_all, GDN_V_HEADS + hv)
                gdiff = gc - g_rows[e * CHUNK:(e + 1) * CHUNK]
                d_strict = jnp.where(strict, jnp.exp(jnp.where(strict, gdiff, 0.0)), 0.0)
                d_incl = jnp.where(incl, jnp.exp(jnp.where(incl, gdiff, 0.0)), 0.0)
                eg = jnp.exp(gc)
                g_last = gc[CHUNK - 1:CHUNK]
                work.append(dict(
                    c=c, e=e, sol=jnp.concatenate([bt * v, (bt * eg) * k], axis=1), pw=bt * kk * d_strict,
                    aq=_bf(qk * d_incl), q_in=_bf(q * eg), k_out=_bf(k * jnp.exp(g_last - gc)),
                    d_last=jnp.exp(g_last)))

    r_i = lax.broadcasted_iota(jnp.int32, (CHUNK, CHUNK), 0)
    c_i = lax.broadcasted_iota(jnp.int32, (CHUNK, CHUNK), 1)
    same = [lax.shift_right_logical(r_i, sh) == lax.shift_right_logical(c_i, sh) for sh in range(3, 7)]
    eye = (r_i == c_i).astype(jnp.float32)
    for wk in work:
        l8 = jnp.where(same[0], wk["pw"], 0.0)
        l8b = _bf(l8)
        wk["t"] = eye - l8
        wk["p"] = _dot(l8b, l8b)
    for wk in work:
        pb = _bf(wk["p"])
        wk["t"] = wk["t"] + _dot(_bf(wk["t"]), pb)
        wk["p"] = _dot(pb, pb)
    for wk in work:
        wk["t"] = wk["t"] + _dot(_bf(wk["t"]), _bf(wk["p"]))
    for lvl in range(1, len(same)):
        for wk in work:
            tb = _bf(wk["t"])
            off = _bf(jnp.where(same[lvl] & jnp.logical_not(same[lvl - 1]), wk["pw"], 0.0))
            wk["t"] = wk["t"] - _dot(tb, _bf(_dot(off, tb)))
    for wk in work:
        wk["sol"] = _dot(_bf(wk["t"]), _bf(wk["sol"]))

    for wk in work:
        c, e = wk["c"], wk["e"]
        sl = slice(c * CHUNK, (c + 1) * CHUNK)
        u0, w = wk["sol"][:, :dv], wk["sol"][:, dv:]
        s = s_ref[e]
        sb = _bf(s)
        u = u0 - _dot(_bf(w), sb)
        o = _dot(wk["q_in"], sb) + _dot(wk["aq"], _bf(u))
        s_ref[e] = wk["d_last"] * s + _dot(wk["k_out"], _bf(u), TN_DIMS)
        o_ref[sl, e * dv:(e + 1) * dv] = _head_rms_gate(o, ng_ref[...], z_ref[sl, e * dv:(e + 1) * dv])

    @pl.when(ti == pl.num_programs(2) - 1)
    def _():
        s_out_ref[...] = s_ref[...]


def gdn_scan(main, tail, conv_buf, conv_w, a_log, dt_bias, norm_g, s0, batch, seq):
    tt = _row_tile(seq, SCAN_TILE)
    nt = seq // tt
    hq, rep = GDN_QK_HEADS // GDN_HPS, GDN_HPS * GDN_REP
    dk, dv = GDN_HPS * GDN_DK, GDN_DV
    vw = rep * dv
    buf = jnp.pad(conv_buf, ((0, 0), (SUBLANE - (GDN_CONV - 1), 0), (0, 0)))
    pad_lanes = jnp.zeros((LANE - 2 * GDN_V_HEADS,), jnp.float32)
    a_row = jnp.concatenate([jnp.zeros((GDN_V_HEADS,), jnp.float32), a_log, pad_lanes]).reshape(1, LANE)
    dt_row = jnp.concatenate([jnp.zeros((GDN_V_HEADS,), jnp.float32), dt_bias, pad_lanes]).reshape(1, LANE)
    k0 = hq
    v0 = 2 * hq * dk // vw
    z0 = GDN_CONV_DIM // vw
    row = lambda b, h, t: b * nt + t
    return pl.pallas_call(
        _gdn_body,
        grid=(batch, hq, nt),
        in_specs=[pl.BlockSpec((tt, dk), lambda b, h, t: (row(b, h, t), h)),
                  pl.BlockSpec((tt, dk), lambda b, h, t: (row(b, h, t), k0 + h)),
                  pl.BlockSpec((tt, vw), lambda b, h, t: (row(b, h, t), v0 + h)),
                  pl.BlockSpec((tt, vw), lambda b, h, t: (row(b, h, t), z0 + h)),
                  pl.BlockSpec((tt, LANE), lambda b, h, t: (row(b, h, t), 0)),
                  pl.BlockSpec((None, SUBLANE, dk), lambda b, h, t: (b, 0, h)),
                  pl.BlockSpec((None, SUBLANE, dk), lambda b, h, t: (b, 0, k0 + h)),
                  pl.BlockSpec((None, SUBLANE, vw), lambda b, h, t: (b, 0, v0 + h)),
                  pl.BlockSpec((GDN_CONV, dk), lambda b, h, t: (0, h)),
                  pl.BlockSpec((GDN_CONV, dk), lambda b, h, t: (0, k0 + h)),
                  pl.BlockSpec((GDN_CONV, vw), lambda b, h, t: (0, v0 + h)),
                  pl.BlockSpec((1, LANE), lambda b, h, t: (0, 0)),
                  pl.BlockSpec((1, LANE), lambda b, h, t: (0, 0)),
                  pl.BlockSpec((1, dv), lambda b, h, t: (0, 0)),
                  pl.BlockSpec((None, rep, GDN_DK, dv), lambda b, h, t: (b, h, 0, 0))],
        out_specs=[pl.BlockSpec((tt, vw), lambda b, h, t: (row(b, h, t), h)),
                   pl.BlockSpec((None, rep, GDN_DK, dv), lambda b, h, t: (b, h, 0, 0))],
        out_shape=[jax.ShapeDtypeStruct((batch * seq, GDN_V_HEADS * dv), jnp.float32),
                   jax.ShapeDtypeStruct((batch, GDN_V_HEADS, GDN_DK, dv), jnp.float32)],
        scratch_shapes=[pltpu.VMEM((tt + SUBLANE, 2 * dk + vw), jnp.float32),
                        pltpu.VMEM((tt, 2 * dk + vw), jnp.float32),
                        pltpu.VMEM((rep, GDN_DK, dv), jnp.float32)],
        compiler_params=_cparams("parallel", "parallel", "arbitrary"),
        name="gdn_scan",
    )(main, main, main, main, tail, buf, buf, buf, conv_w, conv_w, conv_w, a_row, dt_row,
      norm_g.reshape(1, dv), s0)


PAGE = 128
KV_SLABS = 2 * KV_HEADS
PAGE_ROWS = PAGE * KV_SLABS
CMP_PAGES = 8
CHUNKS_PER_PAGE = PAGE // CMP_STRIDE
ROWS = GROUP * 4


def _page_slab(pg, slab):
    return pg[pl.ds(slab, PAGE, stride=KV_SLABS), :]


def _compress_part_body(*refs, n_prefetch, paged):
    refs = refs[n_prefetch:]
    pages, w_ref, o_ref, xs_ref = refs[:CMP_PAGES], refs[CMP_PAGES], refs[CMP_PAGES + 1], refs[CMP_PAGES + 2]
    per_g = CMP_PAGES * CHUNKS_PER_PAGE
    for k, pg in enumerate(pages):
        for cg in range(KV_SLABS):
            xs_ref[k, cg] = _page_slab(pg, cg) if paged else pg[:, cg * HEAD_DIM:(cg + 1) * HEAD_DIM]
    for c in range(2):
        acc = None
        for l in range(CMP_STRIDE):
            rows = [xs_ref.at[k, c * KV_HEADS + g][pl.ds(l, CHUNKS_PER_PAGE, stride=CMP_STRIDE), :]
                    for g in range(KV_HEADS) for k in range(CMP_PAGES)]
            d = _dot(_bf(jnp.concatenate(rows, axis=0)), w_ref[c, l])
            acc = d if acc is None else acc + d
        for g in range(KV_HEADS):
            o_ref[c, g] = acc[g * per_g:(g + 1) * per_g]


def _compress_weights(cmp_w1):
    n_part = CMP_BLOCK // CMP_STRIDE
    w = cmp_w1.reshape(2, n_part, CMP_STRIDE, HEAD_DIM, CMP_HIDDEN).transpose(0, 2, 3, 1, 4)
    return w.reshape(2, CMP_STRIDE, HEAD_DIM, n_part * CMP_HIDDEN).astype(jnp.bfloat16)


def compress_part_rows(kv_arr, col_block, cmp_w1, batch, seq):
    w = _compress_weights(cmp_w1)
    pages_per_b = seq // PAGE
    steps = pages_per_b // CMP_PAGES
    per_g = CMP_PAGES * CHUNKS_PER_PAGE

    def page_spec(k):
        return pl.BlockSpec((PAGE, KV_WIDTH), lambda b, s: (b * pages_per_b + s * CMP_PAGES + k, col_block))

    return pl.pallas_call(
        functools.partial(_compress_part_body, n_prefetch=0, paged=False),
        grid=(batch, steps),
        in_specs=[page_spec(k) for k in range(CMP_PAGES)] + [pl.BlockSpec(w.shape, lambda b, s: (0, 0, 0, 0))],
        out_specs=pl.BlockSpec((None, 2, KV_HEADS, per_g, w.shape[-1]), lambda b, s: (b, 0, 0, s, 0)),
        out_shape=jax.ShapeDtypeStruct((batch, 2, KV_HEADS, seq // CMP_STRIDE, w.shape[-1]), jnp.float32),
        scratch_shapes=[pltpu.VMEM((CMP_PAGES, 2 * KV_HEADS, PAGE, HEAD_DIM), jnp.float32)],
        compiler_params=_cparams("parallel", "arbitrary"),
        name="compress_part_rows",
    )(*([kv_arr] * CMP_PAGES), w)


def compress_part_paged(pool, page_table, cmp_w1):
    w = _compress_weights(cmp_w1)
    batch, n_pages = page_table.shape
    steps = n_pages // CMP_PAGES
    per_g = CMP_PAGES * CHUNKS_PER_PAGE

    def page_spec(k):
        return pl.BlockSpec((None, PAGE_ROWS, HEAD_DIM), lambda b, s, pt: (pt[b, s * CMP_PAGES + k], 0, 0))

    return pl.pallas_call(
        functools.partial(_compress_part_body, n_prefetch=1, paged=True),
        grid_spec=pltpu.PrefetchScalarGridSpec(
            num_scalar_prefetch=1, grid=(batch, steps),
            in_specs=[page_spec(k) for k in range(CMP_PAGES)]
            + [pl.BlockSpec(w.shape, lambda b, s, pt: (0, 0, 0, 0))],
            out_specs=pl.BlockSpec((None, 2, KV_HEADS, per_g, w.shape[-1]), lambda b, s, pt: (b, 0, 0, s, 0)),
            scratch_shapes=[pltpu.VMEM((CMP_PAGES, 2 * KV_HEADS, PAGE, HEAD_DIM), jnp.float32)]),
        out_shape=jax.ShapeDtypeStruct((batch, 2, KV_HEADS, n_pages * CHUNKS_PER_PAGE, w.shape[-1]), jnp.float32),
        compiler_params=_cparams("parallel", "arbitrary"),
        name="compress_part_paged",
    )(page_table, *([pool] * CMP_PAGES), w)


def _gelu_tanh(x):
    return x * (0.5 * (1.0 + jnp.tanh(math.sqrt(2.0 / math.pi) * (x + 0.044715 * (x * x * x)))))


def _compress_finish_body(p_ref, peh_ref, w2_ref, o_ref):
    n = p_ref.shape[1]
    for c in range(2):
        p = p_ref[c]
        hid = peh_ref[c:c + 1, :] + p[:, :CMP_HIDDEN]
        hid = hid + pltpu.roll(p[:, CMP_HIDDEN:], n - 1, 0)
        o_ref[c] = _dot(_bf(_gelu_tanh(hid)), _bf(w2_ref[c]))


def compress_finish(part, cmp_w1, cmp_w2, cmp_pe):
    batch, _, _, n, width = part.shape
    pe_hid = jnp.einsum('cld,cldh->ch', cmp_pe, cmp_w1)
    return pl.pallas_call(
        _compress_finish_body,
        grid=(batch, KV_HEADS),
        in_specs=[pl.BlockSpec((None, 2, None, n, width), lambda b, g: (b, 0, g, 0, 0)),
                  pl.BlockSpec((2, CMP_HIDDEN), lambda b, g: (0, 0)),
                  pl.BlockSpec((2, CMP_HIDDEN, HEAD_DIM), lambda b, g: (0, 0, 0))],
        out_specs=pl.BlockSpec((None, None, 2, n, HEAD_DIM), lambda b, g: (b, g, 0, 0, 0)),
        out_shape=jax.ShapeDtypeStruct((batch, KV_HEADS, 2, n, HEAD_DIM), jnp.float32),
        compiler_params=_cparams("parallel", "parallel"),
        name="compress_finish",
    )(part, pe_hid, cmp_w2)


def _rows_to_col(row, n):
    eye = lax.broadcasted_iota(jnp.int32, (n, n), 0) == lax.broadcasted_iota(jnp.int32, (n, n), 1)
    return jnp.sum(jnp.where(eye, jnp.broadcast_to(row, (n, n)), 0.0), axis=1, keepdims=True)


def _sample_bias_tiles(rel_table, past, n_new):
    j = np.arange(PAGE)[:, None]
    t = np.arange(n_new)[None, :]
    far = np.full((PAGE, n_new), REL_MAX_DIST)
    first = WINDOW + t - j
    last = PAGE + t - j
    new = t - j
    tiles = []
    for dist, ok in ((far, far > 0), (first, first < WINDOW), (last, last > 0), (new, (new >= 0) & (j < n_new))):
        b = jnp.where(ok[..., None], _bias_lookup(rel_table, dist), NEG_INF)
        b = b.reshape(PAGE, n_new, KV_HEADS, GROUP).transpose(2, 0, 3, 1).reshape(KV_HEADS, PAGE, GROUP * n_new)
        tiles.append(b)
    return jnp.stack(tiles, axis=1)


def _sample_cmp_body(q_ref, kc_ref, bias_ref, gt_ref, o_ref, pen_ref, *, n_sel_blocks, past):
    n = kc_ref.shape[1]
    nbp = pen_ref.shape[0]
    q = _bf(q_ref[...] * ATTN_SCALE)
    s = _dot(_bf(kc_ref[0]), q, NT_DIMS) + bias_ref[...]
    m = jnp.max(s, axis=0, keepdims=True)
    e = jnp.exp(s - m)
    p = e / jnp.maximum(jnp.sum(e, axis=0, keepdims=True), 1e-30)
    o = _dot(_bf(p), _bf(kc_ref[1]), TN_DIMS)
    o_ref[...] = o * jax.nn.sigmoid(gt_ref[...])

    r_i = lax.broadcasted_iota(jnp.int32, (ROWS, ROWS), 0)
    c_i = lax.broadcasted_iota(jnp.int32, (ROWS, ROWS), 1)
    n_tok = ROWS // GROUP
    same_tok = _bf((r_i & (n_tok - 1)) == (c_i & (n_tok - 1)))
    ratio = SEL_BLOCK // CMP_STRIDE
    j_i = lax.broadcasted_iota(jnp.int32, (nbp, n), 0)
    k_i = lax.broadcasted_iota(jnp.int32, (nbp, n), 1)
    w = _bf((k_i >= ratio * j_i - 1) & (k_i <= ratio * j_i + ratio - 1) & (j_i < n_sel_blocks))
    imp = sum(_dot(part, same_tok) for part in _split_bf16(p, 3))
    score = sum(_dot(w, part) for part in _split_bf16(imp, 3))
    blk = lax.broadcasted_iota(jnp.int32, (nbp, ROWS), 0)
    tok = past + (lax.broadcasted_iota(jnp.int32, (nbp, ROWS), 1) & (n_tok - 1))
    cur = lax.shift_right_logical(tok, int(math.log2(SEL_BLOCK)))
    forced = (blk == 0) | (blk == cur) | (blk == cur - 1)
    causal = blk <= cur
    score = jnp.where(forced, FORCE_SCORE, score)
    score = jnp.where(causal, score, -1.0)
    nbl = -(-nbp // LANE) * LANE
    n_idx = lax.broadcasted_iota(jnp.int32, (nbp, nbl), 0)
    m_idx = lax.broadcasted_iota(jnp.int32, (nbp, nbl), 1)
    lane_tok = lax.broadcasted_iota(jnp.int32, (nbp, ROWS), 1) & (n_tok - 1)
    rank = jnp.zeros((nbp, ROWS), jnp.float32)
    for t in range(n_tok):
        s_col = score[:, t:t + 1]
        s_row = jnp.sum(jnp.where(n_idx == m_idx, s_col, 0.0), axis=0, keepdims=True)
        s_row = jnp.where(m_idx[0:1] < n_sel_blocks, s_row, -2.0)
        ahead = (s_row > s_col) | ((s_row == s_col) & (m_idx < n_idx))
        rank_t = jnp.sum(ahead.astype(jnp.float32), axis=1, keepdims=True)
        rank = jnp.where(lane_tok == t, rank_t, rank)
    chosen = (rank < N_SEL) & causal & (blk < n_sel_blocks)
    pen_ref[...] = jnp.where(chosen, 0.0, NEG_INF)


def sample_cmp_select(qs, kc, bias, gate, past, n_sel_blocks):
    batch = qs.shape[0]
    n = kc.shape[3]
    nbp = -(-n_sel_blocks // SUBLANE) * SUBLANE
    return pl.pallas_call(
        functools.partial(_sample_cmp_body, n_sel_blocks=n_sel_blocks, past=past),
        grid=(batch, KV_HEADS),
        in_specs=[pl.BlockSpec((None, None, ROWS, HEAD_DIM), lambda b, g: (b, g, 0, 0)),
                  pl.BlockSpec((None, None, 2, n, HEAD_DIM), lambda b, g: (b, g, 0, 0, 0)),
                  pl.BlockSpec((None, n, ROWS), lambda b, g: (g, 0, 0)),
                  pl.BlockSpec((None, None, ROWS, HEAD_DIM), lambda b, g: (b, g, 0, 0))],
        out_specs=[pl.BlockSpec((None, None, ROWS, HEAD_DIM), lambda b, g: (b, g, 0, 0)),
                   pl.BlockSpec((None, None, nbp, ROWS), lambda b, g: (b, g, 0, 0))],
        out_shape=[jax.ShapeDtypeStruct((batch, KV_HEADS, ROWS, HEAD_DIM), jnp.float32),
                   jax.ShapeDtypeStruct((batch, KV_HEADS, nbp, ROWS), jnp.float32)],
        compiler_params=_cparams("parallel", "parallel"),
        name="sample_cmp_select",
    )(qs, kc, bias, gate)


ATTN_PAGES = 4


def _paged_attn_body(*refs, pen_block, gated, n_pages):
    it = iter(refs)
    pt_ref, tid_ref, q_ref = next(it), next(it), next(it)
    pages = [next(it) for _ in range(ATTN_PAGES)]
    new_ref, bt_ref = next(it), next(it)
    pen_ref = next(it) if pen_block else None
    gt_ref = next(it) if gated else None
    o_ref, m_ref, l_ref, acc_ref = next(it), next(it), next(it), next(it)
    step = pl.program_id(1)

    @pl.when(step == 0)
    def _():
        m_ref[...] = jnp.full(m_ref.shape, NEG_INF, jnp.float32)
        l_ref[...] = jnp.zeros(l_ref.shape, jnp.float32)
        acc_ref[...] = jnp.zeros(acc_ref.shape, jnp.float32)

    def attend(g, k, v, bias, page, n_keys):
        s = _dot(_bf(k), _bf(q_ref[g] * ATTN_SCALE), NT_DIMS) + bias
        if pen_block:
            if pen_block >= PAGE:
                s = s + pen_ref[g, pl.ds(page // (pen_block // PAGE), 1), :]
            else:
                per_page = PAGE // pen_block
                pieces = [s[a * pen_block:min((a + 1) * pen_block, n_keys)]
                          + pen_ref[g, pl.ds(page * per_page + a, 1), :]
                          for a in range(-(-n_keys // pen_block))]
                s = pieces[0] if len(pieces) == 1 else jnp.concatenate(pieces, axis=0)
        m_prev = m_ref[g]
        m_new = jnp.maximum(m_prev, jnp.max(s, axis=0, keepdims=True))
        alpha = jnp.exp(m_prev - m_new)
        p = jnp.exp(s - m_new)
        l_ref[g] = alpha * l_ref[g] + jnp.sum(p, axis=0, keepdims=True)
        acc_ref[g] = _rows_to_col(alpha, ROWS) * acc_ref[g] + _dot(_bf(p), _bf(v), TN_DIMS)
        m_ref[g] = m_new

    half = KV_HEADS * HEAD_DIM
    for kk in range(ATTN_PAGES):
        page = step * ATTN_PAGES + kk
        tile = tid_ref[page]
        for g in range(KV_HEADS):
            attend(g, _page_slab(pages[kk], g), _page_slab(pages[kk], KV_HEADS + g), bt_ref[g, tile], page, PAGE)

    @pl.when(step == pl.num_programs(1) - 1)
    def _():
        n_new = new_ref.shape[0]
        tile = tid_ref[n_pages]
        for g in range(KV_HEADS):
            attend(g, new_ref[:, g * HEAD_DIM:(g + 1) * HEAD_DIM],
                   new_ref[:, half + g * HEAD_DIM:half + (g + 1) * HEAD_DIM], bt_ref[g, tile, 0:n_new, :],
                   n_pages, n_new)
            o = acc_ref[g] / _rows_to_col(jnp.maximum(l_ref[g], 1e-30), ROWS)
            if gated:
                o = o * jax.nn.sigmoid(gt_ref[g])
            o_ref[g] = o


def paged_attention(qs, pool, page_table, tile_ids, new_kv, bias_tiles, *, pen=None, pen_block=0, gate=None):
    batch, n_pages = page_table.shape
    steps = n_pages // ATTN_PAGES
    n_new = new_kv.shape[1]

    def page_spec(k):
        return pl.BlockSpec((None, PAGE_ROWS, HEAD_DIM), lambda b, s, pt, tid: (pt[b, s * ATTN_PAGES + k], 0, 0))

    grp = lambda b, s, pt, tid: (b, 0, 0, 0)
    in_specs = ([pl.BlockSpec((None, KV_HEADS, ROWS, HEAD_DIM), grp)] + [page_spec(k) for k in range(ATTN_PAGES)]
                + [pl.BlockSpec((None, n_new, KV_WIDTH), lambda b, s, pt, tid: (b, 0, 0)),
                   pl.BlockSpec(bias_tiles.shape, lambda b, s, pt, tid: (0, 0, 0, 0))])
    args = [qs] + [pool] * ATTN_PAGES + [new_kv, bias_tiles]
    if pen is not None:
        in_specs.append(pl.BlockSpec((None,) + pen.shape[1:], grp))
        args.append(pen)
    if gate is not None:
        in_specs.append(pl.BlockSpec((None, KV_HEADS, ROWS, HEAD_DIM), grp))
        args.append(gate)
    return pl.pallas_call(
        functools.partial(_paged_attn_body, pen_block=pen_block if pen is not None else 0, gated=gate is not None,
                          n_pages=n_pages),
        grid_spec=pltpu.PrefetchScalarGridSpec(
            num_scalar_prefetch=2, grid=(batch, steps), in_specs=in_specs,
            out_specs=pl.BlockSpec((None, KV_HEADS, ROWS, HEAD_DIM), grp),
            scratch_shapes=[pltpu.VMEM((KV_HEADS, 1, ROWS), jnp.float32), pltpu.VMEM((KV_HEADS, 1, ROWS), jnp.float32),
                            pltpu.VMEM((KV_HEADS, ROWS, HEAD_DIM), jnp.float32)]),
        out_shape=jax.ShapeDtypeStruct((batch, KV_HEADS, ROWS, HEAD_DIM), jnp.float32),
        compiler_params=_cparams("parallel", "arbitrary"),
        name="paged_attention",
    )(page_table, tile_ids, *args)


def _moba_sample_gate_body(pt_ref, q_ref, *refs, n_blocks, past):
    pages, (pen_ref, km_ref) = refs[:ATTN_PAGES], refs[ATTN_PAGES:]
    step = pl.program_id(1)
    half = KV_HEADS * HEAD_DIM
    per_block = MOBA_BLOCK // PAGE

    @pl.when(step == 0)
    def _():
        km_ref[...] = jnp.zeros(km_ref.shape, jnp.float32)

    for kk in range(ATTN_PAGES):
        blk = (step * ATTN_PAGES + kk) // per_block
        slab_sums = jnp.sum(pages[kk][...].reshape(PAGE, KV_SLABS, HEAD_DIM), axis=0)
        for g in range(KV_HEADS):
            km_ref[g, pl.ds(blk, 1), :] += slab_sums[g:g + 1]

    @pl.when(step == pl.num_programs(1) - 1)
    def _():
        nbp = pen_ref.shape[1]
        blk = lax.broadcasted_iota(jnp.int32, (nbp, ROWS), 0)
        n_tok = ROWS // GROUP
        tok = past + (lax.broadcasted_iota(jnp.int32, (nbp, ROWS), 1) & (n_tok - 1))
        own = lax.shift_right_logical(tok, int(math.log2(MOBA_BLOCK)))
        for g in range(KV_HEADS):
            kh, kl = _split_bf16(km_ref[g] / MOBA_BLOCK, 2)
            qh, ql = _split_bf16(q_ref[g], 2)
            gate = _dot(kh, qh, NT_DIMS) + _dot(kh, ql, NT_DIMS) + _dot(kl, qh, NT_DIMS)
            gate = jnp.where(blk < own, gate, NEG_INF)
            chosen = ((_rank_rows(gate, n_blocks) < MOBA_TOPK) & (blk < own)) | (blk == own)
            pen_ref[g] = jnp.where(chosen, 0.0, NEG_INF)


def moba_sample_gate(qs, pool, page_table, past, n_new):
    batch, n_pages = page_table.shape
    steps = n_pages // ATTN_PAGES
    n_blocks = -(-(past + n_new) // MOBA_BLOCK)
    nbp = -(-n_blocks // SUBLANE) * SUBLANE

    def page_spec(k):
        return pl.BlockSpec((None, PAGE_ROWS, HEAD_DIM), lambda b, s, pt: (pt[b, s * ATTN_PAGES + k], 0, 0))

    return pl.pallas_call(
        functools.partial(_moba_sample_gate_body, n_blocks=n_blocks, past=past),
        grid_spec=pltpu.PrefetchScalarGridSpec(
            num_scalar_prefetch=1, grid=(batch, steps),
            in_specs=[pl.BlockSpec((None, KV_HEADS, ROWS, HEAD_DIM), lambda b, s, pt: (b, 0, 0, 0))]
            + [page_spec(k) for k in range(ATTN_PAGES)],
            out_specs=pl.BlockSpec((None, KV_HEADS, nbp, ROWS), lambda b, s, pt: (b, 0, 0, 0)),
            scratch_shapes=[pltpu.VMEM((KV_HEADS, nbp, HEAD_DIM), jnp.float32)]),
        out_shape=jax.ShapeDtypeStruct((batch, KV_HEADS, nbp, ROWS), jnp.float32),
        compiler_params=_cparams("parallel", "arbitrary"),
        name="moba_sample_gate",
    )(page_table, qs, *([pool] * ATTN_PAGES))


def _sample_rows(x, batch, n_tok):
    return x.reshape(batch, n_tok, KV_HEADS, GROUP, HEAD_DIM).transpose(0, 2, 3, 1, 4).reshape(
        batch, KV_HEADS, GROUP * n_tok, HEAD_DIM)


def _sample_unrows(o, batch, n_tok):
    return o.reshape(batch, KV_HEADS, GROUP, n_tok, HEAD_DIM).transpose(0, 3, 1, 2, 4).reshape(batch * n_tok, Q_WIDTH)


def _sample_gate_rows(tail, branch, batch, n_tok):
    gt = tail[:, branch * N_HEADS:(branch + 1) * N_HEADS].reshape(batch, n_tok, KV_HEADS, GROUP)
    gt = gt.transpose(0, 2, 3, 1).reshape(batch, KV_HEADS, GROUP * n_tok, 1)
    return jnp.broadcast_to(gt, (batch, KV_HEADS, GROUP * n_tok, HEAD_DIM))


def _pad_new(kv_new, batch, n_tok):
    return jnp.pad(kv_new.reshape(batch, n_tok, KV_WIDTH), ((0, 0), (0, SUBLANE - n_tok), (0, 0)))


def _nsa_sample_pallas(main, tail, cache_c, cache_s, cache_w, page_table, cmp_w1, cmp_w2, cmp_pe, rel_table):
    batch, n_pages = page_table.shape
    n_tok = main.shape[0] // batch
    past = n_pages * PAGE
    wbuf = cache_w.shape[1]
    assert n_tok == ROWS // GROUP and cache_c.shape[1] == PAGE and wbuf == WINDOW and WINDOW % PAGE == 0
    n_cmp = (past + n_tok - CMP_BLOCK) // CMP_STRIDE + 1
    assert n_cmp + CMP_BLOCK // CMP_STRIDE - 1 == past // CMP_STRIDE
    qs = _sample_rows(main[:, :Q_WIDTH], batch, n_tok)
    kv_new = [main[:, Q_WIDTH + c * KV_WIDTH:Q_WIDTH + (c + 1) * KV_WIDTH] for c in range(3)]
    tiles = _sample_bias_tiles(rel_table, past, n_tok)
    flat = lambda pool: pool.reshape(pool.shape[0], PAGE_ROWS, HEAD_DIM)

    kc = compress_finish(compress_part_paged(flat(cache_c), page_table, cmp_w1), cmp_w1, cmp_w2, cmp_pe)
    n = kc.shape[3]
    dist = past + np.arange(n_tok)[None, :] - (np.arange(n)[:, None] * CMP_STRIDE + CMP_BLOCK - 1)
    ok = (dist >= 0) & (np.arange(n)[:, None] < n_cmp)
    cb = jnp.where(ok[..., None], _bias_lookup(rel_table, dist), NEG_INF)
    cb = cb.reshape(n, n_tok, KV_HEADS, GROUP).transpose(2, 0, 3, 1).reshape(KV_HEADS, n, GROUP * n_tok)
    n_sel_blocks = -(-(past + n_tok) // SEL_BLOCK)
    o_cmp, pen = sample_cmp_select(qs, kc, cb, _sample_gate_rows(tail, 0, batch, n_tok), past, n_sel_blocks)

    far_then_last = jnp.asarray([0] * (n_pages - 1) + [2, 3], jnp.int32)
    o_sel = paged_attention(qs, flat(cache_s), page_table, far_then_last, _pad_new(kv_new[1], batch, n_tok), tiles,
                            pen=pen, pen_block=SEL_BLOCK, gate=_sample_gate_rows(tail, 1, batch, n_tok))
    w_pages = wbuf // PAGE
    win_table = jnp.arange(batch * w_pages, dtype=jnp.int32).reshape(batch, w_pages)
    win_tiles = jnp.asarray([1] + [0] * (w_pages - 2) + [2, 3], jnp.int32)
    o_win = paged_attention(qs, cache_w.reshape(batch * w_pages, PAGE_ROWS, HEAD_DIM), win_table, win_tiles,
                            _pad_new(kv_new[2], batch, n_tok), tiles, gate=_sample_gate_rows(tail, 2, batch, n_tok))
    outs = [_sample_unrows(o, batch, n_tok) for o in (o_cmp, o_sel, o_win)]
    shape = (batch, n_tok, 2, KV_HEADS, HEAD_DIM)
    new_win = jnp.concatenate([cache_w[:, n_tok:], kv_new[2].reshape(shape)], axis=1)
    return outs, kv_new[0].reshape(shape), kv_new[1].reshape(shape), new_win


def _moba_sample_pallas(proj, cache_kv, page_table, rel_table):
    batch, n_pages = page_table.shape
    n_tok = proj.shape[0] // batch
    past = n_pages * PAGE
    assert n_tok == ROWS // GROUP and (past // MOBA_BLOCK) * MOBA_BLOCK == past
    qs = _sample_rows(proj[:, :Q_WIDTH], batch, n_tok)
    kv_new = proj[:, Q_WIDTH:]
    pool = cache_kv.reshape(cache_kv.shape[0], PAGE_ROWS, HEAD_DIM)
    pen = moba_sample_gate(qs, pool, page_table, past, n_tok)
    tile_ids = jnp.asarray([0] * (n_pages - 1) + [2, 3], jnp.int32)
    o = paged_attention(qs, pool, page_table, tile_ids, _pad_new(kv_new, batch, n_tok),
                        _sample_bias_tiles(rel_table, past, n_tok), pen=pen, pen_block=MOBA_BLOCK)
    return _sample_unrows(o, batch, n_tok), kv_new.reshape(batch, n_tok, 2, KV_HEADS, HEAD_DIM)


def _rms_norm(x, g):
    xf = x.astype(jnp.float32)
    y = xf * lax.rsqrt(jnp.mean(xf * xf, axis=-1, keepdims=True) + NORM_EPS)
    return (y * g.astype(jnp.float32)).astype(x.dtype)


def _l2norm(x):
    return x * lax.rsqrt(jnp.sum(x * x, axis=-1, keepdims=True) + NORM_EPS)


def _masked_softmax(logits, mask):
    logits = jnp.where(mask, logits, NEG_INF)
    m = jnp.max(logits, axis=-1, keepdims=True)
    e = jnp.where(mask, jnp.exp(logits - m), 0.0)
    return e / jnp.maximum(jnp.sum(e, axis=-1, keepdims=True), 1e-30)


def _t5_bucket(dist):
    exact = REL_BUCKETS // 2
    d = jnp.maximum(dist, 0)
    ratio = jnp.log(jnp.maximum(d, 1).astype(jnp.float32) / exact) / math.log(REL_MAX_DIST / exact)
    large = jnp.minimum(exact + (ratio * (REL_BUCKETS - exact)).astype(jnp.int32), REL_BUCKETS - 1)
    return jnp.where(d < exact, d, large)


def _rel_bias(rel_table, dist):
    return rel_table.astype(jnp.float32)[_t5_bucket(dist)]


def _gather_pages(pool, page_table):
    rows = pool[page_table]
    return rows.reshape(page_table.shape[0], page_table.shape[1] * pool.shape[1], *pool.shape[2:])


def _causal_conv(x, buf, w):
    T = x.shape[1]
    xp = jnp.concatenate([buf.astype(x.dtype), x], axis=1)
    y = xp[:, 0:T] * w[0]
    for i in range(1, GDN_CONV):
        y = y + xp[:, i:i + T] * w[i]
    return jax.nn.silu(y), xp[:, T:]


def _nsa_split(proj):
    B, T, _ = proj.shape
    q = proj[..., :Q_WIDTH].reshape(B, T, KV_HEADS, GROUP, HEAD_DIM)
    kv_c, kv_s, kv_w = (proj[..., Q_WIDTH + c * KV_WIDTH:Q_WIDTH + (c + 1) * KV_WIDTH]
                        .reshape(B, T, 2, KV_HEADS, HEAD_DIM) for c in range(3))
    gates = jax.nn.sigmoid(proj[..., NSA_MAIN:NSA_MAIN + 3 * N_HEADS]).reshape(B, T, 3, KV_HEADS, GROUP)
    return q, kv_c, kv_s, kv_w, gates


def _nsa_compress(kv, cmp_w1, cmp_w2, cmp_pe):
    B, Tk = kv.shape[:2]
    nc = (Tk - CMP_BLOCK) // CMP_STRIDE + 1
    n_part = CMP_BLOCK // CMP_STRIDE
    n_chunk = nc + n_part - 1
    chunks = kv[:, :n_chunk * CMP_STRIDE].astype(jnp.float32).reshape(B, n_chunk, CMP_STRIDE, 2, KV_HEADS, HEAD_DIM)
    w1 = cmp_w1.astype(jnp.float32)
    part = jnp.einsum('bnlcgd,crldh->bncrgh', chunks, w1.reshape(2, n_part, CMP_STRIDE, HEAD_DIM, CMP_HIDDEN))
    hid = jnp.einsum('cld,cldh->ch', cmp_pe.astype(jnp.float32), w1)[None, None, :, None, :]
    for r in range(n_part):
        hid = hid + part[:, r:r + nc, :, r]
    return jnp.einsum('bncgh,chd->bncgd', jax.nn.gelu(hid), cmp_w2.astype(jnp.float32))


def _nsa_cmp_attn(q, q_pos, kc, rel_table):
    nc = kc.shape[1]
    end_pos = jnp.arange(nc) * CMP_STRIDE + (CMP_BLOCK - 1)
    dist = q_pos[:, None] - end_pos[None, :]
    bias = _rel_bias(rel_table, dist).reshape(q_pos.shape[0], nc, KV_HEADS, GROUP).transpose(0, 2, 3, 1)
    logits = jnp.einsum('bqgrd,bcgd->bqgrc', q, kc[:, :, 0]) * ATTN_SCALE + bias
    p = _masked_softmax(logits, (dist >= 0)[:, None, None, :])
    return jnp.einsum('bqgrc,bcgd->bqgrd', p, kc[:, :, 1]), p


def _nsa_select(p_cmp, q_pos, tk):
    ns = -(-tk // SEL_BLOCK)
    ratio = SEL_BLOCK // CMP_STRIDE
    imp = p_cmp.sum(axis=3)
    nc = imp.shape[-1]
    imp = jnp.pad(imp, ((0, 0), (0, 0), (0, 0), (1, ratio * ns + ratio - 1 - nc)))
    score = imp[..., :ratio * ns].reshape(*imp.shape[:3], ns, ratio).sum(-1) + imp[..., ratio::ratio]
    cur = q_pos // SEL_BLOCK
    blk = jnp.arange(ns)
    forced = (blk[None, :] == 0) | (blk[None, :] == cur[:, None]) | (blk[None, :] == cur[:, None] - 1)
    causal = blk[None, :] <= cur[:, None]
    score = jnp.where(forced[None, :, None, :], FORCE_SCORE, score)
    score = jnp.where(causal[None, :, None, :], score, -1.0)
    _, idx = lax.top_k(score, min(N_SEL, ns))
    valid = idx <= cur[None, :, None, None]
    return idx, valid


def _block_mask(idx, valid, n_blocks, block, tk):
    hit = (idx[..., None] == jnp.arange(n_blocks)) & valid[..., None]
    return jnp.repeat(jnp.any(hit, axis=-2), block, axis=-1)[..., :tk]


def _dense_attn(q, q_pos, kv, key_ok, rel_table):
    tk = kv.shape[1]
    dist = q_pos[:, None] - jnp.arange(tk)[None, :]
    bias = _rel_bias(rel_table, dist).reshape(q_pos.shape[0], tk, KV_HEADS, GROUP).transpose(0, 2, 3, 1)
    logits = jnp.einsum('bqgrd,bkgd->bqgrk', q, kv[:, :, 0]) * ATTN_SCALE + bias
    p = _masked_softmax(logits, key_ok & (dist >= 0)[None, :, None, None, :])
    return jnp.einsum('bqgrk,bkgd->bqgrd', p, kv[:, :, 1])


def _band_attn(q, q_pos, kv_band, k_pos, rel_table):
    dist = q_pos[:, :, None] - k_pos[:, None, :]
    mask = (dist >= 0) & (dist < WINDOW) & (k_pos[:, None, :] >= 0)
    n, qb, kb = dist.shape
    bias = _rel_bias(rel_table, dist).reshape(n, qb, kb, KV_HEADS, GROUP).transpose(0, 1, 3, 4, 2)
    kvf = kv_band.astype(jnp.float32)
    logits = jnp.einsum('bnqgrd,bnkgd->bnqgrk', q, kvf[:, :, :, 0]) * ATTN_SCALE + bias
    p = _masked_softmax(logits, mask[:, :, None, None, :])
    return jnp.einsum('bnqgrk,bnkgd->bnqgrd', p, kvf[:, :, :, 1])


def _nsa_sample(proj, cache_c, cache_s, cache_w, page_table, cmp_w1, cmp_w2, cmp_pe, rel_table):
    B, T, _ = proj.shape
    past = page_table.shape[1] * cache_c.shape[1]
    q, kv_c, kv_s, kv_w, gates = _nsa_split(proj)
    pos = past + jnp.arange(T)
    kv_c_full = jnp.concatenate([_gather_pages(cache_c, page_table), kv_c], axis=1)
    kv_s_full = jnp.concatenate([_gather_pages(cache_s, page_table), kv_s], axis=1)
    wbuf = cache_w.shape[1]
    band = jnp.concatenate([cache_w, kv_w], axis=1)
    band_k_pos = past - wbuf + jnp.arange(wbuf + T)
    kc = _nsa_compress(kv_c_full, cmp_w1, cmp_w2, cmp_pe)
    o_cmp, p_cmp = _nsa_cmp_attn(q, pos, kc, rel_table)
    tk = kv_s_full.shape[1]
    idx, valid = _nsa_select(p_cmp, pos, tk)
    key_ok = _block_mask(idx, valid, -(-tk // SEL_BLOCK), SEL_BLOCK, tk)[:, :, :, None, :]
    o_sel = _dense_attn(q, pos, kv_s_full, key_ok, rel_table)
    o_win = _band_attn(q[:, None], pos[None, :], band[:, None], band_k_pos[None, :], rel_table)[:, 0]
    o = gates[:, :, 0, ..., None] * o_cmp + gates[:, :, 1, ..., None] * o_sel + gates[:, :, 2, ..., None] * o_win
    return o.reshape(B, T, Q_WIDTH), kv_c, kv_s, band[:, -wbuf:]


def _moba_sample(proj, cache_kv, page_table, rel_table):
    B, T, _ = proj.shape
    past = page_table.shape[1] * cache_kv.shape[1]
    q = proj[..., :Q_WIDTH].reshape(B, T, KV_HEADS, GROUP, HEAD_DIM)
    kv_new = proj[..., Q_WIDTH:].reshape(B, T, 2, KV_HEADS, HEAD_DIM)
    kv = jnp.concatenate([_gather_pages(cache_kv, page_table), kv_new], axis=1)
    q_pos = past + jnp.arange(T)
    tk = kv.shape[1]
    nb = -(-tk // MOBA_BLOCK)
    kpad = jnp.pad(kv[:, :, 0], ((0, 0), (0, nb * MOBA_BLOCK - tk), (0, 0), (0, 0)))
    kmean = jnp.mean(kpad.reshape(B, nb, MOBA_BLOCK, KV_HEADS, HEAD_DIM), axis=2)
    own = q_pos // MOBA_BLOCK
    gate = jnp.einsum('bqgrd,bngd->bqgrn', q, kmean)
    is_past = jnp.arange(nb)[None, :] < own[:, None]
    gate = jnp.where(is_past[:, None, None, :], gate, NEG_INF)
    _, idx = lax.top_k(gate, min(MOBA_TOPK, nb))
    valid = idx < own[:, None, None, None]
    own_ok = (jnp.arange(tk)[None, :] // MOBA_BLOCK) == own[:, None]
    key_ok = _block_mask(idx, valid, nb, MOBA_BLOCK, tk) | own_ok[None, :, None, None, :]
    o = _dense_attn(q, q_pos, kv, key_ok, rel_table)
    return o.reshape(B, T, Q_WIDTH), kv_new


def _gla_chunked(q, k, v, logf, s0):
    B, T, H, dk = q.shape
    dv = v.shape[-1]
    C = math.gcd(T, CHUNK)
    N = T // C
    q, k, v, logf = (a.reshape(B, N, C, *a.shape[2:]) for a in (q, k, v, logf))
    b = jnp.cumsum(logf, axis=2)
    b_ref = b[:, :, C // 2:C // 2 + 1]
    a = jnp.einsum('bnihd,bnjhd->bnhij', q * jnp.exp(b - b_ref), k * jnp.exp(b_ref - b))
    causal = jnp.arange(C)[:, None] >= jnp.arange(C)[None, :]
    a = jnp.where(causal, a, 0.0)
    o_intra = jnp.einsum('bnhij,bnjhv->bnihv', a, v)
    q_in = q * jnp.exp(b)
    k_out = k * jnp.exp(b[:, :, -1:] - b)
    d_last = jnp.exp(b[:, :, -1])

    def step(s, xs):
        q_c, k_c, v_c, d_c = xs
        o = jnp.einsum('bihd,bhdv->bihv', q_c, s)
        s = d_c[..., None] * s + jnp.einsum('bjhd,bjhv->bhdv', k_c, v_c)
        return s, o

    s, o_inter = lax.scan(step, s0, tuple(jnp.moveaxis(t, 1, 0) for t in (q_in, k_out, v, d_last)))
    o = o_intra + jnp.moveaxis(o_inter, 0, 1)
    return o.reshape(B, T, H, dv), s


def _hgrn2_core(proj, s0, lb_logits, layer, norm_g):
    B, T, _ = proj.shape
    dk = HG_HEADS * HG_DK
    dv = HG_HEADS * HG_DV
    q, f, i, g = jnp.split(proj, [dk, 2 * dk, 2 * dk + dv], axis=-1)
    p = jax.nn.softmax(lb_logits.astype(jnp.float32), axis=0)
    lb = (jnp.cumsum(p, axis=0) - p[0])[layer]
    fg = lb + (1.0 - lb) * jax.nn.sigmoid(f)
    shp = (B, T, HG_HEADS, HG_DK)
    o, s = _gla_chunked((jax.nn.silu(q) * HG_DK ** -0.5).reshape(shp), (1.0 - fg).reshape(shp),
                        i.reshape(B, T, HG_HEADS, HG_DV), jnp.log(fg).reshape(shp), s0.astype(jnp.float32))
    o = _rms_norm(o, norm_g) * jax.nn.silu(g.reshape(B, T, HG_HEADS, HG_DV))
    return o.reshape(B, T, dv), s


def _gdn_chunked(q, k, v, log_a, beta, s0):
    B, T, H, dk = q.shape
    dv = v.shape[-1]
    C = math.gcd(T, CHUNK)
    N = T // C

    def heads_first(a):
        return jnp.moveaxis(a.reshape(B, N, C, *a.shape[2:]), 3, 2)

    qh, kh, vh, bt = heads_first(q), heads_first(k), heads_first(v), heads_first(beta)
    g = jnp.cumsum(heads_first(log_a), axis=-1)
    ar = jnp.arange(C)
    strict = ar[:, None] > ar[None, :]
    incl = ar[:, None] >= ar[None, :]
    gdiff = g[..., :, None] - g[..., None, :]
    d_strict = jnp.where(strict, jnp.exp(jnp.where(strict, gdiff, 0.0)), 0.0)
    d_incl = jnp.where(incl, jnp.exp(jnp.where(incl, gdiff, 0.0)), 0.0)
    kk = jnp.einsum('bnhid,bnhjd->bnhij', kh, kh)
    m = jnp.eye(C, dtype=jnp.float32) + bt[..., :, None] * kk * d_strict
    rhs = jnp.concatenate([bt[..., None] * vh, (bt * jnp.exp(g))[..., None] * kh], axis=-1)
    sol = lax.linalg.triangular_solve(m, rhs, left_side=True, lower=True, unit_diagonal=True)
    u0, w = sol[..., :dv], sol[..., dv:]
    aq = jnp.einsum('bnhid,bnhjd->bnhij', qh, kh) * d_incl
    q_in = qh * jnp.exp(g)[..., None]
    k_out = kh * jnp.exp(g[..., -1:] - g)[..., None]
    d_last = jnp.exp(g[..., -1])

    def step(s, xs):
        u0_c, w_c, aq_c, q_c, k_c, d_c = xs
        u = u0_c - jnp.einsum('bhcd,bhdv->bhcv', w_c, s)
        o = jnp.einsum('bhcd,bhdv->bhcv', q_c, s) + jnp.einsum('bhij,bhjv->bhiv', aq_c, u)
        s = d_c[..., None, None] * s + jnp.einsum('bhcd,bhcv->bhdv', k_c, u)
        return s, o

    xs = tuple(jnp.moveaxis(a, 1, 0) for a in (u0, w, aq, q_in, k_out, d_last))
    s, o = lax.scan(step, s0, xs)
    return jnp.transpose(o, (1, 0, 3, 2, 4)).reshape(B, T, H, dv), s


def _gdn_core(proj, conv_buf, s0, conv_w, a_log, dt_bias, norm_g):
    B, T, _ = proj.shape
    vw = GDN_V_HEADS * GDN_DV
    qkv = proj[..., :GDN_CONV_DIM]
    z = proj[..., GDN_CONV_DIM:GDN_MAIN]
    b_logit = proj[..., GDN_MAIN:GDN_MAIN + GDN_V_HEADS]
    a_in = proj[..., GDN_MAIN + GDN_V_HEADS:GDN_MAIN + 2 * GDN_V_HEADS]
    conv_out, new_buf = _causal_conv(qkv, conv_buf, conv_w)
    qkw = GDN_QK_HEADS * GDN_DK
    q = jnp.repeat(_l2norm(conv_out[..., :qkw].reshape(B, T, GDN_QK_HEADS, GDN_DK)) * GDN_DK ** -0.5, GDN_REP, axis=2)
    k = jnp.repeat(_l2norm(conv_out[..., qkw:2 * qkw].reshape(B, T, GDN_QK_HEADS, GDN_DK)), GDN_REP, axis=2)
    v = conv_out[..., 2 * qkw:].reshape(B, T, GDN_V_HEADS, GDN_DV)
    beta = jax.nn.sigmoid(b_logit)
    log_a = -jnp.exp(a_log.astype(jnp.float32)) * jax.nn.softplus(a_in + dt_bias.astype(jnp.float32))
    o, s = _gdn_chunked(q, k, v, log_a, beta, s0.astype(jnp.float32))
    o = _rms_norm(o, norm_g) * jax.nn.silu(z.reshape(B, T, GDN_V_HEADS, GDN_DV))
    return o.reshape(B, T, vw), new_buf, s


def _pad_cols(w, mult=LANE):
    n = w.shape[1]
    return jnp.pad(w, ((0, 0), (0, (-n) % mult)))


def _nsa_prompt(main, tail, batch, seq, cmp_w1, cmp_w2, cmp_pe, rel_table):
    kv = main[:, Q_WIDTH:].reshape(batch, seq, 3, 2, KV_HEADS, HEAD_DIM)
    kv_c, kv_s, kv_w = kv[:, :, 0], kv[:, :, 1], kv[:, :, 2]
    kc = compress_finish(compress_part_rows(main, Q_WIDTH // KV_WIDTH, cmp_w1, batch, seq), cmp_w1, cmp_w2, cmp_pe)
    o_cmp, pen = cmp_select(main, tail, kc, _cmp_bias_table(rel_table, kc.shape[3]), batch, seq)
    col = Q_WIDTH // HEAD_DIM
    o_sel = flash_attention(main, main, col + 2 * KV_HEADS, col + 3 * KV_HEADS, _flash_bias_tiles(rel_table, 0),
                            batch, seq, pen=pen, pen_block=SEL_BLOCK, gate_arr=tail, gate_col0=N_HEADS)
    o_win = flash_attention(main, main, col + 4 * KV_HEADS, col + 5 * KV_HEADS,
                            _flash_bias_tiles(rel_table, WINDOW), batch, seq, k_back=WINDOW // ATTN_TILE,
                            gate_arr=tail, gate_col0=2 * N_HEADS)
    return [o_cmp, o_sel, o_win], kv_c, kv_s, kv_w[:, -min(WINDOW, seq):]


def _moba_prompt(proj, batch, seq, rel_table):
    col = Q_WIDTH // HEAD_DIM
    pen = moba_gate(proj, col, batch, seq)
    o = flash_attention(proj, proj, col, col + KV_HEADS, _flash_bias_tiles(rel_table, 0), batch, seq,
                        pen=pen, pen_block=MOBA_BLOCK)
    return o, proj[:, Q_WIDTH:].reshape(batch, seq, 2, KV_HEADS, HEAD_DIM)


def kernel(x_prompt, x_sample, cache_nsa_cmp_kv, cache_nsa_sel_kv, cache_nsa_win_kv, cache_moba_kv,
           state_hgrn2, state_gdn_conv, state_gdn_ssm, page_table, rel_table, ln_mix, ln_ffn, ln_final,
           ffn_w_up, ffn_w_down, nsa_w_in, nsa_cmp_w1, nsa_cmp_w2, nsa_cmp_pe, nsa_w_out, moba_w_in, moba_w_out,
           hg_w_in, hg_lb_logits, hg_norm, hg_w_out, gdn_w_in, gdn_conv_w, gdn_a_log, gdn_dt_bias, gdn_norm,
           gdn_w_out):
    bf = jnp.bfloat16
    bp, tp = x_prompt.shape[:2]
    bs, ts = x_sample.shape[:2]
    assert tp % ATTN_TILE == 0 and WINDOW % ATTN_TILE == 0 and ATTN_TILE == MOBA_BLOCK and tp % SCAN_TILE == 0
    xp = x_prompt.reshape(bp * tp, D_MODEL)
    xs = x_sample.reshape(bs * ts, D_MODEL)

    for layer in range(DEPTH):
        kind = layer % N_MIXERS
        g_mix = ln_mix[layer]
        if kind == 0:
            w_main, w_tail = nsa_w_in[:, :NSA_MAIN].astype(bf), _pad_cols(nsa_w_in[:, NSA_MAIN:]).astype(bf)
            main_p, tail_p = norm_matmul(xp, g_mix, w_main), norm_matmul(xp, g_mix, w_tail)
            main_s, tail_s = norm_matmul(xs, g_mix, w_main), norm_matmul(xs, g_mix, w_tail)
            op, nsa_cmp_p, nsa_sel_p, nsa_win_p = _nsa_prompt(main_p, tail_p, bp, tp, nsa_cmp_w1, nsa_cmp_w2,
                                                              nsa_cmp_pe, rel_table)
            os_, nsa_cmp_s, nsa_sel_s, nsa_win_s = _nsa_sample_pallas(main_s, tail_s, cache_nsa_cmp_kv,
                                                                     cache_nsa_sel_kv, cache_nsa_win_kv, page_table,
                                                                     nsa_cmp_w1, nsa_cmp_w2, nsa_cmp_pe, rel_table)
            w_out = nsa_w_out.astype(bf)
        elif kind == 1:
            w_in = moba_w_in.astype(bf)
            pp = norm_matmul(xp, g_mix, w_in)
            op, moba_p = _moba_prompt(pp, bp, tp, rel_table)
            os_, moba_s = _moba_sample_pallas(norm_matmul(xs, g_mix, w_in), cache_moba_kv, page_table, rel_table)
            op, os_ = [op], [os_]
            w_out = moba_w_out.astype(bf)
        elif kind == 2:
            w_in = hg_w_in.astype(bf)
            pp = norm_matmul(xp, g_mix, w_in)
            ps = norm_matmul(xs, g_mix, w_in).reshape(bs, ts, -1)
            s0 = jnp.zeros((bp, HG_HEADS, HG_DK, HG_DV), jnp.float32)
            op, hg_p = hgrn2_scan(pp, hg_lb_logits, hg_norm, s0, layer, bp, tp)
            os_, hg_s = _hgrn2_core(ps, state_hgrn2, hg_lb_logits, layer, hg_norm)
            op, os_ = [op], [os_.reshape(bs * ts, -1)]
            w_out = hg_w_out.astype(bf)
        else:
            w_main, w_tail = gdn_w_in[:, :GDN_MAIN].astype(bf), _pad_cols(gdn_w_in[:, GDN_MAIN:]).astype(bf)
            main_p, tail_p = norm_matmul(xp, g_mix, w_main), norm_matmul(xp, g_mix, w_tail)
            ps = jnp.concatenate([norm_matmul(xs, g_mix, w_main), norm_matmul(xs, g_mix, w_tail)], axis=-1)
            buf0 = jnp.zeros((bp, GDN_CONV - 1, GDN_CONV_DIM), jnp.float32)
            s0 = jnp.zeros((bp, GDN_V_HEADS, GDN_DK, GDN_DV), jnp.float32)
            op, ssm_p = gdn_scan(main_p, tail_p, buf0, gdn_conv_w, gdn_a_log, gdn_dt_bias, gdn_norm, s0, bp, tp)
            conv_p = main_p.reshape(bp, tp, -1)[:, tp - (GDN_CONV - 1):, :GDN_CONV_DIM]
            os_, conv_s, ssm_s = _gdn_core(ps.reshape(bs, ts, -1), state_gdn_conv, state_gdn_ssm, gdn_conv_w,
                                           gdn_a_log, gdn_dt_bias, gdn_norm)
            op, os_ = [op], [os_.reshape(bs * ts, -1)]
            w_out = gdn_w_out.astype(bf)
        xp = matmul_res(op, w_out, xp)
        xs = matmul_res(os_, w_out, xs)
        w_up, w_down = ffn_w_up[layer].astype(bf), ffn_w_down[layer].astype(bf)
        xp = ffn(xp, ln_ffn[layer], w_up, w_down)
        xs = ffn(xs, ln_ffn[layer], w_up, w_down)
    y_prompt = final_norm(xp, ln_final).reshape(bp, tp, D_MODEL)
    y_sample = final_norm(xs, ln_final).reshape(bs, ts, D_MODEL)
    return (y_prompt, y_sample, nsa_cmp_p, nsa_cmp_s, nsa_sel_p, nsa_sel_s, nsa_win_p, nsa_win_s,
            moba_p, moba_s, hg_p, hg_s, conv_p, conv_s, ssm_p, ssm_s)
```

```python
import functools
import math

import jax
import jax.numpy as jnp
import numpy as np
from jax import lax
from jax.experimental import pallas as pl
from jax.experimental.pallas import tpu as pltpu

D_MODEL = 2048
DEPTH = 4
N_MIXERS = 4
HEAD_DIM = 128
N_HEADS = D_MODEL // HEAD_DIM
KV_HEADS = 4
GROUP = N_HEADS // KV_HEADS
ATTN_SCALE = HEAD_DIM ** -0.5
REL_BUCKETS = 32
REL_MAX_DIST = 128
CMP_BLOCK = 32
CMP_STRIDE = 16
CMP_HIDDEN = HEAD_DIM
SEL_BLOCK = 64
N_SEL = 16
WINDOW = 512
FORCE_SCORE = 1.0e4
MOBA_BLOCK = 256
MOBA_TOPK = 3
HG_DK = 128
HG_HEADS = D_MODEL // HG_DK
HG_DV = D_MODEL // HG_HEADS
GDN_DK = 128
GDN_DV = 128
GDN_QK_HEADS = D_MODEL // GDN_DK
GDN_V_HEADS = 2 * GDN_QK_HEADS
GDN_REP = GDN_V_HEADS // GDN_QK_HEADS
GDN_CONV = 4
GDN_CONV_DIM = 2 * GDN_QK_HEADS * GDN_DK + GDN_V_HEADS * GDN_DV
CHUNK = 64
NEG_INF = -1.0e30
NORM_EPS = 1e-6

Q_WIDTH = N_HEADS * HEAD_DIM
KV_WIDTH = 2 * KV_HEADS * HEAD_DIM
NSA_MAIN = Q_WIDTH + 3 * KV_WIDTH
GDN_MAIN = GDN_CONV_DIM + GDN_V_HEADS * GDN_DV

V7X_VMEM_LIMIT_BYTES = 56 * 1024 * 1024
LANE = 128
SUBLANE = 8
ATTN_TILE = 256
CMP_TILE = 128
SCAN_TILE = 512
GDN_HPS = 2
LOG2E = math.log2(math.e)
NT_DIMS = (((1,), (1,)), ((), ()))
TN_DIMS = (((0,), (0,)), ((), ()))


def _cparams(*sem):
    return pltpu.CompilerParams(dimension_semantics=sem, vmem_limit_bytes=V7X_VMEM_LIMIT_BYTES)


def _row_tile(m, target):
    t = min(m, target)
    while m % t:
        t //= 2
    return t


def _col_tile(n, target):
    t = min(n, target)
    while n % t or t % LANE:
        t -= LANE
    return t


def _split_bf16(x, parts):
    out = []
    for _ in range(parts - 1):
        hi = x.astype(jnp.bfloat16)
        out.append(hi)
        x = x - hi.astype(jnp.float32)
    out.append(x.astype(jnp.bfloat16))
    return out


def _bf(x):
    return x.astype(jnp.bfloat16)


def _dot(a, b, dims=None):
    if dims is None:
        return jnp.dot(a, b, preferred_element_type=jnp.float32)
    return lax.dot_general(a, b, dims, preferred_element_type=jnp.float32)


def _norm_matmul_body(x_ref, g_ref, w_ref, o_ref, *rest, kv_from):
    kv_ref, h_ref = rest if kv_from is not None else (None, rest[0])
    j = pl.program_id(1)

    @pl.when(j == 0)
    def _():
        x = x_ref[...]
        ms = jnp.mean(x * x, axis=-1, keepdims=True)
        h_ref[...] = _bf(x * lax.rsqrt(ms + NORM_EPS) * g_ref[...])

    out = _dot(h_ref[...], w_ref[...])
    o_ref[...] = out
    if kv_from is not None:
        @pl.when(j >= kv_from)
        def _():
            tm = out.shape[0]
            for slab in range(out.shape[1] // HEAD_DIM):
                kv_ref[pl.ds(slab, tm, stride=out.shape[1] // HEAD_DIM), :] = out[:, slab * HEAD_DIM:(slab + 1) * HEAD_DIM]


def norm_matmul(x, g, w, kv_from=None):
    m, k = x.shape
    n = w.shape[1]
    tm = _row_tile(m, 1024)
    tn = _col_tile(n, 1024)
    out_specs = [pl.BlockSpec((tm, tn), lambda i, j: (i, j))]
    out_shape = [jax.ShapeDtypeStruct((m, n), jnp.float32)]
    if kv_from is not None:
        assert tn == KV_WIDTH
        slabs = tn // HEAD_DIM
        out_specs.append(pl.BlockSpec((None, tm * slabs, HEAD_DIM), lambda i, j: (jnp.maximum(j - kv_from, 0), i, 0)))
        out_shape.append(jax.ShapeDtypeStruct((n // tn - kv_from, m * slabs, HEAD_DIM), jnp.float32))
    res = pl.pallas_call(
        functools.partial(_norm_matmul_body, kv_from=kv_from),
        grid=(m // tm, n // tn),
        in_specs=[pl.BlockSpec((tm, k), lambda i, j: (i, 0)),
                  pl.BlockSpec((1, k), lambda i, j: (0, 0)),
                  pl.BlockSpec((k, tn), lambda i, j: (0, j))],
        out_specs=out_specs,
        out_shape=out_shape,
        scratch_shapes=[pltpu.VMEM((tm, k), jnp.bfloat16)],
        compiler_params=_cparams("parallel", "arbitrary"),
        name="norm_matmul",
    )(x, g.reshape(1, k), w)
    return res if kv_from is not None else res[0]


def _matmul_res_body(*refs):
    *a_refs, w_ref, r_ref, o_ref = refs
    a = a_refs[0][...]
    for a_ref in a_refs[1:]:
        a = a + a_ref[...]
    o_ref[...] = r_ref[...] + _dot(_bf(a), w_ref[...])


def matmul_res(a_list, w, res):
    m, k = a_list[0].shape
    n = w.shape[1]
    tm = _row_tile(m, 512)
    tn = _col_tile(n, 1024)
    return pl.pallas_call(
        _matmul_res_body,
        grid=(m // tm, n // tn),
        in_specs=[pl.BlockSpec((tm, k), lambda i, j: (i, 0)) for _ in a_list]
        + [pl.BlockSpec((k, tn), lambda i, j: (0, j)),
           pl.BlockSpec((tm, tn), lambda i, j: (i, j))],
        out_specs=pl.BlockSpec((tm, tn), lambda i, j: (i, j)),
        out_shape=jax.ShapeDtypeStruct((m, n), jnp.float32),
        compiler_params=_cparams("parallel", "arbitrary"),
        name="matmul_res",
    )(*a_list, w, res)


def _ffn_body(x_ref, g_ref, wa_ref, wb_ref, wd_ref, o_ref, h_ref):
    @pl.when(pl.program_id(1) == 0)
    def _():
        x = x_ref[...]
        ms = jnp.mean(x * x, axis=-1, keepdims=True)
        h_ref[...] = _bf(x * lax.rsqrt(ms + NORM_EPS) * g_ref[...])
        o_ref[...] = x

    h = h_ref[...]
    a = _dot(h, wa_ref[...])
    b = _dot(h, wb_ref[...])
    o_ref[...] += _dot(_bf(a * jax.nn.sigmoid(a) * b), wd_ref[...])


def ffn(x, g, w_up, w_down):
    m, k = x.shape
    hdim = w_down.shape[0]
    tm = _row_tile(m, 1024)
    th = _col_tile(hdim, 512)
    nh = hdim // th
    return pl.pallas_call(
        _ffn_body,
        grid=(m // tm, nh),
        in_specs=[pl.BlockSpec((tm, k), lambda i, j: (i, 0)),
                  pl.BlockSpec((1, k), lambda i, j: (0, 0)),
                  pl.BlockSpec((k, th), lambda i, j: (0, j)),
                  pl.BlockSpec((k, th), lambda i, j: (0, j + nh)),
                  pl.BlockSpec((th, k), lambda i, j: (j, 0))],
        out_specs=pl.BlockSpec((tm, k), lambda i, j: (i, 0)),
        out_shape=jax.ShapeDtypeStruct((m, k), jnp.float32),
        scratch_shapes=[pltpu.VMEM((tm, k), jnp.bfloat16)],
        compiler_params=_cparams("parallel", "arbitrary"),
        name="ffn",
    )(x, g.reshape(1, k), w_up, w_up, w_down)


def _norm_body(x_ref, g_ref, o_ref):
    x = x_ref[...]
    ms = jnp.mean(x * x, axis=-1, keepdims=True)
    o_ref[...] = x * lax.rsqrt(ms + NORM_EPS) * g_ref[...]


def final_norm(x, g):
    m, k = x.shape
    tm = _row_tile(m, 512)
    return pl.pallas_call(
        _norm_body,
        grid=(m // tm,),
        in_specs=[pl.BlockSpec((tm, k), lambda i: (i, 0)), pl.BlockSpec((1, k), lambda i: (0, 0))],
        out_specs=pl.BlockSpec((tm, k), lambda i: (i, 0)),
        out_shape=jax.ShapeDtypeStruct((m, k), jnp.float32),
        compiler_params=_cparams("parallel"),
        name="final_norm",
    )(x, g.reshape(1, k))


def _bucket_np(dist):
    exact = REL_BUCKETS // 2
    d = np.maximum(dist, 0)
    ratio = np.log(np.maximum(d, 1).astype(np.float32) / exact) / math.log(REL_MAX_DIST / exact)
    large = np.minimum(exact + (ratio * (REL_BUCKETS - exact)).astype(np.int32), REL_BUCKETS - 1)
    return np.where(d < exact, d, large)


def _bias_lookup(rel_table, dist):
    bucket = _bucket_np(dist).astype(np.int32)
    ids = [int(b) for b in np.unique(bucket)]
    bk = jnp.asarray(bucket)[..., None]
    out = jnp.broadcast_to(rel_table[ids[0]], bucket.shape + (rel_table.shape[1],))
    for b in ids[1:]:
        out = jnp.where(bk == b, rel_table[b], out)
    return out


def _heads_to_lanes(t):
    keys, queries, _ = t.shape
    return t.reshape(keys, queries, KV_HEADS, GROUP).transpose(2, 0, 3, 1).reshape(KV_HEADS, keys, GROUP * queries)


def _flash_bias_tiles(rel_table, window):
    j = np.arange(ATTN_TILE)[:, None]
    i = np.arange(ATTN_TILE)[None, :]
    n_cls = window // ATTN_TILE + 1 if window else -(-REL_MAX_DIST // ATTN_TILE) + 2
    tiles = []
    for d in range(n_cls):
        dist = d * ATTN_TILE + i - j
        ok = dist >= 0
        if window:
            ok = ok & (dist < window)
        tiles.append(_heads_to_lanes(jnp.where(ok[..., None], _bias_lookup(rel_table, dist) * LOG2E, NEG_INF)))
    return jnp.stack(tiles, axis=1)


def _cmp_bias_table(rel_table, ncp):
    x = np.arange(ncp)[:, None]
    i = np.arange(CMP_TILE)[None, :]
    dist = i - CMP_STRIDE * (x - 16) - (CMP_BLOCK - 1)
    far = rel_table[REL_BUCKETS - 1]
    b = _heads_to_lanes(jnp.where((dist >= 0)[..., None], _bias_lookup(rel_table, dist), far))
    return jnp.concatenate([b, b], axis=1)


def _stack_heads(q):
    return jnp.concatenate([q[:, r * HEAD_DIM:(r + 1) * HEAD_DIM] for r in range(GROUP)], axis=0)


def _gate_columns(gt_ref, col0):
    gt = jax.nn.sigmoid(gt_ref[...])
    lane = lax.broadcasted_iota(jnp.int32, gt.shape, 1)
    return [jnp.sum(jnp.where(lane == col0 + r, gt, 0.0), axis=1, keepdims=True) for r in range(GROUP)]


def _heads_from_lanes(o_t, rows, cols=None):
    parts = []
    for r in range(GROUP):
        part = o_t[:, r * rows:(r + 1) * rows].T
        if cols is not None:
            part = part * cols[r]
        parts.append(part)
    return jnp.concatenate(parts, axis=1)


def _rank_rows(score, n_rows):
    row = lax.broadcasted_iota(jnp.int32, score.shape, 0)
    rank = jnp.zeros(score.shape, jnp.int32)
    for mm in range(n_rows):
        sm = score[mm:mm + 1, :]
        ahead = (sm > score) | ((sm == score) & (row > mm))
        rank = rank + ahead.astype(jnp.int32)
    return rank


def _cmp_select_body(q_ref, kc_ref, dt_ref, gt_ref, o_ref, pen_ref, *, n_sel_blocks):
    g = pl.program_id(1)
    qi = pl.program_id(2)
    tq = CMP_TILE
    cols = GROUP * tq
    ncp = kc_ref.shape[1]
    q4 = _bf(_stack_heads(q_ref[...]) * ATTN_SCALE)
    s = _dot(_bf(kc_ref[0]), q4, NT_DIMS)
    shift = (qi * (tq // CMP_STRIDE) + ncp - 16) % ncp
    bias = dt_ref[pl.ds(pl.multiple_of(ncp - shift, SUBLANE), ncp), :]
    t_col = qi * tq + (lax.broadcasted_iota(jnp.int32, (ncp, cols), 1) & (tq - 1))
    end_pos = lax.broadcasted_iota(jnp.int32, (ncp, cols), 0) * CMP_STRIDE + (CMP_BLOCK - 1)
    mask = t_col >= end_pos
    s = jnp.where(mask, s + bias, NEG_INF)
    m = jnp.max(s, axis=0, keepdims=True)
    e = jnp.where(mask, jnp.exp(s - m), 0.0)
    p = e / jnp.maximum(jnp.sum(e, axis=0, keepdims=True), 1e-30)
    o_t = _dot(_bf(kc_ref[1].T), _bf(p))
    o_ref[...] = _heads_from_lanes(o_t, tq, _gate_columns(gt_ref, g * GROUP))

    imp = p[:, 0:tq]
    for r in range(1, GROUP):
        imp = imp + p[:, r * tq:(r + 1) * tq]
    ratio = SEL_BLOCK // CMP_STRIDE
    j_i = lax.broadcasted_iota(jnp.int32, (n_sel_blocks, ncp), 0)
    c_i = lax.broadcasted_iota(jnp.int32, (n_sel_blocks, ncp), 1)
    w = _bf((c_i >= ratio * j_i - 1) & (c_i <= ratio * j_i + ratio - 1))
    score = sum(_dot(w, part) for part in _split_bf16(imp, 3))
    blk = lax.broadcasted_iota(jnp.int32, (n_sel_blocks, tq), 0)
    tok = qi * tq + lax.broadcasted_iota(jnp.int32, (n_sel_blocks, tq), 1)
    cur = lax.shift_right_logical(tok, int(math.log2(SEL_BLOCK)))
    forced = (blk == 0) | (blk == cur) | (blk == cur - 1)
    causal = blk <= cur
    score = jnp.where(forced, FORCE_SCORE, score)
    score = jnp.where(causal, score, -1.0)
    chosen = (_rank_rows(score, n_sel_blocks) < N_SEL) & causal
    pen_ref[...] = jnp.where(chosen, 0.0, NEG_INF)


def cmp_select(proj, tail, kc, dt, batch, seq):
    tq = CMP_TILE
    nq = seq // tq
    ncp = kc.shape[3]
    nsb = seq // SEL_BLOCK
    return pl.pallas_call(
        functools.partial(_cmp_select_body, n_sel_blocks=nsb),
        grid=(batch, KV_HEADS, nq),
        in_specs=[pl.BlockSpec((tq, GROUP * HEAD_DIM), lambda b, g, i: (b * nq + i, g)),
                  pl.BlockSpec((None, None, 2, ncp, HEAD_DIM), lambda b, g, i: (b, g, 0, 0, 0)),
                  pl.BlockSpec((None, 2 * ncp, GROUP * tq), lambda b, g, i: (g, 0, 0)),
                  pl.BlockSpec((tq, LANE), lambda b, g, i: (b * nq + i, 0))],
        out_specs=[pl.BlockSpec((tq, GROUP * HEAD_DIM), lambda b, g, i: (b * nq + i, g)),
                   pl.BlockSpec((None, None, nsb, tq), lambda b, g, i: (b, g, 0, i))],
        out_shape=[jax.ShapeDtypeStruct((batch * seq, Q_WIDTH), jnp.float32),
                   jax.ShapeDtypeStruct((batch, KV_HEADS, nsb, seq), jnp.float32)],
        compiler_params=_cparams("parallel", "parallel", "arbitrary"),
        name="cmp_select",
    )(proj, kc, dt, tail)


def _flash_body(*refs, pen_block, pen_per_head, k_back, gate_col0, seq):
    it = iter(refs)
    q_ref, k_ref, v_ref, bt_ref = next(it), next(it), next(it), next(it)
    pen_ref = next(it) if pen_block else None
    gt_ref = next(it) if gate_col0 is not None else None
    o_ref, m_ref, acc_ref, qa_ref, kb_ref, vt_ref, sa_ref, sb_ref = (next(it) for _ in range(8))
    g = pl.program_id(1)
    qi = pl.program_id(2)
    tq = tk = ATTN_TILE
    n_cls = bt_ref.shape[0]

    @pl.when(qi == 0)
    def _():
        vt_ref[HEAD_DIM:, :] = jnp.ones((SUBLANE, seq), jnp.bfloat16)
        for c in range(seq // tk):
            kb_ref[c * tk:(c + 1) * tk, :] = _bf(k_ref[c * tk:(c + 1) * tk, :])
            vt_ref[0:HEAD_DIM, c * tk:(c + 1) * tk] = _bf(v_ref[c * tk:(c + 1) * tk, :].T)

    qa_ref[...] = _bf(_stack_heads(q_ref[...]) * (ATTN_SCALE * LOG2E))
    m_ref[...] = jnp.full(m_ref.shape, NEG_INF, jnp.float32)
    acc_ref[...] = jnp.zeros(acc_ref.shape, jnp.float32)

    def raw_logits(kj):
        k0 = pl.multiple_of(jnp.minimum(kj, qi) * tk, tk)
        return _dot(kb_ref[pl.ds(k0, tk), :], qa_ref[...], NT_DIMS)

    def attend(s_ref, kj):
        kc = jnp.minimum(kj, qi)
        k0 = pl.multiple_of(kc * tk, tk)
        skip = jnp.where(kj <= qi, 0.0, NEG_INF)
        bias = bt_ref[jnp.minimum(qi - kc, n_cls - 1)]
        if pen_block:
            per_tile = tk // pen_block
            pieces = []
            for a in range(per_tile):
                pen = pen_ref[pl.ds(kc * per_tile + a, 1), :] + skip
                if not pen_per_head:
                    pen = jnp.concatenate([pen] * GROUP, axis=1)
                rows = slice(a * pen_block, (a + 1) * pen_block)
                pieces.append(s_ref[rows, :] + bias[rows] + pen)
            s = pieces[0] if per_tile == 1 else jnp.concatenate(pieces, axis=0)
        else:
            s = s_ref[...] + bias + skip
        m_prev = m_ref[...]
        m_new = jnp.maximum(m_prev, jnp.max(s, axis=0, keepdims=True))
        p = jnp.exp2(s - m_new)
        acc_ref[...] = jnp.exp2(m_prev - m_new) * acc_ref[...] + _dot(vt_ref[:, pl.ds(k0, tk)], _bf(p))
        m_ref[...] = m_new

    k_lo = jnp.maximum(qi - k_back, 0) if k_back is not None else 0
    sa_ref[...] = raw_logits(k_lo)

    def pair(pi, carry):
        ka = k_lo + 2 * pi
        sb_ref[...] = raw_logits(ka + 1)
        attend(sa_ref, ka)
        sa_ref[...] = raw_logits(ka + 2)
        attend(sb_ref, ka + 1)
        return carry

    lax.fori_loop(0, (qi - k_lo + 2) // 2, pair, 0)
    acc = acc_ref[...]
    o_t = acc[0:HEAD_DIM] / jnp.maximum(acc[HEAD_DIM:HEAD_DIM + 1], 1e-30)
    cols_g = _gate_columns(gt_ref, gate_col0 + g * GROUP) if gate_col0 is not None else None
    o_ref[...] = _heads_from_lanes(o_t, tq, cols_g)


def flash_attention(q_arr, kv_arr, k_col, v_col, bias, batch, seq, *, pen=None, pen_block=0,
                    k_back=None, gate_arr=None, gate_col0=None):
    tq = ATTN_TILE
    nq = seq // tq
    cols = GROUP * tq
    assert k_back is None or k_back == bias.shape[1] - 1
    in_specs = [pl.BlockSpec((tq, GROUP * HEAD_DIM), lambda b, g, i: (b * nq + i, g)),
                pl.BlockSpec((seq, HEAD_DIM), lambda b, g, i: (b, k_col + g)),
                pl.BlockSpec((seq, HEAD_DIM), lambda b, g, i: (b, v_col + g)),
                pl.BlockSpec((None,) + bias.shape[1:], lambda b, g, i: (g, 0, 0, 0))]
    args = [q_arr, kv_arr, kv_arr, bias]
    pen_per_head = False
    if pen is not None:
        if pen.ndim == 4:
            in_specs.append(pl.BlockSpec((None, None, pen.shape[2], tq), lambda b, g, i: (b, g, 0, i)))
        else:
            pen_per_head = True
            in_specs.append(pl.BlockSpec((None, None, None, pen.shape[3], cols), lambda b, g, i: (b, g, i, 0, 0)))
        args.append(pen)
    if gate_arr is not None:
        in_specs.append(pl.BlockSpec((tq, LANE), lambda b, g, i: (b * nq + i, 0)))
        args.append(gate_arr)
    return pl.pallas_call(
        functools.partial(_flash_body, pen_block=pen_block if pen is not None else 0, pen_per_head=pen_per_head,
                          k_back=k_back, gate_col0=gate_col0 if gate_arr is not None else None, seq=seq),
        grid=(batch, KV_HEADS, nq),
        in_specs=in_specs,
        out_specs=pl.BlockSpec((tq, GROUP * HEAD_DIM), lambda b, g, i: (b * nq + i, g)),
        out_shape=jax.ShapeDtypeStruct((batch * seq, Q_WIDTH), jnp.float32),
        scratch_shapes=[pltpu.VMEM((1, cols), jnp.float32),
                        pltpu.VMEM((HEAD_DIM + SUBLANE, cols), jnp.float32),
                        pltpu.VMEM((cols, HEAD_DIM), jnp.bfloat16),
                        pltpu.VMEM((seq, HEAD_DIM), jnp.bfloat16),
                        pltpu.VMEM((HEAD_DIM + SUBLANE, seq), jnp.bfloat16),
                        pltpu.VMEM((tq, cols), jnp.float32), pltpu.VMEM((tq, cols), jnp.float32)],
        compiler_params=_cparams("parallel", "parallel", "arbitrary"),
        name="flash_attention",
    )(*args)


def _moba_gate_body(q_ref, k_ref, pen_ref, km_ref, *, n_blocks):
    qi = pl.program_id(2)
    tq = ATTN_TILE
    cols = GROUP * tq

    @pl.when(qi == 0)
    def _():
        k = k_ref[...]
        km_ref[...] = jnp.sum(k.reshape(n_blocks, MOBA_BLOCK, HEAD_DIM), axis=1) / MOBA_BLOCK

    qh, ql = _split_bf16(_stack_heads(q_ref[...]), 2)
    kh, kl = _split_bf16(km_ref[...], 2)
    gate = _dot(kh, qh, NT_DIMS) + _dot(kh, ql, NT_DIMS) + _dot(kl, qh, NT_DIMS)
    blk = lax.broadcasted_iota(jnp.int32, (n_blocks, cols), 0)
    tok = qi * tq + (lax.broadcasted_iota(jnp.int32, (n_blocks, cols), 1) & (tq - 1))
    own = lax.shift_right_logical(tok, int(math.log2(MOBA_BLOCK)))
    gate = jnp.where(blk < own, gate, NEG_INF)
    chosen = ((_rank_rows(gate, n_blocks) < MOBA_TOPK) & (blk < own)) | (blk == own)
    pen_ref[...] = jnp.where(chosen, 0.0, NEG_INF)


def moba_gate(proj, k_col, batch, seq):
    tq = ATTN_TILE
    nq = seq // tq
    cols = GROUP * tq
    nb = seq // MOBA_BLOCK
    return pl.pallas_call(
        functools.partial(_moba_gate_body, n_blocks=nb),
        grid=(batch, KV_HEADS, nq),
        in_specs=[pl.BlockSpec((tq, GROUP * HEAD_DIM), lambda b, g, i: (b * nq + i, g)),
                  pl.BlockSpec((seq, HEAD_DIM), lambda b, g, i: (b, k_col + g))],
        out_specs=pl.BlockSpec((None, None, None, nb, cols), lambda b, g, i: (b, g, i, 0, 0)),
        out_shape=jax.ShapeDtypeStruct((batch, KV_HEADS, nq, nb, cols), jnp.float32),
        scratch_shapes=[pltpu.VMEM((nb, HEAD_DIM), jnp.float32)],
        compiler_params=_cparams("parallel", "parallel", "arbitrary"),
        name="moba_gate",
    )(proj, proj)


def _tril_ones(n, strict=False):
    r = lax.broadcasted_iota(jnp.int32, (n, n), 0)
    c = lax.broadcasted_iota(jnp.int32, (n, n), 1)
    return (r > c) if strict else (r >= c)


def _chunk_cumsum(x):
    tril = _bf(_tril_ones(x.shape[0]))
    return sum(_dot(tril, part) for part in _split_bf16(x, 3))


def _head_rms_gate(o, norm_g, gate):
    ms = jnp.mean(o * o, axis=-1, keepdims=True)
    return o * lax.rsqrt(ms + NORM_EPS) * norm_g * (gate * jax.nn.sigmoid(gate))


def _hgrn2_body(q_ref, f_ref, i_ref, g_ref, lbl_ref, ng_ref, s0_ref, o_ref, s_out_ref, st_ref, *, layer):
    ti = pl.program_id(2)

    @pl.when(ti == 0)
    def _():
        st_ref[...] = s0_ref[...].T

    lbl = lbl_ref[...]
    e = jnp.exp(lbl - jnp.max(lbl, axis=0, keepdims=True))
    p = e / jnp.sum(e, axis=0, keepdims=True)
    lb = jnp.zeros((1, HG_DK), jnp.float32)
    for r in range(1, layer + 1):
        lb = lb + p[r:r + 1]
    causal = _tril_ones(CHUNK)
    work = []
    for c in range(q_ref.shape[0] // CHUNK):
        sl = slice(c * CHUNK, (c + 1) * CHUNK)
        q = q_ref[sl, :]
        qh = q * jax.nn.sigmoid(q) * HG_DK ** -0.5
        fg = lb + (1.0 - lb) * jax.nn.sigmoid(f_ref[sl, :])
        k = 1.0 - fg
        v = _bf(i_ref[sl, :])
        b = _chunk_cumsum(jnp.log(fg))
        b_mid = b[CHUNK // 2:CHUNK // 2 + 1]
        b_last = b[CHUNK - 1:CHUNK]
        a = _dot(_bf(qh * jnp.exp(b - b_mid)), _bf(k * jnp.exp(b_mid - b)), NT_DIMS)
        a = jnp.where(causal, a, 0.0)
        work.append((sl, _dot(_bf(a), v), _bf(qh * jnp.exp(b)), jnp.exp(b_last),
                     _dot(v, _bf(k * jnp.exp(b_last - b)), TN_DIMS)))
    for sl, o_intra, q_in, d_last, kv in work:
        st = st_ref[...]
        o = o_intra + _dot(q_in, _bf(st), NT_DIMS)
        st_ref[...] = st * d_last + kv
        o_ref[sl, :] = _head_rms_gate(o, ng_ref[...], g_ref[sl, :])

    @pl.when(ti == pl.num_programs(2) - 1)
    def _():
        s_out_ref[...] = st_ref[...].T


def hgrn2_scan(proj, lb_logits, norm_g, s0, layer, batch, seq):
    tt = _row_tile(seq, SCAN_TILE)
    nt = seq // tt
    h = HG_HEADS

    def col(k):
        return pl.BlockSpec((tt, HG_DK), lambda b, hh, t: (b * nt + t, k * h + hh))

    return pl.pallas_call(
        functools.partial(_hgrn2_body, layer=layer),
        grid=(batch, h, nt),
        in_specs=[col(0), col(1), col(2), col(3),
                  pl.BlockSpec((DEPTH, HG_DK), lambda b, hh, t: (0, hh)),
                  pl.BlockSpec((1, HG_DV), lambda b, hh, t: (0, 0)),
                  pl.BlockSpec((None, None, HG_DK, HG_DV), lambda b, hh, t: (b, hh, 0, 0))],
        out_specs=[pl.BlockSpec((tt, HG_DV), lambda b, hh, t: (b * nt + t, hh)),
                   pl.BlockSpec((None, None, HG_DK, HG_DV), lambda b, hh, t: (b, hh, 0, 0))],
        out_shape=[jax.ShapeDtypeStruct((batch * seq, h * HG_DV), jnp.float32),
                   jax.ShapeDtypeStruct((batch, h, HG_DK, HG_DV), jnp.float32)],
        scratch_shapes=[pltpu.VMEM((HG_DV, HG_DK), jnp.float32)],
        compiler_params=_cparams("parallel", "parallel", "arbitrary"),
        name="hgrn2_scan",
    )(proj, proj, proj, proj, lb_logits, norm_g.reshape(1, HG_DV), s0)


def _lane_column(x, lane_idx):
    lane = lax.broadcasted_iota(jnp.int32, x.shape, 1)
    return jnp.sum(jnp.where(lane == lane_idx, x, 0.0), axis=1, keepdims=True)


def _softplus(x):
    return jnp.maximum(x, 0.0) + jnp.log(1.0 + jnp.exp(-jnp.abs(x)))


def _l2n(x):
    return x * lax.rsqrt(jnp.sum(x * x, axis=-1, keepdims=True) + NORM_EPS)


def _gdn_body(q_ref, k_ref, v_ref, z_ref, t_ref, bq_ref, bk_ref, bv_ref, wq_ref, wk_ref, wv_ref,
              al_ref, dtb_ref, ng_ref, s0_ref, o_ref, s_out_ref, xs_ref, y_ref, s_ref):
    hq0 = pl.program_id(1) * GDN_HPS
    n_v = GDN_HPS * GDN_REP
    ti = pl.program_id(2)
    tt = q_ref.shape[0]
    dk, dv = GDN_DK, GDN_DV
    pad = SUBLANE

    @pl.when(ti == 0)
    def _():
        s_ref[...] = s0_ref[...]
        xs_ref[0:pad, :] = jnp.concatenate([bq_ref[...], bk_ref[...], bv_ref[...]], axis=1)

    x = jnp.concatenate([q_ref[...], k_ref[...], v_ref[...]], axis=1)
    xs_ref[pad:, :] = x
    cw = jnp.concatenate([wq_ref[...], wk_ref[...], wv_ref[...]], axis=1)
    y = xs_ref[pad - 3:pad - 3 + tt, :] * cw[0:1]
    for i in range(1, GDN_CONV - 1):
        y = y + xs_ref[pad - 3 + i:pad - 3 + i + tt, :] * cw[i:i + 1]
    y = y + x * cw[GDN_CONV - 1:GDN_CONV]
    xs_ref[0:pad, :] = x[tt - pad:tt]
    y_ref[...] = y * jax.nn.sigmoid(y)

    strict = _tril_ones(CHUNK, strict=True)
    incl = _tril_ones(CHUNK)
    sel_rows = lax.shift_right_logical(lax.broadcasted_iota(jnp.int32, (n_v * CHUNK, LANE), 0), int(math.log2(CHUNK)))
    sel_lane = lax.broadcasted_iota(jnp.int32, (n_v * CHUNK, LANE), 1)
    pick = _bf(sel_lane == GDN_V_HEADS + hq0 * GDN_REP + sel_rows)
    n_chunks = tt // CHUNK

    work = []
    for c in range(n_chunks):
        sl = slice(c * CHUNK, (c + 1) * CHUNK)
        yc = y_ref[sl, :]
        tl = t_ref[sl, :]
        beta_all = jax.nn.sigmoid(tl)
        g_all = _chunk_cumsum(-jnp.exp(al_ref[...]) * _softplus(tl + dtb_ref[...]))
        g_rows = sum(_dot(pick, part, NT_DIMS) for part in _split_bf16(g_all, 3))
        for hh in range(GDN_HPS):
            q = _l2n(yc[:, hh * dk:(hh + 1) * dk]) * dk ** -0.5
            k = _l2n(yc[:, (GDN_HPS + hh) * dk:(GDN_HPS + hh + 1) * dk])
            qb, kb = _bf(q), _bf(k)
            kk = _dot(kb, kb, NT_DIMS)
            qk = _dot(qb, kb, NT_DIMS)
            for e in range(hh * GDN_REP, (hh + 1) * GDN_REP):
                hv = hq0 * GDN_REP + e
                v = yc[:, 2 * GDN_HPS * dk + e * dv:2 * GDN_HPS * dk + (e + 1) * dv]
                bt = _lane_column(beta_all, hv)
                gc = _lane_column(g_all, GDN_V_HEADS + hv)
                gdiff = gc - g_rows[e * CHUNK:(e + 1) * CHUNK]
                d_strict = jnp.where(strict, jnp.exp(jnp.where(strict, gdiff, 0.0)), 0.0)
                d_incl = jnp.where(incl, jnp.exp(jnp.where(incl, gdiff, 0.0)), 0.0)
                eg = jnp.exp(gc)
                g_last = gc[CHUNK - 1:CHUNK]
                work.append(dict(
                    c=c, e=e, sol=jnp.concatenate([bt * v, (bt * eg) * k], axis=1), pw=bt * kk * d_strict,
                    aq=_bf(qk * d_incl), q_in=_bf(q * eg), k_out=_bf(k * jnp.exp(g_last - gc)),
                    d_last=jnp.exp(g_last)))

    r_i = lax.broadcasted_iota(jnp.int32, (CHUNK, CHUNK), 0)
    c_i = lax.broadcasted_iota(jnp.int32, (CHUNK, CHUNK), 1)
    same = [lax.shift_right_logical(r_i, sh) == lax.shift_right_logical(c_i, sh) for sh in range(3, 7)]
    eye = (r_i == c_i).astype(jnp.float32)
    for wk in work:
        l8 = jnp.where(same[0], wk["pw"], 0.0)
        l8b = _bf(l8)
        wk["t"] = eye - l8
        wk["p"] = _dot(l8b, l8b)
    for wk in work:
        pb = _bf(wk["p"])
        wk["t"] = wk["t"] + _dot(_bf(wk["t"]), pb)
        wk["p"] = _dot(pb, pb)
    for wk in work:
        wk["t"] = wk["t"] + _dot(_bf(wk["t"]), _bf(wk["p"]))
    for lvl in range(1, len(same)):
        for wk in work:
            tb = _bf(wk["t"])
            off = _bf(jnp.where(same[lvl] & jnp.logical_not(same[lvl - 1]), wk["pw"], 0.0))
            wk["t"] = wk["t"] - _dot(tb, _bf(_dot(off, tb)))
    for wk in work:
        wk["sol"] = _dot(_bf(wk["t"]), _bf(wk["sol"]))

    for wk in work:
        c, e = wk["c"], wk["e"]
        sl = slice(c * CHUNK, (c + 1) * CHUNK)
        u0, w = wk["sol"][:, :dv], wk["sol"][:, dv:]
        s = s_ref[e]
        sb = _bf(s)
        u = u0 - _dot(_bf(w), sb)
        o = _dot(wk["q_in"], sb) + _dot(wk["aq"], _bf(u))
        s_ref[e] = wk["d_last"] * s + _dot(wk["k_out"], _bf(u), TN_DIMS)
        o_ref[sl, e * dv:(e + 1) * dv] = _head_rms_gate(o, ng_ref[...], z_ref[sl, e * dv:(e + 1) * dv])

    @pl.when(ti == pl.num_programs(2) - 1)
    def _():
        s_out_ref[...] = s_ref[...]


def gdn_scan(main, tail, conv_buf, conv_w, a_log, dt_bias, norm_g, s0, batch, seq):
    tt = _row_tile(seq, SCAN_TILE)
    nt = seq // tt
    hq, rep = GDN_QK_HEADS // GDN_HPS, GDN_HPS * GDN_REP
    dk, dv = GDN_HPS * GDN_DK, GDN_DV
    vw = rep * dv
    buf = jnp.pad(conv_buf, ((0, 0), (SUBLANE - (GDN_CONV - 1), 0), (0, 0)))
    pad_lanes = jnp.zeros((LANE - 2 * GDN_V_HEADS,), jnp.float32)
    a_row = jnp.concatenate([jnp.zeros((GDN_V_HEADS,), jnp.float32), a_log, pad_lanes]).reshape(1, LANE)
    dt_row = jnp.concatenate([jnp.zeros((GDN_V_HEADS,), jnp.float32), dt_bias, pad_lanes]).reshape(1, LANE)
    k0 = hq
    v0 = 2 * hq * dk // vw
    z0 = GDN_CONV_DIM // vw
    row = lambda b, h, t: b * nt + t
    return pl.pallas_call(
        _gdn_body,
        grid=(batch, hq, nt),
        in_specs=[pl.BlockSpec((tt, dk), lambda b, h, t: (row(b, h, t), h)),
                  pl.BlockSpec((tt, dk), lambda b, h, t: (row(b, h, t), k0 + h)),
                  pl.BlockSpec((tt, vw), lambda b, h, t: (row(b, h, t), v0 + h)),
                  pl.BlockSpec((tt, vw), lambda b, h, t: (row(b, h, t), z0 + h)),
                  pl.BlockSpec((tt, LANE), lambda b, h, t: (row(b, h, t), 0)),
                  pl.BlockSpec((None, SUBLANE, dk), lambda b, h, t: (b, 0, h)),
                  pl.BlockSpec((None, SUBLANE, dk), lambda b, h, t: (b, 0, k0 + h)),
                  pl.BlockSpec((None, SUBLANE, vw), lambda b, h, t: (b, 0, v0 + h)),
                  pl.BlockSpec((GDN_CONV, dk), lambda b, h, t: (0, h)),
                  pl.BlockSpec((GDN_CONV, dk), lambda b, h, t: (0, k0 + h)),
                  pl.BlockSpec((GDN_CONV, vw), lambda b, h, t: (0, v0 + h)),
                  pl.BlockSpec((1, LANE), lambda b, h, t: (0, 0)),
                  pl.BlockSpec((1, LANE), lambda b, h, t: (0, 0)),
                  pl.BlockSpec((1, dv), lambda b, h, t: (0, 0)),
                  pl.BlockSpec((None, rep, GDN_DK, dv), lambda b, h, t: (b, h, 0, 0))],
        out_specs=[pl.BlockSpec((tt, vw), lambda b, h, t: (row(b, h, t), h)),
                   pl.BlockSpec((None, rep, GDN_DK, dv), lambda b, h, t: (b, h, 0, 0))],
        out_shape=[jax.ShapeDtypeStruct((batch * seq, GDN_V_HEADS * dv), jnp.float32),
                   jax.ShapeDtypeStruct((batch, GDN_V_HEADS, GDN_DK, dv), jnp.float32)],
        scratch_shapes=[pltpu.VMEM((tt + SUBLANE, 2 * dk + vw), jnp.float32),
                        pltpu.VMEM((tt, 2 * dk + vw), jnp.float32),
                        pltpu.VMEM((rep, GDN_DK, dv), jnp.float32)],
        compiler_params=_cparams("parallel", "parallel", "arbitrary"),
        name="gdn_scan",
    )(main, main, main, main, tail, buf, buf, buf, conv_w, conv_w, conv_w, a_row, dt_row,
      norm_g.reshape(1, dv), s0)


PAGE = 128
KV_SLABS = 2 * KV_HEADS
PAGE_ROWS = PAGE * KV_SLABS
CMP_PAGES = 8
CHUNKS_PER_PAGE = PAGE // CMP_STRIDE
ROWS = GROUP * 4


def _page_slab(pg, slab):
    return pg[pl.ds(slab, PAGE, stride=KV_SLABS), :]


def _compress_part_body(*refs, n_prefetch, paged):
    refs = refs[n_prefetch:]
    pages, w_ref, o_ref, xs_ref = refs[:CMP_PAGES], refs[CMP_PAGES], refs[CMP_PAGES + 1], refs[CMP_PAGES + 2]
    per_g = CMP_PAGES * CHUNKS_PER_PAGE
    for k, pg in enumerate(pages):
        for cg in range(KV_SLABS):
            xs_ref[k, cg] = _page_slab(pg, cg) if paged else pg[:, cg * HEAD_DIM:(cg + 1) * HEAD_DIM]
    for c in range(2):
        acc = None
        for l in range(CMP_STRIDE):
            rows = [xs_ref.at[k, c * KV_HEADS + g][pl.ds(l, CHUNKS_PER_PAGE, stride=CMP_STRIDE), :]
                    for g in range(KV_HEADS) for k in range(CMP_PAGES)]
            d = _dot(_bf(jnp.concatenate(rows, axis=0)), w_ref[c, l])
            acc = d if acc is None else acc + d
        for g in range(KV_HEADS):
            o_ref[c, g] = acc[g * per_g:(g + 1) * per_g]


def _compress_weights(cmp_w1):
    n_part = CMP_BLOCK // CMP_STRIDE
    w = cmp_w1.reshape(2, n_part, CMP_STRIDE, HEAD_DIM, CMP_HIDDEN).transpose(0, 2, 3, 1, 4)
    return w.reshape(2, CMP_STRIDE, HEAD_DIM, n_part * CMP_HIDDEN).astype(jnp.bfloat16)


def compress_part_rows(kv_arr, col_block, cmp_w1, batch, seq):
    w = _compress_weights(cmp_w1)
    pages_per_b = seq // PAGE
    steps = pages_per_b // CMP_PAGES
    per_g = CMP_PAGES * CHUNKS_PER_PAGE

    def page_spec(k):
        return pl.BlockSpec((PAGE, KV_WIDTH), lambda b, s: (b * pages_per_b + s * CMP_PAGES + k, col_block))

    return pl.pallas_call(
        functools.partial(_compress_part_body, n_prefetch=0, paged=False),
        grid=(batch, steps),
        in_specs=[page_spec(k) for k in range(CMP_PAGES)] + [pl.BlockSpec(w.shape, lambda b, s: (0, 0, 0, 0))],
        out_specs=pl.BlockSpec((None, 2, KV_HEADS, per_g, w.shape[-1]), lambda b, s: (b, 0, 0, s, 0)),
        out_shape=jax.ShapeDtypeStruct((batch, 2, KV_HEADS, seq // CMP_STRIDE, w.shape[-1]), jnp.float32),
        scratch_shapes=[pltpu.VMEM((CMP_PAGES, 2 * KV_HEADS, PAGE, HEAD_DIM), jnp.float32)],
        compiler_params=_cparams("parallel", "arbitrary"),
        name="compress_part_rows",
    )(*([kv_arr] * CMP_PAGES), w)


def compress_part_paged(pool, page_table, cmp_w1):
    w = _compress_weights(cmp_w1)
    batch, n_pages = page_table.shape
    steps = n_pages // CMP_PAGES
    per_g = CMP_PAGES * CHUNKS_PER_PAGE

    def page_spec(k):
        return pl.BlockSpec((None, PAGE_ROWS, HEAD_DIM), lambda b, s, pt: (pt[b, s * CMP_PAGES + k], 0, 0))

    return pl.pallas_call(
        functools.partial(_compress_part_body, n_prefetch=1, paged=True),
        grid_spec=pltpu.PrefetchScalarGridSpec(
            num_scalar_prefetch=1, grid=(batch, steps),
            in_specs=[page_spec(k) for k in range(CMP_PAGES)]
            + [pl.BlockSpec(w.shape, lambda b, s, pt: (0, 0, 0, 0))],
            out_specs=pl.BlockSpec((None, 2, KV_HEADS, per_g, w.shape[-1]), lambda b, s, pt: (b, 0, 0, s, 0)),
            scratch_shapes=[pltpu.VMEM((CMP_PAGES, 2 * KV_HEADS, PAGE, HEAD_DIM), jnp.float32)]),
        out_shape=jax.ShapeDtypeStruct((batch, 2, KV_HEADS, n_pages * CHUNKS_PER_PAGE, w.shape[-1]), jnp.float32),
        compiler_params=_cparams("parallel", "arbitrary"),
        name="compress_part_paged",
    )(page_table, *([pool] * CMP_PAGES), w)


def _gelu_tanh(x):
    return x * (0.5 * (1.0 + jnp.tanh(math.sqrt(2.0 / math.pi) * (x + 0.044715 * (x * x * x)))))


def _compress_finish_body(p_ref, peh_ref, w2_ref, o_ref):
    n = p_ref.shape[1]
    for c in range(2):
        p = p_ref[c]
        hid = peh_ref[c:c + 1, :] + p[:, :CMP_HIDDEN]
        hid = hid + pltpu.roll(p[:, CMP_HIDDEN:], n - 1, 0)
        o_ref[c] = _dot(_bf(_gelu_tanh(hid)), _bf(w2_ref[c]))


def compress_finish(part, cmp_w1, cmp_w2, cmp_pe):
    batch, _, _, n, width = part.shape
    pe_hid = jnp.einsum('cld,cldh->ch', cmp_pe, cmp_w1)
    return pl.pallas_call(
        _compress_finish_body,
        grid=(batch, KV_HEADS),
        in_specs=[pl.BlockSpec((None, 2, None, n, width), lambda b, g: (b, 0, g, 0, 0)),
                  pl.BlockSpec((2, CMP_HIDDEN), lambda b, g: (0, 0)),
                  pl.BlockSpec((2, CMP_HIDDEN, HEAD_DIM), lambda b, g: (0, 0, 0))],
        out_specs=pl.BlockSpec((None, None, 2, n, HEAD_DIM), lambda b, g: (b, g, 0, 0, 0)),
        out_shape=jax.ShapeDtypeStruct((batch, KV_HEADS, 2, n, HEAD_DIM), jnp.float32),
        compiler_params=_cparams("parallel", "parallel"),
        name="compress_finish",
    )(part, pe_hid, cmp_w2)


def _rows_to_col(row, n):
    eye = lax.broadcasted_iota(jnp.int32, (n, n), 0) == lax.broadcasted_iota(jnp.int32, (n, n), 1)
    return jnp.sum(jnp.where(eye, jnp.broadcast_to(row, (n, n)), 0.0), axis=1, keepdims=True)


def _sample_bias_tiles(rel_table, past, n_new):
    j = np.arange(PAGE)[:, None]
    t = np.arange(n_new)[None, :]
    far = np.full((PAGE, n_new), REL_MAX_DIST)
    first = WINDOW + t - j
    last = PAGE + t - j
    new = t - j
    tiles = []
    for dist, ok in ((far, far > 0), (first, first < WINDOW), (last, last > 0), (new, (new >= 0) & (j < n_new))):
        b = jnp.where(ok[..., None], _bias_lookup(rel_table, dist), NEG_INF)
        b = b.reshape(PAGE, n_new, KV_HEADS, GROUP).transpose(2, 0, 3, 1).reshape(KV_HEADS, PAGE, GROUP * n_new)
        tiles.append(b)
    return jnp.stack(tiles, axis=1)


def _sample_cmp_body(q_ref, kc_ref, bias_ref, gt_ref, o_ref, pen_ref, *, n_sel_blocks, past):
    n = kc_ref.shape[1]
    nbp = pen_ref.shape[0]
    q = _bf(q_ref[...] * ATTN_SCALE)
    s = _dot(_bf(kc_ref[0]), q, NT_DIMS) + bias_ref[...]
    m = jnp.max(s, axis=0, keepdims=True)
    e = jnp.exp(s - m)
    p = e / jnp.maximum(jnp.sum(e, axis=0, keepdims=True), 1e-30)
    o = _dot(_bf(p), _bf(kc_ref[1]), TN_DIMS)
    o_ref[...] = o * jax.nn.sigmoid(gt_ref[...])

    r_i = lax.broadcasted_iota(jnp.int32, (ROWS, ROWS), 0)
    c_i = lax.broadcasted_iota(jnp.int32, (ROWS, ROWS), 1)
    n_tok = ROWS // GROUP
    same_tok = _bf((r_i & (n_tok - 1)) == (c_i & (n_tok - 1)))
    ratio = SEL_BLOCK // CMP_STRIDE
    j_i = lax.broadcasted_iota(jnp.int32, (nbp, n), 0)
    k_i = lax.broadcasted_iota(jnp.int32, (nbp, n), 1)
    w = _bf((k_i >= ratio * j_i - 1) & (k_i <= ratio * j_i + ratio - 1) & (j_i < n_sel_blocks))
    imp = sum(_dot(part, same_tok) for part in _split_bf16(p, 3))
    score = sum(_dot(w, part) for part in _split_bf16(imp, 3))
    blk = lax.broadcasted_iota(jnp.int32, (nbp, ROWS), 0)
    tok = past + (lax.broadcasted_iota(jnp.int32, (nbp, ROWS), 1) & (n_tok - 1))
    cur = lax.shift_right_logical(tok, int(math.log2(SEL_BLOCK)))
    forced = (blk == 0) | (blk == cur) | (blk == cur - 1)
    causal = blk <= cur
    score = jnp.where(forced, FORCE_SCORE, score)
    score = jnp.where(causal, score, -1.0)
    nbl = -(-nbp // LANE) * LANE
    n_idx = lax.broadcasted_iota(jnp.int32, (nbp, nbl), 0)
    m_idx = lax.broadcasted_iota(jnp.int32, (nbp, nbl), 1)
    lane_tok = lax.broadcasted_iota(jnp.int32, (nbp, ROWS), 1) & (n_tok - 1)
    rank = jnp.zeros((nbp, ROWS), jnp.float32)
    for t in range(n_tok):
        s_col = score[:, t:t + 1]
        s_row = jnp.sum(jnp.where(n_idx == m_idx, s_col, 0.0), axis=0, keepdims=True)
        s_row = jnp.where(m_idx[0:1] < n_sel_blocks, s_row, -2.0)
        ahead = (s_row > s_col) | ((s_row == s_col) & (m_idx < n_idx))
        rank_t = jnp.sum(ahead.astype(jnp.float32), axis=1, keepdims=True)
        rank = jnp.where(lane_tok == t, rank_t, rank)
    chosen = (rank < N_SEL) & causal & (blk < n_sel_blocks)
    pen_ref[...] = jnp.where(chosen, 0.0, NEG_INF)


def sample_cmp_select(qs, kc, bias, gate, past, n_sel_blocks):
    batch = qs.shape[0]
    n = kc.shape[3]
    nbp = -(-n_sel_blocks // SUBLANE) * SUBLANE
    return pl.pallas_call(
        functools.partial(_sample_cmp_body, n_sel_blocks=n_sel_blocks, past=past),
        grid=(batch, KV_HEADS),
        in_specs=[pl.BlockSpec((None, None, ROWS, HEAD_DIM), lambda b, g: (b, g, 0, 0)),
                  pl.BlockSpec((None, None, 2, n, HEAD_DIM), lambda b, g: (b, g, 0, 0, 0)),
                  pl.BlockSpec((None, n, ROWS), lambda b, g: (g, 0, 0)),
                  pl.BlockSpec((None, None, ROWS, HEAD_DIM), lambda b, g: (b, g, 0, 0))],
        out_specs=[pl.BlockSpec((None, None, ROWS, HEAD_DIM), lambda b, g: (b, g, 0, 0)),
                   pl.BlockSpec((None, None, nbp, ROWS), lambda b, g: (b, g, 0, 0))],
        out_shape=[jax.ShapeDtypeStruct((batch, KV_HEADS, ROWS, HEAD_DIM), jnp.float32),
                   jax.ShapeDtypeStruct((batch, KV_HEADS, nbp, ROWS), jnp.float32)],
        compiler_params=_cparams("parallel", "parallel"),
        name="sample_cmp_select",
    )(qs, kc, bias, gate)


ATTN_PAGES = 4


def _paged_attn_body(*refs, pen_block, gated, n_pages):
    it = iter(refs)
    pt_ref, tid_ref, q_ref = next(it), next(it), next(it)
    pages = [next(it) for _ in range(ATTN_PAGES)]
    new_ref, bt_ref = next(it), next(it)
    pen_ref = next(it) if pen_block else None
    gt_ref = next(it) if gated else None
    o_ref, m_ref, l_ref, acc_ref = next(it), next(it), next(it), next(it)
    step = pl.program_id(1)

    @pl.when(step == 0)
    def _():
        m_ref[...] = jnp.full(m_ref.shape, NEG_INF, jnp.float32)
        l_ref[...] = jnp.zeros(l_ref.shape, jnp.float32)
        acc_ref[...] = jnp.zeros(acc_ref.shape, jnp.float32)

    def attend(g, k, v, bias, page, n_keys):
        s = _dot(_bf(k), _bf(q_ref[g] * ATTN_SCALE), NT_DIMS) + bias
        if pen_block:
            if pen_block >= PAGE:
                s = s + pen_ref[g, pl.ds(page // (pen_block // PAGE), 1), :]
            else:
                per_page = PAGE // pen_block
                pieces = [s[a * pen_block:min((a + 1) * pen_block, n_keys)]
                          + pen_ref[g, pl.ds(page * per_page + a, 1), :]
                          for a in range(-(-n_keys // pen_block))]
                s = pieces[0] if len(pieces) == 1 else jnp.concatenate(pieces, axis=0)
        m_prev = m_ref[g]
        m_new = jnp.maximum(m_prev, jnp.max(s, axis=0, keepdims=True))
        alpha = jnp.exp(m_prev - m_new)
        p = jnp.exp(s - m_new)
        l_ref[g] = alpha * l_ref[g] + jnp.sum(p, axis=0, keepdims=True)
        acc_ref[g] = _rows_to_col(alpha, ROWS) * acc_ref[g] + _dot(_bf(p), _bf(v), TN_DIMS)
        m_ref[g] = m_new

    half = KV_HEADS * HEAD_DIM
    for kk in range(ATTN_PAGES):
        page = step * ATTN_PAGES + kk
        tile = tid_ref[page]
        for g in range(KV_HEADS):
            attend(g, _page_slab(pages[kk], g), _page_slab(pages[kk], KV_HEADS + g), bt_ref[g, tile], page, PAGE)

    @pl.when(step == pl.num_programs(1) - 1)
    def _():
        n_new = new_ref.shape[0]
        tile = tid_ref[n_pages]
        for g in range(KV_HEADS):
            attend(g, new_ref[:, g * HEAD_DIM:(g + 1) * HEAD_DIM],
                   new_ref[:, half + g * HEAD_DIM:half + (g + 1) * HEAD_DIM], bt_ref[g, tile, 0:n_new, :],
                   n_pages, n_new)
            o = acc_ref[g] / _rows_to_col(jnp.maximum(l_ref[g], 1e-30), ROWS)
            if gated:
                o = o * jax.nn.sigmoid(gt_ref[g])
            o_ref[g] = o


def paged_attention(qs, pool, page_table, tile_ids, new_kv, bias_tiles, *, pen=None, pen_block=0, gate=None):
    batch, n_pages = page_table.shape
    steps = n_pages // ATTN_PAGES
    n_new = new_kv.shape[1]

    def page_spec(k):
        return pl.BlockSpec((None, PAGE_ROWS, HEAD_DIM), lambda b, s, pt, tid: (pt[b, s * ATTN_PAGES + k], 0, 0))

    grp = lambda b, s, pt, tid: (b, 0, 0, 0)
    in_specs = ([pl.BlockSpec((None, KV_HEADS, ROWS, HEAD_DIM), grp)] + [page_spec(k) for k in range(ATTN_PAGES)]
                + [pl.BlockSpec((None, n_new, KV_WIDTH), lambda b, s, pt, tid: (b, 0, 0)),
                   pl.BlockSpec(bias_tiles.shape, lambda b, s, pt, tid: (0, 0, 0, 0))])
    args = [qs] + [pool] * ATTN_PAGES + [new_kv, bias_tiles]
    if pen is not None:
        in_specs.append(pl.BlockSpec((None,) + pen.shape[1:], grp))
        args.append(pen)
    if gate is not None:
        in_specs.append(pl.BlockSpec((None, KV_HEADS, ROWS, HEAD_DIM), grp))
        args.append(gate)
    return pl.pallas_call(
        functools.partial(_paged_attn_body, pen_block=pen_block if pen is not None else 0, gated=gate is not None,
                          n_pages=n_pages),
        grid_spec=pltpu.PrefetchScalarGridSpec(
            num_scalar_prefetch=2, grid=(batch, steps), in_specs=in_specs,
            out_specs=pl.BlockSpec((None, KV_HEADS, ROWS, HEAD_DIM), grp),
            scratch_shapes=[pltpu.VMEM((KV_HEADS, 1, ROWS), jnp.float32), pltpu.VMEM((KV_HEADS, 1, ROWS), jnp.float32),
                            pltpu.VMEM((KV_HEADS, ROWS, HEAD_DIM), jnp.float32)]),
        out_shape=jax.ShapeDtypeStruct((batch, KV_HEADS, ROWS, HEAD_DIM), jnp.float32),
        compiler_params=_cparams("parallel", "arbitrary"),
        name="paged_attention",
    )(page_table, tile_ids, *args)


def _moba_sample_gate_body(pt_ref, q_ref, *refs, n_blocks, past):
    pages, (pen_ref, km_ref) = refs[:ATTN_PAGES], refs[ATTN_PAGES:]
    step = pl.program_id(1)
    half = KV_HEADS * HEAD_DIM
    per_block = MOBA_BLOCK // PAGE

    @pl.when(step == 0)
    def _():
        km_ref[...] = jnp.zeros(km_ref.shape, jnp.float32)

    for kk in range(ATTN_PAGES):
        blk = (step * ATTN_PAGES + kk) // per_block
        slab_sums = jnp.sum(pages[kk][...].reshape(PAGE, KV_SLABS, HEAD_DIM), axis=0)
        for g in range(KV_HEADS):
            km_ref[g, pl.ds(blk, 1), :] += slab_sums[g:g + 1]

    @pl.when(step == pl.num_programs(1) - 1)
    def _():
        nbp = pen_ref.shape[1]
        blk = lax.broadcasted_iota(jnp.int32, (nbp, ROWS), 0)
        n_tok = ROWS // GROUP
        tok = past + (lax.broadcasted_iota(jnp.int32, (nbp, ROWS), 1) & (n_tok - 1))
        own = lax.shift_right_logical(tok, int(math.log2(MOBA_BLOCK)))
        for g in range(KV_HEADS):
            kh, kl = _split_bf16(km_ref[g] / MOBA_BLOCK, 2)
            qh, ql = _split_bf16(q_ref[g], 2)
            gate = _dot(kh, qh, NT_DIMS) + _dot(kh, ql, NT_DIMS) + _dot(kl, qh, NT_DIMS)
            gate = jnp.where(blk < own, gate, NEG_INF)
            chosen = ((_rank_rows(gate, n_blocks) < MOBA_TOPK) & (blk < own)) | (blk == own)
            pen_ref[g] = jnp.where(chosen, 0.0, NEG_INF)


def moba_sample_gate(qs, pool, page_table, past, n_new):
    batch, n_pages = page_table.shape
    steps = n_pages // ATTN_PAGES
    n_blocks = -(-(past + n_new) // MOBA_BLOCK)
    nbp = -(-n_blocks // SUBLANE) * SUBLANE

    def page_spec(k):
        return pl.BlockSpec((None, PAGE_ROWS, HEAD_DIM), lambda b, s, pt: (pt[b, s * ATTN_PAGES + k], 0, 0))

    return pl.pallas_call(
        functools.partial(_moba_sample_gate_body, n_blocks=n_blocks, past=past),
        grid_spec=pltpu.PrefetchScalarGridSpec(
            num_scalar_prefetch=1, grid=(batch, steps),
            in_specs=[pl.BlockSpec((None, KV_HEADS, ROWS, HEAD_DIM), lambda b, s, pt: (b, 0, 0, 0))]
            + [page_spec(k) for k in range(ATTN_PAGES)],
            out_specs=pl.BlockSpec((None, KV_HEADS, nbp, ROWS), lambda b, s, pt: (b, 0, 0, 0)),
            scratch_shapes=[pltpu.VMEM((KV_HEADS, nbp, HEAD_DIM), jnp.float32)]),
        out_shape=jax.ShapeDtypeStruct((batch, KV_HEADS, nbp, ROWS), jnp.float32),
        compiler_params=_cparams("parallel", "arbitrary"),
        name="moba_sample_gate",
    )(page_table, qs, *([pool] * ATTN_PAGES))


def _sample_rows(x, batch, n_tok):
    return x.reshape(batch, n_tok, KV_HEADS, GROUP, HEAD_DIM).transpose(0, 2, 3, 1, 4).reshape(
        batch, KV_HEADS, GROUP * n_tok, HEAD_DIM)


def _sample_unrows(o, batch, n_tok):
    return o.reshape(batch, KV_HEADS, GROUP, n_tok, HEAD_DIM).transpose(0, 3, 1, 2, 4).reshape(batch * n_tok, Q_WIDTH)


def _sample_gate_rows(tail, branch, batch, n_tok):
    gt = tail[:, branch * N_HEADS:(branch + 1) * N_HEADS].reshape(batch, n_tok, KV_HEADS, GROUP)
    gt = gt.transpose(0, 2, 3, 1).reshape(batch, KV_HEADS, GROUP * n_tok, 1)
    return jnp.broadcast_to(gt, (batch, KV_HEADS, GROUP * n_tok, HEAD_DIM))


def _pad_new(kv_new, batch, n_tok):
    return jnp.pad(kv_new.reshape(batch, n_tok, KV_WIDTH), ((0, 0), (0, SUBLANE - n_tok), (0, 0)))


def _nsa_sample_pallas(main, tail, cache_c, cache_s, cache_w, page_table, cmp_w1, cmp_w2, cmp_pe, rel_table):
    batch, n_pages = page_table.shape
    n_tok = main.shape[0] // batch
    past = n_pages * PAGE
    wbuf = cache_w.shape[1]
    assert n_tok == ROWS // GROUP and cache_c.shape[1] == PAGE and wbuf == WINDOW and WINDOW % PAGE == 0
    n_cmp = (past + n_tok - CMP_BLOCK) // CMP_STRIDE + 1
    assert n_cmp + CMP_BLOCK // CMP_STRIDE - 1 == past // CMP_STRIDE
    qs = _sample_rows(main[:, :Q_WIDTH], batch, n_tok)
    kv_new = [main[:, Q_WIDTH + c * KV_WIDTH:Q_WIDTH + (c + 1) * KV_WIDTH] for c in range(3)]
    tiles = _sample_bias_tiles(rel_table, past, n_tok)
    flat = lambda pool: pool.reshape(pool.shape[0], PAGE_ROWS, HEAD_DIM)

    kc = compress_finish(compress_part_paged(flat(cache_c), page_table, cmp_w1), cmp_w1, cmp_w2, cmp_pe)
    n = kc.shape[3]
    dist = past + np.arange(n_tok)[None, :] - (np.arange(n)[:, None] * CMP_STRIDE + CMP_BLOCK - 1)
    ok = (dist >= 0) & (np.arange(n)[:, None] < n_cmp)
    cb = jnp.where(ok[..., None], _bias_lookup(rel_table, dist), NEG_INF)
    cb = cb.reshape(n, n_tok, KV_HEADS, GROUP).transpose(2, 0, 3, 1).reshape(KV_HEADS, n, GROUP * n_tok)
    n_sel_blocks = -(-(past + n_tok) // SEL_BLOCK)
    o_cmp, pen = sample_cmp_select(qs, kc, cb, _sample_gate_rows(tail, 0, batch, n_tok), past, n_sel_blocks)

    far_then_last = jnp.asarray([0] * (n_pages - 1) + [2, 3], jnp.int32)
    o_sel = paged_attention(qs, flat(cache_s), page_table, far_then_last, _pad_new(kv_new[1], batch, n_tok), tiles,
                            pen=pen, pen_block=SEL_BLOCK, gate=_sample_gate_rows(tail, 1, batch, n_tok))
    w_pages = wbuf // PAGE
    win_table = jnp.arange(batch * w_pages, dtype=jnp.int32).reshape(batch, w_pages)
    win_tiles = jnp.asarray([1] + [0] * (w_pages - 2) + [2, 3], jnp.int32)
    o_win = paged_attention(qs, cache_w.reshape(batch * w_pages, PAGE_ROWS, HEAD_DIM), win_table, win_tiles,
                            _pad_new(kv_new[2], batch, n_tok), tiles, gate=_sample_gate_rows(tail, 2, batch, n_tok))
    outs = [_sample_unrows(o, batch, n_tok) for o in (o_cmp, o_sel, o_win)]
    shape = (batch, n_tok, 2, KV_HEADS, HEAD_DIM)
    new_win = jnp.concatenate([cache_w[:, n_tok:], kv_new[2].reshape(shape)], axis=1)
    return outs, kv_new[0].reshape(shape), kv_new[1].reshape(shape), new_win


def _moba_sample_pallas(proj, cache_kv, page_table, rel_table):
    batch, n_pages = page_table.shape
    n_tok = proj.shape[0] // batch
    past = n_pages * PAGE
    assert n_tok == ROWS // GROUP and (past // MOBA_BLOCK) * MOBA_BLOCK == past
    qs = _sample_rows(proj[:, :Q_WIDTH], batch, n_tok)
    kv_new = proj[:, Q_WIDTH:]
    pool = cache_kv.reshape(cache_kv.shape[0], PAGE_ROWS, HEAD_DIM)
    pen = moba_sample_gate(qs, pool, page_table, past, n_tok)
    tile_ids = jnp.asarray([0] * (n_pages - 1) + [2, 3], jnp.int32)
    o = paged_attention(qs, pool, page_table, tile_ids, _pad_new(kv_new, batch, n_tok),
                        _sample_bias_tiles(rel_table, past, n_tok), pen=pen, pen_block=MOBA_BLOCK)
    return _sample_unrows(o, batch, n_tok), kv_new.reshape(batch, n_tok, 2, KV_HEADS, HEAD_DIM)


def _cols_to_row(col, n):
    eye = lax.broadcasted_iota(jnp.int32, (n, n), 0) == lax.broadcasted_iota(jnp.int32, (n, n), 1)
    return jnp.sum(jnp.where(eye, jnp.broadcast_to(col, (n, n)), 0.0), axis=0, keepdims=True)


def _hgrn2_step_body(q_ref, f_ref, i_ref, g_ref, lbl_ref, ng_ref, s0_ref, o_ref, s_out_ref, *, layer):
    n_tok = q_ref.shape[0]
    lbl = lbl_ref[...]
    e = jnp.exp(lbl - jnp.max(lbl, axis=0, keepdims=True))
    p = e / jnp.sum(e, axis=0, keepdims=True)
    lb = jnp.zeros((1, HG_DK), jnp.float32)
    for r in range(1, layer + 1):
        lb = lb + p[r:r + 1]
    q = q_ref[...]
    qh = q * jax.nn.sigmoid(q) * HG_DK ** -0.5
    fg = lb + (1.0 - lb) * jax.nn.sigmoid(f_ref[...])
    k = 1.0 - fg
    v = i_ref[...]
    st = s0_ref[...].T
    rows = []
    for t in range(n_tok):
        st = st * fg[t:t + 1] + _rows_to_col(v[t:t + 1], HG_DV) * k[t:t + 1]
        rows.append(_cols_to_row(jnp.sum(st * qh[t:t + 1], axis=1, keepdims=True), HG_DV))
    o_ref[...] = _head_rms_gate(jnp.concatenate(rows, axis=0), ng_ref[...], g_ref[...])
    s_out_ref[...] = st.T


def hgrn2_step(proj, lb_logits, norm_g, s0, layer):
    batch, n_tok, _ = proj.shape
    h = HG_HEADS

    def col(k):
        return pl.BlockSpec((None, n_tok, HG_DK), lambda b, hh: (b, 0, k * h + hh))

    return pl.pallas_call(
        functools.partial(_hgrn2_step_body, layer=layer),
        grid=(batch, h),
        in_specs=[col(0), col(1), col(2), col(3),
                  pl.BlockSpec((DEPTH, HG_DK), lambda b, hh: (0, hh)),
                  pl.BlockSpec((1, HG_DV), lambda b, hh: (0, 0)),
                  pl.BlockSpec((None, None, HG_DK, HG_DV), lambda b, hh: (b, hh, 0, 0))],
        out_specs=[pl.BlockSpec((None, n_tok, HG_DV), lambda b, hh: (b, 0, hh)),
                   pl.BlockSpec((None, None, HG_DK, HG_DV), lambda b, hh: (b, hh, 0, 0))],
        out_shape=[jax.ShapeDtypeStruct((batch, n_tok, h * HG_DV), jnp.float32),
                   jax.ShapeDtypeStruct((batch, h, HG_DK, HG_DV), jnp.float32)],
        compiler_params=_cparams("parallel", "parallel"),
        name="hgrn2_step",
    )(proj, proj, proj, proj, lb_logits, norm_g.reshape(1, HG_DV), s0)


def _gdn_step_body(q_ref, k_ref, v_ref, z_ref, t_ref, bq_ref, bk_ref, bv_ref, wq_ref, wk_ref, wv_ref,
                   al_ref, dtb_ref, ng_ref, s0_ref, o_ref, s_out_ref, xs_ref):
    hq = pl.program_id(1)
    n_tok = q_ref.shape[0]
    dk, dv = GDN_DK, GDN_DV
    pad = SUBLANE
    xs_ref[0:pad, :] = jnp.concatenate([bq_ref[...], bk_ref[...], bv_ref[...]], axis=1)
    x = jnp.concatenate([q_ref[...], k_ref[...], v_ref[...]], axis=1)
    xs_ref[pad:pad + n_tok, :] = x
    cw = jnp.concatenate([wq_ref[...], wk_ref[...], wv_ref[...]], axis=1)
    y = xs_ref[pad - 3:pad - 3 + n_tok, :] * cw[0:1]
    for i in range(1, GDN_CONV - 1):
        y = y + xs_ref[pad - 3 + i:pad - 3 + i + n_tok, :] * cw[i:i + 1]
    y = y + x * cw[GDN_CONV - 1:GDN_CONV]
    y = y * jax.nn.sigmoid(y)
    q = _l2n(y[:, 0:dk]) * dk ** -0.5
    k = _l2n(y[:, dk:2 * dk])
    tl = t_ref[...]
    beta_all = jax.nn.sigmoid(tl)
    la_all = -jnp.exp(al_ref[...]) * _softplus(tl + dtb_ref[...])
    for e in range(GDN_REP):
        hv = hq * GDN_REP + e
        v = y[:, 2 * dk + e * dv:2 * dk + (e + 1) * dv]
        bt = _lane_column(beta_all, hv)
        a = jnp.exp(_lane_column(la_all, GDN_V_HEADS + hv))
        s = s0_ref[e]
        rows = []
        for t in range(n_tok):
            k_col = _rows_to_col(k[t:t + 1], dk)
            ks = jnp.sum(s * k_col, axis=0, keepdims=True)
            u = bt[t:t + 1] * (v[t:t + 1] - a[t:t + 1] * ks)
            s = a[t:t + 1] * s + k_col * u
            rows.append(jnp.sum(s * _rows_to_col(q[t:t + 1], dk), axis=0, keepdims=True))
        s_out_ref[e] = s
        o_ref[:, e * dv:(e + 1) * dv] = _head_rms_gate(jnp.concatenate(rows, axis=0), ng_ref[...],
                                                       z_ref[:, e * dv:(e + 1) * dv])


def gdn_step(main, tail, conv_buf, conv_w, a_log, dt_bias, norm_g, s0):
    batch, n_tok, _ = main.shape
    hq, rep, dk, dv = GDN_QK_HEADS, GDN_REP, GDN_DK, GDN_DV
    vw = rep * dv
    buf = jnp.pad(conv_buf, ((0, 0), (SUBLANE - (GDN_CONV - 1), 0), (0, 0)))
    pad_lanes = jnp.zeros((LANE - 2 * GDN_V_HEADS,), jnp.float32)
    a_row = jnp.concatenate([jnp.zeros((GDN_V_HEADS,), jnp.float32), a_log, pad_lanes]).reshape(1, LANE)
    dt_row = jnp.concatenate([jnp.zeros((GDN_V_HEADS,), jnp.float32), dt_bias, pad_lanes]).reshape(1, LANE)
    k0 = hq
    v0 = 2 * hq * dk // vw
    z0 = GDN_CONV_DIM // vw
    return pl.pallas_call(
        _gdn_step_body,
        grid=(batch, hq),
        in_specs=[pl.BlockSpec((None, n_tok, dk), lambda b, h: (b, 0, h)),
                  pl.BlockSpec((None, n_tok, dk), lambda b, h: (b, 0, k0 + h)),
                  pl.BlockSpec((None, n_tok, vw), lambda b, h: (b, 0, v0 + h)),
                  pl.BlockSpec((None, n_tok, vw), lambda b, h: (b, 0, z0 + h)),
                  pl.BlockSpec((None, n_tok, LANE), lambda b, h: (b, 0, 0)),
                  pl.BlockSpec((None, SUBLANE, dk), lambda b, h: (b, 0, h)),
                  pl.BlockSpec((None, SUBLANE, dk), lambda b, h: (b, 0, k0 + h)),
                  pl.BlockSpec((None, SUBLANE, vw), lambda b, h: (b, 0, v0 + h)),
                  pl.BlockSpec((GDN_CONV, dk), lambda b, h: (0, h)),
                  pl.BlockSpec((GDN_CONV, dk), lambda b, h: (0, k0 + h)),
                  pl.BlockSpec((GDN_CONV, vw), lambda b, h: (0, v0 + h)),
                  pl.BlockSpec((1, LANE), lambda b, h: (0, 0)),
                  pl.BlockSpec((1, LANE), lambda b, h: (0, 0)),
                  pl.BlockSpec((1, dv), lambda b, h: (0, 0)),
                  pl.BlockSpec((None, rep, dk, dv), lambda b, h: (b, h, 0, 0))],
        out_specs=[pl.BlockSpec((None, n_tok, vw), lambda b, h: (b, 0, h)),
                   pl.BlockSpec((None, rep, dk, dv), lambda b, h: (b, h, 0, 0))],
        out_shape=[jax.ShapeDtypeStruct((batch, n_tok, GDN_V_HEADS * dv), jnp.float32),
                   jax.ShapeDtypeStruct((batch, GDN_V_HEADS, dk, dv), jnp.float32)],
        scratch_shapes=[pltpu.VMEM((2 * SUBLANE, 2 * dk + vw), jnp.float32)],
        compiler_params=_cparams("parallel", "parallel"),
        name="gdn_step",
    )(main, main, main, main, tail, buf, buf, buf, conv_w, conv_w, conv_w, a_row, dt_row, norm_g.reshape(1, dv), s0)


def _pad_cols(w, mult=LANE):
    n = w.shape[1]
    return jnp.pad(w, ((0, 0), (0, (-n) % mult)))


def _nsa_prompt(main, kv_rows, tail, batch, seq, cmp_w1, cmp_w2, cmp_pe, rel_table):
    kv_c, kv_s, kv_w = (kv_rows[c].reshape(batch, seq, 2, KV_HEADS, HEAD_DIM) for c in range(3))
    kc = compress_finish(compress_part_rows(main, Q_WIDTH // KV_WIDTH, cmp_w1, batch, seq), cmp_w1, cmp_w2, cmp_pe)
    o_cmp, pen = cmp_select(main, tail, kc, _cmp_bias_table(rel_table, kc.shape[3]), batch, seq)
    col = Q_WIDTH // HEAD_DIM
    o_sel = flash_attention(main, main, col + 2 * KV_HEADS, col + 3 * KV_HEADS, _flash_bias_tiles(rel_table, 0),
                            batch, seq, pen=pen, pen_block=SEL_BLOCK, gate_arr=tail, gate_col0=N_HEADS)
    o_win = flash_attention(main, main, col + 4 * KV_HEADS, col + 5 * KV_HEADS,
                            _flash_bias_tiles(rel_table, WINDOW), batch, seq, k_back=WINDOW // ATTN_TILE,
                            gate_arr=tail, gate_col0=2 * N_HEADS)
    return [o_cmp, o_sel, o_win], kv_c, kv_s, kv_w[:, -min(WINDOW, seq):]


def _moba_prompt(proj, batch, seq, rel_table):
    col = Q_WIDTH // HEAD_DIM
    pen = moba_gate(proj, col, batch, seq)
    return flash_attention(proj, proj, col, col + KV_HEADS, _flash_bias_tiles(rel_table, 0), batch, seq,
                           pen=pen, pen_block=MOBA_BLOCK)


def kernel(x_prompt, x_sample, cache_nsa_cmp_kv, cache_nsa_sel_kv, cache_nsa_win_kv, cache_moba_kv,
           state_hgrn2, state_gdn_conv, state_gdn_ssm, page_table, rel_table, ln_mix, ln_ffn, ln_final,
           ffn_w_up, ffn_w_down, nsa_w_in, nsa_cmp_w1, nsa_cmp_w2, nsa_cmp_pe, nsa_w_out, moba_w_in, moba_w_out,
           hg_w_in, hg_lb_logits, hg_norm, hg_w_out, gdn_w_in, gdn_conv_w, gdn_a_log, gdn_dt_bias, gdn_norm,
           gdn_w_out):
    bf = jnp.bfloat16
    bp, tp = x_prompt.shape[:2]
    bs, ts = x_sample.shape[:2]
    assert tp % ATTN_TILE == 0 and WINDOW % ATTN_TILE == 0 and ATTN_TILE == MOBA_BLOCK and tp % SCAN_TILE == 0
    xp = x_prompt.reshape(bp * tp, D_MODEL)
    xs = x_sample.reshape(bs * ts, D_MODEL)

    for layer in range(DEPTH):
        kind = layer % N_MIXERS
        g_mix = ln_mix[layer]
        if kind == 0:
            w_main, w_tail = nsa_w_in[:, :NSA_MAIN].astype(bf), _pad_cols(nsa_w_in[:, NSA_MAIN:]).astype(bf)
            (main_p, kv_p), tail_p = norm_matmul(xp, g_mix, w_main, kv_from=Q_WIDTH // KV_WIDTH), norm_matmul(xp, g_mix, w_tail)
            main_s, tail_s = norm_matmul(xs, g_mix, w_main), norm_matmul(xs, g_mix, w_tail)
            op, nsa_cmp_p, nsa_sel_p, nsa_win_p = _nsa_prompt(main_p, kv_p, tail_p, bp, tp, nsa_cmp_w1, nsa_cmp_w2,
                                                              nsa_cmp_pe, rel_table)
            os_, nsa_cmp_s, nsa_sel_s, nsa_win_s = _nsa_sample_pallas(main_s, tail_s, cache_nsa_cmp_kv,
                                                                     cache_nsa_sel_kv, cache_nsa_win_kv, page_table,
                                                                     nsa_cmp_w1, nsa_cmp_w2, nsa_cmp_pe, rel_table)
            w_out = nsa_w_out.astype(bf)
        elif kind == 1:
            w_in = moba_w_in.astype(bf)
            pp, kv_p = norm_matmul(xp, g_mix, w_in, kv_from=Q_WIDTH // KV_WIDTH)
            op, moba_p = _moba_prompt(pp, bp, tp, rel_table), kv_p.reshape(bp, tp, 2, KV_HEADS, HEAD_DIM)
            os_, moba_s = _moba_sample_pallas(norm_matmul(xs, g_mix, w_in), cache_moba_kv, page_table, rel_table)
            op, os_ = [op], [os_]
            w_out = moba_w_out.astype(bf)
        elif kind == 2:
            w_in = hg_w_in.astype(bf)
            pp = norm_matmul(xp, g_mix, w_in)
            ps = norm_matmul(xs, g_mix, w_in).reshape(bs, ts, -1)
            s0 = jnp.zeros((bp, HG_HEADS, HG_DK, HG_DV), jnp.float32)
            op, hg_p = hgrn2_scan(pp, hg_lb_logits, hg_norm, s0, layer, bp, tp)
            os_, hg_s = hgrn2_step(ps, hg_lb_logits, hg_norm, state_hgrn2, layer)
            op, os_ = [op], [os_.reshape(bs * ts, -1)]
            w_out = hg_w_out.astype(bf)
        else:
            w_main, w_tail = gdn_w_in[:, :GDN_MAIN].astype(bf), _pad_cols(gdn_w_in[:, GDN_MAIN:]).astype(bf)
            main_p, tail_p = norm_matmul(xp, g_mix, w_main), norm_matmul(xp, g_mix, w_tail)
            main_s = norm_matmul(xs, g_mix, w_main).reshape(bs, ts, -1)
            tail_s = norm_matmul(xs, g_mix, w_tail).reshape(bs, ts, -1)
            buf0 = jnp.zeros((bp, GDN_CONV - 1, GDN_CONV_DIM), jnp.float32)
            s0 = jnp.zeros((bp, GDN_V_HEADS, GDN_DK, GDN_DV), jnp.float32)
            op, ssm_p = gdn_scan(main_p, tail_p, buf0, gdn_conv_w, gdn_a_log, gdn_dt_bias, gdn_norm, s0, bp, tp)
            conv_p = main_p.reshape(bp, tp, -1)[:, tp - (GDN_CONV - 1):, :GDN_CONV_DIM]
            os_, ssm_s = gdn_step(main_s, tail_s, state_gdn_conv, gdn_conv_w, gdn_a_log, gdn_dt_bias, gdn_norm,
                                  state_gdn_ssm)
            conv_s = jnp.concatenate([state_gdn_conv, main_s[:, :, :GDN_CONV_DIM]], axis=1)[:, ts:]
            op, os_ = [op], [os_.reshape(bs * ts, -1)]
            w_out = gdn_w_out.astype(bf)
        xp = matmul_res(op, w_out, xp)
        xs = matmul_res(os_, w_out, xs)
        w_up, w_down = ffn_w_up[layer].astype(bf), ffn_w_down[layer].astype(bf)
        xp = ffn(xp, ln_ffn[layer], w_up, w_down)
        xs = ffn(xs, ln_ffn[layer], w_up, w_down)
    y_prompt = final_norm(xp, ln_final).reshape(bp, tp, D_MODEL)
    y_sample = final_norm(xs, ln_final).reshape(bs, ts, D_MODEL)
    return (y_prompt, y_sample, nsa_cmp_p, nsa_cmp_s, nsa_sel_p, nsa_sel_s, nsa_win_p, nsa_win_s,
            moba_p, moba_s, hg_p, hg_s, conv_p, conv_s, ssm_p, ssm_s)
```

```python
import functools
import math

import jax
import jax.numpy as jnp
import numpy as np
from jax import lax
from jax.experimental import pallas as pl
from jax.experimental.pallas import tpu as pltpu

D_MODEL = 2048
DEPTH = 4
N_MIXERS = 4
HEAD_DIM = 128
N_HEADS = D_MODEL // HEAD_DIM
KV_HEADS = 4
GROUP = N_HEADS // KV_HEADS
ATTN_SCALE = HEAD_DIM ** -0.5
REL_BUCKETS = 32
REL_MAX_DIST = 128
CMP_BLOCK = 32
CMP_STRIDE = 16
CMP_HIDDEN = HEAD_DIM
SEL_BLOCK = 64
N_SEL = 16
WINDOW = 512
FORCE_SCORE = 1.0e4
MOBA_BLOCK = 256
MOBA_TOPK = 3
HG_DK = 128
HG_HEADS = D_MODEL // HG_DK
HG_DV = D_MODEL // HG_HEADS
GDN_DK = 128
GDN_DV = 128
GDN_QK_HEADS = D_MODEL // GDN_DK
GDN_V_HEADS = 2 * GDN_QK_HEADS
GDN_REP = GDN_V_HEADS // GDN_QK_HEADS
GDN_CONV = 4
GDN_CONV_DIM = 2 * GDN_QK_HEADS * GDN_DK + GDN_V_HEADS * GDN_DV
CHUNK = 64
NEG_INF = -1.0e30
NORM_EPS = 1e-6

Q_WIDTH = N_HEADS * HEAD_DIM
KV_WIDTH = 2 * KV_HEADS * HEAD_DIM
NSA_MAIN = Q_WIDTH + 3 * KV_WIDTH
GDN_MAIN = GDN_CONV_DIM + GDN_V_HEADS * GDN_DV

V7X_VMEM_LIMIT_BYTES = 56 * 1024 * 1024
LANE = 128
SUBLANE = 8
ATTN_TILE = 256
CMP_TILE = 128
SCAN_TILE = 512
GDN_HPS = 2
LOG2E = math.log2(math.e)
NT_DIMS = (((1,), (1,)), ((), ()))
TN_DIMS = (((0,), (0,)), ((), ()))


def _cparams(*sem):
    return pltpu.CompilerParams(dimension_semantics=sem, vmem_limit_bytes=V7X_VMEM_LIMIT_BYTES)


def _row_tile(m, target):
    t = min(m, target)
    while m % t:
        t //= 2
    return t


def _col_tile(n, target):
    t = min(n, target)
    while n % t or t % LANE:
        t -= LANE
    return t


def _split_bf16(x, parts):
    out = []
    for _ in range(parts - 1):
        hi = x.astype(jnp.bfloat16)
        out.append(hi)
        x = x - hi.astype(jnp.float32)
    out.append(x.astype(jnp.bfloat16))
    return out


def _bf(x):
    return x.astype(jnp.bfloat16)


def _dot(a, b, dims=None):
    if dims is None:
        return jnp.dot(a, b, preferred_element_type=jnp.float32)
    return lax.dot_general(a, b, dims, preferred_element_type=jnp.float32)


def _norm_matmul_body(x_ref, g_ref, w_ref, o_ref, *rest, kv_from):
    kv_ref, h_ref = rest if kv_from is not None else (None, rest[0])
    j = pl.program_id(1)

    @pl.when(j == 0)
    def _():
        x = x_ref[...]
        ms = jnp.mean(x * x, axis=-1, keepdims=True)
        h_ref[...] = _bf(x * lax.rsqrt(ms + NORM_EPS) * g_ref[...])

    out = _dot(h_ref[...], w_ref[...])
    o_ref[...] = out
    if kv_from is not None:
        @pl.when(j >= kv_from)
        def _():
            tm = out.shape[0]
            for slab in range(out.shape[1] // HEAD_DIM):
                kv_ref[pl.ds(slab, tm, stride=out.shape[1] // HEAD_DIM), :] = out[:, slab * HEAD_DIM:(slab + 1) * HEAD_DIM]


def norm_matmul(x, g, w, kv_from=None, n_cols=None):
    m, k = x.shape
    n = n_cols or w.shape[1]
    tm = _row_tile(m, 1024)
    tn = _col_tile(n, 1024)
    out_specs = [pl.BlockSpec((tm, tn), lambda i, j: (i, j))]
    out_shape = [jax.ShapeDtypeStruct((m, n), jnp.float32)]
    if kv_from is not None:
        assert tn == KV_WIDTH
        slabs = tn // HEAD_DIM
        out_specs.append(pl.BlockSpec((None, tm * slabs, HEAD_DIM), lambda i, j: (jnp.maximum(j - kv_from, 0), i, 0)))
        out_shape.append(jax.ShapeDtypeStruct((n // tn - kv_from, m * slabs, HEAD_DIM), jnp.float32))
    res = pl.pallas_call(
        functools.partial(_norm_matmul_body, kv_from=kv_from),
        grid=(m // tm, n // tn),
        in_specs=[pl.BlockSpec((tm, k), lambda i, j: (i, 0)),
                  pl.BlockSpec((1, k), lambda i, j: (0, 0)),
                  pl.BlockSpec((k, tn), lambda i, j: (0, j))],
        out_specs=out_specs,
        out_shape=out_shape,
        scratch_shapes=[pltpu.VMEM((tm, k), jnp.bfloat16)],
        compiler_params=_cparams("parallel", "arbitrary"),
        name="norm_matmul",
    )(x, g.reshape(1, k), w)
    return res if kv_from is not None else res[0]


def _matmul_res_body(*refs):
    *a_refs, w_ref, r_ref, o_ref = refs
    a = a_refs[0][...]
    for a_ref in a_refs[1:]:
        a = a + a_ref[...]
    o_ref[...] = r_ref[...] + _dot(_bf(a), w_ref[...])


def matmul_res(a_list, w, res):
    m, k = a_list[0].shape
    n = w.shape[1]
    tm = _row_tile(m, 512)
    tn = _col_tile(n, 1024)
    return pl.pallas_call(
        _matmul_res_body,
        grid=(m // tm, n // tn),
        in_specs=[pl.BlockSpec((tm, k), lambda i, j: (i, 0)) for _ in a_list]
        + [pl.BlockSpec((k, tn), lambda i, j: (0, j)),
           pl.BlockSpec((tm, tn), lambda i, j: (i, j))],
        out_specs=pl.BlockSpec((tm, tn), lambda i, j: (i, j)),
        out_shape=jax.ShapeDtypeStruct((m, n), jnp.float32),
        compiler_params=_cparams("parallel", "arbitrary"),
        name="matmul_res",
    )(*a_list, w, res)


def _ffn_body(x_ref, g_ref, wa_ref, wb_ref, wd_ref, o_ref, h_ref):
    @pl.when(pl.program_id(1) == 0)
    def _():
        x = x_ref[...]
        ms = jnp.mean(x * x, axis=-1, keepdims=True)
        h_ref[...] = _bf(x * lax.rsqrt(ms + NORM_EPS) * g_ref[...])
        o_ref[...] = x

    h = h_ref[...]
    a = _dot(h, wa_ref[...])
    b = _dot(h, wb_ref[...])
    o_ref[...] += _dot(_bf(a * jax.nn.sigmoid(a) * b), wd_ref[...])


def ffn(x, g, w_up, w_down, layer):
    m, k = x.shape
    hdim = w_down.shape[1]
    tm = _row_tile(m, 1024)
    th = _col_tile(hdim, 512)
    nh = hdim // th
    return pl.pallas_call(
        _ffn_body,
        grid=(m // tm, nh),
        in_specs=[pl.BlockSpec((tm, k), lambda i, j: (i, 0)),
                  pl.BlockSpec((1, k), lambda i, j: (0, 0)),
                  pl.BlockSpec((None, k, th), lambda i, j: (layer, 0, j)),
                  pl.BlockSpec((None, k, th), lambda i, j: (layer, 0, j + nh)),
                  pl.BlockSpec((None, th, k), lambda i, j: (layer, j, 0))],
        out_specs=pl.BlockSpec((tm, k), lambda i, j: (i, 0)),
        out_shape=jax.ShapeDtypeStruct((m, k), jnp.float32),
        scratch_shapes=[pltpu.VMEM((tm, k), jnp.bfloat16)],
        compiler_params=_cparams("parallel", "arbitrary"),
        name="ffn",
    )(x, g.reshape(1, k), w_up, w_up, w_down)


def _norm_body(x_ref, g_ref, o_ref):
    x = x_ref[...]
    ms = jnp.mean(x * x, axis=-1, keepdims=True)
    o_ref[...] = x * lax.rsqrt(ms + NORM_EPS) * g_ref[...]


def final_norm(x, g):
    m, k = x.shape
    tm = _row_tile(m, 512)
    return pl.pallas_call(
        _norm_body,
        grid=(m // tm,),
        in_specs=[pl.BlockSpec((tm, k), lambda i: (i, 0)), pl.BlockSpec((1, k), lambda i: (0, 0))],
        out_specs=pl.BlockSpec((tm, k), lambda i: (i, 0)),
        out_shape=jax.ShapeDtypeStruct((m, k), jnp.float32),
        compiler_params=_cparams("parallel"),
        name="final_norm",
    )(x, g.reshape(1, k))


def _bucket_np(dist):
    exact = REL_BUCKETS // 2
    d = np.maximum(dist, 0)
    ratio = np.log(np.maximum(d, 1).astype(np.float32) / exact) / math.log(REL_MAX_DIST / exact)
    large = np.minimum(exact + (ratio * (REL_BUCKETS - exact)).astype(np.int32), REL_BUCKETS - 1)
    return np.where(d < exact, d, large)


def _bias_lookup(rel_table, dist):
    bucket = _bucket_np(dist).astype(np.int32)
    ids = [int(b) for b in np.unique(bucket)]
    bk = jnp.asarray(bucket)[..., None]
    out = jnp.broadcast_to(rel_table[ids[0]], bucket.shape + (rel_table.shape[1],))
    for b in ids[1:]:
        out = jnp.where(bk == b, rel_table[b], out)
    return out


def _heads_to_lanes(t):
    keys, queries, _ = t.shape
    return t.reshape(keys, queries, KV_HEADS, GROUP).transpose(2, 0, 3, 1).reshape(KV_HEADS, keys, GROUP * queries)


def _flash_bias_tiles(rel_table, window):
    j = np.arange(ATTN_TILE)[:, None]
    i = np.arange(ATTN_TILE)[None, :]
    n_cls = window // ATTN_TILE + 1 if window else -(-REL_MAX_DIST // ATTN_TILE) + 2
    tiles = []
    for d in range(n_cls):
        dist = d * ATTN_TILE + i - j
        ok = dist >= 0
        if window:
            ok = ok & (dist < window)
        tiles.append(_heads_to_lanes(jnp.where(ok[..., None], _bias_lookup(rel_table, dist) * LOG2E, NEG_INF)))
    return jnp.stack(tiles, axis=1)


def _cmp_bias_table(rel_table, ncp):
    x = np.arange(ncp)[:, None]
    i = np.arange(CMP_TILE)[None, :]
    dist = i - CMP_STRIDE * (x - 16) - (CMP_BLOCK - 1)
    far = rel_table[REL_BUCKETS - 1]
    b = _heads_to_lanes(jnp.where((dist >= 0)[..., None], _bias_lookup(rel_table, dist), far))
    return jnp.concatenate([b, b], axis=1)


def _stack_heads(q):
    return jnp.concatenate([q[:, r * HEAD_DIM:(r + 1) * HEAD_DIM] for r in range(GROUP)], axis=0)


def _gate_columns(gt_ref, col0):
    gt = jax.nn.sigmoid(gt_ref[...])
    lane = lax.broadcasted_iota(jnp.int32, gt.shape, 1)
    return [jnp.sum(jnp.where(lane == col0 + r, gt, 0.0), axis=1, keepdims=True) for r in range(GROUP)]


def _heads_from_lanes(o_t, rows, cols=None):
    parts = []
    for r in range(GROUP):
        part = o_t[:, r * rows:(r + 1) * rows].T
        if cols is not None:
            part = part * cols[r]
        parts.append(part)
    return jnp.concatenate(parts, axis=1)


def _rank_rows(score, n_rows):
    row = lax.broadcasted_iota(jnp.int32, score.shape, 0)
    rank = jnp.zeros(score.shape, jnp.int32)
    for mm in range(n_rows):
        sm = score[mm:mm + 1, :]
        ahead = (sm > score) | ((sm == score) & (row > mm))
        rank = rank + ahead.astype(jnp.int32)
    return rank


def _cmp_select_body(q_ref, kc_ref, dt_ref, gt_ref, o_ref, pen_ref, *, n_sel_blocks):
    g = pl.program_id(1)
    qi = pl.program_id(2)
    tq = CMP_TILE
    cols = GROUP * tq
    ncp = kc_ref.shape[1]
    q4 = _bf(_stack_heads(q_ref[...]) * ATTN_SCALE)
    s = _dot(_bf(kc_ref[0]), q4, NT_DIMS)
    shift = (qi * (tq // CMP_STRIDE) + ncp - 16) % ncp
    bias = dt_ref[pl.ds(pl.multiple_of(ncp - shift, SUBLANE), ncp), :]
    t_col = qi * tq + (lax.broadcasted_iota(jnp.int32, (ncp, cols), 1) & (tq - 1))
    end_pos = lax.broadcasted_iota(jnp.int32, (ncp, cols), 0) * CMP_STRIDE + (CMP_BLOCK - 1)
    mask = t_col >= end_pos
    s = jnp.where(mask, s + bias, NEG_INF)
    m = jnp.max(s, axis=0, keepdims=True)
    e = jnp.where(mask, jnp.exp(s - m), 0.0)
    p = e / jnp.maximum(jnp.sum(e, axis=0, keepdims=True), 1e-30)
    o_t = _dot(_bf(kc_ref[1].T), _bf(p))
    o_ref[...] = _heads_from_lanes(o_t, tq, _gate_columns(gt_ref, g * GROUP))

    imp = p[:, 0:tq]
    for r in range(1, GROUP):
        imp = imp + p[:, r * tq:(r + 1) * tq]
    ratio = SEL_BLOCK // CMP_STRIDE
    j_i = lax.broadcasted_iota(jnp.int32, (n_sel_blocks, ncp), 0)
    c_i = lax.broadcasted_iota(jnp.int32, (n_sel_blocks, ncp), 1)
    w = _bf((c_i >= ratio * j_i - 1) & (c_i <= ratio * j_i + ratio - 1))
    score = sum(_dot(w, part) for part in _split_bf16(imp, 3))
    blk = lax.broadcasted_iota(jnp.int32, (n_sel_blocks, tq), 0)
    tok = qi * tq + lax.broadcasted_iota(jnp.int32, (n_sel_blocks, tq), 1)
    cur = lax.shift_right_logical(tok, int(math.log2(SEL_BLOCK)))
    forced = (blk == 0) | (blk == cur) | (blk == cur - 1)
    causal = blk <= cur
    score = jnp.where(forced, FORCE_SCORE, score)
    score = jnp.where(causal, score, -1.0)
    chosen = (_rank_rows(score, n_sel_blocks) < N_SEL) & causal
    pen_ref[...] = jnp.where(chosen, 0.0, NEG_INF)


def cmp_select(proj, tail, kc, dt, batch, seq):
    tq = CMP_TILE
    nq = seq // tq
    ncp = kc.shape[3]
    nsb = seq // SEL_BLOCK
    return pl.pallas_call(
        functools.partial(_cmp_select_body, n_sel_blocks=nsb),
        grid=(batch, KV_HEADS, nq),
        in_specs=[pl.BlockSpec((tq, GROUP * HEAD_DIM), lambda b, g, i: (b * nq + i, g)),
                  pl.BlockSpec((None, None, 2, ncp, HEAD_DIM), lambda b, g, i: (b, g, 0, 0, 0)),
                  pl.BlockSpec((None, 2 * ncp, GROUP * tq), lambda b, g, i: (g, 0, 0)),
                  pl.BlockSpec((tq, LANE), lambda b, g, i: (b * nq + i, 0))],
        out_specs=[pl.BlockSpec((tq, GROUP * HEAD_DIM), lambda b, g, i: (b * nq + i, g)),
                   pl.BlockSpec((None, None, nsb, tq), lambda b, g, i: (b, g, 0, i))],
        out_shape=[jax.ShapeDtypeStruct((batch * seq, Q_WIDTH), jnp.float32),
                   jax.ShapeDtypeStruct((batch, KV_HEADS, nsb, seq), jnp.float32)],
        compiler_params=_cparams("parallel", "parallel", "arbitrary"),
        name="cmp_select",
    )(proj, kc, dt, tail)


def _flash_body(*refs, pen_block, pen_per_head, k_back, gate_col0, seq):
    it = iter(refs)
    q_ref, k_ref, v_ref, bt_ref = next(it), next(it), next(it), next(it)
    pen_ref = next(it) if pen_block else None
    gt_ref = next(it) if gate_col0 is not None else None
    o_ref, m_ref, acc_ref, qa_ref, kb_ref, vt_ref, sa_ref, sb_ref = (next(it) for _ in range(8))
    g = pl.program_id(1)
    qi = pl.program_id(2)
    tq = tk = ATTN_TILE
    n_cls = bt_ref.shape[0]

    @pl.when(qi == 0)
    def _():
        vt_ref[HEAD_DIM:, :] = jnp.ones((SUBLANE, seq), jnp.bfloat16)
        for c in range(seq // tk):
            kb_ref[c * tk:(c + 1) * tk, :] = _bf(k_ref[c * tk:(c + 1) * tk, :])
            vt_ref[0:HEAD_DIM, c * tk:(c + 1) * tk] = _bf(v_ref[c * tk:(c + 1) * tk, :].T)

    qa_ref[...] = _bf(_stack_heads(q_ref[...]) * (ATTN_SCALE * LOG2E))
    m_ref[...] = jnp.full(m_ref.shape, NEG_INF, jnp.float32)
    acc_ref[...] = jnp.zeros(acc_ref.shape, jnp.float32)

    def raw_logits(kj):
        k0 = pl.multiple_of(jnp.minimum(kj, qi) * tk, tk)
        return _dot(kb_ref[pl.ds(k0, tk), :], qa_ref[...], NT_DIMS)

    def attend(s_ref, kj):
        kc = jnp.minimum(kj, qi)
        k0 = pl.multiple_of(kc * tk, tk)
        skip = jnp.where(kj <= qi, 0.0, NEG_INF)
        bias = bt_ref[jnp.minimum(qi - kc, n_cls - 1)]
        if pen_block:
            per_tile = tk // pen_block
            pieces = []
            for a in range(per_tile):
                pen = pen_ref[pl.ds(kc * per_tile + a, 1), :] + skip
                if not pen_per_head:
                    pen = jnp.concatenate([pen] * GROUP, axis=1)
                rows = slice(a * pen_block, (a + 1) * pen_block)
                pieces.append(s_ref[rows, :] + bias[rows] + pen)
            s = pieces[0] if per_tile == 1 else jnp.concatenate(pieces, axis=0)
        else:
            s = s_ref[...] + bias + skip
        m_prev = m_ref[...]
        m_new = jnp.maximum(m_prev, jnp.max(s, axis=0, keepdims=True))
        p = jnp.exp2(s - m_new)
        acc_ref[...] = jnp.exp2(m_prev - m_new) * acc_ref[...] + _dot(vt_ref[:, pl.ds(k0, tk)], _bf(p))
        m_ref[...] = m_new

    k_lo = jnp.maximum(qi - k_back, 0) if k_back is not None else 0
    sa_ref[...] = raw_logits(k_lo)

    def pair(pi, carry):
        ka = k_lo + 2 * pi
        sb_ref[...] = raw_logits(ka + 1)
        attend(sa_ref, ka)
        sa_ref[...] = raw_logits(ka + 2)
        attend(sb_ref, ka + 1)
        return carry

    lax.fori_loop(0, (qi - k_lo + 2) // 2, pair, 0)
    acc = acc_ref[...]
    o_t = acc[0:HEAD_DIM] / jnp.maximum(acc[HEAD_DIM:HEAD_DIM + 1], 1e-30)
    cols_g = _gate_columns(gt_ref, gate_col0 + g * GROUP) if gate_col0 is not None else None
    o_ref[...] = _heads_from_lanes(o_t, tq, cols_g)


def flash_attention(q_arr, kv_arr, k_col, v_col, bias, batch, seq, *, pen=None, pen_block=0,
                    k_back=None, gate_arr=None, gate_col0=None):
    tq = ATTN_TILE
    nq = seq // tq
    cols = GROUP * tq
    assert k_back is None or k_back == bias.shape[1] - 1
    in_specs = [pl.BlockSpec((tq, GROUP * HEAD_DIM), lambda b, g, i: (b * nq + i, g)),
                pl.BlockSpec((seq, HEAD_DIM), lambda b, g, i: (b, k_col + g)),
                pl.BlockSpec((seq, HEAD_DIM), lambda b, g, i: (b, v_col + g)),
                pl.BlockSpec((None,) + bias.shape[1:], lambda b, g, i: (g, 0, 0, 0))]
    args = [q_arr, kv_arr, kv_arr, bias]
    pen_per_head = False
    if pen is not None:
        if pen.ndim == 4:
            in_specs.append(pl.BlockSpec((None, None, pen.shape[2], tq), lambda b, g, i: (b, g, 0, i)))
        else:
            pen_per_head = True
            in_specs.append(pl.BlockSpec((None, None, None, pen.shape[3], cols), lambda b, g, i: (b, g, i, 0, 0)))
        args.append(pen)
    if gate_arr is not None:
        in_specs.append(pl.BlockSpec((tq, LANE), lambda b, g, i: (b * nq + i, 0)))
        args.append(gate_arr)
    return pl.pallas_call(
        functools.partial(_flash_body, pen_block=pen_block if pen is not None else 0, pen_per_head=pen_per_head,
                          k_back=k_back, gate_col0=gate_col0 if gate_arr is not None else None, seq=seq),
        grid=(batch, KV_HEADS, nq),
        in_specs=in_specs,
        out_specs=pl.BlockSpec((tq, GROUP * HEAD_DIM), lambda b, g, i: (b * nq + i, g)),
        out_shape=jax.ShapeDtypeStruct((batch * seq, Q_WIDTH), jnp.float32),
        scratch_shapes=[pltpu.VMEM((1, cols), jnp.float32),
                        pltpu.VMEM((HEAD_DIM + SUBLANE, cols), jnp.float32),
                        pltpu.VMEM((cols, HEAD_DIM), jnp.bfloat16),
                        pltpu.VMEM((seq, HEAD_DIM), jnp.bfloat16),
                        pltpu.VMEM((HEAD_DIM + SUBLANE, seq), jnp.bfloat16),
                        pltpu.VMEM((tq, cols), jnp.float32), pltpu.VMEM((tq, cols), jnp.float32)],
        compiler_params=_cparams("parallel", "parallel", "arbitrary"),
        name="flash_attention",
    )(*args)


def _moba_gate_body(q_ref, k_ref, pen_ref, km_ref, *, n_blocks):
    qi = pl.program_id(2)
    tq = ATTN_TILE
    cols = GROUP * tq

    @pl.when(qi == 0)
    def _():
        k = k_ref[...]
        km_ref[...] = jnp.sum(k.reshape(n_blocks, MOBA_BLOCK, HEAD_DIM), axis=1) / MOBA_BLOCK

    qh, ql = _split_bf16(_stack_heads(q_ref[...]), 2)
    kh, kl = _split_bf16(km_ref[...], 2)
    gate = _dot(kh, qh, NT_DIMS) + _dot(kh, ql, NT_DIMS) + _dot(kl, qh, NT_DIMS)
    blk = lax.broadcasted_iota(jnp.int32, (n_blocks, cols), 0)
    tok = qi * tq + (lax.broadcasted_iota(jnp.int32, (n_blocks, cols), 1) & (tq - 1))
    own = lax.shift_right_logical(tok, int(math.log2(MOBA_BLOCK)))
    gate = jnp.where(blk < own, gate, NEG_INF)
    chosen = ((_rank_rows(gate, n_blocks) < MOBA_TOPK) & (blk < own)) | (blk == own)
    pen_ref[...] = jnp.where(chosen, 0.0, NEG_INF)


def moba_gate(proj, k_col, batch, seq):
    tq = ATTN_TILE
    nq = seq // tq
    cols = GROUP * tq
    nb = seq // MOBA_BLOCK
    return pl.pallas_call(
        functools.partial(_moba_gate_body, n_blocks=nb),
        grid=(batch, KV_HEADS, nq),
        in_specs=[pl.BlockSpec((tq, GROUP * HEAD_DIM), lambda b, g, i: (b * nq + i, g)),
                  pl.BlockSpec((seq, HEAD_DIM), lambda b, g, i: (b, k_col + g))],
        out_specs=pl.BlockSpec((None, None, None, nb, cols), lambda b, g, i: (b, g, i, 0, 0)),
        out_shape=jax.ShapeDtypeStruct((batch, KV_HEADS, nq, nb, cols), jnp.float32),
        scratch_shapes=[pltpu.VMEM((nb, HEAD_DIM), jnp.float32)],
        compiler_params=_cparams("parallel", "parallel", "arbitrary"),
        name="moba_gate",
    )(proj, proj)


def _tril_ones(n, strict=False):
    r = lax.broadcasted_iota(jnp.int32, (n, n), 0)
    c = lax.broadcasted_iota(jnp.int32, (n, n), 1)
    return (r > c) if strict else (r >= c)


def _chunk_cumsum(x):
    tril = _bf(_tril_ones(x.shape[0]))
    return sum(_dot(tril, part) for part in _split_bf16(x, 3))


def _head_rms_gate(o, norm_g, gate):
    ms = jnp.mean(o * o, axis=-1, keepdims=True)
    return o * lax.rsqrt(ms + NORM_EPS) * norm_g * (gate * jax.nn.sigmoid(gate))


def _hgrn2_body(q_ref, f_ref, i_ref, g_ref, lbl_ref, ng_ref, s0_ref, o_ref, s_out_ref, st_ref, *, layer):
    ti = pl.program_id(2)

    @pl.when(ti == 0)
    def _():
        st_ref[...] = s0_ref[...].T

    lbl = lbl_ref[...]
    e = jnp.exp(lbl - jnp.max(lbl, axis=0, keepdims=True))
    p = e / jnp.sum(e, axis=0, keepdims=True)
    lb = jnp.zeros((1, HG_DK), jnp.float32)
    for r in range(1, layer + 1):
        lb = lb + p[r:r + 1]
    causal = _tril_ones(CHUNK)
    work = []
    for c in range(q_ref.shape[0] // CHUNK):
        sl = slice(c * CHUNK, (c + 1) * CHUNK)
        q = q_ref[sl, :]
        qh = q * jax.nn.sigmoid(q) * HG_DK ** -0.5
        fg = lb + (1.0 - lb) * jax.nn.sigmoid(f_ref[sl, :])
        k = 1.0 - fg
        v = _bf(i_ref[sl, :])
        b = _chunk_cumsum(jnp.log(fg))
        b_mid = b[CHUNK // 2:CHUNK // 2 + 1]
        b_last = b[CHUNK - 1:CHUNK]
        a = _dot(_bf(qh * jnp.exp(b - b_mid)), _bf(k * jnp.exp(b_mid - b)), NT_DIMS)
        a = jnp.where(causal, a, 0.0)
        work.append((sl, _dot(_bf(a), v), _bf(qh * jnp.exp(b)), jnp.exp(b_last),
                     _dot(v, _bf(k * jnp.exp(b_last - b)), TN_DIMS)))
    for sl, o_intra, q_in, d_last, kv in work:
        st = st_ref[...]
        o = o_intra + _dot(q_in, _bf(st), NT_DIMS)
        st_ref[...] = st * d_last + kv
        o_ref[sl, :] = _head_rms_gate(o, ng_ref[...], g_ref[sl, :])

    @pl.when(ti == pl.num_programs(2) - 1)
    def _():
        s_out_ref[...] = st_ref[...].T


def hgrn2_scan(proj, lb_logits, norm_g, s0, layer, batch, seq):
    tt = _row_tile(seq, SCAN_TILE)
    nt = seq // tt
    h = HG_HEADS

    def col(k):
        return pl.BlockSpec((tt, HG_DK), lambda b, hh, t: (b * nt + t, k * h + hh))

    return pl.pallas_call(
        functools.partial(_hgrn2_body, layer=layer),
        grid=(batch, h, nt),
        in_specs=[col(0), col(1), col(2), col(3),
                  pl.BlockSpec((DEPTH, HG_DK), lambda b, hh, t: (0, hh)),
                  pl.BlockSpec((1, HG_DV), lambda b, hh, t: (0, 0)),
                  pl.BlockSpec((None, None, HG_DK, HG_DV), lambda b, hh, t: (b, hh, 0, 0))],
        out_specs=[pl.BlockSpec((tt, HG_DV), lambda b, hh, t: (b * nt + t, hh)),
                   pl.BlockSpec((None, None, HG_DK, HG_DV), lambda b, hh, t: (b, hh, 0, 0))],
        out_shape=[jax.ShapeDtypeStruct((batch * seq, h * HG_DV), jnp.float32),
                   jax.ShapeDtypeStruct((batch, h, HG_DK, HG_DV), jnp.float32)],
        scratch_shapes=[pltpu.VMEM((HG_DV, HG_DK), jnp.float32)],
        compiler_params=_cparams("parallel", "parallel", "arbitrary"),
        name="hgrn2_scan",
    )(proj, proj, proj, proj, lb_logits, norm_g.reshape(1, HG_DV), s0)


def _lane_column(x, lane_idx):
    lane = lax.broadcasted_iota(jnp.int32, x.shape, 1)
    return jnp.sum(jnp.where(lane == lane_idx, x, 0.0), axis=1, keepdims=True)


def _softplus(x):
    return jnp.maximum(x, 0.0) + jnp.log(1.0 + jnp.exp(-jnp.abs(x)))


def _l2n(x):
    return x * lax.rsqrt(jnp.sum(x * x, axis=-1, keepdims=True) + NORM_EPS)


def _gdn_body(q_ref, k_ref, v_ref, z_ref, t_ref, bq_ref, bk_ref, bv_ref, wq_ref, wk_ref, wv_ref,
              al_ref, dtb_ref, ng_ref, s0_ref, o_ref, s_out_ref, xs_ref, y_ref, s_ref):
    hq0 = pl.program_id(1) * GDN_HPS
    n_v = GDN_HPS * GDN_REP
    ti = pl.program_id(2)
    tt = q_ref.shape[0]
    dk, dv = GDN_DK, GDN_DV
    pad = SUBLANE

    @pl.when(ti == 0)
    def _():
        s_ref[...] = s0_ref[...]
        xs_ref[0:pad, :] = jnp.concatenate([bq_ref[...], bk_ref[...], bv_ref[...]], axis=1)

    x = jnp.concatenate([q_ref[...], k_ref[...], v_ref[...]], axis=1)
    xs_ref[pad:, :] = x
    cw = jnp.concatenate([wq_ref[...], wk_ref[...], wv_ref[...]], axis=1)
    y = xs_ref[pad - 3:pad - 3 + tt, :] * cw[0:1]
    for i in range(1, GDN_CONV - 1):
        y = y + xs_ref[pad - 3 + i:pad - 3 + i + tt, :] * cw[i:i + 1]
    y = y + x * cw[GDN_CONV - 1:GDN_CONV]
    xs_ref[0:pad, :] = x[tt - pad:tt]
    y_ref[...] = y * jax.nn.sigmoid(y)

    strict = _tril_ones(CHUNK, strict=True)
    incl = _tril_ones(CHUNK)
    sel_rows = lax.shift_right_logical(lax.broadcasted_iota(jnp.int32, (n_v * CHUNK, LANE), 0), int(math.log2(CHUNK)))
    sel_lane = lax.broadcasted_iota(jnp.int32, (n_v * CHUNK, LANE), 1)
    pick = _bf(sel_lane == GDN_V_HEADS + hq0 * GDN_REP + sel_rows)
    n_chunks = tt // CHUNK

    work = []
    for c in range(n_chunks):
        sl = slice(c * CHUNK, (c + 1) * CHUNK)
        yc = y_ref[sl, :]
        tl = t_ref[sl, :]
        beta_all = jax.nn.sigmoid(tl)
        g_all = _chunk_cumsum(-jnp.exp(al_ref[...]) * _softplus(tl + dtb_ref[...]))
        g_rows = sum(_dot(pick, part, NT_DIMS) for part in _split_bf16(g_all, 3))
        for hh in range(GDN_HPS):
            q = _l2n(yc[:, hh * dk:(hh + 1) * dk]) * dk ** -0.5
            k = _l2n(yc[:, (GDN_HPS + hh) * dk:(GDN_HPS + hh + 1) * dk])
            qb, kb = _bf(q), _bf(k)
            kk = _dot(kb, kb, NT_DIMS)
            qk = _dot(qb, kb, NT_DIMS)
            for e in range(hh * GDN_REP, (hh + 1) * GDN_REP):
                hv = hq0 * GDN_REP + e
                v = yc[:, 2 * GDN_HPS * dk + e * dv:2 * GDN_HPS * dk + (e + 1) * dv]
                bt = _lane_column(beta_all, hv)
                gc = _lane_column(g_all, GDN_V_HEADS + hv)
                gdiff = gc - g_rows[e * CHUNK:(e + 1) * CHUNK]
                decay = jnp.exp(jnp.where(incl, gdiff, 0.0))
                d_strict = jnp.where(strict, decay, 0.0)
                d_incl = jnp.where(incl, decay, 0.0)
                eg = jnp.exp(gc)
                g_last = gc[CHUNK - 1:CHUNK]
                work.append(dict(
                    c=c, e=e, sol=jnp.concatenate([bt * v, (bt * eg) * k], axis=1), pw=bt * kk * d_strict,
                    aq=_bf(qk * d_incl), q_in=_bf(q * eg), k_out=_bf(k * jnp.exp(g_last - gc)),
                    d_last=jnp.exp(g_last)))

    r_i = lax.broadcasted_iota(jnp.int32, (CHUNK, CHUNK), 0)
    c_i = lax.broadcasted_iota(jnp.int32, (CHUNK, CHUNK), 1)
    same = [lax.shift_right_logical(r_i, sh) == lax.shift_right_logical(c_i, sh) for sh in range(3, 7)]
    eye = (r_i == c_i).astype(jnp.float32)
    for wk in work:
        l8 = jnp.where(same[0], wk["pw"], 0.0)
        l8b = _bf(l8)
        wk["t"] = eye - l8
        wk["p"] = _dot(l8b, l8b)
    for wk in work:
        pb = _bf(wk["p"])
        wk["t"] = wk["t"] + _dot(_bf(wk["t"]), pb)
        wk["p"] = _dot(pb, pb)
    for wk in work:
        wk["t"] = wk["t"] + _dot(_bf(wk["t"]), _bf(wk["p"]))
    for lvl in range(1, len(same)):
        for wk in work:
            tb = _bf(wk["t"])
            off = _bf(jnp.where(same[lvl] & jnp.logical_not(same[lvl - 1]), wk["pw"], 0.0))
            wk["t"] = wk["t"] - _dot(tb, _bf(_dot(off, tb)))
    for wk in work:
        wk["sol"] = _dot(_bf(wk["t"]), _bf(wk["sol"]))

    for wk in work:
        c, e = wk["c"], wk["e"]
        sl = slice(c * CHUNK, (c + 1) * CHUNK)
        u0, w = wk["sol"][:, :dv], wk["sol"][:, dv:]
        s = s_ref[e]
        sb = _bf(s)
        u = u0 - _dot(_bf(w), sb)
        o = _dot(wk["q_in"], sb) + _dot(wk["aq"], _bf(u))
        s_ref[e] = wk["d_last"] * s + _dot(wk["k_out"], _bf(u), TN_DIMS)
        o_ref[sl, e * dv:(e + 1) * dv] = _head_rms_gate(o, ng_ref[...], z_ref[sl, e * dv:(e + 1) * dv])

    @pl.when(ti == pl.num_programs(2) - 1)
    def _():
        s_out_ref[...] = s_ref[...]


def gdn_scan(main, tail, conv_buf, conv_w, a_log, dt_bias, norm_g, s0, batch, seq):
    tt = _row_tile(seq, SCAN_TILE)
    nt = seq // tt
    hq, rep = GDN_QK_HEADS // GDN_HPS, GDN_HPS * GDN_REP
    dk, dv = GDN_HPS * GDN_DK, GDN_DV
    vw = rep * dv
    buf = jnp.pad(conv_buf, ((0, 0), (SUBLANE - (GDN_CONV - 1), 0), (0, 0)))
    pad_lanes = jnp.zeros((LANE - 2 * GDN_V_HEADS,), jnp.float32)
    a_row = jnp.concatenate([jnp.zeros((GDN_V_HEADS,), jnp.float32), a_log, pad_lanes]).reshape(1, LANE)
    dt_row = jnp.concatenate([jnp.zeros((GDN_V_HEADS,), jnp.float32), dt_bias, pad_lanes]).reshape(1, LANE)
    k0 = hq
    v0 = 2 * hq * dk // vw
    z0 = GDN_CONV_DIM // vw
    row = lambda b, h, t: b * nt + t
    return pl.pallas_call(
        _gdn_body,
        grid=(batch, hq, nt),
        in_specs=[pl.BlockSpec((tt, dk), lambda b, h, t: (row(b, h, t), h)),
                  pl.BlockSpec((tt, dk), lambda b, h, t: (row(b, h, t), k0 + h)),
                  pl.BlockSpec((tt, vw), lambda b, h, t: (row(b, h, t), v0 + h)),
                  pl.BlockSpec((tt, vw), lambda b, h, t: (row(b, h, t), z0 + h)),
                  pl.BlockSpec((tt, LANE), lambda b, h, t: (row(b, h, t), 0)),
                  pl.BlockSpec((None, SUBLANE, dk), lambda b, h, t: (b, 0, h)),
                  pl.BlockSpec((None, SUBLANE, dk), lambda b, h, t: (b, 0, k0 + h)),
                  pl.BlockSpec((None, SUBLANE, vw), lambda b, h, t: (b, 0, v0 + h)),
                  pl.BlockSpec((GDN_CONV, dk), lambda b, h, t: (0, h)),
                  pl.BlockSpec((GDN_CONV, dk), lambda b, h, t: (0, k0 + h)),
                  pl.BlockSpec((GDN_CONV, vw), lambda b, h, t: (0, v0 + h)),
                  pl.BlockSpec((1, LANE), lambda b, h, t: (0, 0)),
                  pl.BlockSpec((1, LANE), lambda b, h, t: (0, 0)),
                  pl.BlockSpec((1, dv), lambda b, h, t: (0, 0)),
                  pl.BlockSpec((None, rep, GDN_DK, dv), lambda b, h, t: (b, h, 0, 0))],
        out_specs=[pl.BlockSpec((tt, vw), lambda b, h, t: (row(b, h, t), h)),
                   pl.BlockSpec((None, rep, GDN_DK, dv), lambda b, h, t: (b, h, 0, 0))],
        out_shape=[jax.ShapeDtypeStruct((batch * seq, GDN_V_HEADS * dv), jnp.float32),
                   jax.ShapeDtypeStruct((batch, GDN_V_HEADS, GDN_DK, dv), jnp.float32)],
        scratch_shapes=[pltpu.VMEM((tt + SUBLANE, 2 * dk + vw), jnp.float32),
                        pltpu.VMEM((tt, 2 * dk + vw), jnp.float32),
                        pltpu.VMEM((rep, GDN_DK, dv), jnp.float32)],
        compiler_params=_cparams("parallel", "parallel", "arbitrary"),
        name="gdn_scan",
    )(main, main, main, main, tail, buf, buf, buf, conv_w, conv_w, conv_w, a_row, dt_row,
      norm_g.reshape(1, dv), s0)


PAGE = 128
KV_SLABS = 2 * KV_HEADS
PAGE_ROWS = PAGE * KV_SLABS
CMP_PAGES = 8
CHUNKS_PER_PAGE = PAGE // CMP_STRIDE
ROWS = GROUP * 4


def _page_slab(pg, slab):
    return pg[pl.ds(slab, PAGE, stride=KV_SLABS), :]


def _compress_part_body(*refs, n_prefetch, paged):
    refs = refs[n_prefetch:]
    pages, w_ref, o_ref, xs_ref = refs[:CMP_PAGES], refs[CMP_PAGES], refs[CMP_PAGES + 1], refs[CMP_PAGES + 2]
    per_g = CMP_PAGES * CHUNKS_PER_PAGE
    for k, pg in enumerate(pages):
        for cg in range(KV_SLABS):
            xs_ref[k, cg] = _page_slab(pg, cg) if paged else pg[:, cg * HEAD_DIM:(cg + 1) * HEAD_DIM]
    for c in range(2):
        acc = None
        for l in range(CMP_STRIDE):
            rows = [xs_ref.at[k, c * KV_HEADS + g][pl.ds(l, CHUNKS_PER_PAGE, stride=CMP_STRIDE), :]
                    for g in range(KV_HEADS) for k in range(CMP_PAGES)]
            d = _dot(_bf(jnp.concatenate(rows, axis=0)), w_ref[c, l])
            acc = d if acc is None else acc + d
        for g in range(KV_HEADS):
            o_ref[c, g] = acc[g * per_g:(g + 1) * per_g]


def _compress_weights(cmp_w1):
    n_part = CMP_BLOCK // CMP_STRIDE
    w = cmp_w1.reshape(2, n_part, CMP_STRIDE, HEAD_DIM, CMP_HIDDEN).transpose(0, 2, 3, 1, 4)
    return w.reshape(2, CMP_STRIDE, HEAD_DIM, n_part * CMP_HIDDEN).astype(jnp.bfloat16)


def compress_part_rows(kv_arr, col_block, cmp_w1, batch, seq):
    w = _compress_weights(cmp_w1)
    pages_per_b = seq // PAGE
    steps = pages_per_b // CMP_PAGES
    per_g = CMP_PAGES * CHUNKS_PER_PAGE

    def page_spec(k):
        return pl.BlockSpec((PAGE, KV_WIDTH), lambda b, s: (b * pages_per_b + s * CMP_PAGES + k, col_block))

    return pl.pallas_call(
        functools.partial(_compress_part_body, n_prefetch=0, paged=False),
        grid=(batch, steps),
        in_specs=[page_spec(k) for k in range(CMP_PAGES)] + [pl.BlockSpec(w.shape, lambda b, s: (0, 0, 0, 0))],
        out_specs=pl.BlockSpec((None, 2, KV_HEADS, per_g, w.shape[-1]), lambda b, s: (b, 0, 0, s, 0)),
        out_shape=jax.ShapeDtypeStruct((batch, 2, KV_HEADS, seq // CMP_STRIDE, w.shape[-1]), jnp.float32),
        scratch_shapes=[pltpu.VMEM((CMP_PAGES, 2 * KV_HEADS, PAGE, HEAD_DIM), jnp.float32)],
        compiler_params=_cparams("parallel", "arbitrary"),
        name="compress_part_rows",
    )(*([kv_arr] * CMP_PAGES), w)


def compress_part_paged(pool, page_table, cmp_w1):
    w = _compress_weights(cmp_w1)
    batch, n_pages = page_table.shape
    steps = n_pages // CMP_PAGES
    per_g = CMP_PAGES * CHUNKS_PER_PAGE

    def page_spec(k):
        return pl.BlockSpec((None, PAGE_ROWS, HEAD_DIM), lambda b, s, pt: (pt[b, s * CMP_PAGES + k], 0, 0))

    return pl.pallas_call(
        functools.partial(_compress_part_body, n_prefetch=1, paged=True),
        grid_spec=pltpu.PrefetchScalarGridSpec(
            num_scalar_prefetch=1, grid=(batch, steps),
            in_specs=[page_spec(k) for k in range(CMP_PAGES)]
            + [pl.BlockSpec(w.shape, lambda b, s, pt: (0, 0, 0, 0))],
            out_specs=pl.BlockSpec((None, 2, KV_HEADS, per_g, w.shape[-1]), lambda b, s, pt: (b, 0, 0, s, 0)),
            scratch_shapes=[pltpu.VMEM((CMP_PAGES, 2 * KV_HEADS, PAGE, HEAD_DIM), jnp.float32)]),
        out_shape=jax.ShapeDtypeStruct((batch, 2, KV_HEADS, n_pages * CHUNKS_PER_PAGE, w.shape[-1]), jnp.float32),
        compiler_params=_cparams("parallel", "arbitrary"),
        name="compress_part_paged",
    )(page_table, *([pool] * CMP_PAGES), w)


def _gelu_tanh(x):
    return x * (0.5 * (1.0 + jnp.tanh(math.sqrt(2.0 / math.pi) * (x + 0.044715 * (x * x * x)))))


def _compress_finish_body(p_ref, peh_ref, w2_ref, o_ref):
    n = p_ref.shape[1]
    for c in range(2):
        p = p_ref[c]
        hid = peh_ref[c:c + 1, :] + p[:, :CMP_HIDDEN]
        hid = hid + pltpu.roll(p[:, CMP_HIDDEN:], n - 1, 0)
        o_ref[c] = _dot(_bf(_gelu_tanh(hid)), _bf(w2_ref[c]))


def compress_finish(part, cmp_w1, cmp_w2, cmp_pe):
    batch, _, _, n, width = part.shape
    pe_hid = jnp.einsum('cld,cldh->ch', cmp_pe, cmp_w1)
    return pl.pallas_call(
        _compress_finish_body,
        grid=(batch, KV_HEADS),
        in_specs=[pl.BlockSpec((None, 2, None, n, width), lambda b, g: (b, 0, g, 0, 0)),
                  pl.BlockSpec((2, CMP_HIDDEN), lambda b, g: (0, 0)),
                  pl.BlockSpec((2, CMP_HIDDEN, HEAD_DIM), lambda b, g: (0, 0, 0))],
        out_specs=pl.BlockSpec((None, None, 2, n, HEAD_DIM), lambda b, g: (b, g, 0, 0, 0)),
        out_shape=jax.ShapeDtypeStruct((batch, KV_HEADS, 2, n, HEAD_DIM), jnp.float32),
        compiler_params=_cparams("parallel", "parallel"),
        name="compress_finish",
    )(part, pe_hid, cmp_w2)


def _rows_to_col(row, n):
    eye = lax.broadcasted_iota(jnp.int32, (n, n), 0) == lax.broadcasted_iota(jnp.int32, (n, n), 1)
    return jnp.sum(jnp.where(eye, jnp.broadcast_to(row, (n, n)), 0.0), axis=1, keepdims=True)


def _sample_bias_tiles(rel_table, past, n_new):
    j = np.arange(PAGE)[:, None]
    t = np.arange(n_new)[None, :]
    far = np.full((PAGE, n_new), REL_MAX_DIST)
    first = WINDOW + t - j
    last = PAGE + t - j
    new = t - j
    tiles = []
    for dist, ok in ((far, far > 0), (first, first < WINDOW), (last, last > 0), (new, (new >= 0) & (j < n_new))):
        b = jnp.where(ok[..., None], _bias_lookup(rel_table, dist), NEG_INF)
        b = b.reshape(PAGE, n_new, KV_HEADS, GROUP).transpose(2, 0, 3, 1).reshape(KV_HEADS, PAGE, GROUP * n_new)
        tiles.append(b)
    return jnp.stack(tiles, axis=1)


def _sample_cmp_body(q_ref, kc_ref, bias_ref, gt_ref, o_ref, pen_ref, *, n_sel_blocks, past):
    n = kc_ref.shape[1]
    nbp = pen_ref.shape[0]
    q = _bf(q_ref[...] * ATTN_SCALE)
    s = _dot(_bf(kc_ref[0]), q, NT_DIMS) + bias_ref[...]
    m = jnp.max(s, axis=0, keepdims=True)
    e = jnp.exp(s - m)
    p = e / jnp.maximum(jnp.sum(e, axis=0, keepdims=True), 1e-30)
    o = _dot(_bf(p), _bf(kc_ref[1]), TN_DIMS)
    o_ref[...] = o * jax.nn.sigmoid(gt_ref[...])

    r_i = lax.broadcasted_iota(jnp.int32, (ROWS, ROWS), 0)
    c_i = lax.broadcasted_iota(jnp.int32, (ROWS, ROWS), 1)
    n_tok = ROWS // GROUP
    same_tok = _bf((r_i & (n_tok - 1)) == (c_i & (n_tok - 1)))
    ratio = SEL_BLOCK // CMP_STRIDE
    j_i = lax.broadcasted_iota(jnp.int32, (nbp, n), 0)
    k_i = lax.broadcasted_iota(jnp.int32, (nbp, n), 1)
    w = _bf((k_i >= ratio * j_i - 1) & (k_i <= ratio * j_i + ratio - 1) & (j_i < n_sel_blocks))
    imp = sum(_dot(part, same_tok) for part in _split_bf16(p, 3))
    score = sum(_dot(w, part) for part in _split_bf16(imp, 3))
    blk = lax.broadcasted_iota(jnp.int32, (nbp, ROWS), 0)
    tok = past + (lax.broadcasted_iota(jnp.int32, (nbp, ROWS), 1) & (n_tok - 1))
    cur = lax.shift_right_logical(tok, int(math.log2(SEL_BLOCK)))
    forced = (blk == 0) | (blk == cur) | (blk == cur - 1)
    causal = blk <= cur
    score = jnp.where(forced, FORCE_SCORE, score)
    score = jnp.where(causal, score, -1.0)
    nbl = -(-nbp // LANE) * LANE
    n_idx = lax.broadcasted_iota(jnp.int32, (nbp, nbl), 0)
    m_idx = lax.broadcasted_iota(jnp.int32, (nbp, nbl), 1)
    lane_tok = lax.broadcasted_iota(jnp.int32, (nbp, ROWS), 1) & (n_tok - 1)
    rank = jnp.zeros((nbp, ROWS), jnp.float32)
    for t in range(n_tok):
        s_col = score[:, t:t + 1]
        s_row = jnp.sum(jnp.where(n_idx == m_idx, s_col, 0.0), axis=0, keepdims=True)
        s_row = jnp.where(m_idx[0:1] < n_sel_blocks, s_row, -2.0)
        ahead = (s_row > s_col) | ((s_row == s_col) & (m_idx < n_idx))
        rank_t = jnp.sum(ahead.astype(jnp.float32), axis=1, keepdims=True)
        rank = jnp.where(lane_tok == t, rank_t, rank)
    chosen = (rank < N_SEL) & causal & (blk < n_sel_blocks)
    pen_ref[...] = jnp.where(chosen, 0.0, NEG_INF)


def sample_cmp_select(qs, kc, bias, gate, past, n_sel_blocks):
    batch = qs.shape[0]
    n = kc.shape[3]
    nbp = -(-n_sel_blocks // SUBLANE) * SUBLANE
    return pl.pallas_call(
        functools.partial(_sample_cmp_body, n_sel_blocks=n_sel_blocks, past=past),
        grid=(batch, KV_HEADS),
        in_specs=[pl.BlockSpec((None, None, ROWS, HEAD_DIM), lambda b, g: (b, g, 0, 0)),
                  pl.BlockSpec((None, None, 2, n, HEAD_DIM), lambda b, g: (b, g, 0, 0, 0)),
                  pl.BlockSpec((None, n, ROWS), lambda b, g: (g, 0, 0)),
                  pl.BlockSpec((None, None, ROWS, HEAD_DIM), lambda b, g: (b, g, 0, 0))],
        out_specs=[pl.BlockSpec((None, None, ROWS, HEAD_DIM), lambda b, g: (b, g, 0, 0)),
                   pl.BlockSpec((None, None, nbp, ROWS), lambda b, g: (b, g, 0, 0))],
        out_shape=[jax.ShapeDtypeStruct((batch, KV_HEADS, ROWS, HEAD_DIM), jnp.float32),
                   jax.ShapeDtypeStruct((batch, KV_HEADS, nbp, ROWS), jnp.float32)],
        compiler_params=_cparams("parallel", "parallel"),
        name="sample_cmp_select",
    )(qs, kc, bias, gate)


ATTN_PAGES = 4


def _paged_attn_body(*refs, pen_block, gated, n_pages):
    it = iter(refs)
    pt_ref, tid_ref, q_ref = next(it), next(it), next(it)
    pages = [next(it) for _ in range(ATTN_PAGES)]
    new_ref, bt_ref = next(it), next(it)
    pen_ref = next(it) if pen_block else None
    gt_ref = next(it) if gated else None
    o_ref, m_ref, l_ref, acc_ref = next(it), next(it), next(it), next(it)
    step = pl.program_id(1)

    @pl.when(step == 0)
    def _():
        m_ref[...] = jnp.full(m_ref.shape, NEG_INF, jnp.float32)
        l_ref[...] = jnp.zeros(l_ref.shape, jnp.float32)
        acc_ref[...] = jnp.zeros(acc_ref.shape, jnp.float32)

    def page_terms(g, page, n_keys):
        tile = tid_ref[page]
        b = bt_ref[g, tile] if n_keys == PAGE else bt_ref[g, tile, 0:n_keys, :]
        if not pen_block:
            return b
        if pen_block >= PAGE:
            return b + pen_ref[g, pl.ds(page // (pen_block // PAGE), 1), :]
        per_page = PAGE // pen_block
        pieces = [b[a * pen_block:min((a + 1) * pen_block, n_keys)] + pen_ref[g, pl.ds(page * per_page + a, 1), :]
                  for a in range(-(-n_keys // pen_block))]
        return pieces[0] if len(pieces) == 1 else jnp.concatenate(pieces, axis=0)

    def attend(g, k, v, terms):
        s = _dot(_bf(k), _bf(q_ref[g] * ATTN_SCALE), NT_DIMS) + terms
        m_prev = m_ref[g]
        m_new = jnp.maximum(m_prev, jnp.max(s, axis=0, keepdims=True))
        alpha = jnp.exp(m_prev - m_new)
        p = jnp.exp(s - m_new)
        l_ref[g] = alpha * l_ref[g] + jnp.sum(p, axis=0, keepdims=True)
        acc_ref[g] = _rows_to_col(alpha, ROWS) * acc_ref[g] + _dot(_bf(p), _bf(v), TN_DIMS)
        m_ref[g] = m_new

    half = KV_HEADS * HEAD_DIM
    for kk in range(ATTN_PAGES):
        for g in range(KV_HEADS):
            attend(g, _page_slab(pages[kk], g), _page_slab(pages[kk], KV_HEADS + g),
                   page_terms(g, step * ATTN_PAGES + kk, PAGE))

    @pl.when(step == pl.num_programs(1) - 1)
    def _():
        n_new = new_ref.shape[0]
        for g in range(KV_HEADS):
            attend(g, new_ref[:, g * HEAD_DIM:(g + 1) * HEAD_DIM],
                   new_ref[:, half + g * HEAD_DIM:half + (g + 1) * HEAD_DIM], page_terms(g, n_pages, n_new))
            o = acc_ref[g] / _rows_to_col(jnp.maximum(l_ref[g], 1e-30), ROWS)
            if gated:
                o = o * jax.nn.sigmoid(gt_ref[g])
            o_ref[g] = o


def paged_attention(qs, pool, page_table, tile_ids, new_kv, bias_tiles, *, pen=None, pen_block=0, gate=None):
    batch, n_pages = page_table.shape
    steps = n_pages // ATTN_PAGES
    n_new = new_kv.shape[1]

    def page_spec(k):
        return pl.BlockSpec((None, PAGE_ROWS, HEAD_DIM), lambda b, s, pt, tid: (pt[b, s * ATTN_PAGES + k], 0, 0))

    grp = lambda b, s, pt, tid: (b, 0, 0, 0)
    in_specs = ([pl.BlockSpec((None, KV_HEADS, ROWS, HEAD_DIM), grp)] + [page_spec(k) for k in range(ATTN_PAGES)]
                + [pl.BlockSpec((None, n_new, KV_WIDTH), lambda b, s, pt, tid: (b, 0, 0)),
                   pl.BlockSpec(bias_tiles.shape, lambda b, s, pt, tid: (0, 0, 0, 0))])
    args = [qs] + [pool] * ATTN_PAGES + [new_kv, bias_tiles]
    if pen is not None:
        in_specs.append(pl.BlockSpec((None,) + pen.shape[1:], grp))
        args.append(pen)
    if gate is not None:
        in_specs.append(pl.BlockSpec((None, KV_HEADS, ROWS, HEAD_DIM), grp))
        args.append(gate)
    return pl.pallas_call(
        functools.partial(_paged_attn_body, pen_block=pen_block if pen is not None else 0, gated=gate is not None,
                          n_pages=n_pages),
        grid_spec=pltpu.PrefetchScalarGridSpec(
            num_scalar_prefetch=2, grid=(batch, steps), in_specs=in_specs,
            out_specs=pl.BlockSpec((None, KV_HEADS, ROWS, HEAD_DIM), grp),
            scratch_shapes=[pltpu.VMEM((KV_HEADS, 1, ROWS), jnp.float32), pltpu.VMEM((KV_HEADS, 1, ROWS), jnp.float32),
                            pltpu.VMEM((KV_HEADS, ROWS, HEAD_DIM), jnp.float32)]),
        out_shape=jax.ShapeDtypeStruct((batch, KV_HEADS, ROWS, HEAD_DIM), jnp.float32),
        compiler_params=_cparams("parallel", "arbitrary"),
        name="paged_attention",
    )(page_table, tile_ids, *args)


def _moba_sample_gate_body(pt_ref, q_ref, *refs, n_blocks, past):
    pages, (pen_ref, km_ref) = refs[:ATTN_PAGES], refs[ATTN_PAGES:]
    step = pl.program_id(1)
    half = KV_HEADS * HEAD_DIM
    per_block = MOBA_BLOCK // PAGE

    @pl.when(step == 0)
    def _():
        km_ref[...] = jnp.zeros(km_ref.shape, jnp.float32)

    for kk in range(ATTN_PAGES):
        blk = (step * ATTN_PAGES + kk) // per_block
        slab_sums = jnp.sum(pages[kk][...].reshape(PAGE, KV_SLABS, HEAD_DIM), axis=0)
        for g in range(KV_HEADS):
            km_ref[g, pl.ds(blk, 1), :] += slab_sums[g:g + 1]

    @pl.when(step == pl.num_programs(1) - 1)
    def _():
        nbp = pen_ref.shape[1]
        blk = lax.broadcasted_iota(jnp.int32, (nbp, ROWS), 0)
        n_tok = ROWS // GROUP
        tok = past + (lax.broadcasted_iota(jnp.int32, (nbp, ROWS), 1) & (n_tok - 1))
        own = lax.shift_right_logical(tok, int(math.log2(MOBA_BLOCK)))
        for g in range(KV_HEADS):
            kh, kl = _split_bf16(km_ref[g] / MOBA_BLOCK, 2)
            qh, ql = _split_bf16(q_ref[g], 2)
            gate = _dot(kh, qh, NT_DIMS) + _dot(kh, ql, NT_DIMS) + _dot(kl, qh, NT_DIMS)
            gate = jnp.where(blk < own, gate, NEG_INF)
            chosen = ((_rank_rows(gate, n_blocks) < MOBA_TOPK) & (blk < own)) | (blk == own)
            pen_ref[g] = jnp.where(chosen, 0.0, NEG_INF)


def moba_sample_gate(qs, pool, page_table, past, n_new):
    batch, n_pages = page_table.shape
    steps = n_pages // ATTN_PAGES
    n_blocks = -(-(past + n_new) // MOBA_BLOCK)
    nbp = -(-n_blocks // SUBLANE) * SUBLANE

    def page_spec(k):
        return pl.BlockSpec((None, PAGE_ROWS, HEAD_DIM), lambda b, s, pt: (pt[b, s * ATTN_PAGES + k], 0, 0))

    return pl.pallas_call(
        functools.partial(_moba_sample_gate_body, n_blocks=n_blocks, past=past),
        grid_spec=pltpu.PrefetchScalarGridSpec(
            num_scalar_prefetch=1, grid=(batch, steps),
            in_specs=[pl.BlockSpec((None, KV_HEADS, ROWS, HEAD_DIM), lambda b, s, pt: (b, 0, 0, 0))]
            + [page_spec(k) for k in range(ATTN_PAGES)],
            out_specs=pl.BlockSpec((None, KV_HEADS, nbp, ROWS), lambda b, s, pt: (b, 0, 0, 0)),
            scratch_shapes=[pltpu.VMEM((KV_HEADS, nbp, HEAD_DIM), jnp.float32)]),
        out_shape=jax.ShapeDtypeStruct((batch, KV_HEADS, nbp, ROWS), jnp.float32),
        compiler_params=_cparams("parallel", "arbitrary"),
        name="moba_sample_gate",
    )(page_table, qs, *([pool] * ATTN_PAGES))


def _sample_rows(x, batch, n_tok):
    return x.reshape(batch, n_tok, KV_HEADS, GROUP, HEAD_DIM).transpose(0, 2, 3, 1, 4).reshape(
        batch, KV_HEADS, GROUP * n_tok, HEAD_DIM)


def _sample_unrows(o, batch, n_tok):
    return o.reshape(batch, KV_HEADS, GROUP, n_tok, HEAD_DIM).transpose(0, 3, 1, 2, 4).reshape(batch * n_tok, Q_WIDTH)


def _sample_gate_rows(tail, branch, batch, n_tok):
    gt = tail[:, branch * N_HEADS:(branch + 1) * N_HEADS].reshape(batch, n_tok, KV_HEADS, GROUP)
    gt = gt.transpose(0, 2, 3, 1).reshape(batch, KV_HEADS, GROUP * n_tok, 1)
    return jnp.broadcast_to(gt, (batch, KV_HEADS, GROUP * n_tok, HEAD_DIM))


def _pad_new(kv_new, batch, n_tok):
    return jnp.pad(kv_new.reshape(batch, n_tok, KV_WIDTH), ((0, 0), (0, SUBLANE - n_tok), (0, 0)))


def _nsa_sample_pallas(main, tail, cache_c, cache_s, cache_w, page_table, cmp_w1, cmp_w2, cmp_pe, rel_table):
    batch, n_pages = page_table.shape
    n_tok = main.shape[0] // batch
    past = n_pages * PAGE
    wbuf = cache_w.shape[1]
    assert n_tok == ROWS // GROUP and cache_c.shape[1] == PAGE and wbuf == WINDOW and WINDOW % PAGE == 0
    n_cmp = (past + n_tok - CMP_BLOCK) // CMP_STRIDE + 1
    assert n_cmp + CMP_BLOCK // CMP_STRIDE - 1 == past // CMP_STRIDE
    qs = _sample_rows(main[:, :Q_WIDTH], batch, n_tok)
    kv_new = [main[:, Q_WIDTH + c * KV_WIDTH:Q_WIDTH + (c + 1) * KV_WIDTH] for c in range(3)]
    tiles = _sample_bias_tiles(rel_table, past, n_tok)
    flat = lambda pool: pool.reshape(pool.shape[0], PAGE_ROWS, HEAD_DIM)

    kc = compress_finish(compress_part_paged(flat(cache_c), page_table, cmp_w1), cmp_w1, cmp_w2, cmp_pe)
    n = kc.shape[3]
    dist = past + np.arange(n_tok)[None, :] - (np.arange(n)[:, None] * CMP_STRIDE + CMP_BLOCK - 1)
    ok = (dist >= 0) & (np.arange(n)[:, None] < n_cmp)
    cb = jnp.where(ok[..., None], _bias_lookup(rel_table, dist), NEG_INF)
    cb = cb.reshape(n, n_tok, KV_HEADS, GROUP).transpose(2, 0, 3, 1).reshape(KV_HEADS, n, GROUP * n_tok)
    n_sel_blocks = -(-(past + n_tok) // SEL_BLOCK)
    o_cmp, pen = sample_cmp_select(qs, kc, cb, _sample_gate_rows(tail, 0, batch, n_tok), past, n_sel_blocks)

    far_then_last = jnp.asarray([0] * (n_pages - 1) + [2, 3], jnp.int32)
    o_sel = paged_attention(qs, flat(cache_s), page_table, far_then_last, _pad_new(kv_new[1], batch, n_tok), tiles,
                            pen=pen, pen_block=SEL_BLOCK, gate=_sample_gate_rows(tail, 1, batch, n_tok))
    w_pages = wbuf // PAGE
    win_table = jnp.arange(batch * w_pages, dtype=jnp.int32).reshape(batch, w_pages)
    win_tiles = jnp.asarray([1] + [0] * (w_pages - 2) + [2, 3], jnp.int32)
    o_win = paged_attention(qs, cache_w.reshape(batch * w_pages, PAGE_ROWS, HEAD_DIM), win_table, win_tiles,
                            _pad_new(kv_new[2], batch, n_tok), tiles, gate=_sample_gate_rows(tail, 2, batch, n_tok))
    outs = [_sample_unrows(o, batch, n_tok) for o in (o_cmp, o_sel, o_win)]
    shape = (batch, n_tok, 2, KV_HEADS, HEAD_DIM)
    new_win = jnp.concatenate([cache_w[:, n_tok:], kv_new[2].reshape(shape)], axis=1)
    return outs, kv_new[0].reshape(shape), kv_new[1].reshape(shape), new_win


def _moba_sample_pallas(proj, cache_kv, page_table, rel_table):
    batch, n_pages = page_table.shape
    n_tok = proj.shape[0] // batch
    past = n_pages * PAGE
    assert n_tok == ROWS // GROUP and (past // MOBA_BLOCK) * MOBA_BLOCK == past
    qs = _sample_rows(proj[:, :Q_WIDTH], batch, n_tok)
    kv_new = proj[:, Q_WIDTH:]
    pool = cache_kv.reshape(cache_kv.shape[0], PAGE_ROWS, HEAD_DIM)
    pen = moba_sample_gate(qs, pool, page_table, past, n_tok)
    tile_ids = jnp.asarray([0] * (n_pages - 1) + [2, 3], jnp.int32)
    o = paged_attention(qs, pool, page_table, tile_ids, _pad_new(kv_new, batch, n_tok),
                        _sample_bias_tiles(rel_table, past, n_tok), pen=pen, pen_block=MOBA_BLOCK)
    return _sample_unrows(o, batch, n_tok), kv_new.reshape(batch, n_tok, 2, KV_HEADS, HEAD_DIM)


def _cols_to_row(col, n):
    eye = lax.broadcasted_iota(jnp.int32, (n, n), 0) == lax.broadcasted_iota(jnp.int32, (n, n), 1)
    return jnp.sum(jnp.where(eye, jnp.broadcast_to(col, (n, n)), 0.0), axis=0, keepdims=True)


def _hgrn2_step_body(q_ref, f_ref, i_ref, g_ref, lbl_ref, ng_ref, s0_ref, o_ref, s_out_ref, *, layer):
    n_tok = q_ref.shape[0]
    lbl = lbl_ref[...]
    e = jnp.exp(lbl - jnp.max(lbl, axis=0, keepdims=True))
    p = e / jnp.sum(e, axis=0, keepdims=True)
    lb = jnp.zeros((1, HG_DK), jnp.float32)
    for r in range(1, layer + 1):
        lb = lb + p[r:r + 1]
    q = q_ref[...]
    qh = q * jax.nn.sigmoid(q) * HG_DK ** -0.5
    fg = lb + (1.0 - lb) * jax.nn.sigmoid(f_ref[...])
    k = 1.0 - fg
    v = i_ref[...]
    st = s0_ref[...].T
    rows = []
    for t in range(n_tok):
        st = st * fg[t:t + 1] + _rows_to_col(v[t:t + 1], HG_DV) * k[t:t + 1]
        rows.append(_cols_to_row(jnp.sum(st * qh[t:t + 1], axis=1, keepdims=True), HG_DV))
    o_ref[...] = _head_rms_gate(jnp.concatenate(rows, axis=0), ng_ref[...], g_ref[...])
    s_out_ref[...] = st.T


def hgrn2_step(proj, lb_logits, norm_g, s0, layer):
    batch, n_tok, _ = proj.shape
    h = HG_HEADS

    def col(k):
        return pl.BlockSpec((None, n_tok, HG_DK), lambda b, hh: (b, 0, k * h + hh))

    return pl.pallas_call(
        functools.partial(_hgrn2_step_body, layer=layer),
        grid=(batch, h),
        in_specs=[col(0), col(1), col(2), col(3),
                  pl.BlockSpec((DEPTH, HG_DK), lambda b, hh: (0, hh)),
                  pl.BlockSpec((1, HG_DV), lambda b, hh: (0, 0)),
                  pl.BlockSpec((None, None, HG_DK, HG_DV), lambda b, hh: (b, hh, 0, 0))],
        out_specs=[pl.BlockSpec((None, n_tok, HG_DV), lambda b, hh: (b, 0, hh)),
                   pl.BlockSpec((None, None, HG_DK, HG_DV), lambda b, hh: (b, hh, 0, 0))],
        out_shape=[jax.ShapeDtypeStruct((batch, n_tok, h * HG_DV), jnp.float32),
                   jax.ShapeDtypeStruct((batch, h, HG_DK, HG_DV), jnp.float32)],
        compiler_params=_cparams("parallel", "parallel"),
        name="hgrn2_step",
    )(proj, proj, proj, proj, lb_logits, norm_g.reshape(1, HG_DV), s0)


def _gdn_step_body(q_ref, k_ref, v_ref, z_ref, t_ref, bq_ref, bk_ref, bv_ref, wq_ref, wk_ref, wv_ref,
                   al_ref, dtb_ref, ng_ref, s0_ref, o_ref, s_out_ref, xs_ref):
    hq = pl.program_id(1)
    n_tok = q_ref.shape[0]
    dk, dv = GDN_DK, GDN_DV
    pad = SUBLANE
    xs_ref[0:pad, :] = jnp.concatenate([bq_ref[...], bk_ref[...], bv_ref[...]], axis=1)
    x = jnp.concatenate([q_ref[...], k_ref[...], v_ref[...]], axis=1)
    xs_ref[pad:pad + n_tok, :] = x
    cw = jnp.concatenate([wq_ref[...], wk_ref[...], wv_ref[...]], axis=1)
    y = xs_ref[pad - 3:pad - 3 + n_tok, :] * cw[0:1]
    for i in range(1, GDN_CONV - 1):
        y = y + xs_ref[pad - 3 + i:pad - 3 + i + n_tok, :] * cw[i:i + 1]
    y = y + x * cw[GDN_CONV - 1:GDN_CONV]
    y = y * jax.nn.sigmoid(y)
    q = _l2n(y[:, 0:dk]) * dk ** -0.5
    k = _l2n(y[:, dk:2 * dk])
    tl = t_ref[...]
    beta_all = jax.nn.sigmoid(tl)
    la_all = -jnp.exp(al_ref[...]) * _softplus(tl + dtb_ref[...])
    for e in range(GDN_REP):
        hv = hq * GDN_REP + e
        v = y[:, 2 * dk + e * dv:2 * dk + (e + 1) * dv]
        bt = _lane_column(beta_all, hv)
        a = jnp.exp(_lane_column(la_all, GDN_V_HEADS + hv))
        s = s0_ref[e]
        rows = []
        for t in range(n_tok):
            k_col = _rows_to_col(k[t:t + 1], dk)
            ks = jnp.sum(s * k_col, axis=0, keepdims=True)
            u = bt[t:t + 1] * (v[t:t + 1] - a[t:t + 1] * ks)
            s = a[t:t + 1] * s + k_col * u
            rows.append(jnp.sum(s * _rows_to_col(q[t:t + 1], dk), axis=0, keepdims=True))
        s_out_ref[e] = s
        o_ref[:, e * dv:(e + 1) * dv] = _head_rms_gate(jnp.concatenate(rows, axis=0), ng_ref[...],
                                                       z_ref[:, e * dv:(e + 1) * dv])


def gdn_step(main, tail, conv_buf, conv_w, a_log, dt_bias, norm_g, s0):
    batch, n_tok, _ = main.shape
    hq, rep, dk, dv = GDN_QK_HEADS, GDN_REP, GDN_DK, GDN_DV
    vw = rep * dv
    buf = jnp.pad(conv_buf, ((0, 0), (SUBLANE - (GDN_CONV - 1), 0), (0, 0)))
    pad_lanes = jnp.zeros((LANE - 2 * GDN_V_HEADS,), jnp.float32)
    a_row = jnp.concatenate([jnp.zeros((GDN_V_HEADS,), jnp.float32), a_log, pad_lanes]).reshape(1, LANE)
    dt_row = jnp.concatenate([jnp.zeros((GDN_V_HEADS,), jnp.float32), dt_bias, pad_lanes]).reshape(1, LANE)
    k0 = hq
    v0 = 2 * hq * dk // vw
    z0 = GDN_CONV_DIM // vw
    return pl.pallas_call(
        _gdn_step_body,
        grid=(batch, hq),
        in_specs=[pl.BlockSpec((None, n_tok, dk), lambda b, h: (b, 0, h)),
                  pl.BlockSpec((None, n_tok, dk), lambda b, h: (b, 0, k0 + h)),
                  pl.BlockSpec((None, n_tok, vw), lambda b, h: (b, 0, v0 + h)),
                  pl.BlockSpec((None, n_tok, vw), lambda b, h: (b, 0, z0 + h)),
                  pl.BlockSpec((None, n_tok, LANE), lambda b, h: (b, 0, 0)),
                  pl.BlockSpec((None, SUBLANE, dk), lambda b, h: (b, 0, h)),
                  pl.BlockSpec((None, SUBLANE, dk), lambda b, h: (b, 0, k0 + h)),
                  pl.BlockSpec((None, SUBLANE, vw), lambda b, h: (b, 0, v0 + h)),
                  pl.BlockSpec((GDN_CONV, dk), lambda b, h: (0, h)),
                  pl.BlockSpec((GDN_CONV, dk), lambda b, h: (0, k0 + h)),
                  pl.BlockSpec((GDN_CONV, vw), lambda b, h: (0, v0 + h)),
                  pl.BlockSpec((1, LANE), lambda b, h: (0, 0)),
                  pl.BlockSpec((1, LANE), lambda b, h: (0, 0)),
                  pl.BlockSpec((1, dv), lambda b, h: (0, 0)),
                  pl.BlockSpec((None, rep, dk, dv), lambda b, h: (b, h, 0, 0))],
        out_specs=[pl.BlockSpec((None, n_tok, vw), lambda b, h: (b, 0, h)),
                   pl.BlockSpec((None, rep, dk, dv), lambda b, h: (b, h, 0, 0))],
        out_shape=[jax.ShapeDtypeStruct((batch, n_tok, GDN_V_HEADS * dv), jnp.float32),
                   jax.ShapeDtypeStruct((batch, GDN_V_HEADS, dk, dv), jnp.float32)],
        scratch_shapes=[pltpu.VMEM((2 * SUBLANE, 2 * dk + vw), jnp.float32)],
        compiler_params=_cparams("parallel", "parallel"),
        name="gdn_step",
    )(main, main, main, main, tail, buf, buf, buf, conv_w, conv_w, conv_w, a_row, dt_row, norm_g.reshape(1, dv), s0)


def _pad_cols(w, mult=LANE):
    n = w.shape[1]
    return jnp.pad(w, ((0, 0), (0, (-n) % mult)))


def _nsa_prompt(main, kv_rows, tail, batch, seq, cmp_w1, cmp_w2, cmp_pe, rel_table):
    kv_c, kv_s, kv_w = (kv_rows[c].reshape(batch, seq, 2, KV_HEADS, HEAD_DIM) for c in range(3))
    kc = compress_finish(compress_part_rows(main, Q_WIDTH // KV_WIDTH, cmp_w1, batch, seq), cmp_w1, cmp_w2, cmp_pe)
    o_cmp, pen = cmp_select(main, tail, kc, _cmp_bias_table(rel_table, kc.shape[3]), batch, seq)
    col = Q_WIDTH // HEAD_DIM
    o_sel = flash_attention(main, main, col + 2 * KV_HEADS, col + 3 * KV_HEADS, _flash_bias_tiles(rel_table, 0),
                            batch, seq, pen=pen, pen_block=SEL_BLOCK, gate_arr=tail, gate_col0=N_HEADS)
    o_win = flash_attention(main, main, col + 4 * KV_HEADS, col + 5 * KV_HEADS,
                            _flash_bias_tiles(rel_table, WINDOW), batch, seq, k_back=WINDOW // ATTN_TILE,
                            gate_arr=tail, gate_col0=2 * N_HEADS)
    return [o_cmp, o_sel, o_win], kv_c, kv_s, kv_w[:, -min(WINDOW, seq):]


def _moba_prompt(proj, batch, seq, rel_table):
    col = Q_WIDTH // HEAD_DIM
    pen = moba_gate(proj, col, batch, seq)
    return flash_attention(proj, proj, col, col + KV_HEADS, _flash_bias_tiles(rel_table, 0), batch, seq,
                           pen=pen, pen_block=MOBA_BLOCK)


def kernel(x_prompt, x_sample, cache_nsa_cmp_kv, cache_nsa_sel_kv, cache_nsa_win_kv, cache_moba_kv,
           state_hgrn2, state_gdn_conv, state_gdn_ssm, page_table, rel_table, ln_mix, ln_ffn, ln_final,
           ffn_w_up, ffn_w_down, nsa_w_in, nsa_cmp_w1, nsa_cmp_w2, nsa_cmp_pe, nsa_w_out, moba_w_in, moba_w_out,
           hg_w_in, hg_lb_logits, hg_norm, hg_w_out, gdn_w_in, gdn_conv_w, gdn_a_log, gdn_dt_bias, gdn_norm,
           gdn_w_out):
    bf = jnp.bfloat16
    bp, tp = x_prompt.shape[:2]
    bs, ts = x_sample.shape[:2]
    assert tp % ATTN_TILE == 0 and WINDOW % ATTN_TILE == 0 and ATTN_TILE == MOBA_BLOCK and tp % SCAN_TILE == 0
    xp = x_prompt.reshape(bp * tp, D_MODEL)
    xs = x_sample.reshape(bs * ts, D_MODEL)
    w_up, w_down = ffn_w_up.astype(bf), ffn_w_down.astype(bf)

    for layer in range(DEPTH):
        kind = layer % N_MIXERS
        g_mix = ln_mix[layer]
        if kind == 0:
            w_main, w_tail = nsa_w_in.astype(bf), _pad_cols(nsa_w_in[:, NSA_MAIN:]).astype(bf)
            main_p, kv_p = norm_matmul(xp, g_mix, w_main, kv_from=Q_WIDTH // KV_WIDTH, n_cols=NSA_MAIN)
            tail_p = norm_matmul(xp, g_mix, w_tail)
            main_s, tail_s = norm_matmul(xs, g_mix, w_main, n_cols=NSA_MAIN), norm_matmul(xs, g_mix, w_tail)
            op, nsa_cmp_p, nsa_sel_p, nsa_win_p = _nsa_prompt(main_p, kv_p, tail_p, bp, tp, nsa_cmp_w1, nsa_cmp_w2,
                                                              nsa_cmp_pe, rel_table)
            os_, nsa_cmp_s, nsa_sel_s, nsa_win_s = _nsa_sample_pallas(main_s, tail_s, cache_nsa_cmp_kv,
                                                                     cache_nsa_sel_kv, cache_nsa_win_kv, page_table,
                                                                     nsa_cmp_w1, nsa_cmp_w2, nsa_cmp_pe, rel_table)
            w_out = nsa_w_out.astype(bf)
        elif kind == 1:
            w_in = moba_w_in.astype(bf)
            pp, kv_p = norm_matmul(xp, g_mix, w_in, kv_from=Q_WIDTH // KV_WIDTH)
            op, moba_p = _moba_prompt(pp, bp, tp, rel_table), kv_p.reshape(bp, tp, 2, KV_HEADS, HEAD_DIM)
            os_, moba_s = _moba_sample_pallas(norm_matmul(xs, g_mix, w_in), cache_moba_kv, page_table, rel_table)
            op, os_ = [op], [os_]
            w_out = moba_w_out.astype(bf)
        elif kind == 2:
            w_in = hg_w_in.astype(bf)
            pp = norm_matmul(xp, g_mix, w_in)
            ps = norm_matmul(xs, g_mix, w_in).reshape(bs, ts, -1)
            s0 = jnp.zeros((bp, HG_HEADS, HG_DK, HG_DV), jnp.float32)
            op, hg_p = hgrn2_scan(pp, hg_lb_logits, hg_norm, s0, layer, bp, tp)
            os_, hg_s = hgrn2_step(ps, hg_lb_logits, hg_norm, state_hgrn2, layer)
            op, os_ = [op], [os_.reshape(bs * ts, -1)]
            w_out = hg_w_out.astype(bf)
        else:
            w_main, w_tail = gdn_w_in.astype(bf), _pad_cols(gdn_w_in[:, GDN_MAIN:]).astype(bf)
            main_p, tail_p = norm_matmul(xp, g_mix, w_main, n_cols=GDN_MAIN), norm_matmul(xp, g_mix, w_tail)
            main_s = norm_matmul(xs, g_mix, w_main, n_cols=GDN_MAIN).reshape(bs, ts, -1)
            tail_s = norm_matmul(xs, g_mix, w_tail).reshape(bs, ts, -1)
            buf0 = jnp.zeros((bp, GDN_CONV - 1, GDN_CONV_DIM), jnp.float32)
            s0 = jnp.zeros((bp, GDN_V_HEADS, GDN_DK, GDN_DV), jnp.float32)
            op, ssm_p = gdn_scan(main_p, tail_p, buf0, gdn_conv_w, gdn_a_log, gdn_dt_bias, gdn_norm, s0, bp, tp)
            conv_p = main_p.reshape(bp, tp, -1)[:, tp - (GDN_CONV - 1):, :GDN_CONV_DIM]
            os_, ssm_s = gdn_step(main_s, tail_s, state_gdn_conv, gdn_conv_w, gdn_a_log, gdn_dt_bias, gdn_norm,
                                  state_gdn_ssm)
            conv_s = jnp.concatenate([state_gdn_conv, main_s[:, :, :GDN_CONV_DIM]], axis=1)[:, ts:]
            op, os_ = [op], [os_.reshape(bs * ts, -1)]
            w_out = gdn_w_out.astype(bf)
        xp = matmul_res(op, w_out, xp)
        xs = matmul_res(os_, w_out, xs)
        xp = ffn(xp, ln_ffn[layer], w_up, w_down, layer)
        xs = ffn(xs, ln_ffn[layer], w_up, w_down, layer)
    y_prompt = final_norm(xp, ln_final).reshape(bp, tp, D_MODEL)
    y_sample = final_norm(xs, ln_final).reshape(bs, ts, D_MODEL)
    return (y_prompt, y_sample, nsa_cmp_p, nsa_cmp_s, nsa_sel_p, nsa_sel_s, nsa_win_p, nsa_win_s,
            moba_p, moba_s, hg_p, hg_s, conv_p, conv_s, ssm_p, ssm_s)
```

```python
import functools
import math

import jax
import jax.numpy as jnp
import numpy as np
from jax import lax
from jax.experimental import pallas as pl
from jax.experimental.pallas import tpu as pltpu

D_MODEL = 2048
DEPTH = 4
N_MIXERS = 4
HEAD_DIM = 128
N_HEADS = D_MODEL // HEAD_DIM
KV_HEADS = 4
GROUP = N_HEADS // KV_HEADS
ATTN_SCALE = HEAD_DIM ** -0.5
REL_BUCKETS = 32
REL_MAX_DIST = 128
CMP_BLOCK = 32
CMP_STRIDE = 16
CMP_HIDDEN = HEAD_DIM
SEL_BLOCK = 64
N_SEL = 16
WINDOW = 512
FORCE_SCORE = 1.0e4
MOBA_BLOCK = 256
MOBA_TOPK = 3
HG_DK = 128
HG_HEADS = D_MODEL // HG_DK
HG_DV = D_MODEL // HG_HEADS
GDN_DK = 128
GDN_DV = 128
GDN_QK_HEADS = D_MODEL // GDN_DK
GDN_V_HEADS = 2 * GDN_QK_HEADS
GDN_REP = GDN_V_HEADS // GDN_QK_HEADS
GDN_CONV = 4
GDN_CONV_DIM = 2 * GDN_QK_HEADS * GDN_DK + GDN_V_HEADS * GDN_DV
CHUNK = 64
NEG_INF = -1.0e30
NORM_EPS = 1e-6

Q_WIDTH = N_HEADS * HEAD_DIM
KV_WIDTH = 2 * KV_HEADS * HEAD_DIM
NSA_MAIN = Q_WIDTH + 3 * KV_WIDTH
GDN_MAIN = GDN_CONV_DIM + GDN_V_HEADS * GDN_DV

V7X_VMEM_LIMIT_BYTES = 56 * 1024 * 1024
LANE = 128
SUBLANE = 8
ATTN_TILE = 256
CMP_TILE = 128
SCAN_TILE = 512
GDN_HPS = 2
LOG2E = math.log2(math.e)
NT_DIMS = (((1,), (1,)), ((), ()))
TN_DIMS = (((0,), (0,)), ((), ()))


def _cparams(*sem):
    return pltpu.CompilerParams(dimension_semantics=sem, vmem_limit_bytes=V7X_VMEM_LIMIT_BYTES)


def _row_tile(m, target):
    t = min(m, target)
    while m % t:
        t //= 2
    return t


def _col_tile(n, target):
    t = min(n, target)
    while n % t or t % LANE:
        t -= LANE
    return t


def _split_bf16(x, parts):
    out = []
    for _ in range(parts - 1):
        hi = x.astype(jnp.bfloat16)
        out.append(hi)
        x = x - hi.astype(jnp.float32)
    out.append(x.astype(jnp.bfloat16))
    return out


def _bf(x):
    return x.astype(jnp.bfloat16)


def _dot(a, b, dims=None):
    if dims is None:
        return jnp.dot(a, b, preferred_element_type=jnp.float32)
    return lax.dot_general(a, b, dims, preferred_element_type=jnp.float32)


def _norm_matmul_body(x_ref, g_ref, w_ref, o_ref, *rest, kv_from):
    kv_ref, h_ref = rest if kv_from is not None else (None, rest[0])
    j = pl.program_id(1)

    @pl.when(j == 0)
    def _():
        x = x_ref[...]
        ms = jnp.mean(x * x, axis=-1, keepdims=True)
        h_ref[...] = _bf(x * lax.rsqrt(ms + NORM_EPS) * g_ref[...])

    out = _dot(h_ref[...], w_ref[...])
    o_ref[...] = out
    if kv_from is not None:
        @pl.when(j >= kv_from)
        def _():
            tm = out.shape[0]
            for slab in range(out.shape[1] // HEAD_DIM):
                kv_ref[pl.ds(slab, tm, stride=out.shape[1] // HEAD_DIM), :] = out[:, slab * HEAD_DIM:(slab + 1) * HEAD_DIM]


def norm_matmul(x, g, w, kv_from=None, n_cols=None):
    m, k = x.shape
    n = n_cols or w.shape[1]
    tm = _row_tile(m, 1024)
    tn = _col_tile(n, 1024)
    out_specs = [pl.BlockSpec((tm, tn), lambda i, j: (i, j))]
    out_shape = [jax.ShapeDtypeStruct((m, n), jnp.float32)]
    if kv_from is not None:
        assert tn == KV_WIDTH
        slabs = tn // HEAD_DIM
        out_specs.append(pl.BlockSpec((None, tm * slabs, HEAD_DIM), lambda i, j: (jnp.maximum(j - kv_from, 0), i, 0)))
        out_shape.append(jax.ShapeDtypeStruct((n // tn - kv_from, m * slabs, HEAD_DIM), jnp.float32))
    res = pl.pallas_call(
        functools.partial(_norm_matmul_body, kv_from=kv_from),
        grid=(m // tm, n // tn),
        in_specs=[pl.BlockSpec((tm, k), lambda i, j: (i, 0)),
                  pl.BlockSpec((1, k), lambda i, j: (0, 0)),
                  pl.BlockSpec((k, tn), lambda i, j: (0, j))],
        out_specs=out_specs,
        out_shape=out_shape,
        scratch_shapes=[pltpu.VMEM((tm, k), jnp.bfloat16)],
        compiler_params=_cparams("parallel", "arbitrary"),
        name="norm_matmul",
    )(x, g.reshape(1, k), w)
    return res if kv_from is not None else res[0]


def _matmul_res_body(*refs):
    *a_refs, w_ref, r_ref, o_ref = refs
    a = a_refs[0][...]
    for a_ref in a_refs[1:]:
        a = a + a_ref[...]
    o_ref[...] = r_ref[...] + _dot(_bf(a), w_ref[...])


def matmul_res(a_list, w, res):
    m, k = a_list[0].shape
    n = w.shape[1]
    tm = _row_tile(m, 512)
    tn = _col_tile(n, 1024)
    return pl.pallas_call(
        _matmul_res_body,
        grid=(m // tm, n // tn),
        in_specs=[pl.BlockSpec((tm, k), lambda i, j: (i, 0)) for _ in a_list]
        + [pl.BlockSpec((k, tn), lambda i, j: (0, j)),
           pl.BlockSpec((tm, tn), lambda i, j: (i, j))],
        out_specs=pl.BlockSpec((tm, tn), lambda i, j: (i, j)),
        out_shape=jax.ShapeDtypeStruct((m, n), jnp.float32),
        compiler_params=_cparams("parallel", "arbitrary"),
        name="matmul_res",
    )(*a_list, w, res)


def _ffn_body(x_ref, g_ref, wa_ref, wb_ref, wd_ref, o_ref, h_ref):
    @pl.when(pl.program_id(1) == 0)
    def _():
        x = x_ref[...]
        ms = jnp.mean(x * x, axis=-1, keepdims=True)
        h_ref[...] = _bf(x * lax.rsqrt(ms + NORM_EPS) * g_ref[...])
        o_ref[...] = x

    h = h_ref[...]
    a = _dot(h, wa_ref[...])
    b = _dot(h, wb_ref[...])
    o_ref[...] += _dot(_bf(a * jax.nn.sigmoid(a) * b), wd_ref[...])


def ffn(x, g, w_up, w_down, layer):
    m, k = x.shape
    hdim = w_down.shape[1]
    tm = _row_tile(m, 1024)
    th = _col_tile(hdim, 512)
    nh = hdim // th
    return pl.pallas_call(
        _ffn_body,
        grid=(m // tm, nh),
        in_specs=[pl.BlockSpec((tm, k), lambda i, j: (i, 0)),
                  pl.BlockSpec((1, k), lambda i, j: (0, 0)),
                  pl.BlockSpec((None, k, th), lambda i, j: (layer, 0, j)),
                  pl.BlockSpec((None, k, th), lambda i, j: (layer, 0, j + nh)),
                  pl.BlockSpec((None, th, k), lambda i, j: (layer, j, 0))],
        out_specs=pl.BlockSpec((tm, k), lambda i, j: (i, 0)),
        out_shape=jax.ShapeDtypeStruct((m, k), jnp.float32),
        scratch_shapes=[pltpu.VMEM((tm, k), jnp.bfloat16)],
        compiler_params=_cparams("parallel", "arbitrary"),
        name="ffn",
    )(x, g.reshape(1, k), w_up, w_up, w_down)


def _norm_body(x_ref, g_ref, o_ref):
    x = x_ref[...]
    ms = jnp.mean(x * x, axis=-1, keepdims=True)
    o_ref[...] = x * lax.rsqrt(ms + NORM_EPS) * g_ref[...]


def final_norm(x, g):
    m, k = x.shape
    tm = _row_tile(m, 512)
    return pl.pallas_call(
        _norm_body,
        grid=(m // tm,),
        in_specs=[pl.BlockSpec((tm, k), lambda i: (i, 0)), pl.BlockSpec((1, k), lambda i: (0, 0))],
        out_specs=pl.BlockSpec((tm, k), lambda i: (i, 0)),
        out_shape=jax.ShapeDtypeStruct((m, k), jnp.float32),
        compiler_params=_cparams("parallel"),
        name="final_norm",
    )(x, g.reshape(1, k))


def _bucket_np(dist):
    exact = REL_BUCKETS // 2
    d = np.maximum(dist, 0)
    ratio = np.log(np.maximum(d, 1).astype(np.float32) / exact) / math.log(REL_MAX_DIST / exact)
    large = np.minimum(exact + (ratio * (REL_BUCKETS - exact)).astype(np.int32), REL_BUCKETS - 1)
    return np.where(d < exact, d, large)


def _bias_lookup(rel_table, dist):
    bucket = _bucket_np(dist).astype(np.int32)
    ids = [int(b) for b in np.unique(bucket)]
    bk = jnp.asarray(bucket)[..., None]
    out = jnp.broadcast_to(rel_table[ids[0]], bucket.shape + (rel_table.shape[1],))
    for b in ids[1:]:
        out = jnp.where(bk == b, rel_table[b], out)
    return out


def _heads_to_lanes(t):
    keys, queries, _ = t.shape
    return t.reshape(keys, queries, KV_HEADS, GROUP).transpose(2, 0, 3, 1).reshape(KV_HEADS, keys, GROUP * queries)


def _flash_bias_tiles(rel_table, window):
    j = np.arange(ATTN_TILE)[:, None]
    i = np.arange(ATTN_TILE)[None, :]
    n_cls = window // ATTN_TILE + 1 if window else -(-REL_MAX_DIST // ATTN_TILE) + 2
    tiles = []
    for d in range(n_cls):
        dist = d * ATTN_TILE + i - j
        ok = dist >= 0
        if window:
            ok = ok & (dist < window)
        tiles.append(_heads_to_lanes(jnp.where(ok[..., None], _bias_lookup(rel_table, dist) * LOG2E, NEG_INF)))
    return jnp.stack(tiles, axis=1)


def _cmp_bias_table(rel_table, ncp):
    x = np.arange(ncp)[:, None]
    i = np.arange(CMP_TILE)[None, :]
    dist = i - CMP_STRIDE * (x - 16) - (CMP_BLOCK - 1)
    far = rel_table[REL_BUCKETS - 1]
    b = _heads_to_lanes(jnp.where((dist >= 0)[..., None], _bias_lookup(rel_table, dist), far))
    return jnp.concatenate([b, b], axis=1)


def _stack_heads(q):
    return jnp.concatenate([q[:, r * HEAD_DIM:(r + 1) * HEAD_DIM] for r in range(GROUP)], axis=0)


def _gate_columns(gt_ref, col0):
    gt = jax.nn.sigmoid(gt_ref[...])
    lane = lax.broadcasted_iota(jnp.int32, gt.shape, 1)
    return [jnp.sum(jnp.where(lane == col0 + r, gt, 0.0), axis=1, keepdims=True) for r in range(GROUP)]


def _heads_from_lanes(o_t, rows, cols=None):
    parts = []
    for r in range(GROUP):
        part = o_t[:, r * rows:(r + 1) * rows].T
        if cols is not None:
            part = part * cols[r]
        parts.append(part)
    return jnp.concatenate(parts, axis=1)


def _rank_rows(score, n_rows):
    row = lax.broadcasted_iota(jnp.int32, score.shape, 0)
    rank = jnp.zeros(score.shape, jnp.int32)
    for mm in range(n_rows):
        sm = score[mm:mm + 1, :]
        ahead = (sm > score) | ((sm == score) & (row > mm))
        rank = rank + ahead.astype(jnp.int32)
    return rank


def _cmp_select_body(q_ref, kc_ref, dt_ref, gt_ref, o_ref, pen_ref, *, n_sel_blocks):
    g = pl.program_id(1)
    qi = pl.program_id(2)
    tq = CMP_TILE
    cols = GROUP * tq
    ncp = kc_ref.shape[1]
    q4 = _bf(_stack_heads(q_ref[...]) * ATTN_SCALE)
    s = _dot(_bf(kc_ref[0]), q4, NT_DIMS)
    shift = (qi * (tq // CMP_STRIDE) + ncp - 16) % ncp
    bias = dt_ref[pl.ds(pl.multiple_of(ncp - shift, SUBLANE), ncp), :]
    t_col = qi * tq + (lax.broadcasted_iota(jnp.int32, (ncp, cols), 1) & (tq - 1))
    end_pos = lax.broadcasted_iota(jnp.int32, (ncp, cols), 0) * CMP_STRIDE + (CMP_BLOCK - 1)
    mask = t_col >= end_pos
    s = jnp.where(mask, s + bias, NEG_INF)
    m = jnp.max(s, axis=0, keepdims=True)
    e = jnp.where(mask, jnp.exp(s - m), 0.0)
    p = e / jnp.maximum(jnp.sum(e, axis=0, keepdims=True), 1e-30)
    o_t = _dot(_bf(kc_ref[1].T), _bf(p))
    o_ref[...] = _heads_from_lanes(o_t, tq, _gate_columns(gt_ref, g * GROUP))

    imp = p[:, 0:tq]
    for r in range(1, GROUP):
        imp = imp + p[:, r * tq:(r + 1) * tq]
    ratio = SEL_BLOCK // CMP_STRIDE
    j_i = lax.broadcasted_iota(jnp.int32, (n_sel_blocks, ncp), 0)
    c_i = lax.broadcasted_iota(jnp.int32, (n_sel_blocks, ncp), 1)
    w = _bf((c_i >= ratio * j_i - 1) & (c_i <= ratio * j_i + ratio - 1))
    score = sum(_dot(w, part) for part in _split_bf16(imp, 3))
    blk = lax.broadcasted_iota(jnp.int32, (n_sel_blocks, tq), 0)
    tok = qi * tq + lax.broadcasted_iota(jnp.int32, (n_sel_blocks, tq), 1)
    cur = lax.shift_right_logical(tok, int(math.log2(SEL_BLOCK)))
    forced = (blk == 0) | (blk == cur) | (blk == cur - 1)
    causal = blk <= cur
    score = jnp.where(forced, FORCE_SCORE, score)
    score = jnp.where(causal, score, -1.0)
    chosen = (_rank_rows(score, n_sel_blocks) < N_SEL) & causal
    pen_ref[...] = jnp.where(chosen, 0.0, NEG_INF)


def cmp_select(proj, tail, kc, dt, batch, seq):
    tq = CMP_TILE
    nq = seq // tq
    ncp = kc.shape[3]
    nsb = seq // SEL_BLOCK
    return pl.pallas_call(
        functools.partial(_cmp_select_body, n_sel_blocks=nsb),
        grid=(batch, KV_HEADS, nq),
        in_specs=[pl.BlockSpec((tq, GROUP * HEAD_DIM), lambda b, g, i: (b * nq + i, g)),
                  pl.BlockSpec((None, None, 2, ncp, HEAD_DIM), lambda b, g, i: (b, g, 0, 0, 0)),
                  pl.BlockSpec((None, 2 * ncp, GROUP * tq), lambda b, g, i: (g, 0, 0)),
                  pl.BlockSpec((tq, LANE), lambda b, g, i: (b * nq + i, 0))],
        out_specs=[pl.BlockSpec((tq, GROUP * HEAD_DIM), lambda b, g, i: (b * nq + i, g)),
                   pl.BlockSpec((None, None, nsb, tq), lambda b, g, i: (b, g, 0, i))],
        out_shape=[jax.ShapeDtypeStruct((batch * seq, Q_WIDTH), jnp.float32),
                   jax.ShapeDtypeStruct((batch, KV_HEADS, nsb, seq), jnp.float32)],
        compiler_params=_cparams("parallel", "parallel", "arbitrary"),
        name="cmp_select",
    )(proj, kc, dt, tail)


def _flash_body(*refs, pen_block, pen_per_head, k_back, gate_col0, seq):
    it = iter(refs)
    q_ref, k_ref, v_ref, bt_ref = next(it), next(it), next(it), next(it)
    pen_ref = next(it) if pen_block else None
    gt_ref = next(it) if gate_col0 is not None else None
    o_ref, m_ref, acc_ref, qa_ref, kb_ref, vt_ref, sa_ref, sb_ref = (next(it) for _ in range(8))
    g = pl.program_id(1)
    qi = pl.program_id(2)
    tq = tk = ATTN_TILE
    n_cls = bt_ref.shape[0]

    @pl.when(qi == 0)
    def _():
        vt_ref[HEAD_DIM:, :] = jnp.ones((SUBLANE, seq), jnp.bfloat16)
        for c in range(seq // tk):
            kb_ref[c * tk:(c + 1) * tk, :] = _bf(k_ref[c * tk:(c + 1) * tk, :])
            vt_ref[0:HEAD_DIM, c * tk:(c + 1) * tk] = _bf(v_ref[c * tk:(c + 1) * tk, :].T)

    qa_ref[...] = _bf(_stack_heads(q_ref[...]) * (ATTN_SCALE * LOG2E))
    m_ref[...] = jnp.full(m_ref.shape, NEG_INF, jnp.float32)
    acc_ref[...] = jnp.zeros(acc_ref.shape, jnp.float32)

    def raw_logits(kj):
        k0 = pl.multiple_of(jnp.minimum(kj, qi) * tk, tk)
        return _dot(kb_ref[pl.ds(k0, tk), :], qa_ref[...], NT_DIMS)

    def attend(s_ref, kj):
        kc = jnp.minimum(kj, qi)
        k0 = pl.multiple_of(kc * tk, tk)
        skip = jnp.where(kj <= qi, 0.0, NEG_INF)
        cls = jnp.minimum(qi - kc, n_cls - 1)
        vt = vt_ref[:, pl.ds(k0, tk)]
        per_tile = tk // pen_block if pen_block else 1
        blk = tk // per_tile
        for r in range(GROUP):
            cs = slice(r * tq, (r + 1) * tq)
            pieces = []
            for a in range(per_tile):
                rows = slice(a * blk, (a + 1) * blk)
                add = skip
                if pen_block:
                    pen = pen_ref[pl.ds(kc * per_tile + a, 1), :]
                    add = (pen[:, cs] if pen_per_head else pen) + skip
                pieces.append(s_ref[rows, cs] + bt_ref[cls, rows, cs] + add)
            s = pieces[0] if per_tile == 1 else jnp.concatenate(pieces, axis=0)
            m_prev = m_ref[:, cs]
            m_new = jnp.maximum(m_prev, jnp.max(s, axis=0, keepdims=True))
            p = jnp.exp2(s - m_new)
            acc_ref[:, cs] = jnp.exp2(m_prev - m_new) * acc_ref[:, cs] + _dot(vt, _bf(p))
            m_ref[:, cs] = m_new

    k_lo = jnp.maximum(qi - k_back, 0) if k_back is not None else 0
    sa_ref[...] = raw_logits(k_lo)

    def pair(pi, carry):
        ka = k_lo + 2 * pi
        sb_ref[...] = raw_logits(ka + 1)
        attend(sa_ref, ka)
        sa_ref[...] = raw_logits(ka + 2)
        attend(sb_ref, ka + 1)
        return carry

    lax.fori_loop(0, (qi - k_lo + 2) // 2, pair, 0)
    acc = acc_ref[...]
    o_t = acc[0:HEAD_DIM] / jnp.maximum(acc[HEAD_DIM:HEAD_DIM + 1], 1e-30)
    cols_g = _gate_columns(gt_ref, gate_col0 + g * GROUP) if gate_col0 is not None else None
    o_ref[...] = _heads_from_lanes(o_t, tq, cols_g)


def flash_attention(q_arr, kv_arr, k_col, v_col, bias, batch, seq, *, pen=None, pen_block=0,
                    k_back=None, gate_arr=None, gate_col0=None):
    tq = ATTN_TILE
    nq = seq // tq
    cols = GROUP * tq
    assert k_back is None or k_back == bias.shape[1] - 1
    in_specs = [pl.BlockSpec((tq, GROUP * HEAD_DIM), lambda b, g, i: (b * nq + i, g)),
                pl.BlockSpec((seq, HEAD_DIM), lambda b, g, i: (b, k_col + g)),
                pl.BlockSpec((seq, HEAD_DIM), lambda b, g, i: (b, v_col + g)),
                pl.BlockSpec((None,) + bias.shape[1:], lambda b, g, i: (g, 0, 0, 0))]
    args = [q_arr, kv_arr, kv_arr, bias]
    pen_per_head = False
    if pen is not None:
        if pen.ndim == 4:
            in_specs.append(pl.BlockSpec((None, None, pen.shape[2], tq), lambda b, g, i: (b, g, 0, i)))
        else:
            pen_per_head = True
            in_specs.append(pl.BlockSpec((None, None, None, pen.shape[3], cols), lambda b, g, i: (b, g, i, 0, 0)))
        args.append(pen)
    if gate_arr is not None:
        in_specs.append(pl.BlockSpec((tq, LANE), lambda b, g, i: (b * nq + i, 0)))
        args.append(gate_arr)
    return pl.pallas_call(
        functools.partial(_flash_body, pen_block=pen_block if pen is not None else 0, pen_per_head=pen_per_head,
                          k_back=k_back, gate_col0=gate_col0 if gate_arr is not None else None, seq=seq),
        grid=(batch, KV_HEADS, nq),
        in_specs=in_specs,
        out_specs=pl.BlockSpec((tq, GROUP * HEAD_DIM), lambda b, g, i: (b * nq + i, g)),
        out_shape=jax.ShapeDtypeStruct((batch * seq, Q_WIDTH), jnp.float32),
        scratch_shapes=[pltpu.VMEM((1, cols), jnp.float32),
                        pltpu.VMEM((HEAD_DIM + SUBLANE, cols), jnp.float32),
                        pltpu.VMEM((cols, HEAD_DIM), jnp.bfloat16),
                        pltpu.VMEM((seq, HEAD_DIM), jnp.bfloat16),
                        pltpu.VMEM((HEAD_DIM + SUBLANE, seq), jnp.bfloat16),
                        pltpu.VMEM((tq, cols), jnp.float32), pltpu.VMEM((tq, cols), jnp.float32)],
        compiler_params=_cparams("parallel", "parallel", "arbitrary"),
        name="flash_attention",
    )(*args)


def _moba_gate_body(q_ref, k_ref, pen_ref, km_ref, *, n_blocks):
    qi = pl.program_id(2)
    tq = ATTN_TILE
    cols = GROUP * tq

    @pl.when(qi == 0)
    def _():
        k = k_ref[...]
        km_ref[...] = jnp.sum(k.reshape(n_blocks, MOBA_BLOCK, HEAD_DIM), axis=1) / MOBA_BLOCK

    qh, ql = _split_bf16(_stack_heads(q_ref[...]), 2)
    kh, kl = _split_bf16(km_ref[...], 2)
    gate = _dot(kh, qh, NT_DIMS) + _dot(kh, ql, NT_DIMS) + _dot(kl, qh, NT_DIMS)
    blk = lax.broadcasted_iota(jnp.int32, (n_blocks, cols), 0)
    tok = qi * tq + (lax.broadcasted_iota(jnp.int32, (n_blocks, cols), 1) & (tq - 1))
    own = lax.shift_right_logical(tok, int(math.log2(MOBA_BLOCK)))
    gate = jnp.where(blk < own, gate, NEG_INF)
    chosen = ((_rank_rows(gate, n_blocks) < MOBA_TOPK) & (blk < own)) | (blk == own)
    pen_ref[...] = jnp.where(chosen, 0.0, NEG_INF)


def moba_gate(proj, k_col, batch, seq):
    tq = ATTN_TILE
    nq = seq // tq
    cols = GROUP * tq
    nb = seq // MOBA_BLOCK
    return pl.pallas_call(
        functools.partial(_moba_gate_body, n_blocks=nb),
        grid=(batch, KV_HEADS, nq),
        in_specs=[pl.BlockSpec((tq, GROUP * HEAD_DIM), lambda b, g, i: (b * nq + i, g)),
                  pl.BlockSpec((seq, HEAD_DIM), lambda b, g, i: (b, k_col + g))],
        out_specs=pl.BlockSpec((None, None, None, nb, cols), lambda b, g, i: (b, g, i, 0, 0)),
        out_shape=jax.ShapeDtypeStruct((batch, KV_HEADS, nq, nb, cols), jnp.float32),
        scratch_shapes=[pltpu.VMEM((nb, HEAD_DIM), jnp.float32)],
        compiler_params=_cparams("parallel", "parallel", "arbitrary"),
        name="moba_gate",
    )(proj, proj)


def _tril_ones(n, strict=False):
    r = lax.broadcasted_iota(jnp.int32, (n, n), 0)
    c = lax.broadcasted_iota(jnp.int32, (n, n), 1)
    return (r > c) if strict else (r >= c)


def _chunk_cumsum(x):
    tril = _bf(_tril_ones(x.shape[0]))
    return sum(_dot(tril, part) for part in _split_bf16(x, 3))


def _head_rms_gate(o, norm_g, gate):
    ms = jnp.mean(o * o, axis=-1, keepdims=True)
    return o * lax.rsqrt(ms + NORM_EPS) * norm_g * (gate * jax.nn.sigmoid(gate))


def _hgrn2_body(q_ref, f_ref, i_ref, g_ref, lbl_ref, ng_ref, s0_ref, o_ref, s_out_ref, st_ref, *, layer):
    ti = pl.program_id(2)

    @pl.when(ti == 0)
    def _():
        st_ref[...] = s0_ref[...].T

    lbl = lbl_ref[...]
    e = jnp.exp(lbl - jnp.max(lbl, axis=0, keepdims=True))
    p = e / jnp.sum(e, axis=0, keepdims=True)
    lb = jnp.zeros((1, HG_DK), jnp.float32)
    for r in range(1, layer + 1):
        lb = lb + p[r:r + 1]
    causal = _tril_ones(CHUNK)
    work = []
    for c in range(q_ref.shape[0] // CHUNK):
        sl = slice(c * CHUNK, (c + 1) * CHUNK)
        q = q_ref[sl, :]
        qh = q * jax.nn.sigmoid(q) * HG_DK ** -0.5
        fg = lb + (1.0 - lb) * jax.nn.sigmoid(f_ref[sl, :])
        k = 1.0 - fg
        v = _bf(i_ref[sl, :])
        b = _chunk_cumsum(jnp.log(fg))
        b_mid = b[CHUNK // 2:CHUNK // 2 + 1]
        b_last = b[CHUNK - 1:CHUNK]
        a = _dot(_bf(qh * jnp.exp(b - b_mid)), _bf(k * jnp.exp(b_mid - b)), NT_DIMS)
        a = jnp.where(causal, a, 0.0)
        work.append((sl, _dot(_bf(a), v), _bf(qh * jnp.exp(b)), jnp.exp(b_last),
                     _dot(v, _bf(k * jnp.exp(b_last - b)), TN_DIMS)))
    for sl, o_intra, q_in, d_last, kv in work:
        st = st_ref[...]
        o = o_intra + _dot(q_in, _bf(st), NT_DIMS)
        st_ref[...] = st * d_last + kv
        o_ref[sl, :] = _head_rms_gate(o, ng_ref[...], g_ref[sl, :])

    @pl.when(ti == pl.num_programs(2) - 1)
    def _():
        s_out_ref[...] = st_ref[...].T


def hgrn2_scan(proj, lb_logits, norm_g, s0, layer, batch, seq):
    tt = _row_tile(seq, SCAN_TILE)
    nt = seq // tt
    h = HG_HEADS

    def col(k):
        return pl.BlockSpec((tt, HG_DK), lambda b, hh, t: (b * nt + t, k * h + hh))

    return pl.pallas_call(
        functools.partial(_hgrn2_body, layer=layer),
        grid=(batch, h, nt),
        in_specs=[col(0), col(1), col(2), col(3),
                  pl.BlockSpec((DEPTH, HG_DK), lambda b, hh, t: (0, hh)),
                  pl.BlockSpec((1, HG_DV), lambda b, hh, t: (0, 0)),
                  pl.BlockSpec((None, None, HG_DK, HG_DV), lambda b, hh, t: (b, hh, 0, 0))],
        out_specs=[pl.BlockSpec((tt, HG_DV), lambda b, hh, t: (b * nt + t, hh)),
                   pl.BlockSpec((None, None, HG_DK, HG_DV), lambda b, hh, t: (b, hh, 0, 0))],
        out_shape=[jax.ShapeDtypeStruct((batch * seq, h * HG_DV), jnp.float32),
                   jax.ShapeDtypeStruct((batch, h, HG_DK, HG_DV), jnp.float32)],
        scratch_shapes=[pltpu.VMEM((HG_DV, HG_DK), jnp.float32)],
        compiler_params=_cparams("parallel", "parallel", "arbitrary"),
        name="hgrn2_scan",
    )(proj, proj, proj, proj, lb_logits, norm_g.reshape(1, HG_DV), s0)


def _lane_column(x, lane_idx):
    lane = lax.broadcasted_iota(jnp.int32, x.shape, 1)
    return jnp.sum(jnp.where(lane == lane_idx, x, 0.0), axis=1, keepdims=True)


def _softplus(x):
    return jnp.maximum(x, 0.0) + jnp.log(1.0 + jnp.exp(-jnp.abs(x)))


def _l2n(x):
    return x * lax.rsqrt(jnp.sum(x * x, axis=-1, keepdims=True) + NORM_EPS)


def _gdn_body(q_ref, k_ref, v_ref, z_ref, t_ref, bq_ref, bk_ref, bv_ref, wq_ref, wk_ref, wv_ref,
              al_ref, dtb_ref, ng_ref, s0_ref, o_ref, s_out_ref, xs_ref, y_ref, s_ref):
    hq0 = pl.program_id(1) * GDN_HPS
    n_v = GDN_HPS * GDN_REP
    ti = pl.program_id(2)
    tt = q_ref.shape[0]
    dk, dv = GDN_DK, GDN_DV
    pad = SUBLANE

    @pl.when(ti == 0)
    def _():
        s_ref[...] = s0_ref[...]
        xs_ref[0:pad, :] = jnp.concatenate([bq_ref[...], bk_ref[...], bv_ref[...]], axis=1)

    x = jnp.concatenate([q_ref[...], k_ref[...], v_ref[...]], axis=1)
    xs_ref[pad:, :] = x
    cw = jnp.concatenate([wq_ref[...], wk_ref[...], wv_ref[...]], axis=1)
    y = xs_ref[pad - 3:pad - 3 + tt, :] * cw[0:1]
    for i in range(1, GDN_CONV - 1):
        y = y + xs_ref[pad - 3 + i:pad - 3 + i + tt, :] * cw[i:i + 1]
    y = y + x * cw[GDN_CONV - 1:GDN_CONV]
    xs_ref[0:pad, :] = x[tt - pad:tt]
    y_ref[...] = y * jax.nn.sigmoid(y)

    strict = _tril_ones(CHUNK, strict=True)
    incl = _tril_ones(CHUNK)
    sel_rows = lax.shift_right_logical(lax.broadcasted_iota(jnp.int32, (n_v * CHUNK, LANE), 0), int(math.log2(CHUNK)))
    sel_lane = lax.broadcasted_iota(jnp.int32, (n_v * CHUNK, LANE), 1)
    pick = _bf(sel_lane == GDN_V_HEADS + hq0 * GDN_REP + sel_rows)
    n_chunks = tt // CHUNK

    work = []
    for c in range(n_chunks):
        sl = slice(c * CHUNK, (c + 1) * CHUNK)
        yc = y_ref[sl, :]
        tl = t_ref[sl, :]
        beta_all = jax.nn.sigmoid(tl)
        g_all = _chunk_cumsum(-jnp.exp(al_ref[...]) * _softplus(tl + dtb_ref[...]))
        g_rows = sum(_dot(pick, part, NT_DIMS) for part in _split_bf16(g_all, 3))
        for hh in range(GDN_HPS):
            q = _l2n(yc[:, hh * dk:(hh + 1) * dk]) * dk ** -0.5
            k = _l2n(yc[:, (GDN_HPS + hh) * dk:(GDN_HPS + hh + 1) * dk])
            qb, kb = _bf(q), _bf(k)
            kk = _dot(kb, kb, NT_DIMS)
            qk = _dot(qb, kb, NT_DIMS)
            for e in range(hh * GDN_REP, (hh + 1) * GDN_REP):
                hv = hq0 * GDN_REP + e
                v = yc[:, 2 * GDN_HPS * dk + e * dv:2 * GDN_HPS * dk + (e + 1) * dv]
                bt = _lane_column(beta_all, hv)
                gc = _lane_column(g_all, GDN_V_HEADS + hv)
                gdiff = gc - g_rows[e * CHUNK:(e + 1) * CHUNK]
                decay = jnp.exp(jnp.where(incl, gdiff, 0.0))
                d_strict = jnp.where(strict, decay, 0.0)
                d_incl = jnp.where(incl, decay, 0.0)
                eg = jnp.exp(gc)
                g_last = gc[CHUNK - 1:CHUNK]
                work.append(dict(
                    c=c, e=e, sol=jnp.concatenate([bt * v, (bt * eg) * k], axis=1), pw=bt * kk * d_strict,
                    aq=_bf(qk * d_incl), q_in=_bf(q * eg), k_out=_bf(k * jnp.exp(g_last - gc)),
                    d_last=jnp.exp(g_last)))

    r_i = lax.broadcasted_iota(jnp.int32, (CHUNK, CHUNK), 0)
    c_i = lax.broadcasted_iota(jnp.int32, (CHUNK, CHUNK), 1)
    same = [lax.shift_right_logical(r_i, sh) == lax.shift_right_logical(c_i, sh) for sh in range(3, 7)]
    eye = (r_i == c_i).astype(jnp.float32)
    for wk in work:
        l8 = jnp.where(same[0], wk["pw"], 0.0)
        l8b = _bf(l8)
        wk["t"] = eye - l8
        wk["p"] = _dot(l8b, l8b)
    for wk in work:
        pb = _bf(wk["p"])
        wk["t"] = wk["t"] + _dot(_bf(wk["t"]), pb)
        wk["p"] = _dot(pb, pb)
    for wk in work:
        wk["t"] = wk["t"] + _dot(_bf(wk["t"]), _bf(wk["p"]))
    for lvl in range(1, len(same)):
        for wk in work:
            tb = _bf(wk["t"])
            off = _bf(jnp.where(same[lvl] & jnp.logical_not(same[lvl - 1]), wk["pw"], 0.0))
            wk["t"] = wk["t"] - _dot(tb, _bf(_dot(off, tb)))
    for wk in work:
        wk["sol"] = _dot(_bf(wk["t"]), _bf(wk["sol"]))

    for wk in work:
        c, e = wk["c"], wk["e"]
        sl = slice(c * CHUNK, (c + 1) * CHUNK)
        u0, w = wk["sol"][:, :dv], wk["sol"][:, dv:]
        s = s_ref[e]
        sb = _bf(s)
        u = u0 - _dot(_bf(w), sb)
        o = _dot(wk["q_in"], sb) + _dot(wk["aq"], _bf(u))
        s_ref[e] = wk["d_last"] * s + _dot(wk["k_out"], _bf(u), TN_DIMS)
        o_ref[sl, e * dv:(e + 1) * dv] = _head_rms_gate(o, ng_ref[...], z_ref[sl, e * dv:(e + 1) * dv])

    @pl.when(ti == pl.num_programs(2) - 1)
    def _():
        s_out_ref[...] = s_ref[...]


def gdn_scan(main, tail, conv_buf, conv_w, a_log, dt_bias, norm_g, s0, batch, seq):
    tt = _row_tile(seq, SCAN_TILE)
    nt = seq // tt
    hq, rep = GDN_QK_HEADS // GDN_HPS, GDN_HPS * GDN_REP
    dk, dv = GDN_HPS * GDN_DK, GDN_DV
    vw = rep * dv
    buf = jnp.pad(conv_buf, ((0, 0), (SUBLANE - (GDN_CONV - 1), 0), (0, 0)))
    pad_lanes = jnp.zeros((LANE - 2 * GDN_V_HEADS,), jnp.float32)
    a_row = jnp.concatenate([jnp.zeros((GDN_V_HEADS,), jnp.float32), a_log, pad_lanes]).reshape(1, LANE)
    dt_row = jnp.concatenate([jnp.zeros((GDN_V_HEADS,), jnp.float32), dt_bias, pad_lanes]).reshape(1, LANE)
    k0 = hq
    v0 = 2 * hq * dk // vw
    z0 = GDN_CONV_DIM // vw
    row = lambda b, h, t: b * nt + t
    return pl.pallas_call(
        _gdn_body,
        grid=(batch, hq, nt),
        in_specs=[pl.BlockSpec((tt, dk), lambda b, h, t: (row(b, h, t), h)),
                  pl.BlockSpec((tt, dk), lambda b, h, t: (row(b, h, t), k0 + h)),
                  pl.BlockSpec((tt, vw), lambda b, h, t: (row(b, h, t), v0 + h)),
                  pl.BlockSpec((tt, vw), lambda b, h, t: (row(b, h, t), z0 + h)),
                  pl.BlockSpec((tt, LANE), lambda b, h, t: (row(b, h, t), 0)),
                  pl.BlockSpec((None, SUBLANE, dk), lambda b, h, t: (b, 0, h)),
                  pl.BlockSpec((None, SUBLANE, dk), lambda b, h, t: (b, 0, k0 + h)),
                  pl.BlockSpec((None, SUBLANE, vw), lambda b, h, t: (b, 0, v0 + h)),
                  pl.BlockSpec((GDN_CONV, dk), lambda b, h, t: (0, h)),
                  pl.BlockSpec((GDN_CONV, dk), lambda b, h, t: (0, k0 + h)),
                  pl.BlockSpec((GDN_CONV, vw), lambda b, h, t: (0, v0 + h)),
                  pl.BlockSpec((1, LANE), lambda b, h, t: (0, 0)),
                  pl.BlockSpec((1, LANE), lambda b, h, t: (0, 0)),
                  pl.BlockSpec((1, dv), lambda b, h, t: (0, 0)),
                  pl.BlockSpec((None, rep, GDN_DK, dv), lambda b, h, t: (b, h, 0, 0))],
        out_specs=[pl.BlockSpec((tt, vw), lambda b, h, t: (row(b, h, t), h)),
                   pl.BlockSpec((None, rep, GDN_DK, dv), lambda b, h, t: (b, h, 0, 0))],
        out_shape=[jax.ShapeDtypeStruct((batch * seq, GDN_V_HEADS * dv), jnp.float32),
                   jax.ShapeDtypeStruct((batch, GDN_V_HEADS, GDN_DK, dv), jnp.float32)],
        scratch_shapes=[pltpu.VMEM((tt + SUBLANE, 2 * dk + vw), jnp.float32),
                        pltpu.VMEM((tt, 2 * dk + vw), jnp.float32),
                        pltpu.VMEM((rep, GDN_DK, dv), jnp.float32)],
        compiler_params=_cparams("parallel", "parallel", "arbitrary"),
        name="gdn_scan",
    )(main, main, main, main, tail, buf, buf, buf, conv_w, conv_w, conv_w, a_row, dt_row,
      norm_g.reshape(1, dv), s0)


PAGE = 128
KV_SLABS = 2 * KV_HEADS
PAGE_ROWS = PAGE * KV_SLABS
CMP_PAGES = 8
CHUNKS_PER_PAGE = PAGE // CMP_STRIDE
ROWS = GROUP * 4


def _page_slab(pg, slab):
    return pg[pl.ds(slab, PAGE, stride=KV_SLABS), :]


def _compress_part_body(*refs, n_prefetch, paged):
    refs = refs[n_prefetch:]
    pages, w_ref, o_ref, xs_ref = refs[:CMP_PAGES], refs[CMP_PAGES], refs[CMP_PAGES + 1], refs[CMP_PAGES + 2]
    per_g = CMP_PAGES * CHUNKS_PER_PAGE
    for k, pg in enumerate(pages):
        for cg in range(KV_SLABS):
            xs_ref[k, cg] = _page_slab(pg, cg) if paged else pg[:, cg * HEAD_DIM:(cg + 1) * HEAD_DIM]
    def token_rows(c, l):
        return jnp.concatenate([xs_ref.at[k, c * KV_HEADS + g][pl.ds(l, CHUNKS_PER_PAGE, stride=CMP_STRIDE), :]
                                for g in range(KV_HEADS) for k in range(CMP_PAGES)], axis=0)

    for c in range(2):
        acc = None
        for lp in range(CMP_STRIDE // 2):
            x = jnp.concatenate([token_rows(c, 2 * lp), token_rows(c, 2 * lp + 1)], axis=1)
            d = _dot(_bf(x), w_ref[c, lp])
            acc = d if acc is None else acc + d
        for g in range(KV_HEADS):
            o_ref[c, g] = acc[g * per_g:(g + 1) * per_g]


def _compress_weights(cmp_w1):
    n_part = CMP_BLOCK // CMP_STRIDE
    w = cmp_w1.reshape(2, n_part, CMP_STRIDE, HEAD_DIM, CMP_HIDDEN).transpose(0, 2, 3, 1, 4)
    return w.reshape(2, CMP_STRIDE // 2, 2 * HEAD_DIM, n_part * CMP_HIDDEN).astype(jnp.bfloat16)


def compress_part_rows(kv_arr, col_block, cmp_w1, batch, seq):
    w = _compress_weights(cmp_w1)
    pages_per_b = seq // PAGE
    steps = pages_per_b // CMP_PAGES
    per_g = CMP_PAGES * CHUNKS_PER_PAGE

    def page_spec(k):
        return pl.BlockSpec((PAGE, KV_WIDTH), lambda b, s: (b * pages_per_b + s * CMP_PAGES + k, col_block))

    return pl.pallas_call(
        functools.partial(_compress_part_body, n_prefetch=0, paged=False),
        grid=(batch, steps),
        in_specs=[page_spec(k) for k in range(CMP_PAGES)] + [pl.BlockSpec(w.shape, lambda b, s: (0, 0, 0, 0))],
        out_specs=pl.BlockSpec((None, 2, KV_HEADS, per_g, w.shape[-1]), lambda b, s: (b, 0, 0, s, 0)),
        out_shape=jax.ShapeDtypeStruct((batch, 2, KV_HEADS, seq // CMP_STRIDE, w.shape[-1]), jnp.float32),
        scratch_shapes=[pltpu.VMEM((CMP_PAGES, 2 * KV_HEADS, PAGE, HEAD_DIM), jnp.float32)],
        compiler_params=_cparams("parallel", "arbitrary"),
        name="compress_part_rows",
    )(*([kv_arr] * CMP_PAGES), w)


def compress_part_paged(pool, page_table, cmp_w1):
    w = _compress_weights(cmp_w1)
    batch, n_pages = page_table.shape
    steps = n_pages // CMP_PAGES
    per_g = CMP_PAGES * CHUNKS_PER_PAGE

    def page_spec(k):
        return pl.BlockSpec((None, PAGE_ROWS, HEAD_DIM), lambda b, s, pt: (pt[b, s * CMP_PAGES + k], 0, 0))

    return pl.pallas_call(
        functools.partial(_compress_part_body, n_prefetch=1, paged=True),
        grid_spec=pltpu.PrefetchScalarGridSpec(
            num_scalar_prefetch=1, grid=(batch, steps),
            in_specs=[page_spec(k) for k in range(CMP_PAGES)]
            + [pl.BlockSpec(w.shape, lambda b, s, pt: (0, 0, 0, 0))],
            out_specs=pl.BlockSpec((None, 2, KV_HEADS, per_g, w.shape[-1]), lambda b, s, pt: (b, 0, 0, s, 0)),
            scratch_shapes=[pltpu.VMEM((CMP_PAGES, 2 * KV_HEADS, PAGE, HEAD_DIM), jnp.float32)]),
        out_shape=jax.ShapeDtypeStruct((batch, 2, KV_HEADS, n_pages * CHUNKS_PER_PAGE, w.shape[-1]), jnp.float32),
        compiler_params=_cparams("parallel", "arbitrary"),
        name="compress_part_paged",
    )(page_table, *([pool] * CMP_PAGES), w)


def _gelu_tanh(x):
    return x * (0.5 * (1.0 + jnp.tanh(math.sqrt(2.0 / math.pi) * (x + 0.044715 * (x * x * x)))))


def _compress_finish_body(p_ref, peh_ref, w2_ref, o_ref):
    n = p_ref.shape[1]
    for c in range(2):
        p = p_ref[c]
        hid = peh_ref[c:c + 1, :] + p[:, :CMP_HIDDEN]
        hid = hid + pltpu.roll(p[:, CMP_HIDDEN:], n - 1, 0)
        o_ref[c] = _dot(_bf(_gelu_tanh(hid)), _bf(w2_ref[c]))


def compress_finish(part, cmp_w1, cmp_w2, cmp_pe):
    batch, _, _, n, width = part.shape
    pe_hid = jnp.einsum('cld,cldh->ch', cmp_pe, cmp_w1)
    return pl.pallas_call(
        _compress_finish_body,
        grid=(batch, KV_HEADS),
        in_specs=[pl.BlockSpec((None, 2, None, n, width), lambda b, g: (b, 0, g, 0, 0)),
                  pl.BlockSpec((2, CMP_HIDDEN), lambda b, g: (0, 0)),
                  pl.BlockSpec((2, CMP_HIDDEN, HEAD_DIM), lambda b, g: (0, 0, 0))],
        out_specs=pl.BlockSpec((None, None, 2, n, HEAD_DIM), lambda b, g: (b, g, 0, 0, 0)),
        out_shape=jax.ShapeDtypeStruct((batch, KV_HEADS, 2, n, HEAD_DIM), jnp.float32),
        compiler_params=_cparams("parallel", "parallel"),
        name="compress_finish",
    )(part, pe_hid, cmp_w2)


def _rows_to_col(row, n):
    eye = lax.broadcasted_iota(jnp.int32, (n, n), 0) == lax.broadcasted_iota(jnp.int32, (n, n), 1)
    return jnp.sum(jnp.where(eye, jnp.broadcast_to(row, (n, n)), 0.0), axis=1, keepdims=True)


def _sample_bias_tiles(rel_table, past, n_new):
    j = np.arange(PAGE)[:, None]
    t = np.arange(n_new)[None, :]
    far = np.full((PAGE, n_new), REL_MAX_DIST)
    first = WINDOW + t - j
    last = PAGE + t - j
    new = t - j
    tiles = []
    for dist, ok in ((far, far > 0), (first, first < WINDOW), (last, last > 0), (new, (new >= 0) & (j < n_new))):
        b = jnp.where(ok[..., None], _bias_lookup(rel_table, dist), NEG_INF)
        b = b.reshape(PAGE, n_new, KV_HEADS, GROUP).transpose(2, 0, 3, 1).reshape(KV_HEADS, PAGE, GROUP * n_new)
        tiles.append(b)
    return jnp.stack(tiles, axis=1)


def _sample_cmp_body(q_ref, kc_ref, bias_ref, gt_ref, o_ref, pen_ref, *, n_sel_blocks, past):
    n = kc_ref.shape[1]
    nbp = pen_ref.shape[0]
    q = _bf(q_ref[...] * ATTN_SCALE)
    s = _dot(_bf(kc_ref[0]), q, NT_DIMS) + bias_ref[...]
    m = jnp.max(s, axis=0, keepdims=True)
    e = jnp.exp(s - m)
    p = e / jnp.maximum(jnp.sum(e, axis=0, keepdims=True), 1e-30)
    o = _dot(_bf(p), _bf(kc_ref[1]), TN_DIMS)
    o_ref[...] = o * jax.nn.sigmoid(gt_ref[...])

    r_i = lax.broadcasted_iota(jnp.int32, (ROWS, ROWS), 0)
    c_i = lax.broadcasted_iota(jnp.int32, (ROWS, ROWS), 1)
    n_tok = ROWS // GROUP
    same_tok = _bf((r_i & (n_tok - 1)) == (c_i & (n_tok - 1)))
    ratio = SEL_BLOCK // CMP_STRIDE
    j_i = lax.broadcasted_iota(jnp.int32, (nbp, n), 0)
    k_i = lax.broadcasted_iota(jnp.int32, (nbp, n), 1)
    w = _bf((k_i >= ratio * j_i - 1) & (k_i <= ratio * j_i + ratio - 1) & (j_i < n_sel_blocks))
    imp = sum(_dot(part, same_tok) for part in _split_bf16(p, 3))
    score = sum(_dot(w, part) for part in _split_bf16(imp, 3))
    blk = lax.broadcasted_iota(jnp.int32, (nbp, ROWS), 0)
    tok = past + (lax.broadcasted_iota(jnp.int32, (nbp, ROWS), 1) & (n_tok - 1))
    cur = lax.shift_right_logical(tok, int(math.log2(SEL_BLOCK)))
    forced = (blk == 0) | (blk == cur) | (blk == cur - 1)
    causal = blk <= cur
    score = jnp.where(forced, FORCE_SCORE, score)
    score = jnp.where(causal, score, -1.0)
    nbl = -(-nbp // LANE) * LANE
    n_idx = lax.broadcasted_iota(jnp.int32, (nbp, nbl), 0)
    m_idx = lax.broadcasted_iota(jnp.int32, (nbp, nbl), 1)
    lane_tok = lax.broadcasted_iota(jnp.int32, (nbp, ROWS), 1) & (n_tok - 1)
    rank = jnp.zeros((nbp, ROWS), jnp.float32)
    for t in range(n_tok):
        s_col = score[:, t:t + 1]
        s_row = jnp.sum(jnp.where(n_idx == m_idx, s_col, 0.0), axis=0, keepdims=True)
        s_row = jnp.where(m_idx[0:1] < n_sel_blocks, s_row, -2.0)
        ahead = (s_row > s_col) | ((s_row == s_col) & (m_idx < n_idx))
        rank_t = jnp.sum(ahead.astype(jnp.float32), axis=1, keepdims=True)
        rank = jnp.where(lane_tok == t, rank_t, rank)
    chosen = (rank < N_SEL) & causal & (blk < n_sel_blocks)
    pen_ref[...] = jnp.where(chosen, 0.0, NEG_INF)


def sample_cmp_select(qs, kc, bias, gate, past, n_sel_blocks):
    batch = qs.shape[0]
    n = kc.shape[3]
    nbp = -(-n_sel_blocks // SUBLANE) * SUBLANE
    return pl.pallas_call(
        functools.partial(_sample_cmp_body, n_sel_blocks=n_sel_blocks, past=past),
        grid=(batch, KV_HEADS),
        in_specs=[pl.BlockSpec((None, None, ROWS, HEAD_DIM), lambda b, g: (b, g, 0, 0)),
                  pl.BlockSpec((None, None, 2, n, HEAD_DIM), lambda b, g: (b, g, 0, 0, 0)),
                  pl.BlockSpec((None, n, ROWS), lambda b, g: (g, 0, 0)),
                  pl.BlockSpec((None, None, ROWS, HEAD_DIM), lambda b, g: (b, g, 0, 0))],
        out_specs=[pl.BlockSpec((None, None, ROWS, HEAD_DIM), lambda b, g: (b, g, 0, 0)),
                   pl.BlockSpec((None, None, nbp, ROWS), lambda b, g: (b, g, 0, 0))],
        out_shape=[jax.ShapeDtypeStruct((batch, KV_HEADS, ROWS, HEAD_DIM), jnp.float32),
                   jax.ShapeDtypeStruct((batch, KV_HEADS, nbp, ROWS), jnp.float32)],
        compiler_params=_cparams("parallel", "parallel"),
        name="sample_cmp_select",
    )(qs, kc, bias, gate)


ATTN_PAGES = 4


def _paged_attn_body(*refs, pen_block, gated, n_pages):
    it = iter(refs)
    pt_ref, tid_ref, q_ref = next(it), next(it), next(it)
    pages = [next(it) for _ in range(ATTN_PAGES)]
    new_ref, bt_ref = next(it), next(it)
    pen_ref = next(it) if pen_block else None
    gt_ref = next(it) if gated else None
    o_ref, m_ref, l_ref, acc_ref = next(it), next(it), next(it), next(it)
    step = pl.program_id(1)

    @pl.when(step == 0)
    def _():
        m_ref[...] = jnp.full(m_ref.shape, NEG_INF, jnp.float32)
        l_ref[...] = jnp.zeros(l_ref.shape, jnp.float32)
        acc_ref[...] = jnp.zeros(acc_ref.shape, jnp.float32)

    def page_terms(g, page, n_keys):
        tile = tid_ref[page]
        b = bt_ref[g, tile] if n_keys == PAGE else bt_ref[g, tile, 0:n_keys, :]
        if not pen_block:
            return b
        if pen_block >= PAGE:
            return b + pen_ref[g, pl.ds(page // (pen_block // PAGE), 1), :]
        per_page = PAGE // pen_block
        pieces = [b[a * pen_block:min((a + 1) * pen_block, n_keys)] + pen_ref[g, pl.ds(page * per_page + a, 1), :]
                  for a in range(-(-n_keys // pen_block))]
        return pieces[0] if len(pieces) == 1 else jnp.concatenate(pieces, axis=0)

    def attend(g, k, v, terms):
        s = _dot(_bf(k), _bf(q_ref[g] * ATTN_SCALE), NT_DIMS) + terms
        m_prev = m_ref[g]
        m_new = jnp.maximum(m_prev, jnp.max(s, axis=0, keepdims=True))
        alpha = jnp.exp(m_prev - m_new)
        p = jnp.exp(s - m_new)
        l_ref[g] = alpha * l_ref[g] + jnp.sum(p, axis=0, keepdims=True)
        acc_ref[g] = _rows_to_col(alpha, ROWS) * acc_ref[g] + _dot(_bf(p), _bf(v), TN_DIMS)
        m_ref[g] = m_new

    half = KV_HEADS * HEAD_DIM
    for kk in range(ATTN_PAGES):
        for g in range(KV_HEADS):
            attend(g, _page_slab(pages[kk], g), _page_slab(pages[kk], KV_HEADS + g),
                   page_terms(g, step * ATTN_PAGES + kk, PAGE))

    @pl.when(step == pl.num_programs(1) - 1)
    def _():
        n_new = new_ref.shape[0]
        for g in range(KV_HEADS):
            attend(g, new_ref[:, g * HEAD_DIM:(g + 1) * HEAD_DIM],
                   new_ref[:, half + g * HEAD_DIM:half + (g + 1) * HEAD_DIM], page_terms(g, n_pages, n_new))
            o = acc_ref[g] / _rows_to_col(jnp.maximum(l_ref[g], 1e-30), ROWS)
            if gated:
                o = o * jax.nn.sigmoid(gt_ref[g])
            o_ref[g] = o


def paged_attention(qs, pool, page_table, tile_ids, new_kv, bias_tiles, *, pen=None, pen_block=0, gate=None):
    batch, n_pages = page_table.shape
    steps = n_pages // ATTN_PAGES
    n_new = new_kv.shape[1]

    def page_spec(k):
        return pl.BlockSpec((None, PAGE_ROWS, HEAD_DIM), lambda b, s, pt, tid: (pt[b, s * ATTN_PAGES + k], 0, 0))

    grp = lambda b, s, pt, tid: (b, 0, 0, 0)
    in_specs = ([pl.BlockSpec((None, KV_HEADS, ROWS, HEAD_DIM), grp)] + [page_spec(k) for k in range(ATTN_PAGES)]
                + [pl.BlockSpec((None, n_new, KV_WIDTH), lambda b, s, pt, tid: (b, 0, 0)),
                   pl.BlockSpec(bias_tiles.shape, lambda b, s, pt, tid: (0, 0, 0, 0))])
    args = [qs] + [pool] * ATTN_PAGES + [new_kv, bias_tiles]
    if pen is not None:
        in_specs.append(pl.BlockSpec((None,) + pen.shape[1:], grp))
        args.append(pen)
    if gate is not None:
        in_specs.append(pl.BlockSpec((None, KV_HEADS, ROWS, HEAD_DIM), grp))
        args.append(gate)
    return pl.pallas_call(
        functools.partial(_paged_attn_body, pen_block=pen_block if pen is not None else 0, gated=gate is not None,
                          n_pages=n_pages),
        grid_spec=pltpu.PrefetchScalarGridSpec(
            num_scalar_prefetch=2, grid=(batch, steps), in_specs=in_specs,
            out_specs=pl.BlockSpec((None, KV_HEADS, ROWS, HEAD_DIM), grp),
            scratch_shapes=[pltpu.VMEM((KV_HEADS, 1, ROWS), jnp.float32), pltpu.VMEM((KV_HEADS, 1, ROWS), jnp.float32),
                            pltpu.VMEM((KV_HEADS, ROWS, HEAD_DIM), jnp.float32)]),
        out_shape=jax.ShapeDtypeStruct((batch, KV_HEADS, ROWS, HEAD_DIM), jnp.float32),
        compiler_params=_cparams("parallel", "arbitrary"),
        name="paged_attention",
    )(page_table, tile_ids, *args)


def _moba_sample_gate_body(pt_ref, q_ref, *refs, n_blocks, past):
    pages, (pen_ref, km_ref) = refs[:ATTN_PAGES], refs[ATTN_PAGES:]
    step = pl.program_id(1)
    per_block = MOBA_BLOCK // PAGE

    @pl.when(step == 0)
    def _():
        km_ref[...] = jnp.zeros(km_ref.shape, jnp.float32)

    for kk in range(ATTN_PAGES):
        blk = (step * ATTN_PAGES + kk) // per_block
        slab_sums = jnp.sum(pages[kk][...], axis=0)
        for g in range(KV_HEADS):
            km_ref[g, pl.ds(blk, 1), :] += slab_sums[g:g + 1]

    @pl.when(step == pl.num_programs(1) - 1)
    def _():
        nbp = pen_ref.shape[1]
        blk = lax.broadcasted_iota(jnp.int32, (nbp, ROWS), 0)
        n_tok = ROWS // GROUP
        tok = past + (lax.broadcasted_iota(jnp.int32, (nbp, ROWS), 1) & (n_tok - 1))
        own = lax.shift_right_logical(tok, int(math.log2(MOBA_BLOCK)))
        for g in range(KV_HEADS):
            kh, kl = _split_bf16(km_ref[g] / MOBA_BLOCK, 2)
            qh, ql = _split_bf16(q_ref[g], 2)
            gate = _dot(kh, qh, NT_DIMS) + _dot(kh, ql, NT_DIMS) + _dot(kl, qh, NT_DIMS)
            gate = jnp.where(blk < own, gate, NEG_INF)
            chosen = ((_rank_rows(gate, n_blocks) < MOBA_TOPK) & (blk < own)) | (blk == own)
            pen_ref[g] = jnp.where(chosen, 0.0, NEG_INF)


def moba_sample_gate(qs, pool, page_table, past, n_new):
    batch, n_pages = page_table.shape
    steps = n_pages // ATTN_PAGES
    n_blocks = -(-(past + n_new) // MOBA_BLOCK)
    nbp = -(-n_blocks // SUBLANE) * SUBLANE

    def page_spec(k):
        return pl.BlockSpec((None, PAGE, None, KV_HEADS, HEAD_DIM),
                            lambda b, s, pt: (pt[b, s * ATTN_PAGES + k], 0, 0, 0, 0))

    return pl.pallas_call(
        functools.partial(_moba_sample_gate_body, n_blocks=n_blocks, past=past),
        grid_spec=pltpu.PrefetchScalarGridSpec(
            num_scalar_prefetch=1, grid=(batch, steps),
            in_specs=[pl.BlockSpec((None, KV_HEADS, ROWS, HEAD_DIM), lambda b, s, pt: (b, 0, 0, 0))]
            + [page_spec(k) for k in range(ATTN_PAGES)],
            out_specs=pl.BlockSpec((None, KV_HEADS, nbp, ROWS), lambda b, s, pt: (b, 0, 0, 0)),
            scratch_shapes=[pltpu.VMEM((KV_HEADS, nbp, HEAD_DIM), jnp.float32)]),
        out_shape=jax.ShapeDtypeStruct((batch, KV_HEADS, nbp, ROWS), jnp.float32),
        compiler_params=_cparams("parallel", "arbitrary"),
        name="moba_sample_gate",
    )(page_table, qs, *([pool] * ATTN_PAGES))


def _sample_rows(x, batch, n_tok):
    return x.reshape(batch, n_tok, KV_HEADS, GROUP, HEAD_DIM).transpose(0, 2, 3, 1, 4).reshape(
        batch, KV_HEADS, GROUP * n_tok, HEAD_DIM)


def _sample_unrows(o, batch, n_tok):
    return o.reshape(batch, KV_HEADS, GROUP, n_tok, HEAD_DIM).transpose(0, 3, 1, 2, 4).reshape(batch * n_tok, Q_WIDTH)


def _sample_gate_rows(tail, branch, batch, n_tok):
    gt = tail[:, branch * N_HEADS:(branch + 1) * N_HEADS].reshape(batch, n_tok, KV_HEADS, GROUP)
    gt = gt.transpose(0, 2, 3, 1).reshape(batch, KV_HEADS, GROUP * n_tok, 1)
    return jnp.broadcast_to(gt, (batch, KV_HEADS, GROUP * n_tok, HEAD_DIM))


def _pad_new(kv_new, batch, n_tok):
    return jnp.pad(kv_new.reshape(batch, n_tok, KV_WIDTH), ((0, 0), (0, SUBLANE - n_tok), (0, 0)))


def _nsa_sample_pallas(main, tail, cache_c, cache_s, cache_w, page_table, cmp_w1, cmp_w2, cmp_pe, rel_table):
    batch, n_pages = page_table.shape
    n_tok = main.shape[0] // batch
    past = n_pages * PAGE
    wbuf = cache_w.shape[1]
    assert n_tok == ROWS // GROUP and cache_c.shape[1] == PAGE and wbuf == WINDOW and WINDOW % PAGE == 0
    n_cmp = (past + n_tok - CMP_BLOCK) // CMP_STRIDE + 1
    assert n_cmp + CMP_BLOCK // CMP_STRIDE - 1 == past // CMP_STRIDE
    qs = _sample_rows(main[:, :Q_WIDTH], batch, n_tok)
    kv_new = [main[:, Q_WIDTH + c * KV_WIDTH:Q_WIDTH + (c + 1) * KV_WIDTH] for c in range(3)]
    tiles = _sample_bias_tiles(rel_table, past, n_tok)
    flat = lambda pool: pool.reshape(pool.shape[0], PAGE_ROWS, HEAD_DIM)

    kc = compress_finish(compress_part_paged(flat(cache_c), page_table, cmp_w1), cmp_w1, cmp_w2, cmp_pe)
    n = kc.shape[3]
    dist = past + np.arange(n_tok)[None, :] - (np.arange(n)[:, None] * CMP_STRIDE + CMP_BLOCK - 1)
    ok = (dist >= 0) & (np.arange(n)[:, None] < n_cmp)
    cb = jnp.where(ok[..., None], _bias_lookup(rel_table, dist), NEG_INF)
    cb = cb.reshape(n, n_tok, KV_HEADS, GROUP).transpose(2, 0, 3, 1).reshape(KV_HEADS, n, GROUP * n_tok)
    n_sel_blocks = -(-(past + n_tok) // SEL_BLOCK)
    o_cmp, pen = sample_cmp_select(qs, kc, cb, _sample_gate_rows(tail, 0, batch, n_tok), past, n_sel_blocks)

    far_then_last = jnp.asarray([0] * (n_pages - 1) + [2, 3], jnp.int32)
    o_sel = paged_attention(qs, flat(cache_s), page_table, far_then_last, _pad_new(kv_new[1], batch, n_tok), tiles,
                            pen=pen, pen_block=SEL_BLOCK, gate=_sample_gate_rows(tail, 1, batch, n_tok))
    w_pages = wbuf // PAGE
    win_table = jnp.arange(batch * w_pages, dtype=jnp.int32).reshape(batch, w_pages)
    win_tiles = jnp.asarray([1] + [0] * (w_pages - 2) + [2, 3], jnp.int32)
    o_win = paged_attention(qs, cache_w.reshape(batch * w_pages, PAGE_ROWS, HEAD_DIM), win_table, win_tiles,
                            _pad_new(kv_new[2], batch, n_tok), tiles, gate=_sample_gate_rows(tail, 2, batch, n_tok))
    outs = [_sample_unrows(o, batch, n_tok) for o in (o_cmp, o_sel, o_win)]
    shape = (batch, n_tok, 2, KV_HEADS, HEAD_DIM)
    new_win = jnp.concatenate([cache_w[:, n_tok:], kv_new[2].reshape(shape)], axis=1)
    return outs, kv_new[0].reshape(shape), kv_new[1].reshape(shape), new_win


def _moba_sample_pallas(proj, cache_kv, page_table, rel_table):
    batch, n_pages = page_table.shape
    n_tok = proj.shape[0] // batch
    past = n_pages * PAGE
    assert n_tok == ROWS // GROUP and (past // MOBA_BLOCK) * MOBA_BLOCK == past
    qs = _sample_rows(proj[:, :Q_WIDTH], batch, n_tok)
    kv_new = proj[:, Q_WIDTH:]
    pool = cache_kv.reshape(cache_kv.shape[0], PAGE_ROWS, HEAD_DIM)
    pen = moba_sample_gate(qs, cache_kv, page_table, past, n_tok)
    tile_ids = jnp.asarray([0] * (n_pages - 1) + [2, 3], jnp.int32)
    o = paged_attention(qs, pool, page_table, tile_ids, _pad_new(kv_new, batch, n_tok),
                        _sample_bias_tiles(rel_table, past, n_tok), pen=pen, pen_block=MOBA_BLOCK)
    return _sample_unrows(o, batch, n_tok), kv_new.reshape(batch, n_tok, 2, KV_HEADS, HEAD_DIM)


def _cols_to_row(col, n):
    eye = lax.broadcasted_iota(jnp.int32, (n, n), 0) == lax.broadcasted_iota(jnp.int32, (n, n), 1)
    return jnp.sum(jnp.where(eye, jnp.broadcast_to(col, (n, n)), 0.0), axis=0, keepdims=True)


def _hgrn2_step_body(q_ref, f_ref, i_ref, g_ref, lbl_ref, ng_ref, s0_ref, o_ref, s_out_ref, *, layer):
    n_tok = q_ref.shape[0]
    lbl = lbl_ref[...]
    e = jnp.exp(lbl - jnp.max(lbl, axis=0, keepdims=True))
    p = e / jnp.sum(e, axis=0, keepdims=True)
    lb = jnp.zeros((1, HG_DK), jnp.float32)
    for r in range(1, layer + 1):
        lb = lb + p[r:r + 1]
    q = q_ref[...]
    qh = q * jax.nn.sigmoid(q) * HG_DK ** -0.5
    fg = lb + (1.0 - lb) * jax.nn.sigmoid(f_ref[...])
    k = 1.0 - fg
    v = i_ref[...]
    st = s0_ref[...].T
    rows = []
    for t in range(n_tok):
        st = st * fg[t:t + 1] + _rows_to_col(v[t:t + 1], HG_DV) * k[t:t + 1]
        rows.append(_cols_to_row(jnp.sum(st * qh[t:t + 1], axis=1, keepdims=True), HG_DV))
    o_ref[...] = _head_rms_gate(jnp.concatenate(rows, axis=0), ng_ref[...], g_ref[...])
    s_out_ref[...] = st.T


def hgrn2_step(proj, lb_logits, norm_g, s0, layer):
    batch, n_tok, _ = proj.shape
    h = HG_HEADS

    def col(k):
        return pl.BlockSpec((None, n_tok, HG_DK), lambda b, hh: (b, 0, k * h + hh))

    return pl.pallas_call(
        functools.partial(_hgrn2_step_body, layer=layer),
        grid=(batch, h),
        in_specs=[col(0), col(1), col(2), col(3),
                  pl.BlockSpec((DEPTH, HG_DK), lambda b, hh: (0, hh)),
                  pl.BlockSpec((1, HG_DV), lambda b, hh: (0, 0)),
                  pl.BlockSpec((None, None, HG_DK, HG_DV), lambda b, hh: (b, hh, 0, 0))],
        out_specs=[pl.BlockSpec((None, n_tok, HG_DV), lambda b, hh: (b, 0, hh)),
                   pl.BlockSpec((None, None, HG_DK, HG_DV), lambda b, hh: (b, hh, 0, 0))],
        out_shape=[jax.ShapeDtypeStruct((batch, n_tok, h * HG_DV), jnp.float32),
                   jax.ShapeDtypeStruct((batch, h, HG_DK, HG_DV), jnp.float32)],
        compiler_params=_cparams("parallel", "parallel"),
        name="hgrn2_step",
    )(proj, proj, proj, proj, lb_logits, norm_g.reshape(1, HG_DV), s0)


def _gdn_step_body(q_ref, k_ref, v_ref, z_ref, t_ref, bq_ref, bk_ref, bv_ref, wq_ref, wk_ref, wv_ref,
                   al_ref, dtb_ref, ng_ref, s0_ref, o_ref, s_out_ref, xs_ref):
    hq = pl.program_id(1)
    n_tok = q_ref.shape[0]
    dk, dv = GDN_DK, GDN_DV
    pad = SUBLANE
    xs_ref[0:pad, :] = jnp.concatenate([bq_ref[...], bk_ref[...], bv_ref[...]], axis=1)
    x = jnp.concatenate([q_ref[...], k_ref[...], v_ref[...]], axis=1)
    xs_ref[pad:pad + n_tok, :] = x
    cw = jnp.concatenate([wq_ref[...], wk_ref[...], wv_ref[...]], axis=1)
    y = xs_ref[pad - 3:pad - 3 + n_tok, :] * cw[0:1]
    for i in range(1, GDN_CONV - 1):
        y = y + xs_ref[pad - 3 + i:pad - 3 + i + n_tok, :] * cw[i:i + 1]
    y = y + x * cw[GDN_CONV - 1:GDN_CONV]
    y = y * jax.nn.sigmoid(y)
    q = _l2n(y[:, 0:dk]) * dk ** -0.5
    k = _l2n(y[:, dk:2 * dk])
    tl = t_ref[...]
    beta_all = jax.nn.sigmoid(tl)
    la_all = -jnp.exp(al_ref[...]) * _softplus(tl + dtb_ref[...])
    for e in range(GDN_REP):
        hv = hq * GDN_REP + e
        v = y[:, 2 * dk + e * dv:2 * dk + (e + 1) * dv]
        bt = _lane_column(beta_all, hv)
        a = jnp.exp(_lane_column(la_all, GDN_V_HEADS + hv))
        s = s0_ref[e]
        rows = []
        for t in range(n_tok):
            k_col = _rows_to_col(k[t:t + 1], dk)
            ks = jnp.sum(s * k_col, axis=0, keepdims=True)
            u = bt[t:t + 1] * (v[t:t + 1] - a[t:t + 1] * ks)
            s = a[t:t + 1] * s + k_col * u
            rows.append(jnp.sum(s * _rows_to_col(q[t:t + 1], dk), axis=0, keepdims=True))
        s_out_ref[e] = s
        o_ref[:, e * dv:(e + 1) * dv] = _head_rms_gate(jnp.concatenate(rows, axis=0), ng_ref[...],
                                                       z_ref[:, e * dv:(e + 1) * dv])


def gdn_step(main, tail, conv_buf, conv_w, a_log, dt_bias, norm_g, s0):
    batch, n_tok, _ = main.shape
    hq, rep, dk, dv = GDN_QK_HEADS, GDN_REP, GDN_DK, GDN_DV
    vw = rep * dv
    buf = jnp.pad(conv_buf, ((0, 0), (SUBLANE - (GDN_CONV - 1), 0), (0, 0)))
    pad_lanes = jnp.zeros((LANE - 2 * GDN_V_HEADS,), jnp.float32)
    a_row = jnp.concatenate([jnp.zeros((GDN_V_HEADS,), jnp.float32), a_log, pad_lanes]).reshape(1, LANE)
    dt_row = jnp.concatenate([jnp.zeros((GDN_V_HEADS,), jnp.float32), dt_bias, pad_lanes]).reshape(1, LANE)
    k0 = hq
    v0 = 2 * hq * dk // vw
    z0 = GDN_CONV_DIM // vw
    return pl.pallas_call(
        _gdn_step_body,
        grid=(batch, hq),
        in_specs=[pl.BlockSpec((None, n_tok, dk), lambda b, h: (b, 0, h)),
                  pl.BlockSpec((None, n_tok, dk), lambda b, h: (b, 0, k0 + h)),
                  pl.BlockSpec((None, n_tok, vw), lambda b, h: (b, 0, v0 + h)),
                  pl.BlockSpec((None, n_tok, vw), lambda b, h: (b, 0, z0 + h)),
                  pl.BlockSpec((None, n_tok, LANE), lambda b, h: (b, 0, 0)),
                  pl.BlockSpec((None, SUBLANE, dk), lambda b, h: (b, 0, h)),
                  pl.BlockSpec((None, SUBLANE, dk), lambda b, h: (b, 0, k0 + h)),
                  pl.BlockSpec((None, SUBLANE, vw), lambda b, h: (b, 0, v0 + h)),
                  pl.BlockSpec((GDN_CONV, dk), lambda b, h: (0, h)),
                  pl.BlockSpec((GDN_CONV, dk), lambda b, h: (0, k0 + h)),
                  pl.BlockSpec((GDN_CONV, vw), lambda b, h: (0, v0 + h)),
                  pl.BlockSpec((1, LANE), lambda b, h: (0, 0)),
                  pl.BlockSpec((1, LANE), lambda b, h: (0, 0)),
                  pl.BlockSpec((1, dv), lambda b, h: (0, 0)),
                  pl.BlockSpec((None, rep, dk, dv), lambda b, h: (b, h, 0, 0))],
        out_specs=[pl.BlockSpec((None, n_tok, vw), lambda b, h: (b, 0, h)),
                   pl.BlockSpec((None, rep, dk, dv), lambda b, h: (b, h, 0, 0))],
        out_shape=[jax.ShapeDtypeStruct((batch, n_tok, GDN_V_HEADS * dv), jnp.float32),
                   jax.ShapeDtypeStruct((batch, GDN_V_HEADS, dk, dv), jnp.float32)],
        scratch_shapes=[pltpu.VMEM((2 * SUBLANE, 2 * dk + vw), jnp.float32)],
        compiler_params=_cparams("parallel", "parallel"),
        name="gdn_step",
    )(main, main, main, main, tail, buf, buf, buf, conv_w, conv_w, conv_w, a_row, dt_row, norm_g.reshape(1, dv), s0)


def _pad_cols(w, mult=LANE):
    n = w.shape[1]
    return jnp.pad(w, ((0, 0), (0, (-n) % mult)))


def _nsa_prompt(main, kv_rows, tail, batch, seq, cmp_w1, cmp_w2, cmp_pe, rel_table):
    kv_c, kv_s, kv_w = (kv_rows[c].reshape(batch, seq, 2, KV_HEADS, HEAD_DIM) for c in range(3))
    kc = compress_finish(compress_part_rows(main, Q_WIDTH // KV_WIDTH, cmp_w1, batch, seq), cmp_w1, cmp_w2, cmp_pe)
    o_cmp, pen = cmp_select(main, tail, kc, _cmp_bias_table(rel_table, kc.shape[3]), batch, seq)
    col = Q_WIDTH // HEAD_DIM
    o_sel = flash_attention(main, main, col + 2 * KV_HEADS, col + 3 * KV_HEADS, _flash_bias_tiles(rel_table, 0),
                            batch, seq, pen=pen, pen_block=SEL_BLOCK, gate_arr=tail, gate_col0=N_HEADS)
    o_win = flash_attention(main, main, col + 4 * KV_HEADS, col + 5 * KV_HEADS,
                            _flash_bias_tiles(rel_table, WINDOW), batch, seq, k_back=WINDOW // ATTN_TILE,
                            gate_arr=tail, gate_col0=2 * N_HEADS)
    return [o_cmp, o_sel, o_win], kv_c, kv_s, kv_w[:, -min(WINDOW, seq):]


def _moba_prompt(proj, batch, seq, rel_table):
    col = Q_WIDTH // HEAD_DIM
    pen = moba_gate(proj, col, batch, seq)
    return flash_attention(proj, proj, col, col + KV_HEADS, _flash_bias_tiles(rel_table, 0), batch, seq,
                           pen=pen, pen_block=MOBA_BLOCK)


def kernel(x_prompt, x_sample, cache_nsa_cmp_kv, cache_nsa_sel_kv, cache_nsa_win_kv, cache_moba_kv,
           state_hgrn2, state_gdn_conv, state_gdn_ssm, page_table, rel_table, ln_mix, ln_ffn, ln_final,
           ffn_w_up, ffn_w_down, nsa_w_in, nsa_cmp_w1, nsa_cmp_w2, nsa_cmp_pe, nsa_w_out, moba_w_in, moba_w_out,
           hg_w_in, hg_lb_logits, hg_norm, hg_w_out, gdn_w_in, gdn_conv_w, gdn_a_log, gdn_dt_bias, gdn_norm,
           gdn_w_out):
    bf = jnp.bfloat16
    bp, tp = x_prompt.shape[:2]
    bs, ts = x_sample.shape[:2]
    assert tp % ATTN_TILE == 0 and WINDOW % ATTN_TILE == 0 and ATTN_TILE == MOBA_BLOCK and tp % SCAN_TILE == 0
    xp = x_prompt.reshape(bp * tp, D_MODEL)
    xs = x_sample.reshape(bs * ts, D_MODEL)
    w_up, w_down = ffn_w_up.astype(bf), ffn_w_down.astype(bf)

    for layer in range(DEPTH):
        kind = layer % N_MIXERS
        g_mix = ln_mix[layer]
        if kind == 0:
            w_main, w_tail = nsa_w_in.astype(bf), _pad_cols(nsa_w_in[:, NSA_MAIN:]).astype(bf)
            main_p, kv_p = norm_matmul(xp, g_mix, w_main, kv_from=Q_WIDTH // KV_WIDTH, n_cols=NSA_MAIN)
            tail_p = norm_matmul(xp, g_mix, w_tail)
            main_s, tail_s = norm_matmul(xs, g_mix, w_main, n_cols=NSA_MAIN), norm_matmul(xs, g_mix, w_tail)
            op, nsa_cmp_p, nsa_sel_p, nsa_win_p = _nsa_prompt(main_p, kv_p, tail_p, bp, tp, nsa_cmp_w1, nsa_cmp_w2,
                                                              nsa_cmp_pe, rel_table)
            os_, nsa_cmp_s, nsa_sel_s, nsa_win_s = _nsa_sample_pallas(main_s, tail_s, cache_nsa_cmp_kv,
                                                                     cache_nsa_sel_kv, cache_nsa_win_kv, page_table,
                                                                     nsa_cmp_w1, nsa_cmp_w2, nsa_cmp_pe, rel_table)
            w_out = nsa_w_out.astype(bf)
        elif kind == 1:
            w_in = moba_w_in.astype(bf)
            pp, kv_p = norm_matmul(xp, g_mix, w_in, kv_from=Q_WIDTH // KV_WIDTH)
            op, moba_p = _moba_prompt(pp, bp, tp, rel_table), kv_p.reshape(bp, tp, 2, KV_HEADS, HEAD_DIM)
            os_, moba_s = _moba_sample_pallas(norm_matmul(xs, g_mix, w_in), cache_moba_kv, page_table, rel_table)
            op, os_ = [op], [os_]
            w_out = moba_w_out.astype(bf)
        elif kind == 2:
            w_in = hg_w_in.astype(bf)
            pp = norm_matmul(xp, g_mix, w_in)
            ps = norm_matmul(xs, g_mix, w_in).reshape(bs, ts, -1)
            s0 = jnp.zeros((bp, HG_HEADS, HG_DK, HG_DV), jnp.float32)
            op, hg_p = hgrn2_scan(pp, hg_lb_logits, hg_norm, s0, layer, bp, tp)
            os_, hg_s = hgrn2_step(ps, hg_lb_logits, hg_norm, state_hgrn2, layer)
            op, os_ = [op], [os_.reshape(bs * ts, -1)]
            w_out = hg_w_out.astype(bf)
        else:
            w_main, w_tail = gdn_w_in.astype(bf), _pad_cols(gdn_w_in[:, GDN_MAIN:]).astype(bf)
            main_p, tail_p = norm_matmul(xp, g_mix, w_main, n_cols=GDN_MAIN), norm_matmul(xp, g_mix, w_tail)
            main_s = norm_matmul(xs, g_mix, w_main, n_cols=GDN_MAIN).reshape(bs, ts, -1)
            tail_s = norm_matmul(xs, g_mix, w_tail).reshape(bs, ts, -1)
            buf0 = jnp.zeros((bp, GDN_CONV - 1, GDN_CONV_DIM), jnp.float32)
            s0 = jnp.zeros((bp, GDN_V_HEADS, GDN_DK, GDN_DV), jnp.float32)
            op, ssm_p = gdn_scan(main_p, tail_p, buf0, gdn_conv_w, gdn_a_log, gdn_dt_bias, gdn_norm, s0, bp, tp)
            conv_p = main_p.reshape(bp, tp, -1)[:, tp - (GDN_CONV - 1):, :GDN_CONV_DIM]
            os_, ssm_s = gdn_step(main_s, tail_s, state_gdn_conv, gdn_conv_w, gdn_a_log, gdn_dt_bias, gdn_norm,
                                  state_gdn_ssm)
            conv_s = jnp.concatenate([state_gdn_conv, main_s[:, :, :GDN_CONV_DIM]], axis=1)[:, ts:]
            op, os_ = [op], [os_.reshape(bs * ts, -1)]
            w_out = gdn_w_out.astype(bf)
        xp = matmul_res(op, w_out, xp)
        xs = matmul_res(os_, w_out, xs)
        xp = ffn(xp, ln_ffn[layer], w_up, w_down, layer)
        xs = ffn(xs, ln_ffn[layer], w_up, w_down, layer)
    y_prompt = final_norm(xp, ln_final).reshape(bp, tp, D_MODEL)
    y_sample = final_norm(xs, ln_final).reshape(bs, ts, D_MODEL)
    return (y_prompt, y_sample, nsa_cmp_p, nsa_cmp_s, nsa_sel_p, nsa_sel_s, nsa_win_p, nsa_win_s,
            moba_p, moba_s, hg_p, hg_s, conv_p, conv_s, ssm_p, ssm_s)
```

```python
import functools
import math

import jax
import jax.numpy as jnp
import numpy as np
from jax import lax
from jax.experimental import pallas as pl
from jax.experimental.pallas import tpu as pltpu

D_MODEL = 2048
DEPTH = 4
N_MIXERS = 4
HEAD_DIM = 128
N_HEADS = D_MODEL // HEAD_DIM
KV_HEADS = 4
GROUP = N_HEADS // KV_HEADS
ATTN_SCALE = HEAD_DIM ** -0.5
REL_BUCKETS = 32
REL_MAX_DIST = 128
CMP_BLOCK = 32
CMP_STRIDE = 16
CMP_HIDDEN = HEAD_DIM
SEL_BLOCK = 64
N_SEL = 16
WINDOW = 512
FORCE_SCORE = 1.0e4
MOBA_BLOCK = 256
MOBA_TOPK = 3
HG_DK = 128
HG_HEADS = D_MODEL // HG_DK
HG_DV = D_MODEL // HG_HEADS
GDN_DK = 128
GDN_DV = 128
GDN_QK_HEADS = D_MODEL // GDN_DK
GDN_V_HEADS = 2 * GDN_QK_HEADS
GDN_REP = GDN_V_HEADS // GDN_QK_HEADS
GDN_CONV = 4
GDN_CONV_DIM = 2 * GDN_QK_HEADS * GDN_DK + GDN_V_HEADS * GDN_DV
CHUNK = 64
NEG_INF = -1.0e30
NORM_EPS = 1e-6

Q_WIDTH = N_HEADS * HEAD_DIM
KV_WIDTH = 2 * KV_HEADS * HEAD_DIM
NSA_MAIN = Q_WIDTH + 3 * KV_WIDTH
GDN_MAIN = GDN_CONV_DIM + GDN_V_HEADS * GDN_DV

V7X_VMEM_LIMIT_BYTES = 56 * 1024 * 1024
LANE = 128
SUBLANE = 8
ATTN_TILE = 256
CMP_TILE = 128
SCAN_TILE = 512
GDN_HPS = 2
LOG2E = math.log2(math.e)
NT_DIMS = (((1,), (1,)), ((), ()))
TN_DIMS = (((0,), (0,)), ((), ()))


def _cparams(*sem):
    return pltpu.CompilerParams(dimension_semantics=sem, vmem_limit_bytes=V7X_VMEM_LIMIT_BYTES)


def _row_tile(m, target):
    t = min(m, target)
    while m % t:
        t //= 2
    return t


def _col_tile(n, target):
    t = min(n, target)
    while n % t or t % LANE:
        t -= LANE
    return t


def _split_bf16(x, parts):
    out = []
    for _ in range(parts - 1):
        hi = x.astype(jnp.bfloat16)
        out.append(hi)
        x = x - hi.astype(jnp.float32)
    out.append(x.astype(jnp.bfloat16))
    return out


def _bf(x):
    return x.astype(jnp.bfloat16)


def _dot(a, b, dims=None):
    if dims is None:
        return jnp.dot(a, b, preferred_element_type=jnp.float32)
    return lax.dot_general(a, b, dims, preferred_element_type=jnp.float32)


def _norm_matmul_body(x_ref, g_ref, w_ref, o_ref, *rest, kv_from):
    kv_ref, h_ref = rest if kv_from is not None else (None, rest[0])
    j = pl.program_id(1)

    @pl.when(j == 0)
    def _():
        x = x_ref[...]
        ms = jnp.mean(x * x, axis=-1, keepdims=True)
        h_ref[...] = _bf(x * lax.rsqrt(ms + NORM_EPS) * g_ref[...])

    out = _dot(h_ref[...], w_ref[...])
    o_ref[...] = out
    if kv_from is not None:
        @pl.when(j >= kv_from)
        def _():
            tm = out.shape[0]
            for slab in range(out.shape[1] // HEAD_DIM):
                kv_ref[pl.ds(slab, tm, stride=out.shape[1] // HEAD_DIM), :] = out[:, slab * HEAD_DIM:(slab + 1) * HEAD_DIM]


def norm_matmul(x, g, w, kv_from=None, n_cols=None):
    m, k = x.shape
    n = n_cols or w.shape[1]
    tm = _row_tile(m, 1024)
    tn = _col_tile(n, 1024)
    out_specs = [pl.BlockSpec((tm, tn), lambda i, j: (i, j))]
    out_shape = [jax.ShapeDtypeStruct((m, n), jnp.float32)]
    if kv_from is not None:
        assert tn == KV_WIDTH
        slabs = tn // HEAD_DIM
        out_specs.append(pl.BlockSpec((None, tm * slabs, HEAD_DIM), lambda i, j: (jnp.maximum(j - kv_from, 0), i, 0)))
        out_shape.append(jax.ShapeDtypeStruct((n // tn - kv_from, m * slabs, HEAD_DIM), jnp.float32))
    res = pl.pallas_call(
        functools.partial(_norm_matmul_body, kv_from=kv_from),
        grid=(m // tm, n // tn),
        in_specs=[pl.BlockSpec((tm, k), lambda i, j: (i, 0)),
                  pl.BlockSpec((1, k), lambda i, j: (0, 0)),
                  pl.BlockSpec((k, tn), lambda i, j: (0, j))],
        out_specs=out_specs,
        out_shape=out_shape,
        scratch_shapes=[pltpu.VMEM((tm, k), jnp.bfloat16)],
        compiler_params=_cparams("parallel", "arbitrary"),
        name="norm_matmul",
    )(x, g.reshape(1, k), w)
    return res if kv_from is not None else res[0]


def _matmul_res_body(*refs):
    *a_refs, w_ref, r_ref, o_ref = refs
    a = a_refs[0][...]
    for a_ref in a_refs[1:]:
        a = a + a_ref[...]
    o_ref[...] = r_ref[...] + _dot(_bf(a), w_ref[...])


def matmul_res(a_list, w, res):
    m, k = a_list[0].shape
    n = w.shape[1]
    tm = _row_tile(m, 512)
    tn = _col_tile(n, 1024)
    return pl.pallas_call(
        _matmul_res_body,
        grid=(m // tm, n // tn),
        in_specs=[pl.BlockSpec((tm, k), lambda i, j: (i, 0)) for _ in a_list]
        + [pl.BlockSpec((k, tn), lambda i, j: (0, j)),
           pl.BlockSpec((tm, tn), lambda i, j: (i, j))],
        out_specs=pl.BlockSpec((tm, tn), lambda i, j: (i, j)),
        out_shape=jax.ShapeDtypeStruct((m, n), jnp.float32),
        compiler_params=_cparams("parallel", "arbitrary"),
        name="matmul_res",
    )(*a_list, w, res)


def _ffn_body(x_ref, g_ref, wa_ref, wb_ref, wd_ref, o_ref, h_ref):
    @pl.when(pl.program_id(1) == 0)
    def _():
        x = x_ref[...]
        ms = jnp.mean(x * x, axis=-1, keepdims=True)
        h_ref[...] = _bf(x * lax.rsqrt(ms + NORM_EPS) * g_ref[...])
        o_ref[...] = x

    h = h_ref[...]
    a = _dot(h, wa_ref[...])
    b = _dot(h, wb_ref[...])
    o_ref[...] += _dot(_bf(a * jax.nn.sigmoid(a) * b), wd_ref[...])


def ffn(x, g, w_up, w_down, layer):
    m, k = x.shape
    hdim = w_down.shape[1]
    tm = _row_tile(m, 1024)
    th = _col_tile(hdim, 512)
    nh = hdim // th
    return pl.pallas_call(
        _ffn_body,
        grid=(m // tm, nh),
        in_specs=[pl.BlockSpec((tm, k), lambda i, j: (i, 0)),
                  pl.BlockSpec((1, k), lambda i, j: (0, 0)),
                  pl.BlockSpec((None, k, th), lambda i, j: (layer, 0, j)),
                  pl.BlockSpec((None, k, th), lambda i, j: (layer, 0, j + nh)),
                  pl.BlockSpec((None, th, k), lambda i, j: (layer, j, 0))],
        out_specs=pl.BlockSpec((tm, k), lambda i, j: (i, 0)),
        out_shape=jax.ShapeDtypeStruct((m, k), jnp.float32),
        scratch_shapes=[pltpu.VMEM((tm, k), jnp.bfloat16)],
        compiler_params=_cparams("parallel", "arbitrary"),
        name="ffn",
    )(x, g.reshape(1, k), w_up, w_up, w_down)


def _norm_body(x_ref, g_ref, o_ref):
    x = x_ref[...]
    ms = jnp.mean(x * x, axis=-1, keepdims=True)
    o_ref[...] = x * lax.rsqrt(ms + NORM_EPS) * g_ref[...]


def final_norm(x, g):
    m, k = x.shape
    tm = _row_tile(m, 512)
    return pl.pallas_call(
        _norm_body,
        grid=(m // tm,),
        in_specs=[pl.BlockSpec((tm, k), lambda i: (i, 0)), pl.BlockSpec((1, k), lambda i: (0, 0))],
        out_specs=pl.BlockSpec((tm, k), lambda i: (i, 0)),
        out_shape=jax.ShapeDtypeStruct((m, k), jnp.float32),
        compiler_params=_cparams("parallel"),
        name="final_norm",
    )(x, g.reshape(1, k))


def _bucket_np(dist):
    exact = REL_BUCKETS // 2
    d = np.maximum(dist, 0)
    ratio = np.log(np.maximum(d, 1).astype(np.float32) / exact) / math.log(REL_MAX_DIST / exact)
    large = np.minimum(exact + (ratio * (REL_BUCKETS - exact)).astype(np.int32), REL_BUCKETS - 1)
    return np.where(d < exact, d, large)


def _bias_lookup(rel_table, dist):
    bucket = _bucket_np(dist).astype(np.int32)
    ids = [int(b) for b in np.unique(bucket)]
    bk = jnp.asarray(bucket)[..., None]
    out = jnp.broadcast_to(rel_table[ids[0]], bucket.shape + (rel_table.shape[1],))
    for b in ids[1:]:
        out = jnp.where(bk == b, rel_table[b], out)
    return out


def _heads_to_lanes(t):
    keys, queries, _ = t.shape
    return t.reshape(keys, queries, KV_HEADS, GROUP).transpose(2, 0, 3, 1).reshape(KV_HEADS, keys, GROUP * queries)


def _flash_bias_tiles(rel_table, window):
    j = np.arange(ATTN_TILE)[:, None]
    i = np.arange(ATTN_TILE)[None, :]
    n_cls = window // ATTN_TILE + 1 if window else -(-REL_MAX_DIST // ATTN_TILE) + 2
    tiles = []
    for d in range(n_cls):
        dist = d * ATTN_TILE + i - j
        ok = dist >= 0
        if window:
            ok = ok & (dist < window)
        tiles.append(_heads_to_lanes(jnp.where(ok[..., None], _bias_lookup(rel_table, dist) * LOG2E, NEG_INF)))
    return jnp.stack(tiles, axis=1)


def _cmp_bias_table(rel_table, ncp):
    x = np.arange(ncp)[:, None]
    i = np.arange(CMP_TILE)[None, :]
    dist = i - CMP_STRIDE * (x - 16) - (CMP_BLOCK - 1)
    far = rel_table[REL_BUCKETS - 1]
    b = _heads_to_lanes(jnp.where((dist >= 0)[..., None], _bias_lookup(rel_table, dist), far))
    return jnp.concatenate([b, b], axis=1)


def _stack_heads(q):
    return jnp.concatenate([q[:, r * HEAD_DIM:(r + 1) * HEAD_DIM] for r in range(GROUP)], axis=0)


def _gate_columns(gt_ref, col0):
    gt = jax.nn.sigmoid(gt_ref[...])
    lane = lax.broadcasted_iota(jnp.int32, gt.shape, 1)
    return [jnp.sum(jnp.where(lane == col0 + r, gt, 0.0), axis=1, keepdims=True) for r in range(GROUP)]


def _heads_from_lanes(o_t, rows, cols=None):
    parts = []
    for r in range(GROUP):
        part = o_t[:, r * rows:(r + 1) * rows].T
        if cols is not None:
            part = part * cols[r]
        parts.append(part)
    return jnp.concatenate(parts, axis=1)


def _rank_rows(score, n_rows):
    row = lax.broadcasted_iota(jnp.int32, score.shape, 0)
    rank = jnp.zeros(score.shape, jnp.int32)
    for mm in range(n_rows):
        sm = score[mm:mm + 1, :]
        ahead = (sm > score) | ((sm == score) & (row > mm))
        rank = rank + ahead.astype(jnp.int32)
    return rank


def _cmp_select_body(q_ref, kc_ref, dt_ref, gt_ref, o_ref, pen_ref, *, n_sel_blocks):
    g = pl.program_id(1)
    qi = pl.program_id(2)
    tq = CMP_TILE
    cols = GROUP * tq
    ncp = kc_ref.shape[1]
    q4 = _bf(_stack_heads(q_ref[...]) * ATTN_SCALE)
    s = _dot(_bf(kc_ref[0]), q4, NT_DIMS)
    shift = (qi * (tq // CMP_STRIDE) + ncp - 16) % ncp
    bias = dt_ref[pl.ds(pl.multiple_of(ncp - shift, SUBLANE), ncp), :]
    t_col = qi * tq + (lax.broadcasted_iota(jnp.int32, (ncp, cols), 1) & (tq - 1))
    end_pos = lax.broadcasted_iota(jnp.int32, (ncp, cols), 0) * CMP_STRIDE + (CMP_BLOCK - 1)
    mask = t_col >= end_pos
    s = jnp.where(mask, s + bias, NEG_INF)
    m = jnp.max(s, axis=0, keepdims=True)
    e = jnp.where(mask, jnp.exp(s - m), 0.0)
    p = e / jnp.maximum(jnp.sum(e, axis=0, keepdims=True), 1e-30)
    o_t = _dot(_bf(kc_ref[1].T), _bf(p))
    o_ref[...] = _heads_from_lanes(o_t, tq, _gate_columns(gt_ref, g * GROUP))

    imp = p[:, 0:tq]
    for r in range(1, GROUP):
        imp = imp + p[:, r * tq:(r + 1) * tq]
    ratio = SEL_BLOCK // CMP_STRIDE
    j_i = lax.broadcasted_iota(jnp.int32, (n_sel_blocks, ncp), 0)
    c_i = lax.broadcasted_iota(jnp.int32, (n_sel_blocks, ncp), 1)
    w = _bf((c_i >= ratio * j_i - 1) & (c_i <= ratio * j_i + ratio - 1))
    score = sum(_dot(w, part) for part in _split_bf16(imp, 3))
    blk = lax.broadcasted_iota(jnp.int32, (n_sel_blocks, tq), 0)
    tok = qi * tq + lax.broadcasted_iota(jnp.int32, (n_sel_blocks, tq), 1)
    cur = lax.shift_right_logical(tok, int(math.log2(SEL_BLOCK)))
    forced = (blk == 0) | (blk == cur) | (blk == cur - 1)
    causal = blk <= cur
    score = jnp.where(forced, FORCE_SCORE, score)
    score = jnp.where(causal, score, -1.0)
    chosen = (_rank_rows(score, n_sel_blocks) < N_SEL) & causal
    pen_ref[...] = jnp.where(chosen, 0.0, NEG_INF)


def cmp_select(proj, tail, kc, dt, batch, seq):
    tq = CMP_TILE
    nq = seq // tq
    ncp = kc.shape[3]
    nsb = seq // SEL_BLOCK
    return pl.pallas_call(
        functools.partial(_cmp_select_body, n_sel_blocks=nsb),
        grid=(batch, KV_HEADS, nq),
        in_specs=[pl.BlockSpec((tq, GROUP * HEAD_DIM), lambda b, g, i: (b * nq + i, g)),
                  pl.BlockSpec((None, None, 2, ncp, HEAD_DIM), lambda b, g, i: (b, g, 0, 0, 0)),
                  pl.BlockSpec((None, 2 * ncp, GROUP * tq), lambda b, g, i: (g, 0, 0)),
                  pl.BlockSpec((tq, LANE), lambda b, g, i: (b * nq + i, 0))],
        out_specs=[pl.BlockSpec((tq, GROUP * HEAD_DIM), lambda b, g, i: (b * nq + i, g)),
                   pl.BlockSpec((None, None, nsb, tq), lambda b, g, i: (b, g, 0, i))],
        out_shape=[jax.ShapeDtypeStruct((batch * seq, Q_WIDTH), jnp.float32),
                   jax.ShapeDtypeStruct((batch, KV_HEADS, nsb, seq), jnp.float32)],
        compiler_params=_cparams("parallel", "parallel", "arbitrary"),
        name="cmp_select",
    )(proj, kc, dt, tail)


def _flash_body(*refs, pen_block, pen_per_head, k_back, gate_col0, seq):
    it = iter(refs)
    q_ref, k_ref, v_ref, bt_ref = next(it), next(it), next(it), next(it)
    pen_ref = next(it) if pen_block else None
    gt_ref = next(it) if gate_col0 is not None else None
    o_ref, m_ref, acc_ref, qa_ref, kb_ref, vt_ref, sa_ref, sb_ref = (next(it) for _ in range(8))
    g = pl.program_id(1)
    qi = pl.program_id(2)
    tq = tk = ATTN_TILE
    n_cls = bt_ref.shape[0]

    @pl.when(qi == 0)
    def _():
        vt_ref[HEAD_DIM:, :] = jnp.ones((SUBLANE, seq), jnp.bfloat16)
        for c in range(seq // tk):
            kb_ref[c * tk:(c + 1) * tk, :] = _bf(k_ref[c * tk:(c + 1) * tk, :])
            vt_ref[0:HEAD_DIM, c * tk:(c + 1) * tk] = _bf(v_ref[c * tk:(c + 1) * tk, :].T)

    qa_ref[...] = _bf(_stack_heads(q_ref[...]) * (ATTN_SCALE * LOG2E))
    m_ref[...] = jnp.full(m_ref.shape, NEG_INF, jnp.float32)
    acc_ref[...] = jnp.zeros(acc_ref.shape, jnp.float32)

    def raw_logits(kj):
        k0 = pl.multiple_of(jnp.minimum(kj, qi) * tk, tk)
        return _dot(kb_ref[pl.ds(k0, tk), :], qa_ref[...], NT_DIMS)

    per_tile = tk // pen_block if pen_block else 1
    blk = tk // per_tile

    def attend(s_ref, kj, far):
        kc = jnp.minimum(kj, qi)
        k0 = pl.multiple_of(kc * tk, tk)
        if far:
            row = bt_ref[n_cls - 1, 0:1, :]
        else:
            row = jnp.where(kj <= qi, 0.0, NEG_INF)
            bias = bt_ref[jnp.minimum(qi - kc, n_cls - 1)]
        pieces = []
        for a in range(per_tile):
            rows = slice(a * blk, (a + 1) * blk)
            add = row
            if pen_block:
                pen = pen_ref[pl.ds(kc * per_tile + a, 1), :]
                add = (pen if pen_per_head else jnp.concatenate([pen] * GROUP, axis=1)) + row
            pieces.append(s_ref[rows, :] + add if far else s_ref[rows, :] + bias[rows] + add)
        s = pieces[0] if per_tile == 1 else jnp.concatenate(pieces, axis=0)
        m_prev = m_ref[...]
        m_new = jnp.maximum(m_prev, jnp.max(s, axis=0, keepdims=True))
        p = jnp.exp2(s - m_new)
        acc_ref[...] = jnp.exp2(m_prev - m_new) * acc_ref[...] + _dot(vt_ref[:, pl.ds(k0, tk)], _bf(p))
        m_ref[...] = m_new

    k_lo = jnp.maximum(qi - k_back, 0) if k_back is not None else 0
    n_tiles = qi - k_lo + 1
    far_pairs = jnp.maximum(n_tiles - (n_cls - 1), 0) // 2 if k_back is None else 0
    sa_ref[...] = raw_logits(k_lo)

    def pair(pi, carry, far):
        ka = k_lo + 2 * pi
        sb_ref[...] = raw_logits(ka + 1)
        attend(sa_ref, ka, far)
        sa_ref[...] = raw_logits(ka + 2)
        attend(sb_ref, ka + 1, far)
        return carry

    lax.fori_loop(0, far_pairs, functools.partial(pair, far=True), 0)
    lax.fori_loop(far_pairs, (n_tiles + 1) // 2, functools.partial(pair, far=False), 0)
    acc = acc_ref[...]
    o_t = acc[0:HEAD_DIM] / jnp.maximum(acc[HEAD_DIM:HEAD_DIM + 1], 1e-30)
    cols_g = _gate_columns(gt_ref, gate_col0 + g * GROUP) if gate_col0 is not None else None
    o_ref[...] = _heads_from_lanes(o_t, tq, cols_g)


def flash_attention(q_arr, kv_arr, k_col, v_col, bias, batch, seq, *, pen=None, pen_block=0,
                    k_back=None, gate_arr=None, gate_col0=None):
    tq = ATTN_TILE
    nq = seq // tq
    cols = GROUP * tq
    assert k_back is None or k_back == bias.shape[1] - 1
    in_specs = [pl.BlockSpec((tq, GROUP * HEAD_DIM), lambda b, g, i: (b * nq + i, g)),
                pl.BlockSpec((seq, HEAD_DIM), lambda b, g, i: (b, k_col + g)),
                pl.BlockSpec((seq, HEAD_DIM), lambda b, g, i: (b, v_col + g)),
                pl.BlockSpec((None,) + bias.shape[1:], lambda b, g, i: (g, 0, 0, 0))]
    args = [q_arr, kv_arr, kv_arr, bias]
    pen_per_head = False
    if pen is not None:
        if pen.ndim == 4:
            in_specs.append(pl.BlockSpec((None, None, pen.shape[2], tq), lambda b, g, i: (b, g, 0, i)))
        else:
            pen_per_head = True
            in_specs.append(pl.BlockSpec((None, None, None, pen.shape[3], cols), lambda b, g, i: (b, g, i, 0, 0)))
        args.append(pen)
    if gate_arr is not None:
        in_specs.append(pl.BlockSpec((tq, LANE), lambda b, g, i: (b * nq + i, 0)))
        args.append(gate_arr)
    return pl.pallas_call(
        functools.partial(_flash_body, pen_block=pen_block if pen is not None else 0, pen_per_head=pen_per_head,
                          k_back=k_back, gate_col0=gate_col0 if gate_arr is not None else None, seq=seq),
        grid=(batch, KV_HEADS, nq),
        in_specs=in_specs,
        out_specs=pl.BlockSpec((tq, GROUP * HEAD_DIM), lambda b, g, i: (b * nq + i, g)),
        out_shape=jax.ShapeDtypeStruct((batch * seq, Q_WIDTH), jnp.float32),
        scratch_shapes=[pltpu.VMEM((1, cols), jnp.float32),
                        pltpu.VMEM((HEAD_DIM + SUBLANE, cols), jnp.float32),
                        pltpu.VMEM((cols, HEAD_DIM), jnp.bfloat16),
                        pltpu.VMEM((seq, HEAD_DIM), jnp.bfloat16),
                        pltpu.VMEM((HEAD_DIM + SUBLANE, seq), jnp.bfloat16),
                        pltpu.VMEM((tq, cols), jnp.float32), pltpu.VMEM((tq, cols), jnp.float32)],
        compiler_params=_cparams("parallel", "parallel", "arbitrary"),
        name="flash_attention",
    )(*args)


def _moba_gate_body(q_ref, k_ref, pen_ref, km_ref, *, n_blocks):
    qi = pl.program_id(2)
    tq = ATTN_TILE
    cols = GROUP * tq

    @pl.when(qi == 0)
    def _():
        k = k_ref[...]
        km_ref[...] = jnp.sum(k.reshape(n_blocks, MOBA_BLOCK, HEAD_DIM), axis=1) / MOBA_BLOCK

    qh, ql = _split_bf16(_stack_heads(q_ref[...]), 2)
    kh, kl = _split_bf16(km_ref[...], 2)
    gate = _dot(kh, qh, NT_DIMS) + _dot(kh, ql, NT_DIMS) + _dot(kl, qh, NT_DIMS)
    blk = lax.broadcasted_iota(jnp.int32, (n_blocks, cols), 0)
    tok = qi * tq + (lax.broadcasted_iota(jnp.int32, (n_blocks, cols), 1) & (tq - 1))
    own = lax.shift_right_logical(tok, int(math.log2(MOBA_BLOCK)))
    gate = jnp.where(blk < own, gate, NEG_INF)
    chosen = ((_rank_rows(gate, n_blocks) < MOBA_TOPK) & (blk < own)) | (blk == own)
    pen_ref[...] = jnp.where(chosen, 0.0, NEG_INF)


def moba_gate(proj, k_col, batch, seq):
    tq = ATTN_TILE
    nq = seq // tq
    cols = GROUP * tq
    nb = seq // MOBA_BLOCK
    return pl.pallas_call(
        functools.partial(_moba_gate_body, n_blocks=nb),
        grid=(batch, KV_HEADS, nq),
        in_specs=[pl.BlockSpec((tq, GROUP * HEAD_DIM), lambda b, g, i: (b * nq + i, g)),
                  pl.BlockSpec((seq, HEAD_DIM), lambda b, g, i: (b, k_col + g))],
        out_specs=pl.BlockSpec((None, None, None, nb, cols), lambda b, g, i: (b, g, i, 0, 0)),
        out_shape=jax.ShapeDtypeStruct((batch, KV_HEADS, nq, nb, cols), jnp.float32),
        scratch_shapes=[pltpu.VMEM((nb, HEAD_DIM), jnp.float32)],
        compiler_params=_cparams("parallel", "parallel", "arbitrary"),
        name="moba_gate",
    )(proj, proj)


def _tril_ones(n, strict=False):
    r = lax.broadcasted_iota(jnp.int32, (n, n), 0)
    c = lax.broadcasted_iota(jnp.int32, (n, n), 1)
    return (r > c) if strict else (r >= c)


def _chunk_cumsum(x):
    tril = _bf(_tril_ones(x.shape[0]))
    return sum(_dot(tril, part) for part in _split_bf16(x, 3))


def _head_rms_gate(o, norm_g, gate):
    ms = jnp.mean(o * o, axis=-1, keepdims=True)
    return o * lax.rsqrt(ms + NORM_EPS) * norm_g * (gate * jax.nn.sigmoid(gate))


def _hgrn2_body(q_ref, f_ref, i_ref, g_ref, lbl_ref, ng_ref, s0_ref, o_ref, s_out_ref, st_ref, *, layer):
    ti = pl.program_id(2)

    @pl.when(ti == 0)
    def _():
        st_ref[...] = s0_ref[...].T

    lbl = lbl_ref[...]
    e = jnp.exp(lbl - jnp.max(lbl, axis=0, keepdims=True))
    p = e / jnp.sum(e, axis=0, keepdims=True)
    lb = jnp.zeros((1, HG_DK), jnp.float32)
    for r in range(1, layer + 1):
        lb = lb + p[r:r + 1]
    causal = _tril_ones(CHUNK)
    work = []
    for c in range(q_ref.shape[0] // CHUNK):
        sl = slice(c * CHUNK, (c + 1) * CHUNK)
        q = q_ref[sl, :]
        qh = q * jax.nn.sigmoid(q) * HG_DK ** -0.5
        fg = lb + (1.0 - lb) * jax.nn.sigmoid(f_ref[sl, :])
        k = 1.0 - fg
        v = _bf(i_ref[sl, :])
        b = _chunk_cumsum(jnp.log(fg))
        b_mid = b[CHUNK // 2:CHUNK // 2 + 1]
        b_last = b[CHUNK - 1:CHUNK]
        a = _dot(_bf(qh * jnp.exp(b - b_mid)), _bf(k * jnp.exp(b_mid - b)), NT_DIMS)
        a = jnp.where(causal, a, 0.0)
        work.append((sl, _dot(_bf(a), v), _bf(qh * jnp.exp(b)), jnp.exp(b_last),
                     _dot(v, _bf(k * jnp.exp(b_last - b)), TN_DIMS)))
    for sl, o_intra, q_in, d_last, kv in work:
        st = st_ref[...]
        o = o_intra + _dot(q_in, _bf(st), NT_DIMS)
        st_ref[...] = st * d_last + kv
        o_ref[sl, :] = _head_rms_gate(o, ng_ref[...], g_ref[sl, :])

    @pl.when(ti == pl.num_programs(2) - 1)
    def _():
        s_out_ref[...] = st_ref[...].T


def hgrn2_scan(proj, lb_logits, norm_g, s0, layer, batch, seq):
    tt = _row_tile(seq, SCAN_TILE)
    nt = seq // tt
    h = HG_HEADS

    def col(k):
        return pl.BlockSpec((tt, HG_DK), lambda b, hh, t: (b * nt + t, k * h + hh))

    return pl.pallas_call(
        functools.partial(_hgrn2_body, layer=layer),
        grid=(batch, h, nt),
        in_specs=[col(0), col(1), col(2), col(3),
                  pl.BlockSpec((DEPTH, HG_DK), lambda b, hh, t: (0, hh)),
                  pl.BlockSpec((1, HG_DV), lambda b, hh, t: (0, 0)),
                  pl.BlockSpec((None, None, HG_DK, HG_DV), lambda b, hh, t: (b, hh, 0, 0))],
        out_specs=[pl.BlockSpec((tt, HG_DV), lambda b, hh, t: (b * nt + t, hh)),
                   pl.BlockSpec((None, None, HG_DK, HG_DV), lambda b, hh, t: (b, hh, 0, 0))],
        out_shape=[jax.ShapeDtypeStruct((batch * seq, h * HG_DV), jnp.float32),
                   jax.ShapeDtypeStruct((batch, h, HG_DK, HG_DV), jnp.float32)],
        scratch_shapes=[pltpu.VMEM((HG_DV, HG_DK), jnp.float32)],
        compiler_params=_cparams("parallel", "parallel", "arbitrary"),
        name="hgrn2_scan",
    )(proj, proj, proj, proj, lb_logits, norm_g.reshape(1, HG_DV), s0)


def _lane_column(x, lane_idx):
    lane = lax.broadcasted_iota(jnp.int32, x.shape, 1)
    return jnp.sum(jnp.where(lane == lane_idx, x, 0.0), axis=1, keepdims=True)


def _softplus(x):
    return jnp.maximum(x, 0.0) + jnp.log(1.0 + jnp.exp(-jnp.abs(x)))


def _l2n(x):
    return x * lax.rsqrt(jnp.sum(x * x, axis=-1, keepdims=True) + NORM_EPS)


def _gdn_body(q_ref, k_ref, v_ref, z_ref, t_ref, bq_ref, bk_ref, bv_ref, wq_ref, wk_ref, wv_ref,
              al_ref, dtb_ref, ng_ref, s0_ref, o_ref, s_out_ref, xs_ref, y_ref, s_ref):
    hq0 = pl.program_id(1) * GDN_HPS
    n_v = GDN_HPS * GDN_REP
    ti = pl.program_id(2)
    tt = q_ref.shape[0]
    dk, dv = GDN_DK, GDN_DV
    pad = SUBLANE

    @pl.when(ti == 0)
    def _():
        s_ref[...] = s0_ref[...]
        xs_ref[0:pad, :] = jnp.concatenate([bq_ref[...], bk_ref[...], bv_ref[...]], axis=1)

    x = jnp.concatenate([q_ref[...], k_ref[...], v_ref[...]], axis=1)
    xs_ref[pad:, :] = x
    cw = jnp.concatenate([wq_ref[...], wk_ref[...], wv_ref[...]], axis=1)
    y = xs_ref[pad - 3:pad - 3 + tt, :] * cw[0:1]
    for i in range(1, GDN_CONV - 1):
        y = y + xs_ref[pad - 3 + i:pad - 3 + i + tt, :] * cw[i:i + 1]
    y = y + x * cw[GDN_CONV - 1:GDN_CONV]
    xs_ref[0:pad, :] = x[tt - pad:tt]
    y_ref[...] = y * jax.nn.sigmoid(y)

    strict = _tril_ones(CHUNK, strict=True)
    incl = _tril_ones(CHUNK)
    sel_rows = lax.shift_right_logical(lax.broadcasted_iota(jnp.int32, (n_v * CHUNK, LANE), 0), int(math.log2(CHUNK)))
    sel_lane = lax.broadcasted_iota(jnp.int32, (n_v * CHUNK, LANE), 1)
    pick = _bf(sel_lane == GDN_V_HEADS + hq0 * GDN_REP + sel_rows)
    n_chunks = tt // CHUNK

    work = []
    for c in range(n_chunks):
        sl = slice(c * CHUNK, (c + 1) * CHUNK)
        yc = y_ref[sl, :]
        tl = t_ref[sl, :]
        beta_all = jax.nn.sigmoid(tl)
        g_all = _chunk_cumsum(-jnp.exp(al_ref[...]) * _softplus(tl + dtb_ref[...]))
        g_rows = sum(_dot(pick, part, NT_DIMS) for part in _split_bf16(g_all, 3))
        for hh in range(GDN_HPS):
            q = _l2n(yc[:, hh * dk:(hh + 1) * dk]) * dk ** -0.5
            k = _l2n(yc[:, (GDN_HPS + hh) * dk:(GDN_HPS + hh + 1) * dk])
            qb, kb = _bf(q), _bf(k)
            kk = _dot(kb, kb, NT_DIMS)
            qk = _dot(qb, kb, NT_DIMS)
            for e in range(hh * GDN_REP, (hh + 1) * GDN_REP):
                hv = hq0 * GDN_REP + e
                v = yc[:, 2 * GDN_HPS * dk + e * dv:2 * GDN_HPS * dk + (e + 1) * dv]
                bt = _lane_column(beta_all, hv)
                gc = _lane_column(g_all, GDN_V_HEADS + hv)
                gdiff = gc - g_rows[e * CHUNK:(e + 1) * CHUNK]
                decay = jnp.exp(jnp.where(incl, gdiff, 0.0))
                d_strict = jnp.where(strict, decay, 0.0)
                d_incl = jnp.where(incl, decay, 0.0)
                eg = jnp.exp(gc)
                g_last = gc[CHUNK - 1:CHUNK]
                work.append(dict(
                    c=c, e=e, sol=jnp.concatenate([bt * v, (bt * eg) * k], axis=1), pw=bt * kk * d_strict,
                    aq=_bf(qk * d_incl), q_in=_bf(q * eg), k_out=_bf(k * jnp.exp(g_last - gc)),
                    d_last=jnp.exp(g_last)))

    r_i = lax.broadcasted_iota(jnp.int32, (CHUNK, CHUNK), 0)
    c_i = lax.broadcasted_iota(jnp.int32, (CHUNK, CHUNK), 1)
    same = [lax.shift_right_logical(r_i, sh) == lax.shift_right_logical(c_i, sh) for sh in range(3, 7)]
    eye = (r_i == c_i).astype(jnp.float32)
    for wk in work:
        l8 = jnp.where(same[0], wk["pw"], 0.0)
        l8b = _bf(l8)
        wk["t"] = eye - l8
        wk["p"] = _dot(l8b, l8b)
    for wk in work:
        pb = _bf(wk["p"])
        wk["t"] = wk["t"] + _dot(_bf(wk["t"]), pb)
        wk["p"] = _dot(pb, pb)
    for wk in work:
        wk["t"] = wk["t"] + _dot(_bf(wk["t"]), _bf(wk["p"]))
    for lvl in range(1, len(same)):
        for wk in work:
            tb = _bf(wk["t"])
            off = _bf(jnp.where(same[lvl] & jnp.logical_not(same[lvl - 1]), wk["pw"], 0.0))
            wk["t"] = wk["t"] - _dot(tb, _bf(_dot(off, tb)))
    for wk in work:
        wk["sol"] = _dot(_bf(wk["t"]), _bf(wk["sol"]))

    for wk in work:
        c, e = wk["c"], wk["e"]
        sl = slice(c * CHUNK, (c + 1) * CHUNK)
        u0, w = wk["sol"][:, :dv], wk["sol"][:, dv:]
        s = s_ref[e]
        sb = _bf(s)
        u = u0 - _dot(_bf(w), sb)
        o = _dot(wk["q_in"], sb) + _dot(wk["aq"], _bf(u))
        s_ref[e] = wk["d_last"] * s + _dot(wk["k_out"], _bf(u), TN_DIMS)
        o_ref[sl, e * dv:(e + 1) * dv] = _head_rms_gate(o, ng_ref[...], z_ref[sl, e * dv:(e + 1) * dv])

    @pl.when(ti == pl.num_programs(2) - 1)
    def _():
        s_out_ref[...] = s_ref[...]


def gdn_scan(main, tail, conv_buf, conv_w, a_log, dt_bias, norm_g, s0, batch, seq):
    tt = _row_tile(seq, SCAN_TILE)
    nt = seq // tt
    hq, rep = GDN_QK_HEADS // GDN_HPS, GDN_HPS * GDN_REP
    dk, dv = GDN_HPS * GDN_DK, GDN_DV
    vw = rep * dv
    buf = jnp.pad(conv_buf, ((0, 0), (SUBLANE - (GDN_CONV - 1), 0), (0, 0)))
    pad_lanes = jnp.zeros((LANE - 2 * GDN_V_HEADS,), jnp.float32)
    a_row = jnp.concatenate([jnp.zeros((GDN_V_HEADS,), jnp.float32), a_log, pad_lanes]).reshape(1, LANE)
    dt_row = jnp.concatenate([jnp.zeros((GDN_V_HEADS,), jnp.float32), dt_bias, pad_lanes]).reshape(1, LANE)
    k0 = hq
    v0 = 2 * hq * dk // vw
    z0 = GDN_CONV_DIM // vw
    row = lambda b, h, t: b * nt + t
    return pl.pallas_call(
        _gdn_body,
        grid=(batch, hq, nt),
        in_specs=[pl.BlockSpec((tt, dk), lambda b, h, t: (row(b, h, t), h)),
                  pl.BlockSpec((tt, dk), lambda b, h, t: (row(b, h, t), k0 + h)),
                  pl.BlockSpec((tt, vw), lambda b, h, t: (row(b, h, t), v0 + h)),
                  pl.BlockSpec((tt, vw), lambda b, h, t: (row(b, h, t), z0 + h)),
                  pl.BlockSpec((tt, LANE), lambda b, h, t: (row(b, h, t), 0)),
                  pl.BlockSpec((None, SUBLANE, dk), lambda b, h, t: (b, 0, h)),
                  pl.BlockSpec((None, SUBLANE, dk), lambda b, h, t: (b, 0, k0 + h)),
                  pl.BlockSpec((None, SUBLANE, vw), lambda b, h, t: (b, 0, v0 + h)),
                  pl.BlockSpec((GDN_CONV, dk), lambda b, h, t: (0, h)),
                  pl.BlockSpec((GDN_CONV, dk), lambda b, h, t: (0, k0 + h)),
                  pl.BlockSpec((GDN_CONV, vw), lambda b, h, t: (0, v0 + h)),
                  pl.BlockSpec((1, LANE), lambda b, h, t: (0, 0)),
                  pl.BlockSpec((1, LANE), lambda b, h, t: (0, 0)),
                  pl.BlockSpec((1, dv), lambda b, h, t: (0, 0)),
                  pl.BlockSpec((None, rep, GDN_DK, dv), lambda b, h, t: (b, h, 0, 0))],
        out_specs=[pl.BlockSpec((tt, vw), lambda b, h, t: (row(b, h, t), h)),
                   pl.BlockSpec((None, rep, GDN_DK, dv), lambda b, h, t: (b, h, 0, 0))],
        out_shape=[jax.ShapeDtypeStruct((batch * seq, GDN_V_HEADS * dv), jnp.float32),
                   jax.ShapeDtypeStruct((batch, GDN_V_HEADS, GDN_DK, dv), jnp.float32)],
        scratch_shapes=[pltpu.VMEM((tt + SUBLANE, 2 * dk + vw), jnp.float32),
                        pltpu.VMEM((tt, 2 * dk + vw), jnp.float32),
                        pltpu.VMEM((rep, GDN_DK, dv), jnp.float32)],
        compiler_params=_cparams("parallel", "parallel", "arbitrary"),
        name="gdn_scan",
    )(main, main, main, main, tail, buf, buf, buf, conv_w, conv_w, conv_w, a_row, dt_row,
      norm_g.reshape(1, dv), s0)


PAGE = 128
KV_SLABS = 2 * KV_HEADS
PAGE_ROWS = PAGE * KV_SLABS
CMP_PAGES = 8
CHUNKS_PER_PAGE = PAGE // CMP_STRIDE
ROWS = GROUP * 4


def _page_slab(pg, slab):
    return pg[pl.ds(slab, PAGE, stride=KV_SLABS), :]


def _compress_part_body(*refs, n_prefetch, paged):
    refs = refs[n_prefetch:]
    pages, w_ref, o_ref, xs_ref = refs[:CMP_PAGES], refs[CMP_PAGES], refs[CMP_PAGES + 1], refs[CMP_PAGES + 2]
    per_g = CMP_PAGES * CHUNKS_PER_PAGE
    for k, pg in enumerate(pages):
        for cg in range(KV_SLABS):
            xs_ref[k, cg] = _page_slab(pg, cg) if paged else pg[:, cg * HEAD_DIM:(cg + 1) * HEAD_DIM]
    def token_rows(c, l):
        return jnp.concatenate([xs_ref.at[k, c * KV_HEADS + g][pl.ds(l, CHUNKS_PER_PAGE, stride=CMP_STRIDE), :]
                                for g in range(KV_HEADS) for k in range(CMP_PAGES)], axis=0)

    for c in range(2):
        acc = None
        for lp in range(CMP_STRIDE // 2):
            x = jnp.concatenate([token_rows(c, 2 * lp), token_rows(c, 2 * lp + 1)], axis=1)
            d = _dot(_bf(x), w_ref[c, lp])
            acc = d if acc is None else acc + d
        for g in range(KV_HEADS):
            o_ref[c, g] = acc[g * per_g:(g + 1) * per_g]


def _compress_weights(cmp_w1):
    n_part = CMP_BLOCK // CMP_STRIDE
    w = cmp_w1.reshape(2, n_part, CMP_STRIDE, HEAD_DIM, CMP_HIDDEN).transpose(0, 2, 3, 1, 4)
    return w.reshape(2, CMP_STRIDE // 2, 2 * HEAD_DIM, n_part * CMP_HIDDEN).astype(jnp.bfloat16)


def compress_part_rows(kv_arr, col_block, cmp_w1, batch, seq):
    w = _compress_weights(cmp_w1)
    pages_per_b = seq // PAGE
    steps = pages_per_b // CMP_PAGES
    per_g = CMP_PAGES * CHUNKS_PER_PAGE

    def page_spec(k):
        return pl.BlockSpec((PAGE, KV_WIDTH), lambda b, s: (b * pages_per_b + s * CMP_PAGES + k, col_block))

    return pl.pallas_call(
        functools.partial(_compress_part_body, n_prefetch=0, paged=False),
        grid=(batch, steps),
        in_specs=[page_spec(k) for k in range(CMP_PAGES)] + [pl.BlockSpec(w.shape, lambda b, s: (0, 0, 0, 0))],
        out_specs=pl.BlockSpec((None, 2, KV_HEADS, per_g, w.shape[-1]), lambda b, s: (b, 0, 0, s, 0)),
        out_shape=jax.ShapeDtypeStruct((batch, 2, KV_HEADS, seq // CMP_STRIDE, w.shape[-1]), jnp.float32),
        scratch_shapes=[pltpu.VMEM((CMP_PAGES, 2 * KV_HEADS, PAGE, HEAD_DIM), jnp.float32)],
        compiler_params=_cparams("parallel", "arbitrary"),
        name="compress_part_rows",
    )(*([kv_arr] * CMP_PAGES), w)


def compress_part_paged(pool, page_table, cmp_w1):
    w = _compress_weights(cmp_w1)
    batch, n_pages = page_table.shape
    steps = n_pages // CMP_PAGES
    per_g = CMP_PAGES * CHUNKS_PER_PAGE

    def page_spec(k):
        return pl.BlockSpec((None, PAGE_ROWS, HEAD_DIM), lambda b, s, pt: (pt[b, s * CMP_PAGES + k], 0, 0))

    return pl.pallas_call(
        functools.partial(_compress_part_body, n_prefetch=1, paged=True),
        grid_spec=pltpu.PrefetchScalarGridSpec(
            num_scalar_prefetch=1, grid=(batch, steps),
            in_specs=[page_spec(k) for k in range(CMP_PAGES)]
            + [pl.BlockSpec(w.shape, lambda b, s, pt: (0, 0, 0, 0))],
            out_specs=pl.BlockSpec((None, 2, KV_HEADS, per_g, w.shape[-1]), lambda b, s, pt: (b, 0, 0, s, 0)),
            scratch_shapes=[pltpu.VMEM((CMP_PAGES, 2 * KV_HEADS, PAGE, HEAD_DIM), jnp.float32)]),
        out_shape=jax.ShapeDtypeStruct((batch, 2, KV_HEADS, n_pages * CHUNKS_PER_PAGE, w.shape[-1]), jnp.float32),
        compiler_params=_cparams("parallel", "arbitrary"),
        name="compress_part_paged",
    )(page_table, *([pool] * CMP_PAGES), w)


def _gelu_tanh(x):
    return x * (0.5 * (1.0 + jnp.tanh(math.sqrt(2.0 / math.pi) * (x + 0.044715 * (x * x * x)))))


def _compress_finish_body(p_ref, peh_ref, w2_ref, o_ref):
    n = p_ref.shape[1]
    for c in range(2):
        p = p_ref[c]
        hid = peh_ref[c:c + 1, :] + p[:, :CMP_HIDDEN]
        hid = hid + pltpu.roll(p[:, CMP_HIDDEN:], n - 1, 0)
        o_ref[c] = _dot(_bf(_gelu_tanh(hid)), _bf(w2_ref[c]))


def compress_finish(part, cmp_w1, cmp_w2, cmp_pe):
    batch, _, _, n, width = part.shape
    pe_hid = jnp.einsum('cld,cldh->ch', cmp_pe, cmp_w1)
    return pl.pallas_call(
        _compress_finish_body,
        grid=(batch, KV_HEADS),
        in_specs=[pl.BlockSpec((None, 2, None, n, width), lambda b, g: (b, 0, g, 0, 0)),
                  pl.BlockSpec((2, CMP_HIDDEN), lambda b, g: (0, 0)),
                  pl.BlockSpec((2, CMP_HIDDEN, HEAD_DIM), lambda b, g: (0, 0, 0))],
        out_specs=pl.BlockSpec((None, None, 2, n, HEAD_DIM), lambda b, g: (b, g, 0, 0, 0)),
        out_shape=jax.ShapeDtypeStruct((batch, KV_HEADS, 2, n, HEAD_DIM), jnp.float32),
        compiler_params=_cparams("parallel", "parallel"),
        name="compress_finish",
    )(part, pe_hid, cmp_w2)


def _rows_to_col(row, n):
    eye = lax.broadcasted_iota(jnp.int32, (n, n), 0) == lax.broadcasted_iota(jnp.int32, (n, n), 1)
    return jnp.sum(jnp.where(eye, jnp.broadcast_to(row, (n, n)), 0.0), axis=1, keepdims=True)


def _sample_bias_tiles(rel_table, past, n_new):
    j = np.arange(PAGE)[:, None]
    t = np.arange(n_new)[None, :]
    far = np.full((PAGE, n_new), REL_MAX_DIST)
    first = WINDOW + t - j
    last = PAGE + t - j
    new = t - j
    tiles = []
    for dist, ok in ((far, far > 0), (first, first < WINDOW), (last, last > 0), (new, (new >= 0) & (j < n_new))):
        b = jnp.where(ok[..., None], _bias_lookup(rel_table, dist), NEG_INF)
        b = b.reshape(PAGE, n_new, KV_HEADS, GROUP).transpose(2, 0, 3, 1).reshape(KV_HEADS, PAGE, GROUP * n_new)
        tiles.append(b)
    return jnp.stack(tiles, axis=1)


def _sample_cmp_body(q_ref, kc_ref, bias_ref, gt_ref, o_ref, pen_ref, *, n_sel_blocks, past):
    n = kc_ref.shape[1]
    nbp = pen_ref.shape[0]
    q = _bf(q_ref[...] * ATTN_SCALE)
    s = _dot(_bf(kc_ref[0]), q, NT_DIMS) + bias_ref[...]
    m = jnp.max(s, axis=0, keepdims=True)
    e = jnp.exp(s - m)
    p = e / jnp.maximum(jnp.sum(e, axis=0, keepdims=True), 1e-30)
    o = _dot(_bf(p), _bf(kc_ref[1]), TN_DIMS)
    o_ref[...] = o * jax.nn.sigmoid(gt_ref[...])

    r_i = lax.broadcasted_iota(jnp.int32, (ROWS, ROWS), 0)
    c_i = lax.broadcasted_iota(jnp.int32, (ROWS, ROWS), 1)
    n_tok = ROWS // GROUP
    same_tok = _bf((r_i & (n_tok - 1)) == (c_i & (n_tok - 1)))
    ratio = SEL_BLOCK // CMP_STRIDE
    j_i = lax.broadcasted_iota(jnp.int32, (nbp, n), 0)
    k_i = lax.broadcasted_iota(jnp.int32, (nbp, n), 1)
    w = _bf((k_i >= ratio * j_i - 1) & (k_i <= ratio * j_i + ratio - 1) & (j_i < n_sel_blocks))
    imp = sum(_dot(part, same_tok) for part in _split_bf16(p, 3))
    score = sum(_dot(w, part) for part in _split_bf16(imp, 3))
    blk = lax.broadcasted_iota(jnp.int32, (nbp, ROWS), 0)
    tok = past + (lax.broadcasted_iota(jnp.int32, (nbp, ROWS), 1) & (n_tok - 1))
    cur = lax.shift_right_logical(tok, int(math.log2(SEL_BLOCK)))
    forced = (blk == 0) | (blk == cur) | (blk == cur - 1)
    causal = blk <= cur
    score = jnp.where(forced, FORCE_SCORE, score)
    score = jnp.where(causal, score, -1.0)
    nbl = -(-nbp // LANE) * LANE
    n_idx = lax.broadcasted_iota(jnp.int32, (nbp, nbl), 0)
    m_idx = lax.broadcasted_iota(jnp.int32, (nbp, nbl), 1)
    lane_tok = lax.broadcasted_iota(jnp.int32, (nbp, ROWS), 1) & (n_tok - 1)
    rank = jnp.zeros((nbp, ROWS), jnp.float32)
    for t in range(n_tok):
        s_col = score[:, t:t + 1]
        s_row = jnp.sum(jnp.where(n_idx == m_idx, s_col, 0.0), axis=0, keepdims=True)
        s_row = jnp.where(m_idx[0:1] < n_sel_blocks, s_row, -2.0)
        ahead = (s_row > s_col) | ((s_row == s_col) & (m_idx < n_idx))
        rank_t = jnp.sum(ahead.astype(jnp.float32), axis=1, keepdims=True)
        rank = jnp.where(lane_tok == t, rank_t, rank)
    chosen = (rank < N_SEL) & causal & (blk < n_sel_blocks)
    pen_ref[...] = jnp.where(chosen, 0.0, NEG_INF)


def sample_cmp_select(qs, kc, bias, gate, past, n_sel_blocks):
    batch = qs.shape[0]
    n = kc.shape[3]
    nbp = -(-n_sel_blocks // SUBLANE) * SUBLANE
    return pl.pallas_call(
        functools.partial(_sample_cmp_body, n_sel_blocks=n_sel_blocks, past=past),
        grid=(batch, KV_HEADS),
        in_specs=[pl.BlockSpec((None, None, ROWS, HEAD_DIM), lambda b, g: (b, g, 0, 0)),
                  pl.BlockSpec((None, None, 2, n, HEAD_DIM), lambda b, g: (b, g, 0, 0, 0)),
                  pl.BlockSpec((None, n, ROWS), lambda b, g: (g, 0, 0)),
                  pl.BlockSpec((None, None, ROWS, HEAD_DIM), lambda b, g: (b, g, 0, 0))],
        out_specs=[pl.BlockSpec((None, None, ROWS, HEAD_DIM), lambda b, g: (b, g, 0, 0)),
                   pl.BlockSpec((None, None, nbp, ROWS), lambda b, g: (b, g, 0, 0))],
        out_shape=[jax.ShapeDtypeStruct((batch, KV_HEADS, ROWS, HEAD_DIM), jnp.float32),
                   jax.ShapeDtypeStruct((batch, KV_HEADS, nbp, ROWS), jnp.float32)],
        compiler_params=_cparams("parallel", "parallel"),
        name="sample_cmp_select",
    )(qs, kc, bias, gate)


ATTN_PAGES = 4


def _paged_attn_body(*refs, pen_block, gated, n_pages):
    it = iter(refs)
    pt_ref, tid_ref, q_ref = next(it), next(it), next(it)
    pages = [next(it) for _ in range(ATTN_PAGES)]
    new_ref, bt_ref = next(it), next(it)
    pen_ref = next(it) if pen_block else None
    gt_ref = next(it) if gated else None
    o_ref, m_ref, l_ref, acc_ref = next(it), next(it), next(it), next(it)
    step = pl.program_id(1)

    @pl.when(step == 0)
    def _():
        m_ref[...] = jnp.full(m_ref.shape, NEG_INF, jnp.float32)
        l_ref[...] = jnp.zeros(l_ref.shape, jnp.float32)
        acc_ref[...] = jnp.zeros(acc_ref.shape, jnp.float32)

    def page_terms(g, page, n_keys):
        tile = tid_ref[page]
        b = bt_ref[g, tile] if n_keys == PAGE else bt_ref[g, tile, 0:n_keys, :]
        if not pen_block:
            return b
        if pen_block >= PAGE:
            return b + pen_ref[g, pl.ds(page // (pen_block // PAGE), 1), :]
        per_page = PAGE // pen_block
        pieces = [b[a * pen_block:min((a + 1) * pen_block, n_keys)] + pen_ref[g, pl.ds(page * per_page + a, 1), :]
                  for a in range(-(-n_keys // pen_block))]
        return pieces[0] if len(pieces) == 1 else jnp.concatenate(pieces, axis=0)

    def attend(g, k, v, terms):
        s = _dot(_bf(k), _bf(q_ref[g] * ATTN_SCALE), NT_DIMS) + terms
        m_prev = m_ref[g]
        m_new = jnp.maximum(m_prev, jnp.max(s, axis=0, keepdims=True))
        alpha = jnp.exp(m_prev - m_new)
        p = jnp.exp(s - m_new)
        l_ref[g] = alpha * l_ref[g] + jnp.sum(p, axis=0, keepdims=True)
        acc_ref[g] = _rows_to_col(alpha, ROWS) * acc_ref[g] + _dot(_bf(p), _bf(v), TN_DIMS)
        m_ref[g] = m_new

    half = KV_HEADS * HEAD_DIM
    for kk in range(ATTN_PAGES):
        for g in range(KV_HEADS):
            attend(g, _page_slab(pages[kk], g), _page_slab(pages[kk], KV_HEADS + g),
                   page_terms(g, step * ATTN_PAGES + kk, PAGE))

    @pl.when(step == pl.num_programs(1) - 1)
    def _():
        n_new = new_ref.shape[0]
        for g in range(KV_HEADS):
            attend(g, new_ref[:, g * HEAD_DIM:(g + 1) * HEAD_DIM],
                   new_ref[:, half + g * HEAD_DIM:half + (g + 1) * HEAD_DIM], page_terms(g, n_pages, n_new))
            o = acc_ref[g] / _rows_to_col(jnp.maximum(l_ref[g], 1e-30), ROWS)
            if gated:
                o = o * jax.nn.sigmoid(gt_ref[g])
            o_ref[g] = o


def paged_attention(qs, pool, page_table, tile_ids, new_kv, bias_tiles, *, pen=None, pen_block=0, gate=None):
    batch, n_pages = page_table.shape
    steps = n_pages // ATTN_PAGES
    n_new = new_kv.shape[1]

    def page_spec(k):
        return pl.BlockSpec((None, PAGE_ROWS, HEAD_DIM), lambda b, s, pt, tid: (pt[b, s * ATTN_PAGES + k], 0, 0))

    grp = lambda b, s, pt, tid: (b, 0, 0, 0)
    in_specs = ([pl.BlockSpec((None, KV_HEADS, ROWS, HEAD_DIM), grp)] + [page_spec(k) for k in range(ATTN_PAGES)]
                + [pl.BlockSpec((None, n_new, KV_WIDTH), lambda b, s, pt, tid: (b, 0, 0)),
                   pl.BlockSpec(bias_tiles.shape, lambda b, s, pt, tid: (0, 0, 0, 0))])
    args = [qs] + [pool] * ATTN_PAGES + [new_kv, bias_tiles]
    if pen is not None:
        in_specs.append(pl.BlockSpec((None,) + pen.shape[1:], grp))
        args.append(pen)
    if gate is not None:
        in_specs.append(pl.BlockSpec((None, KV_HEADS, ROWS, HEAD_DIM), grp))
        args.append(gate)
    return pl.pallas_call(
        functools.partial(_paged_attn_body, pen_block=pen_block if pen is not None else 0, gated=gate is not None,
                          n_pages=n_pages),
        grid_spec=pltpu.PrefetchScalarGridSpec(
            num_scalar_prefetch=2, grid=(batch, steps), in_specs=in_specs,
            out_specs=pl.BlockSpec((None, KV_HEADS, ROWS, HEAD_DIM), grp),
            scratch_shapes=[pltpu.VMEM((KV_HEADS, 1, ROWS), jnp.float32), pltpu.VMEM((KV_HEADS, 1, ROWS), jnp.float32),
                            pltpu.VMEM((KV_HEADS, ROWS, HEAD_DIM), jnp.float32)]),
        out_shape=jax.ShapeDtypeStruct((batch, KV_HEADS, ROWS, HEAD_DIM), jnp.float32),
        compiler_params=_cparams("parallel", "arbitrary"),
        name="paged_attention",
    )(page_table, tile_ids, *args)


def _moba_sample_gate_body(pt_ref, q_ref, *refs, n_blocks, past):
    pages, (pen_ref, km_ref) = refs[:ATTN_PAGES], refs[ATTN_PAGES:]
    step = pl.program_id(1)
    per_block = MOBA_BLOCK // PAGE

    @pl.when(step == 0)
    def _():
        km_ref[...] = jnp.zeros(km_ref.shape, jnp.float32)

    for kk in range(ATTN_PAGES):
        blk = (step * ATTN_PAGES + kk) // per_block
        slab_sums = jnp.sum(pages[kk][...], axis=0)
        for g in range(KV_HEADS):
            km_ref[g, pl.ds(blk, 1), :] += slab_sums[g:g + 1]

    @pl.when(step == pl.num_programs(1) - 1)
    def _():
        nbp = pen_ref.shape[1]
        blk = lax.broadcasted_iota(jnp.int32, (nbp, ROWS), 0)
        n_tok = ROWS // GROUP
        tok = past + (lax.broadcasted_iota(jnp.int32, (nbp, ROWS), 1) & (n_tok - 1))
        own = lax.shift_right_logical(tok, int(math.log2(MOBA_BLOCK)))
        for g in range(KV_HEADS):
            kh, kl = _split_bf16(km_ref[g] / MOBA_BLOCK, 2)
            qh, ql = _split_bf16(q_ref[g], 2)
            gate = _dot(kh, qh, NT_DIMS) + _dot(kh, ql, NT_DIMS) + _dot(kl, qh, NT_DIMS)
            gate = jnp.where(blk < own, gate, NEG_INF)
            chosen = ((_rank_rows(gate, n_blocks) < MOBA_TOPK) & (blk < own)) | (blk == own)
            pen_ref[g] = jnp.where(chosen, 0.0, NEG_INF)


def moba_sample_gate(qs, pool, page_table, past, n_new):
    batch, n_pages = page_table.shape
    steps = n_pages // ATTN_PAGES
    n_blocks = -(-(past + n_new) // MOBA_BLOCK)
    nbp = -(-n_blocks // SUBLANE) * SUBLANE

    def page_spec(k):
        return pl.BlockSpec((None, PAGE, None, KV_HEADS, HEAD_DIM),
                            lambda b, s, pt: (pt[b, s * ATTN_PAGES + k], 0, 0, 0, 0))

    return pl.pallas_call(
        functools.partial(_moba_sample_gate_body, n_blocks=n_blocks, past=past),
        grid_spec=pltpu.PrefetchScalarGridSpec(
            num_scalar_prefetch=1, grid=(batch, steps),
            in_specs=[pl.BlockSpec((None, KV_HEADS, ROWS, HEAD_DIM), lambda b, s, pt: (b, 0, 0, 0))]
            + [page_spec(k) for k in range(ATTN_PAGES)],
            out_specs=pl.BlockSpec((None, KV_HEADS, nbp, ROWS), lambda b, s, pt: (b, 0, 0, 0)),
            scratch_shapes=[pltpu.VMEM((KV_HEADS, nbp, HEAD_DIM), jnp.float32)]),
        out_shape=jax.ShapeDtypeStruct((batch, KV_HEADS, nbp, ROWS), jnp.float32),
        compiler_params=_cparams("parallel", "arbitrary"),
        name="moba_sample_gate",
    )(page_table, qs, *([pool] * ATTN_PAGES))


def _sample_rows(x, batch, n_tok):
    return x.reshape(batch, n_tok, KV_HEADS, GROUP, HEAD_DIM).transpose(0, 2, 3, 1, 4).reshape(
        batch, KV_HEADS, GROUP * n_tok, HEAD_DIM)


def _sample_unrows(o, batch, n_tok):
    return o.reshape(batch, KV_HEADS, GROUP, n_tok, HEAD_DIM).transpose(0, 3, 1, 2, 4).reshape(batch * n_tok, Q_WIDTH)


def _sample_gate_rows(tail, branch, batch, n_tok):
    gt = tail[:, branch * N_HEADS:(branch + 1) * N_HEADS].reshape(batch, n_tok, KV_HEADS, GROUP)
    gt = gt.transpose(0, 2, 3, 1).reshape(batch, KV_HEADS, GROUP * n_tok, 1)
    return jnp.broadcast_to(gt, (batch, KV_HEADS, GROUP * n_tok, HEAD_DIM))


def _pad_new(kv_new, batch, n_tok):
    return jnp.pad(kv_new.reshape(batch, n_tok, KV_WIDTH), ((0, 0), (0, SUBLANE - n_tok), (0, 0)))


def _nsa_sample_pallas(main, tail, cache_c, cache_s, cache_w, page_table, cmp_w1, cmp_w2, cmp_pe, rel_table):
    batch, n_pages = page_table.shape
    n_tok = main.shape[0] // batch
    past = n_pages * PAGE
    wbuf = cache_w.shape[1]
    assert n_tok == ROWS // GROUP and cache_c.shape[1] == PAGE and wbuf == WINDOW and WINDOW % PAGE == 0
    n_cmp = (past + n_tok - CMP_BLOCK) // CMP_STRIDE + 1
    assert n_cmp + CMP_BLOCK // CMP_STRIDE - 1 == past // CMP_STRIDE
    qs = _sample_rows(main[:, :Q_WIDTH], batch, n_tok)
    kv_new = [main[:, Q_WIDTH + c * KV_WIDTH:Q_WIDTH + (c + 1) * KV_WIDTH] for c in range(3)]
    tiles = _sample_bias_tiles(rel_table, past, n_tok)
    flat = lambda pool: pool.reshape(pool.shape[0], PAGE_ROWS, HEAD_DIM)

    kc = compress_finish(compress_part_paged(flat(cache_c), page_table, cmp_w1), cmp_w1, cmp_w2, cmp_pe)
    n = kc.shape[3]
    dist = past + np.arange(n_tok)[None, :] - (np.arange(n)[:, None] * CMP_STRIDE + CMP_BLOCK - 1)
    ok = (dist >= 0) & (np.arange(n)[:, None] < n_cmp)
    cb = jnp.where(ok[..., None], _bias_lookup(rel_table, dist), NEG_INF)
    cb = cb.reshape(n, n_tok, KV_HEADS, GROUP).transpose(2, 0, 3, 1).reshape(KV_HEADS, n, GROUP * n_tok)
    n_sel_blocks = -(-(past + n_tok) // SEL_BLOCK)
    o_cmp, pen = sample_cmp_select(qs, kc, cb, _sample_gate_rows(tail, 0, batch, n_tok), past, n_sel_blocks)

    far_then_last = jnp.asarray([0] * (n_pages - 1) + [2, 3], jnp.int32)
    o_sel = paged_attention(qs, flat(cache_s), page_table, far_then_last, _pad_new(kv_new[1], batch, n_tok), tiles,
                            pen=pen, pen_block=SEL_BLOCK, gate=_sample_gate_rows(tail, 1, batch, n_tok))
    w_pages = wbuf // PAGE
    win_table = jnp.arange(batch * w_pages, dtype=jnp.int32).reshape(batch, w_pages)
    win_tiles = jnp.asarray([1] + [0] * (w_pages - 2) + [2, 3], jnp.int32)
    o_win = paged_attention(qs, cache_w.reshape(batch * w_pages, PAGE_ROWS, HEAD_DIM), win_table, win_tiles,
                            _pad_new(kv_new[2], batch, n_tok), tiles, gate=_sample_gate_rows(tail, 2, batch, n_tok))
    outs = [_sample_unrows(o, batch, n_tok) for o in (o_cmp, o_sel, o_win)]
    shape = (batch, n_tok, 2, KV_HEADS, HEAD_DIM)
    new_win = jnp.concatenate([cache_w[:, n_tok:], kv_new[2].reshape(shape)], axis=1)
    return outs, kv_new[0].reshape(shape), kv_new[1].reshape(shape), new_win


def _moba_sample_pallas(proj, cache_kv, page_table, rel_table):
    batch, n_pages = page_table.shape
    n_tok = proj.shape[0] // batch
    past = n_pages * PAGE
    assert n_tok == ROWS // GROUP and (past // MOBA_BLOCK) * MOBA_BLOCK == past
    qs = _sample_rows(proj[:, :Q_WIDTH], batch, n_tok)
    kv_new = proj[:, Q_WIDTH:]
    pool = cache_kv.reshape(cache_kv.shape[0], PAGE_ROWS, HEAD_DIM)
    pen = moba_sample_gate(qs, cache_kv, page_table, past, n_tok)
    tile_ids = jnp.asarray([0] * (n_pages - 1) + [2, 3], jnp.int32)
    o = paged_attention(qs, pool, page_table, tile_ids, _pad_new(kv_new, batch, n_tok),
                        _sample_bias_tiles(rel_table, past, n_tok), pen=pen, pen_block=MOBA_BLOCK)
    return _sample_unrows(o, batch, n_tok), kv_new.reshape(batch, n_tok, 2, KV_HEADS, HEAD_DIM)


def _cols_to_row(col, n):
    eye = lax.broadcasted_iota(jnp.int32, (n, n), 0) == lax.broadcasted_iota(jnp.int32, (n, n), 1)
    return jnp.sum(jnp.where(eye, jnp.broadcast_to(col, (n, n)), 0.0), axis=0, keepdims=True)


def _hgrn2_step_body(q_ref, f_ref, i_ref, g_ref, lbl_ref, ng_ref, s0_ref, o_ref, s_out_ref, *, layer):
    n_tok = q_ref.shape[0]
    lbl = lbl_ref[...]
    e = jnp.exp(lbl - jnp.max(lbl, axis=0, keepdims=True))
    p = e / jnp.sum(e, axis=0, keepdims=True)
    lb = jnp.zeros((1, HG_DK), jnp.float32)
    for r in range(1, layer + 1):
        lb = lb + p[r:r + 1]
    q = q_ref[...]
    qh = q * jax.nn.sigmoid(q) * HG_DK ** -0.5
    fg = lb + (1.0 - lb) * jax.nn.sigmoid(f_ref[...])
    k = 1.0 - fg
    v = i_ref[...]
    st = s0_ref[...].T
    rows = []
    for t in range(n_tok):
        st = st * fg[t:t + 1] + _rows_to_col(v[t:t + 1], HG_DV) * k[t:t + 1]
        rows.append(_cols_to_row(jnp.sum(st * qh[t:t + 1], axis=1, keepdims=True), HG_DV))
    o_ref[...] = _head_rms_gate(jnp.concatenate(rows, axis=0), ng_ref[...], g_ref[...])
    s_out_ref[...] = st.T


def hgrn2_step(proj, lb_logits, norm_g, s0, layer):
    batch, n_tok, _ = proj.shape
    h = HG_HEADS

    def col(k):
        return pl.BlockSpec((None, n_tok, HG_DK), lambda b, hh: (b, 0, k * h + hh))

    return pl.pallas_call(
        functools.partial(_hgrn2_step_body, layer=layer),
        grid=(batch, h),
        in_specs=[col(0), col(1), col(2), col(3),
                  pl.BlockSpec((DEPTH, HG_DK), lambda b, hh: (0, hh)),
                  pl.BlockSpec((1, HG_DV), lambda b, hh: (0, 0)),
                  pl.BlockSpec((None, None, HG_DK, HG_DV), lambda b, hh: (b, hh, 0, 0))],
        out_specs=[pl.BlockSpec((None, n_tok, HG_DV), lambda b, hh: (b, 0, hh)),
                   pl.BlockSpec((None, None, HG_DK, HG_DV), lambda b, hh: (b, hh, 0, 0))],
        out_shape=[jax.ShapeDtypeStruct((batch, n_tok, h * HG_DV), jnp.float32),
                   jax.ShapeDtypeStruct((batch, h, HG_DK, HG_DV), jnp.float32)],
        compiler_params=_cparams("parallel", "parallel"),
        name="hgrn2_step",
    )(proj, proj, proj, proj, lb_logits, norm_g.reshape(1, HG_DV), s0)


def _gdn_step_body(q_ref, k_ref, v_ref, z_ref, t_ref, bq_ref, bk_ref, bv_ref, wq_ref, wk_ref, wv_ref,
                   al_ref, dtb_ref, ng_ref, s0_ref, o_ref, s_out_ref, xs_ref):
    hq = pl.program_id(1)
    n_tok = q_ref.shape[0]
    dk, dv = GDN_DK, GDN_DV
    pad = SUBLANE
    xs_ref[0:pad, :] = jnp.concatenate([bq_ref[...], bk_ref[...], bv_ref[...]], axis=1)
    x = jnp.concatenate([q_ref[...], k_ref[...], v_ref[...]], axis=1)
    xs_ref[pad:pad + n_tok, :] = x
    cw = jnp.concatenate([wq_ref[...], wk_ref[...], wv_ref[...]], axis=1)
    y = xs_ref[pad - 3:pad - 3 + n_tok, :] * cw[0:1]
    for i in range(1, GDN_CONV - 1):
        y = y + xs_ref[pad - 3 + i:pad - 3 + i + n_tok, :] * cw[i:i + 1]
    y = y + x * cw[GDN_CONV - 1:GDN_CONV]
    y = y * jax.nn.sigmoid(y)
    q = _l2n(y[:, 0:dk]) * dk ** -0.5
    k = _l2n(y[:, dk:2 * dk])
    tl = t_ref[...]
    beta_all = jax.nn.sigmoid(tl)
    la_all = -jnp.exp(al_ref[...]) * _softplus(tl + dtb_ref[...])
    for e in range(GDN_REP):
        hv = hq * GDN_REP + e
        v = y[:, 2 * dk + e * dv:2 * dk + (e + 1) * dv]
        bt = _lane_column(beta_all, hv)
        a = jnp.exp(_lane_column(la_all, GDN_V_HEADS + hv))
        s = s0_ref[e]
        rows = []
        for t in range(n_tok):
            k_col = _rows_to_col(k[t:t + 1], dk)
            ks = jnp.sum(s * k_col, axis=0, keepdims=True)
            u = bt[t:t + 1] * (v[t:t + 1] - a[t:t + 1] * ks)
            s = a[t:t + 1] * s + k_col * u
            rows.append(jnp.sum(s * _rows_to_col(q[t:t + 1], dk), axis=0, keepdims=True))
        s_out_ref[e] = s
        o_ref[:, e * dv:(e + 1) * dv] = _head_rms_gate(jnp.concatenate(rows, axis=0), ng_ref[...],
                                                       z_ref[:, e * dv:(e + 1) * dv])


def gdn_step(main, tail, conv_buf, conv_w, a_log, dt_bias, norm_g, s0):
    batch, n_tok, _ = main.shape
    hq, rep, dk, dv = GDN_QK_HEADS, GDN_REP, GDN_DK, GDN_DV
    vw = rep * dv
    buf = jnp.pad(conv_buf, ((0, 0), (SUBLANE - (GDN_CONV - 1), 0), (0, 0)))
    pad_lanes = jnp.zeros((LANE - 2 * GDN_V_HEADS,), jnp.float32)
    a_row = jnp.concatenate([jnp.zeros((GDN_V_HEADS,), jnp.float32), a_log, pad_lanes]).reshape(1, LANE)
    dt_row = jnp.concatenate([jnp.zeros((GDN_V_HEADS,), jnp.float32), dt_bias, pad_lanes]).reshape(1, LANE)
    k0 = hq
    v0 = 2 * hq * dk // vw
    z0 = GDN_CONV_DIM // vw
    return pl.pallas_call(
        _gdn_step_body,
        grid=(batch, hq),
        in_specs=[pl.BlockSpec((None, n_tok, dk), lambda b, h: (b, 0, h)),
                  pl.BlockSpec((None, n_tok, dk), lambda b, h: (b, 0, k0 + h)),
                  pl.BlockSpec((None, n_tok, vw), lambda b, h: (b, 0, v0 + h)),
                  pl.BlockSpec((None, n_tok, vw), lambda b, h: (b, 0, z0 + h)),
                  pl.BlockSpec((None, n_tok, LANE), lambda b, h: (b, 0, 0)),
                  pl.BlockSpec((None, SUBLANE, dk), lambda b, h: (b, 0, h)),
                  pl.BlockSpec((None, SUBLANE, dk), lambda b, h: (b, 0, k0 + h)),
                  pl.BlockSpec((None, SUBLANE, vw), lambda b, h: (b, 0, v0 + h)),
                  pl.BlockSpec((GDN_CONV, dk), lambda b, h: (0, h)),
                  pl.BlockSpec((GDN_CONV, dk), lambda b, h: (0, k0 + h)),
                  pl.BlockSpec((GDN_CONV, vw), lambda b, h: (0, v0 + h)),
                  pl.BlockSpec((1, LANE), lambda b, h: (0, 0)),
                  pl.BlockSpec((1, LANE), lambda b, h: (0, 0)),
                  pl.BlockSpec((1, dv), lambda b, h: (0, 0)),
                  pl.BlockSpec((None, rep, dk, dv), lambda b, h: (b, h, 0, 0))],
        out_specs=[pl.BlockSpec((None, n_tok, vw), lambda b, h: (b, 0, h)),
                   pl.BlockSpec((None, rep, dk, dv), lambda b, h: (b, h, 0, 0))],
        out_shape=[jax.ShapeDtypeStruct((batch, n_tok, GDN_V_HEADS * dv), jnp.float32),
                   jax.ShapeDtypeStruct((batch, GDN_V_HEADS, dk, dv), jnp.float32)],
        scratch_shapes=[pltpu.VMEM((2 * SUBLANE, 2 * dk + vw), jnp.float32)],
        compiler_params=_cparams("parallel", "parallel"),
        name="gdn_step",
    )(main, main, main, main, tail, buf, buf, buf, conv_w, conv_w, conv_w, a_row, dt_row, norm_g.reshape(1, dv), s0)


def _pad_cols(w, mult=LANE):
    n = w.shape[1]
    return jnp.pad(w, ((0, 0), (0, (-n) % mult)))


def _nsa_prompt(main, kv_rows, tail, batch, seq, cmp_w1, cmp_w2, cmp_pe, rel_table):
    kv_c, kv_s, kv_w = (kv_rows[c].reshape(batch, seq, 2, KV_HEADS, HEAD_DIM) for c in range(3))
    kc = compress_finish(compress_part_rows(main, Q_WIDTH // KV_WIDTH, cmp_w1, batch, seq), cmp_w1, cmp_w2, cmp_pe)
    o_cmp, pen = cmp_select(main, tail, kc, _cmp_bias_table(rel_table, kc.shape[3]), batch, seq)
    col = Q_WIDTH // HEAD_DIM
    o_sel = flash_attention(main, main, col + 2 * KV_HEADS, col + 3 * KV_HEADS, _flash_bias_tiles(rel_table, 0),
                            batch, seq, pen=pen, pen_block=SEL_BLOCK, gate_arr=tail, gate_col0=N_HEADS)
    o_win = flash_attention(main, main, col + 4 * KV_HEADS, col + 5 * KV_HEADS,
                            _flash_bias_tiles(rel_table, WINDOW), batch, seq, k_back=WINDOW // ATTN_TILE,
                            gate_arr=tail, gate_col0=2 * N_HEADS)
    return [o_cmp, o_sel, o_win], kv_c, kv_s, kv_w[:, -min(WINDOW, seq):]


def _moba_prompt(proj, batch, seq, rel_table):
    col = Q_WIDTH // HEAD_DIM
    pen = moba_gate(proj, col, batch, seq)
    return flash_attention(proj, proj, col, col + KV_HEADS, _flash_bias_tiles(rel_table, 0), batch, seq,
                           pen=pen, pen_block=MOBA_BLOCK)


def kernel(x_prompt, x_sample, cache_nsa_cmp_kv, cache_nsa_sel_kv, cache_nsa_win_kv, cache_moba_kv,
           state_hgrn2, state_gdn_conv, state_gdn_ssm, page_table, rel_table, ln_mix, ln_ffn, ln_final,
           ffn_w_up, ffn_w_down, nsa_w_in, nsa_cmp_w1, nsa_cmp_w2, nsa_cmp_pe, nsa_w_out, moba_w_in, moba_w_out,
           hg_w_in, hg_lb_logits, hg_norm, hg_w_out, gdn_w_in, gdn_conv_w, gdn_a_log, gdn_dt_bias, gdn_norm,
           gdn_w_out):
    bf = jnp.bfloat16
    bp, tp = x_prompt.shape[:2]
    bs, ts = x_sample.shape[:2]
    assert tp % ATTN_TILE == 0 and WINDOW % ATTN_TILE == 0 and ATTN_TILE == MOBA_BLOCK and tp % SCAN_TILE == 0
    xp = x_prompt.reshape(bp * tp, D_MODEL)
    xs = x_sample.reshape(bs * ts, D_MODEL)
    w_up, w_down = ffn_w_up.astype(bf), ffn_w_down.astype(bf)

    for layer in range(DEPTH):
        kind = layer % N_MIXERS
        g_mix = ln_mix[layer]
        if kind == 0:
            w_main, w_tail = nsa_w_in.astype(bf), _pad_cols(nsa_w_in[:, NSA_MAIN:]).astype(bf)
            main_p, kv_p = norm_matmul(xp, g_mix, w_main, kv_from=Q_WIDTH // KV_WIDTH, n_cols=NSA_MAIN)
            tail_p = norm_matmul(xp, g_mix, w_tail)
            main_s, tail_s = norm_matmul(xs, g_mix, w_main, n_cols=NSA_MAIN), norm_matmul(xs, g_mix, w_tail)
            op, nsa_cmp_p, nsa_sel_p, nsa_win_p = _nsa_prompt(main_p, kv_p, tail_p, bp, tp, nsa_cmp_w1, nsa_cmp_w2,
                                                              nsa_cmp_pe, rel_table)
            os_, nsa_cmp_s, nsa_sel_s, nsa_win_s = _nsa_sample_pallas(main_s, tail_s, cache_nsa_cmp_kv,
                                                                     cache_nsa_sel_kv, cache_nsa_win_kv, page_table,
                                                                     nsa_cmp_w1, nsa_cmp_w2, nsa_cmp_pe, rel_table)
            w_out = nsa_w_out.astype(bf)
        elif kind == 1:
            w_in = moba_w_in.astype(bf)
            pp, kv_p = norm_matmul(xp, g_mix, w_in, kv_from=Q_WIDTH // KV_WIDTH)
            op, moba_p = _moba_prompt(pp, bp, tp, rel_table), kv_p.reshape(bp, tp, 2, KV_HEADS, HEAD_DIM)
            os_, moba_s = _moba_sample_pallas(norm_matmul(xs, g_mix, w_in), cache_moba_kv, page_table, rel_table)
            op, os_ = [op], [os_]
            w_out = moba_w_out.astype(bf)
        elif kind == 2:
            w_in = hg_w_in.astype(bf)
            pp = norm_matmul(xp, g_mix, w_in)
            ps = norm_matmul(xs, g_mix, w_in).reshape(bs, ts, -1)
            s0 = jnp.zeros((bp, HG_HEADS, HG_DK, HG_DV), jnp.float32)
            op, hg_p = hgrn2_scan(pp, hg_lb_logits, hg_norm, s0, layer, bp, tp)
            os_, hg_s = hgrn2_step(ps, hg_lb_logits, hg_norm, state_hgrn2, layer)
            op, os_ = [op], [os_.reshape(bs * ts, -1)]
            w_out = hg_w_out.astype(bf)
        else:
            w_main, w_tail = gdn_w_in.astype(bf), _pad_cols(gdn_w_in[:, GDN_MAIN:]).astype(bf)
            main_p, tail_p = norm_matmul(xp, g_mix, w_main, n_cols=GDN_MAIN), norm_matmul(xp, g_mix, w_tail)
            main_s = norm_matmul(xs, g_mix, w_main, n_cols=GDN_MAIN).reshape(bs, ts, -1)
            tail_s = norm_matmul(xs, g_mix, w_tail).reshape(bs, ts, -1)
            buf0 = jnp.zeros((bp, GDN_CONV - 1, GDN_CONV_DIM), jnp.float32)
            s0 = jnp.zeros((bp, GDN_V_HEADS, GDN_DK, GDN_DV), jnp.float32)
            op, ssm_p = gdn_scan(main_p, tail_p, buf0, gdn_conv_w, gdn_a_log, gdn_dt_bias, gdn_norm, s0, bp, tp)
            conv_p = main_p.reshape(bp, tp, -1)[:, tp - (GDN_CONV - 1):, :GDN_CONV_DIM]
            os_, ssm_s = gdn_step(main_s, tail_s, state_gdn_conv, gdn_conv_w, gdn_a_log, gdn_dt_bias, gdn_norm,
                                  state_gdn_ssm)
            conv_s = jnp.concatenate([state_gdn_conv, main_s[:, :, :GDN_CONV_DIM]], axis=1)[:, ts:]
            op, os_ = [op], [os_.reshape(bs * ts, -1)]
            w_out = gdn_w_out.astype(bf)
        xp = matmul_res(op, w_out, xp)
        xs = matmul_res(os_, w_out, xs)
        xp = ffn(xp, ln_ffn[layer], w_up, w_down, layer)
        xs = ffn(xs, ln_ffn[layer], w_up, w_down, layer)
    y_prompt = final_norm(xp, ln_final).reshape(bp, tp, D_MODEL)
    y_sample = final_norm(xs, ln_final).reshape(bs, ts, D_MODEL)
    return (y_prompt, y_sample, nsa_cmp_p, nsa_cmp_s, nsa_sel_p, nsa_sel_s, nsa_win_p, nsa_win_s,
            moba_p, moba_s, hg_p, hg_s, conv_p, conv_s, ssm_p, ssm_s)
```

```python
import functools
import math

import jax
import jax.numpy as jnp
import numpy as np
from jax import lax
from jax.experimental import pallas as pl
from jax.experimental.pallas import tpu as pltpu

D_MODEL = 2048
DEPTH = 4
N_MIXERS = 4
HEAD_DIM = 128
N_HEADS = D_MODEL // HEAD_DIM
KV_HEADS = 4
GROUP = N_HEADS // KV_HEADS
ATTN_SCALE = HEAD_DIM ** -0.5
REL_BUCKETS = 32
REL_MAX_DIST = 128
CMP_BLOCK = 32
CMP_STRIDE = 16
CMP_HIDDEN = HEAD_DIM
SEL_BLOCK = 64
N_SEL = 16
WINDOW = 512
FORCE_SCORE = 1.0e4
MOBA_BLOCK = 256
MOBA_TOPK = 3
HG_DK = 128
HG_HEADS = D_MODEL // HG_DK
HG_DV = D_MODEL // HG_HEADS
GDN_DK = 128
GDN_DV = 128
GDN_QK_HEADS = D_MODEL // GDN_DK
GDN_V_HEADS = 2 * GDN_QK_HEADS
GDN_REP = GDN_V_HEADS // GDN_QK_HEADS
GDN_CONV = 4
GDN_CONV_DIM = 2 * GDN_QK_HEADS * GDN_DK + GDN_V_HEADS * GDN_DV
CHUNK = 64
NEG_INF = -1.0e30
NORM_EPS = 1e-6

Q_WIDTH = N_HEADS * HEAD_DIM
KV_WIDTH = 2 * KV_HEADS * HEAD_DIM
NSA_MAIN = Q_WIDTH + 3 * KV_WIDTH
GDN_MAIN = GDN_CONV_DIM + GDN_V_HEADS * GDN_DV

V7X_VMEM_LIMIT_BYTES = 56 * 1024 * 1024
LANE = 128
SUBLANE = 8
ATTN_TILE = 256
CMP_TILE = 128
SCAN_TILE = 512
GDN_HPS = 2
LOG2E = math.log2(math.e)
NT_DIMS = (((1,), (1,)), ((), ()))
TN_DIMS = (((0,), (0,)), ((), ()))


def _cparams(*sem):
    return pltpu.CompilerParams(dimension_semantics=sem, vmem_limit_bytes=V7X_VMEM_LIMIT_BYTES)


def _row_tile(m, target):
    t = min(m, target)
    while m % t:
        t //= 2
    return t


def _col_tile(n, target):
    t = min(n, target)
    while n % t or t % LANE:
        t -= LANE
    return t


def _split_bf16(x, parts):
    out = []
    for _ in range(parts - 1):
        hi = x.astype(jnp.bfloat16)
        out.append(hi)
        x = x - hi.astype(jnp.float32)
    out.append(x.astype(jnp.bfloat16))
    return out


def _bf(x):
    return x.astype(jnp.bfloat16)


def _dot(a, b, dims=None):
    if dims is None:
        return jnp.dot(a, b, preferred_element_type=jnp.float32)
    return lax.dot_general(a, b, dims, preferred_element_type=jnp.float32)


def _norm_matmul_body(x_ref, g_ref, w_ref, o_ref, *rest, kv_from):
    kv_ref, h_ref = rest if kv_from is not None else (None, rest[0])
    j = pl.program_id(1)

    @pl.when(j == 0)
    def _():
        x = x_ref[...]
        ms = jnp.mean(x * x, axis=-1, keepdims=True)
        h_ref[...] = _bf(x * lax.rsqrt(ms + NORM_EPS) * g_ref[...])

    out = _dot(h_ref[...], w_ref[...])
    o_ref[...] = out
    if kv_from is not None:
        @pl.when(j >= kv_from)
        def _():
            tm = out.shape[0]
            for slab in range(out.shape[1] // HEAD_DIM):
                kv_ref[pl.ds(slab, tm, stride=out.shape[1] // HEAD_DIM), :] = out[:, slab * HEAD_DIM:(slab + 1) * HEAD_DIM]


def norm_matmul(x, g, w, kv_from=None, n_cols=None):
    m, k = x.shape
    n = n_cols or w.shape[1]
    tm = _row_tile(m, 1024)
    tn = _col_tile(n, 1024)
    out_specs = [pl.BlockSpec((tm, tn), lambda i, j: (i, j))]
    out_shape = [jax.ShapeDtypeStruct((m, n), jnp.float32)]
    if kv_from is not None:
        assert tn == KV_WIDTH
        slabs = tn // HEAD_DIM
        out_specs.append(pl.BlockSpec((None, tm * slabs, HEAD_DIM), lambda i, j: (jnp.maximum(j - kv_from, 0), i, 0)))
        out_shape.append(jax.ShapeDtypeStruct((n // tn - kv_from, m * slabs, HEAD_DIM), jnp.float32))
    res = pl.pallas_call(
        functools.partial(_norm_matmul_body, kv_from=kv_from),
        grid=(m // tm, n // tn),
        in_specs=[pl.BlockSpec((tm, k), lambda i, j: (i, 0)),
                  pl.BlockSpec((1, k), lambda i, j: (0, 0)),
                  pl.BlockSpec((k, tn), lambda i, j: (0, j))],
        out_specs=out_specs,
        out_shape=out_shape,
        scratch_shapes=[pltpu.VMEM((tm, k), jnp.bfloat16)],
        compiler_params=_cparams("parallel", "arbitrary"),
        name="norm_matmul",
    )(x, g.reshape(1, k), w)
    return res if kv_from is not None else res[0]


def _matmul_res_body(*refs):
    *a_refs, w_ref, r_ref, o_ref = refs
    a = a_refs[0][...]
    for a_ref in a_refs[1:]:
        a = a + a_ref[...]
    o_ref[...] = r_ref[...] + _dot(_bf(a), w_ref[...])


def matmul_res(a_list, w, res):
    m, k = a_list[0].shape
    n = w.shape[1]
    tm = _row_tile(m, 512)
    tn = _col_tile(n, 1024)
    return pl.pallas_call(
        _matmul_res_body,
        grid=(m // tm, n // tn),
        in_specs=[pl.BlockSpec((tm, k), lambda i, j: (i, 0)) for _ in a_list]
        + [pl.BlockSpec((k, tn), lambda i, j: (0, j)),
           pl.BlockSpec((tm, tn), lambda i, j: (i, j))],
        out_specs=pl.BlockSpec((tm, tn), lambda i, j: (i, j)),
        out_shape=jax.ShapeDtypeStruct((m, n), jnp.float32),
        compiler_params=_cparams("parallel", "arbitrary"),
        name="matmul_res",
    )(*a_list, w, res)


def _ffn_body(x_ref, g_ref, wa_ref, wb_ref, wd_ref, o_ref, h_ref):
    @pl.when(pl.program_id(1) == 0)
    def _():
        x = x_ref[...]
        ms = jnp.mean(x * x, axis=-1, keepdims=True)
        h_ref[...] = _bf(x * lax.rsqrt(ms + NORM_EPS) * g_ref[...])
        o_ref[...] = x

    h = h_ref[...]
    a = _dot(h, wa_ref[...])
    b = _dot(h, wb_ref[...])
    o_ref[...] += _dot(_bf(a * jax.nn.sigmoid(a) * b), wd_ref[...])


def ffn(x, g, w_up, w_down, layer):
    m, k = x.shape
    hdim = w_down.shape[1]
    tm = _row_tile(m, 1024)
    th = _col_tile(hdim, 512)
    nh = hdim // th
    return pl.pallas_call(
        _ffn_body,
        grid=(m // tm, nh),
        in_specs=[pl.BlockSpec((tm, k), lambda i, j: (i, 0)),
                  pl.BlockSpec((1, k), lambda i, j: (0, 0)),
                  pl.BlockSpec((None, k, th), lambda i, j: (layer, 0, j)),
                  pl.BlockSpec((None, k, th), lambda i, j: (layer, 0, j + nh)),
                  pl.BlockSpec((None, th, k), lambda i, j: (layer, j, 0))],
        out_specs=pl.BlockSpec((tm, k), lambda i, j: (i, 0)),
        out_shape=jax.ShapeDtypeStruct((m, k), jnp.float32),
        scratch_shapes=[pltpu.VMEM((tm, k), jnp.bfloat16)],
        compiler_params=_cparams("parallel", "arbitrary"),
        name="ffn",
    )(x, g.reshape(1, k), w_up, w_up, w_down)


def _norm_body(x_ref, g_ref, o_ref):
    x = x_ref[...]
    ms = jnp.mean(x * x, axis=-1, keepdims=True)
    o_ref[...] = x * lax.rsqrt(ms + NORM_EPS) * g_ref[...]


def final_norm(x, g):
    m, k = x.shape
    tm = _row_tile(m, 512)
    return pl.pallas_call(
        _norm_body,
        grid=(m // tm,),
        in_specs=[pl.BlockSpec((tm, k), lambda i: (i, 0)), pl.BlockSpec((1, k), lambda i: (0, 0))],
        out_specs=pl.BlockSpec((tm, k), lambda i: (i, 0)),
        out_shape=jax.ShapeDtypeStruct((m, k), jnp.float32),
        compiler_params=_cparams("parallel"),
        name="final_norm",
    )(x, g.reshape(1, k))


def _bucket_np(dist):
    exact = REL_BUCKETS // 2
    d = np.maximum(dist, 0)
    ratio = np.log(np.maximum(d, 1).astype(np.float32) / exact) / math.log(REL_MAX_DIST / exact)
    large = np.minimum(exact + (ratio * (REL_BUCKETS - exact)).astype(np.int32), REL_BUCKETS - 1)
    return np.where(d < exact, d, large)


def _bias_lookup(rel_table, dist):
    bucket = _bucket_np(dist).astype(np.int32)
    ids = [int(b) for b in np.unique(bucket)]
    bk = jnp.asarray(bucket)[..., None]
    out = jnp.broadcast_to(rel_table[ids[0]], bucket.shape + (rel_table.shape[1],))
    for b in ids[1:]:
        out = jnp.where(bk == b, rel_table[b], out)
    return out


def _heads_to_lanes(t):
    keys, queries, _ = t.shape
    return t.reshape(keys, queries, KV_HEADS, GROUP).transpose(2, 0, 3, 1).reshape(KV_HEADS, keys, GROUP * queries)


def _flash_bias_tiles(rel_table, window):
    j = np.arange(ATTN_TILE)[:, None]
    i = np.arange(ATTN_TILE)[None, :]
    n_cls = window // ATTN_TILE + 1 if window else -(-REL_MAX_DIST // ATTN_TILE) + 2
    tiles = []
    for d in range(n_cls):
        dist = d * ATTN_TILE + i - j
        ok = dist >= 0
        if window:
            ok = ok & (dist < window)
        tiles.append(_heads_to_lanes(jnp.where(ok[..., None], _bias_lookup(rel_table, dist) * LOG2E, NEG_INF)))
    return jnp.stack(tiles, axis=1)


def _cmp_bias_table(rel_table, ncp):
    x = np.arange(ncp)[:, None]
    i = np.arange(CMP_TILE)[None, :]
    dist = i - CMP_STRIDE * (x - 16) - (CMP_BLOCK - 1)
    far = rel_table[REL_BUCKETS - 1]
    b = _heads_to_lanes(jnp.where((dist >= 0)[..., None], _bias_lookup(rel_table, dist), far))
    return jnp.concatenate([b, b], axis=1)


def _stack_heads(q):
    return jnp.concatenate([q[:, r * HEAD_DIM:(r + 1) * HEAD_DIM] for r in range(GROUP)], axis=0)


def _gate_columns(gt_ref, col0):
    gt = jax.nn.sigmoid(gt_ref[...])
    lane = lax.broadcasted_iota(jnp.int32, gt.shape, 1)
    return [jnp.sum(jnp.where(lane == col0 + r, gt, 0.0), axis=1, keepdims=True) for r in range(GROUP)]


def _heads_from_lanes(o_t, rows, cols=None):
    parts = []
    for r in range(GROUP):
        part = o_t[:, r * rows:(r + 1) * rows].T
        if cols is not None:
            part = part * cols[r]
        parts.append(part)
    return jnp.concatenate(parts, axis=1)


def _rank_rows(score, n_rows):
    row = lax.broadcasted_iota(jnp.int32, score.shape, 0)
    rank = jnp.zeros(score.shape, jnp.int32)
    for mm in range(n_rows):
        sm = score[mm:mm + 1, :]
        ahead = (sm > score) | ((sm == score) & (row > mm))
        rank = rank + ahead.astype(jnp.int32)
    return rank


def _cmp_select_body(q_ref, kc_ref, dt_ref, gt_ref, o_ref, pen_ref, *, n_sel_blocks):
    g = pl.program_id(1)
    qi = pl.program_id(2)
    tq = CMP_TILE
    cols = GROUP * tq
    ncp = kc_ref.shape[1]
    q4 = _bf(_stack_heads(q_ref[...]) * ATTN_SCALE)
    s = _dot(_bf(kc_ref[0]), q4, NT_DIMS)
    shift = (qi * (tq // CMP_STRIDE) + ncp - 16) % ncp
    bias = dt_ref[pl.ds(pl.multiple_of(ncp - shift, SUBLANE), ncp), :]
    t_col = qi * tq + (lax.broadcasted_iota(jnp.int32, (ncp, cols), 1) & (tq - 1))
    end_pos = lax.broadcasted_iota(jnp.int32, (ncp, cols), 0) * CMP_STRIDE + (CMP_BLOCK - 1)
    mask = t_col >= end_pos
    s = jnp.where(mask, s + bias, NEG_INF)
    m = jnp.max(s, axis=0, keepdims=True)
    e = jnp.where(mask, jnp.exp(s - m), 0.0)
    p = e / jnp.maximum(jnp.sum(e, axis=0, keepdims=True), 1e-30)
    o_t = _dot(_bf(kc_ref[1].T), _bf(p))
    o_ref[...] = _heads_from_lanes(o_t, tq, _gate_columns(gt_ref, g * GROUP))

    imp = p[:, 0:tq]
    for r in range(1, GROUP):
        imp = imp + p[:, r * tq:(r + 1) * tq]
    ratio = SEL_BLOCK // CMP_STRIDE
    j_i = lax.broadcasted_iota(jnp.int32, (n_sel_blocks, ncp), 0)
    c_i = lax.broadcasted_iota(jnp.int32, (n_sel_blocks, ncp), 1)
    w = _bf((c_i >= ratio * j_i - 1) & (c_i <= ratio * j_i + ratio - 1))
    score = sum(_dot(w, part) for part in _split_bf16(imp, 3))
    blk = lax.broadcasted_iota(jnp.int32, (n_sel_blocks, tq), 0)
    tok = qi * tq + lax.broadcasted_iota(jnp.int32, (n_sel_blocks, tq), 1)
    cur = lax.shift_right_logical(tok, int(math.log2(SEL_BLOCK)))
    forced = (blk == 0) | (blk == cur) | (blk == cur - 1)
    causal = blk <= cur
    score = jnp.where(forced, FORCE_SCORE, score)
    score = jnp.where(causal, score, -1.0)
    chosen = (_rank_rows(score, n_sel_blocks) < N_SEL) & causal
    pen_ref[...] = jnp.where(chosen, 0.0, NEG_INF)


def cmp_select(proj, tail, kc, dt, batch, seq):
    tq = CMP_TILE
    nq = seq // tq
    ncp = kc.shape[3]
    nsb = seq // SEL_BLOCK
    return pl.pallas_call(
        functools.partial(_cmp_select_body, n_sel_blocks=nsb),
        grid=(batch, KV_HEADS, nq),
        in_specs=[pl.BlockSpec((tq, GROUP * HEAD_DIM), lambda b, g, i: (b * nq + i, g)),
                  pl.BlockSpec((None, None, 2, ncp, HEAD_DIM), lambda b, g, i: (b, g, 0, 0, 0)),
                  pl.BlockSpec((None, 2 * ncp, GROUP * tq), lambda b, g, i: (g, 0, 0)),
                  pl.BlockSpec((tq, LANE), lambda b, g, i: (b * nq + i, 0))],
        out_specs=[pl.BlockSpec((tq, GROUP * HEAD_DIM), lambda b, g, i: (b * nq + i, g)),
                   pl.BlockSpec((None, None, nsb, tq), lambda b, g, i: (b, g, 0, i))],
        out_shape=[jax.ShapeDtypeStruct((batch * seq, Q_WIDTH), jnp.float32),
                   jax.ShapeDtypeStruct((batch, KV_HEADS, nsb, seq), jnp.float32)],
        compiler_params=_cparams("parallel", "parallel", "arbitrary"),
        name="cmp_select",
    )(proj, kc, dt, tail)


def _flash_body(*refs, pen_block, pen_per_head, k_back, gate_col0, seq):
    it = iter(refs)
    q_ref, k_ref, v_ref, bt_ref = next(it), next(it), next(it), next(it)
    pen_ref = next(it) if pen_block else None
    gt_ref = next(it) if gate_col0 is not None else None
    o_ref, m_ref, acc_ref, qa_ref, kb_ref, vt_ref, sa_ref, sb_ref = (next(it) for _ in range(8))
    g = pl.program_id(1)
    qi = pl.program_id(2)
    tq = tk = ATTN_TILE
    n_cls = bt_ref.shape[0]

    @pl.when(qi == 0)
    def _():
        vt_ref[HEAD_DIM:, :] = jnp.ones((SUBLANE, seq), jnp.bfloat16)
        for c in range(seq // tk):
            kb_ref[c * tk:(c + 1) * tk, :] = _bf(k_ref[c * tk:(c + 1) * tk, :])
            vt_ref[0:HEAD_DIM, c * tk:(c + 1) * tk] = _bf(v_ref[c * tk:(c + 1) * tk, :].T)

    qa_ref[...] = _bf(_stack_heads(q_ref[...]) * (ATTN_SCALE * LOG2E))
    m_ref[...] = jnp.full(m_ref.shape, NEG_INF, jnp.float32)
    acc_ref[...] = jnp.zeros(acc_ref.shape, jnp.float32)

    def raw_logits(kj):
        k0 = pl.multiple_of(jnp.minimum(kj, qi) * tk, tk)
        return _dot(kb_ref[pl.ds(k0, tk), :], qa_ref[...], NT_DIMS)

    per_tile = tk // pen_block if pen_block else 1
    blk = tk // per_tile

    def attend(s_ref, kj, far):
        k0 = pl.multiple_of(kj * tk, tk)
        row = bt_ref[n_cls - 1, 0:1, :] if far else None
        bias = None if far else bt_ref[jnp.minimum(qi - kj, n_cls - 1)]
        pieces = []
        for a in range(per_tile):
            rows = slice(a * blk, (a + 1) * blk)
            piece = s_ref[rows, :]
            add = row
            if pen_block:
                pen = pen_ref[pl.ds(kj * per_tile + a, 1), :]
                pen = pen if pen_per_head else jnp.concatenate([pen] * GROUP, axis=1)
                add = pen if row is None else pen + row
            if bias is not None:
                piece = piece + bias[rows]
            pieces.append(piece if add is None else piece + add)
        s = pieces[0] if per_tile == 1 else jnp.concatenate(pieces, axis=0)
        m_prev = m_ref[...]
        m_new = jnp.maximum(m_prev, jnp.max(s, axis=0, keepdims=True))
        p = jnp.exp2(s - m_new)
        acc_ref[...] = jnp.exp2(m_prev - m_new) * acc_ref[...] + _dot(vt_ref[:, pl.ds(k0, tk)], _bf(p))
        m_ref[...] = m_new

    k_lo = jnp.maximum(qi - k_back, 0) if k_back is not None else 0
    n_tiles = qi - k_lo + 1
    far_pairs = jnp.maximum(n_tiles - (n_cls - 1), 0) // 2 if k_back is None else 0
    sa_ref[...] = raw_logits(k_lo)

    def pair(pi, carry, far):
        ka = k_lo + 2 * pi
        sb_ref[...] = raw_logits(ka + 1)
        attend(sa_ref, ka, far)
        sa_ref[...] = raw_logits(ka + 2)
        attend(sb_ref, ka + 1, far)
        return carry

    lax.fori_loop(0, far_pairs, functools.partial(pair, far=True), 0)
    lax.fori_loop(far_pairs, n_tiles // 2, functools.partial(pair, far=False), 0)

    @pl.when(n_tiles % 2 == 1)
    def _():
        attend(sa_ref, qi, False)
    acc = acc_ref[...]
    o_t = acc[0:HEAD_DIM] / jnp.maximum(acc[HEAD_DIM:HEAD_DIM + 1], 1e-30)
    cols_g = _gate_columns(gt_ref, gate_col0 + g * GROUP) if gate_col0 is not None else None
    o_ref[...] = _heads_from_lanes(o_t, tq, cols_g)


def flash_attention(q_arr, kv_arr, k_col, v_col, bias, batch, seq, *, pen=None, pen_block=0,
                    k_back=None, gate_arr=None, gate_col0=None):
    tq = ATTN_TILE
    nq = seq // tq
    cols = GROUP * tq
    assert k_back is None or k_back == bias.shape[1] - 1
    in_specs = [pl.BlockSpec((tq, GROUP * HEAD_DIM), lambda b, g, i: (b * nq + i, g)),
                pl.BlockSpec((seq, HEAD_DIM), lambda b, g, i: (b, k_col + g)),
                pl.BlockSpec((seq, HEAD_DIM), lambda b, g, i: (b, v_col + g)),
                pl.BlockSpec((None,) + bias.shape[1:], lambda b, g, i: (g, 0, 0, 0))]
    args = [q_arr, kv_arr, kv_arr, bias]
    pen_per_head = False
    if pen is not None:
        if pen.ndim == 4:
            in_specs.append(pl.BlockSpec((None, None, pen.shape[2], tq), lambda b, g, i: (b, g, 0, i)))
        else:
            pen_per_head = True
            in_specs.append(pl.BlockSpec((None, None, None, pen.shape[3], cols), lambda b, g, i: (b, g, i, 0, 0)))
        args.append(pen)
    if gate_arr is not None:
        in_specs.append(pl.BlockSpec((tq, LANE), lambda b, g, i: (b * nq + i, 0)))
        args.append(gate_arr)
    return pl.pallas_call(
        functools.partial(_flash_body, pen_block=pen_block if pen is not None else 0, pen_per_head=pen_per_head,
                          k_back=k_back, gate_col0=gate_col0 if gate_arr is not None else None, seq=seq),
        grid=(batch, KV_HEADS, nq),
        in_specs=in_specs,
        out_specs=pl.BlockSpec((tq, GROUP * HEAD_DIM), lambda b, g, i: (b * nq + i, g)),
        out_shape=jax.ShapeDtypeStruct((batch * seq, Q_WIDTH), jnp.float32),
        scratch_shapes=[pltpu.VMEM((1, cols), jnp.float32),
                        pltpu.VMEM((HEAD_DIM + SUBLANE, cols), jnp.float32),
                        pltpu.VMEM((cols, HEAD_DIM), jnp.bfloat16),
                        pltpu.VMEM((seq, HEAD_DIM), jnp.bfloat16),
                        pltpu.VMEM((HEAD_DIM + SUBLANE, seq), jnp.bfloat16),
                        pltpu.VMEM((tq, cols), jnp.float32), pltpu.VMEM((tq, cols), jnp.float32)],
        compiler_params=_cparams("parallel", "parallel", "arbitrary"),
        name="flash_attention",
    )(*args)


def _moba_gate_body(q_ref, k_ref, pen_ref, km_ref, *, n_blocks):
    qi = pl.program_id(2)
    tq = ATTN_TILE
    cols = GROUP * tq

    @pl.when(qi == 0)
    def _():
        k = k_ref[...]
        km_ref[...] = jnp.sum(k.reshape(n_blocks, MOBA_BLOCK, HEAD_DIM), axis=1) / MOBA_BLOCK

    qh, ql = _split_bf16(_stack_heads(q_ref[...]), 2)
    kh, kl = _split_bf16(km_ref[...], 2)
    gate = _dot(kh, qh, NT_DIMS) + _dot(kh, ql, NT_DIMS) + _dot(kl, qh, NT_DIMS)
    blk = lax.broadcasted_iota(jnp.int32, (n_blocks, cols), 0)
    tok = qi * tq + (lax.broadcasted_iota(jnp.int32, (n_blocks, cols), 1) & (tq - 1))
    own = lax.shift_right_logical(tok, int(math.log2(MOBA_BLOCK)))
    gate = jnp.where(blk < own, gate, NEG_INF)
    chosen = ((_rank_rows(gate, n_blocks) < MOBA_TOPK) & (blk < own)) | (blk == own)
    pen_ref[...] = jnp.where(chosen, 0.0, NEG_INF)


def moba_gate(proj, k_col, batch, seq):
    tq = ATTN_TILE
    nq = seq // tq
    cols = GROUP * tq
    nb = seq // MOBA_BLOCK
    return pl.pallas_call(
        functools.partial(_moba_gate_body, n_blocks=nb),
        grid=(batch, KV_HEADS, nq),
        in_specs=[pl.BlockSpec((tq, GROUP * HEAD_DIM), lambda b, g, i: (b * nq + i, g)),
                  pl.BlockSpec((seq, HEAD_DIM), lambda b, g, i: (b, k_col + g))],
        out_specs=pl.BlockSpec((None, None, None, nb, cols), lambda b, g, i: (b, g, i, 0, 0)),
        out_shape=jax.ShapeDtypeStruct((batch, KV_HEADS, nq, nb, cols), jnp.float32),
        scratch_shapes=[pltpu.VMEM((nb, HEAD_DIM), jnp.float32)],
        compiler_params=_cparams("parallel", "parallel", "arbitrary"),
        name="moba_gate",
    )(proj, proj)


def _tril_ones(n, strict=False):
    r = lax.broadcasted_iota(jnp.int32, (n, n), 0)
    c = lax.broadcasted_iota(jnp.int32, (n, n), 1)
    return (r > c) if strict else (r >= c)


def _chunk_cumsum(x):
    tril = _bf(_tril_ones(x.shape[0]))
    return sum(_dot(tril, part) for part in _split_bf16(x, 3))


def _head_rms_gate(o, norm_g, gate):
    ms = jnp.mean(o * o, axis=-1, keepdims=True)
    return o * lax.rsqrt(ms + NORM_EPS) * norm_g * (gate * jax.nn.sigmoid(gate))


def _hgrn2_body(q_ref, f_ref, i_ref, g_ref, lbl_ref, ng_ref, s0_ref, o_ref, s_out_ref, st_ref, *, layer):
    ti = pl.program_id(2)

    @pl.when(ti == 0)
    def _():
        st_ref[...] = s0_ref[...].T

    lbl = lbl_ref[...]
    e = jnp.exp(lbl - jnp.max(lbl, axis=0, keepdims=True))
    p = e / jnp.sum(e, axis=0, keepdims=True)
    lb = jnp.zeros((1, HG_DK), jnp.float32)
    for r in range(1, layer + 1):
        lb = lb + p[r:r + 1]
    causal = _tril_ones(CHUNK)
    work = []
    for c in range(q_ref.shape[0] // CHUNK):
        sl = slice(c * CHUNK, (c + 1) * CHUNK)
        q = q_ref[sl, :]
        qh = q * jax.nn.sigmoid(q) * HG_DK ** -0.5
        fg = lb + (1.0 - lb) * jax.nn.sigmoid(f_ref[sl, :])
        k = 1.0 - fg
        v = _bf(i_ref[sl, :])
        b = _chunk_cumsum(jnp.log(fg))
        b_mid = b[CHUNK // 2:CHUNK // 2 + 1]
        b_last = b[CHUNK - 1:CHUNK]
        a = _dot(_bf(qh * jnp.exp(b - b_mid)), _bf(k * jnp.exp(b_mid - b)), NT_DIMS)
        a = jnp.where(causal, a, 0.0)
        work.append((sl, _dot(_bf(a), v), _bf(qh * jnp.exp(b)), jnp.exp(b_last),
                     _dot(v, _bf(k * jnp.exp(b_last - b)), TN_DIMS)))
    for sl, o_intra, q_in, d_last, kv in work:
        st = st_ref[...]
        o = o_intra + _dot(q_in, _bf(st), NT_DIMS)
        st_ref[...] = st * d_last + kv
        o_ref[sl, :] = _head_rms_gate(o, ng_ref[...], g_ref[sl, :])

    @pl.when(ti == pl.num_programs(2) - 1)
    def _():
        s_out_ref[...] = st_ref[...].T


def hgrn2_scan(proj, lb_logits, norm_g, s0, layer, batch, seq):
    tt = _row_tile(seq, SCAN_TILE)
    nt = seq // tt
    h = HG_HEADS

    def col(k):
        return pl.BlockSpec((tt, HG_DK), lambda b, hh, t: (b * nt + t, k * h + hh))

    return pl.pallas_call(
        functools.partial(_hgrn2_body, layer=layer),
        grid=(batch, h, nt),
        in_specs=[col(0), col(1), col(2), col(3),
                  pl.BlockSpec((DEPTH, HG_DK), lambda b, hh, t: (0, hh)),
                  pl.BlockSpec((1, HG_DV), lambda b, hh, t: (0, 0)),
                  pl.BlockSpec((None, None, HG_DK, HG_DV), lambda b, hh, t: (b, hh, 0, 0))],
        out_specs=[pl.BlockSpec((tt, HG_DV), lambda b, hh, t: (b * nt + t, hh)),
                   pl.BlockSpec((None, None, HG_DK, HG_DV), lambda b, hh, t: (b, hh, 0, 0))],
        out_shape=[jax.ShapeDtypeStruct((batch * seq, h * HG_DV), jnp.float32),
                   jax.ShapeDtypeStruct((batch, h, HG_DK, HG_DV), jnp.float32)],
        scratch_shapes=[pltpu.VMEM((HG_DV, HG_DK), jnp.float32)],
        compiler_params=_cparams("parallel", "parallel", "arbitrary"),
        name="hgrn2_scan",
    )(proj, proj, proj, proj, lb_logits, norm_g.reshape(1, HG_DV), s0)


def _lane_column(x, lane_idx):
    lane = lax.broadcasted_iota(jnp.int32, x.shape, 1)
    return jnp.sum(jnp.where(lane == lane_idx, x, 0.0), axis=1, keepdims=True)


def _softplus(x):
    return jnp.maximum(x, 0.0) + jnp.log(1.0 + jnp.exp(-jnp.abs(x)))


def _l2n(x):
    return x * lax.rsqrt(jnp.sum(x * x, axis=-1, keepdims=True) + NORM_EPS)


def _gdn_body(q_ref, k_ref, v_ref, z_ref, t_ref, bq_ref, bk_ref, bv_ref, wq_ref, wk_ref, wv_ref,
              al_ref, dtb_ref, ng_ref, s0_ref, o_ref, s_out_ref, xs_ref, y_ref, s_ref):
    hq0 = pl.program_id(1) * GDN_HPS
    n_v = GDN_HPS * GDN_REP
    ti = pl.program_id(2)
    tt = q_ref.shape[0]
    dk, dv = GDN_DK, GDN_DV
    pad = SUBLANE

    @pl.when(ti == 0)
    def _():
        s_ref[...] = s0_ref[...]
        xs_ref[0:pad, :] = jnp.concatenate([bq_ref[...], bk_ref[...], bv_ref[...]], axis=1)

    x = jnp.concatenate([q_ref[...], k_ref[...], v_ref[...]], axis=1)
    xs_ref[pad:, :] = x
    cw = jnp.concatenate([wq_ref[...], wk_ref[...], wv_ref[...]], axis=1)
    y = xs_ref[pad - 3:pad - 3 + tt, :] * cw[0:1]
    for i in range(1, GDN_CONV - 1):
        y = y + xs_ref[pad - 3 + i:pad - 3 + i + tt, :] * cw[i:i + 1]
    y = y + x * cw[GDN_CONV - 1:GDN_CONV]
    xs_ref[0:pad, :] = x[tt - pad:tt]
    y_ref[...] = y * jax.nn.sigmoid(y)

    strict = _tril_ones(CHUNK, strict=True)
    incl = _tril_ones(CHUNK)
    sel_rows = lax.shift_right_logical(lax.broadcasted_iota(jnp.int32, (n_v * CHUNK, LANE), 0), int(math.log2(CHUNK)))
    sel_lane = lax.broadcasted_iota(jnp.int32, (n_v * CHUNK, LANE), 1)
    pick = _bf(sel_lane == GDN_V_HEADS + hq0 * GDN_REP + sel_rows)
    n_chunks = tt // CHUNK

    work = []
    for c in range(n_chunks):
        sl = slice(c * CHUNK, (c + 1) * CHUNK)
        yc = y_ref[sl, :]
        tl = t_ref[sl, :]
        beta_all = jax.nn.sigmoid(tl)
        g_all = _chunk_cumsum(-jnp.exp(al_ref[...]) * _softplus(tl + dtb_ref[...]))
        g_rows = sum(_dot(pick, part, NT_DIMS) for part in _split_bf16(g_all, 3))
        for hh in range(GDN_HPS):
            q = _l2n(yc[:, hh * dk:(hh + 1) * dk]) * dk ** -0.5
            k = _l2n(yc[:, (GDN_HPS + hh) * dk:(GDN_HPS + hh + 1) * dk])
            qb, kb = _bf(q), _bf(k)
            kk = _dot(kb, kb, NT_DIMS)
            qk = _dot(qb, kb, NT_DIMS)
            for e in range(hh * GDN_REP, (hh + 1) * GDN_REP):
                hv = hq0 * GDN_REP + e
                v = yc[:, 2 * GDN_HPS * dk + e * dv:2 * GDN_HPS * dk + (e + 1) * dv]
                bt = _lane_column(beta_all, hv)
                gc = _lane_column(g_all, GDN_V_HEADS + hv)
                gdiff = gc - g_rows[e * CHUNK:(e + 1) * CHUNK]
                decay = jnp.exp(jnp.where(incl, gdiff, 0.0))
                d_strict = jnp.where(strict, decay, 0.0)
                d_incl = jnp.where(incl, decay, 0.0)
                eg = jnp.exp(gc)
                g_last = gc[CHUNK - 1:CHUNK]
                work.append(dict(
                    c=c, e=e, sol=jnp.concatenate([bt * v, (bt * eg) * k], axis=1), pw=bt * kk * d_strict,
                    aq=_bf(qk * d_incl), q_in=_bf(q * eg), k_out=_bf(k * jnp.exp(g_last - gc)),
                    d_last=jnp.exp(g_last)))

    r_i = lax.broadcasted_iota(jnp.int32, (CHUNK, CHUNK), 0)
    c_i = lax.broadcasted_iota(jnp.int32, (CHUNK, CHUNK), 1)
    same = [lax.shift_right_logical(r_i, sh) == lax.shift_right_logical(c_i, sh) for sh in range(3, 7)]
    eye = (r_i == c_i).astype(jnp.float32)
    for wk in work:
        l8 = jnp.where(same[0], wk["pw"], 0.0)
        l8b = _bf(l8)
        wk["t"] = eye - l8
        wk["p"] = _dot(l8b, l8b)
    for wk in work:
        pb = _bf(wk["p"])
        wk["t"] = wk["t"] + _dot(_bf(wk["t"]), pb)
        wk["p"] = _dot(pb, pb)
    for wk in work:
        wk["t"] = wk["t"] + _dot(_bf(wk["t"]), _bf(wk["p"]))
    for lvl in range(1, len(same)):
        for wk in work:
            tb = _bf(wk["t"])
            off = _bf(jnp.where(same[lvl] & jnp.logical_not(same[lvl - 1]), wk["pw"], 0.0))
            wk["t"] = wk["t"] - _dot(tb, _bf(_dot(off, tb)))
    for wk in work:
        wk["sol"] = _dot(_bf(wk["t"]), _bf(wk["sol"]))

    for wk in work:
        c, e = wk["c"], wk["e"]
        sl = slice(c * CHUNK, (c + 1) * CHUNK)
        u0, w = wk["sol"][:, :dv], wk["sol"][:, dv:]
        s = s_ref[e]
        sb = _bf(s)
        u = u0 - _dot(_bf(w), sb)
        o = _dot(wk["q_in"], sb) + _dot(wk["aq"], _bf(u))
        s_ref[e] = wk["d_last"] * s + _dot(wk["k_out"], _bf(u), TN_DIMS)
        o_ref[sl, e * dv:(e + 1) * dv] = _head_rms_gate(o, ng_ref[...], z_ref[sl, e * dv:(e + 1) * dv])

    @pl.when(ti == pl.num_programs(2) - 1)
    def _():
        s_out_ref[...] = s_ref[...]


def gdn_scan(main, tail, conv_buf, conv_w, a_log, dt_bias, norm_g, s0, batch, seq):
    tt = _row_tile(seq, SCAN_TILE)
    nt = seq // tt
    hq, rep = GDN_QK_HEADS // GDN_HPS, GDN_HPS * GDN_REP
    dk, dv = GDN_HPS * GDN_DK, GDN_DV
    vw = rep * dv
    buf = jnp.pad(conv_buf, ((0, 0), (SUBLANE - (GDN_CONV - 1), 0), (0, 0)))
    pad_lanes = jnp.zeros((LANE - 2 * GDN_V_HEADS,), jnp.float32)
    a_row = jnp.concatenate([jnp.zeros((GDN_V_HEADS,), jnp.float32), a_log, pad_lanes]).reshape(1, LANE)
    dt_row = jnp.concatenate([jnp.zeros((GDN_V_HEADS,), jnp.float32), dt_bias, pad_lanes]).reshape(1, LANE)
    k0 = hq
    v0 = 2 * hq * dk // vw
    z0 = GDN_CONV_DIM // vw
    row = lambda b, h, t: b * nt + t
    return pl.pallas_call(
        _gdn_body,
        grid=(batch, hq, nt),
        in_specs=[pl.BlockSpec((tt, dk), lambda b, h, t: (row(b, h, t), h)),
                  pl.BlockSpec((tt, dk), lambda b, h, t: (row(b, h, t), k0 + h)),
                  pl.BlockSpec((tt, vw), lambda b, h, t: (row(b, h, t), v0 + h)),
                  pl.BlockSpec((tt, vw), lambda b, h, t: (row(b, h, t), z0 + h)),
                  pl.BlockSpec((tt, LANE), lambda b, h, t: (row(b, h, t), 0)),
                  pl.BlockSpec((None, SUBLANE, dk), lambda b, h, t: (b, 0, h)),
                  pl.BlockSpec((None, SUBLANE, dk), lambda b, h, t: (b, 0, k0 + h)),
                  pl.BlockSpec((None, SUBLANE, vw), lambda b, h, t: (b, 0, v0 + h)),
                  pl.BlockSpec((GDN_CONV, dk), lambda b, h, t: (0, h)),
                  pl.BlockSpec((GDN_CONV, dk), lambda b, h, t: (0, k0 + h)),
                  pl.BlockSpec((GDN_CONV, vw), lambda b, h, t: (0, v0 + h)),
                  pl.BlockSpec((1, LANE), lambda b, h, t: (0, 0)),
                  pl.BlockSpec((1, LANE), lambda b, h, t: (0, 0)),
                  pl.BlockSpec((1, dv), lambda b, h, t: (0, 0)),
                  pl.BlockSpec((None, rep, GDN_DK, dv), lambda b, h, t: (b, h, 0, 0))],
        out_specs=[pl.BlockSpec((tt, vw), lambda b, h, t: (row(b, h, t), h)),
                   pl.BlockSpec((None, rep, GDN_DK, dv), lambda b, h, t: (b, h, 0, 0))],
        out_shape=[jax.ShapeDtypeStruct((batch * seq, GDN_V_HEADS * dv), jnp.float32),
                   jax.ShapeDtypeStruct((batch, GDN_V_HEADS, GDN_DK, dv), jnp.float32)],
        scratch_shapes=[pltpu.VMEM((tt + SUBLANE, 2 * dk + vw), jnp.float32),
                        pltpu.VMEM((tt, 2 * dk + vw), jnp.float32),
                        pltpu.VMEM((rep, GDN_DK, dv), jnp.float32)],
        compiler_params=_cparams("parallel", "parallel", "arbitrary"),
        name="gdn_scan",
    )(main, main, main, main, tail, buf, buf, buf, conv_w, conv_w, conv_w, a_row, dt_row,
      norm_g.reshape(1, dv), s0)


PAGE = 128
KV_SLABS = 2 * KV_HEADS
PAGE_ROWS = PAGE * KV_SLABS
CMP_PAGES = 8
CHUNKS_PER_PAGE = PAGE // CMP_STRIDE
ROWS = GROUP * 4


def _page_slab(pg, slab):
    return pg[pl.ds(slab, PAGE, stride=KV_SLABS), :]


def _compress_part_body(*refs, n_prefetch, paged):
    refs = refs[n_prefetch:]
    pages, w_ref, o_ref, xs_ref = refs[:CMP_PAGES], refs[CMP_PAGES], refs[CMP_PAGES + 1], refs[CMP_PAGES + 2]
    per_g = CMP_PAGES * CHUNKS_PER_PAGE
    for k, pg in enumerate(pages):
        for cg in range(KV_SLABS):
            xs_ref[k, cg] = _page_slab(pg, cg) if paged else pg[:, cg * HEAD_DIM:(cg + 1) * HEAD_DIM]
    def token_rows(c, l):
        return jnp.concatenate([xs_ref.at[k, c * KV_HEADS + g][pl.ds(l, CHUNKS_PER_PAGE, stride=CMP_STRIDE), :]
                                for g in range(KV_HEADS) for k in range(CMP_PAGES)], axis=0)

    for c in range(2):
        acc = None
        for lp in range(CMP_STRIDE // 2):
            x = jnp.concatenate([token_rows(c, 2 * lp), token_rows(c, 2 * lp + 1)], axis=1)
            d = _dot(_bf(x), w_ref[c, lp])
            acc = d if acc is None else acc + d
        for g in range(KV_HEADS):
            o_ref[c, g] = acc[g * per_g:(g + 1) * per_g]


def _compress_weights(cmp_w1):
    n_part = CMP_BLOCK // CMP_STRIDE
    w = cmp_w1.reshape(2, n_part, CMP_STRIDE, HEAD_DIM, CMP_HIDDEN).transpose(0, 2, 3, 1, 4)
    return w.reshape(2, CMP_STRIDE // 2, 2 * HEAD_DIM, n_part * CMP_HIDDEN).astype(jnp.bfloat16)


def compress_part_rows(kv_arr, col_block, cmp_w1, batch, seq):
    w = _compress_weights(cmp_w1)
    pages_per_b = seq // PAGE
    steps = pages_per_b // CMP_PAGES
    per_g = CMP_PAGES * CHUNKS_PER_PAGE

    def page_spec(k):
        return pl.BlockSpec((PAGE, KV_WIDTH), lambda b, s: (b * pages_per_b + s * CMP_PAGES + k, col_block))

    return pl.pallas_call(
        functools.partial(_compress_part_body, n_prefetch=0, paged=False),
        grid=(batch, steps),
        in_specs=[page_spec(k) for k in range(CMP_PAGES)] + [pl.BlockSpec(w.shape, lambda b, s: (0, 0, 0, 0))],
        out_specs=pl.BlockSpec((None, 2, KV_HEADS, per_g, w.shape[-1]), lambda b, s: (b, 0, 0, s, 0)),
        out_shape=jax.ShapeDtypeStruct((batch, 2, KV_HEADS, seq // CMP_STRIDE, w.shape[-1]), jnp.float32),
        scratch_shapes=[pltpu.VMEM((CMP_PAGES, 2 * KV_HEADS, PAGE, HEAD_DIM), jnp.float32)],
        compiler_params=_cparams("parallel", "arbitrary"),
        name="compress_part_rows",
    )(*([kv_arr] * CMP_PAGES), w)


def compress_part_paged(pool, page_table, cmp_w1):
    w = _compress_weights(cmp_w1)
    batch, n_pages = page_table.shape
    steps = n_pages // CMP_PAGES
    per_g = CMP_PAGES * CHUNKS_PER_PAGE

    def page_spec(k):
        return pl.BlockSpec((None, PAGE_ROWS, HEAD_DIM), lambda b, s, pt: (pt[b, s * CMP_PAGES + k], 0, 0))

    return pl.pallas_call(
        functools.partial(_compress_part_body, n_prefetch=1, paged=True),
        grid_spec=pltpu.PrefetchScalarGridSpec(
            num_scalar_prefetch=1, grid=(batch, steps),
            in_specs=[page_spec(k) for k in range(CMP_PAGES)]
            + [pl.BlockSpec(w.shape, lambda b, s, pt: (0, 0, 0, 0))],
            out_specs=pl.BlockSpec((None, 2, KV_HEADS, per_g, w.shape[-1]), lambda b, s, pt: (b, 0, 0, s, 0)),
            scratch_shapes=[pltpu.VMEM((CMP_PAGES, 2 * KV_HEADS, PAGE, HEAD_DIM), jnp.float32)]),
        out_shape=jax.ShapeDtypeStruct((batch, 2, KV_HEADS, n_pages * CHUNKS_PER_PAGE, w.shape[-1]), jnp.float32),
        compiler_params=_cparams("parallel", "arbitrary"),
        name="compress_part_paged",
    )(page_table, *([pool] * CMP_PAGES), w)


def _gelu_tanh(x):
    return x * (0.5 * (1.0 + jnp.tanh(math.sqrt(2.0 / math.pi) * (x + 0.044715 * (x * x * x)))))


def _compress_finish_body(p_ref, peh_ref, w2_ref, o_ref):
    n = p_ref.shape[1]
    for c in range(2):
        p = p_ref[c]
        hid = peh_ref[c:c + 1, :] + p[:, :CMP_HIDDEN]
        hid = hid + pltpu.roll(p[:, CMP_HIDDEN:], n - 1, 0)
        o_ref[c] = _dot(_bf(_gelu_tanh(hid)), _bf(w2_ref[c]))


def compress_finish(part, cmp_w1, cmp_w2, cmp_pe):
    batch, _, _, n, width = part.shape
    pe_hid = jnp.einsum('cld,cldh->ch', cmp_pe, cmp_w1)
    return pl.pallas_call(
        _compress_finish_body,
        grid=(batch, KV_HEADS),
        in_specs=[pl.BlockSpec((None, 2, None, n, width), lambda b, g: (b, 0, g, 0, 0)),
                  pl.BlockSpec((2, CMP_HIDDEN), lambda b, g: (0, 0)),
                  pl.BlockSpec((2, CMP_HIDDEN, HEAD_DIM), lambda b, g: (0, 0, 0))],
        out_specs=pl.BlockSpec((None, None, 2, n, HEAD_DIM), lambda b, g: (b, g, 0, 0, 0)),
        out_shape=jax.ShapeDtypeStruct((batch, KV_HEADS, 2, n, HEAD_DIM), jnp.float32),
        compiler_params=_cparams("parallel", "parallel"),
        name="compress_finish",
    )(part, pe_hid, cmp_w2)


def _rows_to_col(row, n):
    eye = lax.broadcasted_iota(jnp.int32, (n, n), 0) == lax.broadcasted_iota(jnp.int32, (n, n), 1)
    return jnp.sum(jnp.where(eye, jnp.broadcast_to(row, (n, n)), 0.0), axis=1, keepdims=True)


def _sample_bias_tiles(rel_table, past, n_new):
    j = np.arange(PAGE)[:, None]
    t = np.arange(n_new)[None, :]
    far = np.full((PAGE, n_new), REL_MAX_DIST)
    first = WINDOW + t - j
    last = PAGE + t - j
    new = t - j
    tiles = []
    for dist, ok in ((far, far > 0), (first, first < WINDOW), (last, last > 0), (new, (new >= 0) & (j < n_new))):
        b = jnp.where(ok[..., None], _bias_lookup(rel_table, dist), NEG_INF)
        b = b.reshape(PAGE, n_new, KV_HEADS, GROUP).transpose(2, 0, 3, 1).reshape(KV_HEADS, PAGE, GROUP * n_new)
        tiles.append(b)
    return jnp.stack(tiles, axis=1)


def _sample_cmp_body(q_ref, kc_ref, bias_ref, gt_ref, o_ref, pen_ref, *, n_sel_blocks, past):
    n = kc_ref.shape[1]
    nbp = pen_ref.shape[0]
    q = _bf(q_ref[...] * ATTN_SCALE)
    s = _dot(_bf(kc_ref[0]), q, NT_DIMS) + bias_ref[...]
    m = jnp.max(s, axis=0, keepdims=True)
    e = jnp.exp(s - m)
    p = e / jnp.maximum(jnp.sum(e, axis=0, keepdims=True), 1e-30)
    o = _dot(_bf(p), _bf(kc_ref[1]), TN_DIMS)
    o_ref[...] = o * jax.nn.sigmoid(gt_ref[...])

    r_i = lax.broadcasted_iota(jnp.int32, (ROWS, ROWS), 0)
    c_i = lax.broadcasted_iota(jnp.int32, (ROWS, ROWS), 1)
    n_tok = ROWS // GROUP
    same_tok = _bf((r_i & (n_tok - 1)) == (c_i & (n_tok - 1)))
    ratio = SEL_BLOCK // CMP_STRIDE
    j_i = lax.broadcasted_iota(jnp.int32, (nbp, n), 0)
    k_i = lax.broadcasted_iota(jnp.int32, (nbp, n), 1)
    w = _bf((k_i >= ratio * j_i - 1) & (k_i <= ratio * j_i + ratio - 1) & (j_i < n_sel_blocks))
    imp = sum(_dot(part, same_tok) for part in _split_bf16(p, 3))
    score = sum(_dot(w, part) for part in _split_bf16(imp, 3))
    blk = lax.broadcasted_iota(jnp.int32, (nbp, ROWS), 0)
    tok = past + (lax.broadcasted_iota(jnp.int32, (nbp, ROWS), 1) & (n_tok - 1))
    cur = lax.shift_right_logical(tok, int(math.log2(SEL_BLOCK)))
    forced = (blk == 0) | (blk == cur) | (blk == cur - 1)
    causal = blk <= cur
    score = jnp.where(forced, FORCE_SCORE, score)
    score = jnp.where(causal, score, -1.0)
    nbl = -(-nbp // LANE) * LANE
    n_idx = lax.broadcasted_iota(jnp.int32, (nbp, nbl), 0)
    m_idx = lax.broadcasted_iota(jnp.int32, (nbp, nbl), 1)
    lane_tok = lax.broadcasted_iota(jnp.int32, (nbp, ROWS), 1) & (n_tok - 1)
    rank = jnp.zeros((nbp, ROWS), jnp.float32)
    for t in range(n_tok):
        s_col = score[:, t:t + 1]
        s_row = jnp.sum(jnp.where(n_idx == m_idx, s_col, 0.0), axis=0, keepdims=True)
        s_row = jnp.where(m_idx[0:1] < n_sel_blocks, s_row, -2.0)
        ahead = (s_row > s_col) | ((s_row == s_col) & (m_idx < n_idx))
        rank_t = jnp.sum(ahead.astype(jnp.float32), axis=1, keepdims=True)
        rank = jnp.where(lane_tok == t, rank_t, rank)
    chosen = (rank < N_SEL) & causal & (blk < n_sel_blocks)
    pen_ref[...] = jnp.where(chosen, 0.0, NEG_INF)


def sample_cmp_select(qs, kc, bias, gate, past, n_sel_blocks):
    batch = qs.shape[0]
    n = kc.shape[3]
    nbp = -(-n_sel_blocks // SUBLANE) * SUBLANE
    return pl.pallas_call(
        functools.partial(_sample_cmp_body, n_sel_blocks=n_sel_blocks, past=past),
        grid=(batch, KV_HEADS),
        in_specs=[pl.BlockSpec((None, None, ROWS, HEAD_DIM), lambda b, g: (b, g, 0, 0)),
                  pl.BlockSpec((None, None, 2, n, HEAD_DIM), lambda b, g: (b, g, 0, 0, 0)),
                  pl.BlockSpec((None, n, ROWS), lambda b, g: (g, 0, 0)),
                  pl.BlockSpec((None, None, ROWS, HEAD_DIM), lambda b, g: (b, g, 0, 0))],
        out_specs=[pl.BlockSpec((None, None, ROWS, HEAD_DIM), lambda b, g: (b, g, 0, 0)),
                   pl.BlockSpec((None, None, nbp, ROWS), lambda b, g: (b, g, 0, 0))],
        out_shape=[jax.ShapeDtypeStruct((batch, KV_HEADS, ROWS, HEAD_DIM), jnp.float32),
                   jax.ShapeDtypeStruct((batch, KV_HEADS, nbp, ROWS), jnp.float32)],
        compiler_params=_cparams("parallel", "parallel"),
        name="sample_cmp_select",
    )(qs, kc, bias, gate)


ATTN_PAGES = 4


def _paged_attn_body(*refs, pen_block, gated, n_pages):
    it = iter(refs)
    pt_ref, tid_ref, q_ref = next(it), next(it), next(it)
    pages = [next(it) for _ in range(ATTN_PAGES)]
    new_ref, bt_ref = next(it), next(it)
    pen_ref = next(it) if pen_block else None
    gt_ref = next(it) if gated else None
    o_ref, m_ref, l_ref, acc_ref = next(it), next(it), next(it), next(it)
    step = pl.program_id(1)

    @pl.when(step == 0)
    def _():
        m_ref[...] = jnp.full(m_ref.shape, NEG_INF, jnp.float32)
        l_ref[...] = jnp.zeros(l_ref.shape, jnp.float32)
        acc_ref[...] = jnp.zeros(acc_ref.shape, jnp.float32)

    def page_terms(g, page, n_keys):
        tile = tid_ref[page]
        b = bt_ref[g, tile] if n_keys == PAGE else bt_ref[g, tile, 0:n_keys, :]
        if not pen_block:
            return b
        if pen_block >= PAGE:
            return b + pen_ref[g, pl.ds(page // (pen_block // PAGE), 1), :]
        per_page = PAGE // pen_block
        pieces = [b[a * pen_block:min((a + 1) * pen_block, n_keys)] + pen_ref[g, pl.ds(page * per_page + a, 1), :]
                  for a in range(-(-n_keys // pen_block))]
        return pieces[0] if len(pieces) == 1 else jnp.concatenate(pieces, axis=0)

    def attend(g, k, v, terms):
        s = _dot(_bf(k), _bf(q_ref[g] * ATTN_SCALE), NT_DIMS) + terms
        m_prev = m_ref[g]
        m_new = jnp.maximum(m_prev, jnp.max(s, axis=0, keepdims=True))
        alpha = jnp.exp(m_prev - m_new)
        p = jnp.exp(s - m_new)
        l_ref[g] = alpha * l_ref[g] + jnp.sum(p, axis=0, keepdims=True)
        acc_ref[g] = _rows_to_col(alpha, ROWS) * acc_ref[g] + _dot(_bf(p), _bf(v), TN_DIMS)
        m_ref[g] = m_new

    half = KV_HEADS * HEAD_DIM
    for kk in range(ATTN_PAGES):
        for g in range(KV_HEADS):
            attend(g, _page_slab(pages[kk], g), _page_slab(pages[kk], KV_HEADS + g),
                   page_terms(g, step * ATTN_PAGES + kk, PAGE))

    @pl.when(step == pl.num_programs(1) - 1)
    def _():
        n_new = new_ref.shape[0]
        for g in range(KV_HEADS):
            attend(g, new_ref[:, g * HEAD_DIM:(g + 1) * HEAD_DIM],
                   new_ref[:, half + g * HEAD_DIM:half + (g + 1) * HEAD_DIM], page_terms(g, n_pages, n_new))
            o = acc_ref[g] / _rows_to_col(jnp.maximum(l_ref[g], 1e-30), ROWS)
            if gated:
                o = o * jax.nn.sigmoid(gt_ref[g])
            o_ref[g] = o


def paged_attention(qs, pool, page_table, tile_ids, new_kv, bias_tiles, *, pen=None, pen_block=0, gate=None):
    batch, n_pages = page_table.shape
    steps = n_pages // ATTN_PAGES
    n_new = new_kv.shape[1]

    def page_spec(k):
        return pl.BlockSpec((None, PAGE_ROWS, HEAD_DIM), lambda b, s, pt, tid: (pt[b, s * ATTN_PAGES + k], 0, 0))

    grp = lambda b, s, pt, tid: (b, 0, 0, 0)
    in_specs = ([pl.BlockSpec((None, KV_HEADS, ROWS, HEAD_DIM), grp)] + [page_spec(k) for k in range(ATTN_PAGES)]
                + [pl.BlockSpec((None, n_new, KV_WIDTH), lambda b, s, pt, tid: (b, 0, 0)),
                   pl.BlockSpec(bias_tiles.shape, lambda b, s, pt, tid: (0, 0, 0, 0))])
    args = [qs] + [pool] * ATTN_PAGES + [new_kv, bias_tiles]
    if pen is not None:
        in_specs.append(pl.BlockSpec((None,) + pen.shape[1:], grp))
        args.append(pen)
    if gate is not None:
        in_specs.append(pl.BlockSpec((None, KV_HEADS, ROWS, HEAD_DIM), grp))
        args.append(gate)
    return pl.pallas_call(
        functools.partial(_paged_attn_body, pen_block=pen_block if pen is not None else 0, gated=gate is not None,
                          n_pages=n_pages),
        grid_spec=pltpu.PrefetchScalarGridSpec(
            num_scalar_prefetch=2, grid=(batch, steps), in_specs=in_specs,
            out_specs=pl.BlockSpec((None, KV_HEADS, ROWS, HEAD_DIM), grp),
            scratch_shapes=[pltpu.VMEM((KV_HEADS, 1, ROWS), jnp.float32), pltpu.VMEM((KV_HEADS, 1, ROWS), jnp.float32),
                            pltpu.VMEM((KV_HEADS, ROWS, HEAD_DIM), jnp.float32)]),
        out_shape=jax.ShapeDtypeStruct((batch, KV_HEADS, ROWS, HEAD_DIM), jnp.float32),
        compiler_params=_cparams("parallel", "arbitrary"),
        name="paged_attention",
    )(page_table, tile_ids, *args)


def _moba_sample_gate_body(pt_ref, q_ref, *refs, n_blocks, past):
    pages, (pen_ref, km_ref) = refs[:ATTN_PAGES], refs[ATTN_PAGES:]
    step = pl.program_id(1)
    per_block = MOBA_BLOCK // PAGE

    @pl.when(step == 0)
    def _():
        km_ref[...] = jnp.zeros(km_ref.shape, jnp.float32)

    for kk in range(ATTN_PAGES):
        blk = (step * ATTN_PAGES + kk) // per_block
        slab_sums = jnp.sum(pages[kk][...], axis=0)
        for g in range(KV_HEADS):
            km_ref[g, pl.ds(blk, 1), :] += slab_sums[g:g + 1]

    @pl.when(step == pl.num_programs(1) - 1)
    def _():
        nbp = pen_ref.shape[1]
        blk = lax.broadcasted_iota(jnp.int32, (nbp, ROWS), 0)
        n_tok = ROWS // GROUP
        tok = past + (lax.broadcasted_iota(jnp.int32, (nbp, ROWS), 1) & (n_tok - 1))
        own = lax.shift_right_logical(tok, int(math.log2(MOBA_BLOCK)))
        for g in range(KV_HEADS):
            kh, kl = _split_bf16(km_ref[g] / MOBA_BLOCK, 2)
            qh, ql = _split_bf16(q_ref[g], 2)
            gate = _dot(kh, qh, NT_DIMS) + _dot(kh, ql, NT_DIMS) + _dot(kl, qh, NT_DIMS)
            gate = jnp.where(blk < own, gate, NEG_INF)
            chosen = ((_rank_rows(gate, n_blocks) < MOBA_TOPK) & (blk < own)) | (blk == own)
            pen_ref[g] = jnp.where(chosen, 0.0, NEG_INF)


def moba_sample_gate(qs, pool, page_table, past, n_new):
    batch, n_pages = page_table.shape
    steps = n_pages // ATTN_PAGES
    n_blocks = -(-(past + n_new) // MOBA_BLOCK)
    nbp = -(-n_blocks // SUBLANE) * SUBLANE

    def page_spec(k):
        return pl.BlockSpec((None, PAGE, None, KV_HEADS, HEAD_DIM),
                            lambda b, s, pt: (pt[b, s * ATTN_PAGES + k], 0, 0, 0, 0))

    return pl.pallas_call(
        functools.partial(_moba_sample_gate_body, n_blocks=n_blocks, past=past),
        grid_spec=pltpu.PrefetchScalarGridSpec(
            num_scalar_prefetch=1, grid=(batch, steps),
            in_specs=[pl.BlockSpec((None, KV_HEADS, ROWS, HEAD_DIM), lambda b, s, pt: (b, 0, 0, 0))]
            + [page_spec(k) for k in range(ATTN_PAGES)],
            out_specs=pl.BlockSpec((None, KV_HEADS, nbp, ROWS), lambda b, s, pt: (b, 0, 0, 0)),
            scratch_shapes=[pltpu.VMEM((KV_HEADS, nbp, HEAD_DIM), jnp.float32)]),
        out_shape=jax.ShapeDtypeStruct((batch, KV_HEADS, nbp, ROWS), jnp.float32),
        compiler_params=_cparams("parallel", "arbitrary"),
        name="moba_sample_gate",
    )(page_table, qs, *([pool] * ATTN_PAGES))


def _sample_rows(x, batch, n_tok):
    return x.reshape(batch, n_tok, KV_HEADS, GROUP, HEAD_DIM).transpose(0, 2, 3, 1, 4).reshape(
        batch, KV_HEADS, GROUP * n_tok, HEAD_DIM)


def _sample_unrows(o, batch, n_tok):
    return o.reshape(batch, KV_HEADS, GROUP, n_tok, HEAD_DIM).transpose(0, 3, 1, 2, 4).reshape(batch * n_tok, Q_WIDTH)


def _sample_gate_rows(tail, branch, batch, n_tok):
    gt = tail[:, branch * N_HEADS:(branch + 1) * N_HEADS].reshape(batch, n_tok, KV_HEADS, GROUP)
    gt = gt.transpose(0, 2, 3, 1).reshape(batch, KV_HEADS, GROUP * n_tok, 1)
    return jnp.broadcast_to(gt, (batch, KV_HEADS, GROUP * n_tok, HEAD_DIM))


def _pad_new(kv_new, batch, n_tok):
    return jnp.pad(kv_new.reshape(batch, n_tok, KV_WIDTH), ((0, 0), (0, SUBLANE - n_tok), (0, 0)))


def _nsa_sample_pallas(main, tail, cache_c, cache_s, cache_w, page_table, cmp_w1, cmp_w2, cmp_pe, rel_table):
    batch, n_pages = page_table.shape
    n_tok = main.shape[0] // batch
    past = n_pages * PAGE
    wbuf = cache_w.shape[1]
    assert n_tok == ROWS // GROUP and cache_c.shape[1] == PAGE and wbuf == WINDOW and WINDOW % PAGE == 0
    n_cmp = (past + n_tok - CMP_BLOCK) // CMP_STRIDE + 1
    assert n_cmp + CMP_BLOCK // CMP_STRIDE - 1 == past // CMP_STRIDE
    qs = _sample_rows(main[:, :Q_WIDTH], batch, n_tok)
    kv_new = [main[:, Q_WIDTH + c * KV_WIDTH:Q_WIDTH + (c + 1) * KV_WIDTH] for c in range(3)]
    tiles = _sample_bias_tiles(rel_table, past, n_tok)
    flat = lambda pool: pool.reshape(pool.shape[0], PAGE_ROWS, HEAD_DIM)

    kc = compress_finish(compress_part_paged(flat(cache_c), page_table, cmp_w1), cmp_w1, cmp_w2, cmp_pe)
    n = kc.shape[3]
    dist = past + np.arange(n_tok)[None, :] - (np.arange(n)[:, None] * CMP_STRIDE + CMP_BLOCK - 1)
    ok = (dist >= 0) & (np.arange(n)[:, None] < n_cmp)
    cb = jnp.where(ok[..., None], _bias_lookup(rel_table, dist), NEG_INF)
    cb = cb.reshape(n, n_tok, KV_HEADS, GROUP).transpose(2, 0, 3, 1).reshape(KV_HEADS, n, GROUP * n_tok)
    n_sel_blocks = -(-(past + n_tok) // SEL_BLOCK)
    o_cmp, pen = sample_cmp_select(qs, kc, cb, _sample_gate_rows(tail, 0, batch, n_tok), past, n_sel_blocks)

    far_then_last = jnp.asarray([0] * (n_pages - 1) + [2, 3], jnp.int32)
    o_sel = paged_attention(qs, flat(cache_s), page_table, far_then_last, _pad_new(kv_new[1], batch, n_tok), tiles,
                            pen=pen, pen_block=SEL_BLOCK, gate=_sample_gate_rows(tail, 1, batch, n_tok))
    w_pages = wbuf // PAGE
    win_table = jnp.arange(batch * w_pages, dtype=jnp.int32).reshape(batch, w_pages)
    win_tiles = jnp.asarray([1] + [0] * (w_pages - 2) + [2, 3], jnp.int32)
    o_win = paged_attention(qs, cache_w.reshape(batch * w_pages, PAGE_ROWS, HEAD_DIM), win_table, win_tiles,
                            _pad_new(kv_new[2], batch, n_tok), tiles, gate=_sample_gate_rows(tail, 2, batch, n_tok))
    outs = [_sample_unrows(o, batch, n_tok) for o in (o_cmp, o_sel, o_win)]
    shape = (batch, n_tok, 2, KV_HEADS, HEAD_DIM)
    new_win = jnp.concatenate([cache_w[:, n_tok:], kv_new[2].reshape(shape)], axis=1)
    return outs, kv_new[0].reshape(shape), kv_new[1].reshape(shape), new_win


def _moba_sample_pallas(proj, cache_kv, page_table, rel_table):
    batch, n_pages = page_table.shape
    n_tok = proj.shape[0] // batch
    past = n_pages * PAGE
    assert n_tok == ROWS // GROUP and (past // MOBA_BLOCK) * MOBA_BLOCK == past
    qs = _sample_rows(proj[:, :Q_WIDTH], batch, n_tok)
    kv_new = proj[:, Q_WIDTH:]
    pool = cache_kv.reshape(cache_kv.shape[0], PAGE_ROWS, HEAD_DIM)
    pen = moba_sample_gate(qs, cache_kv, page_table, past, n_tok)
    tile_ids = jnp.asarray([0] * (n_pages - 1) + [2, 3], jnp.int32)
    o = paged_attention(qs, pool, page_table, tile_ids, _pad_new(kv_new, batch, n_tok),
                        _sample_bias_tiles(rel_table, past, n_tok), pen=pen, pen_block=MOBA_BLOCK)
    return _sample_unrows(o, batch, n_tok), kv_new.reshape(batch, n_tok, 2, KV_HEADS, HEAD_DIM)


def _cols_to_row(col, n):
    eye = lax.broadcasted_iota(jnp.int32, (n, n), 0) == lax.broadcasted_iota(jnp.int32, (n, n), 1)
    return jnp.sum(jnp.where(eye, jnp.broadcast_to(col, (n, n)), 0.0), axis=0, keepdims=True)


def _hgrn2_step_body(q_ref, f_ref, i_ref, g_ref, lbl_ref, ng_ref, s0_ref, o_ref, s_out_ref, *, layer):
    n_tok = q_ref.shape[0]
    lbl = lbl_ref[...]
    e = jnp.exp(lbl - jnp.max(lbl, axis=0, keepdims=True))
    p = e / jnp.sum(e, axis=0, keepdims=True)
    lb = jnp.zeros((1, HG_DK), jnp.float32)
    for r in range(1, layer + 1):
        lb = lb + p[r:r + 1]
    q = q_ref[...]
    qh = q * jax.nn.sigmoid(q) * HG_DK ** -0.5
    fg = lb + (1.0 - lb) * jax.nn.sigmoid(f_ref[...])
    k = 1.0 - fg
    v = i_ref[...]
    st = s0_ref[...].T
    rows = []
    for t in range(n_tok):
        st = st * fg[t:t + 1] + _rows_to_col(v[t:t + 1], HG_DV) * k[t:t + 1]
        rows.append(_cols_to_row(jnp.sum(st * qh[t:t + 1], axis=1, keepdims=True), HG_DV))
    o_ref[...] = _head_rms_gate(jnp.concatenate(rows, axis=0), ng_ref[...], g_ref[...])
    s_out_ref[...] = st.T


def hgrn2_step(proj, lb_logits, norm_g, s0, layer):
    batch, n_tok, _ = proj.shape
    h = HG_HEADS

    def col(k):
        return pl.BlockSpec((None, n_tok, HG_DK), lambda b, hh: (b, 0, k * h + hh))

    return pl.pallas_call(
        functools.partial(_hgrn2_step_body, layer=layer),
        grid=(batch, h),
        in_specs=[col(0), col(1), col(2), col(3),
                  pl.BlockSpec((DEPTH, HG_DK), lambda b, hh: (0, hh)),
                  pl.BlockSpec((1, HG_DV), lambda b, hh: (0, 0)),
                  pl.BlockSpec((None, None, HG_DK, HG_DV), lambda b, hh: (b, hh, 0, 0))],
        out_specs=[pl.BlockSpec((None, n_tok, HG_DV), lambda b, hh: (b, 0, hh)),
                   pl.BlockSpec((None, None, HG_DK, HG_DV), lambda b, hh: (b, hh, 0, 0))],
        out_shape=[jax.ShapeDtypeStruct((batch, n_tok, h * HG_DV), jnp.float32),
                   jax.ShapeDtypeStruct((batch, h, HG_DK, HG_DV), jnp.float32)],
        compiler_params=_cparams("parallel", "parallel"),
        name="hgrn2_step",
    )(proj, proj, proj, proj, lb_logits, norm_g.reshape(1, HG_DV), s0)


def _gdn_step_body(q_ref, k_ref, v_ref, z_ref, t_ref, bq_ref, bk_ref, bv_ref, wq_ref, wk_ref, wv_ref,
                   al_ref, dtb_ref, ng_ref, s0_ref, o_ref, s_out_ref, xs_ref):
    hq = pl.program_id(1)
    n_tok = q_ref.shape[0]
    dk, dv = GDN_DK, GDN_DV
    pad = SUBLANE
    xs_ref[0:pad, :] = jnp.concatenate([bq_ref[...], bk_ref[...], bv_ref[...]], axis=1)
    x = jnp.concatenate([q_ref[...], k_ref[...], v_ref[...]], axis=1)
    xs_ref[pad:pad + n_tok, :] = x
    cw = jnp.concatenate([wq_ref[...], wk_ref[...], wv_ref[...]], axis=1)
    y = xs_ref[pad - 3:pad - 3 + n_tok, :] * cw[0:1]
    for i in range(1, GDN_CONV - 1):
        y = y + xs_ref[pad - 3 + i:pad - 3 + i + n_tok, :] * cw[i:i + 1]
    y = y + x * cw[GDN_CONV - 1:GDN_CONV]
    y = y * jax.nn.sigmoid(y)
    q = _l2n(y[:, 0:dk]) * dk ** -0.5
    k = _l2n(y[:, dk:2 * dk])
    tl = t_ref[...]
    beta_all = jax.nn.sigmoid(tl)
    la_all = -jnp.exp(al_ref[...]) * _softplus(tl + dtb_ref[...])
    for e in range(GDN_REP):
        hv = hq * GDN_REP + e
        v = y[:, 2 * dk + e * dv:2 * dk + (e + 1) * dv]
        bt = _lane_column(beta_all, hv)
        a = jnp.exp(_lane_column(la_all, GDN_V_HEADS + hv))
        s = s0_ref[e]
        rows = []
        for t in range(n_tok):
            k_col = _rows_to_col(k[t:t + 1], dk)
            ks = jnp.sum(s * k_col, axis=0, keepdims=True)
            u = bt[t:t + 1] * (v[t:t + 1] - a[t:t + 1] * ks)
            s = a[t:t + 1] * s + k_col * u
            rows.append(jnp.sum(s * _rows_to_col(q[t:t + 1], dk), axis=0, keepdims=True))
        s_out_ref[e] = s
        o_ref[:, e * dv:(e + 1) * dv] = _head_rms_gate(jnp.concatenate(rows, axis=0), ng_ref[...],
                                                       z_ref[:, e * dv:(e + 1) * dv])


def gdn_step(main, tail, conv_buf, conv_w, a_log, dt_bias, norm_g, s0):
    batch, n_tok, _ = main.shape
    hq, rep, dk, dv = GDN_QK_HEADS, GDN_REP, GDN_DK, GDN_DV
    vw = rep * dv
    buf = jnp.pad(conv_buf, ((0, 0), (SUBLANE - (GDN_CONV - 1), 0), (0, 0)))
    pad_lanes = jnp.zeros((LANE - 2 * GDN_V_HEADS,), jnp.float32)
    a_row = jnp.concatenate([jnp.zeros((GDN_V_HEADS,), jnp.float32), a_log, pad_lanes]).reshape(1, LANE)
    dt_row = jnp.concatenate([jnp.zeros((GDN_V_HEADS,), jnp.float32), dt_bias, pad_lanes]).reshape(1, LANE)
    k0 = hq
    v0 = 2 * hq * dk // vw
    z0 = GDN_CONV_DIM // vw
    return pl.pallas_call(
        _gdn_step_body,
        grid=(batch, hq),
        in_specs=[pl.BlockSpec((None, n_tok, dk), lambda b, h: (b, 0, h)),
                  pl.BlockSpec((None, n_tok, dk), lambda b, h: (b, 0, k0 + h)),
                  pl.BlockSpec((None, n_tok, vw), lambda b, h: (b, 0, v0 + h)),
                  pl.BlockSpec((None, n_tok, vw), lambda b, h: (b, 0, z0 + h)),
                  pl.BlockSpec((None, n_tok, LANE), lambda b, h: (b, 0, 0)),
                  pl.BlockSpec((None, SUBLANE, dk), lambda b, h: (b, 0, h)),
                  pl.BlockSpec((None, SUBLANE, dk), lambda b, h: (b, 0, k0 + h)),
                  pl.BlockSpec((None, SUBLANE, vw), lambda b, h: (b, 0, v0 + h)),
                  pl.BlockSpec((GDN_CONV, dk), lambda b, h: (0, h)),
                  pl.BlockSpec((GDN_CONV, dk), lambda b, h: (0, k0 + h)),
                  pl.BlockSpec((GDN_CONV, vw), lambda b, h: (0, v0 + h)),
                  pl.BlockSpec((1, LANE), lambda b, h: (0, 0)),
                  pl.BlockSpec((1, LANE), lambda b, h: (0, 0)),
                  pl.BlockSpec((1, dv), lambda b, h: (0, 0)),
                  pl.BlockSpec((None, rep, dk, dv), lambda b, h: (b, h, 0, 0))],
        out_specs=[pl.BlockSpec((None, n_tok, vw), lambda b, h: (b, 0, h)),
                   pl.BlockSpec((None, rep, dk, dv), lambda b, h: (b, h, 0, 0))],
        out_shape=[jax.ShapeDtypeStruct((batch, n_tok, GDN_V_HEADS * dv), jnp.float32),
                   jax.ShapeDtypeStruct((batch, GDN_V_HEADS, dk, dv), jnp.float32)],
        scratch_shapes=[pltpu.VMEM((2 * SUBLANE, 2 * dk + vw), jnp.float32)],
        compiler_params=_cparams("parallel", "parallel"),
        name="gdn_step",
    )(main, main, main, main, tail, buf, buf, buf, conv_w, conv_w, conv_w, a_row, dt_row, norm_g.reshape(1, dv), s0)


def _pad_cols(w, mult=LANE):
    n = w.shape[1]
    return jnp.pad(w, ((0, 0), (0, (-n) % mult)))


def _nsa_prompt(main, kv_rows, tail, batch, seq, cmp_w1, cmp_w2, cmp_pe, rel_table):
    kv_c, kv_s, kv_w = (kv_rows[c].reshape(batch, seq, 2, KV_HEADS, HEAD_DIM) for c in range(3))
    kc = compress_finish(compress_part_rows(main, Q_WIDTH // KV_WIDTH, cmp_w1, batch, seq), cmp_w1, cmp_w2, cmp_pe)
    o_cmp, pen = cmp_select(main, tail, kc, _cmp_bias_table(rel_table, kc.shape[3]), batch, seq)
    col = Q_WIDTH // HEAD_DIM
    o_sel = flash_attention(main, main, col + 2 * KV_HEADS, col + 3 * KV_HEADS, _flash_bias_tiles(rel_table, 0),
                            batch, seq, pen=pen, pen_block=SEL_BLOCK, gate_arr=tail, gate_col0=N_HEADS)
    o_win = flash_attention(main, main, col + 4 * KV_HEADS, col + 5 * KV_HEADS,
                            _flash_bias_tiles(rel_table, WINDOW), batch, seq, k_back=WINDOW // ATTN_TILE,
                            gate_arr=tail, gate_col0=2 * N_HEADS)
    return [o_cmp, o_sel, o_win], kv_c, kv_s, kv_w[:, -min(WINDOW, seq):]


def _moba_prompt(proj, batch, seq, rel_table):
    col = Q_WIDTH // HEAD_DIM
    pen = moba_gate(proj, col, batch, seq)
    return flash_attention(proj, proj, col, col + KV_HEADS, _flash_bias_tiles(rel_table, 0), batch, seq,
                           pen=pen, pen_block=MOBA_BLOCK)


def kernel(x_prompt, x_sample, cache_nsa_cmp_kv, cache_nsa_sel_kv, cache_nsa_win_kv, cache_moba_kv,
           state_hgrn2, state_gdn_conv, state_gdn_ssm, page_table, rel_table, ln_mix, ln_ffn, ln_final,
           ffn_w_up, ffn_w_down, nsa_w_in, nsa_cmp_w1, nsa_cmp_w2, nsa_cmp_pe, nsa_w_out, moba_w_in, moba_w_out,
           hg_w_in, hg_lb_logits, hg_norm, hg_w_out, gdn_w_in, gdn_conv_w, gdn_a_log, gdn_dt_bias, gdn_norm,
           gdn_w_out):
    bf = jnp.bfloat16
    bp, tp = x_prompt.shape[:2]
    bs, ts = x_sample.shape[:2]
    assert tp % ATTN_TILE == 0 and WINDOW % ATTN_TILE == 0 and ATTN_TILE == MOBA_BLOCK and tp % SCAN_TILE == 0
    xp = x_prompt.reshape(bp * tp, D_MODEL)
    xs = x_sample.reshape(bs * ts, D_MODEL)
    w_up, w_down = ffn_w_up.astype(bf), ffn_w_down.astype(bf)

    for layer in range(DEPTH):
        kind = layer % N_MIXERS
        g_mix = ln_mix[layer]
        if kind == 0:
            w_main, w_tail = nsa_w_in.astype(bf), _pad_cols(nsa_w_in[:, NSA_MAIN:]).astype(bf)
            main_p, kv_p = norm_matmul(xp, g_mix, w_main, kv_from=Q_WIDTH // KV_WIDTH, n_cols=NSA_MAIN)
            tail_p = norm_matmul(xp, g_mix, w_tail)
            main_s, tail_s = norm_matmul(xs, g_mix, w_main, n_cols=NSA_MAIN), norm_matmul(xs, g_mix, w_tail)
            op, nsa_cmp_p, nsa_sel_p, nsa_win_p = _nsa_prompt(main_p, kv_p, tail_p, bp, tp, nsa_cmp_w1, nsa_cmp_w2,
                                                              nsa_cmp_pe, rel_table)
            os_, nsa_cmp_s, nsa_sel_s, nsa_win_s = _nsa_sample_pallas(main_s, tail_s, cache_nsa_cmp_kv,
                                                                     cache_nsa_sel_kv, cache_nsa_win_kv, page_table,
                                                                     nsa_cmp_w1, nsa_cmp_w2, nsa_cmp_pe, rel_table)
            w_out = nsa_w_out.astype(bf)
        elif kind == 1:
            w_in = moba_w_in.astype(bf)
            pp, kv_p = norm_matmul(xp, g_mix, w_in, kv_from=Q_WIDTH // KV_WIDTH)
            op, moba_p = _moba_prompt(pp, bp, tp, rel_table), kv_p.reshape(bp, tp, 2, KV_HEADS, HEAD_DIM)
            os_, moba_s = _moba_sample_pallas(norm_matmul(xs, g_mix, w_in), cache_moba_kv, page_table, rel_table)
            op, os_ = [op], [os_]
            w_out = moba_w_out.astype(bf)
        elif kind == 2:
            w_in = hg_w_in.astype(bf)
            pp = norm_matmul(xp, g_mix, w_in)
            ps = norm_matmul(xs, g_mix, w_in).reshape(bs, ts, -1)
            s0 = jnp.zeros((bp, HG_HEADS, HG_DK, HG_DV), jnp.float32)
            op, hg_p = hgrn2_scan(pp, hg_lb_logits, hg_norm, s0, layer, bp, tp)
            os_, hg_s = hgrn2_step(ps, hg_lb_logits, hg_norm, state_hgrn2, layer)
            op, os_ = [op], [os_.reshape(bs * ts, -1)]
            w_out = hg_w_out.astype(bf)
        else:
            w_main, w_tail = gdn_w_in.astype(bf), _pad_cols(gdn_w_in[:, GDN_MAIN:]).astype(bf)
            main_p, tail_p = norm_matmul(xp, g_mix, w_main, n_cols=GDN_MAIN), norm_matmul(xp, g_mix, w_tail)
            main_s = norm_matmul(xs, g_mix, w_main, n_cols=GDN_MAIN).reshape(bs, ts, -1)
            tail_s = norm_matmul(xs, g_mix, w_tail).reshape(bs, ts, -1)
            buf0 = jnp.zeros((bp, GDN_CONV - 1, GDN_CONV_DIM), jnp.float32)
            s0 = jnp.zeros((bp, GDN_V_HEADS, GDN_DK, GDN_DV), jnp.float32)
            op, ssm_p = gdn_scan(main_p, tail_p, buf0, gdn_conv_w, gdn_a_log, gdn_dt_bias, gdn_norm, s0, bp, tp)
            conv_p = main_p.reshape(bp, tp, -1)[:, tp - (GDN_CONV - 1):, :GDN_CONV_DIM]
            os_, ssm_s = gdn_step(main_s, tail_s, state_gdn_conv, gdn_conv_w, gdn_a_log, gdn_dt_bias, gdn_norm,
                                  state_gdn_ssm)
            conv_s = jnp.concatenate([state_gdn_conv, main_s[:, :, :GDN_CONV_DIM]], axis=1)[:, ts:]
            op, os_ = [op], [os_.reshape(bs * ts, -1)]
            w_out = gdn_w_out.astype(bf)
        xp = matmul_res(op, w_out, xp)
        xs = matmul_res(os_, w_out, xs)
        xp = ffn(xp, ln_ffn[layer], w_up, w_down, layer)
        xs = ffn(xs, ln_ffn[layer], w_up, w_down, layer)
    y_prompt = final_norm(xp, ln_final).reshape(bp, tp, D_MODEL)
    y_sample = final_norm(xs, ln_final).reshape(bs, ts, D_MODEL)
    return (y_prompt, y_sample, nsa_cmp_p, nsa_cmp_s, nsa_sel_p, nsa_sel_s, nsa_win_p, nsa_win_s,
            moba_p, moba_s, hg_p, hg_s, conv_p, conv_s, ssm_p, ssm_s)
```

```python
import functools
import math

import jax
import jax.numpy as jnp
import numpy as np
from jax import lax
from jax.experimental import pallas as pl
from jax.experimental.pallas import tpu as pltpu

D_MODEL = 2048
DEPTH = 4
N_MIXERS = 4
HEAD_DIM = 128
N_HEADS = D_MODEL // HEAD_DIM
KV_HEADS = 4
GROUP = N_HEADS // KV_HEADS
ATTN_SCALE = HEAD_DIM ** -0.5
REL_BUCKETS = 32
REL_MAX_DIST = 128
CMP_BLOCK = 32
CMP_STRIDE = 16
CMP_HIDDEN = HEAD_DIM
SEL_BLOCK = 64
N_SEL = 16
WINDOW = 512
FORCE_SCORE = 1.0e4
MOBA_BLOCK = 256
MOBA_TOPK = 3
HG_DK = 128
HG_HEADS = D_MODEL // HG_DK
HG_DV = D_MODEL // HG_HEADS
GDN_DK = 128
GDN_DV = 128
GDN_QK_HEADS = D_MODEL // GDN_DK
GDN_V_HEADS = 2 * GDN_QK_HEADS
GDN_REP = GDN_V_HEADS // GDN_QK_HEADS
GDN_CONV = 4
GDN_CONV_DIM = 2 * GDN_QK_HEADS * GDN_DK + GDN_V_HEADS * GDN_DV
CHUNK = 64
NEG_INF = -1.0e30
NORM_EPS = 1e-6

Q_WIDTH = N_HEADS * HEAD_DIM
KV_WIDTH = 2 * KV_HEADS * HEAD_DIM
NSA_MAIN = Q_WIDTH + 3 * KV_WIDTH
GDN_MAIN = GDN_CONV_DIM + GDN_V_HEADS * GDN_DV

V7X_VMEM_LIMIT_BYTES = 56 * 1024 * 1024
LANE = 128
SUBLANE = 8
ATTN_TILE = 256
CMP_TILE = 128
SCAN_TILE = 512
HG_HPS = 4
GDN_HPS = 2
LOG2E = math.log2(math.e)
NT_DIMS = (((1,), (1,)), ((), ()))
TN_DIMS = (((0,), (0,)), ((), ()))


def _cparams(*sem):
    return pltpu.CompilerParams(dimension_semantics=sem, vmem_limit_bytes=V7X_VMEM_LIMIT_BYTES)


def _row_tile(m, target):
    t = min(m, target)
    while m % t:
        t //= 2
    return t


def _col_tile(n, target):
    t = min(n, target)
    while n % t or t % LANE:
        t -= LANE
    return t


def _split_bf16(x, parts):
    out = []
    for _ in range(parts - 1):
        hi = x.astype(jnp.bfloat16)
        out.append(hi)
        x = x - hi.astype(jnp.float32)
    out.append(x.astype(jnp.bfloat16))
    return out


def _bf(x):
    return x.astype(jnp.bfloat16)


def _dot(a, b, dims=None):
    if dims is None:
        return jnp.dot(a, b, preferred_element_type=jnp.float32)
    return lax.dot_general(a, b, dims, preferred_element_type=jnp.float32)


def _norm_matmul_body(x_ref, g_ref, w_ref, o_ref, *rest, kv_from):
    kv_ref, h_ref = rest if kv_from is not None else (None, rest[0])
    j = pl.program_id(1)

    @pl.when(j == 0)
    def _():
        x = x_ref[...]
        ms = jnp.mean(x * x, axis=-1, keepdims=True)
        h_ref[...] = _bf(x * lax.rsqrt(ms + NORM_EPS) * g_ref[...])

    out = _dot(h_ref[...], w_ref[...])
    o_ref[...] = out
    if kv_from is not None:
        @pl.when(j >= kv_from)
        def _():
            tm = out.shape[0]
            for slab in range(out.shape[1] // HEAD_DIM):
                kv_ref[pl.ds(slab, tm, stride=out.shape[1] // HEAD_DIM), :] = out[:, slab * HEAD_DIM:(slab + 1) * HEAD_DIM]


def norm_matmul(x, g, w, kv_from=None, n_cols=None):
    m, k = x.shape
    n = n_cols or w.shape[1]
    tm = _row_tile(m, 1024)
    tn = _col_tile(n, 1024)
    out_specs = [pl.BlockSpec((tm, tn), lambda i, j: (i, j))]
    out_shape = [jax.ShapeDtypeStruct((m, n), jnp.float32)]
    if kv_from is not None:
        assert tn == KV_WIDTH
        slabs = tn // HEAD_DIM
        out_specs.append(pl.BlockSpec((None, tm * slabs, HEAD_DIM), lambda i, j: (jnp.maximum(j - kv_from, 0), i, 0)))
        out_shape.append(jax.ShapeDtypeStruct((n // tn - kv_from, m * slabs, HEAD_DIM), jnp.float32))
    res = pl.pallas_call(
        functools.partial(_norm_matmul_body, kv_from=kv_from),
        grid=(m // tm, n // tn),
        in_specs=[pl.BlockSpec((tm, k), lambda i, j: (i, 0)),
                  pl.BlockSpec((1, k), lambda i, j: (0, 0)),
                  pl.BlockSpec((k, tn), lambda i, j: (0, j))],
        out_specs=out_specs,
        out_shape=out_shape,
        scratch_shapes=[pltpu.VMEM((tm, k), jnp.bfloat16)],
        compiler_params=_cparams("parallel", "arbitrary"),
        name="norm_matmul",
    )(x, g.reshape(1, k), w)
    return res if kv_from is not None else res[0]


def _matmul_res_body(*refs):
    *a_refs, w_ref, r_ref, o_ref = refs
    a = a_refs[0][...]
    for a_ref in a_refs[1:]:
        a = a + a_ref[...]
    o_ref[...] = r_ref[...] + _dot(_bf(a), w_ref[...])


def matmul_res(a_list, w, res):
    m, k = a_list[0].shape
    n = w.shape[1]
    tm = _row_tile(m, 512)
    tn = _col_tile(n, 1024)
    return pl.pallas_call(
        _matmul_res_body,
        grid=(m // tm, n // tn),
        in_specs=[pl.BlockSpec((tm, k), lambda i, j: (i, 0)) for _ in a_list]
        + [pl.BlockSpec((k, tn), lambda i, j: (0, j)),
           pl.BlockSpec((tm, tn), lambda i, j: (i, j))],
        out_specs=pl.BlockSpec((tm, tn), lambda i, j: (i, j)),
        out_shape=jax.ShapeDtypeStruct((m, n), jnp.float32),
        compiler_params=_cparams("parallel", "arbitrary"),
        name="matmul_res",
    )(*a_list, w, res)


def _ffn_body(x_ref, g_ref, wa_ref, wb_ref, wd_ref, o_ref, h_ref):
    @pl.when(pl.program_id(1) == 0)
    def _():
        x = x_ref[...]
        ms = jnp.mean(x * x, axis=-1, keepdims=True)
        h_ref[...] = _bf(x * lax.rsqrt(ms + NORM_EPS) * g_ref[...])
        o_ref[...] = x

    h = h_ref[...]
    a = _dot(h, wa_ref[...])
    b = _dot(h, wb_ref[...])
    o_ref[...] += _dot(_bf(a * jax.nn.sigmoid(a) * b), wd_ref[...])


def ffn(x, g, w_up, w_down, layer):
    m, k = x.shape
    hdim = w_down.shape[1]
    tm = _row_tile(m, 1024)
    th = _col_tile(hdim, 512)
    nh = hdim // th
    return pl.pallas_call(
        _ffn_body,
        grid=(m // tm, nh),
        in_specs=[pl.BlockSpec((tm, k), lambda i, j: (i, 0)),
                  pl.BlockSpec((1, k), lambda i, j: (0, 0)),
                  pl.BlockSpec((None, k, th), lambda i, j: (layer, 0, j)),
                  pl.BlockSpec((None, k, th), lambda i, j: (layer, 0, j + nh)),
                  pl.BlockSpec((None, th, k), lambda i, j: (layer, j, 0))],
        out_specs=pl.BlockSpec((tm, k), lambda i, j: (i, 0)),
        out_shape=jax.ShapeDtypeStruct((m, k), jnp.float32),
        scratch_shapes=[pltpu.VMEM((tm, k), jnp.bfloat16)],
        compiler_params=_cparams("parallel", "arbitrary"),
        name="ffn",
    )(x, g.reshape(1, k), w_up, w_up, w_down)


def _norm_body(x_ref, g_ref, o_ref):
    x = x_ref[...]
    ms = jnp.mean(x * x, axis=-1, keepdims=True)
    o_ref[...] = x * lax.rsqrt(ms + NORM_EPS) * g_ref[...]


def final_norm(x, g):
    m, k = x.shape
    tm = _row_tile(m, 512)
    return pl.pallas_call(
        _norm_body,
        grid=(m // tm,),
        in_specs=[pl.BlockSpec((tm, k), lambda i: (i, 0)), pl.BlockSpec((1, k), lambda i: (0, 0))],
        out_specs=pl.BlockSpec((tm, k), lambda i: (i, 0)),
        out_shape=jax.ShapeDtypeStruct((m, k), jnp.float32),
        compiler_params=_cparams("parallel"),
        name="final_norm",
    )(x, g.reshape(1, k))


def _bucket_np(dist):
    exact = REL_BUCKETS // 2
    d = np.maximum(dist, 0)
    ratio = np.log(np.maximum(d, 1).astype(np.float32) / exact) / math.log(REL_MAX_DIST / exact)
    large = np.minimum(exact + (ratio * (REL_BUCKETS - exact)).astype(np.int32), REL_BUCKETS - 1)
    return np.where(d < exact, d, large)


def _bias_lookup(rel_table, dist):
    bucket = _bucket_np(dist).astype(np.int32)
    ids = [int(b) for b in np.unique(bucket)]
    bk = jnp.asarray(bucket)[..., None]
    out = jnp.broadcast_to(rel_table[ids[0]], bucket.shape + (rel_table.shape[1],))
    for b in ids[1:]:
        out = jnp.where(bk == b, rel_table[b], out)
    return out


def _heads_to_lanes(t):
    keys, queries, _ = t.shape
    return t.reshape(keys, queries, KV_HEADS, GROUP).transpose(2, 0, 3, 1).reshape(KV_HEADS, keys, GROUP * queries)


def _flash_bias_tiles(rel_table, window):
    j = np.arange(ATTN_TILE)[:, None]
    i = np.arange(ATTN_TILE)[None, :]
    n_cls = window // ATTN_TILE + 1 if window else -(-REL_MAX_DIST // ATTN_TILE) + 2
    tiles = []
    for d in range(n_cls):
        dist = d * ATTN_TILE + i - j
        ok = dist >= 0
        if window:
            ok = ok & (dist < window)
        tiles.append(_heads_to_lanes(jnp.where(ok[..., None], _bias_lookup(rel_table, dist) * LOG2E, NEG_INF)))
    return jnp.stack(tiles, axis=1)


def _cmp_bias_table(rel_table, ncp):
    x = np.arange(ncp)[:, None]
    i = np.arange(CMP_TILE)[None, :]
    dist = i - CMP_STRIDE * (x - 16) - (CMP_BLOCK - 1)
    far = rel_table[REL_BUCKETS - 1]
    b = _heads_to_lanes(jnp.where((dist >= 0)[..., None], _bias_lookup(rel_table, dist), far))
    return jnp.concatenate([b, b], axis=1)


def _stack_heads(q):
    return jnp.concatenate([q[:, r * HEAD_DIM:(r + 1) * HEAD_DIM] for r in range(GROUP)], axis=0)


def _gate_columns(gt_ref, col0):
    gt = jax.nn.sigmoid(gt_ref[...])
    lane = lax.broadcasted_iota(jnp.int32, gt.shape, 1)
    return [jnp.sum(jnp.where(lane == col0 + r, gt, 0.0), axis=1, keepdims=True) for r in range(GROUP)]


def _heads_from_lanes(o_t, rows, cols=None):
    parts = []
    for r in range(GROUP):
        part = o_t[:, r * rows:(r + 1) * rows].T
        if cols is not None:
            part = part * cols[r]
        parts.append(part)
    return jnp.concatenate(parts, axis=1)


def _rank_rows(score, n_rows):
    row = lax.broadcasted_iota(jnp.int32, score.shape, 0)
    rank = jnp.zeros(score.shape, jnp.int32)
    for mm in range(n_rows):
        sm = score[mm:mm + 1, :]
        ahead = (sm > score) | ((sm == score) & (row > mm))
        rank = rank + ahead.astype(jnp.int32)
    return rank


def _cmp_select_body(q_ref, kc_ref, dt_ref, gt_ref, o_ref, pen_ref, *, n_sel_blocks):
    g = pl.program_id(1)
    qi = pl.program_id(2)
    tq = CMP_TILE
    cols = GROUP * tq
    ncp = kc_ref.shape[1]
    q4 = _bf(_stack_heads(q_ref[...]) * ATTN_SCALE)
    s = _dot(_bf(kc_ref[0]), q4, NT_DIMS)
    shift = (qi * (tq // CMP_STRIDE) + ncp - 16) % ncp
    bias = dt_ref[pl.ds(pl.multiple_of(ncp - shift, SUBLANE), ncp), :]
    t_col = qi * tq + (lax.broadcasted_iota(jnp.int32, (ncp, cols), 1) & (tq - 1))
    end_pos = lax.broadcasted_iota(jnp.int32, (ncp, cols), 0) * CMP_STRIDE + (CMP_BLOCK - 1)
    mask = t_col >= end_pos
    s = jnp.where(mask, s + bias, NEG_INF)
    m = jnp.max(s, axis=0, keepdims=True)
    e = jnp.where(mask, jnp.exp(s - m), 0.0)
    p = e / jnp.maximum(jnp.sum(e, axis=0, keepdims=True), 1e-30)
    o_t = _dot(_bf(kc_ref[1].T), _bf(p))
    o_ref[...] = _heads_from_lanes(o_t, tq, _gate_columns(gt_ref, g * GROUP))

    imp = p[:, 0:tq]
    for r in range(1, GROUP):
        imp = imp + p[:, r * tq:(r + 1) * tq]
    ratio = SEL_BLOCK // CMP_STRIDE
    j_i = lax.broadcasted_iota(jnp.int32, (n_sel_blocks, ncp), 0)
    c_i = lax.broadcasted_iota(jnp.int32, (n_sel_blocks, ncp), 1)
    w = _bf((c_i >= ratio * j_i - 1) & (c_i <= ratio * j_i + ratio - 1))
    score = sum(_dot(w, part) for part in _split_bf16(imp, 3))
    blk = lax.broadcasted_iota(jnp.int32, (n_sel_blocks, tq), 0)
    tok = qi * tq + lax.broadcasted_iota(jnp.int32, (n_sel_blocks, tq), 1)
    cur = lax.shift_right_logical(tok, int(math.log2(SEL_BLOCK)))
    forced = (blk == 0) | (blk == cur) | (blk == cur - 1)
    causal = blk <= cur
    score = jnp.where(forced, FORCE_SCORE, score)
    score = jnp.where(causal, score, -1.0)
    chosen = (_rank_rows(score, n_sel_blocks) < N_SEL) & causal
    pen_ref[...] = jnp.where(chosen, 0.0, NEG_INF)


def cmp_select(proj, tail, kc, dt, batch, seq):
    tq = CMP_TILE
    nq = seq // tq
    ncp = kc.shape[3]
    nsb = seq // SEL_BLOCK
    return pl.pallas_call(
        functools.partial(_cmp_select_body, n_sel_blocks=nsb),
        grid=(batch, KV_HEADS, nq),
        in_specs=[pl.BlockSpec((tq, GROUP * HEAD_DIM), lambda b, g, i: (b * nq + i, g)),
                  pl.BlockSpec((None, None, 2, ncp, HEAD_DIM), lambda b, g, i: (b, g, 0, 0, 0)),
                  pl.BlockSpec((None, 2 * ncp, GROUP * tq), lambda b, g, i: (g, 0, 0)),
                  pl.BlockSpec((tq, LANE), lambda b, g, i: (b * nq + i, 0))],
        out_specs=[pl.BlockSpec((tq, GROUP * HEAD_DIM), lambda b, g, i: (b * nq + i, g)),
                   pl.BlockSpec((None, None, nsb, tq), lambda b, g, i: (b, g, 0, i))],
        out_shape=[jax.ShapeDtypeStruct((batch * seq, Q_WIDTH), jnp.float32),
                   jax.ShapeDtypeStruct((batch, KV_HEADS, nsb, seq), jnp.float32)],
        compiler_params=_cparams("parallel", "parallel", "arbitrary"),
        name="cmp_select",
    )(proj, kc, dt, tail)


def _flash_body(*refs, pen_block, pen_per_head, k_back, gate_col0, seq):
    it = iter(refs)
    q_ref, k_ref, v_ref, bt_ref = next(it), next(it), next(it), next(it)
    pen_ref = next(it) if pen_block else None
    gt_ref = next(it) if gate_col0 is not None else None
    o_ref, m_ref, acc_ref, qa_ref, kb_ref, vt_ref, sa_ref, sb_ref = (next(it) for _ in range(8))
    g = pl.program_id(1)
    qi = pl.program_id(2)
    tq = tk = ATTN_TILE
    n_cls = bt_ref.shape[0]

    @pl.when(qi == 0)
    def _():
        vt_ref[HEAD_DIM:, :] = jnp.ones((SUBLANE, seq), jnp.bfloat16)
        for c in range(seq // tk):
            kb_ref[c * tk:(c + 1) * tk, :] = _bf(k_ref[c * tk:(c + 1) * tk, :])
            vt_ref[0:HEAD_DIM, c * tk:(c + 1) * tk] = _bf(v_ref[c * tk:(c + 1) * tk, :].T)

    qa_ref[...] = _bf(_stack_heads(q_ref[...]) * (ATTN_SCALE * LOG2E))
    m_ref[...] = jnp.full(m_ref.shape, NEG_INF, jnp.float32)
    acc_ref[...] = jnp.zeros(acc_ref.shape, jnp.float32)

    def raw_logits(kj):
        k0 = pl.multiple_of(jnp.minimum(kj, qi) * tk, tk)
        return _dot(kb_ref[pl.ds(k0, tk), :], qa_ref[...], NT_DIMS)

    per_tile = tk // pen_block if pen_block else 1
    blk = tk // per_tile

    def attend(s_ref, kj, far):
        k0 = pl.multiple_of(kj * tk, tk)
        row = bt_ref[n_cls - 1, 0:1, :] if far else None
        bias = None if far else bt_ref[jnp.minimum(qi - kj, n_cls - 1)]
        pieces = []
        for a in range(per_tile):
            rows = slice(a * blk, (a + 1) * blk)
            piece = s_ref[rows, :]
            add = row
            if pen_block:
                pen = pen_ref[pl.ds(kj * per_tile + a, 1), :]
                pen = pen if pen_per_head else jnp.concatenate([pen] * GROUP, axis=1)
                add = pen if row is None else pen + row
            if bias is not None:
                piece = piece + bias[rows]
            pieces.append(piece if add is None else piece + add)
        s = pieces[0] if per_tile == 1 else jnp.concatenate(pieces, axis=0)
        m_prev = m_ref[...]
        m_new = jnp.maximum(m_prev, jnp.max(s, axis=0, keepdims=True))
        p = jnp.exp2(s - m_new)
        acc_ref[...] = jnp.exp2(m_prev - m_new) * acc_ref[...] + _dot(vt_ref[:, pl.ds(k0, tk)], _bf(p))
        m_ref[...] = m_new

    k_lo = jnp.maximum(qi - k_back, 0) if k_back is not None else 0
    n_tiles = qi - k_lo + 1
    far_pairs = jnp.maximum(n_tiles - (n_cls - 1), 0) // 2 if k_back is None else 0
    sa_ref[...] = raw_logits(k_lo)

    def pair(pi, carry, far):
        ka = k_lo + 2 * pi
        sb_ref[...] = raw_logits(ka + 1)
        attend(sa_ref, ka, far)
        sa_ref[...] = raw_logits(ka + 2)
        attend(sb_ref, ka + 1, far)
        return carry

    lax.fori_loop(0, far_pairs, functools.partial(pair, far=True), 0)
    lax.fori_loop(far_pairs, n_tiles // 2, functools.partial(pair, far=False), 0)

    @pl.when(n_tiles % 2 == 1)
    def _():
        attend(sa_ref, qi, False)
    acc = acc_ref[...]
    o_t = acc[0:HEAD_DIM] / jnp.maximum(acc[HEAD_DIM:HEAD_DIM + 1], 1e-30)
    cols_g = _gate_columns(gt_ref, gate_col0 + g * GROUP) if gate_col0 is not None else None
    o_ref[...] = _heads_from_lanes(o_t, tq, cols_g)


def flash_attention(q_arr, kv_arr, k_col, v_col, bias, batch, seq, *, pen=None, pen_block=0,
                    k_back=None, gate_arr=None, gate_col0=None):
    tq = ATTN_TILE
    nq = seq // tq
    cols = GROUP * tq
    assert k_back is None or k_back == bias.shape[1] - 1
    in_specs = [pl.BlockSpec((tq, GROUP * HEAD_DIM), lambda b, g, i: (b * nq + i, g)),
                pl.BlockSpec((seq, HEAD_DIM), lambda b, g, i: (b, k_col + g)),
                pl.BlockSpec((seq, HEAD_DIM), lambda b, g, i: (b, v_col + g)),
                pl.BlockSpec((None,) + bias.shape[1:], lambda b, g, i: (g, 0, 0, 0))]
    args = [q_arr, kv_arr, kv_arr, bias]
    pen_per_head = False
    if pen is not None:
        if pen.ndim == 4:
            in_specs.append(pl.BlockSpec((None, None, pen.shape[2], tq), lambda b, g, i: (b, g, 0, i)))
        else:
            pen_per_head = True
            in_specs.append(pl.BlockSpec((None, None, None, pen.shape[3], cols), lambda b, g, i: (b, g, i, 0, 0)))
        args.append(pen)
    if gate_arr is not None:
        in_specs.append(pl.BlockSpec((tq, LANE), lambda b, g, i: (b * nq + i, 0)))
        args.append(gate_arr)
    return pl.pallas_call(
        functools.partial(_flash_body, pen_block=pen_block if pen is not None else 0, pen_per_head=pen_per_head,
                          k_back=k_back, gate_col0=gate_col0 if gate_arr is not None else None, seq=seq),
        grid=(batch, KV_HEADS, nq),
        in_specs=in_specs,
        out_specs=pl.BlockSpec((tq, GROUP * HEAD_DIM), lambda b, g, i: (b * nq + i, g)),
        out_shape=jax.ShapeDtypeStruct((batch * seq, Q_WIDTH), jnp.float32),
        scratch_shapes=[pltpu.VMEM((1, cols), jnp.float32),
                        pltpu.VMEM((HEAD_DIM + SUBLANE, cols), jnp.float32),
                        pltpu.VMEM((cols, HEAD_DIM), jnp.bfloat16),
                        pltpu.VMEM((seq, HEAD_DIM), jnp.bfloat16),
                        pltpu.VMEM((HEAD_DIM + SUBLANE, seq), jnp.bfloat16),
                        pltpu.VMEM((tq, cols), jnp.float32), pltpu.VMEM((tq, cols), jnp.float32)],
        compiler_params=_cparams("parallel", "parallel", "arbitrary"),
        name="flash_attention",
    )(*args)


def _moba_gate_body(q_ref, k_ref, pen_ref, km_ref, *, n_blocks):
    qi = pl.program_id(2)
    tq = ATTN_TILE
    cols = GROUP * tq

    @pl.when(qi == 0)
    def _():
        k = k_ref[...]
        km_ref[...] = jnp.sum(k.reshape(n_blocks, MOBA_BLOCK, HEAD_DIM), axis=1) / MOBA_BLOCK

    qh, ql = _split_bf16(_stack_heads(q_ref[...]), 2)
    kh, kl = _split_bf16(km_ref[...], 2)
    gate = _dot(kh, qh, NT_DIMS) + _dot(kh, ql, NT_DIMS) + _dot(kl, qh, NT_DIMS)
    blk = lax.broadcasted_iota(jnp.int32, (n_blocks, cols), 0)
    tok = qi * tq + (lax.broadcasted_iota(jnp.int32, (n_blocks, cols), 1) & (tq - 1))
    own = lax.shift_right_logical(tok, int(math.log2(MOBA_BLOCK)))
    gate = jnp.where(blk < own, gate, NEG_INF)
    chosen = ((_rank_rows(gate, n_blocks) < MOBA_TOPK) & (blk < own)) | (blk == own)
    pen_ref[...] = jnp.where(chosen, 0.0, NEG_INF)


def moba_gate(proj, k_col, batch, seq):
    tq = ATTN_TILE
    nq = seq // tq
    cols = GROUP * tq
    nb = seq // MOBA_BLOCK
    return pl.pallas_call(
        functools.partial(_moba_gate_body, n_blocks=nb),
        grid=(batch, KV_HEADS, nq),
        in_specs=[pl.BlockSpec((tq, GROUP * HEAD_DIM), lambda b, g, i: (b * nq + i, g)),
                  pl.BlockSpec((seq, HEAD_DIM), lambda b, g, i: (b, k_col + g))],
        out_specs=pl.BlockSpec((None, None, None, nb, cols), lambda b, g, i: (b, g, i, 0, 0)),
        out_shape=jax.ShapeDtypeStruct((batch, KV_HEADS, nq, nb, cols), jnp.float32),
        scratch_shapes=[pltpu.VMEM((nb, HEAD_DIM), jnp.float32)],
        compiler_params=_cparams("parallel", "parallel", "arbitrary"),
        name="moba_gate",
    )(proj, proj)


def _tril_ones(n, strict=False):
    r = lax.broadcasted_iota(jnp.int32, (n, n), 0)
    c = lax.broadcasted_iota(jnp.int32, (n, n), 1)
    return (r > c) if strict else (r >= c)


def _chunk_cumsum(x):
    tril = _bf(_tril_ones(x.shape[0]))
    return sum(_dot(tril, part) for part in _split_bf16(x, 3))


def _head_rms_gate(o, norm_g, gate):
    ms = jnp.mean(o * o, axis=-1, keepdims=True)
    return o * lax.rsqrt(ms + NORM_EPS) * norm_g * (gate * jax.nn.sigmoid(gate))


def _hgrn2_body(q_ref, f_ref, i_ref, g_ref, lbl_ref, ng_ref, s0_ref, o_ref, s_out_ref, st_ref, *, layer):
    ti = pl.program_id(2)

    @pl.when(ti == 0)
    def _():
        for hh in range(HG_HPS):
            st_ref[hh] = s0_ref[hh].T

    lbl = lbl_ref[...]
    e = jnp.exp(lbl - jnp.max(lbl, axis=0, keepdims=True))
    p = e / jnp.sum(e, axis=0, keepdims=True)
    lb_all = jnp.zeros((1, HG_HPS * HG_DK), jnp.float32)
    for r in range(1, layer + 1):
        lb_all = lb_all + p[r:r + 1]
    causal = _tril_ones(CHUNK)
    work = []
    for c in range(q_ref.shape[0] // CHUNK):
        sl = slice(c * CHUNK, (c + 1) * CHUNK)
        for hh in range(HG_HPS):
            cs = slice(hh * HG_DK, (hh + 1) * HG_DK)
            lb = lb_all[:, cs]
            q = q_ref[sl, cs]
            qh = q * jax.nn.sigmoid(q) * HG_DK ** -0.5
            fg = lb + (1.0 - lb) * jax.nn.sigmoid(f_ref[sl, cs])
            k = 1.0 - fg
            v = _bf(i_ref[sl, cs])
            b = _chunk_cumsum(jnp.log(fg))
            b_mid = b[CHUNK // 2:CHUNK // 2 + 1]
            b_last = b[CHUNK - 1:CHUNK]
            a = _dot(_bf(qh * jnp.exp(b - b_mid)), _bf(k * jnp.exp(b_mid - b)), NT_DIMS)
            a = jnp.where(causal, a, 0.0)
            work.append((sl, hh, cs, _dot(_bf(a), v), _bf(qh * jnp.exp(b)), jnp.exp(b_last),
                         _dot(v, _bf(k * jnp.exp(b_last - b)), TN_DIMS)))
    for sl, hh, cs, o_intra, q_in, d_last, kv in work:
        st = st_ref[hh]
        o = o_intra + _dot(q_in, _bf(st), NT_DIMS)
        st_ref[hh] = st * d_last + kv
        o_ref[sl, cs] = _head_rms_gate(o, ng_ref[...], g_ref[sl, cs])

    @pl.when(ti == pl.num_programs(2) - 1)
    def _():
        for hh in range(HG_HPS):
            s_out_ref[hh] = st_ref[hh].T


def hgrn2_scan(proj, lb_logits, norm_g, s0, layer, batch, seq):
    tt = _row_tile(seq, SCAN_TILE)
    nt = seq // tt
    h = HG_HEADS // HG_HPS
    width = HG_HPS * HG_DK

    def col(k):
        return pl.BlockSpec((tt, width), lambda b, hh, t: (b * nt + t, k * h + hh))

    return pl.pallas_call(
        functools.partial(_hgrn2_body, layer=layer),
        grid=(batch, h, nt),
        in_specs=[col(0), col(1), col(2), col(3),
                  pl.BlockSpec((DEPTH, width), lambda b, hh, t: (0, hh)),
                  pl.BlockSpec((1, HG_DV), lambda b, hh, t: (0, 0)),
                  pl.BlockSpec((None, HG_HPS, HG_DK, HG_DV), lambda b, hh, t: (b, hh, 0, 0))],
        out_specs=[pl.BlockSpec((tt, width), lambda b, hh, t: (b * nt + t, hh)),
                   pl.BlockSpec((None, HG_HPS, HG_DK, HG_DV), lambda b, hh, t: (b, hh, 0, 0))],
        out_shape=[jax.ShapeDtypeStruct((batch * seq, HG_HEADS * HG_DV), jnp.float32),
                   jax.ShapeDtypeStruct((batch, HG_HEADS, HG_DK, HG_DV), jnp.float32)],
        scratch_shapes=[pltpu.VMEM((HG_HPS, HG_DV, HG_DK), jnp.float32)],
        compiler_params=_cparams("parallel", "parallel", "arbitrary"),
        name="hgrn2_scan",
    )(proj, proj, proj, proj, lb_logits, norm_g.reshape(1, HG_DV), s0)


def _lane_column(x, lane_idx):
    lane = lax.broadcasted_iota(jnp.int32, x.shape, 1)
    return jnp.sum(jnp.where(lane == lane_idx, x, 0.0), axis=1, keepdims=True)


def _softplus(x):
    return jnp.maximum(x, 0.0) + jnp.log(1.0 + jnp.exp(-jnp.abs(x)))


def _l2n(x):
    return x * lax.rsqrt(jnp.sum(x * x, axis=-1, keepdims=True) + NORM_EPS)


def _gdn_body(q_ref, k_ref, v_ref, z_ref, t_ref, bq_ref, bk_ref, bv_ref, wq_ref, wk_ref, wv_ref,
              al_ref, dtb_ref, ng_ref, s0_ref, o_ref, s_out_ref, xs_ref, y_ref, s_ref):
    hq0 = pl.program_id(1) * GDN_HPS
    n_v = GDN_HPS * GDN_REP
    ti = pl.program_id(2)
    tt = q_ref.shape[0]
    dk, dv = GDN_DK, GDN_DV
    pad = SUBLANE

    @pl.when(ti == 0)
    def _():
        s_ref[...] = s0_ref[...]
        xs_ref[0:pad, :] = jnp.concatenate([bq_ref[...], bk_ref[...], bv_ref[...]], axis=1)

    x = jnp.concatenate([q_ref[...], k_ref[...], v_ref[...]], axis=1)
    xs_ref[pad:, :] = x
    cw = jnp.concatenate([wq_ref[...], wk_ref[...], wv_ref[...]], axis=1)
    y = xs_ref[pad - 3:pad - 3 + tt, :] * cw[0:1]
    for i in range(1, GDN_CONV - 1):
        y = y + xs_ref[pad - 3 + i:pad - 3 + i + tt, :] * cw[i:i + 1]
    y = y + x * cw[GDN_CONV - 1:GDN_CONV]
    xs_ref[0:pad, :] = x[tt - pad:tt]
    y_ref[...] = y * jax.nn.sigmoid(y)

    strict = _tril_ones(CHUNK, strict=True)
    incl = _tril_ones(CHUNK)
    sel_rows = lax.shift_right_logical(lax.broadcasted_iota(jnp.int32, (n_v * CHUNK, LANE), 0), int(math.log2(CHUNK)))
    sel_lane = lax.broadcasted_iota(jnp.int32, (n_v * CHUNK, LANE), 1)
    pick = _bf(sel_lane == GDN_V_HEADS + hq0 * GDN_REP + sel_rows)
    n_chunks = tt // CHUNK

    work = []
    for c in range(n_chunks):
        sl = slice(c * CHUNK, (c + 1) * CHUNK)
        yc = y_ref[sl, :]
        tl = t_ref[sl, :]
        beta_all = jax.nn.sigmoid(tl)
        g_all = _chunk_cumsum(-jnp.exp(al_ref[...]) * _softplus(tl + dtb_ref[...]))
        g_rows = sum(_dot(pick, part, NT_DIMS) for part in _split_bf16(g_all, 3))
        for hh in range(GDN_HPS):
            q = _l2n(yc[:, hh * dk:(hh + 1) * dk]) * dk ** -0.5
            k = _l2n(yc[:, (GDN_HPS + hh) * dk:(GDN_HPS + hh + 1) * dk])
            qb, kb = _bf(q), _bf(k)
            kk = _dot(kb, kb, NT_DIMS)
            qk = _dot(qb, kb, NT_DIMS)
            for e in range(hh * GDN_REP, (hh + 1) * GDN_REP):
                hv = hq0 * GDN_REP + e
                v = yc[:, 2 * GDN_HPS * dk + e * dv:2 * GDN_HPS * dk + (e + 1) * dv]
                bt = _lane_column(beta_all, hv)
                gc = _lane_column(g_all, GDN_V_HEADS + hv)
                gdiff = gc - g_rows[e * CHUNK:(e + 1) * CHUNK]
                decay = jnp.exp(jnp.where(incl, gdiff, 0.0))
                d_strict = jnp.where(strict, decay, 0.0)
                d_incl = jnp.where(incl, decay, 0.0)
                eg = jnp.exp(gc)
                g_last = gc[CHUNK - 1:CHUNK]
                work.append(dict(
                    c=c, e=e, sol=jnp.concatenate([bt * v, (bt * eg) * k], axis=1), pw=bt * kk * d_strict,
                    aq=_bf(qk * d_incl), q_in=_bf(q * eg), k_out=_bf(k * jnp.exp(g_last - gc)),
                    d_last=jnp.exp(g_last)))

    r_i = lax.broadcasted_iota(jnp.int32, (CHUNK, CHUNK), 0)
    c_i = lax.broadcasted_iota(jnp.int32, (CHUNK, CHUNK), 1)
    same = [lax.shift_right_logical(r_i, sh) == lax.shift_right_logical(c_i, sh) for sh in range(3, 7)]
    eye = (r_i == c_i).astype(jnp.float32)
    for wk in work:
        l8 = jnp.where(same[0], wk["pw"], 0.0)
        l8b = _bf(l8)
        wk["t"] = eye - l8
        wk["p"] = _dot(l8b, l8b)
    for wk in work:
        pb = _bf(wk["p"])
        wk["t"] = wk["t"] + _dot(_bf(wk["t"]), pb)
        wk["p"] = _dot(pb, pb)
    for wk in work:
        wk["t"] = wk["t"] + _dot(_bf(wk["t"]), _bf(wk["p"]))
    for lvl in range(1, len(same)):
        for wk in work:
            tb = _bf(wk["t"])
            off = _bf(jnp.where(same[lvl] & jnp.logical_not(same[lvl - 1]), wk["pw"], 0.0))
            wk["t"] = wk["t"] - _dot(tb, _bf(_dot(off, tb)))
    for wk in work:
        wk["sol"] = _dot(_bf(wk["t"]), _bf(wk["sol"]))

    for wk in work:
        c, e = wk["c"], wk["e"]
        sl = slice(c * CHUNK, (c + 1) * CHUNK)
        u0, w = wk["sol"][:, :dv], wk["sol"][:, dv:]
        s = s_ref[e]
        sb = _bf(s)
        u = u0 - _dot(_bf(w), sb)
        o = _dot(wk["q_in"], sb) + _dot(wk["aq"], _bf(u))
        s_ref[e] = wk["d_last"] * s + _dot(wk["k_out"], _bf(u), TN_DIMS)
        o_ref[sl, e * dv:(e + 1) * dv] = _head_rms_gate(o, ng_ref[...], z_ref[sl, e * dv:(e + 1) * dv])

    @pl.when(ti == pl.num_programs(2) - 1)
    def _():
        s_out_ref[...] = s_ref[...]


def gdn_scan(main, tail, conv_buf, conv_w, a_log, dt_bias, norm_g, s0, batch, seq):
    tt = _row_tile(seq, SCAN_TILE)
    nt = seq // tt
    hq, rep = GDN_QK_HEADS // GDN_HPS, GDN_HPS * GDN_REP
    dk, dv = GDN_HPS * GDN_DK, GDN_DV
    vw = rep * dv
    buf = jnp.pad(conv_buf, ((0, 0), (SUBLANE - (GDN_CONV - 1), 0), (0, 0)))
    pad_lanes = jnp.zeros((LANE - 2 * GDN_V_HEADS,), jnp.float32)
    a_row = jnp.concatenate([jnp.zeros((GDN_V_HEADS,), jnp.float32), a_log, pad_lanes]).reshape(1, LANE)
    dt_row = jnp.concatenate([jnp.zeros((GDN_V_HEADS,), jnp.float32), dt_bias, pad_lanes]).reshape(1, LANE)
    k0 = hq
    v0 = 2 * hq * dk // vw
    z0 = GDN_CONV_DIM // vw
    row = lambda b, h, t: b * nt + t
    return pl.pallas_call(
        _gdn_body,
        grid=(batch, hq, nt),
        in_specs=[pl.BlockSpec((tt, dk), lambda b, h, t: (row(b, h, t), h)),
                  pl.BlockSpec((tt, dk), lambda b, h, t: (row(b, h, t), k0 + h)),
                  pl.BlockSpec((tt, vw), lambda b, h, t: (row(b, h, t), v0 + h)),
                  pl.BlockSpec((tt, vw), lambda b, h, t: (row(b, h, t), z0 + h)),
                  pl.BlockSpec((tt, LANE), lambda b, h, t: (row(b, h, t), 0)),
                  pl.BlockSpec((None, SUBLANE, dk), lambda b, h, t: (b, 0, h)),
                  pl.BlockSpec((None, SUBLANE, dk), lambda b, h, t: (b, 0, k0 + h)),
                  pl.BlockSpec((None, SUBLANE, vw), lambda b, h, t: (b, 0, v0 + h)),
                  pl.BlockSpec((GDN_CONV, dk), lambda b, h, t: (0, h)),
                  pl.BlockSpec((GDN_CONV, dk), lambda b, h, t: (0, k0 + h)),
                  pl.BlockSpec((GDN_CONV, vw), lambda b, h, t: (0, v0 + h)),
                  pl.BlockSpec((1, LANE), lambda b, h, t: (0, 0)),
                  pl.BlockSpec((1, LANE), lambda b, h, t: (0, 0)),
                  pl.BlockSpec((1, dv), lambda b, h, t: (0, 0)),
                  pl.BlockSpec((None, rep, GDN_DK, dv), lambda b, h, t: (b, h, 0, 0))],
        out_specs=[pl.BlockSpec((tt, vw), lambda b, h, t: (row(b, h, t), h)),
                   pl.BlockSpec((None, rep, GDN_DK, dv), lambda b, h, t: (b, h, 0, 0))],
        out_shape=[jax.ShapeDtypeStruct((batch * seq, GDN_V_HEADS * dv), jnp.float32),
                   jax.ShapeDtypeStruct((batch, GDN_V_HEADS, GDN_DK, dv), jnp.float32)],
        scratch_shapes=[pltpu.VMEM((tt + SUBLANE, 2 * dk + vw), jnp.float32),
                        pltpu.VMEM((tt, 2 * dk + vw), jnp.float32),
                        pltpu.VMEM((rep, GDN_DK, dv), jnp.float32)],
        compiler_params=_cparams("parallel", "parallel", "arbitrary"),
        name="gdn_scan",
    )(main, main, main, main, tail, buf, buf, buf, conv_w, conv_w, conv_w, a_row, dt_row,
      norm_g.reshape(1, dv), s0)


PAGE = 128
KV_SLABS = 2 * KV_HEADS
PAGE_ROWS = PAGE * KV_SLABS
CMP_PAGES = 8
CHUNKS_PER_PAGE = PAGE // CMP_STRIDE
ROWS = GROUP * 4


def _page_slab(pg, slab):
    return pg[pl.ds(slab, PAGE, stride=KV_SLABS), :]


def _compress_part_body(*refs, n_prefetch, paged):
    refs = refs[n_prefetch:]
    pages, w_ref, o_ref, xs_ref = refs[:CMP_PAGES], refs[CMP_PAGES], refs[CMP_PAGES + 1], refs[CMP_PAGES + 2]
    per_g = CMP_PAGES * CHUNKS_PER_PAGE
    for k, pg in enumerate(pages):
        for cg in range(KV_SLABS):
            xs_ref[k, cg] = _page_slab(pg, cg) if paged else pg[:, cg * HEAD_DIM:(cg + 1) * HEAD_DIM]
    def token_rows(c, l):
        return jnp.concatenate([xs_ref.at[k, c * KV_HEADS + g][pl.ds(l, CHUNKS_PER_PAGE, stride=CMP_STRIDE), :]
                                for g in range(KV_HEADS) for k in range(CMP_PAGES)], axis=0)

    for c in range(2):
        acc = None
        for lp in range(CMP_STRIDE // 2):
            x = jnp.concatenate([token_rows(c, 2 * lp), token_rows(c, 2 * lp + 1)], axis=1)
            d = _dot(_bf(x), w_ref[c, lp])
            acc = d if acc is None else acc + d
        for g in range(KV_HEADS):
            o_ref[c, g] = acc[g * per_g:(g + 1) * per_g]


def _compress_weights(cmp_w1):
    n_part = CMP_BLOCK // CMP_STRIDE
    w = cmp_w1.reshape(2, n_part, CMP_STRIDE, HEAD_DIM, CMP_HIDDEN).transpose(0, 2, 3, 1, 4)
    return w.reshape(2, CMP_STRIDE // 2, 2 * HEAD_DIM, n_part * CMP_HIDDEN).astype(jnp.bfloat16)


def compress_part_rows(kv_arr, col_block, cmp_w1, batch, seq):
    w = _compress_weights(cmp_w1)
    pages_per_b = seq // PAGE
    steps = pages_per_b // CMP_PAGES
    per_g = CMP_PAGES * CHUNKS_PER_PAGE

    def page_spec(k):
        return pl.BlockSpec((PAGE, KV_WIDTH), lambda b, s: (b * pages_per_b + s * CMP_PAGES + k, col_block))

    return pl.pallas_call(
        functools.partial(_compress_part_body, n_prefetch=0, paged=False),
        grid=(batch, steps),
        in_specs=[page_spec(k) for k in range(CMP_PAGES)] + [pl.BlockSpec(w.shape, lambda b, s: (0, 0, 0, 0))],
        out_specs=pl.BlockSpec((None, 2, KV_HEADS, per_g, w.shape[-1]), lambda b, s: (b, 0, 0, s, 0)),
        out_shape=jax.ShapeDtypeStruct((batch, 2, KV_HEADS, seq // CMP_STRIDE, w.shape[-1]), jnp.float32),
        scratch_shapes=[pltpu.VMEM((CMP_PAGES, 2 * KV_HEADS, PAGE, HEAD_DIM), jnp.float32)],
        compiler_params=_cparams("parallel", "arbitrary"),
        name="compress_part_rows",
    )(*([kv_arr] * CMP_PAGES), w)


def compress_part_paged(pool, page_table, cmp_w1):
    w = _compress_weights(cmp_w1)
    batch, n_pages = page_table.shape
    steps = n_pages // CMP_PAGES
    per_g = CMP_PAGES * CHUNKS_PER_PAGE

    def page_spec(k):
        return pl.BlockSpec((None, PAGE_ROWS, HEAD_DIM), lambda b, s, pt: (pt[b, s * CMP_PAGES + k], 0, 0))

    return pl.pallas_call(
        functools.partial(_compress_part_body, n_prefetch=1, paged=True),
        grid_spec=pltpu.PrefetchScalarGridSpec(
            num_scalar_prefetch=1, grid=(batch, steps),
            in_specs=[page_spec(k) for k in range(CMP_PAGES)]
            + [pl.BlockSpec(w.shape, lambda b, s, pt: (0, 0, 0, 0))],
            out_specs=pl.BlockSpec((None, 2, KV_HEADS, per_g, w.shape[-1]), lambda b, s, pt: (b, 0, 0, s, 0)),
            scratch_shapes=[pltpu.VMEM((CMP_PAGES, 2 * KV_HEADS, PAGE, HEAD_DIM), jnp.float32)]),
        out_shape=jax.ShapeDtypeStruct((batch, 2, KV_HEADS, n_pages * CHUNKS_PER_PAGE, w.shape[-1]), jnp.float32),
        compiler_params=_cparams("parallel", "arbitrary"),
        name="compress_part_paged",
    )(page_table, *([pool] * CMP_PAGES), w)


def _gelu_tanh(x):
    return x * (0.5 * (1.0 + jnp.tanh(math.sqrt(2.0 / math.pi) * (x + 0.044715 * (x * x * x)))))


def _compress_finish_body(p_ref, peh_ref, w2_ref, o_ref):
    n = p_ref.shape[1]
    for c in range(2):
        p = p_ref[c]
        hid = peh_ref[c:c + 1, :] + p[:, :CMP_HIDDEN]
        hid = hid + pltpu.roll(p[:, CMP_HIDDEN:], n - 1, 0)
        o_ref[c] = _dot(_bf(_gelu_tanh(hid)), _bf(w2_ref[c]))


def compress_finish(part, cmp_w1, cmp_w2, cmp_pe):
    batch, _, _, n, width = part.shape
    pe_hid = jnp.einsum('cld,cldh->ch', cmp_pe, cmp_w1)
    return pl.pallas_call(
        _compress_finish_body,
        grid=(batch, KV_HEADS),
        in_specs=[pl.BlockSpec((None, 2, None, n, width), lambda b, g: (b, 0, g, 0, 0)),
                  pl.BlockSpec((2, CMP_HIDDEN), lambda b, g: (0, 0)),
                  pl.BlockSpec((2, CMP_HIDDEN, HEAD_DIM), lambda b, g: (0, 0, 0))],
        out_specs=pl.BlockSpec((None, None, 2, n, HEAD_DIM), lambda b, g: (b, g, 0, 0, 0)),
        out_shape=jax.ShapeDtypeStruct((batch, KV_HEADS, 2, n, HEAD_DIM), jnp.float32),
        compiler_params=_cparams("parallel", "parallel"),
        name="compress_finish",
    )(part, pe_hid, cmp_w2)


def _rows_to_col(row, n):
    eye = lax.broadcasted_iota(jnp.int32, (n, n), 0) == lax.broadcasted_iota(jnp.int32, (n, n), 1)
    return jnp.sum(jnp.where(eye, jnp.broadcast_to(row, (n, n)), 0.0), axis=1, keepdims=True)


def _sample_bias_tiles(rel_table, past, n_new):
    j = np.arange(PAGE)[:, None]
    t = np.arange(n_new)[None, :]
    far = np.full((PAGE, n_new), REL_MAX_DIST)
    first = WINDOW + t - j
    last = PAGE + t - j
    new = t - j
    tiles = []
    for dist, ok in ((far, far > 0), (first, first < WINDOW), (last, last > 0), (new, (new >= 0) & (j < n_new))):
        b = jnp.where(ok[..., None], _bias_lookup(rel_table, dist), NEG_INF)
        b = b.reshape(PAGE, n_new, KV_HEADS, GROUP).transpose(2, 0, 3, 1).reshape(KV_HEADS, PAGE, GROUP * n_new)
        tiles.append(b)
    return jnp.stack(tiles, axis=1)


def _sample_cmp_body(q_ref, kc_ref, bias_ref, gt_ref, o_ref, pen_ref, *, n_sel_blocks, past):
    n = kc_ref.shape[1]
    nbp = pen_ref.shape[0]
    q = _bf(q_ref[...] * ATTN_SCALE)
    s = _dot(_bf(kc_ref[0]), q, NT_DIMS) + bias_ref[...]
    m = jnp.max(s, axis=0, keepdims=True)
    e = jnp.exp(s - m)
    p = e / jnp.maximum(jnp.sum(e, axis=0, keepdims=True), 1e-30)
    o = _dot(_bf(p), _bf(kc_ref[1]), TN_DIMS)
    o_ref[...] = o * jax.nn.sigmoid(gt_ref[...])

    r_i = lax.broadcasted_iota(jnp.int32, (ROWS, ROWS), 0)
    c_i = lax.broadcasted_iota(jnp.int32, (ROWS, ROWS), 1)
    n_tok = ROWS // GROUP
    same_tok = _bf((r_i & (n_tok - 1)) == (c_i & (n_tok - 1)))
    ratio = SEL_BLOCK // CMP_STRIDE
    j_i = lax.broadcasted_iota(jnp.int32, (nbp, n), 0)
    k_i = lax.broadcasted_iota(jnp.int32, (nbp, n), 1)
    w = _bf((k_i >= ratio * j_i - 1) & (k_i <= ratio * j_i + ratio - 1) & (j_i < n_sel_blocks))
    imp = sum(_dot(part, same_tok) for part in _split_bf16(p, 3))
    score = sum(_dot(w, part) for part in _split_bf16(imp, 3))
    blk = lax.broadcasted_iota(jnp.int32, (nbp, ROWS), 0)
    tok = past + (lax.broadcasted_iota(jnp.int32, (nbp, ROWS), 1) & (n_tok - 1))
    cur = lax.shift_right_logical(tok, int(math.log2(SEL_BLOCK)))
    forced = (blk == 0) | (blk == cur) | (blk == cur - 1)
    causal = blk <= cur
    score = jnp.where(forced, FORCE_SCORE, score)
    score = jnp.where(causal, score, -1.0)
    nbl = -(-nbp // LANE) * LANE
    n_idx = lax.broadcasted_iota(jnp.int32, (nbp, nbl), 0)
    m_idx = lax.broadcasted_iota(jnp.int32, (nbp, nbl), 1)
    lane_tok = lax.broadcasted_iota(jnp.int32, (nbp, ROWS), 1) & (n_tok - 1)
    rank = jnp.zeros((nbp, ROWS), jnp.float32)
    for t in range(n_tok):
        s_col = score[:, t:t + 1]
        s_row = jnp.sum(jnp.where(n_idx == m_idx, s_col, 0.0), axis=0, keepdims=True)
        s_row = jnp.where(m_idx[0:1] < n_sel_blocks, s_row, -2.0)
        ahead = (s_row > s_col) | ((s_row == s_col) & (m_idx < n_idx))
        rank_t = jnp.sum(ahead.astype(jnp.float32), axis=1, keepdims=True)
        rank = jnp.where(lane_tok == t, rank_t, rank)
    chosen = (rank < N_SEL) & causal & (blk < n_sel_blocks)
    pen_ref[...] = jnp.where(chosen, 0.0, NEG_INF)


def sample_cmp_select(qs, kc, bias, gate, past, n_sel_blocks):
    batch = qs.shape[0]
    n = kc.shape[3]
    nbp = -(-n_sel_blocks // SUBLANE) * SUBLANE
    return pl.pallas_call(
        functools.partial(_sample_cmp_body, n_sel_blocks=n_sel_blocks, past=past),
        grid=(batch, KV_HEADS),
        in_specs=[pl.BlockSpec((None, None, ROWS, HEAD_DIM), lambda b, g: (b, g, 0, 0)),
                  pl.BlockSpec((None, None, 2, n, HEAD_DIM), lambda b, g: (b, g, 0, 0, 0)),
                  pl.BlockSpec((None, n, ROWS), lambda b, g: (g, 0, 0)),
                  pl.BlockSpec((None, None, ROWS, HEAD_DIM), lambda b, g: (b, g, 0, 0))],
        out_specs=[pl.BlockSpec((None, None, ROWS, HEAD_DIM), lambda b, g: (b, g, 0, 0)),
                   pl.BlockSpec((None, None, nbp, ROWS), lambda b, g: (b, g, 0, 0))],
        out_shape=[jax.ShapeDtypeStruct((batch, KV_HEADS, ROWS, HEAD_DIM), jnp.float32),
                   jax.ShapeDtypeStruct((batch, KV_HEADS, nbp, ROWS), jnp.float32)],
        compiler_params=_cparams("parallel", "parallel"),
        name="sample_cmp_select",
    )(qs, kc, bias, gate)


ATTN_PAGES = 4


def _paged_attn_body(*refs, pen_block, gated, n_pages):
    it = iter(refs)
    pt_ref, tid_ref, q_ref = next(it), next(it), next(it)
    pages = [next(it) for _ in range(ATTN_PAGES)]
    new_ref, bt_ref = next(it), next(it)
    pen_ref = next(it) if pen_block else None
    gt_ref = next(it) if gated else None
    o_ref, m_ref, l_ref, acc_ref = next(it), next(it), next(it), next(it)
    step = pl.program_id(1)

    @pl.when(step == 0)
    def _():
        m_ref[...] = jnp.full(m_ref.shape, NEG_INF, jnp.float32)
        l_ref[...] = jnp.zeros(l_ref.shape, jnp.float32)
        acc_ref[...] = jnp.zeros(acc_ref.shape, jnp.float32)

    def page_terms(g, page, n_keys):
        tile = tid_ref[page]
        b = bt_ref[g, tile] if n_keys == PAGE else bt_ref[g, tile, 0:n_keys, :]
        if not pen_block:
            return b
        if pen_block >= PAGE:
            return b + pen_ref[g, pl.ds(page // (pen_block // PAGE), 1), :]
        per_page = PAGE // pen_block
        pieces = [b[a * pen_block:min((a + 1) * pen_block, n_keys)] + pen_ref[g, pl.ds(page * per_page + a, 1), :]
                  for a in range(-(-n_keys // pen_block))]
        return pieces[0] if len(pieces) == 1 else jnp.concatenate(pieces, axis=0)

    def attend(g, k, v, terms):
        s = _dot(_bf(k), _bf(q_ref[g] * ATTN_SCALE), NT_DIMS) + terms
        m_prev = m_ref[g]
        m_new = jnp.maximum(m_prev, jnp.max(s, axis=0, keepdims=True))
        alpha = jnp.exp(m_prev - m_new)
        p = jnp.exp(s - m_new)
        l_ref[g] = alpha * l_ref[g] + jnp.sum(p, axis=0, keepdims=True)
        acc_ref[g] = _rows_to_col(alpha, ROWS) * acc_ref[g] + _dot(_bf(p), _bf(v), TN_DIMS)
        m_ref[g] = m_new

    half = KV_HEADS * HEAD_DIM
    for kk in range(ATTN_PAGES):
        for g in range(KV_HEADS):
            attend(g, _page_slab(pages[kk], g), _page_slab(pages[kk], KV_HEADS + g),
                   page_terms(g, step * ATTN_PAGES + kk, PAGE))

    @pl.when(step == pl.num_programs(1) - 1)
    def _():
        n_new = new_ref.shape[0]
        for g in range(KV_HEADS):
            attend(g, new_ref[:, g * HEAD_DIM:(g + 1) * HEAD_DIM],
                   new_ref[:, half + g * HEAD_DIM:half + (g + 1) * HEAD_DIM], page_terms(g, n_pages, n_new))
            o = acc_ref[g] / _rows_to_col(jnp.maximum(l_ref[g], 1e-30), ROWS)
            if gated:
                o = o * jax.nn.sigmoid(gt_ref[g])
            o_ref[g] = o


def paged_attention(qs, pool, page_table, tile_ids, new_kv, bias_tiles, *, pen=None, pen_block=0, gate=None):
    batch, n_pages = page_table.shape
    steps = n_pages // ATTN_PAGES
    n_new = new_kv.shape[1]

    def page_spec(k):
        return pl.BlockSpec((None, PAGE_ROWS, HEAD_DIM), lambda b, s, pt, tid: (pt[b, s * ATTN_PAGES + k], 0, 0))

    grp = lambda b, s, pt, tid: (b, 0, 0, 0)
    in_specs = ([pl.BlockSpec((None, KV_HEADS, ROWS, HEAD_DIM), grp)] + [page_spec(k) for k in range(ATTN_PAGES)]
                + [pl.BlockSpec((None, n_new, KV_WIDTH), lambda b, s, pt, tid: (b, 0, 0)),
                   pl.BlockSpec(bias_tiles.shape, lambda b, s, pt, tid: (0, 0, 0, 0))])
    args = [qs] + [pool] * ATTN_PAGES + [new_kv, bias_tiles]
    if pen is not None:
        in_specs.append(pl.BlockSpec((None,) + pen.shape[1:], grp))
        args.append(pen)
    if gate is not None:
        in_specs.append(pl.BlockSpec((None, KV_HEADS, ROWS, HEAD_DIM), grp))
        args.append(gate)
    return pl.pallas_call(
        functools.partial(_paged_attn_body, pen_block=pen_block if pen is not None else 0, gated=gate is not None,
                          n_pages=n_pages),
        grid_spec=pltpu.PrefetchScalarGridSpec(
            num_scalar_prefetch=2, grid=(batch, steps), in_specs=in_specs,
            out_specs=pl.BlockSpec((None, KV_HEADS, ROWS, HEAD_DIM), grp),
            scratch_shapes=[pltpu.VMEM((KV_HEADS, 1, ROWS), jnp.float32), pltpu.VMEM((KV_HEADS, 1, ROWS), jnp.float32),
                            pltpu.VMEM((KV_HEADS, ROWS, HEAD_DIM), jnp.float32)]),
        out_shape=jax.ShapeDtypeStruct((batch, KV_HEADS, ROWS, HEAD_DIM), jnp.float32),
        compiler_params=_cparams("parallel", "arbitrary"),
        name="paged_attention",
    )(page_table, tile_ids, *args)


def _moba_sample_gate_body(pt_ref, q_ref, *refs, n_blocks, past):
    pages, (pen_ref, km_ref) = refs[:ATTN_PAGES], refs[ATTN_PAGES:]
    step = pl.program_id(1)
    per_block = MOBA_BLOCK // PAGE

    @pl.when(step == 0)
    def _():
        km_ref[...] = jnp.zeros(km_ref.shape, jnp.float32)

    for kk in range(ATTN_PAGES):
        blk = (step * ATTN_PAGES + kk) // per_block
        slab_sums = jnp.sum(pages[kk][...], axis=0)
        for g in range(KV_HEADS):
            km_ref[g, pl.ds(blk, 1), :] += slab_sums[g:g + 1]

    @pl.when(step == pl.num_programs(1) - 1)
    def _():
        nbp = pen_ref.shape[1]
        blk = lax.broadcasted_iota(jnp.int32, (nbp, ROWS), 0)
        n_tok = ROWS // GROUP
        tok = past + (lax.broadcasted_iota(jnp.int32, (nbp, ROWS), 1) & (n_tok - 1))
        own = lax.shift_right_logical(tok, int(math.log2(MOBA_BLOCK)))
        for g in range(KV_HEADS):
            kh, kl = _split_bf16(km_ref[g] / MOBA_BLOCK, 2)
            qh, ql = _split_bf16(q_ref[g], 2)
            gate = _dot(kh, qh, NT_DIMS) + _dot(kh, ql, NT_DIMS) + _dot(kl, qh, NT_DIMS)
            gate = jnp.where(blk < own, gate, NEG_INF)
            chosen = ((_rank_rows(gate, n_blocks) < MOBA_TOPK) & (blk < own)) | (blk == own)
            pen_ref[g] = jnp.where(chosen, 0.0, NEG_INF)


def moba_sample_gate(qs, pool, page_table, past, n_new):
    batch, n_pages = page_table.shape
    steps = n_pages // ATTN_PAGES
    n_blocks = -(-(past + n_new) // MOBA_BLOCK)
    nbp = -(-n_blocks // SUBLANE) * SUBLANE

    def page_spec(k):
        return pl.BlockSpec((None, PAGE, None, KV_HEADS, HEAD_DIM),
                            lambda b, s, pt: (pt[b, s * ATTN_PAGES + k], 0, 0, 0, 0))

    return pl.pallas_call(
        functools.partial(_moba_sample_gate_body, n_blocks=n_blocks, past=past),
        grid_spec=pltpu.PrefetchScalarGridSpec(
            num_scalar_prefetch=1, grid=(batch, steps),
            in_specs=[pl.BlockSpec((None, KV_HEADS, ROWS, HEAD_DIM), lambda b, s, pt: (b, 0, 0, 0))]
            + [page_spec(k) for k in range(ATTN_PAGES)],
            out_specs=pl.BlockSpec((None, KV_HEADS, nbp, ROWS), lambda b, s, pt: (b, 0, 0, 0)),
            scratch_shapes=[pltpu.VMEM((KV_HEADS, nbp, HEAD_DIM), jnp.float32)]),
        out_shape=jax.ShapeDtypeStruct((batch, KV_HEADS, nbp, ROWS), jnp.float32),
        compiler_params=_cparams("parallel", "arbitrary"),
        name="moba_sample_gate",
    )(page_table, qs, *([pool] * ATTN_PAGES))


def _sample_rows(x, batch, n_tok):
    return x.reshape(batch, n_tok, KV_HEADS, GROUP, HEAD_DIM).transpose(0, 2, 3, 1, 4).reshape(
        batch, KV_HEADS, GROUP * n_tok, HEAD_DIM)


def _sample_unrows(o, batch, n_tok):
    return o.reshape(batch, KV_HEADS, GROUP, n_tok, HEAD_DIM).transpose(0, 3, 1, 2, 4).reshape(batch * n_tok, Q_WIDTH)


def _sample_gate_rows(tail, branch, batch, n_tok):
    gt = tail[:, branch * N_HEADS:(branch + 1) * N_HEADS].reshape(batch, n_tok, KV_HEADS, GROUP)
    gt = gt.transpose(0, 2, 3, 1).reshape(batch, KV_HEADS, GROUP * n_tok, 1)
    return jnp.broadcast_to(gt, (batch, KV_HEADS, GROUP * n_tok, HEAD_DIM))


def _pad_new(kv_new, batch, n_tok):
    return jnp.pad(kv_new.reshape(batch, n_tok, KV_WIDTH), ((0, 0), (0, SUBLANE - n_tok), (0, 0)))


def _nsa_sample_pallas(main, tail, cache_c, cache_s, cache_w, page_table, cmp_w1, cmp_w2, cmp_pe, rel_table):
    batch, n_pages = page_table.shape
    n_tok = main.shape[0] // batch
    past = n_pages * PAGE
    wbuf = cache_w.shape[1]
    assert n_tok == ROWS // GROUP and cache_c.shape[1] == PAGE and wbuf == WINDOW and WINDOW % PAGE == 0
    n_cmp = (past + n_tok - CMP_BLOCK) // CMP_STRIDE + 1
    assert n_cmp + CMP_BLOCK // CMP_STRIDE - 1 == past // CMP_STRIDE
    qs = _sample_rows(main[:, :Q_WIDTH], batch, n_tok)
    kv_new = [main[:, Q_WIDTH + c * KV_WIDTH:Q_WIDTH + (c + 1) * KV_WIDTH] for c in range(3)]
    tiles = _sample_bias_tiles(rel_table, past, n_tok)
    flat = lambda pool: pool.reshape(pool.shape[0], PAGE_ROWS, HEAD_DIM)

    kc = compress_finish(compress_part_paged(flat(cache_c), page_table, cmp_w1), cmp_w1, cmp_w2, cmp_pe)
    n = kc.shape[3]
    dist = past + np.arange(n_tok)[None, :] - (np.arange(n)[:, None] * CMP_STRIDE + CMP_BLOCK - 1)
    ok = (dist >= 0) & (np.arange(n)[:, None] < n_cmp)
    cb = jnp.where(ok[..., None], _bias_lookup(rel_table, dist), NEG_INF)
    cb = cb.reshape(n, n_tok, KV_HEADS, GROUP).transpose(2, 0, 3, 1).reshape(KV_HEADS, n, GROUP * n_tok)
    n_sel_blocks = -(-(past + n_tok) // SEL_BLOCK)
    o_cmp, pen = sample_cmp_select(qs, kc, cb, _sample_gate_rows(tail, 0, batch, n_tok), past, n_sel_blocks)

    far_then_last = jnp.asarray([0] * (n_pages - 1) + [2, 3], jnp.int32)
    o_sel = paged_attention(qs, flat(cache_s), page_table, far_then_last, _pad_new(kv_new[1], batch, n_tok), tiles,
                            pen=pen, pen_block=SEL_BLOCK, gate=_sample_gate_rows(tail, 1, batch, n_tok))
    w_pages = wbuf // PAGE
    win_table = jnp.arange(batch * w_pages, dtype=jnp.int32).reshape(batch, w_pages)
    win_tiles = jnp.asarray([1] + [0] * (w_pages - 2) + [2, 3], jnp.int32)
    o_win = paged_attention(qs, cache_w.reshape(batch * w_pages, PAGE_ROWS, HEAD_DIM), win_table, win_tiles,
                            _pad_new(kv_new[2], batch, n_tok), tiles, gate=_sample_gate_rows(tail, 2, batch, n_tok))
    outs = [_sample_unrows(o, batch, n_tok) for o in (o_cmp, o_sel, o_win)]
    shape = (batch, n_tok, 2, KV_HEADS, HEAD_DIM)
    new_win = jnp.concatenate([cache_w[:, n_tok:], kv_new[2].reshape(shape)], axis=1)
    return outs, kv_new[0].reshape(shape), kv_new[1].reshape(shape), new_win


def _moba_sample_pallas(proj, cache_kv, page_table, rel_table):
    batch, n_pages = page_table.shape
    n_tok = proj.shape[0] // batch
    past = n_pages * PAGE
    assert n_tok == ROWS // GROUP and (past // MOBA_BLOCK) * MOBA_BLOCK == past
    qs = _sample_rows(proj[:, :Q_WIDTH], batch, n_tok)
    kv_new = proj[:, Q_WIDTH:]
    pool = cache_kv.reshape(cache_kv.shape[0], PAGE_ROWS, HEAD_DIM)
    pen = moba_sample_gate(qs, cache_kv, page_table, past, n_tok)
    tile_ids = jnp.asarray([0] * (n_pages - 1) + [2, 3], jnp.int32)
    o = paged_attention(qs, pool, page_table, tile_ids, _pad_new(kv_new, batch, n_tok),
                        _sample_bias_tiles(rel_table, past, n_tok), pen=pen, pen_block=MOBA_BLOCK)
    return _sample_unrows(o, batch, n_tok), kv_new.reshape(batch, n_tok, 2, KV_HEADS, HEAD_DIM)


def _cols_to_row(col, n):
    eye = lax.broadcasted_iota(jnp.int32, (n, n), 0) == lax.broadcasted_iota(jnp.int32, (n, n), 1)
    return jnp.sum(jnp.where(eye, jnp.broadcast_to(col, (n, n)), 0.0), axis=0, keepdims=True)


def _hgrn2_step_body(q_ref, f_ref, i_ref, g_ref, lbl_ref, ng_ref, s0_ref, o_ref, s_out_ref, *, layer):
    n_tok = q_ref.shape[0]
    lbl = lbl_ref[...]
    e = jnp.exp(lbl - jnp.max(lbl, axis=0, keepdims=True))
    p = e / jnp.sum(e, axis=0, keepdims=True)
    lb = jnp.zeros((1, HG_DK), jnp.float32)
    for r in range(1, layer + 1):
        lb = lb + p[r:r + 1]
    q = q_ref[...]
    qh = q * jax.nn.sigmoid(q) * HG_DK ** -0.5
    fg = lb + (1.0 - lb) * jax.nn.sigmoid(f_ref[...])
    k = 1.0 - fg
    v = i_ref[...]
    st = s0_ref[...].T
    rows = []
    for t in range(n_tok):
        st = st * fg[t:t + 1] + _rows_to_col(v[t:t + 1], HG_DV) * k[t:t + 1]
        rows.append(_cols_to_row(jnp.sum(st * qh[t:t + 1], axis=1, keepdims=True), HG_DV))
    o_ref[...] = _head_rms_gate(jnp.concatenate(rows, axis=0), ng_ref[...], g_ref[...])
    s_out_ref[...] = st.T


def hgrn2_step(proj, lb_logits, norm_g, s0, layer):
    batch, n_tok, _ = proj.shape
    h = HG_HEADS

    def col(k):
        return pl.BlockSpec((None, n_tok, HG_DK), lambda b, hh: (b, 0, k * h + hh))

    return pl.pallas_call(
        functools.partial(_hgrn2_step_body, layer=layer),
        grid=(batch, h),
        in_specs=[col(0), col(1), col(2), col(3),
                  pl.BlockSpec((DEPTH, HG_DK), lambda b, hh: (0, hh)),
                  pl.BlockSpec((1, HG_DV), lambda b, hh: (0, 0)),
                  pl.BlockSpec((None, None, HG_DK, HG_DV), lambda b, hh: (b, hh, 0, 0))],
        out_specs=[pl.BlockSpec((None, n_tok, HG_DV), lambda b, hh: (b, 0, hh)),
                   pl.BlockSpec((None, None, HG_DK, HG_DV), lambda b, hh: (b, hh, 0, 0))],
        out_shape=[jax.ShapeDtypeStruct((batch, n_tok, h * HG_DV), jnp.float32),
                   jax.ShapeDtypeStruct((batch, h, HG_DK, HG_DV), jnp.float32)],
        compiler_params=_cparams("parallel", "parallel"),
        name="hgrn2_step",
    )(proj, proj, proj, proj, lb_logits, norm_g.reshape(1, HG_DV), s0)


def _gdn_step_body(q_ref, k_ref, v_ref, z_ref, t_ref, bq_ref, bk_ref, bv_ref, wq_ref, wk_ref, wv_ref,
                   al_ref, dtb_ref, ng_ref, s0_ref, o_ref, s_out_ref, xs_ref):
    hq = pl.program_id(1)
    n_tok = q_ref.shape[0]
    dk, dv = GDN_DK, GDN_DV
    pad = SUBLANE
    xs_ref[0:pad, :] = jnp.concatenate([bq_ref[...], bk_ref[...], bv_ref[...]], axis=1)
    x = jnp.concatenate([q_ref[...], k_ref[...], v_ref[...]], axis=1)
    xs_ref[pad:pad + n_tok, :] = x
    cw = jnp.concatenate([wq_ref[...], wk_ref[...], wv_ref[...]], axis=1)
    y = xs_ref[pad - 3:pad - 3 + n_tok, :] * cw[0:1]
    for i in range(1, GDN_CONV - 1):
        y = y + xs_ref[pad - 3 + i:pad - 3 + i + n_tok, :] * cw[i:i + 1]
    y = y + x * cw[GDN_CONV - 1:GDN_CONV]
    y = y * jax.nn.sigmoid(y)
    q = _l2n(y[:, 0:dk]) * dk ** -0.5
    k = _l2n(y[:, dk:2 * dk])
    tl = t_ref[...]
    beta_all = jax.nn.sigmoid(tl)
    la_all = -jnp.exp(al_ref[...]) * _softplus(tl + dtb_ref[...])
    for e in range(GDN_REP):
        hv = hq * GDN_REP + e
        v = y[:, 2 * dk + e * dv:2 * dk + (e + 1) * dv]
        bt = _lane_column(beta_all, hv)
        a = jnp.exp(_lane_column(la_all, GDN_V_HEADS + hv))
        s = s0_ref[e]
        rows = []
        for t in range(n_tok):
            k_col = _rows_to_col(k[t:t + 1], dk)
            ks = jnp.sum(s * k_col, axis=0, keepdims=True)
            u = bt[t:t + 1] * (v[t:t + 1] - a[t:t + 1] * ks)
            s = a[t:t + 1] * s + k_col * u
            rows.append(jnp.sum(s * _rows_to_col(q[t:t + 1], dk), axis=0, keepdims=True))
        s_out_ref[e] = s
        o_ref[:, e * dv:(e + 1) * dv] = _head_rms_gate(jnp.concatenate(rows, axis=0), ng_ref[...],
                                                       z_ref[:, e * dv:(e + 1) * dv])


def gdn_step(main, tail, conv_buf, conv_w, a_log, dt_bias, norm_g, s0):
    batch, n_tok, _ = main.shape
    hq, rep, dk, dv = GDN_QK_HEADS, GDN_REP, GDN_DK, GDN_DV
    vw = rep * dv
    buf = jnp.pad(conv_buf, ((0, 0), (SUBLANE - (GDN_CONV - 1), 0), (0, 0)))
    pad_lanes = jnp.zeros((LANE - 2 * GDN_V_HEADS,), jnp.float32)
    a_row = jnp.concatenate([jnp.zeros((GDN_V_HEADS,), jnp.float32), a_log, pad_lanes]).reshape(1, LANE)
    dt_row = jnp.concatenate([jnp.zeros((GDN_V_HEADS,), jnp.float32), dt_bias, pad_lanes]).reshape(1, LANE)
    k0 = hq
    v0 = 2 * hq * dk // vw
    z0 = GDN_CONV_DIM // vw
    return pl.pallas_call(
        _gdn_step_body,
        grid=(batch, hq),
        in_specs=[pl.BlockSpec((None, n_tok, dk), lambda b, h: (b, 0, h)),
                  pl.BlockSpec((None, n_tok, dk), lambda b, h: (b, 0, k0 + h)),
                  pl.BlockSpec((None, n_tok, vw), lambda b, h: (b, 0, v0 + h)),
                  pl.BlockSpec((None, n_tok, vw), lambda b, h: (b, 0, z0 + h)),
                  pl.BlockSpec((None, n_tok, LANE), lambda b, h: (b, 0, 0)),
                  pl.BlockSpec((None, SUBLANE, dk), lambda b, h: (b, 0, h)),
                  pl.BlockSpec((None, SUBLANE, dk), lambda b, h: (b, 0, k0 + h)),
                  pl.BlockSpec((None, SUBLANE, vw), lambda b, h: (b, 0, v0 + h)),
                  pl.BlockSpec((GDN_CONV, dk), lambda b, h: (0, h)),
                  pl.BlockSpec((GDN_CONV, dk), lambda b, h: (0, k0 + h)),
                  pl.BlockSpec((GDN_CONV, vw), lambda b, h: (0, v0 + h)),
                  pl.BlockSpec((1, LANE), lambda b, h: (0, 0)),
                  pl.BlockSpec((1, LANE), lambda b, h: (0, 0)),
                  pl.BlockSpec((1, dv), lambda b, h: (0, 0)),
                  pl.BlockSpec((None, rep, dk, dv), lambda b, h: (b, h, 0, 0))],
        out_specs=[pl.BlockSpec((None, n_tok, vw), lambda b, h: (b, 0, h)),
                   pl.BlockSpec((None, rep, dk, dv), lambda b, h: (b, h, 0, 0))],
        out_shape=[jax.ShapeDtypeStruct((batch, n_tok, GDN_V_HEADS * dv), jnp.float32),
                   jax.ShapeDtypeStruct((batch, GDN_V_HEADS, dk, dv), jnp.float32)],
        scratch_shapes=[pltpu.VMEM((2 * SUBLANE, 2 * dk + vw), jnp.float32)],
        compiler_params=_cparams("parallel", "parallel"),
        name="gdn_step",
    )(main, main, main, main, tail, buf, buf, buf, conv_w, conv_w, conv_w, a_row, dt_row, norm_g.reshape(1, dv), s0)


def _pad_cols(w, mult=LANE):
    n = w.shape[1]
    return jnp.pad(w, ((0, 0), (0, (-n) % mult)))


def _nsa_prompt(main, kv_rows, tail, batch, seq, cmp_w1, cmp_w2, cmp_pe, rel_table):
    kv_c, kv_s, kv_w = (kv_rows[c].reshape(batch, seq, 2, KV_HEADS, HEAD_DIM) for c in range(3))
    kc = compress_finish(compress_part_rows(main, Q_WIDTH // KV_WIDTH, cmp_w1, batch, seq), cmp_w1, cmp_w2, cmp_pe)
    o_cmp, pen = cmp_select(main, tail, kc, _cmp_bias_table(rel_table, kc.shape[3]), batch, seq)
    col = Q_WIDTH // HEAD_DIM
    o_sel = flash_attention(main, main, col + 2 * KV_HEADS, col + 3 * KV_HEADS, _flash_bias_tiles(rel_table, 0),
                            batch, seq, pen=pen, pen_block=SEL_BLOCK, gate_arr=tail, gate_col0=N_HEADS)
    o_win = flash_attention(main, main, col + 4 * KV_HEADS, col + 5 * KV_HEADS,
                            _flash_bias_tiles(rel_table, WINDOW), batch, seq, k_back=WINDOW // ATTN_TILE,
                            gate_arr=tail, gate_col0=2 * N_HEADS)
    return [o_cmp, o_sel, o_win], kv_c, kv_s, kv_w[:, -min(WINDOW, seq):]


def _moba_prompt(proj, batch, seq, rel_table):
    col = Q_WIDTH // HEAD_DIM
    pen = moba_gate(proj, col, batch, seq)
    return flash_attention(proj, proj, col, col + KV_HEADS, _flash_bias_tiles(rel_table, 0), batch, seq,
                           pen=pen, pen_block=MOBA_BLOCK)


def kernel(x_prompt, x_sample, cache_nsa_cmp_kv, cache_nsa_sel_kv, cache_nsa_win_kv, cache_moba_kv,
           state_hgrn2, state_gdn_conv, state_gdn_ssm, page_table, rel_table, ln_mix, ln_ffn, ln_final,
           ffn_w_up, ffn_w_down, nsa_w_in, nsa_cmp_w1, nsa_cmp_w2, nsa_cmp_pe, nsa_w_out, moba_w_in, moba_w_out,
           hg_w_in, hg_lb_logits, hg_norm, hg_w_out, gdn_w_in, gdn_conv_w, gdn_a_log, gdn_dt_bias, gdn_norm,
           gdn_w_out):
    bf = jnp.bfloat16
    bp, tp = x_prompt.shape[:2]
    bs, ts = x_sample.shape[:2]
    assert tp % ATTN_TILE == 0 and WINDOW % ATTN_TILE == 0 and ATTN_TILE == MOBA_BLOCK and tp % SCAN_TILE == 0
    xp = x_prompt.reshape(bp * tp, D_MODEL)
    xs = x_sample.reshape(bs * ts, D_MODEL)
    w_up, w_down = ffn_w_up.astype(bf), ffn_w_down.astype(bf)

    for layer in range(DEPTH):
        kind = layer % N_MIXERS
        g_mix = ln_mix[layer]
        if kind == 0:
            w_main, w_tail = nsa_w_in.astype(bf), _pad_cols(nsa_w_in[:, NSA_MAIN:]).astype(bf)
            main_p, kv_p = norm_matmul(xp, g_mix, w_main, kv_from=Q_WIDTH // KV_WIDTH, n_cols=NSA_MAIN)
            tail_p = norm_matmul(xp, g_mix, w_tail)
            main_s, tail_s = norm_matmul(xs, g_mix, w_main, n_cols=NSA_MAIN), norm_matmul(xs, g_mix, w_tail)
            op, nsa_cmp_p, nsa_sel_p, nsa_win_p = _nsa_prompt(main_p, kv_p, tail_p, bp, tp, nsa_cmp_w1, nsa_cmp_w2,
                                                              nsa_cmp_pe, rel_table)
            os_, nsa_cmp_s, nsa_sel_s, nsa_win_s = _nsa_sample_pallas(main_s, tail_s, cache_nsa_cmp_kv,
                                                                     cache_nsa_sel_kv, cache_nsa_win_kv, page_table,
                                                                     nsa_cmp_w1, nsa_cmp_w2, nsa_cmp_pe, rel_table)
            w_out = nsa_w_out.astype(bf)
        elif kind == 1:
            w_in = moba_w_in.astype(bf)
            pp, kv_p = norm_matmul(xp, g_mix, w_in, kv_from=Q_WIDTH // KV_WIDTH)
            op, moba_p = _moba_prompt(pp, bp, tp, rel_table), kv_p.reshape(bp, tp, 2, KV_HEADS, HEAD_DIM)
            os_, moba_s = _moba_sample_pallas(norm_matmul(xs, g_mix, w_in), cache_moba_kv, page_table, rel_table)
            op, os_ = [op], [os_]
            w_out = moba_w_out.astype(bf)
        elif kind == 2:
            w_in = hg_w_in.astype(bf)
            pp = norm_matmul(xp, g_mix, w_in)
            ps = norm_matmul(xs, g_mix, w_in).reshape(bs, ts, -1)
            s0 = jnp.zeros((bp, HG_HEADS, HG_DK, HG_DV), jnp.float32)
            op, hg_p = hgrn2_scan(pp, hg_lb_logits, hg_norm, s0, layer, bp, tp)
            os_, hg_s = hgrn2_step(ps, hg_lb_logits, hg_norm, state_hgrn2, layer)
            op, os_ = [op], [os_.reshape(bs * ts, -1)]
            w_out = hg_w_out.astype(bf)
        else:
            w_main, w_tail = gdn_w_in.astype(bf), _pad_cols(gdn_w_in[:, GDN_MAIN:]).astype(bf)
            main_p, tail_p = norm_matmul(xp, g_mix, w_main, n_cols=GDN_MAIN), norm_matmul(xp, g_mix, w_tail)
            main_s = norm_matmul(xs, g_mix, w_main, n_cols=GDN_MAIN).reshape(bs, ts, -1)
            tail_s = norm_matmul(xs, g_mix, w_tail).reshape(bs, ts, -1)
            buf0 = jnp.zeros((bp, GDN_CONV - 1, GDN_CONV_DIM), jnp.float32)
            s0 = jnp.zeros((bp, GDN_V_HEADS, GDN_DK, GDN_DV), jnp.float32)
            op, ssm_p = gdn_scan(main_p, tail_p, buf0, gdn_conv_w, gdn_a_log, gdn_dt_bias, gdn_norm, s0, bp, tp)
            conv_p = main_p.reshape(bp, tp, -1)[:, tp - (GDN_CONV - 1):, :GDN_CONV_DIM]
            os_, ssm_s = gdn_step(main_s, tail_s, state_gdn_conv, gdn_conv_w, gdn_a_log, gdn_dt_bias, gdn_norm,
                                  state_gdn_ssm)
            conv_s = jnp.concatenate([state_gdn_conv, main_s[:, :, :GDN_CONV_DIM]], axis=1)[:, ts:]
            op, os_ = [op], [os_.reshape(bs * ts, -1)]
            w_out = gdn_w_out.astype(bf)
        xp = matmul_res(op, w_out, xp)
        xs = matmul_res(os_, w_out, xs)
        xp = ffn(xp, ln_ffn[layer], w_up, w_down, layer)
        xs = ffn(xs, ln_ffn[layer], w_up, w_down, layer)
    y_prompt = final_norm(xp, ln_final).reshape(bp, tp, D_MODEL)
    y_sample = final_norm(xs, ln_final).reshape(bs, ts, D_MODEL)
    return (y_prompt, y_sample, nsa_cmp_p, nsa_cmp_s, nsa_sel_p, nsa_sel_s, nsa_win_p, nsa_win_s,
            moba_p, moba_s, hg_p, hg_s, conv_p, conv_s, ssm_p, ssm_s)
```

```python
import functools
import math

import jax
import jax.numpy as jnp
import numpy as np
from jax import lax
from jax.experimental import pallas as pl
from jax.experimental.pallas import tpu as pltpu

D_MODEL = 2048
DEPTH = 4
N_MIXERS = 4
HEAD_DIM = 128
N_HEADS = D_MODEL // HEAD_DIM
KV_HEADS = 4
GROUP = N_HEADS // KV_HEADS
ATTN_SCALE = HEAD_DIM ** -0.5
REL_BUCKETS = 32
REL_MAX_DIST = 128
CMP_BLOCK = 32
CMP_STRIDE = 16
CMP_HIDDEN = HEAD_DIM
SEL_BLOCK = 64
N_SEL = 16
WINDOW = 512
FORCE_SCORE = 1.0e4
MOBA_BLOCK = 256
MOBA_TOPK = 3
HG_DK = 128
HG_HEADS = D_MODEL // HG_DK
HG_DV = D_MODEL // HG_HEADS
GDN_DK = 128
GDN_DV = 128
GDN_QK_HEADS = D_MODEL // GDN_DK
GDN_V_HEADS = 2 * GDN_QK_HEADS
GDN_REP = GDN_V_HEADS // GDN_QK_HEADS
GDN_CONV = 4
GDN_CONV_DIM = 2 * GDN_QK_HEADS * GDN_DK + GDN_V_HEADS * GDN_DV
CHUNK = 64
NEG_INF = -1.0e30
NORM_EPS = 1e-6

Q_WIDTH = N_HEADS * HEAD_DIM
KV_WIDTH = 2 * KV_HEADS * HEAD_DIM
NSA_MAIN = Q_WIDTH + 3 * KV_WIDTH
GDN_MAIN = GDN_CONV_DIM + GDN_V_HEADS * GDN_DV

V7X_VMEM_LIMIT_BYTES = 56 * 1024 * 1024
LANE = 128
SUBLANE = 8
ATTN_TILE = 256
CMP_TILE = 128
CMP_GPS = 2
SCAN_TILE = 512
HG_HPS = 4
GDN_HPS = 2
LOG2E = math.log2(math.e)
NT_DIMS = (((1,), (1,)), ((), ()))
TN_DIMS = (((0,), (0,)), ((), ()))


def _cparams(*sem):
    return pltpu.CompilerParams(dimension_semantics=sem, vmem_limit_bytes=V7X_VMEM_LIMIT_BYTES)


def _row_tile(m, target):
    t = min(m, target)
    while m % t:
        t //= 2
    return t


def _col_tile(n, target):
    t = min(n, target)
    while n % t or t % LANE:
        t -= LANE
    return t


def _split_bf16(x, parts):
    out = []
    for _ in range(parts - 1):
        hi = x.astype(jnp.bfloat16)
        out.append(hi)
        x = x - hi.astype(jnp.float32)
    out.append(x.astype(jnp.bfloat16))
    return out


def _bf(x):
    return x.astype(jnp.bfloat16)


def _dot(a, b, dims=None):
    if dims is None:
        return jnp.dot(a, b, preferred_element_type=jnp.float32)
    return lax.dot_general(a, b, dims, preferred_element_type=jnp.float32)


def _norm_matmul_body(x_ref, g_ref, w_ref, o_ref, *rest, kv_from):
    kv_ref, h_ref = rest if kv_from is not None else (None, rest[0])
    j = pl.program_id(1)

    @pl.when(j == 0)
    def _():
        x = x_ref[...]
        ms = jnp.mean(x * x, axis=-1, keepdims=True)
        h_ref[...] = _bf(x * lax.rsqrt(ms + NORM_EPS) * g_ref[...])

    out = _dot(h_ref[...], w_ref[...])
    o_ref[...] = out
    if kv_from is not None:
        @pl.when(j >= kv_from)
        def _():
            tm = out.shape[0]
            for slab in range(out.shape[1] // HEAD_DIM):
                kv_ref[pl.ds(slab, tm, stride=out.shape[1] // HEAD_DIM), :] = out[:, slab * HEAD_DIM:(slab + 1) * HEAD_DIM]


def norm_matmul(x, g, w, kv_from=None, n_cols=None):
    m, k = x.shape
    n = n_cols or w.shape[1]
    tm = _row_tile(m, 1024)
    tn = _col_tile(n, 1024)
    out_specs = [pl.BlockSpec((tm, tn), lambda i, j: (i, j))]
    out_shape = [jax.ShapeDtypeStruct((m, n), jnp.float32)]
    if kv_from is not None:
        assert tn == KV_WIDTH
        slabs = tn // HEAD_DIM
        out_specs.append(pl.BlockSpec((None, tm * slabs, HEAD_DIM), lambda i, j: (jnp.maximum(j - kv_from, 0), i, 0)))
        out_shape.append(jax.ShapeDtypeStruct((n // tn - kv_from, m * slabs, HEAD_DIM), jnp.float32))
    res = pl.pallas_call(
        functools.partial(_norm_matmul_body, kv_from=kv_from),
        grid=(m // tm, n // tn),
        in_specs=[pl.BlockSpec((tm, k), lambda i, j: (i, 0)),
                  pl.BlockSpec((1, k), lambda i, j: (0, 0)),
                  pl.BlockSpec((k, tn), lambda i, j: (0, j))],
        out_specs=out_specs,
        out_shape=out_shape,
        scratch_shapes=[pltpu.VMEM((tm, k), jnp.bfloat16)],
        compiler_params=_cparams("parallel", "arbitrary"),
        name="norm_matmul",
    )(x, g.reshape(1, k), w)
    return res if kv_from is not None else res[0]


def _matmul_res_body(*refs):
    *a_refs, w_ref, r_ref, o_ref = refs
    a = a_refs[0][...]
    for a_ref in a_refs[1:]:
        a = a + a_ref[...]
    o_ref[...] = r_ref[...] + _dot(_bf(a), w_ref[...])


def matmul_res(a_list, w, res):
    m, k = a_list[0].shape
    n = w.shape[1]
    tm = _row_tile(m, 512)
    tn = _col_tile(n, 1024)
    return pl.pallas_call(
        _matmul_res_body,
        grid=(m // tm, n // tn),
        in_specs=[pl.BlockSpec((tm, k), lambda i, j: (i, 0)) for _ in a_list]
        + [pl.BlockSpec((k, tn), lambda i, j: (0, j)),
           pl.BlockSpec((tm, tn), lambda i, j: (i, j))],
        out_specs=pl.BlockSpec((tm, tn), lambda i, j: (i, j)),
        out_shape=jax.ShapeDtypeStruct((m, n), jnp.float32),
        compiler_params=_cparams("parallel", "arbitrary"),
        name="matmul_res",
    )(*a_list, w, res)


def _ffn_body(x_ref, g_ref, wa_ref, wb_ref, wd_ref, o_ref, h_ref):
    @pl.when(pl.program_id(1) == 0)
    def _():
        x = x_ref[...]
        ms = jnp.mean(x * x, axis=-1, keepdims=True)
        h_ref[...] = _bf(x * lax.rsqrt(ms + NORM_EPS) * g_ref[...])
        o_ref[...] = x

    h = h_ref[...]
    a = _dot(h, wa_ref[...])
    b = _dot(h, wb_ref[...])
    o_ref[...] += _dot(_bf(a * jax.nn.sigmoid(a) * b), wd_ref[...])


def ffn(x, g, w_up, w_down, layer):
    m, k = x.shape
    hdim = w_down.shape[1]
    tm = _row_tile(m, 1024)
    th = _col_tile(hdim, 512)
    nh = hdim // th
    return pl.pallas_call(
        _ffn_body,
        grid=(m // tm, nh),
        in_specs=[pl.BlockSpec((tm, k), lambda i, j: (i, 0)),
                  pl.BlockSpec((1, k), lambda i, j: (0, 0)),
                  pl.BlockSpec((None, k, th), lambda i, j: (layer, 0, j)),
                  pl.BlockSpec((None, k, th), lambda i, j: (layer, 0, j + nh)),
                  pl.BlockSpec((None, th, k), lambda i, j: (layer, j, 0))],
        out_specs=pl.BlockSpec((tm, k), lambda i, j: (i, 0)),
        out_shape=jax.ShapeDtypeStruct((m, k), jnp.float32),
        scratch_shapes=[pltpu.VMEM((tm, k), jnp.bfloat16)],
        compiler_params=_cparams("parallel", "arbitrary"),
        name="ffn",
    )(x, g.reshape(1, k), w_up, w_up, w_down)


def _norm_body(x_ref, g_ref, o_ref):
    x = x_ref[...]
    ms = jnp.mean(x * x, axis=-1, keepdims=True)
    o_ref[...] = x * lax.rsqrt(ms + NORM_EPS) * g_ref[...]


def final_norm(x, g):
    m, k = x.shape
    tm = _row_tile(m, 512)
    return pl.pallas_call(
        _norm_body,
        grid=(m // tm,),
        in_specs=[pl.BlockSpec((tm, k), lambda i: (i, 0)), pl.BlockSpec((1, k), lambda i: (0, 0))],
        out_specs=pl.BlockSpec((tm, k), lambda i: (i, 0)),
        out_shape=jax.ShapeDtypeStruct((m, k), jnp.float32),
        compiler_params=_cparams("parallel"),
        name="final_norm",
    )(x, g.reshape(1, k))


def _bucket_np(dist):
    exact = REL_BUCKETS // 2
    d = np.maximum(dist, 0)
    ratio = np.log(np.maximum(d, 1).astype(np.float32) / exact) / math.log(REL_MAX_DIST / exact)
    large = np.minimum(exact + (ratio * (REL_BUCKETS - exact)).astype(np.int32), REL_BUCKETS - 1)
    return np.where(d < exact, d, large)


def _bias_lookup(rel_table, dist):
    bucket = _bucket_np(dist).astype(np.int32)
    ids = [int(b) for b in np.unique(bucket)]
    bk = jnp.asarray(bucket)[..., None]
    out = jnp.broadcast_to(rel_table[ids[0]], bucket.shape + (rel_table.shape[1],))
    for b in ids[1:]:
        out = jnp.where(bk == b, rel_table[b], out)
    return out


def _heads_to_lanes(t):
    keys, queries, _ = t.shape
    return t.reshape(keys, queries, KV_HEADS, GROUP).transpose(2, 0, 3, 1).reshape(KV_HEADS, keys, GROUP * queries)


def _flash_bias_tiles(rel_table, window):
    j = np.arange(ATTN_TILE)[:, None]
    i = np.arange(ATTN_TILE)[None, :]
    n_cls = window // ATTN_TILE + 1 if window else -(-REL_MAX_DIST // ATTN_TILE) + 2
    tiles = []
    for d in range(n_cls):
        dist = d * ATTN_TILE + i - j
        ok = dist >= 0
        if window:
            ok = ok & (dist < window)
        tiles.append(_heads_to_lanes(jnp.where(ok[..., None], _bias_lookup(rel_table, dist) * LOG2E, NEG_INF)))
    return jnp.stack(tiles, axis=1)


def _cmp_bias_table(rel_table, ncp):
    x = np.arange(ncp)[:, None]
    i = np.arange(CMP_TILE)[None, :]
    dist = i - CMP_STRIDE * (x - 16) - (CMP_BLOCK - 1)
    far = rel_table[REL_BUCKETS - 1]
    b = _heads_to_lanes(jnp.where((dist >= 0)[..., None], _bias_lookup(rel_table, dist), far))
    return jnp.concatenate([b, b], axis=1)


def _stack_heads(q):
    return jnp.concatenate([q[:, r * HEAD_DIM:(r + 1) * HEAD_DIM] for r in range(GROUP)], axis=0)


def _gate_columns(gt_ref, col0):
    gt = jax.nn.sigmoid(gt_ref[...])
    lane = lax.broadcasted_iota(jnp.int32, gt.shape, 1)
    return [jnp.sum(jnp.where(lane == col0 + r, gt, 0.0), axis=1, keepdims=True) for r in range(GROUP)]


def _heads_from_lanes(o_t, rows, cols=None):
    parts = []
    for r in range(GROUP):
        part = o_t[:, r * rows:(r + 1) * rows].T
        if cols is not None:
            part = part * cols[r]
        parts.append(part)
    return jnp.concatenate(parts, axis=1)


def _rank_rows(score, n_rows):
    row = lax.broadcasted_iota(jnp.int32, score.shape, 0)
    rank = jnp.zeros(score.shape, jnp.int32)
    for mm in range(n_rows):
        sm = score[mm:mm + 1, :]
        ahead = (sm > score) | ((sm == score) & (row > mm))
        rank = rank + ahead.astype(jnp.int32)
    return rank


def _cmp_select_body(q_ref, kc_ref, dt_ref, gt_ref, o_ref, pen_ref, *, n_sel_blocks):
    width = GROUP * HEAD_DIM
    for gg in range(CMP_GPS):
        cs = slice(gg * width, (gg + 1) * width)
        _cmp_select_group(q_ref.at[:, cs], kc_ref.at[gg], dt_ref.at[gg], gt_ref, o_ref.at[:, cs], pen_ref.at[gg],
                          pl.program_id(1) * CMP_GPS + gg, pl.program_id(2), n_sel_blocks)


def _cmp_select_group(q_ref, kc_ref, dt_ref, gt_ref, o_ref, pen_ref, g, qi, n_sel_blocks):
    tq = CMP_TILE
    cols = GROUP * tq
    ncp = kc_ref.shape[1]
    q4 = _bf(_stack_heads(q_ref[...]) * ATTN_SCALE)
    s = _dot(_bf(kc_ref[0]), q4, NT_DIMS)
    shift = (qi * (tq // CMP_STRIDE) + ncp - 16) % ncp
    bias = dt_ref[pl.ds(pl.multiple_of(ncp - shift, SUBLANE), ncp), :]
    t_col = qi * tq + (lax.broadcasted_iota(jnp.int32, (ncp, cols), 1) & (tq - 1))
    end_pos = lax.broadcasted_iota(jnp.int32, (ncp, cols), 0) * CMP_STRIDE + (CMP_BLOCK - 1)
    mask = t_col >= end_pos
    s = jnp.where(mask, s + bias, NEG_INF)
    m = jnp.max(s, axis=0, keepdims=True)
    e = jnp.where(mask, jnp.exp(s - m), 0.0)
    p = e / jnp.maximum(jnp.sum(e, axis=0, keepdims=True), 1e-30)
    o_t = _dot(_bf(kc_ref[1].T), _bf(p))
    o_ref[...] = _heads_from_lanes(o_t, tq, _gate_columns(gt_ref, g * GROUP))

    imp = p[:, 0:tq]
    for r in range(1, GROUP):
        imp = imp + p[:, r * tq:(r + 1) * tq]
    ratio = SEL_BLOCK // CMP_STRIDE
    j_i = lax.broadcasted_iota(jnp.int32, (n_sel_blocks, ncp), 0)
    c_i = lax.broadcasted_iota(jnp.int32, (n_sel_blocks, ncp), 1)
    w = _bf((c_i >= ratio * j_i - 1) & (c_i <= ratio * j_i + ratio - 1))
    score = sum(_dot(w, part) for part in _split_bf16(imp, 3))
    blk = lax.broadcasted_iota(jnp.int32, (n_sel_blocks, tq), 0)
    tok = qi * tq + lax.broadcasted_iota(jnp.int32, (n_sel_blocks, tq), 1)
    cur = lax.shift_right_logical(tok, int(math.log2(SEL_BLOCK)))
    forced = (blk == 0) | (blk == cur) | (blk == cur - 1)
    causal = blk <= cur
    score = jnp.where(forced, FORCE_SCORE, score)
    score = jnp.where(causal, score, -1.0)
    chosen = (_rank_rows(score, n_sel_blocks) < N_SEL) & causal
    pen_ref[...] = jnp.where(chosen, 0.0, NEG_INF)


def cmp_select(proj, tail, kc, dt, batch, seq):
    tq = CMP_TILE
    nq = seq // tq
    ncp = kc.shape[3]
    nsb = seq // SEL_BLOCK
    return pl.pallas_call(
        functools.partial(_cmp_select_body, n_sel_blocks=nsb),
        grid=(batch, KV_HEADS // CMP_GPS, nq),
        in_specs=[pl.BlockSpec((tq, CMP_GPS * GROUP * HEAD_DIM), lambda b, g, i: (b * nq + i, g)),
                  pl.BlockSpec((None, CMP_GPS, 2, ncp, HEAD_DIM), lambda b, g, i: (b, g, 0, 0, 0)),
                  pl.BlockSpec((CMP_GPS, 2 * ncp, GROUP * tq), lambda b, g, i: (g, 0, 0)),
                  pl.BlockSpec((tq, LANE), lambda b, g, i: (b * nq + i, 0))],
        out_specs=[pl.BlockSpec((tq, CMP_GPS * GROUP * HEAD_DIM), lambda b, g, i: (b * nq + i, g)),
                   pl.BlockSpec((None, CMP_GPS, nsb, tq), lambda b, g, i: (b, g, 0, i))],
        out_shape=[jax.ShapeDtypeStruct((batch * seq, Q_WIDTH), jnp.float32),
                   jax.ShapeDtypeStruct((batch, KV_HEADS, nsb, seq), jnp.float32)],
        compiler_params=_cparams("parallel", "parallel", "arbitrary"),
        name="cmp_select",
    )(proj, kc, dt, tail)


def _flash_body(*refs, pen_block, pen_per_head, k_back, gate_col0, seq):
    it = iter(refs)
    q_ref, k_ref, v_ref, bt_ref = next(it), next(it), next(it), next(it)
    pen_ref = next(it) if pen_block else None
    gt_ref = next(it) if gate_col0 is not None else None
    o_ref, m_ref, acc_ref, qa_ref, kb_ref, vt_ref, sa_ref, sb_ref = (next(it) for _ in range(8))
    g = pl.program_id(1)
    qi = pl.program_id(2)
    tq = tk = ATTN_TILE
    n_cls = bt_ref.shape[0]

    @pl.when(qi == 0)
    def _():
        vt_ref[HEAD_DIM:, :] = jnp.ones((SUBLANE, seq), jnp.bfloat16)
        for c in range(seq // tk):
            kb_ref[c * tk:(c + 1) * tk, :] = _bf(k_ref[c * tk:(c + 1) * tk, :])
            vt_ref[0:HEAD_DIM, c * tk:(c + 1) * tk] = _bf(v_ref[c * tk:(c + 1) * tk, :].T)

    qa_ref[...] = _bf(_stack_heads(q_ref[...]) * (ATTN_SCALE * LOG2E))
    m_ref[...] = jnp.full(m_ref.shape, NEG_INF, jnp.float32)
    acc_ref[...] = jnp.zeros(acc_ref.shape, jnp.float32)

    def raw_logits(kj):
        k0 = pl.multiple_of(jnp.minimum(kj, qi) * tk, tk)
        return _dot(kb_ref[pl.ds(k0, tk), :], qa_ref[...], NT_DIMS)

    per_tile = tk // pen_block if pen_block else 1
    blk = tk // per_tile

    def attend(s_ref, kj, far):
        k0 = pl.multiple_of(kj * tk, tk)
        row = bt_ref[n_cls - 1, 0:1, :] if far else None
        bias = None if far else bt_ref[jnp.minimum(qi - kj, n_cls - 1)]
        pieces = []
        for a in range(per_tile):
            rows = slice(a * blk, (a + 1) * blk)
            piece = s_ref[rows, :]
            add = row
            if pen_block:
                pen = pen_ref[pl.ds(kj * per_tile + a, 1), :]
                pen = pen if pen_per_head else jnp.concatenate([pen] * GROUP, axis=1)
                add = pen if row is None else pen + row
            if bias is not None:
                piece = piece + bias[rows]
            pieces.append(piece if add is None else piece + add)
        s = pieces[0] if per_tile == 1 else jnp.concatenate(pieces, axis=0)
        m_prev = m_ref[...]
        m_new = jnp.maximum(m_prev, jnp.max(s, axis=0, keepdims=True))
        p = jnp.exp2(s - m_new)
        acc_ref[...] = jnp.exp2(m_prev - m_new) * acc_ref[...] + _dot(vt_ref[:, pl.ds(k0, tk)], _bf(p))
        m_ref[...] = m_new

    k_lo = jnp.maximum(qi - k_back, 0) if k_back is not None else 0
    n_tiles = qi - k_lo + 1
    far_pairs = jnp.maximum(n_tiles - (n_cls - 1), 0) // 2 if k_back is None else 0
    sa_ref[...] = raw_logits(k_lo)

    def pair(pi, carry, far):
        ka = k_lo + 2 * pi
        sb_ref[...] = raw_logits(ka + 1)
        attend(sa_ref, ka, far)
        sa_ref[...] = raw_logits(ka + 2)
        attend(sb_ref, ka + 1, far)
        return carry

    lax.fori_loop(0, far_pairs, functools.partial(pair, far=True), 0)
    lax.fori_loop(far_pairs, n_tiles // 2, functools.partial(pair, far=False), 0)

    @pl.when(n_tiles % 2 == 1)
    def _():
        attend(sa_ref, qi, False)
    acc = acc_ref[...]
    o_t = acc[0:HEAD_DIM] / jnp.maximum(acc[HEAD_DIM:HEAD_DIM + 1], 1e-30)
    cols_g = _gate_columns(gt_ref, gate_col0 + g * GROUP) if gate_col0 is not None else None
    o_ref[...] = _heads_from_lanes(o_t, tq, cols_g)


def flash_attention(q_arr, kv_arr, k_col, v_col, bias, batch, seq, *, pen=None, pen_block=0,
                    k_back=None, gate_arr=None, gate_col0=None):
    tq = ATTN_TILE
    nq = seq // tq
    cols = GROUP * tq
    assert k_back is None or k_back == bias.shape[1] - 1
    in_specs = [pl.BlockSpec((tq, GROUP * HEAD_DIM), lambda b, g, i: (b * nq + i, g)),
                pl.BlockSpec((seq, HEAD_DIM), lambda b, g, i: (b, k_col + g)),
                pl.BlockSpec((seq, HEAD_DIM), lambda b, g, i: (b, v_col + g)),
                pl.BlockSpec((None,) + bias.shape[1:], lambda b, g, i: (g, 0, 0, 0))]
    args = [q_arr, kv_arr, kv_arr, bias]
    pen_per_head = False
    if pen is not None:
        if pen.ndim == 4:
            in_specs.append(pl.BlockSpec((None, None, pen.shape[2], tq), lambda b, g, i: (b, g, 0, i)))
        else:
            pen_per_head = True
            in_specs.append(pl.BlockSpec((None, None, None, pen.shape[3], cols), lambda b, g, i: (b, g, i, 0, 0)))
        args.append(pen)
    if gate_arr is not None:
        in_specs.append(pl.BlockSpec((tq, LANE), lambda b, g, i: (b * nq + i, 0)))
        args.append(gate_arr)
    return pl.pallas_call(
        functools.partial(_flash_body, pen_block=pen_block if pen is not None else 0, pen_per_head=pen_per_head,
                          k_back=k_back, gate_col0=gate_col0 if gate_arr is not None else None, seq=seq),
        grid=(batch, KV_HEADS, nq),
        in_specs=in_specs,
        out_specs=pl.BlockSpec((tq, GROUP * HEAD_DIM), lambda b, g, i: (b * nq + i, g)),
        out_shape=jax.ShapeDtypeStruct((batch * seq, Q_WIDTH), jnp.float32),
        scratch_shapes=[pltpu.VMEM((1, cols), jnp.float32),
                        pltpu.VMEM((HEAD_DIM + SUBLANE, cols), jnp.float32),
                        pltpu.VMEM((cols, HEAD_DIM), jnp.bfloat16),
                        pltpu.VMEM((seq, HEAD_DIM), jnp.bfloat16),
                        pltpu.VMEM((HEAD_DIM + SUBLANE, seq), jnp.bfloat16),
                        pltpu.VMEM((tq, cols), jnp.float32), pltpu.VMEM((tq, cols), jnp.float32)],
        compiler_params=_cparams("parallel", "parallel", "arbitrary"),
        name="flash_attention",
    )(*args)


def _moba_gate_body(q_ref, k_ref, pen_ref, km_ref, *, n_blocks):
    qi = pl.program_id(2)
    tq = ATTN_TILE
    cols = GROUP * tq

    @pl.when(qi == 0)
    def _():
        k = k_ref[...]
        km_ref[...] = jnp.sum(k.reshape(n_blocks, MOBA_BLOCK, HEAD_DIM), axis=1) / MOBA_BLOCK

    qh, ql = _split_bf16(_stack_heads(q_ref[...]), 2)
    kh, kl = _split_bf16(km_ref[...], 2)
    gate = _dot(kh, qh, NT_DIMS) + _dot(kh, ql, NT_DIMS) + _dot(kl, qh, NT_DIMS)
    blk = lax.broadcasted_iota(jnp.int32, (n_blocks, cols), 0)
    tok = qi * tq + (lax.broadcasted_iota(jnp.int32, (n_blocks, cols), 1) & (tq - 1))
    own = lax.shift_right_logical(tok, int(math.log2(MOBA_BLOCK)))
    gate = jnp.where(blk < own, gate, NEG_INF)
    chosen = ((_rank_rows(gate, n_blocks) < MOBA_TOPK) & (blk < own)) | (blk == own)
    pen_ref[...] = jnp.where(chosen, 0.0, NEG_INF)


def moba_gate(proj, k_col, batch, seq):
    tq = ATTN_TILE
    nq = seq // tq
    cols = GROUP * tq
    nb = seq // MOBA_BLOCK
    return pl.pallas_call(
        functools.partial(_moba_gate_body, n_blocks=nb),
        grid=(batch, KV_HEADS, nq),
        in_specs=[pl.BlockSpec((tq, GROUP * HEAD_DIM), lambda b, g, i: (b * nq + i, g)),
                  pl.BlockSpec((seq, HEAD_DIM), lambda b, g, i: (b, k_col + g))],
        out_specs=pl.BlockSpec((None, None, None, nb, cols), lambda b, g, i: (b, g, i, 0, 0)),
        out_shape=jax.ShapeDtypeStruct((batch, KV_HEADS, nq, nb, cols), jnp.float32),
        scratch_shapes=[pltpu.VMEM((nb, HEAD_DIM), jnp.float32)],
        compiler_params=_cparams("parallel", "parallel", "arbitrary"),
        name="moba_gate",
    )(proj, proj)


def _tril_ones(n, strict=False):
    r = lax.broadcasted_iota(jnp.int32, (n, n), 0)
    c = lax.broadcasted_iota(jnp.int32, (n, n), 1)
    return (r > c) if strict else (r >= c)


def _chunk_cumsum(x):
    tril = _bf(_tril_ones(x.shape[0]))
    return sum(_dot(tril, part) for part in _split_bf16(x, 3))


def _head_rms_gate(o, norm_g, gate):
    ms = jnp.mean(o * o, axis=-1, keepdims=True)
    return o * lax.rsqrt(ms + NORM_EPS) * norm_g * (gate * jax.nn.sigmoid(gate))


def _hgrn2_body(q_ref, f_ref, i_ref, g_ref, lbl_ref, ng_ref, s0_ref, o_ref, s_out_ref, st_ref, *, layer):
    ti = pl.program_id(2)

    @pl.when(ti == 0)
    def _():
        for hh in range(HG_HPS):
            st_ref[hh] = s0_ref[hh].T

    lbl = lbl_ref[...]
    e = jnp.exp(lbl - jnp.max(lbl, axis=0, keepdims=True))
    p = e / jnp.sum(e, axis=0, keepdims=True)
    lb_all = jnp.zeros((1, HG_HPS * HG_DK), jnp.float32)
    for r in range(1, layer + 1):
        lb_all = lb_all + p[r:r + 1]
    causal = _tril_ones(CHUNK)
    work = []
    for c in range(q_ref.shape[0] // CHUNK):
        sl = slice(c * CHUNK, (c + 1) * CHUNK)
        for hh in range(HG_HPS):
            cs = slice(hh * HG_DK, (hh + 1) * HG_DK)
            lb = lb_all[:, cs]
            q = q_ref[sl, cs]
            qh = q * jax.nn.sigmoid(q) * HG_DK ** -0.5
            fg = lb + (1.0 - lb) * jax.nn.sigmoid(f_ref[sl, cs])
            k = 1.0 - fg
            v = _bf(i_ref[sl, cs])
            b = _chunk_cumsum(jnp.log(fg))
            b_mid = b[CHUNK // 2:CHUNK // 2 + 1]
            b_last = b[CHUNK - 1:CHUNK]
            a = _dot(_bf(qh * jnp.exp(b - b_mid)), _bf(k * jnp.exp(b_mid - b)), NT_DIMS)
            a = jnp.where(causal, a, 0.0)
            work.append((sl, hh, cs, _dot(_bf(a), v), _bf(qh * jnp.exp(b)), jnp.exp(b_last),
                         _dot(v, _bf(k * jnp.exp(b_last - b)), TN_DIMS)))
    for sl, hh, cs, o_intra, q_in, d_last, kv in work:
        st = st_ref[hh]
        o = o_intra + _dot(q_in, _bf(st), NT_DIMS)
        st_ref[hh] = st * d_last + kv
        o_ref[sl, cs] = _head_rms_gate(o, ng_ref[...], g_ref[sl, cs])

    @pl.when(ti == pl.num_programs(2) - 1)
    def _():
        for hh in range(HG_HPS):
            s_out_ref[hh] = st_ref[hh].T


def hgrn2_scan(proj, lb_logits, norm_g, s0, layer, batch, seq):
    tt = _row_tile(seq, SCAN_TILE)
    nt = seq // tt
    h = HG_HEADS // HG_HPS
    width = HG_HPS * HG_DK

    def col(k):
        return pl.BlockSpec((tt, width), lambda b, hh, t: (b * nt + t, k * h + hh))

    return pl.pallas_call(
        functools.partial(_hgrn2_body, layer=layer),
        grid=(batch, h, nt),
        in_specs=[col(0), col(1), col(2), col(3),
                  pl.BlockSpec((DEPTH, width), lambda b, hh, t: (0, hh)),
                  pl.BlockSpec((1, HG_DV), lambda b, hh, t: (0, 0)),
                  pl.BlockSpec((None, HG_HPS, HG_DK, HG_DV), lambda b, hh, t: (b, hh, 0, 0))],
        out_specs=[pl.BlockSpec((tt, width), lambda b, hh, t: (b * nt + t, hh)),
                   pl.BlockSpec((None, HG_HPS, HG_DK, HG_DV), lambda b, hh, t: (b, hh, 0, 0))],
        out_shape=[jax.ShapeDtypeStruct((batch * seq, HG_HEADS * HG_DV), jnp.float32),
                   jax.ShapeDtypeStruct((batch, HG_HEADS, HG_DK, HG_DV), jnp.float32)],
        scratch_shapes=[pltpu.VMEM((HG_HPS, HG_DV, HG_DK), jnp.float32)],
        compiler_params=_cparams("parallel", "parallel", "arbitrary"),
        name="hgrn2_scan",
    )(proj, proj, proj, proj, lb_logits, norm_g.reshape(1, HG_DV), s0)


def _lane_column(x, lane_idx):
    lane = lax.broadcasted_iota(jnp.int32, x.shape, 1)
    return jnp.sum(jnp.where(lane == lane_idx, x, 0.0), axis=1, keepdims=True)


def _softplus(x):
    return jnp.maximum(x, 0.0) + jnp.log(1.0 + jnp.exp(-jnp.abs(x)))


def _l2n(x):
    return x * lax.rsqrt(jnp.sum(x * x, axis=-1, keepdims=True) + NORM_EPS)


def _gdn_body(q_ref, k_ref, v_ref, z_ref, t_ref, bq_ref, bk_ref, bv_ref, wq_ref, wk_ref, wv_ref,
              al_ref, dtb_ref, ng_ref, s0_ref, o_ref, s_out_ref, xs_ref, y_ref, s_ref):
    hq0 = pl.program_id(1) * GDN_HPS
    n_v = GDN_HPS * GDN_REP
    ti = pl.program_id(2)
    tt = q_ref.shape[0]
    dk, dv = GDN_DK, GDN_DV
    pad = SUBLANE

    @pl.when(ti == 0)
    def _():
        s_ref[...] = s0_ref[...]
        xs_ref[0:pad, :] = jnp.concatenate([bq_ref[...], bk_ref[...], bv_ref[...]], axis=1)

    x = jnp.concatenate([q_ref[...], k_ref[...], v_ref[...]], axis=1)
    xs_ref[pad:, :] = x
    cw = jnp.concatenate([wq_ref[...], wk_ref[...], wv_ref[...]], axis=1)
    y = xs_ref[pad - 3:pad - 3 + tt, :] * cw[0:1]
    for i in range(1, GDN_CONV - 1):
        y = y + xs_ref[pad - 3 + i:pad - 3 + i + tt, :] * cw[i:i + 1]
    y = y + x * cw[GDN_CONV - 1:GDN_CONV]
    xs_ref[0:pad, :] = x[tt - pad:tt]
    y_ref[...] = y * jax.nn.sigmoid(y)

    strict = _tril_ones(CHUNK, strict=True)
    incl = _tril_ones(CHUNK)
    sel_rows = lax.shift_right_logical(lax.broadcasted_iota(jnp.int32, (n_v * CHUNK, LANE), 0), int(math.log2(CHUNK)))
    sel_lane = lax.broadcasted_iota(jnp.int32, (n_v * CHUNK, LANE), 1)
    pick = _bf(sel_lane == GDN_V_HEADS + hq0 * GDN_REP + sel_rows)
    n_chunks = tt // CHUNK

    work = []
    for c in range(n_chunks):
        sl = slice(c * CHUNK, (c + 1) * CHUNK)
        yc = y_ref[sl, :]
        tl = t_ref[sl, :]
        beta_all = jax.nn.sigmoid(tl)
        g_all = _chunk_cumsum(-jnp.exp(al_ref[...]) * _softplus(tl + dtb_ref[...]))
        g_rows = sum(_dot(pick, part, NT_DIMS) for part in _split_bf16(g_all, 3))
        for hh in range(GDN_HPS):
            q = _l2n(yc[:, hh * dk:(hh + 1) * dk]) * dk ** -0.5
            k = _l2n(yc[:, (GDN_HPS + hh) * dk:(GDN_HPS + hh + 1) * dk])
            qb, kb = _bf(q), _bf(k)
            kk = _dot(kb, kb, NT_DIMS)
            qk = _dot(qb, kb, NT_DIMS)
            for e in range(hh * GDN_REP, (hh + 1) * GDN_REP):
                hv = hq0 * GDN_REP + e
                v = yc[:, 2 * GDN_HPS * dk + e * dv:2 * GDN_HPS * dk + (e + 1) * dv]
                bt = _lane_column(beta_all, hv)
                gc = _lane_column(g_all, GDN_V_HEADS + hv)
                gdiff = gc - g_rows[e * CHUNK:(e + 1) * CHUNK]
                decay = jnp.exp(jnp.where(incl, gdiff, 0.0))
                d_strict = jnp.where(strict, decay, 0.0)
                d_incl = jnp.where(incl, decay, 0.0)
                eg = jnp.exp(gc)
                g_last = gc[CHUNK - 1:CHUNK]
                work.append(dict(
                    c=c, e=e, sol=jnp.concatenate([bt * v, (bt * eg) * k], axis=1), pw=bt * kk * d_strict,
                    aq=_bf(qk * d_incl), q_in=_bf(q * eg), k_out=_bf(k * jnp.exp(g_last - gc)),
                    d_last=jnp.exp(g_last)))

    r_i = lax.broadcasted_iota(jnp.int32, (CHUNK, CHUNK), 0)
    c_i = lax.broadcasted_iota(jnp.int32, (CHUNK, CHUNK), 1)
    same = [lax.shift_right_logical(r_i, sh) == lax.shift_right_logical(c_i, sh) for sh in range(3, 7)]
    eye = (r_i == c_i).astype(jnp.float32)
    for wk in work:
        l8 = jnp.where(same[0], wk["pw"], 0.0)
        l8b = _bf(l8)
        wk["t"] = eye - l8
        wk["p"] = _dot(l8b, l8b)
    for wk in work:
        pb = _bf(wk["p"])
        wk["t"] = wk["t"] + _dot(_bf(wk["t"]), pb)
        wk["p"] = _dot(pb, pb)
    for wk in work:
        wk["t"] = wk["t"] + _dot(_bf(wk["t"]), _bf(wk["p"]))
    for lvl in range(1, len(same)):
        for wk in work:
            tb = _bf(wk["t"])
            off = _bf(jnp.where(same[lvl] & jnp.logical_not(same[lvl - 1]), wk["pw"], 0.0))
            wk["t"] = wk["t"] - _dot(tb, _bf(_dot(off, tb)))
    for wk in work:
        wk["sol"] = _dot(_bf(wk["t"]), _bf(wk["sol"]))

    for wk in work:
        c, e = wk["c"], wk["e"]
        sl = slice(c * CHUNK, (c + 1) * CHUNK)
        u0, w = wk["sol"][:, :dv], wk["sol"][:, dv:]
        s = s_ref[e]
        sb = _bf(s)
        u = u0 - _dot(_bf(w), sb)
        o = _dot(wk["q_in"], sb) + _dot(wk["aq"], _bf(u))
        s_ref[e] = wk["d_last"] * s + _dot(wk["k_out"], _bf(u), TN_DIMS)
        o_ref[sl, e * dv:(e + 1) * dv] = _head_rms_gate(o, ng_ref[...], z_ref[sl, e * dv:(e + 1) * dv])

    @pl.when(ti == pl.num_programs(2) - 1)
    def _():
        s_out_ref[...] = s_ref[...]


def gdn_scan(main, tail, conv_buf, conv_w, a_log, dt_bias, norm_g, s0, batch, seq):
    tt = _row_tile(seq, SCAN_TILE)
    nt = seq // tt
    hq, rep = GDN_QK_HEADS // GDN_HPS, GDN_HPS * GDN_REP
    dk, dv = GDN_HPS * GDN_DK, GDN_DV
    vw = rep * dv
    buf = jnp.pad(conv_buf, ((0, 0), (SUBLANE - (GDN_CONV - 1), 0), (0, 0)))
    pad_lanes = jnp.zeros((LANE - 2 * GDN_V_HEADS,), jnp.float32)
    a_row = jnp.concatenate([jnp.zeros((GDN_V_HEADS,), jnp.float32), a_log, pad_lanes]).reshape(1, LANE)
    dt_row = jnp.concatenate([jnp.zeros((GDN_V_HEADS,), jnp.float32), dt_bias, pad_lanes]).reshape(1, LANE)
    k0 = hq
    v0 = 2 * hq * dk // vw
    z0 = GDN_CONV_DIM // vw
    row = lambda b, h, t: b * nt + t
    return pl.pallas_call(
        _gdn_body,
        grid=(batch, hq, nt),
        in_specs=[pl.BlockSpec((tt, dk), lambda b, h, t: (row(b, h, t), h)),
                  pl.BlockSpec((tt, dk), lambda b, h, t: (row(b, h, t), k0 + h)),
                  pl.BlockSpec((tt, vw), lambda b, h, t: (row(b, h, t), v0 + h)),
                  pl.BlockSpec((tt, vw), lambda b, h, t: (row(b, h, t), z0 + h)),
                  pl.BlockSpec((tt, LANE), lambda b, h, t: (row(b, h, t), 0)),
                  pl.BlockSpec((None, SUBLANE, dk), lambda b, h, t: (b, 0, h)),
                  pl.BlockSpec((None, SUBLANE, dk), lambda b, h, t: (b, 0, k0 + h)),
                  pl.BlockSpec((None, SUBLANE, vw), lambda b, h, t: (b, 0, v0 + h)),
                  pl.BlockSpec((GDN_CONV, dk), lambda b, h, t: (0, h)),
                  pl.BlockSpec((GDN_CONV, dk), lambda b, h, t: (0, k0 + h)),
                  pl.BlockSpec((GDN_CONV, vw), lambda b, h, t: (0, v0 + h)),
                  pl.BlockSpec((1, LANE), lambda b, h, t: (0, 0)),
                  pl.BlockSpec((1, LANE), lambda b, h, t: (0, 0)),
                  pl.BlockSpec((1, dv), lambda b, h, t: (0, 0)),
                  pl.BlockSpec((None, rep, GDN_DK, dv), lambda b, h, t: (b, h, 0, 0))],
        out_specs=[pl.BlockSpec((tt, vw), lambda b, h, t: (row(b, h, t), h)),
                   pl.BlockSpec((None, rep, GDN_DK, dv), lambda b, h, t: (b, h, 0, 0))],
        out_shape=[jax.ShapeDtypeStruct((batch * seq, GDN_V_HEADS * dv), jnp.float32),
                   jax.ShapeDtypeStruct((batch, GDN_V_HEADS, GDN_DK, dv), jnp.float32)],
        scratch_shapes=[pltpu.VMEM((tt + SUBLANE, 2 * dk + vw), jnp.float32),
                        pltpu.VMEM((tt, 2 * dk + vw), jnp.float32),
                        pltpu.VMEM((rep, GDN_DK, dv), jnp.float32)],
        compiler_params=_cparams("parallel", "parallel", "arbitrary"),
        name="gdn_scan",
    )(main, main, main, main, tail, buf, buf, buf, conv_w, conv_w, conv_w, a_row, dt_row,
      norm_g.reshape(1, dv), s0)


PAGE = 128
KV_SLABS = 2 * KV_HEADS
PAGE_ROWS = PAGE * KV_SLABS
CMP_PAGES = 8
CHUNKS_PER_PAGE = PAGE // CMP_STRIDE
ROWS = GROUP * 4


def _page_slab(pg, slab):
    return pg[pl.ds(slab, PAGE, stride=KV_SLABS), :]


def _compress_part_body(*refs, n_prefetch, paged):
    refs = refs[n_prefetch:]
    pages, w_ref, o_ref, xs_ref = refs[:CMP_PAGES], refs[CMP_PAGES], refs[CMP_PAGES + 1], refs[CMP_PAGES + 2]
    per_g = CMP_PAGES * CHUNKS_PER_PAGE
    for k, pg in enumerate(pages):
        for cg in range(KV_SLABS):
            xs_ref[k, cg] = _page_slab(pg, cg) if paged else pg[:, cg * HEAD_DIM:(cg + 1) * HEAD_DIM]
    def token_rows(c, l):
        return jnp.concatenate([xs_ref.at[k, c * KV_HEADS + g][pl.ds(l, CHUNKS_PER_PAGE, stride=CMP_STRIDE), :]
                                for g in range(KV_HEADS) for k in range(CMP_PAGES)], axis=0)

    for c in range(2):
        acc = None
        for lp in range(CMP_STRIDE // 2):
            x = jnp.concatenate([token_rows(c, 2 * lp), token_rows(c, 2 * lp + 1)], axis=1)
            d = _dot(_bf(x), w_ref[c, lp])
            acc = d if acc is None else acc + d
        for g in range(KV_HEADS):
            o_ref[c, g] = acc[g * per_g:(g + 1) * per_g]


def _compress_weights(cmp_w1):
    n_part = CMP_BLOCK // CMP_STRIDE
    w = cmp_w1.reshape(2, n_part, CMP_STRIDE, HEAD_DIM, CMP_HIDDEN).transpose(0, 2, 3, 1, 4)
    return w.reshape(2, CMP_STRIDE // 2, 2 * HEAD_DIM, n_part * CMP_HIDDEN).astype(jnp.bfloat16)


def compress_part_rows(kv_arr, col_block, cmp_w1, batch, seq):
    w = _compress_weights(cmp_w1)
    pages_per_b = seq // PAGE
    steps = pages_per_b // CMP_PAGES
    per_g = CMP_PAGES * CHUNKS_PER_PAGE

    def page_spec(k):
        return pl.BlockSpec((PAGE, KV_WIDTH), lambda b, s: (b * pages_per_b + s * CMP_PAGES + k, col_block))

    return pl.pallas_call(
        functools.partial(_compress_part_body, n_prefetch=0, paged=False),
        grid=(batch, steps),
        in_specs=[page_spec(k) for k in range(CMP_PAGES)] + [pl.BlockSpec(w.shape, lambda b, s: (0, 0, 0, 0))],
        out_specs=pl.BlockSpec((None, 2, KV_HEADS, per_g, w.shape[-1]), lambda b, s: (b, 0, 0, s, 0)),
        out_shape=jax.ShapeDtypeStruct((batch, 2, KV_HEADS, seq // CMP_STRIDE, w.shape[-1]), jnp.float32),
        scratch_shapes=[pltpu.VMEM((CMP_PAGES, 2 * KV_HEADS, PAGE, HEAD_DIM), jnp.float32)],
        compiler_params=_cparams("parallel", "arbitrary"),
        name="compress_part_rows",
    )(*([kv_arr] * CMP_PAGES), w)


def compress_part_paged(pool, page_table, cmp_w1):
    w = _compress_weights(cmp_w1)
    batch, n_pages = page_table.shape
    steps = n_pages // CMP_PAGES
    per_g = CMP_PAGES * CHUNKS_PER_PAGE

    def page_spec(k):
        return pl.BlockSpec((None, PAGE_ROWS, HEAD_DIM), lambda b, s, pt: (pt[b, s * CMP_PAGES + k], 0, 0))

    return pl.pallas_call(
        functools.partial(_compress_part_body, n_prefetch=1, paged=True),
        grid_spec=pltpu.PrefetchScalarGridSpec(
            num_scalar_prefetch=1, grid=(batch, steps),
            in_specs=[page_spec(k) for k in range(CMP_PAGES)]
            + [pl.BlockSpec(w.shape, lambda b, s, pt: (0, 0, 0, 0))],
            out_specs=pl.BlockSpec((None, 2, KV_HEADS, per_g, w.shape[-1]), lambda b, s, pt: (b, 0, 0, s, 0)),
            scratch_shapes=[pltpu.VMEM((CMP_PAGES, 2 * KV_HEADS, PAGE, HEAD_DIM), jnp.float32)]),
        out_shape=jax.ShapeDtypeStruct((batch, 2, KV_HEADS, n_pages * CHUNKS_PER_PAGE, w.shape[-1]), jnp.float32),
        compiler_params=_cparams("parallel", "arbitrary"),
        name="compress_part_paged",
    )(page_table, *([pool] * CMP_PAGES), w)


def _gelu_tanh(x):
    return x * (0.5 * (1.0 + jnp.tanh(math.sqrt(2.0 / math.pi) * (x + 0.044715 * (x * x * x)))))


def _compress_finish_body(p_ref, peh_ref, w2_ref, o_ref):
    n = p_ref.shape[1]
    for c in range(2):
        p = p_ref[c]
        hid = peh_ref[c:c + 1, :] + p[:, :CMP_HIDDEN]
        hid = hid + pltpu.roll(p[:, CMP_HIDDEN:], n - 1, 0)
        o_ref[c] = _dot(_bf(_gelu_tanh(hid)), _bf(w2_ref[c]))


def compress_finish(part, cmp_w1, cmp_w2, cmp_pe):
    batch, _, _, n, width = part.shape
    pe_hid = jnp.einsum('cld,cldh->ch', cmp_pe, cmp_w1)
    return pl.pallas_call(
        _compress_finish_body,
        grid=(batch, KV_HEADS),
        in_specs=[pl.BlockSpec((None, 2, None, n, width), lambda b, g: (b, 0, g, 0, 0)),
                  pl.BlockSpec((2, CMP_HIDDEN), lambda b, g: (0, 0)),
                  pl.BlockSpec((2, CMP_HIDDEN, HEAD_DIM), lambda b, g: (0, 0, 0))],
        out_specs=pl.BlockSpec((None, None, 2, n, HEAD_DIM), lambda b, g: (b, g, 0, 0, 0)),
        out_shape=jax.ShapeDtypeStruct((batch, KV_HEADS, 2, n, HEAD_DIM), jnp.float32),
        compiler_params=_cparams("parallel", "parallel"),
        name="compress_finish",
    )(part, pe_hid, cmp_w2)


def _rows_to_col(row, n):
    eye = lax.broadcasted_iota(jnp.int32, (n, n), 0) == lax.broadcasted_iota(jnp.int32, (n, n), 1)
    return jnp.sum(jnp.where(eye, jnp.broadcast_to(row, (n, n)), 0.0), axis=1, keepdims=True)


def _sample_bias_tiles(rel_table, past, n_new):
    j = np.arange(PAGE)[:, None]
    t = np.arange(n_new)[None, :]
    far = np.full((PAGE, n_new), REL_MAX_DIST)
    first = WINDOW + t - j
    last = PAGE + t - j
    new = t - j
    tiles = []
    for dist, ok in ((far, far > 0), (first, first < WINDOW), (last, last > 0), (new, (new >= 0) & (j < n_new))):
        b = jnp.where(ok[..., None], _bias_lookup(rel_table, dist), NEG_INF)
        b = b.reshape(PAGE, n_new, KV_HEADS, GROUP).transpose(2, 0, 3, 1).reshape(KV_HEADS, PAGE, GROUP * n_new)
        tiles.append(b)
    return jnp.stack(tiles, axis=1)


def _sample_cmp_body(q_ref, kc_ref, bias_ref, gt_ref, o_ref, pen_ref, *, n_sel_blocks, past):
    n = kc_ref.shape[1]
    nbp = pen_ref.shape[0]
    q = _bf(q_ref[...] * ATTN_SCALE)
    s = _dot(_bf(kc_ref[0]), q, NT_DIMS) + bias_ref[...]
    m = jnp.max(s, axis=0, keepdims=True)
    e = jnp.exp(s - m)
    p = e / jnp.maximum(jnp.sum(e, axis=0, keepdims=True), 1e-30)
    o = _dot(_bf(p), _bf(kc_ref[1]), TN_DIMS)
    o_ref[...] = o * jax.nn.sigmoid(gt_ref[...])

    r_i = lax.broadcasted_iota(jnp.int32, (ROWS, ROWS), 0)
    c_i = lax.broadcasted_iota(jnp.int32, (ROWS, ROWS), 1)
    n_tok = ROWS // GROUP
    same_tok = _bf((r_i & (n_tok - 1)) == (c_i & (n_tok - 1)))
    ratio = SEL_BLOCK // CMP_STRIDE
    j_i = lax.broadcasted_iota(jnp.int32, (nbp, n), 0)
    k_i = lax.broadcasted_iota(jnp.int32, (nbp, n), 1)
    w = _bf((k_i >= ratio * j_i - 1) & (k_i <= ratio * j_i + ratio - 1) & (j_i < n_sel_blocks))
    imp = sum(_dot(part, same_tok) for part in _split_bf16(p, 3))
    score = sum(_dot(w, part) for part in _split_bf16(imp, 3))
    blk = lax.broadcasted_iota(jnp.int32, (nbp, ROWS), 0)
    tok = past + (lax.broadcasted_iota(jnp.int32, (nbp, ROWS), 1) & (n_tok - 1))
    cur = lax.shift_right_logical(tok, int(math.log2(SEL_BLOCK)))
    forced = (blk == 0) | (blk == cur) | (blk == cur - 1)
    causal = blk <= cur
    score = jnp.where(forced, FORCE_SCORE, score)
    score = jnp.where(causal, score, -1.0)
    nbl = -(-nbp // LANE) * LANE
    n_idx = lax.broadcasted_iota(jnp.int32, (nbp, nbl), 0)
    m_idx = lax.broadcasted_iota(jnp.int32, (nbp, nbl), 1)
    lane_tok = lax.broadcasted_iota(jnp.int32, (nbp, ROWS), 1) & (n_tok - 1)
    rank = jnp.zeros((nbp, ROWS), jnp.float32)
    for t in range(n_tok):
        s_col = score[:, t:t + 1]
        s_row = jnp.sum(jnp.where(n_idx == m_idx, s_col, 0.0), axis=0, keepdims=True)
        s_row = jnp.where(m_idx[0:1] < n_sel_blocks, s_row, -2.0)
        ahead = (s_row > s_col) | ((s_row == s_col) & (m_idx < n_idx))
        rank_t = jnp.sum(ahead.astype(jnp.float32), axis=1, keepdims=True)
        rank = jnp.where(lane_tok == t, rank_t, rank)
    chosen = (rank < N_SEL) & causal & (blk < n_sel_blocks)
    pen_ref[...] = jnp.where(chosen, 0.0, NEG_INF)


def sample_cmp_select(qs, kc, bias, gate, past, n_sel_blocks):
    batch = qs.shape[0]
    n = kc.shape[3]
    nbp = -(-n_sel_blocks // SUBLANE) * SUBLANE
    return pl.pallas_call(
        functools.partial(_sample_cmp_body, n_sel_blocks=n_sel_blocks, past=past),
        grid=(batch, KV_HEADS),
        in_specs=[pl.BlockSpec((None, None, ROWS, HEAD_DIM), lambda b, g: (b, g, 0, 0)),
                  pl.BlockSpec((None, None, 2, n, HEAD_DIM), lambda b, g: (b, g, 0, 0, 0)),
                  pl.BlockSpec((None, n, ROWS), lambda b, g: (g, 0, 0)),
                  pl.BlockSpec((None, None, ROWS, HEAD_DIM), lambda b, g: (b, g, 0, 0))],
        out_specs=[pl.BlockSpec((None, None, ROWS, HEAD_DIM), lambda b, g: (b, g, 0, 0)),
                   pl.BlockSpec((None, None, nbp, ROWS), lambda b, g: (b, g, 0, 0))],
        out_shape=[jax.ShapeDtypeStruct((batch, KV_HEADS, ROWS, HEAD_DIM), jnp.float32),
                   jax.ShapeDtypeStruct((batch, KV_HEADS, nbp, ROWS), jnp.float32)],
        compiler_params=_cparams("parallel", "parallel"),
        name="sample_cmp_select",
    )(qs, kc, bias, gate)


ATTN_PAGES = 4


def _paged_attn_body(*refs, pen_block, gated, n_pages):
    it = iter(refs)
    pt_ref, tid_ref, q_ref = next(it), next(it), next(it)
    pages = [next(it) for _ in range(ATTN_PAGES)]
    new_ref, bt_ref = next(it), next(it)
    pen_ref = next(it) if pen_block else None
    gt_ref = next(it) if gated else None
    o_ref, m_ref, l_ref, acc_ref = next(it), next(it), next(it), next(it)
    step = pl.program_id(1)

    @pl.when(step == 0)
    def _():
        m_ref[...] = jnp.full(m_ref.shape, NEG_INF, jnp.float32)
        l_ref[...] = jnp.zeros(l_ref.shape, jnp.float32)
        acc_ref[...] = jnp.zeros(acc_ref.shape, jnp.float32)

    def page_terms(g, page, n_keys):
        tile = tid_ref[page]
        b = bt_ref[g, tile] if n_keys == PAGE else bt_ref[g, tile, 0:n_keys, :]
        if not pen_block:
            return b
        if pen_block >= PAGE:
            return b + pen_ref[g, pl.ds(page // (pen_block // PAGE), 1), :]
        per_page = PAGE // pen_block
        pieces = [b[a * pen_block:min((a + 1) * pen_block, n_keys)] + pen_ref[g, pl.ds(page * per_page + a, 1), :]
                  for a in range(-(-n_keys // pen_block))]
        return pieces[0] if len(pieces) == 1 else jnp.concatenate(pieces, axis=0)

    def attend(g, k, v, terms):
        s = _dot(_bf(k), _bf(q_ref[g] * ATTN_SCALE), NT_DIMS) + terms
        m_prev = m_ref[g]
        m_new = jnp.maximum(m_prev, jnp.max(s, axis=0, keepdims=True))
        alpha = jnp.exp(m_prev - m_new)
        p = jnp.exp(s - m_new)
        l_ref[g] = alpha * l_ref[g] + jnp.sum(p, axis=0, keepdims=True)
        acc_ref[g] = _rows_to_col(alpha, ROWS) * acc_ref[g] + _dot(_bf(p), _bf(v), TN_DIMS)
        m_ref[g] = m_new

    half = KV_HEADS * HEAD_DIM
    for kk in range(ATTN_PAGES):
        for g in range(KV_HEADS):
            attend(g, _page_slab(pages[kk], g), _page_slab(pages[kk], KV_HEADS + g),
                   page_terms(g, step * ATTN_PAGES + kk, PAGE))

    @pl.when(step == pl.num_programs(1) - 1)
    def _():
        n_new = new_ref.shape[0]
        for g in range(KV_HEADS):
            attend(g, new_ref[:, g * HEAD_DIM:(g + 1) * HEAD_DIM],
                   new_ref[:, half + g * HEAD_DIM:half + (g + 1) * HEAD_DIM], page_terms(g, n_pages, n_new))
            o = acc_ref[g] / _rows_to_col(jnp.maximum(l_ref[g], 1e-30), ROWS)
            if gated:
                o = o * jax.nn.sigmoid(gt_ref[g])
            o_ref[g] = o


def paged_attention(qs, pool, page_table, tile_ids, new_kv, bias_tiles, *, pen=None, pen_block=0, gate=None):
    batch, n_pages = page_table.shape
    steps = n_pages // ATTN_PAGES
    n_new = new_kv.shape[1]

    def page_spec(k):
        return pl.BlockSpec((None, PAGE_ROWS, HEAD_DIM), lambda b, s, pt, tid: (pt[b, s * ATTN_PAGES + k], 0, 0))

    grp = lambda b, s, pt, tid: (b, 0, 0, 0)
    in_specs = ([pl.BlockSpec((None, KV_HEADS, ROWS, HEAD_DIM), grp)] + [page_spec(k) for k in range(ATTN_PAGES)]
                + [pl.BlockSpec((None, n_new, KV_WIDTH), lambda b, s, pt, tid: (b, 0, 0)),
                   pl.BlockSpec(bias_tiles.shape, lambda b, s, pt, tid: (0, 0, 0, 0))])
    args = [qs] + [pool] * ATTN_PAGES + [new_kv, bias_tiles]
    if pen is not None:
        in_specs.append(pl.BlockSpec((None,) + pen.shape[1:], grp))
        args.append(pen)
    if gate is not None:
        in_specs.append(pl.BlockSpec((None, KV_HEADS, ROWS, HEAD_DIM), grp))
        args.append(gate)
    return pl.pallas_call(
        functools.partial(_paged_attn_body, pen_block=pen_block if pen is not None else 0, gated=gate is not None,
                          n_pages=n_pages),
        grid_spec=pltpu.PrefetchScalarGridSpec(
            num_scalar_prefetch=2, grid=(batch, steps), in_specs=in_specs,
            out_specs=pl.BlockSpec((None, KV_HEADS, ROWS, HEAD_DIM), grp),
            scratch_shapes=[pltpu.VMEM((KV_HEADS, 1, ROWS), jnp.float32), pltpu.VMEM((KV_HEADS, 1, ROWS), jnp.float32),
                            pltpu.VMEM((KV_HEADS, ROWS, HEAD_DIM), jnp.float32)]),
        out_shape=jax.ShapeDtypeStruct((batch, KV_HEADS, ROWS, HEAD_DIM), jnp.float32),
        compiler_params=_cparams("parallel", "arbitrary"),
        name="paged_attention",
    )(page_table, tile_ids, *args)


def _moba_sample_gate_body(pt_ref, q_ref, *refs, n_blocks, past):
    pages, (pen_ref, km_ref) = refs[:ATTN_PAGES], refs[ATTN_PAGES:]
    step = pl.program_id(1)
    per_block = MOBA_BLOCK // PAGE

    @pl.when(step == 0)
    def _():
        km_ref[...] = jnp.zeros(km_ref.shape, jnp.float32)

    for kk in range(ATTN_PAGES):
        blk = (step * ATTN_PAGES + kk) // per_block
        slab_sums = jnp.sum(pages[kk][...], axis=0)
        for g in range(KV_HEADS):
            km_ref[g, pl.ds(blk, 1), :] += slab_sums[g:g + 1]

    @pl.when(step == pl.num_programs(1) - 1)
    def _():
        nbp = pen_ref.shape[1]
        blk = lax.broadcasted_iota(jnp.int32, (nbp, ROWS), 0)
        n_tok = ROWS // GROUP
        tok = past + (lax.broadcasted_iota(jnp.int32, (nbp, ROWS), 1) & (n_tok - 1))
        own = lax.shift_right_logical(tok, int(math.log2(MOBA_BLOCK)))
        for g in range(KV_HEADS):
            kh, kl = _split_bf16(km_ref[g] / MOBA_BLOCK, 2)
            qh, ql = _split_bf16(q_ref[g], 2)
            gate = _dot(kh, qh, NT_DIMS) + _dot(kh, ql, NT_DIMS) + _dot(kl, qh, NT_DIMS)
            gate = jnp.where(blk < own, gate, NEG_INF)
            chosen = ((_rank_rows(gate, n_blocks) < MOBA_TOPK) & (blk < own)) | (blk == own)
            pen_ref[g] = jnp.where(chosen, 0.0, NEG_INF)


def moba_sample_gate(qs, pool, page_table, past, n_new):
    batch, n_pages = page_table.shape
    steps = n_pages // ATTN_PAGES
    n_blocks = -(-(past + n_new) // MOBA_BLOCK)
    nbp = -(-n_blocks // SUBLANE) * SUBLANE

    def page_spec(k):
        return pl.BlockSpec((None, PAGE, None, KV_HEADS, HEAD_DIM),
                            lambda b, s, pt: (pt[b, s * ATTN_PAGES + k], 0, 0, 0, 0))

    return pl.pallas_call(
        functools.partial(_moba_sample_gate_body, n_blocks=n_blocks, past=past),
        grid_spec=pltpu.PrefetchScalarGridSpec(
            num_scalar_prefetch=1, grid=(batch, steps),
            in_specs=[pl.BlockSpec((None, KV_HEADS, ROWS, HEAD_DIM), lambda b, s, pt: (b, 0, 0, 0))]
            + [page_spec(k) for k in range(ATTN_PAGES)],
            out_specs=pl.BlockSpec((None, KV_HEADS, nbp, ROWS), lambda b, s, pt: (b, 0, 0, 0)),
            scratch_shapes=[pltpu.VMEM((KV_HEADS, nbp, HEAD_DIM), jnp.float32)]),
        out_shape=jax.ShapeDtypeStruct((batch, KV_HEADS, nbp, ROWS), jnp.float32),
        compiler_params=_cparams("parallel", "arbitrary"),
        name="moba_sample_gate",
    )(page_table, qs, *([pool] * ATTN_PAGES))


def _sample_rows(x, batch, n_tok):
    return x.reshape(batch, n_tok, KV_HEADS, GROUP, HEAD_DIM).transpose(0, 2, 3, 1, 4).reshape(
        batch, KV_HEADS, GROUP * n_tok, HEAD_DIM)


def _sample_unrows(o, batch, n_tok):
    return o.reshape(batch, KV_HEADS, GROUP, n_tok, HEAD_DIM).transpose(0, 3, 1, 2, 4).reshape(batch * n_tok, Q_WIDTH)


def _sample_gate_rows(tail, branch, batch, n_tok):
    gt = tail[:, branch * N_HEADS:(branch + 1) * N_HEADS].reshape(batch, n_tok, KV_HEADS, GROUP)
    gt = gt.transpose(0, 2, 3, 1).reshape(batch, KV_HEADS, GROUP * n_tok, 1)
    return jnp.broadcast_to(gt, (batch, KV_HEADS, GROUP * n_tok, HEAD_DIM))


def _pad_new(kv_new, batch, n_tok):
    return jnp.pad(kv_new.reshape(batch, n_tok, KV_WIDTH), ((0, 0), (0, SUBLANE - n_tok), (0, 0)))


def _nsa_sample_pallas(main, tail, cache_c, cache_s, cache_w, page_table, cmp_w1, cmp_w2, cmp_pe, rel_table):
    batch, n_pages = page_table.shape
    n_tok = main.shape[0] // batch
    past = n_pages * PAGE
    wbuf = cache_w.shape[1]
    assert n_tok == ROWS // GROUP and cache_c.shape[1] == PAGE and wbuf == WINDOW and WINDOW % PAGE == 0
    n_cmp = (past + n_tok - CMP_BLOCK) // CMP_STRIDE + 1
    assert n_cmp + CMP_BLOCK // CMP_STRIDE - 1 == past // CMP_STRIDE
    qs = _sample_rows(main[:, :Q_WIDTH], batch, n_tok)
    kv_new = [main[:, Q_WIDTH + c * KV_WIDTH:Q_WIDTH + (c + 1) * KV_WIDTH] for c in range(3)]
    tiles = _sample_bias_tiles(rel_table, past, n_tok)
    flat = lambda pool: pool.reshape(pool.shape[0], PAGE_ROWS, HEAD_DIM)

    kc = compress_finish(compress_part_paged(flat(cache_c), page_table, cmp_w1), cmp_w1, cmp_w2, cmp_pe)
    n = kc.shape[3]
    dist = past + np.arange(n_tok)[None, :] - (np.arange(n)[:, None] * CMP_STRIDE + CMP_BLOCK - 1)
    ok = (dist >= 0) & (np.arange(n)[:, None] < n_cmp)
    cb = jnp.where(ok[..., None], _bias_lookup(rel_table, dist), NEG_INF)
    cb = cb.reshape(n, n_tok, KV_HEADS, GROUP).transpose(2, 0, 3, 1).reshape(KV_HEADS, n, GROUP * n_tok)
    n_sel_blocks = -(-(past + n_tok) // SEL_BLOCK)
    o_cmp, pen = sample_cmp_select(qs, kc, cb, _sample_gate_rows(tail, 0, batch, n_tok), past, n_sel_blocks)

    far_then_last = jnp.asarray([0] * (n_pages - 1) + [2, 3], jnp.int32)
    o_sel = paged_attention(qs, flat(cache_s), page_table, far_then_last, _pad_new(kv_new[1], batch, n_tok), tiles,
                            pen=pen, pen_block=SEL_BLOCK, gate=_sample_gate_rows(tail, 1, batch, n_tok))
    w_pages = wbuf // PAGE
    win_table = jnp.arange(batch * w_pages, dtype=jnp.int32).reshape(batch, w_pages)
    win_tiles = jnp.asarray([1] + [0] * (w_pages - 2) + [2, 3], jnp.int32)
    o_win = paged_attention(qs, cache_w.reshape(batch * w_pages, PAGE_ROWS, HEAD_DIM), win_table, win_tiles,
                            _pad_new(kv_new[2], batch, n_tok), tiles, gate=_sample_gate_rows(tail, 2, batch, n_tok))
    outs = [_sample_unrows(o, batch, n_tok) for o in (o_cmp, o_sel, o_win)]
    shape = (batch, n_tok, 2, KV_HEADS, HEAD_DIM)
    new_win = jnp.concatenate([cache_w[:, n_tok:], kv_new[2].reshape(shape)], axis=1)
    return outs, kv_new[0].reshape(shape), kv_new[1].reshape(shape), new_win


def _moba_sample_pallas(proj, cache_kv, page_table, rel_table):
    batch, n_pages = page_table.shape
    n_tok = proj.shape[0] // batch
    past = n_pages * PAGE
    assert n_tok == ROWS // GROUP and (past // MOBA_BLOCK) * MOBA_BLOCK == past
    qs = _sample_rows(proj[:, :Q_WIDTH], batch, n_tok)
    kv_new = proj[:, Q_WIDTH:]
    pool = cache_kv.reshape(cache_kv.shape[0], PAGE_ROWS, HEAD_DIM)
    pen = moba_sample_gate(qs, cache_kv, page_table, past, n_tok)
    tile_ids = jnp.asarray([0] * (n_pages - 1) + [2, 3], jnp.int32)
    o = paged_attention(qs, pool, page_table, tile_ids, _pad_new(kv_new, batch, n_tok),
                        _sample_bias_tiles(rel_table, past, n_tok), pen=pen, pen_block=MOBA_BLOCK)
    return _sample_unrows(o, batch, n_tok), kv_new.reshape(batch, n_tok, 2, KV_HEADS, HEAD_DIM)


def _cols_to_row(col, n):
    eye = lax.broadcasted_iota(jnp.int32, (n, n), 0) == lax.broadcasted_iota(jnp.int32, (n, n), 1)
    return jnp.sum(jnp.where(eye, jnp.broadcast_to(col, (n, n)), 0.0), axis=0, keepdims=True)


def _hgrn2_step_body(q_ref, f_ref, i_ref, g_ref, lbl_ref, ng_ref, s0_ref, o_ref, s_out_ref, *, layer):
    n_tok = q_ref.shape[0]
    lbl = lbl_ref[...]
    e = jnp.exp(lbl - jnp.max(lbl, axis=0, keepdims=True))
    p = e / jnp.sum(e, axis=0, keepdims=True)
    lb = jnp.zeros((1, HG_DK), jnp.float32)
    for r in range(1, layer + 1):
        lb = lb + p[r:r + 1]
    q = q_ref[...]
    qh = q * jax.nn.sigmoid(q) * HG_DK ** -0.5
    fg = lb + (1.0 - lb) * jax.nn.sigmoid(f_ref[...])
    k = 1.0 - fg
    v = i_ref[...]
    st = s0_ref[...].T
    rows = []
    for t in range(n_tok):
        st = st * fg[t:t + 1] + _rows_to_col(v[t:t + 1], HG_DV) * k[t:t + 1]
        rows.append(_cols_to_row(jnp.sum(st * qh[t:t + 1], axis=1, keepdims=True), HG_DV))
    o_ref[...] = _head_rms_gate(jnp.concatenate(rows, axis=0), ng_ref[...], g_ref[...])
    s_out_ref[...] = st.T


def hgrn2_step(proj, lb_logits, norm_g, s0, layer):
    batch, n_tok, _ = proj.shape
    h = HG_HEADS

    def col(k):
        return pl.BlockSpec((None, n_tok, HG_DK), lambda b, hh: (b, 0, k * h + hh))

    return pl.pallas_call(
        functools.partial(_hgrn2_step_body, layer=layer),
        grid=(batch, h),
        in_specs=[col(0), col(1), col(2), col(3),
                  pl.BlockSpec((DEPTH, HG_DK), lambda b, hh: (0, hh)),
                  pl.BlockSpec((1, HG_DV), lambda b, hh: (0, 0)),
                  pl.BlockSpec((None, None, HG_DK, HG_DV), lambda b, hh: (b, hh, 0, 0))],
        out_specs=[pl.BlockSpec((None, n_tok, HG_DV), lambda b, hh: (b, 0, hh)),
                   pl.BlockSpec((None, None, HG_DK, HG_DV), lambda b, hh: (b, hh, 0, 0))],
        out_shape=[jax.ShapeDtypeStruct((batch, n_tok, h * HG_DV), jnp.float32),
                   jax.ShapeDtypeStruct((batch, h, HG_DK, HG_DV), jnp.float32)],
        compiler_params=_cparams("parallel", "parallel"),
        name="hgrn2_step",
    )(proj, proj, proj, proj, lb_logits, norm_g.reshape(1, HG_DV), s0)


def _gdn_step_body(q_ref, k_ref, v_ref, z_ref, t_ref, bq_ref, bk_ref, bv_ref, wq_ref, wk_ref, wv_ref,
                   al_ref, dtb_ref, ng_ref, s0_ref, o_ref, s_out_ref, xs_ref):
    hq = pl.program_id(1)
    n_tok = q_ref.shape[0]
    dk, dv = GDN_DK, GDN_DV
    pad = SUBLANE
    xs_ref[0:pad, :] = jnp.concatenate([bq_ref[...], bk_ref[...], bv_ref[...]], axis=1)
    x = jnp.concatenate([q_ref[...], k_ref[...], v_ref[...]], axis=1)
    xs_ref[pad:pad + n_tok, :] = x
    cw = jnp.concatenate([wq_ref[...], wk_ref[...], wv_ref[...]], axis=1)
    y = xs_ref[pad - 3:pad - 3 + n_tok, :] * cw[0:1]
    for i in range(1, GDN_CONV - 1):
        y = y + xs_ref[pad - 3 + i:pad - 3 + i + n_tok, :] * cw[i:i + 1]
    y = y + x * cw[GDN_CONV - 1:GDN_CONV]
    y = y * jax.nn.sigmoid(y)
    q = _l2n(y[:, 0:dk]) * dk ** -0.5
    k = _l2n(y[:, dk:2 * dk])
    tl = t_ref[...]
    beta_all = jax.nn.sigmoid(tl)
    la_all = -jnp.exp(al_ref[...]) * _softplus(tl + dtb_ref[...])
    for e in range(GDN_REP):
        hv = hq * GDN_REP + e
        v = y[:, 2 * dk + e * dv:2 * dk + (e + 1) * dv]
        bt = _lane_column(beta_all, hv)
        a = jnp.exp(_lane_column(la_all, GDN_V_HEADS + hv))
        s = s0_ref[e]
        rows = []
        for t in range(n_tok):
            k_col = _rows_to_col(k[t:t + 1], dk)
            ks = jnp.sum(s * k_col, axis=0, keepdims=True)
            u = bt[t:t + 1] * (v[t:t + 1] - a[t:t + 1] * ks)
            s = a[t:t + 1] * s + k_col * u
            rows.append(jnp.sum(s * _rows_to_col(q[t:t + 1], dk), axis=0, keepdims=True))
        s_out_ref[e] = s
        o_ref[:, e * dv:(e + 1) * dv] = _head_rms_gate(jnp.concatenate(rows, axis=0), ng_ref[...],
                                                       z_ref[:, e * dv:(e + 1) * dv])


def gdn_step(main, tail, conv_buf, conv_w, a_log, dt_bias, norm_g, s0):
    batch, n_tok, _ = main.shape
    hq, rep, dk, dv = GDN_QK_HEADS, GDN_REP, GDN_DK, GDN_DV
    vw = rep * dv
    buf = jnp.pad(conv_buf, ((0, 0), (SUBLANE - (GDN_CONV - 1), 0), (0, 0)))
    pad_lanes = jnp.zeros((LANE - 2 * GDN_V_HEADS,), jnp.float32)
    a_row = jnp.concatenate([jnp.zeros((GDN_V_HEADS,), jnp.float32), a_log, pad_lanes]).reshape(1, LANE)
    dt_row = jnp.concatenate([jnp.zeros((GDN_V_HEADS,), jnp.float32), dt_bias, pad_lanes]).reshape(1, LANE)
    k0 = hq
    v0 = 2 * hq * dk // vw
    z0 = GDN_CONV_DIM // vw
    return pl.pallas_call(
        _gdn_step_body,
        grid=(batch, hq),
        in_specs=[pl.BlockSpec((None, n_tok, dk), lambda b, h: (b, 0, h)),
                  pl.BlockSpec((None, n_tok, dk), lambda b, h: (b, 0, k0 + h)),
                  pl.BlockSpec((None, n_tok, vw), lambda b, h: (b, 0, v0 + h)),
                  pl.BlockSpec((None, n_tok, vw), lambda b, h: (b, 0, z0 + h)),
                  pl.BlockSpec((None, n_tok, LANE), lambda b, h: (b, 0, 0)),
                  pl.BlockSpec((None, SUBLANE, dk), lambda b, h: (b, 0, h)),
                  pl.BlockSpec((None, SUBLANE, dk), lambda b, h: (b, 0, k0 + h)),
                  pl.BlockSpec((None, SUBLANE, vw), lambda b, h: (b, 0, v0 + h)),
                  pl.BlockSpec((GDN_CONV, dk), lambda b, h: (0, h)),
                  pl.BlockSpec((GDN_CONV, dk), lambda b, h: (0, k0 + h)),
                  pl.BlockSpec((GDN_CONV, vw), lambda b, h: (0, v0 + h)),
                  pl.BlockSpec((1, LANE), lambda b, h: (0, 0)),
                  pl.BlockSpec((1, LANE), lambda b, h: (0, 0)),
                  pl.BlockSpec((1, dv), lambda b, h: (0, 0)),
                  pl.BlockSpec((None, rep, dk, dv), lambda b, h: (b, h, 0, 0))],
        out_specs=[pl.BlockSpec((None, n_tok, vw), lambda b, h: (b, 0, h)),
                   pl.BlockSpec((None, rep, dk, dv), lambda b, h: (b, h, 0, 0))],
        out_shape=[jax.ShapeDtypeStruct((batch, n_tok, GDN_V_HEADS * dv), jnp.float32),
                   jax.ShapeDtypeStruct((batch, GDN_V_HEADS, dk, dv), jnp.float32)],
        scratch_shapes=[pltpu.VMEM((2 * SUBLANE, 2 * dk + vw), jnp.float32)],
        compiler_params=_cparams("parallel", "parallel"),
        name="gdn_step",
    )(main, main, main, main, tail, buf, buf, buf, conv_w, conv_w, conv_w, a_row, dt_row, norm_g.reshape(1, dv), s0)


def _pad_cols(w, mult=LANE):
    n = w.shape[1]
    return jnp.pad(w, ((0, 0), (0, (-n) % mult)))


def _nsa_prompt(main, kv_rows, tail, batch, seq, cmp_w1, cmp_w2, cmp_pe, rel_table):
    kv_c, kv_s, kv_w = (kv_rows[c].reshape(batch, seq, 2, KV_HEADS, HEAD_DIM) for c in range(3))
    kc = compress_finish(compress_part_rows(main, Q_WIDTH // KV_WIDTH, cmp_w1, batch, seq), cmp_w1, cmp_w2, cmp_pe)
    o_cmp, pen = cmp_select(main, tail, kc, _cmp_bias_table(rel_table, kc.shape[3]), batch, seq)
    col = Q_WIDTH // HEAD_DIM
    o_sel = flash_attention(main, main, col + 2 * KV_HEADS, col + 3 * KV_HEADS, _flash_bias_tiles(rel_table, 0),
                            batch, seq, pen=pen, pen_block=SEL_BLOCK, gate_arr=tail, gate_col0=N_HEADS)
    o_win = flash_attention(main, main, col + 4 * KV_HEADS, col + 5 * KV_HEADS,
                            _flash_bias_tiles(rel_table, WINDOW), batch, seq, k_back=WINDOW // ATTN_TILE,
                            gate_arr=tail, gate_col0=2 * N_HEADS)
    return [o_cmp, o_sel, o_win], kv_c, kv_s, kv_w[:, -min(WINDOW, seq):]


def _moba_prompt(proj, batch, seq, rel_table):
    col = Q_WIDTH // HEAD_DIM
    pen = moba_gate(proj, col, batch, seq)
    return flash_attention(proj, proj, col, col + KV_HEADS, _flash_bias_tiles(rel_table, 0), batch, seq,
                           pen=pen, pen_block=MOBA_BLOCK)


def kernel(x_prompt, x_sample, cache_nsa_cmp_kv, cache_nsa_sel_kv, cache_nsa_win_kv, cache_moba_kv,
           state_hgrn2, state_gdn_conv, state_gdn_ssm, page_table, rel_table, ln_mix, ln_ffn, ln_final,
           ffn_w_up, ffn_w_down, nsa_w_in, nsa_cmp_w1, nsa_cmp_w2, nsa_cmp_pe, nsa_w_out, moba_w_in, moba_w_out,
           hg_w_in, hg_lb_logits, hg_norm, hg_w_out, gdn_w_in, gdn_conv_w, gdn_a_log, gdn_dt_bias, gdn_norm,
           gdn_w_out):
    bf = jnp.bfloat16
    bp, tp = x_prompt.shape[:2]
    bs, ts = x_sample.shape[:2]
    assert tp % ATTN_TILE == 0 and WINDOW % ATTN_TILE == 0 and ATTN_TILE == MOBA_BLOCK and tp % SCAN_TILE == 0
    xp = x_prompt.reshape(bp * tp, D_MODEL)
    xs = x_sample.reshape(bs * ts, D_MODEL)
    w_up, w_down = ffn_w_up.astype(bf), ffn_w_down.astype(bf)

    for layer in range(DEPTH):
        kind = layer % N_MIXERS
        g_mix = ln_mix[layer]
        if kind == 0:
            w_main, w_tail = nsa_w_in.astype(bf), _pad_cols(nsa_w_in[:, NSA_MAIN:]).astype(bf)
            main_p, kv_p = norm_matmul(xp, g_mix, w_main, kv_from=Q_WIDTH // KV_WIDTH, n_cols=NSA_MAIN)
            tail_p = norm_matmul(xp, g_mix, w_tail)
            main_s, tail_s = norm_matmul(xs, g_mix, w_main, n_cols=NSA_MAIN), norm_matmul(xs, g_mix, w_tail)
            op, nsa_cmp_p, nsa_sel_p, nsa_win_p = _nsa_prompt(main_p, kv_p, tail_p, bp, tp, nsa_cmp_w1, nsa_cmp_w2,
                                                              nsa_cmp_pe, rel_table)
            os_, nsa_cmp_s, nsa_sel_s, nsa_win_s = _nsa_sample_pallas(main_s, tail_s, cache_nsa_cmp_kv,
                                                                     cache_nsa_sel_kv, cache_nsa_win_kv, page_table,
                                                                     nsa_cmp_w1, nsa_cmp_w2, nsa_cmp_pe, rel_table)
            w_out = nsa_w_out.astype(bf)
        elif kind == 1:
            w_in = moba_w_in.astype(bf)
            pp, kv_p = norm_matmul(xp, g_mix, w_in, kv_from=Q_WIDTH // KV_WIDTH)
            op, moba_p = _moba_prompt(pp, bp, tp, rel_table), kv_p.reshape(bp, tp, 2, KV_HEADS, HEAD_DIM)
            os_, moba_s = _moba_sample_pallas(norm_matmul(xs, g_mix, w_in), cache_moba_kv, page_table, rel_table)
            op, os_ = [op], [os_]
            w_out = moba_w_out.astype(bf)
        elif kind == 2:
            w_in = hg_w_in.astype(bf)
            pp = norm_matmul(xp, g_mix, w_in)
            ps = norm_matmul(xs, g_mix, w_in).reshape(bs, ts, -1)
            s0 = jnp.zeros((bp, HG_HEADS, HG_DK, HG_DV), jnp.float32)
            op, hg_p = hgrn2_scan(pp, hg_lb_logits, hg_norm, s0, layer, bp, tp)
            os_, hg_s = hgrn2_step(ps, hg_lb_logits, hg_norm, state_hgrn2, layer)
            op, os_ = [op], [os_.reshape(bs * ts, -1)]
            w_out = hg_w_out.astype(bf)
        else:
            w_main, w_tail = gdn_w_in.astype(bf), _pad_cols(gdn_w_in[:, GDN_MAIN:]).astype(bf)
            main_p, tail_p = norm_matmul(xp, g_mix, w_main, n_cols=GDN_MAIN), norm_matmul(xp, g_mix, w_tail)
            main_s = norm_matmul(xs, g_mix, w_main, n_cols=GDN_MAIN).reshape(bs, ts, -1)
            tail_s = norm_matmul(xs, g_mix, w_tail).reshape(bs, ts, -1)
            buf0 = jnp.zeros((bp, GDN_CONV - 1, GDN_CONV_DIM), jnp.float32)
            s0 = jnp.zeros((bp, GDN_V_HEADS, GDN_DK, GDN_DV), jnp.float32)
            op, ssm_p = gdn_scan(main_p, tail_p, buf0, gdn_conv_w, gdn_a_log, gdn_dt_bias, gdn_norm, s0, bp, tp)
            conv_p = main_p.reshape(bp, tp, -1)[:, tp - (GDN_CONV - 1):, :GDN_CONV_DIM]
            os_, ssm_s = gdn_step(main_s, tail_s, state_gdn_conv, gdn_conv_w, gdn_a_log, gdn_dt_bias, gdn_norm,
                                  state_gdn_ssm)
            conv_s = jnp.concatenate([state_gdn_conv, main_s[:, :, :GDN_CONV_DIM]], axis=1)[:, ts:]
            op, os_ = [op], [os_.reshape(bs * ts, -1)]
            w_out = gdn_w_out.astype(bf)
        xp = matmul_res(op, w_out, xp)
        xs = matmul_res(os_, w_out, xs)
        xp = ffn(xp, ln_ffn[layer], w_up, w_down, layer)
        xs = ffn(xs, ln_ffn[layer], w_up, w_down, layer)
    y_prompt = final_norm(xp, ln_final).reshape(bp, tp, D_MODEL)
    y_sample = final_norm(xs, ln_final).reshape(bs, ts, D_MODEL)
    return (y_prompt, y_sample, nsa_cmp_p, nsa_cmp_s, nsa_sel_p, nsa_sel_s, nsa_win_p, nsa_win_s,
            moba_p, moba_s, hg_p, hg_s, conv_p, conv_s, ssm_p, ssm_s)
```
